```python
import jax, jax.numpy as jnp
from jax import lax
import numpy as np

D_MODEL = 2048
BATCH = 8
SEQ = 8192
DEPTH = 1

GRID_W = 64
CTX_LEN = 256
D_FF = 5632
N_MOD = 9
RET_HEADS = 8
RET_DK = 64
RET_DV = 128
RET_CHUNK = 128
RET_ROPE_BASE = 10000.0
MLA_HEADS = 8
MLA_Q_RANK = 512
MLA_KV_RANK = 256
MLA_NOPE = 128
MLA_ROPE = 64
MLA_V = 128
AXIAL_BASE = 10000.0
Q_BLOCK = 128
RMS_EPS = 1e-6
GN_EPS = 1e-5
MIX_OUT = RET_HEADS * RET_DV + MLA_HEADS * MLA_V
SPLITS = (RET_HEADS * RET_DK, RET_HEADS * RET_DK, RET_HEADS * RET_DV, RET_HEADS * RET_DV,
          MLA_Q_RANK, MLA_KV_RANK, MLA_ROPE)
MIX_IN = sum(SPLITS)

kernel_name = "hymba_retention_mla_macaron_dit"


def rms_norm(x, g):
    xf = x.astype(jnp.float32)
    y = xf * lax.rsqrt(jnp.mean(xf * xf, axis=-1, keepdims=True) + RMS_EPS)
    return (y * g.astype(jnp.float32)).astype(x.dtype)


def modulate(h, shift, scale):
    return h * (1.0 + scale) + shift


def swiglu(h, w_in, w_out):
    g, u = jnp.split(h @ w_in, 2, axis=-1)
    return (jax.nn.silu(g) * u) @ w_out


def rope_tables(pos, dim, base):
    inv = base ** (-jnp.arange(0, dim, 2, dtype=jnp.float32) / dim)
    ang = pos.astype(jnp.float32)[:, None] * inv[None, :]
    return jnp.cos(ang)[:, None, :], jnp.sin(ang)[:, None, :]


def rotate(x, cos, sin):
    x1, x2 = jnp.split(x, 2, axis=-1)
    return jnp.concatenate([x1 * cos - x2 * sin, x2 * cos + x1 * sin], axis=-1).astype(x.dtype)


def axial_rope(x, row_tab, col_tab):
    xr, xc = jnp.split(x, 2, axis=-1)
    return jnp.concatenate([rotate(xr, *row_tab), rotate(xc, *col_tab)], axis=-1)


def project(h, w_in, q_norm_g, w_uq, kv_norm_g, w_ukv):
    b, n, _ = h.shape
    idx = [int(i) for i in np.cumsum(SPLITS)[:-1]]
    rq, rk, rv, rg, cq, ckv, kr = jnp.split(h @ w_in, idx, axis=-1)
    rq = rq.reshape(b, n, RET_HEADS, RET_DK)
    rk = rk.reshape(b, n, RET_HEADS, RET_DK) * (RET_DK ** -0.5)
    rv = rv.reshape(b, n, RET_HEADS, RET_DV)
    q = (rms_norm(cq, q_norm_g) @ w_uq).reshape(b, n, MLA_HEADS, MLA_NOPE + MLA_ROPE)
    kv = (rms_norm(ckv, kv_norm_g) @ w_ukv).reshape(b, n, MLA_HEADS, MLA_NOPE + MLA_V)
    q_nope, q_rope = q[..., :MLA_NOPE], q[..., MLA_NOPE:]
    k_nope, v = kv[..., :MLA_NOPE], kv[..., MLA_NOPE:]
    k_rope = kr[:, :, None, :]
    return rq, rk, rv, rg, q_nope, q_rope, k_nope, k_rope, v


def mla_keys(k_nope, k_rope):
    k_rope = jnp.broadcast_to(k_rope, k_nope.shape[:-1] + (MLA_ROPE,))
    return jnp.concatenate([k_nope, k_rope], axis=-1)


def block_attention(q, k, v):
    b, nq, h, dq = q.shape
    nb = nq // Q_BLOCK
    scale = dq ** -0.5
    qb = jnp.moveaxis(q.reshape(b, nb, Q_BLOCK, h, dq), 1, 0)

    def one(qi):
        s = jnp.einsum('bqhd,bkhd->bhqk', qi, k).astype(jnp.float32) * scale
        p = jax.nn.softmax(s, axis=-1)
        return jnp.einsum('bhqk,bkhe->bqhe', p.astype(v.dtype), v)

    out = lax.map(one, qb)
    return jnp.moveaxis(out, 0, 1).reshape(b, nq, h * v.shape[-1])


def retention_chunked(q, k, v, log_gamma, s0, strict):
    b, h, n, dk = q.shape
    dv = v.shape[-1]
    cs = RET_CHUNK
    nc = n // cs
    qc = q.reshape(b, h, nc, cs, dk)
    kc = k.reshape(b, h, nc, cs, dk)
    vc = v.reshape(b, h, nc, cs, dv)
    idx = jnp.arange(cs, dtype=jnp.float32)
    diff = idx[:, None] - idx[None, :]
    mask = diff > 0 if strict else diff >= 0
    lg = log_gamma[:, None, None]
    dmat = jnp.where(mask[None], jnp.exp(lg * jnp.where(mask, diff, 0.0)[None]), 0.0)
    scores = jnp.einsum('bhnid,bhnjd->bhnij', qc, kc) * dmat[None, :, None]
    y_inner = jnp.einsum('bhnij,bhnje->bhnie', scores, vc)
    zeta = jnp.exp(log_gamma[:, None] * (cs - 1 - idx)[None, :])
    xi = jnp.exp(log_gamma[:, None] * (idx + 1.0)[None, :])
    kv = jnp.einsum('bhnjd,bhnje->bhnde', kc * zeta[None, :, None, :, None], vc)
    decay_chunk = jnp.exp(log_gamma * cs)[None, :, None, None]

    def step(s, kv_n):
        return decay_chunk * s + kv_n, s

    s_final, s_prev = lax.scan(step, s0, jnp.moveaxis(kv, 2, 0))
    s_prev = jnp.moveaxis(s_prev, 0, 2)
    y_cross = jnp.einsum('bhnid,bhnde->bhnie', qc * xi[None, :, None, :, None], s_prev)
    return (y_inner + y_cross).reshape(b, h, n, dv), s_final


def bidir_retention(q, k, v, lg_f, lg_b, s0_f, s0_b):
    y_f, s_f = retention_chunked(q, k, v, lg_f, s0_f, False)
    y_b, s_b = retention_chunked(jnp.flip(q, 2), jnp.flip(k, 2), jnp.flip(v, 2), lg_b, s0_b, True)
    return y_f + jnp.flip(y_b, 2), s_f, s_b


def retention_output(y, gate):
    mu = jnp.mean(y, axis=-1, keepdims=True)
    var = jnp.mean(jnp.square(y - mu), axis=-1, keepdims=True)
    y = (y - mu) * lax.rsqrt(var + GN_EPS)
    b, h, n, dv = y.shape
    y = jnp.swapaxes(y, 1, 2).reshape(b, n, h * dv)
    return (jax.nn.silu(gate.astype(jnp.float32)) * y).astype(gate.dtype)


def _fwd_setup_inputs(seed: int = 0) -> dict:
    key = jax.random.key(seed)
    ks = jax.random.split(key, 24)
    f32 = jnp.float32

    def nrm(k, shape, scale):
        return jax.random.normal(k, shape, f32) * scale

    gam = 1.0 - 2.0 ** (-5.0 - jnp.arange(RET_HEADS, dtype=f32))
    decay_logit = jnp.log(gam) - jnp.log1p(-gam)
    return {
        "x": nrm(ks[0], (BATCH, SEQ, D_MODEL), 1.0),
        "c": nrm(ks[1], (BATCH, D_MODEL), 1.0),
        "ctx": nrm(ks[2], (BATCH, CTX_LEN, D_MODEL), 1.0),
        "c_ctx": nrm(ks[3], (D_MODEL,), 1.0),
        "ada_w": nrm(ks[4], (DEPTH, D_MODEL, N_MOD * D_MODEL), 0.5 * D_MODEL ** -0.5),
        "ada_b": nrm(ks[5], (DEPTH, N_MOD * D_MODEL), 0.02),
        "norm1_g": 1.0 + nrm(ks[6], (DEPTH, D_MODEL), 0.02),
        "ffn1_w_in": nrm(ks[7], (DEPTH, D_MODEL, 2 * D_FF), D_MODEL ** -0.5),
        "ffn1_w_out": nrm(ks[8], (DEPTH, D_FF, D_MODEL), D_FF ** -0.5),
        "norm2_g": 1.0 + nrm(ks[9], (DEPTH, D_MODEL), 0.02),
        "mix_w_in": nrm(ks[10], (DEPTH, D_MODEL, MIX_IN), D_MODEL ** -0.5),
        "ret_decay_fwd": decay_logit[None, :] + nrm(ks[11], (DEPTH, RET_HEADS), 0.05),
        "ret_decay_bwd": decay_logit[None, :] + nrm(ks[12], (DEPTH, RET_HEADS), 0.05),
        "mla_q_norm_g": 1.0 + nrm(ks[13], (DEPTH, MLA_Q_RANK), 0.02),
        "mla_w_uq": nrm(ks[14], (DEPTH, MLA_Q_RANK, MLA_HEADS * (MLA_NOPE + MLA_ROPE)), MLA_Q_RANK ** -0.5),
        "mla_kv_norm_g": 1.0 + nrm(ks[15], (DEPTH, MLA_KV_RANK), 0.02),
        "mla_w_ukv": nrm(ks[16], (DEPTH, MLA_KV_RANK, MLA_HEADS * (MLA_NOPE + MLA_V)), MLA_KV_RANK ** -0.5),
        "mix_w_out": nrm(ks[17], (DEPTH, MIX_OUT, D_MODEL), MIX_OUT ** -0.5),
        "norm3_g": 1.0 + nrm(ks[18], (DEPTH, D_MODEL), 0.02),
        "ffn2_w_in": nrm(ks[19], (DEPTH, D_MODEL, 2 * D_FF), D_MODEL ** -0.5),
        "ffn2_w_out": nrm(ks[20], (DEPTH, D_FF, D_MODEL), D_FF ** -0.5),
        "final_norm_g": 1.0 + nrm(ks[21], (D_MODEL,), 0.02),
    }


def _fwd_reference(x, c, ctx, c_ctx, ada_w, ada_b, norm1_g, ffn1_w_in, ffn1_w_out, norm2_g,
              mix_w_in, ret_decay_fwd, ret_decay_bwd, mla_q_norm_g, mla_w_uq, mla_kv_norm_g,
              mla_w_ukv, mix_w_out, norm3_g, ffn2_w_in, ffn2_w_out, final_norm_g):
    b, n_lat, d = x.shape
    n_rows = n_lat // GRID_W
    pos_row = jnp.repeat(jnp.arange(n_rows), GRID_W)
    pos_col = jnp.tile(jnp.arange(GRID_W), n_rows)
    row_tab = rope_tables(pos_row, MLA_ROPE // 2, AXIAL_BASE)
    col_tab = rope_tables(pos_col, MLA_ROPE // 2, AXIAL_BASE)
    ret_tab = rope_tables(jnp.arange(n_lat), RET_DK, RET_ROPE_BASE)
    s_zero = jnp.zeros((b, RET_HEADS, RET_DK, RET_DV), jnp.float32)

    def heads(t):
        return jnp.swapaxes(t, 1, 2).astype(jnp.float32)

    for l in range(DEPTH):
        is_last = l == DEPTH - 1
        m_lat = (jax.nn.silu(c) @ ada_w[l] + ada_b[l]).reshape(b, N_MOD, 1, d)
        m_ctx = (jax.nn.silu(c_ctx)[None, :] @ ada_w[l] + ada_b[l]).reshape(1, N_MOD, 1, d)

        x = x + 0.5 * m_lat[:, 2] * swiglu(modulate(rms_norm(x, norm1_g[l]), m_lat[:, 0], m_lat[:, 1]),
                                           ffn1_w_in[l], ffn1_w_out[l])
        ctx = ctx + 0.5 * m_ctx[:, 2] * swiglu(modulate(rms_norm(ctx, norm1_g[l]), m_ctx[:, 0], m_ctx[:, 1]),
                                               ffn1_w_in[l], ffn1_w_out[l])

        hx = modulate(rms_norm(x, norm2_g[l]), m_lat[:, 3], m_lat[:, 4])
        hc = modulate(rms_norm(ctx, norm2_g[l]), m_ctx[:, 3], m_ctx[:, 4])
        rq, rk, rv, rg, qn, qr, kn, kr, v = project(
            hx, mix_w_in[l], mla_q_norm_g[l], mla_w_uq[l], mla_kv_norm_g[l], mla_w_ukv[l])
        crq, crk, crv, crg, cqn, cqr, ckn, ckr, cv = project(
            hc, mix_w_in[l], mla_q_norm_g[l], mla_w_uq[l], mla_kv_norm_g[l], mla_w_ukv[l])

        lg_f = jax.nn.log_sigmoid(ret_decay_fwd[l].astype(jnp.float32))
        lg_b = jax.nn.log_sigmoid(ret_decay_bwd[l].astype(jnp.float32))
        y_ctx, s_f, s_b = bidir_retention(heads(crq), heads(crk), heads(crv), lg_f, lg_b, s_zero, s_zero)
        rq = rotate(rq, *ret_tab)
        rk = rotate(rk, *ret_tab)
        y_lat, _, _ = bidir_retention(heads(rq), heads(rk), heads(rv), lg_f, lg_b, s_f, s_b)
        ret_out = retention_output(y_lat, rg)

        q_lat = jnp.concatenate([qn, axial_rope(qr, row_tab, col_tab)], axis=-1)
        k_lat = mla_keys(kn, axial_rope(kr, row_tab, col_tab))
        k_ctx = mla_keys(ckn, ckr)
        k_all = jnp.concatenate([k_lat, k_ctx], axis=1)
        v_all = jnp.concatenate([v, cv], axis=1)
        mla_out = block_attention(q_lat, k_all, v_all)

        mix = jnp.concatenate([ret_out.astype(x.dtype), mla_out.astype(x.dtype)], axis=-1) @ mix_w_out[l]
        x = x + m_lat[:, 5] * mix
        if not is_last:
            ctx_ret = retention_output(y_ctx, crg)
            ctx_mla = block_attention(jnp.concatenate([cqn, cqr], axis=-1), k_ctx, cv)
            ctx_mix = jnp.concatenate([ctx_ret.astype(ctx.dtype), ctx_mla.astype(ctx.dtype)], axis=-1) @ mix_w_out[l]
            ctx = ctx + m_ctx[:, 5] * ctx_mix

        x = x + 0.5 * m_lat[:, 8] * swiglu(modulate(rms_norm(x, norm3_g[l]), m_lat[:, 6], m_lat[:, 7]),
                                           ffn2_w_in[l], ffn2_w_out[l])
        if not is_last:
            ctx = ctx + 0.5 * m_ctx[:, 8] * swiglu(modulate(rms_norm(ctx, norm3_g[l]), m_ctx[:, 6], m_ctx[:, 7]),
                                                   ffn2_w_in[l], ffn2_w_out[l])

    return rms_norm(x, final_norm_g)


import jax as _jax
import jax.numpy as _jnp

TWIN_FORMAT = 'train_step'
FWD_PARAMS = ['x', 'c', 'ctx', 'c_ctx', 'ada_w', 'ada_b', 'norm1_g', 'ffn1_w_in', 'ffn1_w_out', 'norm2_g', 'mix_w_in', 'ret_decay_fwd', 'ret_decay_bwd', 'mla_q_norm_g', 'mla_w_uq', 'mla_kv_norm_g', 'mla_w_ukv', 'mix_w_out', 'norm3_g', 'ffn2_w_in', 'ffn2_w_out', 'final_norm_g']
TWIN_WEIGHTS = ['c_ctx', 'ada_w', 'ada_b', 'norm1_g', 'ffn1_w_in', 'ffn1_w_out', 'norm2_g', 'mix_w_in', 'ret_decay_fwd', 'ret_decay_bwd', 'mla_q_norm_g', 'mla_w_uq', 'mla_kv_norm_g', 'mla_w_ukv', 'mix_w_out', 'norm3_g', 'ffn2_w_in', 'ffn2_w_out', 'final_norm_g']
TWIN_DIFF_INPUT = 'x'
TWIN_INPUTS = ['x', 'c', 'ctx', 'c_ctx', 'ada_w', 'ada_b', 'norm1_g', 'ffn1_w_in', 'ffn1_w_out', 'norm2_g', 'mix_w_in', 'ret_decay_fwd', 'ret_decay_bwd', 'mla_q_norm_g', 'mla_w_uq', 'mla_kv_norm_g', 'mla_w_ukv', 'mix_w_out', 'norm3_g', 'ffn2_w_in', 'ffn2_w_out', 'final_norm_g', 'loss_target', 'm_c_ctx', 'm_ada_w', 'm_ada_b', 'm_norm1_g', 'm_ffn1_w_in', 'm_ffn1_w_out', 'm_norm2_g', 'm_mix_w_in', 'm_ret_decay_fwd', 'm_ret_decay_bwd', 'm_mla_q_norm_g', 'm_mla_w_uq', 'm_mla_kv_norm_g', 'm_mla_w_ukv', 'm_mix_w_out', 'm_norm3_g', 'm_ffn2_w_in', 'm_ffn2_w_out', 'm_final_norm_g', 'v_c_ctx', 'v_ada_w', 'v_ada_b', 'v_norm1_g', 'v_ffn1_w_in', 'v_ffn1_w_out', 'v_norm2_g', 'v_mix_w_in', 'v_ret_decay_fwd', 'v_ret_decay_bwd', 'v_mla_q_norm_g', 'v_mla_w_uq', 'v_mla_kv_norm_g', 'v_mla_w_ukv', 'v_mix_w_out', 'v_norm3_g', 'v_ffn2_w_in', 'v_ffn2_w_out', 'v_final_norm_g']
TWIN_OUTPUTS = ['loss', 'grad_x', 'grad_c_ctx', 'grad_ada_w', 'grad_ada_b', 'grad_norm1_g', 'grad_ffn1_w_in', 'grad_ffn1_w_out', 'grad_norm2_g', 'grad_mix_w_in', 'grad_ret_decay_fwd', 'grad_ret_decay_bwd', 'grad_mla_q_norm_g', 'grad_mla_w_uq', 'grad_mla_kv_norm_g', 'grad_mla_w_ukv', 'grad_mix_w_out', 'grad_norm3_g', 'grad_ffn2_w_in', 'grad_ffn2_w_out', 'grad_final_norm_g', 'delta_c_ctx', 'delta_ada_w', 'delta_ada_b', 'delta_norm1_g', 'delta_ffn1_w_in', 'delta_ffn1_w_out', 'delta_norm2_g', 'delta_mix_w_in', 'delta_ret_decay_fwd', 'delta_ret_decay_bwd', 'delta_mla_q_norm_g', 'delta_mla_w_uq', 'delta_mla_kv_norm_g', 'delta_mla_w_ukv', 'delta_mix_w_out', 'delta_norm3_g', 'delta_ffn2_w_in', 'delta_ffn2_w_out', 'delta_final_norm_g', 'new_m_c_ctx', 'new_m_ada_w', 'new_m_ada_b', 'new_m_norm1_g', 'new_m_ffn1_w_in', 'new_m_ffn1_w_out', 'new_m_norm2_g', 'new_m_mix_w_in', 'new_m_ret_decay_fwd', 'new_m_ret_decay_bwd', 'new_m_mla_q_norm_g', 'new_m_mla_w_uq', 'new_m_mla_kv_norm_g', 'new_m_mla_w_ukv', 'new_m_mix_w_out', 'new_m_norm3_g', 'new_m_ffn2_w_in', 'new_m_ffn2_w_out', 'new_m_final_norm_g', 'new_v_c_ctx', 'new_v_ada_w', 'new_v_ada_b', 'new_v_norm1_g', 'new_v_ffn1_w_in', 'new_v_ffn1_w_out', 'new_v_norm2_g', 'new_v_mix_w_in', 'new_v_ret_decay_fwd', 'new_v_ret_decay_bwd', 'new_v_mla_q_norm_g', 'new_v_mla_w_uq', 'new_v_mla_kv_norm_g', 'new_v_mla_w_ukv', 'new_v_mix_w_out', 'new_v_norm3_g', 'new_v_ffn2_w_in', 'new_v_ffn2_w_out', 'new_v_final_norm_g']
TWIN_LEAF_KINDS = {'loss': 'loss', 'grad_x': 'grad_x', 'grad_c_ctx': 'grad_w', 'grad_ada_w': 'grad_w', 'grad_ada_b': 'grad_w', 'grad_norm1_g': 'grad_w', 'grad_ffn1_w_in': 'grad_w', 'grad_ffn1_w_out': 'grad_w', 'grad_norm2_g': 'grad_w', 'grad_mix_w_in': 'grad_w', 'grad_ret_decay_fwd': 'grad_w', 'grad_ret_decay_bwd': 'grad_w', 'grad_mla_q_norm_g': 'grad_w', 'grad_mla_w_uq': 'grad_w', 'grad_mla_kv_norm_g': 'grad_w', 'grad_mla_w_ukv': 'grad_w', 'grad_mix_w_out': 'grad_w', 'grad_norm3_g': 'grad_w', 'grad_ffn2_w_in': 'grad_w', 'grad_ffn2_w_out': 'grad_w', 'grad_final_norm_g': 'grad_w', 'delta_c_ctx': 'delta_w', 'delta_ada_w': 'delta_w', 'delta_ada_b': 'delta_w', 'delta_norm1_g': 'delta_w', 'delta_ffn1_w_in': 'delta_w', 'delta_ffn1_w_out': 'delta_w', 'delta_norm2_g': 'delta_w', 'delta_mix_w_in': 'delta_w', 'delta_ret_decay_fwd': 'delta_w', 'delta_ret_decay_bwd': 'delta_w', 'delta_mla_q_norm_g': 'delta_w', 'delta_mla_w_uq': 'delta_w', 'delta_mla_kv_norm_g': 'delta_w', 'delta_mla_w_ukv': 'delta_w', 'delta_mix_w_out': 'delta_w', 'delta_norm3_g': 'delta_w', 'delta_ffn2_w_in': 'delta_w', 'delta_ffn2_w_out': 'delta_w', 'delta_final_norm_g': 'delta_w', 'new_m_c_ctx': 'new_m', 'new_m_ada_w': 'new_m', 'new_m_ada_b': 'new_m', 'new_m_norm1_g': 'new_m', 'new_m_ffn1_w_in': 'new_m', 'new_m_ffn1_w_out': 'new_m', 'new_m_norm2_g': 'new_m', 'new_m_mix_w_in': 'new_m', 'new_m_ret_decay_fwd': 'new_m', 'new_m_ret_decay_bwd': 'new_m', 'new_m_mla_q_norm_g': 'new_m', 'new_m_mla_w_uq': 'new_m', 'new_m_mla_kv_norm_g': 'new_m', 'new_m_mla_w_ukv': 'new_m', 'new_m_mix_w_out': 'new_m', 'new_m_norm3_g': 'new_m', 'new_m_ffn2_w_in': 'new_m', 'new_m_ffn2_w_out': 'new_m', 'new_m_final_norm_g': 'new_m', 'new_v_c_ctx': 'new_v', 'new_v_ada_w': 'new_v', 'new_v_ada_b': 'new_v', 'new_v_norm1_g': 'new_v', 'new_v_ffn1_w_in': 'new_v', 'new_v_ffn1_w_out': 'new_v', 'new_v_norm2_g': 'new_v', 'new_v_mix_w_in': 'new_v', 'new_v_ret_decay_fwd': 'new_v', 'new_v_ret_decay_bwd': 'new_v', 'new_v_mla_q_norm_g': 'new_v', 'new_v_mla_w_uq': 'new_v', 'new_v_mla_kv_norm_g': 'new_v', 'new_v_mla_w_ukv': 'new_v', 'new_v_mix_w_out': 'new_v', 'new_v_norm3_g': 'new_v', 'new_v_ffn2_w_in': 'new_v', 'new_v_ffn2_w_out': 'new_v', 'new_v_final_norm_g': 'new_v'}


def _forward(args):
    return _fwd_reference(*[args[k] for k in FWD_PARAMS])


def _output_shape():
    def fwd():
        inp = _fwd_setup_inputs(0)
        return _fwd_reference(*[inp[k] for k in FWD_PARAMS])
    out = _jax.eval_shape(fwd)
    return out.shape, out.dtype

N_MICROBATCH = 1
ADAM_LR = 0.001
ADAM_B1 = 0.9
ADAM_B2 = 0.999
ADAM_EPS = 1e-08
ADAM_WD = 0.01
ADAM_STEP = 10
PER_EXAMPLE_BATCH_AXIS = {'x': 0, 'c': 0, 'ctx': 0, 'loss_target': 0}
SHARED_INPUTS = []
_WEIGHT_DTYPES = {'c_ctx': _jnp.float32, 'ada_w': _jnp.float32, 'ada_b': _jnp.float32, 'norm1_g': _jnp.float32, 'ffn1_w_in': _jnp.float32, 'ffn1_w_out': _jnp.float32, 'norm2_g': _jnp.float32, 'mix_w_in': _jnp.float32, 'ret_decay_fwd': _jnp.float32, 'ret_decay_bwd': _jnp.float32, 'mla_q_norm_g': _jnp.float32, 'mla_w_uq': _jnp.float32, 'mla_kv_norm_g': _jnp.float32, 'mla_w_ukv': _jnp.float32, 'mix_w_out': _jnp.float32, 'norm3_g': _jnp.float32, 'ffn2_w_in': _jnp.float32, 'ffn2_w_out': _jnp.float32, 'final_norm_g': _jnp.float32}
MOMENT_SCALE = {'c_ctx': 1.509250e-02, 'ada_w': 2.472813e-02, 'ada_b': 4.003484e-02, 'norm1_g': 1.896246e-02, 'ffn1_w_in': 8.347040e-03, 'ffn1_w_out': 1.363369e-02, 'norm2_g': 3.180723e-02, 'mix_w_in': 2.687333e-02, 'ret_decay_fwd': 1.991760e-01, 'ret_decay_bwd': 1.912456e-01, 'mla_q_norm_g': 4.738684e-03, 'mla_w_uq': 2.864297e-03, 'mla_kv_norm_g': 2.081251e-02, 'mla_w_ukv': 7.802750e-03, 'mix_w_out': 1.857829e-02, 'norm3_g': 1.877023e-02, 'ffn2_w_in': 8.096859e-03, 'ffn2_w_out': 1.322024e-02, 'final_norm_g': 3.198041e+01}


def _to_microbatches(a, axis):
    t = _jnp.moveaxis(a, axis, 0)
    t = t.reshape((N_MICROBATCH, t.shape[0] // N_MICROBATCH) + t.shape[1:])
    return _jnp.moveaxis(t, 1, axis + 1)


def setup_inputs(seed: int = 0) -> dict:
    inp = _fwd_setup_inputs(seed)
    key = _jax.random.fold_in(_jax.random.key(seed), 7919)
    shape, _ = _output_shape()
    out = dict(inp)
    out["loss_target"] = _jax.random.normal(_jax.random.fold_in(key, 0), shape, _jnp.float32)
    for i, name in enumerate(TWIN_WEIGHTS):
        w = inp[name].astype(_jnp.float32)
        if MOMENT_SCALE is None:
            s = _jnp.sqrt(_jnp.mean(_jnp.square(w)) + 1e-30)
        else:
            s = MOMENT_SCALE[name]
        km, kv = _jax.random.split(_jax.random.fold_in(key, i + 1))
        out[name] = w
        out["m_" + name] = s * _jax.random.normal(km, w.shape, _jnp.float32)
        out["v_" + name] = (s * s) * _jax.random.uniform(kv, w.shape, _jnp.float32, 0.5, 1.5)
    if N_MICROBATCH > 1:
        for name, axis in PER_EXAMPLE_BATCH_AXIS.items():
            out[name] = _to_microbatches(out[name], axis)
    return {'x': out['x'], 'c': out['c'], 'ctx': out['ctx'], 'c_ctx': out['c_ctx'], 'ada_w': out['ada_w'], 'ada_b': out['ada_b'], 'norm1_g': out['norm1_g'], 'ffn1_w_in': out['ffn1_w_in'], 'ffn1_w_out': out['ffn1_w_out'], 'norm2_g': out['norm2_g'], 'mix_w_in': out['mix_w_in'], 'ret_decay_fwd': out['ret_decay_fwd'], 'ret_decay_bwd': out['ret_decay_bwd'], 'mla_q_norm_g': out['mla_q_norm_g'], 'mla_w_uq': out['mla_w_uq'], 'mla_kv_norm_g': out['mla_kv_norm_g'], 'mla_w_ukv': out['mla_w_ukv'], 'mix_w_out': out['mix_w_out'], 'norm3_g': out['norm3_g'], 'ffn2_w_in': out['ffn2_w_in'], 'ffn2_w_out': out['ffn2_w_out'], 'final_norm_g': out['final_norm_g'], 'loss_target': out['loss_target'], 'm_c_ctx': out['m_c_ctx'], 'm_ada_w': out['m_ada_w'], 'm_ada_b': out['m_ada_b'], 'm_norm1_g': out['m_norm1_g'], 'm_ffn1_w_in': out['m_ffn1_w_in'], 'm_ffn1_w_out': out['m_ffn1_w_out'], 'm_norm2_g': out['m_norm2_g'], 'm_mix_w_in': out['m_mix_w_in'], 'm_ret_decay_fwd': out['m_ret_decay_fwd'], 'm_ret_decay_bwd': out['m_ret_decay_bwd'], 'm_mla_q_norm_g': out['m_mla_q_norm_g'], 'm_mla_w_uq': out['m_mla_w_uq'], 'm_mla_kv_norm_g': out['m_mla_kv_norm_g'], 'm_mla_w_ukv': out['m_mla_w_ukv'], 'm_mix_w_out': out['m_mix_w_out'], 'm_norm3_g': out['m_norm3_g'], 'm_ffn2_w_in': out['m_ffn2_w_in'], 'm_ffn2_w_out': out['m_ffn2_w_out'], 'm_final_norm_g': out['m_final_norm_g'], 'v_c_ctx': out['v_c_ctx'], 'v_ada_w': out['v_ada_w'], 'v_ada_b': out['v_ada_b'], 'v_norm1_g': out['v_norm1_g'], 'v_ffn1_w_in': out['v_ffn1_w_in'], 'v_ffn1_w_out': out['v_ffn1_w_out'], 'v_norm2_g': out['v_norm2_g'], 'v_mix_w_in': out['v_mix_w_in'], 'v_ret_decay_fwd': out['v_ret_decay_fwd'], 'v_ret_decay_bwd': out['v_ret_decay_bwd'], 'v_mla_q_norm_g': out['v_mla_q_norm_g'], 'v_mla_w_uq': out['v_mla_w_uq'], 'v_mla_kv_norm_g': out['v_mla_kv_norm_g'], 'v_mla_w_ukv': out['v_mla_w_ukv'], 'v_mix_w_out': out['v_mix_w_out'], 'v_norm3_g': out['v_norm3_g'], 'v_ffn2_w_in': out['v_ffn2_w_in'], 'v_ffn2_w_out': out['v_ffn2_w_out'], 'v_final_norm_g': out['v_final_norm_g']}


def _loss(weights, diff, rest, loss_target):
    with _jax.named_scope("forward"):
        args = {**rest, TWIN_DIFF_INPUT: diff, **{k: w.astype(_WEIGHT_DTYPES[k]) for k, w in weights.items()}}
        y = _forward(args)
    with _jax.named_scope("loss_head"):
        err = _jnp.square(y.astype(_jnp.float32) - loss_target)
        return 0.5 * _jnp.sum(_jnp.mean(err, axis=-1)) if err.ndim else 0.5 * err


def _adamw(w, g, m, v):
    m = ADAM_B1 * m + (1.0 - ADAM_B1) * g
    v = ADAM_B2 * v + (1.0 - ADAM_B2) * _jnp.square(g)
    m_hat = m / (1.0 - ADAM_B1 ** ADAM_STEP)
    v_hat = v / (1.0 - ADAM_B2 ** ADAM_STEP)
    delta = -ADAM_LR * (m_hat / (_jnp.sqrt(v_hat) + ADAM_EPS) + ADAM_WD * w)
    return delta, m, v


def reference(x, c, ctx, c_ctx, ada_w, ada_b, norm1_g, ffn1_w_in, ffn1_w_out, norm2_g, mix_w_in, ret_decay_fwd, ret_decay_bwd, mla_q_norm_g, mla_w_uq, mla_kv_norm_g, mla_w_ukv, mix_w_out, norm3_g, ffn2_w_in, ffn2_w_out, final_norm_g, loss_target, m_c_ctx, m_ada_w, m_ada_b, m_norm1_g, m_ffn1_w_in, m_ffn1_w_out, m_norm2_g, m_mix_w_in, m_ret_decay_fwd, m_ret_decay_bwd, m_mla_q_norm_g, m_mla_w_uq, m_mla_kv_norm_g, m_mla_w_ukv, m_mix_w_out, m_norm3_g, m_ffn2_w_in, m_ffn2_w_out, m_final_norm_g, v_c_ctx, v_ada_w, v_ada_b, v_norm1_g, v_ffn1_w_in, v_ffn1_w_out, v_norm2_g, v_mix_w_in, v_ret_decay_fwd, v_ret_decay_bwd, v_mla_q_norm_g, v_mla_w_uq, v_mla_kv_norm_g, v_mla_w_ukv, v_mix_w_out, v_norm3_g, v_ffn2_w_in, v_ffn2_w_out, v_final_norm_g):
    given = dict(x=x, c=c, ctx=ctx, c_ctx=c_ctx, ada_w=ada_w, ada_b=ada_b, norm1_g=norm1_g, ffn1_w_in=ffn1_w_in, ffn1_w_out=ffn1_w_out, norm2_g=norm2_g, mix_w_in=mix_w_in, ret_decay_fwd=ret_decay_fwd, ret_decay_bwd=ret_decay_bwd, mla_q_norm_g=mla_q_norm_g, mla_w_uq=mla_w_uq, mla_kv_norm_g=mla_kv_norm_g, mla_w_ukv=mla_w_ukv, mix_w_out=mix_w_out, norm3_g=norm3_g, ffn2_w_in=ffn2_w_in, ffn2_w_out=ffn2_w_out, final_norm_g=final_norm_g, loss_target=loss_target, m_c_ctx=m_c_ctx, m_ada_w=m_ada_w, m_ada_b=m_ada_b, m_norm1_g=m_norm1_g, m_ffn1_w_in=m_ffn1_w_in, m_ffn1_w_out=m_ffn1_w_out, m_norm2_g=m_norm2_g, m_mix_w_in=m_mix_w_in, m_ret_decay_fwd=m_ret_decay_fwd, m_ret_decay_bwd=m_ret_decay_bwd, m_mla_q_norm_g=m_mla_q_norm_g, m_mla_w_uq=m_mla_w_uq, m_mla_kv_norm_g=m_mla_kv_norm_g, m_mla_w_ukv=m_mla_w_ukv, m_mix_w_out=m_mix_w_out, m_norm3_g=m_norm3_g, m_ffn2_w_in=m_ffn2_w_in, m_ffn2_w_out=m_ffn2_w_out, m_final_norm_g=m_final_norm_g, v_c_ctx=v_c_ctx, v_ada_w=v_ada_w, v_ada_b=v_ada_b, v_norm1_g=v_norm1_g, v_ffn1_w_in=v_ffn1_w_in, v_ffn1_w_out=v_ffn1_w_out, v_norm2_g=v_norm2_g, v_mix_w_in=v_mix_w_in, v_ret_decay_fwd=v_ret_decay_fwd, v_ret_decay_bwd=v_ret_decay_bwd, v_mla_q_norm_g=v_mla_q_norm_g, v_mla_w_uq=v_mla_w_uq, v_mla_kv_norm_g=v_mla_kv_norm_g, v_mla_w_ukv=v_mla_w_ukv, v_mix_w_out=v_mix_w_out, v_norm3_g=v_norm3_g, v_ffn2_w_in=v_ffn2_w_in, v_ffn2_w_out=v_ffn2_w_out, v_final_norm_g=v_final_norm_g)
    weights = {n: given[n] for n in TWIN_WEIGHTS}
    shared = {n: given[n] for n in SHARED_INPUTS}
    per_example = {n: given[n] for n in ['x', 'c', 'ctx']}
    grad_fn = _jax.value_and_grad(_loss, argnums=(0, 1))

    def one_microbatch(ex, loss_target):
        ex = dict(ex)
        diff = ex.pop(TWIN_DIFF_INPUT)
        return grad_fn(weights, diff, {**shared, **ex}, loss_target)

    if N_MICROBATCH == 1:
        loss, (grad_w, grad_x) = one_microbatch(per_example, given["loss_target"])
    else:
        def body(carry, xs):
            loss_sum, grad_sum = carry
            l_k, (gw_k, gx_k) = one_microbatch(xs[0], xs[1])
            with _jax.named_scope("update"):
                return (loss_sum + l_k, _jax.tree.map(_jnp.add, grad_sum, gw_k)), gx_k

        init = (_jnp.zeros((), _jnp.float32), _jax.tree.map(_jnp.zeros_like, weights))
        (loss, grad_w), grad_x = _jax.lax.scan(body, init, (per_example, given["loss_target"]))
    with _jax.named_scope("update"):
        delta_w, new_m, new_v = {}, {}, {}
        for n in TWIN_WEIGHTS:
            delta_w[n], new_m[n], new_v[n] = _adamw(weights[n], grad_w[n], given["m_" + n], given["v_" + n])
    return (loss, grad_x, *[grad_w[n] for n in TWIN_WEIGHTS], *[delta_w[n] for n in TWIN_WEIGHTS],
            *[new_m[n] for n in TWIN_WEIGHTS], *[new_v[n] for n in TWIN_WEIGHTS])
```

```python
import functools

import jax
import jax.numpy as jnp
from jax import lax
from jax.experimental import pallas as pl
from jax.experimental.pallas import tpu as pltpu

F32 = jnp.float32
BF16 = jnp.bfloat16

N_DEV = 8
MESH_AXES = ("x", "y", "c")

GRID_W = 64
N_MOD = 9
RET_HEADS = 8
RET_DK = 64
RET_DV = 128
RET_CHUNK = 256
RET_ROPE_BASE = 10000.0
MLA_HEADS = 8
MLA_Q_RANK = 512
MLA_KV_RANK = 256
MLA_NOPE = 128
MLA_ROPE = 64
MLA_V = 128
AXIAL_BASE = 10000.0
RMS_EPS = 1e-6
GN_EPS = 1e-5
SPLITS = (RET_HEADS * RET_DK, RET_HEADS * RET_DK, RET_HEADS * RET_DV, RET_HEADS * RET_DV,
          MLA_Q_RANK, MLA_KV_RANK, MLA_ROPE)
MIX_IN = sum(SPLITS)
MIX_IN_PAD = 4096

ADAM_LR = 0.001
ADAM_B1 = 0.9
ADAM_B2 = 0.999
ADAM_EPS = 1e-08
ADAM_WD = 0.01
ADAM_STEP = 10

LANE = 128
VMEM_LIMIT_BYTES = 56 * 1024 * 1024

NN = ((1,), (0,))
NT = ((1,), (1,))
TN = ((0,), (0,))


def _pick(dim, target, align=LANE):
    t = min(dim, target)
    t -= t % align
    while t >= align:
        if dim % t == 0:
            return t
        t -= align
    return dim


def _params():
    return pltpu.CompilerParams(vmem_limit_bytes=VMEM_LIMIT_BYTES)


def _dot(a, b, dims):
    return lax.dot_general(a.astype(BF16), b.astype(BF16), (dims, ((), ())), preferred_element_type=F32)


def _mm_call(name, grid, ins, pairs, outs, acc_shapes, epilogue):
    n_in, n_out = len(ins), len(outs)
    k_axis = len(grid) - 1
    k_steps = grid[k_axis]

    def body(*refs):
        in_refs = refs[:n_in]
        out_refs = refs[n_in:n_in + n_out]
        accs = refs[n_in + n_out:]
        k = pl.program_id(k_axis)

        @pl.when(k == 0)
        def _():
            for acc in accs:
                acc[...] = jnp.zeros_like(acc)

        for ai, bi, dims, ci in pairs:
            accs[ci][...] += _dot(in_refs[ai][...], in_refs[bi][...], dims)

        @pl.when(k == k_steps - 1)
        def _():
            epilogue([acc[...] for acc in accs], in_refs, out_refs)

    res = pl.pallas_call(
        body, name=name, grid=grid,
        in_specs=[s for _, s in ins], out_specs=[s for _, s in outs],
        out_shape=[s for s, _ in outs],
        scratch_shapes=[pltpu.VMEM(s, F32) for s in acc_shapes],
        compiler_params=_params(),
    )(*[a for a, _ in ins])
    return res


def _matmul(a, b, mode, out_dtype, name, tm=1024, tn=1024, tk=512):
    if mode == "nn":
        (m, kd), n = a.shape, b.shape[1]
    elif mode == "nt":
        (m, kd), n = a.shape, b.shape[0]
    else:
        (kd, m), n = a.shape, b.shape[1]
    tm, tn = _pick(m, tm, 16), _pick(n, tn)
    tk = _pick(kd, tk) if mode != "tn" else _pick(kd, tk, 16)
    if mode == "nn":
        a_spec = pl.BlockSpec((tm, tk), lambda i, j, k: (i, k))
        b_spec = pl.BlockSpec((tk, tn), lambda i, j, k: (k, j))
        dims = NN
    elif mode == "nt":
        a_spec = pl.BlockSpec((tm, tk), lambda i, j, k: (i, k))
        b_spec = pl.BlockSpec((tn, tk), lambda i, j, k: (j, k))
        dims = NT
    else:
        a_spec = pl.BlockSpec((tk, tm), lambda i, j, k: (k, i))
        b_spec = pl.BlockSpec((tk, tn), lambda i, j, k: (k, j))
        dims = TN

    def epilogue(accs, in_refs, out_refs):
        out_refs[0][...] = accs[0].astype(out_dtype)

    return _mm_call(
        name, (m // tm, n // tn, kd // tk), [(a, a_spec), (b, b_spec)], [(0, 1, dims, 0)],
        [(jax.ShapeDtypeStruct((m, n), out_dtype), pl.BlockSpec((tm, tn), lambda i, j, k: (i, j)))],
        [(tm, tn)], epilogue)[0]


def _norm_mod_tile(x, ng, sc, sh):
    r = lax.rsqrt(jnp.mean(x * x, axis=-1, keepdims=True) + RMS_EPS)
    return (x * r * ng) * (1.0 + sc) + sh


def _row_spec(tm, d):
    return pl.BlockSpec((tm, d), lambda i: (i, 0))


def _vec_spec(d):
    return pl.BlockSpec((1, d), lambda i: (0, 0))


def _norm_mod_fwd(x, ng, sc, sh, name):
    t, d = x.shape
    tm = _pick(t, 512, 16)

    def body(x_ref, ng_ref, sc_ref, sh_ref, h_ref):
        h_ref[...] = _norm_mod_tile(x_ref[...], ng_ref[...], sc_ref[...], sh_ref[...]).astype(BF16)

    return pl.pallas_call(
        body, name=name, grid=(t // tm,),
        in_specs=[_row_spec(tm, d), _vec_spec(d), _vec_spec(d), _vec_spec(d)],
        out_specs=_row_spec(tm, d), out_shape=jax.ShapeDtypeStruct((t, d), BF16),
        compiler_params=_params(),
    )(x, ng, sc, sh)


def _norm_mod_bwd(x, ng, sc, sh, dh, dres, name):
    t, d = x.shape
    tm = _pick(t, 256, 16)
    has_res = dres is not None

    def body(*refs):
        if has_res:
            x_ref, ng_ref, sc_ref, sh_ref, dh_ref, dres_ref, dx_ref, dng_ref, dsc_ref, dsh_ref = refs
        else:
            x_ref, ng_ref, sc_ref, sh_ref, dh_ref, dx_ref, dng_ref, dsc_ref, dsh_ref = refs
        _, vjp = jax.vjp(_norm_mod_tile, x_ref[...], ng_ref[...], sc_ref[...], sh_ref[...])
        dx, dng, dsc, dsh = vjp(dh_ref[...].astype(F32))
        if has_res:
            dx = dx + dres_ref[...]
        dx_ref[...] = dx

        @pl.when(pl.program_id(0) == 0)
        def _():
            dng_ref[...] = jnp.zeros_like(dng_ref)
            dsc_ref[...] = jnp.zeros_like(dsc_ref)
            dsh_ref[...] = jnp.zeros_like(dsh_ref)

        dng_ref[...] += dng
        dsc_ref[...] += dsc
        dsh_ref[...] += dsh

    ins = [x, ng, sc, sh, dh] + ([dres] if has_res else [])
    in_specs = [_row_spec(tm, d), _vec_spec(d), _vec_spec(d), _vec_spec(d), _row_spec(tm, d)]
    in_specs += [_row_spec(tm, d)] if has_res else []
    vec = jax.ShapeDtypeStruct((1, d), F32)
    return pl.pallas_call(
        body, name=name, grid=(t // tm,), in_specs=in_specs,
        out_specs=[_row_spec(tm, d), _vec_spec(d), _vec_spec(d), _vec_spec(d)],
        out_shape=[jax.ShapeDtypeStruct((t, d), F32), vec, vec, vec],
        compiler_params=_params(),
    )(*ins)


def _res_mm_fwd(a, w, x, gate, coef, name):
    t, kd = a.shape
    d = w.shape[1]
    tm, tn, tk = _pick(t, 1024, 16), _pick(d, 1024), _pick(kd, 512)

    def epilogue(accs, in_refs, out_refs):
        f = accs[0]
        out_refs[0][...] = in_refs[2][...] + (coef * in_refs[3][...]) * f
        out_refs[1][...] = f.astype(BF16)

    tile = pl.BlockSpec((tm, tn), lambda i, j, k: (i, j))
    return _mm_call(
        name, (t // tm, d // tn, kd // tk),
        [(a, pl.BlockSpec((tm, tk), lambda i, j, k: (i, k))), (w, pl.BlockSpec((tk, tn), lambda i, j, k: (k, j))),
         (x, tile), (gate, pl.BlockSpec((1, tn), lambda i, j, k: (0, j)))],
        [(0, 1, NN, 0)],
        [(jax.ShapeDtypeStruct((t, d), F32), tile), (jax.ShapeDtypeStruct((t, d), BF16), tile)],
        [(tm, tn)], epilogue)


def _gate_bwd(dxo, f, gate, coef, name):
    t, d = dxo.shape
    tm = _pick(t, 512, 16)

    def body(dxo_ref, f_ref, gate_ref, df_ref, dgate_ref):
        dxo_t = dxo_ref[...]
        df_ref[...] = ((coef * gate_ref[...]) * dxo_t).astype(BF16)

        @pl.when(pl.program_id(0) == 0)
        def _():
            dgate_ref[...] = jnp.zeros_like(dgate_ref)

        dgate_ref[...] += coef * jnp.sum(dxo_t * f_ref[...].astype(F32), axis=0, keepdims=True)

    return pl.pallas_call(
        body, name=name, grid=(t // tm,),
        in_specs=[_row_spec(tm, d), _row_spec(tm, d), _vec_spec(d)],
        out_specs=[_row_spec(tm, d), _vec_spec(d)],
        out_shape=[jax.ShapeDtypeStruct((t, d), BF16), jax.ShapeDtypeStruct((1, d), F32)],
        compiler_params=_params(),
    )(dxo, f, gate)


def _ffn_in_fwd(h, w_in, name):
    t, d = h.shape
    n = w_in.shape[2]
    half = N_DEV // 2
    f = half * n
    tm = _pick(t, 512, 16)

    def epilogue(accs, in_refs, out_refs):
        g, u = accs
        out_refs[0][...] = (g * jax.nn.sigmoid(g) * u).astype(BF16)
        out_refs[1][0] = g.astype(BF16)
        out_refs[1][1] = u.astype(BF16)

    return _mm_call(
        name, (half, t // tm, 1),
        [(h, pl.BlockSpec((tm, d), lambda j, i, k: (i, 0))),
         (w_in, pl.BlockSpec((None, d, n), lambda j, i, k: (j, 0, 0))),
         (w_in, pl.BlockSpec((None, d, n), lambda j, i, k: (j + half, 0, 0)))],
        [(0, 1, NN, 0), (0, 2, NN, 1)],
        [(jax.ShapeDtypeStruct((t, f), BF16), pl.BlockSpec((tm, n), lambda j, i, k: (i, j))),
         (jax.ShapeDtypeStruct((2, t, f), BF16), pl.BlockSpec((2, tm, n), lambda j, i, k: (0, i, j)))],
        [(tm, n), (tm, n)], epilogue)


def _ffn_da_bwd(df, w_out2d, gu, name):
    t, d = df.shape
    f = w_out2d.shape[0]
    half = N_DEV // 2
    n = f // half
    tm = _pick(t, 512, 16)

    def epilogue(accs, in_refs, out_refs):
        da = accs[0]
        g = in_refs[2][0].astype(F32)
        u = in_refs[2][1].astype(F32)
        s = jax.nn.sigmoid(g)
        out_refs[0][0] = (da * u * (s * (1.0 + g * (1.0 - s)))).astype(BF16)
        out_refs[0][1] = (da * (g * s)).astype(BF16)

    gu_spec = pl.BlockSpec((2, tm, n), lambda j, i, k: (0, i, j))
    return _mm_call(
        name, (half, t // tm, 1),
        [(df, pl.BlockSpec((tm, d), lambda j, i, k: (i, 0))),
         (w_out2d, pl.BlockSpec((n, d), lambda j, i, k: (j, 0))),
         (gu, gu_spec)],
        [(0, 1, NT, 0)],
        [(jax.ShapeDtypeStruct((2, t, f), BF16), gu_spec)],
        [(tm, n)], epilogue)[0]


def _ffn_dh_bwd(dgu, w_in, name):
    _, t, f = dgu.shape
    d, n = w_in.shape[1], w_in.shape[2]
    half = N_DEV // 2
    tm = _pick(t, 512, 16)

    def epilogue(accs, in_refs, out_refs):
        out_refs[0][...] = accs[0]

    return _mm_call(
        name, (t // tm, 1, half),
        [(dgu, pl.BlockSpec((None, tm, n), lambda i, j, k: (0, i, k))),
         (dgu, pl.BlockSpec((None, tm, n), lambda i, j, k: (1, i, k))),
         (w_in, pl.BlockSpec((None, d, n), lambda i, j, k: (k, 0, 0))),
         (w_in, pl.BlockSpec((None, d, n), lambda i, j, k: (k + half, 0, 0)))],
        [(0, 2, NT, 0), (1, 3, NT, 0)],
        [(jax.ShapeDtypeStruct((t, d), F32), pl.BlockSpec((tm, d), lambda i, j, k: (i, 0)))],
        [(tm, d)], epilogue)[0]


def _ffn_dwin_bwd(h, dgu, name):
    t, d = h.shape
    f = dgu.shape[2]
    half = N_DEV // 2
    n = f // half
    tk = _pick(t, 512, 16)

    def epilogue(accs, in_refs, out_refs):
        out_refs[0][...] = accs[0].astype(BF16)

    return _mm_call(
        name, (N_DEV, 1, t // tk),
        [(h, pl.BlockSpec((tk, d), lambda j, i, k: (k, 0))),
         (dgu, pl.BlockSpec((None, tk, n), lambda j, i, k: (j // half, k, j % half)))],
        [(0, 1, TN, 0)],
        [(jax.ShapeDtypeStruct((N_DEV, d, n), BF16), pl.BlockSpec((None, d, n), lambda j, i, k: (j, 0, 0)))],
        [(d, n)], epilogue)[0]


def _make_ffn_block(tag):
    @jax.custom_vjp
    def ffn_block(x, ng, sh, sc, gate, w_in, w_out):
        return fwd(x, ng, sh, sc, gate, w_in, w_out)[0]

    def fwd(x, ng, sh, sc, gate, w_in, w_out):
        f = w_out.shape[0] * w_out.shape[1]
        w_out2d = w_out.reshape(f, w_out.shape[2])
        h = _norm_mod_fwd(x, ng, sc, sh, tag + "_norm")
        a, gu = _ffn_in_fwd(h, w_in, tag + "_in")
        xo, f1 = _res_mm_fwd(a, w_out2d, x, gate, 0.5, tag + "_out")
        return xo, (x, ng, sh, sc, gate, w_in, w_out, h, a, gu, f1)

    def bwd(res, dxo):
        x, ng, sh, sc, gate, w_in, w_out, h, a, gu, f1 = res
        f = w_out.shape[0] * w_out.shape[1]
        w_out2d = w_out.reshape(f, w_out.shape[2])
        df, dgate = _gate_bwd(dxo, f1, gate, 0.5, tag + "_dgate")
        dgu = _ffn_da_bwd(df, w_out2d, gu, tag + "_da")
        dw_out = _matmul(a, df, "tn", BF16, tag + "_dwout", tm=_pick(f, 1408, 16), tn=2048, tk=512)
        dh = _ffn_dh_bwd(dgu, w_in, tag + "_dh")
        dw_in = _ffn_dwin_bwd(h, dgu, tag + "_dwin")
        dx, dng, dsc, dsh = _norm_mod_bwd(x, ng, sc, sh, dh, dxo, tag + "_dnorm")
        return dx, dng, dsh, dsc, dgate, dw_in, dw_out.reshape(w_out.shape)

    ffn_block.defvjp(fwd, bwd)
    return ffn_block


def _make_norm_proj(tag):
    @jax.custom_vjp
    def norm_proj(x, ng, sh, sc, w):
        return fwd(x, ng, sh, sc, w)[0]

    def fwd(x, ng, sh, sc, w):
        h = _norm_mod_fwd(x, ng, sc, sh, tag + "_norm")
        p = _matmul(h, w, "nn", F32, tag + "_mm", tm=1024, tn=1024, tk=w.shape[0])
        return p, (x, ng, sh, sc, w, h)

    def bwd(res, dp):
        x, ng, sh, sc, w, h = res
        dh = _matmul(dp, w, "nt", F32, tag + "_dh", tm=512, tn=w.shape[0], tk=512)
        dw = _matmul(h, dp, "tn", BF16, tag + "_dw", tm=w.shape[0], tn=1024, tk=512)
        dx, dng, dsc, dsh = _norm_mod_bwd(x, ng, sc, sh, dh, None, tag + "_dnorm")
        return dx, dng, dsh, dsc, dw

    norm_proj.defvjp(fwd, bwd)
    return norm_proj


def _make_res_proj(tag):
    @jax.custom_vjp
    def res_proj(a, w, x, gate):
        return fwd(a, w, x, gate)[0]

    def fwd(a, w, x, gate):
        xo, f = _res_mm_fwd(a, w, x, gate, 1.0, tag + "_mm")
        return xo, (a, w, gate, f)

    def bwd(res, dxo):
        a, w, gate, f = res
        df, dgate = _gate_bwd(dxo, f, gate, 1.0, tag + "_dgate")
        da = _matmul(df, w, "nt", BF16, tag + "_da", tm=1024, tn=1024, tk=512)
        dw = _matmul(a, df, "tn", BF16, tag + "_dw", tm=1024, tn=2048, tk=512)
        return da, dw, dxo, dgate

    res_proj.defvjp(fwd, bwd)
    return res_proj


def _make_small_mm(tag):
    @jax.custom_vjp
    def small_mm(a, w):
        return _matmul(a, w, "nn", F32, tag + "_mm", tm=a.shape[0], tn=768, tk=w.shape[0])

    def fwd(a, w):
        return small_mm(a, w), (a, w)

    def bwd(res, dr):
        a, w = res
        da = _matmul(dr, w, "nt", F32, tag + "_da", tm=a.shape[0], tn=w.shape[0], tk=768)
        dw = _matmul(a, dr, "tn", F32, tag + "_dw", tm=1024, tn=768, tk=a.shape[0])
        return da, dw

    small_mm.defvjp(fwd, bwd)
    return small_mm


def _ret_chunk_terms(lg, c, reverse):
    row = lax.broadcasted_iota(jnp.int32, (c, c), 0).astype(F32)
    col = lax.broadcasted_iota(jnp.int32, (c, c), 1).astype(F32)
    pos = lax.broadcasted_iota(jnp.int32, (c, 1), 0).astype(F32)
    if reverse:
        diff = col - row
        mask = diff > 0.0
        e_exp = float(c) - pos
        f_exp = pos
    else:
        diff = row - col
        mask = diff >= 0.0
        e_exp = pos + 1.0
        f_exp = float(c - 1) - pos
    diffm = jnp.where(mask, diff, 0.0)
    dm = jnp.where(mask, jnp.exp(lg * diffm), 0.0)
    return diffm, dm, e_exp, jnp.exp(lg * e_exp), f_exp, jnp.exp(lg * f_exp)


def _lane0(val):
    lane = lax.broadcasted_iota(jnp.int32, (1, LANE), 1)
    return jnp.where(lane == 0, val, 0.0)


def _make_ret_dir(tag, reverse):
    def heads_spec(nc, width, flip):
        if flip:
            return pl.BlockSpec((None, RET_CHUNK, width), lambda h, t: (h, nc - 1 - t, 0))
        return pl.BlockSpec((None, RET_CHUNK, width), lambda h, t: (h, t, 0))

    def state_spec(nc, flip):
        if flip:
            return pl.BlockSpec((None, None, RET_DK, RET_DV), lambda h, t: (h, nc - 1 - t, 0, 0))
        return pl.BlockSpec((None, None, RET_DK, RET_DV), lambda h, t: (h, t, 0, 0))

    lg_spec = pl.BlockSpec((None, 1, LANE), lambda h, t: (h, 0, 0))

    def fwd_call(q, k, v, lgb):
        hh, ll, _ = q.shape
        c = RET_CHUNK
        nc = ll // c

        def body(q_ref, k_ref, v_ref, lg_ref, y_ref, sall_ref, s_scr):
            @pl.when(pl.program_id(1) == 0)
            def _():
                s_scr[...] = jnp.zeros_like(s_scr)

            lg = lg_ref[...][:, :1]
            _, dm, _, xi, _, zeta = _ret_chunk_terms(lg, c, reverse)
            q_t, k_t, v_t = q_ref[...], k_ref[...], v_ref[...]
            s = s_scr[...]
            p = _dot(q_t, k_t, NT) * dm
            y_ref[...] = _dot(p, v_t, NN) + _dot(q_t * xi, s, NN)
            sall_ref[...] = s
            s_scr[...] = jnp.exp(lg * float(c)) * s + _dot(k_t * zeta, v_t, TN)

        return pl.pallas_call(
            body, name=tag + "_fwd", grid=(hh, nc),
            in_specs=[heads_spec(nc, RET_DK, reverse), heads_spec(nc, RET_DK, reverse),
                      heads_spec(nc, RET_DV, reverse), lg_spec],
            out_specs=[heads_spec(nc, RET_DV, reverse), state_spec(nc, reverse)],
            out_shape=[jax.ShapeDtypeStruct((hh, ll, RET_DV), F32),
                       jax.ShapeDtypeStruct((hh, nc, RET_DK, RET_DV), F32)],
            scratch_shapes=[pltpu.VMEM((RET_DK, RET_DV), F32)],
            compiler_params=_params(),
        )(q, k, v, lgb)

    def bwd_call(q, k, v, lgb, sall, dy):
        hh, ll, _ = q.shape
        c = RET_CHUNK
        nc = ll // c
        flip = not reverse

        def body(q_ref, k_ref, v_ref, lg_ref, sall_ref, dy_ref, dq_ref, dk_ref, dv_ref, dlg_ref, ds_scr):
            @pl.when(pl.program_id(1) == 0)
            def _():
                ds_scr[...] = jnp.zeros_like(ds_scr)
                dlg_ref[...] = jnp.zeros_like(dlg_ref)

            lg = lg_ref[...][:, :1]
            diffm, dm, e_exp, xi, f_exp, zeta = _ret_chunk_terms(lg, c, reverse)
            q_t, k_t, v_t, dy_t = q_ref[...], k_ref[...], v_ref[...], dy_ref[...]
            s = sall_ref[...]
            dsn = ds_scr[...]
            a = _dot(q_t, k_t, NT)
            da = _dot(dy_t, v_t, NT) * dm
            g = _dot(dy_t, s, NT)
            hm = _dot(v_t, dsn, NT)
            dq_ref[...] = _dot(da, k_t, NN) + xi * g
            dk_ref[...] = _dot(da, q_t, TN) + zeta * hm
            dv_ref[...] = _dot(a * dm, dy_t, TN) + _dot(k_t * zeta, dsn, NN)
            gc = jnp.exp(lg * float(c))
            ds_scr[...] = gc * dsn + _dot(q_t * xi, dy_t, TN)

            def total(m):
                return jnp.sum(jnp.sum(m, axis=1, keepdims=True), axis=0, keepdims=True)

            dl = (total(da * a * diffm) + total(e_exp * xi * q_t * g)
                  + float(c) * gc * total(s * dsn) + total(f_exp * zeta * k_t * hm))
            dlg_ref[...] += _lane0(dl)

        return pl.pallas_call(
            body, name=tag + "_bwd", grid=(hh, nc),
            in_specs=[heads_spec(nc, RET_DK, flip), heads_spec(nc, RET_DK, flip), heads_spec(nc, RET_DV, flip),
                      lg_spec, state_spec(nc, flip), heads_spec(nc, RET_DV, flip)],
            out_specs=[heads_spec(nc, RET_DK, flip), heads_spec(nc, RET_DK, flip), heads_spec(nc, RET_DV, flip),
                       lg_spec],
            out_shape=[jax.ShapeDtypeStruct((hh, ll, RET_DK), F32), jax.ShapeDtypeStruct((hh, ll, RET_DK), F32),
                       jax.ShapeDtypeStruct((hh, ll, RET_DV), F32), jax.ShapeDtypeStruct((hh, 1, LANE), F32)],
            scratch_shapes=[pltpu.VMEM((RET_DK, RET_DV), F32)],
            compiler_params=_params(),
        )(q, k, v, lgb, sall, dy)

    @jax.custom_vjp
    def ret_dir(q, k, v, lgb):
        return fwd_call(q, k, v, lgb)[0]

    def fwd(q, k, v, lgb):
        y, sall = fwd_call(q, k, v, lgb)
        return y, (q, k, v, lgb, sall)

    def bwd(res, dy):
        q, k, v, lgb, sall = res
        dq, dk, dv, dlg = bwd_call(q, k, v, lgb, sall, dy)
        return dq, dk, dv, dlg

    ret_dir.defvjp(fwd, bwd)
    return ret_dir


def _ret_out_tile(y, g):
    mu = jnp.mean(y, axis=-1, keepdims=True)
    var = jnp.mean(jnp.square(y - mu), axis=-1, keepdims=True)
    return (g * jax.nn.sigmoid(g)) * ((y - mu) * lax.rsqrt(var + GN_EPS))


def _make_ret_out(tag):
    def specs(tm):
        y_spec = pl.BlockSpec((None, tm, RET_DV), lambda h, i: (h, i, 0))
        g_spec = pl.BlockSpec((tm, RET_DV), lambda h, i: (i, h))
        return y_spec, g_spec

    def fwd_call(y, g):
        hh, n, _ = y.shape
        tm = _pick(n, 1024, 16)
        y_spec, g_spec = specs(tm)

        def body(y_ref, g_ref, o_ref):
            o_ref[...] = _ret_out_tile(y_ref[...], g_ref[...]).astype(BF16)

        return pl.pallas_call(
            body, name=tag + "_fwd", grid=(hh, n // tm), in_specs=[y_spec, g_spec], out_specs=g_spec,
            out_shape=jax.ShapeDtypeStruct((n, hh * RET_DV), BF16), compiler_params=_params(),
        )(y, g)

    def bwd_call(y, g, do):
        hh, n, _ = y.shape
        tm = _pick(n, 1024, 16)
        y_spec, g_spec = specs(tm)

        def body(y_ref, g_ref, do_ref, dy_ref, dg_ref):
            _, vjp = jax.vjp(_ret_out_tile, y_ref[...], g_ref[...])
            dy, dg = vjp(do_ref[...].astype(F32))
            dy_ref[...] = dy
            dg_ref[...] = dg

        return pl.pallas_call(
            body, name=tag + "_bwd", grid=(hh, n // tm), in_specs=[y_spec, g_spec, g_spec],
            out_specs=[y_spec, g_spec],
            out_shape=[jax.ShapeDtypeStruct(y.shape, F32), jax.ShapeDtypeStruct(g.shape, F32)],
            compiler_params=_params(),
        )(y, g, do)

    @jax.custom_vjp
    def ret_out(y, g):
        return fwd_call(y, g)

    def fwd(y, g):
        return fwd_call(y, g), (y, g)

    def bwd(res, do):
        y, g = res
        return tuple(bwd_call(y, g, do))

    ret_out.defvjp(fwd, bwd)
    return ret_out


def _make_attention(tag):
    neg_big = -1e30

    def tiles(n, ll):
        return _pick(n, 512, 16), _pick(ll, 768)

    def fwd_call(q, k, v):
        hh, n, dq = q.shape
        ll, dv = k.shape[1], v.shape[2]
        tq, tk = tiles(n, ll)
        scale = dq ** -0.5
        k_steps = ll // tk

        def body(q_ref, k_ref, v_ref, o_ref, lse_ref, m_scr, l_scr, acc_scr):
            j = pl.program_id(2)

            @pl.when(j == 0)
            def _():
                m_scr[...] = jnp.full_like(m_scr, neg_big)
                l_scr[...] = jnp.zeros_like(l_scr)
                acc_scr[...] = jnp.zeros_like(acc_scr)

            s = _dot(q_ref[...], k_ref[...], NT) * scale
            m_prev = m_scr[...]
            m_new = jnp.maximum(m_prev, jnp.max(s, axis=1, keepdims=True))
            p = jnp.exp(s - m_new)
            alpha = jnp.exp(m_prev - m_new)
            l_scr[...] = alpha * l_scr[...] + jnp.sum(p, axis=1, keepdims=True)
            acc_scr[...] = alpha * acc_scr[...] + _dot(p, v_ref[...], NN)
            m_scr[...] = m_new

            @pl.when(j == k_steps - 1)
            def _():
                o_ref[...] = (acc_scr[...] / l_scr[...]).astype(BF16)
                lse_ref[...] = m_scr[...] + jnp.log(l_scr[...])

        return pl.pallas_call(
            body, name=tag + "_fwd", grid=(hh, n // tq, k_steps),
            in_specs=[pl.BlockSpec((None, tq, dq), lambda h, i, j: (h, i, 0)),
                      pl.BlockSpec((None, tk, dq), lambda h, i, j: (h, j, 0)),
                      pl.BlockSpec((None, tk, dv), lambda h, i, j: (h, j, 0))],
            out_specs=[pl.BlockSpec((tq, dv), lambda h, i, j: (i, h)),
                       pl.BlockSpec((None, tq, 1), lambda h, i, j: (h, i, 0))],
            out_shape=[jax.ShapeDtypeStruct((n, hh * dv), BF16), jax.ShapeDtypeStruct((hh, n, 1), F32)],
            scratch_shapes=[pltpu.VMEM((tq, 1), F32), pltpu.VMEM((tq, 1), F32), pltpu.VMEM((tq, dv), F32)],
            compiler_params=_params(),
        )(q, k, v)

    def delta_call(o, do, hh):
        n = o.shape[0]
        dv = o.shape[1] // hh
        tq = _pick(n, 1024, 16)

        def body(o_ref, do_ref, d_ref):
            d_ref[...] = jnp.sum(o_ref[...].astype(F32) * do_ref[...].astype(F32), axis=1, keepdims=True)

        spec = pl.BlockSpec((tq, dv), lambda h, i: (i, h))
        return pl.pallas_call(
            body, name=tag + "_delta", grid=(hh, n // tq), in_specs=[spec, spec],
            out_specs=pl.BlockSpec((None, tq, 1), lambda h, i: (h, i, 0)),
            out_shape=jax.ShapeDtypeStruct((hh, n, 1), F32), compiler_params=_params(),
        )(o, do)

    def bwd_call(q, k, v, do, lse, delta):
        hh, n, dq = q.shape
        ll, dv = k.shape[1], v.shape[2]
        tq, tk = tiles(n, ll)
        scale = dq ** -0.5
        q_steps = n // tq

        def body(q_ref, k_ref, v_ref, do_ref, lse_ref, d_ref, dq_ref, dk_ref, dv_ref, dk_scr, dv_scr):
            j = pl.program_id(1)
            i = pl.program_id(2)

            @pl.when(i == 0)
            def _():
                dk_scr[...] = jnp.zeros_like(dk_scr)
                dv_scr[...] = jnp.zeros_like(dv_scr)

            q_t, k_t, do_t = q_ref[...], k_ref[...], do_ref[...]
            s = _dot(q_t, k_t, NT) * scale
            p = jnp.exp(s - lse_ref[...])
            dp = _dot(do_t, v_ref[...], NT)
            ds = p * (dp - d_ref[...]) * scale
            dv_scr[...] += _dot(p, do_t, TN)
            dk_scr[...] += _dot(ds, q_t, TN)
            dq_part = _dot(ds, k_t, NN)
            rows = pl.ds(pl.multiple_of(i * tq, tq), tq)

            @pl.when(j == 0)
            def _():
                dq_ref[rows, :] = dq_part

            @pl.when(j > 0)
            def _():
                dq_ref[rows, :] += dq_part

            @pl.when(i == q_steps - 1)
            def _():
                dk_ref[...] = dk_scr[...].astype(BF16)
                dv_ref[...] = dv_scr[...].astype(BF16)

        return pl.pallas_call(
            body, name=tag + "_bwd", grid=(hh, ll // tk, q_steps),
            in_specs=[pl.BlockSpec((None, tq, dq), lambda h, j, i: (h, i, 0)),
                      pl.BlockSpec((None, tk, dq), lambda h, j, i: (h, j, 0)),
                      pl.BlockSpec((None, tk, dv), lambda h, j, i: (h, j, 0)),
                      pl.BlockSpec((tq, dv), lambda h, j, i: (i, h)),
                      pl.BlockSpec((None, tq, 1), lambda h, j, i: (h, i, 0)),
                      pl.BlockSpec((None, tq, 1), lambda h, j, i: (h, i, 0))],
            out_specs=[pl.BlockSpec((None, n, dq), lambda h, j, i: (h, 0, 0)),
                       pl.BlockSpec((None, tk, dq), lambda h, j, i: (h, j, 0)),
                       pl.BlockSpec((None, tk, dv), lambda h, j, i: (h, j, 0))],
            out_shape=[jax.ShapeDtypeStruct((hh, n, dq), F32), jax.ShapeDtypeStruct((hh, ll, dq), BF16),
                       jax.ShapeDtypeStruct((hh, ll, dv), BF16)],
            scratch_shapes=[pltpu.VMEM((tk, dq), F32), pltpu.VMEM((tk, dv), F32)],
            compiler_params=_params(),
        )(q, k, v, do, lse, delta)

    @jax.custom_vjp
    def attention(q, k, v):
        return fwd_call(q, k, v)[0]

    def fwd(q, k, v):
        o, lse = fwd_call(q, k, v)
        return o, (q, k, v, o, lse)

    def bwd(res, do):
        q, k, v, o, lse = res
        delta = delta_call(o, do, q.shape[0])
        dq, dk, dv = bwd_call(q, k, v, do, lse, delta)
        return dq.astype(BF16), dk, dv

    attention.defvjp(fwd, bwd)
    return attention


def _loss_tile(x, g, tgt):
    r = lax.rsqrt(jnp.mean(x * x, axis=-1, keepdims=True) + RMS_EPS)
    err = x * r * g - tgt
    per_tok = jnp.mean(err * err, axis=-1, keepdims=True)
    return 0.5 * jnp.sum(per_tok, axis=0, keepdims=True)


def _make_final_loss(tag):
    def fwd_call(x, g, tgt):
        t, d = x.shape
        tm = _pick(t, 512, 16)

        def body(x_ref, g_ref, t_ref, l_ref):
            l_ref[...] = jnp.broadcast_to(_loss_tile(x_ref[...], g_ref[...], t_ref[...]), (1, LANE))

        parts = pl.pallas_call(
            body, name=tag + "_fwd", grid=(t // tm,),
            in_specs=[_row_spec(tm, d), _vec_spec(d), _row_spec(tm, d)],
            out_specs=pl.BlockSpec((None, 1, LANE), lambda i: (i, 0, 0)),
            out_shape=jax.ShapeDtypeStruct((t // tm, 1, LANE), F32), compiler_params=_params(),
        )(x, g, tgt)
        return jnp.sum(parts[:, 0, 0])

    def bwd_call(x, g, tgt, dl):
        t, d = x.shape
        tm = _pick(t, 256, 16)

        def body(x_ref, g_ref, t_ref, dl_ref, dx_ref, dg_ref):
            _, vjp = jax.vjp(_loss_tile, x_ref[...], g_ref[...], t_ref[...])
            dx, dg, _ = vjp(dl_ref[...])
            dx_ref[...] = dx

            @pl.when(pl.program_id(0) == 0)
            def _():
                dg_ref[...] = jnp.zeros_like(dg_ref)

            dg_ref[...] += dg

        return pl.pallas_call(
            body, name=tag + "_bwd", grid=(t // tm,),
            in_specs=[_row_spec(tm, d), _vec_spec(d), _row_spec(tm, d), pl.BlockSpec((1, 1), lambda i: (0, 0))],
            out_specs=[_row_spec(tm, d), _vec_spec(d)],
            out_shape=[jax.ShapeDtypeStruct((t, d), F32), jax.ShapeDtypeStruct((1, d), F32)],
            compiler_params=_params(),
        )(x, g, tgt, dl)

    @jax.custom_vjp
    def final_loss(x, g, tgt):
        return fwd_call(x, g, tgt)

    def fwd(x, g, tgt):
        return fwd_call(x, g, tgt), (x, g, tgt)

    def bwd(res, dl):
        x, g, tgt = res
        dx, dg = bwd_call(x, g, tgt, dl.reshape(1, 1).astype(F32))
        return dx, dg, jnp.zeros_like(tgt)

    final_loss.defvjp(fwd, bwd)
    return final_loss


def _exchange(arrays, gather, name):
    n = len(arrays)

    def body(*refs):
        ins, outs = refs[:n], refs[n:2 * n]
        send_sems, recv_sems, local_sems = refs[2 * n:]
        me = 4 * lax.axis_index("x") + 2 * lax.axis_index("y") + lax.axis_index("c")

        def remote(a, d, wait_side=False):
            peer = (me + d) % N_DEV
            origin = (me + N_DEV - d) % N_DEV
            src = ins[a] if gather else ins[a].at[peer]
            dst = outs[a].at[origin if wait_side else me]
            return pltpu.make_async_remote_copy(
                src_ref=src, dst_ref=dst, send_sem=send_sems.at[a, d - 1], recv_sem=recv_sems.at[a, d - 1],
                device_id=(peer // 4, (peer // 2) % 2, peer % 2), device_id_type=pl.DeviceIdType.MESH)

        def local(a):
            src = ins[a] if gather else ins[a].at[me]
            return pltpu.make_async_copy(src, outs[a].at[me], local_sems.at[a])

        for a in range(n):
            for d in range(1, N_DEV):
                remote(a, d).start()
            local(a).start()
        for a in range(n):
            local(a).wait()
            for d in range(1, N_DEV):
                remote(a, d, wait_side=True).wait_recv()
                remote(a, d).wait_send()

    out_shape = []
    for arr in arrays:
        shape = (N_DEV,) + arr.shape if gather else arr.shape
        out_shape.append(jax.ShapeDtypeStruct(shape, arr.dtype))
    any_spec = pl.BlockSpec(memory_space=pl.ANY)
    return pl.pallas_call(
        body, name=name, in_specs=[any_spec] * n, out_specs=[any_spec] * n, out_shape=out_shape,
        scratch_shapes=[pltpu.SemaphoreType.DMA((n, N_DEV - 1)), pltpu.SemaphoreType.DMA((n, N_DEV - 1)),
                        pltpu.SemaphoreType.DMA((n,))],
        compiler_params=pltpu.CompilerParams(has_side_effects=True),
    )(*arrays)


def _make_gather_op(tag):
    @jax.custom_vjp
    def gather_op(xl):
        return _exchange([xl], True, tag + "_gather")[0]

    def fwd(xl):
        return gather_op(xl), None

    def bwd(_, g):
        return (jnp.sum(_exchange([g], False, tag + "_scatter")[0], axis=0),)

    gather_op.defvjp(fwd, bwd)
    return gather_op


def _adamw(gstack, w, m, v, name):
    s, r, cn = gstack.shape
    tr = _pick(r, max(8, (2 * 1024 * 1024) // (4 * cn) // 8 * 8), 8)
    c1 = 1.0 - ADAM_B1 ** ADAM_STEP
    c2 = 1.0 - ADAM_B2 ** ADAM_STEP

    def body(g_ref, w_ref, m_ref, v_ref, go_ref, d_ref, mo_ref, vo_ref):
        g = g_ref[0].astype(F32)
        for q in range(1, s):
            g = g + g_ref[q].astype(F32)
        m_new = ADAM_B1 * m_ref[...] + (1.0 - ADAM_B1) * g
        v_new = ADAM_B2 * v_ref[...] + (1.0 - ADAM_B2) * (g * g)
        go_ref[...] = g
        mo_ref[...] = m_new
        vo_ref[...] = v_new
        d_ref[...] = -ADAM_LR * ((m_new / c1) / (jnp.sqrt(v_new / c2) + ADAM_EPS) + ADAM_WD * w_ref[...])

    tile = pl.BlockSpec((tr, cn), lambda i: (i, 0))
    out = jax.ShapeDtypeStruct((r, cn), F32)
    return pl.pallas_call(
        body, name=name, grid=(r // tr,),
        in_specs=[pl.BlockSpec((s, tr, cn), lambda i: (0, i, 0)), tile, tile, tile],
        out_specs=[tile, tile, tile, tile], out_shape=[out, out, out, out],
        compiler_params=_params(),
    )(gstack, w, m, v)


def _rope_tables(pos, dim, base):
    inv = base ** (-jnp.arange(0, dim, 2, dtype=F32) / dim)
    ang = pos.astype(F32)[:, None] * inv[None, :]
    return jnp.cos(ang)[:, None, :], jnp.sin(ang)[:, None, :]


def _rotate(x, cos, sin):
    x1, x2 = jnp.split(x, 2, axis=-1)
    return jnp.concatenate([x1 * cos - x2 * sin, x2 * cos + x1 * sin], axis=-1)


def _axial_rope(x, row_tab, col_tab):
    xr, xc = jnp.split(x, 2, axis=-1)
    return jnp.concatenate([_rotate(xr, *row_tab), _rotate(xc, *col_tab)], axis=-1)


def _heads(t, h):
    return jnp.swapaxes(t.reshape(t.shape[0], h, t.shape[1] // h), 0, 1)


def _cols_from_stack(w):
    return jnp.swapaxes(w, 0, 1).reshape(w.shape[1], N_DEV * w.shape[2])


def _local_loss(p, ctx, silu_c_all, tgt, me):
    x = p["x"]
    n_lat, d = x.shape
    n_ctx = ctx.shape[0]
    n_a = p["ada_w"].shape[1]

    a_in = jnp.concatenate([silu_c_all, jax.nn.silu(p["c_ctx"])[None, :], jnp.zeros((7, d), F32)], axis=0)
    b_loc = lax.dynamic_slice(p["ada_b"], (0, me * n_a), (1, n_a))
    r_loc = _make_small_mm("ada")(a_in, p["ada_w"]) + b_loc
    r_full = _make_gather_op("ada")(r_loc)
    m_lat = lax.dynamic_index_in_dim(r_full, me, axis=1, keepdims=False).reshape(N_MOD, 1, d)
    m_ctx = r_full[:, N_DEV, :].reshape(N_MOD, 1, d)

    x1 = _make_ffn_block("ffn1")(x, p["norm1_g"], m_lat[0], m_lat[1], m_lat[2], p["ffn1_w_in"], p["ffn1_w_out"])
    c1 = _make_ffn_block("ffn1c")(ctx, p["norm1_g"], m_ctx[0], m_ctx[1], m_ctx[2], p["ffn1_w_in"], p["ffn1_w_out"])

    w_mix = jnp.pad(_cols_from_stack(p["mix_w_in"]), ((0, 0), (0, MIX_IN_PAD - MIX_IN)))
    proj = _make_norm_proj("mix")(x1, p["norm2_g"], m_lat[3], m_lat[4], w_mix)
    proj_c = _make_norm_proj("mixc")(c1, p["norm2_g"], m_ctx[3], m_ctx[4], w_mix)
    idx = [0]
    for s in SPLITS:
        idx.append(idx[-1] + s)
    rq, rk, rv, rg, cq, ckv, kr = [proj[:, idx[i]:idx[i + 1]] for i in range(7)]
    crk, crv, cckv, ckr = [proj_c[:, idx[i]:idx[i + 1]] for i in (1, 2, 5, 6)]

    zq = jnp.zeros((1, MLA_Q_RANK), F32)
    zkv = jnp.zeros((1, MLA_KV_RANK), F32)
    w_uq = _cols_from_stack(p["mla_w_uq"])
    w_ukv = _cols_from_stack(p["mla_w_ukv"])
    q = _make_norm_proj("uq")(cq, p["mla_q_norm_g"], zq, zq, w_uq)
    kv = _make_norm_proj("ukv")(ckv, p["mla_kv_norm_g"], zkv, zkv, w_ukv)
    kv_c = _make_norm_proj("ukvc")(cckv, p["mla_kv_norm_g"], zkv, zkv, w_ukv)

    lg_f = jax.nn.log_sigmoid(p["ret_decay_fwd"][0])
    lg_b = jax.nn.log_sigmoid(p["ret_decay_bwd"][0])
    ret_tab = _rope_tables(jnp.arange(n_lat), RET_DK, RET_ROPE_BASE)
    rq_h = jnp.swapaxes(_rotate(rq.reshape(n_lat, RET_HEADS, RET_DK), *ret_tab), 0, 1)
    rk_h = jnp.swapaxes(_rotate((rk * (RET_DK ** -0.5)).reshape(n_lat, RET_HEADS, RET_DK), *ret_tab), 0, 1)
    rv_h = _heads(rv, RET_HEADS)
    crk_h = _heads(crk * (RET_DK ** -0.5), RET_HEADS)
    crv_h = _heads(crv, RET_HEADS)
    zero_q = jnp.zeros((RET_HEADS, n_ctx, RET_DK), F32)

    def lanes(lg):
        return jnp.broadcast_to(lg[:, None, None], (RET_HEADS, 1, LANE))

    y_f = _make_ret_dir("retf", False)(
        jnp.concatenate([zero_q, rq_h], axis=1), jnp.concatenate([crk_h, rk_h], axis=1),
        jnp.concatenate([crv_h, rv_h], axis=1), lanes(lg_f))[:, n_ctx:]
    y_b = _make_ret_dir("retb", True)(
        jnp.concatenate([rq_h, zero_q], axis=1), jnp.concatenate([rk_h, crk_h], axis=1),
        jnp.concatenate([rv_h, crv_h], axis=1), lanes(lg_b))[:, :n_lat]
    ret_o = _make_ret_out("reto")(y_f + y_b, rg)

    n_rows = n_lat // GRID_W
    row_tab = _rope_tables(jnp.repeat(jnp.arange(n_rows), GRID_W), MLA_ROPE // 2, AXIAL_BASE)
    col_tab = _rope_tables(jnp.tile(jnp.arange(GRID_W), n_rows), MLA_ROPE // 2, AXIAL_BASE)
    q3 = q.reshape(n_lat, MLA_HEADS, MLA_NOPE + MLA_ROPE)
    q_all = jnp.concatenate([q3[..., :MLA_NOPE], _axial_rope(q3[..., MLA_NOPE:], row_tab, col_tab)], axis=-1)
    kv3 = kv.reshape(n_lat, MLA_HEADS, MLA_NOPE + MLA_V)
    kvc3 = kv_c.reshape(n_ctx, MLA_HEADS, MLA_NOPE + MLA_V)
    kr_rot = _axial_rope(kr[:, None, :], row_tab, col_tab)
    k_lat = jnp.concatenate([kv3[..., :MLA_NOPE], jnp.broadcast_to(kr_rot, (n_lat, MLA_HEADS, MLA_ROPE))], axis=-1)
    k_ctx = jnp.concatenate(
        [kvc3[..., :MLA_NOPE], jnp.broadcast_to(ckr[:, None, :], (n_ctx, MLA_HEADS, MLA_ROPE))], axis=-1)
    k_all = jnp.concatenate([k_lat, k_ctx], axis=0)
    v_all = jnp.concatenate([kv3[..., MLA_NOPE:], kvc3[..., MLA_NOPE:]], axis=0)
    mla_o = _make_attention("mla")(
        jnp.swapaxes(q_all, 0, 1).astype(BF16), jnp.swapaxes(k_all, 0, 1).astype(BF16),
        jnp.swapaxes(v_all, 0, 1).astype(BF16))

    w_mo = p["mix_w_out"].reshape(-1, d)
    x2 = _make_res_proj("mixo")(jnp.concatenate([ret_o, mla_o], axis=-1), w_mo, x1, m_lat[5])
    x3 = _make_ffn_block("ffn2")(x2, p["norm3_g"], m_lat[6], m_lat[7], m_lat[8], p["ffn2_w_in"], p["ffn2_w_out"])
    return _make_final_loss("loss")(x3, p["final_norm_g"], tgt)


BIG = ("ffn1_w_in", "ffn1_w_out", "mix_w_in", "mla_w_uq", "mla_w_ukv", "mix_w_out", "ffn2_w_in", "ffn2_w_out")
SMALL = ("c_ctx", "ada_b", "norm1_g", "norm2_g", "ret_decay_fwd", "ret_decay_bwd", "mla_q_norm_g",
         "mla_kv_norm_g", "norm3_g", "final_norm_g")
WEIGHTS = ("c_ctx", "ada_w", "ada_b", "norm1_g", "ffn1_w_in", "ffn1_w_out", "norm2_g", "mix_w_in", "ret_decay_fwd",
           "ret_decay_bwd", "mla_q_norm_g", "mla_w_uq", "mla_kv_norm_g", "mla_w_ukv", "mix_w_out", "norm3_g",
           "ffn2_w_in", "ffn2_w_out", "final_norm_g")


def _pack(parts):
    flat = jnp.concatenate([t.reshape(-1) for t in parts])
    pad = (-flat.shape[0]) % LANE
    return jnp.pad(flat, (0, pad)).reshape(1, -1)


def _unpack(flat, like):
    out, off = [], 0
    for t in like:
        out.append(flat[0, off:off + t.size].reshape(t.shape))
        off += t.size
    return out


def kernel(x, c, ctx, c_ctx, ada_w, ada_b, norm1_g, ffn1_w_in, ffn1_w_out, norm2_g, mix_w_in, ret_decay_fwd, ret_decay_bwd, mla_q_norm_g, mla_w_uq, mla_kv_norm_g, mla_w_ukv, mix_w_out, norm3_g, ffn2_w_in, ffn2_w_out, final_norm_g, loss_target, m_c_ctx, m_ada_w, m_ada_b, m_norm1_g, m_ffn1_w_in, m_ffn1_w_out, m_norm2_g, m_mix_w_in, m_ret_decay_fwd, m_ret_decay_bwd, m_mla_q_norm_g, m_mla_w_uq, m_mla_kv_norm_g, m_mla_w_ukv, m_mix_w_out, m_norm3_g, m_ffn2_w_in, m_ffn2_w_out, m_final_norm_g, v_c_ctx, v_ada_w, v_ada_b, v_norm1_g, v_ffn1_w_in, v_ffn1_w_out, v_norm2_g, v_mix_w_in, v_ret_decay_fwd, v_ret_decay_bwd, v_mla_q_norm_g, v_mla_w_uq, v_mla_kv_norm_g, v_mla_w_ukv, v_mix_w_out, v_norm3_g, v_ffn2_w_in, v_ffn2_w_out, v_final_norm_g):
    w = dict(c_ctx=c_ctx, ada_w=ada_w, ada_b=ada_b, norm1_g=norm1_g, ffn1_w_in=ffn1_w_in, ffn1_w_out=ffn1_w_out,
             norm2_g=norm2_g, mix_w_in=mix_w_in, ret_decay_fwd=ret_decay_fwd, ret_decay_bwd=ret_decay_bwd,
             mla_q_norm_g=mla_q_norm_g, mla_w_uq=mla_w_uq, mla_kv_norm_g=mla_kv_norm_g, mla_w_ukv=mla_w_ukv,
             mix_w_out=mix_w_out, norm3_g=norm3_g, ffn2_w_in=ffn2_w_in, ffn2_w_out=ffn2_w_out,
             final_norm_g=final_norm_g)
    mom = dict(c_ctx=m_c_ctx, ada_w=m_ada_w, ada_b=m_ada_b, norm1_g=m_norm1_g, ffn1_w_in=m_ffn1_w_in,
               ffn1_w_out=m_ffn1_w_out, norm2_g=m_norm2_g, mix_w_in=m_mix_w_in, ret_decay_fwd=m_ret_decay_fwd,
               ret_decay_bwd=m_ret_decay_bwd, mla_q_norm_g=m_mla_q_norm_g, mla_w_uq=m_mla_w_uq,
               mla_kv_norm_g=m_mla_kv_norm_g, mla_w_ukv=m_mla_w_ukv, mix_w_out=m_mix_w_out, norm3_g=m_norm3_g,
               ffn2_w_in=m_ffn2_w_in, ffn2_w_out=m_ffn2_w_out, final_norm_g=m_final_norm_g)
    var = dict(c_ctx=v_c_ctx, ada_w=v_ada_w, ada_b=v_ada_b, norm1_g=v_norm1_g, ffn1_w_in=v_ffn1_w_in,
               ffn1_w_out=v_ffn1_w_out, norm2_g=v_norm2_g, mix_w_in=v_mix_w_in, ret_decay_fwd=v_ret_decay_fwd,
               ret_decay_bwd=v_ret_decay_bwd, mla_q_norm_g=v_mla_q_norm_g, mla_w_uq=v_mla_w_uq,
               mla_kv_norm_g=v_mla_kv_norm_g, mla_w_ukv=v_mla_w_ukv, mix_w_out=v_mix_w_out, norm3_g=v_norm3_g,
               ffn2_w_in=v_ffn2_w_in, ffn2_w_out=v_ffn2_w_out, final_norm_g=v_final_norm_g)
    me = 4 * lax.axis_index("x") + 2 * lax.axis_index("y") + lax.axis_index("c")

    gathered = _exchange([w[k][0].astype(BF16) for k in BIG] + [jax.nn.silu(c)], True, "weights_gather")
    silu_c_all = gathered[-1][:, 0, :]

    p = dict(zip(BIG, gathered[:-1]))
    p["x"] = x[0]
    p["ada_w"] = ada_w[0]
    p["c_ctx"] = c_ctx
    p["ada_b"] = ada_b
    p["final_norm_g"] = final_norm_g[None, :]
    for k in ("norm1_g", "norm2_g", "norm3_g", "mla_q_norm_g", "mla_kv_norm_g", "ret_decay_fwd", "ret_decay_bwd"):
        p[k] = w[k]

    loss_local, grads = jax.value_and_grad(_local_loss)(p, ctx[0], silu_c_all, loss_target[0], me)
    loss = lax.psum(loss_local, MESH_AXES)
    grads["final_norm_g"] = grads["final_norm_g"][0]

    small_like = [w[k] for k in SMALL]
    exchanged = _exchange([grads[k] for k in BIG], False, "grads_scatter")
    small_all = _exchange([_pack([grads[k] for k in SMALL])], True, "small_grads_gather")[0]

    out_g, out_d, out_m, out_v = {}, {}, {}, {}

    def update(name, gstack, shape2d):
        res = _adamw(gstack, w[name].reshape(shape2d), mom[name].reshape(shape2d), var[name].reshape(shape2d),
                     "adamw_" + name)
        out_g[name], out_d[name], out_m[name], out_v[name] = [t.reshape(w[name].shape) for t in res]

    for k, gs in zip(BIG, exchanged):
        update(k, gs, gs.shape[1:])
    update("ada_w", grads["ada_w"][None], ada_w.shape[1:])
    res = _adamw(small_all, _pack(small_like), _pack([mom[k] for k in SMALL]), _pack([var[k] for k in SMALL]),
                 "adamw_small")
    for dst, flat in zip((out_g, out_d, out_m, out_v), res):
        for k, t in zip(SMALL, _unpack(flat, small_like)):
            dst[k] = t

    return (loss, grads["x"][None], *[out_g[k] for k in WEIGHTS], *[out_d[k] for k in WEIGHTS],
            *[out_m[k] for k in WEIGHTS], *[out_v[k] for k in WEIGHTS])
```

```python
import functools

import jax
import jax.numpy as jnp
from jax import lax
from jax.experimental import pallas as pl
from jax.experimental.pallas import tpu as pltpu

F32 = jnp.float32
BF16 = jnp.bfloat16

N_DEV = 8
MESH_AXES = ("x", "y", "c")

GRID_W = 64
N_MOD = 9
RET_HEADS = 8
RET_DK = 64
RET_DV = 128
RET_CHUNK = 256
RET_ROPE_BASE = 10000.0
MLA_HEADS = 8
MLA_Q_RANK = 512
MLA_KV_RANK = 256
MLA_NOPE = 128
MLA_ROPE = 64
MLA_V = 128
AXIAL_BASE = 10000.0
RMS_EPS = 1e-6
GN_EPS = 1e-5
SPLITS = (RET_HEADS * RET_DK, RET_HEADS * RET_DK, RET_HEADS * RET_DV, RET_HEADS * RET_DV,
          MLA_Q_RANK, MLA_KV_RANK, MLA_ROPE)
MIX_IN = sum(SPLITS)
MIX_IN_PAD = 4096

ADAM_LR = 0.001
ADAM_B1 = 0.9
ADAM_B2 = 0.999
ADAM_EPS = 1e-08
ADAM_WD = 0.01
ADAM_STEP = 10

LANE = 128
VMEM_LIMIT_BYTES = 56 * 1024 * 1024

NN = ((1,), (0,))
NT = ((1,), (1,))
TN = ((0,), (0,))


def _pick(dim, target, align=LANE):
    t = min(dim, target)
    t -= t % align
    while t >= align:
        if dim % t == 0:
            return t
        t -= align
    return dim


def _params():
    return pltpu.CompilerParams(vmem_limit_bytes=VMEM_LIMIT_BYTES)


def _dot(a, b, dims):
    return lax.dot_general(a.astype(BF16), b.astype(BF16), (dims, ((), ())), preferred_element_type=F32)


def _mm_call(name, grid, ins, pairs, outs, acc_shapes, epilogue):
    n_in, n_out = len(ins), len(outs)
    k_axis = len(grid) - 1
    k_steps = grid[k_axis]

    def body(*refs):
        in_refs = refs[:n_in]
        out_refs = refs[n_in:n_in + n_out]
        accs = refs[n_in + n_out:]
        k = pl.program_id(k_axis)

        @pl.when(k == 0)
        def _():
            for acc in accs:
                acc[...] = jnp.zeros_like(acc)

        for ai, bi, dims, ci in pairs:
            accs[ci][...] += _dot(in_refs[ai][...], in_refs[bi][...], dims)

        @pl.when(k == k_steps - 1)
        def _():
            epilogue([acc[...] for acc in accs], in_refs, out_refs)

    res = pl.pallas_call(
        body, name=name, grid=grid,
        in_specs=[s for _, s in ins], out_specs=[s for _, s in outs],
        out_shape=[s for s, _ in outs],
        scratch_shapes=[pltpu.VMEM(s, F32) for s in acc_shapes],
        compiler_params=_params(),
    )(*[a for a, _ in ins])
    return res


def _matmul(a, b, mode, out_dtype, name, tm=1024, tn=1024, tk=512):
    if mode == "nn":
        (m, kd), n = a.shape, b.shape[1]
    elif mode == "nt":
        (m, kd), n = a.shape, b.shape[0]
    else:
        (kd, m), n = a.shape, b.shape[1]
    tm, tn = _pick(m, tm, 16), _pick(n, tn)
    tk = _pick(kd, tk) if mode != "tn" else _pick(kd, tk, 16)
    if mode == "nn":
        a_spec = pl.BlockSpec((tm, tk), lambda i, j, k: (i, k))
        b_spec = pl.BlockSpec((tk, tn), lambda i, j, k: (k, j))
        dims = NN
    elif mode == "nt":
        a_spec = pl.BlockSpec((tm, tk), lambda i, j, k: (i, k))
        b_spec = pl.BlockSpec((tn, tk), lambda i, j, k: (j, k))
        dims = NT
    else:
        a_spec = pl.BlockSpec((tk, tm), lambda i, j, k: (k, i))
        b_spec = pl.BlockSpec((tk, tn), lambda i, j, k: (k, j))
        dims = TN

    def epilogue(accs, in_refs, out_refs):
        out_refs[0][...] = accs[0].astype(out_dtype)

    return _mm_call(
        name, (m // tm, n // tn, kd // tk), [(a, a_spec), (b, b_spec)], [(0, 1, dims, 0)],
        [(jax.ShapeDtypeStruct((m, n), out_dtype), pl.BlockSpec((tm, tn), lambda i, j, k: (i, j)))],
        [(tm, tn)], epilogue)[0]


def _norm_mod_tile(x, ng, sc, sh):
    r = lax.rsqrt(jnp.mean(x * x, axis=-1, keepdims=True) + RMS_EPS)
    return (x * r * ng) * (1.0 + sc) + sh


def _row_spec(tm, d):
    return pl.BlockSpec((tm, d), lambda i: (i, 0))


def _vec_spec(d):
    return pl.BlockSpec((1, d), lambda i: (0, 0))


def _norm_mod_fwd(x, ng, sc, sh, name):
    t, d = x.shape
    tm = _pick(t, 512, 16)

    def body(x_ref, ng_ref, sc_ref, sh_ref, h_ref):
        h_ref[...] = _norm_mod_tile(x_ref[...], ng_ref[...], sc_ref[...], sh_ref[...]).astype(BF16)

    return pl.pallas_call(
        body, name=name, grid=(t // tm,),
        in_specs=[_row_spec(tm, d), _vec_spec(d), _vec_spec(d), _vec_spec(d)],
        out_specs=_row_spec(tm, d), out_shape=jax.ShapeDtypeStruct((t, d), BF16),
        compiler_params=_params(),
    )(x, ng, sc, sh)


def _norm_mod_bwd(x, ng, sc, sh, dh, dres, name):
    t, d = x.shape
    tm = _pick(t, 256, 16)
    has_res = dres is not None

    def body(*refs):
        if has_res:
            x_ref, ng_ref, sc_ref, sh_ref, dh_ref, dres_ref, dx_ref, dng_ref, dsc_ref, dsh_ref = refs
        else:
            x_ref, ng_ref, sc_ref, sh_ref, dh_ref, dx_ref, dng_ref, dsc_ref, dsh_ref = refs
        _, vjp = jax.vjp(_norm_mod_tile, x_ref[...], ng_ref[...], sc_ref[...], sh_ref[...])
        dx, dng, dsc, dsh = vjp(dh_ref[...].astype(F32))
        if has_res:
            dx = dx + dres_ref[...]
        dx_ref[...] = dx

        @pl.when(pl.program_id(0) == 0)
        def _():
            dng_ref[...] = jnp.zeros_like(dng_ref)
            dsc_ref[...] = jnp.zeros_like(dsc_ref)
            dsh_ref[...] = jnp.zeros_like(dsh_ref)

        dng_ref[...] += dng
        dsc_ref[...] += dsc
        dsh_ref[...] += dsh

    ins = [x, ng, sc, sh, dh] + ([dres] if has_res else [])
    in_specs = [_row_spec(tm, d), _vec_spec(d), _vec_spec(d), _vec_spec(d), _row_spec(tm, d)]
    in_specs += [_row_spec(tm, d)] if has_res else []
    vec = jax.ShapeDtypeStruct((1, d), F32)
    return pl.pallas_call(
        body, name=name, grid=(t // tm,), in_specs=in_specs,
        out_specs=[_row_spec(tm, d), _vec_spec(d), _vec_spec(d), _vec_spec(d)],
        out_shape=[jax.ShapeDtypeStruct((t, d), F32), vec, vec, vec],
        compiler_params=_params(),
    )(*ins)


def _res_mm_fwd(a, w, x, gate, coef, name):
    t, kd = a.shape
    d = w.shape[1]
    tm, tn, tk = _pick(t, 1024, 16), _pick(d, 1024), _pick(kd, 512)

    def epilogue(accs, in_refs, out_refs):
        f = accs[0]
        out_refs[0][...] = in_refs[2][...] + (coef * in_refs[3][...]) * f
        out_refs[1][...] = f.astype(BF16)

    tile = pl.BlockSpec((tm, tn), lambda i, j, k: (i, j))
    return _mm_call(
        name, (t // tm, d // tn, kd // tk),
        [(a, pl.BlockSpec((tm, tk), lambda i, j, k: (i, k))), (w, pl.BlockSpec((tk, tn), lambda i, j, k: (k, j))),
         (x, tile), (gate, pl.BlockSpec((1, tn), lambda i, j, k: (0, j)))],
        [(0, 1, NN, 0)],
        [(jax.ShapeDtypeStruct((t, d), F32), tile), (jax.ShapeDtypeStruct((t, d), BF16), tile)],
        [(tm, tn)], epilogue)


def _gate_bwd(dxo, f, gate, coef, name):
    t, d = dxo.shape
    tm = _pick(t, 512, 16)

    def body(dxo_ref, f_ref, gate_ref, df_ref, dgate_ref):
        dxo_t = dxo_ref[...]
        df_ref[...] = ((coef * gate_ref[...]) * dxo_t).astype(BF16)

        @pl.when(pl.program_id(0) == 0)
        def _():
            dgate_ref[...] = jnp.zeros_like(dgate_ref)

        dgate_ref[...] += coef * jnp.sum(dxo_t * f_ref[...].astype(F32), axis=0, keepdims=True)

    return pl.pallas_call(
        body, name=name, grid=(t // tm,),
        in_specs=[_row_spec(tm, d), _row_spec(tm, d), _vec_spec(d)],
        out_specs=[_row_spec(tm, d), _vec_spec(d)],
        out_shape=[jax.ShapeDtypeStruct((t, d), BF16), jax.ShapeDtypeStruct((1, d), F32)],
        compiler_params=_params(),
    )(dxo, f, gate)


def _ffn_in_fwd(h, w_in, name):
    t, d = h.shape
    n = w_in.shape[2]
    half = N_DEV // 2
    f = half * n
    tm = _pick(t, 512, 16)

    def epilogue(accs, in_refs, out_refs):
        g, u = accs
        out_refs[0][...] = (g * jax.nn.sigmoid(g) * u).astype(BF16)
        out_refs[1][0] = g.astype(BF16)
        out_refs[1][1] = u.astype(BF16)

    return _mm_call(
        name, (half, t // tm, 1),
        [(h, pl.BlockSpec((tm, d), lambda j, i, k: (i, 0))),
         (w_in, pl.BlockSpec((None, d, n), lambda j, i, k: (j, 0, 0))),
         (w_in, pl.BlockSpec((None, d, n), lambda j, i, k: (j + half, 0, 0)))],
        [(0, 1, NN, 0), (0, 2, NN, 1)],
        [(jax.ShapeDtypeStruct((t, f), BF16), pl.BlockSpec((tm, n), lambda j, i, k: (i, j))),
         (jax.ShapeDtypeStruct((2, t, f), BF16), pl.BlockSpec((2, tm, n), lambda j, i, k: (0, i, j)))],
        [(tm, n), (tm, n)], epilogue)


def _ffn_da_bwd(df, w_out2d, gu, name):
    t, d = df.shape
    f = w_out2d.shape[0]
    half = N_DEV // 2
    n = f // half
    tm = _pick(t, 512, 16)

    def epilogue(accs, in_refs, out_refs):
        da = accs[0]
        g = in_refs[2][0].astype(F32)
        u = in_refs[2][1].astype(F32)
        s = jax.nn.sigmoid(g)
        out_refs[0][0] = (da * u * (s * (1.0 + g * (1.0 - s)))).astype(BF16)
        out_refs[0][1] = (da * (g * s)).astype(BF16)

    gu_spec = pl.BlockSpec((2, tm, n), lambda j, i, k: (0, i, j))
    return _mm_call(
        name, (half, t // tm, 1),
        [(df, pl.BlockSpec((tm, d), lambda j, i, k: (i, 0))),
         (w_out2d, pl.BlockSpec((n, d), lambda j, i, k: (j, 0))),
         (gu, gu_spec)],
        [(0, 1, NT, 0)],
        [(jax.ShapeDtypeStruct((2, t, f), BF16), gu_spec)],
        [(tm, n)], epilogue)[0]


def _ffn_dh_bwd(dgu, w_in, name):
    _, t, f = dgu.shape
    d, n = w_in.shape[1], w_in.shape[2]
    half = N_DEV // 2
    tm = _pick(t, 512, 16)

    def epilogue(accs, in_refs, out_refs):
        out_refs[0][...] = accs[0]

    return _mm_call(
        name, (t // tm, 1, half),
        [(dgu, pl.BlockSpec((None, tm, n), lambda i, j, k: (0, i, k))),
         (dgu, pl.BlockSpec((None, tm, n), lambda i, j, k: (1, i, k))),
         (w_in, pl.BlockSpec((None, d, n), lambda i, j, k: (k, 0, 0))),
         (w_in, pl.BlockSpec((None, d, n), lambda i, j, k: (k + half, 0, 0)))],
        [(0, 2, NT, 0), (1, 3, NT, 0)],
        [(jax.ShapeDtypeStruct((t, d), F32), pl.BlockSpec((tm, d), lambda i, j, k: (i, 0)))],
        [(tm, d)], epilogue)[0]


def _ffn_dwin_bwd(h, dgu, name):
    t, d = h.shape
    f = dgu.shape[2]
    half = N_DEV // 2
    n = f // half
    tk = _pick(t, 512, 16)

    def epilogue(accs, in_refs, out_refs):
        out_refs[0][...] = accs[0].astype(BF16)

    return _mm_call(
        name, (N_DEV, 1, t // tk),
        [(h, pl.BlockSpec((tk, d), lambda j, i, k: (k, 0))),
         (dgu, pl.BlockSpec((None, tk, n), lambda j, i, k: (j // half, k, j % half)))],
        [(0, 1, TN, 0)],
        [(jax.ShapeDtypeStruct((N_DEV, d, n), BF16), pl.BlockSpec((None, d, n), lambda j, i, k: (j, 0, 0)))],
        [(d, n)], epilogue)[0]


def _make_ffn_block(tag):
    @jax.custom_vjp
    def ffn_block(x, ng, sh, sc, gate, w_in, w_out):
        return fwd(x, ng, sh, sc, gate, w_in, w_out)[0]

    def fwd(x, ng, sh, sc, gate, w_in, w_out):
        f = w_out.shape[0] * w_out.shape[1]
        w_out2d = w_out.reshape(f, w_out.shape[2])
        h = _norm_mod_fwd(x, ng, sc, sh, tag + "_norm")
        a, gu = _ffn_in_fwd(h, w_in, tag + "_in")
        xo, f1 = _res_mm_fwd(a, w_out2d, x, gate, 0.5, tag + "_out")
        return xo, (x, ng, sh, sc, gate, w_in, w_out, h, a, gu, f1)

    def bwd(res, dxo):
        x, ng, sh, sc, gate, w_in, w_out, h, a, gu, f1 = res
        f = w_out.shape[0] * w_out.shape[1]
        w_out2d = w_out.reshape(f, w_out.shape[2])
        df, dgate = _gate_bwd(dxo, f1, gate, 0.5, tag + "_dgate")
        dgu = _ffn_da_bwd(df, w_out2d, gu, tag + "_da")
        dw_out = _matmul(a, df, "tn", BF16, tag + "_dwout", tm=_pick(f, 1408, 16), tn=2048, tk=512)
        dh = _ffn_dh_bwd(dgu, w_in, tag + "_dh")
        dw_in = _ffn_dwin_bwd(h, dgu, tag + "_dwin")
        dx, dng, dsc, dsh = _norm_mod_bwd(x, ng, sc, sh, dh, dxo, tag + "_dnorm")
        return dx, dng, dsh, dsc, dgate, dw_in, dw_out.reshape(w_out.shape)

    ffn_block.defvjp(fwd, bwd)
    return ffn_block


def _make_norm_proj(tag):
    @jax.custom_vjp
    def norm_proj(x, ng, sh, sc, w):
        return fwd(x, ng, sh, sc, w)[0]

    def fwd(x, ng, sh, sc, w):
        h = _norm_mod_fwd(x, ng, sc, sh, tag + "_norm")
        p = _matmul(h, w, "nn", F32, tag + "_mm", tm=1024, tn=1024, tk=w.shape[0])
        return p, (x, ng, sh, sc, w, h)

    def bwd(res, dp):
        x, ng, sh, sc, w, h = res
        dh = _matmul(dp, w, "nt", F32, tag + "_dh", tm=512, tn=w.shape[0], tk=512)
        dw = _matmul(h, dp, "tn", BF16, tag + "_dw", tm=w.shape[0], tn=1024, tk=512)
        dx, dng, dsc, dsh = _norm_mod_bwd(x, ng, sc, sh, dh, None, tag + "_dnorm")
        return dx, dng, dsh, dsc, dw

    norm_proj.defvjp(fwd, bwd)
    return norm_proj


def _make_res_proj(tag):
    @jax.custom_vjp
    def res_proj(a, w, x, gate):
        return fwd(a, w, x, gate)[0]

    def fwd(a, w, x, gate):
        xo, f = _res_mm_fwd(a, w, x, gate, 1.0, tag + "_mm")
        return xo, (a, w, gate, f)

    def bwd(res, dxo):
        a, w, gate, f = res
        df, dgate = _gate_bwd(dxo, f, gate, 1.0, tag + "_dgate")
        da = _matmul(df, w, "nt", BF16, tag + "_da", tm=1024, tn=1024, tk=512)
        dw = _matmul(a, df, "tn", BF16, tag + "_dw", tm=1024, tn=2048, tk=512)
        return da, dw, dxo, dgate

    res_proj.defvjp(fwd, bwd)
    return res_proj


def _make_small_mm(tag):
    @jax.custom_vjp
    def small_mm(a, w):
        return _matmul(a, w, "nn", F32, tag + "_mm", tm=a.shape[0], tn=768, tk=w.shape[0])

    def fwd(a, w):
        return small_mm(a, w), (a, w)

    def bwd(res, dr):
        a, w = res
        da = _matmul(dr, w, "nt", F32, tag + "_da", tm=a.shape[0], tn=w.shape[0], tk=768)
        dw = _matmul(a, dr, "tn", F32, tag + "_dw", tm=1024, tn=768, tk=a.shape[0])
        return da, dw

    small_mm.defvjp(fwd, bwd)
    return small_mm


def _ret_chunk_terms(lg, c, reverse):
    row = lax.broadcasted_iota(jnp.int32, (c, c), 0).astype(F32)
    col = lax.broadcasted_iota(jnp.int32, (c, c), 1).astype(F32)
    pos = lax.broadcasted_iota(jnp.int32, (c, 1), 0).astype(F32)
    if reverse:
        diff = col - row
        mask = diff > 0.0
        e_exp = float(c) - pos
        f_exp = pos
    else:
        diff = row - col
        mask = diff >= 0.0
        e_exp = pos + 1.0
        f_exp = float(c - 1) - pos
    diffm = jnp.where(mask, diff, 0.0)
    dm = jnp.where(mask, jnp.exp(lg * diffm), 0.0)
    return diffm, dm, e_exp, jnp.exp(lg * e_exp), f_exp, jnp.exp(lg * f_exp)


def _lane0(val):
    lane = lax.broadcasted_iota(jnp.int32, (1, LANE), 1)
    return jnp.where(lane == 0, val, 0.0)


def _make_ret_dir(tag, reverse):
    def heads_spec(nc, width, flip):
        if flip:
            return pl.BlockSpec((None, RET_CHUNK, width), lambda h, t: (h, nc - 1 - t, 0))
        return pl.BlockSpec((None, RET_CHUNK, width), lambda h, t: (h, t, 0))

    def state_spec(nc, flip):
        if flip:
            return pl.BlockSpec((None, None, RET_DK, RET_DV), lambda h, t: (h, nc - 1 - t, 0, 0))
        return pl.BlockSpec((None, None, RET_DK, RET_DV), lambda h, t: (h, t, 0, 0))

    lg_spec = pl.BlockSpec((None, 1, LANE), lambda h, t: (h, 0, 0))

    def fwd_call(q, k, v, lgb):
        hh, ll, _ = q.shape
        c = RET_CHUNK
        nc = ll // c

        def body(q_ref, k_ref, v_ref, lg_ref, y_ref, sall_ref, s_scr):
            @pl.when(pl.program_id(1) == 0)
            def _():
                s_scr[...] = jnp.zeros_like(s_scr)

            lg = lg_ref[...][:, :1]
            _, dm, _, xi, _, zeta = _ret_chunk_terms(lg, c, reverse)
            q_t, k_t, v_t = q_ref[...], k_ref[...], v_ref[...]
            s = s_scr[...]
            p = _dot(q_t, k_t, NT) * dm
            y_ref[...] = _dot(p, v_t, NN) + _dot(q_t * xi, s, NN)
            sall_ref[...] = s
            s_scr[...] = jnp.exp(lg * float(c)) * s + _dot(k_t * zeta, v_t, TN)

        return pl.pallas_call(
            body, name=tag + "_fwd", grid=(hh, nc),
            in_specs=[heads_spec(nc, RET_DK, reverse), heads_spec(nc, RET_DK, reverse),
                      heads_spec(nc, RET_DV, reverse), lg_spec],
            out_specs=[heads_spec(nc, RET_DV, reverse), state_spec(nc, reverse)],
            out_shape=[jax.ShapeDtypeStruct((hh, ll, RET_DV), F32),
                       jax.ShapeDtypeStruct((hh, nc, RET_DK, RET_DV), F32)],
            scratch_shapes=[pltpu.VMEM((RET_DK, RET_DV), F32)],
            compiler_params=_params(),
        )(q, k, v, lgb)

    def bwd_call(q, k, v, lgb, sall, dy):
        hh, ll, _ = q.shape
        c = RET_CHUNK
        nc = ll // c
        flip = not reverse

        def body(q_ref, k_ref, v_ref, lg_ref, sall_ref, dy_ref, dq_ref, dk_ref, dv_ref, dlg_ref, ds_scr):
            @pl.when(pl.program_id(1) == 0)
            def _():
                ds_scr[...] = jnp.zeros_like(ds_scr)
                dlg_ref[...] = jnp.zeros_like(dlg_ref)

            lg = lg_ref[...][:, :1]
            diffm, dm, e_exp, xi, f_exp, zeta = _ret_chunk_terms(lg, c, reverse)
            q_t, k_t, v_t, dy_t = q_ref[...], k_ref[...], v_ref[...], dy_ref[...]
            s = sall_ref[...]
            dsn = ds_scr[...]
            a = _dot(q_t, k_t, NT)
            da = _dot(dy_t, v_t, NT) * dm
            g = _dot(dy_t, s, NT)
            hm = _dot(v_t, dsn, NT)
            dq_ref[...] = _dot(da, k_t, NN) + xi * g
            dk_ref[...] = _dot(da, q_t, TN) + zeta * hm
            dv_ref[...] = _dot(a * dm, dy_t, TN) + _dot(k_t * zeta, dsn, NN)
            gc = jnp.exp(lg * float(c))
            ds_scr[...] = gc * dsn + _dot(q_t * xi, dy_t, TN)

            def total(m):
                return jnp.sum(jnp.sum(m, axis=1, keepdims=True), axis=0, keepdims=True)

            dl = (total(da * a * diffm) + total(e_exp * xi * q_t * g)
                  + float(c) * gc * total(s * dsn) + total(f_exp * zeta * k_t * hm))
            dlg_ref[...] += _lane0(dl)

        return pl.pallas_call(
            body, name=tag + "_bwd", grid=(hh, nc),
            in_specs=[heads_spec(nc, RET_DK, flip), heads_spec(nc, RET_DK, flip), heads_spec(nc, RET_DV, flip),
                      lg_spec, state_spec(nc, flip), heads_spec(nc, RET_DV, flip)],
            out_specs=[heads_spec(nc, RET_DK, flip), heads_spec(nc, RET_DK, flip), heads_spec(nc, RET_DV, flip),
                       lg_spec],
            out_shape=[jax.ShapeDtypeStruct((hh, ll, RET_DK), F32), jax.ShapeDtypeStruct((hh, ll, RET_DK), F32),
                       jax.ShapeDtypeStruct((hh, ll, RET_DV), F32), jax.ShapeDtypeStruct((hh, 1, LANE), F32)],
            scratch_shapes=[pltpu.VMEM((RET_DK, RET_DV), F32)],
            compiler_params=_params(),
        )(q, k, v, lgb, sall, dy)

    @jax.custom_vjp
    def ret_dir(q, k, v, lgb):
        return fwd_call(q, k, v, lgb)[0]

    def fwd(q, k, v, lgb):
        y, sall = fwd_call(q, k, v, lgb)
        return y, (q, k, v, lgb, sall)

    def bwd(res, dy):
        q, k, v, lgb, sall = res
        dq, dk, dv, dlg = bwd_call(q, k, v, lgb, sall, dy)
        return dq, dk, dv, dlg

    ret_dir.defvjp(fwd, bwd)
    return ret_dir


def _ret_out_tile(y, g):
    mu = jnp.mean(y, axis=-1, keepdims=True)
    var = jnp.mean(jnp.square(y - mu), axis=-1, keepdims=True)
    return (g * jax.nn.sigmoid(g)) * ((y - mu) * lax.rsqrt(var + GN_EPS))


def _make_ret_out(tag):
    def specs(tm):
        y_spec = pl.BlockSpec((None, tm, RET_DV), lambda h, i: (h, i, 0))
        g_spec = pl.BlockSpec((tm, RET_DV), lambda h, i: (i, h))
        return y_spec, g_spec

    def fwd_call(y, g):
        hh, n, _ = y.shape
        tm = _pick(n, 1024, 16)
        y_spec, g_spec = specs(tm)

        def body(y_ref, g_ref, o_ref):
            o_ref[...] = _ret_out_tile(y_ref[...], g_ref[...]).astype(BF16)

        return pl.pallas_call(
            body, name=tag + "_fwd", grid=(hh, n // tm), in_specs=[y_spec, g_spec], out_specs=g_spec,
            out_shape=jax.ShapeDtypeStruct((n, hh * RET_DV), BF16), compiler_params=_params(),
        )(y, g)

    def bwd_call(y, g, do):
        hh, n, _ = y.shape
        tm = _pick(n, 1024, 16)
        y_spec, g_spec = specs(tm)

        def body(y_ref, g_ref, do_ref, dy_ref, dg_ref):
            _, vjp = jax.vjp(_ret_out_tile, y_ref[...], g_ref[...])
            dy, dg = vjp(do_ref[...].astype(F32))
            dy_ref[...] = dy
            dg_ref[...] = dg

        return pl.pallas_call(
            body, name=tag + "_bwd", grid=(hh, n // tm), in_specs=[y_spec, g_spec, g_spec],
            out_specs=[y_spec, g_spec],
            out_shape=[jax.ShapeDtypeStruct(y.shape, F32), jax.ShapeDtypeStruct(g.shape, F32)],
            compiler_params=_params(),
        )(y, g, do)

    @jax.custom_vjp
    def ret_out(y, g):
        return fwd_call(y, g)

    def fwd(y, g):
        return fwd_call(y, g), (y, g)

    def bwd(res, do):
        y, g = res
        return tuple(bwd_call(y, g, do))

    ret_out.defvjp(fwd, bwd)
    return ret_out


def _make_attention(tag):
    neg_big = -1e30
    log2e = 1.4426950408889634
    sub = 256

    def fwd_call(q, k, vt):
        hh, n, dq = q.shape
        dv, ll = vt.shape[1], vt.shape[2]
        tq, tk = _pick(n, 1024), _pick(ll, 768)
        sb = sub if tk % sub == 0 else tk
        scale = dq ** -0.5
        c2 = scale * log2e
        k_steps = ll // tk

        def body(q_ref, k_ref, vt_ref, o_ref, lse_ref, m_scr, l_scr, acc_scr):
            j = pl.program_id(2)

            @pl.when(j == 0)
            def _():
                m_scr[...] = jnp.full_like(m_scr, neg_big)
                l_scr[...] = jnp.zeros_like(l_scr)
                acc_scr[...] = jnp.zeros_like(acc_scr)

            q_t = q_ref[...]
            for kk in range(tk // sb):
                s_t = _dot(k_ref[kk * sb:(kk + 1) * sb, :], q_t, NT)
                m_prev = m_scr[...]
                m_new = jnp.maximum(m_prev, jnp.max(s_t, axis=0, keepdims=True))
                p_t = jnp.exp2(s_t * c2 - m_new * c2)
                alpha = jnp.exp2((m_prev - m_new) * c2)
                l_scr[...] = alpha * l_scr[...] + jnp.sum(p_t, axis=0, keepdims=True)
                acc_scr[...] = alpha * acc_scr[...] + _dot(vt_ref[:, kk * sb:(kk + 1) * sb], p_t, NN)
                m_scr[...] = m_new

            @pl.when(j == k_steps - 1)
            def _():
                o_ref[...] = jnp.transpose(acc_scr[...] / l_scr[...]).astype(BF16)
                lse_ref[...] = m_scr[...] * scale + jnp.log(l_scr[...])

        return pl.pallas_call(
            body, name=tag + "_fwd", grid=(hh, n // tq, k_steps),
            in_specs=[pl.BlockSpec((None, tq, dq), lambda h, i, j: (h, i, 0)),
                      pl.BlockSpec((None, tk, dq), lambda h, i, j: (h, j, 0)),
                      pl.BlockSpec((None, dv, tk), lambda h, i, j: (h, 0, j))],
            out_specs=[pl.BlockSpec((tq, dv), lambda h, i, j: (i, h)),
                       pl.BlockSpec((None, 1, tq), lambda h, i, j: (h, 0, i))],
            out_shape=[jax.ShapeDtypeStruct((n, hh * dv), BF16), jax.ShapeDtypeStruct((hh, 1, n), F32)],
            scratch_shapes=[pltpu.VMEM((1, tq), F32), pltpu.VMEM((1, tq), F32), pltpu.VMEM((dv, tq), F32)],
            compiler_params=_params(),
        )(q, k, vt)

    def delta_call(o, do, hh):
        n = o.shape[0]
        dv = o.shape[1] // hh
        tq = _pick(n, 1024)

        def body(o_ref, do_ref, d_ref):
            prod_t = jnp.transpose(o_ref[...].astype(F32) * do_ref[...].astype(F32))
            d_ref[...] = jnp.sum(prod_t, axis=0, keepdims=True)

        spec = pl.BlockSpec((tq, dv), lambda h, i: (i, h))
        return pl.pallas_call(
            body, name=tag + "_delta", grid=(hh, n // tq), in_specs=[spec, spec],
            out_specs=pl.BlockSpec((None, 1, tq), lambda h, i: (h, 0, i)),
            out_shape=jax.ShapeDtypeStruct((hh, 1, n), F32), compiler_params=_params(),
        )(o, do)

    def bwd_call(q, k, kt, v, do, lse, delta):
        hh, n, dq = q.shape
        ll, dv = k.shape[1], v.shape[2]
        tq, tk = _pick(n, 512), _pick(ll, 768)
        sb = sub if tk % sub == 0 else tk
        scale = dq ** -0.5
        c2 = scale * log2e
        q_steps = n // tq

        def body(q_ref, k_ref, kt_ref, v_ref, do_ref, lse_ref, d_ref, dqt_ref, dk_ref, dv_ref, dk_scr, dv_scr):
            j = pl.program_id(1)
            i = pl.program_id(2)

            @pl.when(i == 0)
            def _():
                dk_scr[...] = jnp.zeros_like(dk_scr)
                dv_scr[...] = jnp.zeros_like(dv_scr)

            q_t, do_t = q_ref[...], do_ref[...]
            lse2 = lse_ref[...] * log2e
            delta_t = d_ref[...]
            dq_part = None
            for kk in range(tk // sb):
                rows = slice(kk * sb, (kk + 1) * sb)
                s_t = _dot(k_ref[rows, :], q_t, NT)
                p_t = jnp.exp2(s_t * c2 - lse2)
                ds_t = p_t * (_dot(v_ref[rows, :], do_t, NT) - delta_t)
                dv_scr[rows, :] += _dot(p_t, do_t, NN)
                dk_scr[rows, :] += _dot(ds_t, q_t, NN)
                part = _dot(kt_ref[:, rows], ds_t, NN)
                dq_part = part if dq_part is None else dq_part + part
            cols = pl.ds(pl.multiple_of(i * tq, tq), tq)

            @pl.when(j == 0)
            def _():
                dqt_ref[:, cols] = dq_part

            @pl.when(j > 0)
            def _():
                dqt_ref[:, cols] += dq_part

            @pl.when(i == q_steps - 1)
            def _():
                dk_ref[...] = (dk_scr[...] * scale).astype(BF16)
                dv_ref[...] = dv_scr[...].astype(BF16)

        return pl.pallas_call(
            body, name=tag + "_bwd", grid=(hh, ll // tk, q_steps),
            in_specs=[pl.BlockSpec((None, tq, dq), lambda h, j, i: (h, i, 0)),
                      pl.BlockSpec((None, tk, dq), lambda h, j, i: (h, j, 0)),
                      pl.BlockSpec((None, dq, tk), lambda h, j, i: (h, 0, j)),
                      pl.BlockSpec((None, tk, dv), lambda h, j, i: (h, j, 0)),
                      pl.BlockSpec((tq, dv), lambda h, j, i: (i, h)),
                      pl.BlockSpec((None, 1, tq), lambda h, j, i: (h, 0, i)),
                      pl.BlockSpec((None, 1, tq), lambda h, j, i: (h, 0, i))],
            out_specs=[pl.BlockSpec((None, dq, n), lambda h, j, i: (h, 0, 0)),
                       pl.BlockSpec((None, tk, dq), lambda h, j, i: (h, j, 0)),
                       pl.BlockSpec((None, tk, dv), lambda h, j, i: (h, j, 0))],
            out_shape=[jax.ShapeDtypeStruct((hh, dq, n), F32), jax.ShapeDtypeStruct((hh, ll, dq), BF16),
                       jax.ShapeDtypeStruct((hh, ll, dv), BF16)],
            scratch_shapes=[pltpu.VMEM((tk, dq), F32), pltpu.VMEM((tk, dv), F32)],
            compiler_params=_params(),
        )(q, k, kt, v, do, lse, delta)

    @jax.custom_vjp
    def attention(q, k, v):
        return fwd_call(q, k, jnp.swapaxes(v, 1, 2))[0]

    def fwd(q, k, v):
        o, lse = fwd_call(q, k, jnp.swapaxes(v, 1, 2))
        return o, (q, k, v, o, lse)

    def bwd(res, do):
        q, k, v, o, lse = res
        delta = delta_call(o, do, q.shape[0])
        dqt, dk, dv = bwd_call(q, k, jnp.swapaxes(k, 1, 2), v, do, lse, delta)
        return (jnp.swapaxes(dqt, 1, 2) * (q.shape[2] ** -0.5)).astype(BF16), dk, dv

    attention.defvjp(fwd, bwd)
    return attention


def _loss_tile(x, g, tgt):
    r = lax.rsqrt(jnp.mean(x * x, axis=-1, keepdims=True) + RMS_EPS)
    err = x * r * g - tgt
    per_tok = jnp.mean(err * err, axis=-1, keepdims=True)
    return 0.5 * jnp.sum(per_tok, axis=0, keepdims=True)


def _make_final_loss(tag):
    def fwd_call(x, g, tgt):
        t, d = x.shape
        tm = _pick(t, 512, 16)

        def body(x_ref, g_ref, t_ref, l_ref):
            l_ref[...] = jnp.broadcast_to(_loss_tile(x_ref[...], g_ref[...], t_ref[...]), (1, LANE))

        parts = pl.pallas_call(
            body, name=tag + "_fwd", grid=(t // tm,),
            in_specs=[_row_spec(tm, d), _vec_spec(d), _row_spec(tm, d)],
            out_specs=pl.BlockSpec((None, 1, LANE), lambda i: (i, 0, 0)),
            out_shape=jax.ShapeDtypeStruct((t // tm, 1, LANE), F32), compiler_params=_params(),
        )(x, g, tgt)
        return jnp.sum(parts[:, 0, 0])

    def bwd_call(x, g, tgt, dl):
        t, d = x.shape
        tm = _pick(t, 256, 16)

        def body(x_ref, g_ref, t_ref, dl_ref, dx_ref, dg_ref):
            _, vjp = jax.vjp(_loss_tile, x_ref[...], g_ref[...], t_ref[...])
            dx, dg, _ = vjp(dl_ref[...])
            dx_ref[...] = dx

            @pl.when(pl.program_id(0) == 0)
            def _():
                dg_ref[...] = jnp.zeros_like(dg_ref)

            dg_ref[...] += dg

        return pl.pallas_call(
            body, name=tag + "_bwd", grid=(t // tm,),
            in_specs=[_row_spec(tm, d), _vec_spec(d), _row_spec(tm, d), pl.BlockSpec((1, 1), lambda i: (0, 0))],
            out_specs=[_row_spec(tm, d), _vec_spec(d)],
            out_shape=[jax.ShapeDtypeStruct((t, d), F32), jax.ShapeDtypeStruct((1, d), F32)],
            compiler_params=_params(),
        )(x, g, tgt, dl)

    @jax.custom_vjp
    def final_loss(x, g, tgt):
        return fwd_call(x, g, tgt)

    def fwd(x, g, tgt):
        return fwd_call(x, g, tgt), (x, g, tgt)

    def bwd(res, dl):
        x, g, tgt = res
        dx, dg = bwd_call(x, g, tgt, dl.reshape(1, 1).astype(F32))
        return dx, dg, jnp.zeros_like(tgt)

    final_loss.defvjp(fwd, bwd)
    return final_loss


def _exchange(arrays, gather, name):
    n = len(arrays)

    def body(*refs):
        ins, outs = refs[:n], refs[n:2 * n]
        send_sems, recv_sems, local_sems = refs[2 * n:]
        me = 4 * lax.axis_index("x") + 2 * lax.axis_index("y") + lax.axis_index("c")

        def remote(a, d, wait_side=False):
            peer = (me + d) % N_DEV
            origin = (me + N_DEV - d) % N_DEV
            src = ins[a] if gather else ins[a].at[peer]
            dst = outs[a].at[origin if wait_side else me]
            return pltpu.make_async_remote_copy(
                src_ref=src, dst_ref=dst, send_sem=send_sems.at[a, d - 1], recv_sem=recv_sems.at[a, d - 1],
                device_id=(peer // 4, (peer // 2) % 2, peer % 2), device_id_type=pl.DeviceIdType.MESH)

        def local(a):
            src = ins[a] if gather else ins[a].at[me]
            return pltpu.make_async_copy(src, outs[a].at[me], local_sems.at[a])

        for a in range(n):
            for d in range(1, N_DEV):
                remote(a, d).start()
            local(a).start()
        for a in range(n):
            local(a).wait()
            for d in range(1, N_DEV):
                remote(a, d, wait_side=True).wait_recv()
                remote(a, d).wait_send()

    out_shape = []
    for arr in arrays:
        shape = (N_DEV,) + arr.shape if gather else arr.shape
        out_shape.append(jax.ShapeDtypeStruct(shape, arr.dtype))
    any_spec = pl.BlockSpec(memory_space=pl.ANY)
    return pl.pallas_call(
        body, name=name, in_specs=[any_spec] * n, out_specs=[any_spec] * n, out_shape=out_shape,
        scratch_shapes=[pltpu.SemaphoreType.DMA((n, N_DEV - 1)), pltpu.SemaphoreType.DMA((n, N_DEV - 1)),
                        pltpu.SemaphoreType.DMA((n,))],
        compiler_params=pltpu.CompilerParams(has_side_effects=True),
    )(*arrays)


def _coords():
    return lax.axis_index("x"), lax.axis_index("y"), lax.axis_index("c")


def _other_chips(x, y):
    return [(1 - x, y), (x, 1 - y), (1 - x, 1 - y)]


def _gather_two_level(arrays, name):
    n = len(arrays)

    def body(*refs):
        ins, outs = refs[:n], refs[n:2 * n]
        send_sems, recv_sems, local_sems = refs[2 * n:]
        x, y, c = _coords()
        me, sib = (x, y, c), (x, y, 1 - c)
        chips = _other_chips(x, y)

        def copy(a, k, block, to, from_input=False):
            slot = 4 * block[0] + 2 * block[1] + block[2]
            return pltpu.make_async_remote_copy(
                src_ref=ins[a] if from_input else outs[a].at[slot], dst_ref=outs[a].at[slot],
                send_sem=send_sems.at[a, k], recv_sem=recv_sems.at[a, k],
                device_id=to, device_id_type=pl.DeviceIdType.MESH)

        def local(a):
            return pltpu.make_async_copy(ins[a], outs[a].at[4 * x + 2 * y + c], local_sems.at[a])

        for a in range(n):
            for j, chip in enumerate(chips):
                copy(a, 1 + j, me, (*chip, c), True).start()
            copy(a, 0, me, sib, True).start()
            local(a).start()
        for a in range(n):
            for j, chip in enumerate(chips):
                copy(a, 1 + j, (*chip, c), me).wait_recv()
                copy(a, 4 + j, (*chip, c), sib).start()
        for a in range(n):
            copy(a, 0, sib, me).wait_recv()
            for j, chip in enumerate(chips):
                copy(a, 4 + j, (*chip, 1 - c), me).wait_recv()
            for k in range(N_DEV - 1):
                copy(a, k, me, sib, True).wait_send()
            local(a).wait()

    any_spec = pl.BlockSpec(memory_space=pl.ANY)
    return pl.pallas_call(
        body, name=name, in_specs=[any_spec] * n, out_specs=[any_spec] * n,
        out_shape=[jax.ShapeDtypeStruct((N_DEV,) + arr.shape, arr.dtype) for arr in arrays],
        scratch_shapes=[pltpu.SemaphoreType.DMA((n, N_DEV - 1)), pltpu.SemaphoreType.DMA((n, N_DEV - 1)),
                        pltpu.SemaphoreType.DMA((n,))],
        compiler_params=pltpu.CompilerParams(has_side_effects=True),
    )(*arrays)


def _swap_sibling(arrays, name):
    n = len(arrays)
    n_chip = N_DEV // 2

    def body(*refs):
        ins, outs = refs[:n], refs[n:2 * n]
        send_sems, recv_sems = refs[2 * n:]
        x, y, c = _coords()

        def copy(a, q):
            return pltpu.make_async_remote_copy(
                src_ref=ins[a].at[2 * q + (1 - c)], dst_ref=outs[a].at[q],
                send_sem=send_sems.at[a, q], recv_sem=recv_sems.at[a, q],
                device_id=(x, y, 1 - c), device_id_type=pl.DeviceIdType.MESH)

        for a in range(n):
            for q in range(n_chip):
                copy(a, q).start()
        for a in range(n):
            for q in range(n_chip):
                copy(a, q).wait_recv()
                copy(a, q).wait_send()

    any_spec = pl.BlockSpec(memory_space=pl.ANY)
    return pl.pallas_call(
        body, name=name, in_specs=[any_spec] * n, out_specs=[any_spec] * n,
        out_shape=[jax.ShapeDtypeStruct((n_chip,) + arr.shape[1:], arr.dtype) for arr in arrays],
        scratch_shapes=[pltpu.SemaphoreType.DMA((n, n_chip)), pltpu.SemaphoreType.DMA((n, n_chip))],
        compiler_params=pltpu.CompilerParams(has_side_effects=True),
    )(*arrays)


def _scatter_chips(arrays, name):
    n = len(arrays)
    n_chip = N_DEV // 2

    def body(*refs):
        ins, outs = refs[:n], refs[n:2 * n]
        send_sems, recv_sems, local_sems = refs[2 * n:]
        x, y, c = _coords()
        q_me = 2 * x + y
        chips = _other_chips(x, y)

        def copy(a, j, wait_side=False):
            q_peer = 2 * chips[j][0] + chips[j][1]
            return pltpu.make_async_remote_copy(
                src_ref=ins[a].at[q_peer], dst_ref=outs[a].at[q_peer if wait_side else q_me],
                send_sem=send_sems.at[a, j], recv_sem=recv_sems.at[a, j],
                device_id=(*chips[j], c), device_id_type=pl.DeviceIdType.MESH)

        def local(a):
            return pltpu.make_async_copy(ins[a].at[q_me], outs[a].at[q_me], local_sems.at[a])

        for a in range(n):
            for j in range(n_chip - 1):
                copy(a, j).start()
            local(a).start()
        for a in range(n):
            local(a).wait()
            for j in range(n_chip - 1):
                copy(a, j, wait_side=True).wait_recv()
                copy(a, j).wait_send()

    any_spec = pl.BlockSpec(memory_space=pl.ANY)
    return pl.pallas_call(
        body, name=name, in_specs=[any_spec] * n, out_specs=[any_spec] * n,
        out_shape=[jax.ShapeDtypeStruct(arr.shape, arr.dtype) for arr in arrays],
        scratch_shapes=[pltpu.SemaphoreType.DMA((n, n_chip - 1)), pltpu.SemaphoreType.DMA((n, n_chip - 1)),
                        pltpu.SemaphoreType.DMA((n,))],
        compiler_params=pltpu.CompilerParams(has_side_effects=True),
    )(*arrays)


def _pair_add(full, theirs, core, name):
    n_chip, r, cn = theirs.shape
    tr = _pick(r, max(16, (2 * 1024 * 1024) // (4 * cn) // 16 * 16), 16)

    def body(core_ref, mine_ref, theirs_ref, o_ref):
        o_ref[...] = (mine_ref[...].astype(F32) + theirs_ref[...].astype(F32)).astype(BF16)

    tile = pl.BlockSpec((None, tr, cn), lambda q, i, core_ref: (q, i, 0))
    return pl.pallas_call(
        body, name=name,
        grid_spec=pltpu.PrefetchScalarGridSpec(
            num_scalar_prefetch=1, grid=(n_chip, r // tr),
            in_specs=[pl.BlockSpec((None, tr, cn), lambda q, i, core_ref: (2 * q + core_ref[0], i, 0)), tile],
            out_specs=tile),
        out_shape=jax.ShapeDtypeStruct(theirs.shape, BF16), compiler_params=_params(),
    )(core, full, theirs)


def _make_gather_op(tag):
    @jax.custom_vjp
    def gather_op(xl):
        return _exchange([xl], True, tag + "_gather")[0]

    def fwd(xl):
        return gather_op(xl), None

    def bwd(_, g):
        return (jnp.sum(_exchange([g], False, tag + "_scatter")[0], axis=0),)

    gather_op.defvjp(fwd, bwd)
    return gather_op


def _adamw(gstack, w, m, v, name):
    s, r, cn = gstack.shape
    tr = _pick(r, max(8, (2 * 1024 * 1024) // (4 * cn) // 8 * 8), 8)
    c1 = 1.0 - ADAM_B1 ** ADAM_STEP
    c2 = 1.0 - ADAM_B2 ** ADAM_STEP

    def body(g_ref, w_ref, m_ref, v_ref, go_ref, d_ref, mo_ref, vo_ref):
        g = g_ref[0].astype(F32)
        for q in range(1, s):
            g = g + g_ref[q].astype(F32)
        m_new = ADAM_B1 * m_ref[...] + (1.0 - ADAM_B1) * g
        v_new = ADAM_B2 * v_ref[...] + (1.0 - ADAM_B2) * (g * g)
        go_ref[...] = g
        mo_ref[...] = m_new
        vo_ref[...] = v_new
        d_ref[...] = -ADAM_LR * ((m_new / c1) / (jnp.sqrt(v_new / c2) + ADAM_EPS) + ADAM_WD * w_ref[...])

    tile = pl.BlockSpec((tr, cn), lambda i: (i, 0))
    out = jax.ShapeDtypeStruct((r, cn), F32)
    return pl.pallas_call(
        body, name=name, grid=(r // tr,),
        in_specs=[pl.BlockSpec((s, tr, cn), lambda i: (0, i, 0)), tile, tile, tile],
        out_specs=[tile, tile, tile, tile], out_shape=[out, out, out, out],
        compiler_params=_params(),
    )(gstack, w, m, v)


def _rope_tables(pos, dim, base):
    inv = base ** (-jnp.arange(0, dim, 2, dtype=F32) / dim)
    ang = pos.astype(F32)[:, None] * inv[None, :]
    return jnp.cos(ang)[:, None, :], jnp.sin(ang)[:, None, :]


def _rotate(x, cos, sin):
    x1, x2 = jnp.split(x, 2, axis=-1)
    return jnp.concatenate([x1 * cos - x2 * sin, x2 * cos + x1 * sin], axis=-1)


def _axial_rope(x, row_tab, col_tab):
    xr, xc = jnp.split(x, 2, axis=-1)
    return jnp.concatenate([_rotate(xr, *row_tab), _rotate(xc, *col_tab)], axis=-1)


def _heads(t, h):
    return jnp.swapaxes(t.reshape(t.shape[0], h, t.shape[1] // h), 0, 1)


def _cols_from_stack(w):
    return jnp.swapaxes(w, 0, 1).reshape(w.shape[1], N_DEV * w.shape[2])


def _local_loss(p, ctx, silu_c_all, tgt, me):
    x = p["x"]
    n_lat, d = x.shape
    n_ctx = ctx.shape[0]
    n_a = p["ada_w"].shape[1]

    a_in = jnp.concatenate([silu_c_all, jax.nn.silu(p["c_ctx"])[None, :], jnp.zeros((7, d), F32)], axis=0)
    b_loc = lax.dynamic_slice(p["ada_b"], (0, me * n_a), (1, n_a))
    r_loc = _make_small_mm("ada")(a_in, p["ada_w"]) + b_loc
    r_full = _make_gather_op("ada")(r_loc)
    m_lat = lax.dynamic_index_in_dim(r_full, me, axis=1, keepdims=False).reshape(N_MOD, 1, d)
    m_ctx = r_full[:, N_DEV, :].reshape(N_MOD, 1, d)

    x1 = _make_ffn_block("ffn1")(x, p["norm1_g"], m_lat[0], m_lat[1], m_lat[2], p["ffn1_w_in"], p["ffn1_w_out"])
    c1 = _make_ffn_block("ffn1c")(ctx, p["norm1_g"], m_ctx[0], m_ctx[1], m_ctx[2], p["ffn1_w_in"], p["ffn1_w_out"])

    w_mix = jnp.pad(_cols_from_stack(p["mix_w_in"]), ((0, 0), (0, MIX_IN_PAD - MIX_IN)))
    proj = _make_norm_proj("mix")(x1, p["norm2_g"], m_lat[3], m_lat[4], w_mix)
    proj_c = _make_norm_proj("mixc")(c1, p["norm2_g"], m_ctx[3], m_ctx[4], w_mix)
    idx = [0]
    for s in SPLITS:
        idx.append(idx[-1] + s)
    rq, rk, rv, rg, cq, ckv, kr = [proj[:, idx[i]:idx[i + 1]] for i in range(7)]
    crk, crv, cckv, ckr = [proj_c[:, idx[i]:idx[i + 1]] for i in (1, 2, 5, 6)]

    zq = jnp.zeros((1, MLA_Q_RANK), F32)
    zkv = jnp.zeros((1, MLA_KV_RANK), F32)
    w_uq = _cols_from_stack(p["mla_w_uq"])
    w_ukv = _cols_from_stack(p["mla_w_ukv"])
    q = _make_norm_proj("uq")(cq, p["mla_q_norm_g"], zq, zq, w_uq)
    kv = _make_norm_proj("ukv")(ckv, p["mla_kv_norm_g"], zkv, zkv, w_ukv)
    kv_c = _make_norm_proj("ukvc")(cckv, p["mla_kv_norm_g"], zkv, zkv, w_ukv)

    lg_f = jax.nn.log_sigmoid(p["ret_decay_fwd"][0])
    lg_b = jax.nn.log_sigmoid(p["ret_decay_bwd"][0])
    ret_tab = _rope_tables(jnp.arange(n_lat), RET_DK, RET_ROPE_BASE)
    rq_h = jnp.swapaxes(_rotate(rq.reshape(n_lat, RET_HEADS, RET_DK), *ret_tab), 0, 1)
    rk_h = jnp.swapaxes(_rotate((rk * (RET_DK ** -0.5)).reshape(n_lat, RET_HEADS, RET_DK), *ret_tab), 0, 1)
    rv_h = _heads(rv, RET_HEADS)
    crk_h = _heads(crk * (RET_DK ** -0.5), RET_HEADS)
    crv_h = _heads(crv, RET_HEADS)
    zero_q = jnp.zeros((RET_HEADS, n_ctx, RET_DK), F32)

    def lanes(lg):
        return jnp.broadcast_to(lg[:, None, None], (RET_HEADS, 1, LANE))

    y_f = _make_ret_dir("retf", False)(
        jnp.concatenate([zero_q, rq_h], axis=1), jnp.concatenate([crk_h, rk_h], axis=1),
        jnp.concatenate([crv_h, rv_h], axis=1), lanes(lg_f))[:, n_ctx:]
    y_b = _make_ret_dir("retb", True)(
        jnp.concatenate([rq_h, zero_q], axis=1), jnp.concatenate([rk_h, crk_h], axis=1),
        jnp.concatenate([rv_h, crv_h], axis=1), lanes(lg_b))[:, :n_lat]
    ret_o = _make_ret_out("reto")(y_f + y_b, rg)

    n_rows = n_lat // GRID_W
    row_tab = _rope_tables(jnp.repeat(jnp.arange(n_rows), GRID_W), MLA_ROPE // 2, AXIAL_BASE)
    col_tab = _rope_tables(jnp.tile(jnp.arange(GRID_W), n_rows), MLA_ROPE // 2, AXIAL_BASE)
    q3 = q.reshape(n_lat, MLA_HEADS, MLA_NOPE + MLA_ROPE)
    q_all = jnp.concatenate([q3[..., :MLA_NOPE], _axial_rope(q3[..., MLA_NOPE:], row_tab, col_tab)], axis=-1)
    kv3 = kv.reshape(n_lat, MLA_HEADS, MLA_NOPE + MLA_V)
    kvc3 = kv_c.reshape(n_ctx, MLA_HEADS, MLA_NOPE + MLA_V)
    kr_rot = _axial_rope(kr[:, None, :], row_tab, col_tab)
    k_lat = jnp.concatenate([kv3[..., :MLA_NOPE], jnp.broadcast_to(kr_rot, (n_lat, MLA_HEADS, MLA_ROPE))], axis=-1)
    k_ctx = jnp.concatenate(
        [kvc3[..., :MLA_NOPE], jnp.broadcast_to(ckr[:, None, :], (n_ctx, MLA_HEADS, MLA_ROPE))], axis=-1)
    k_all = jnp.concatenate([k_lat, k_ctx], axis=0)
    v_all = jnp.concatenate([kv3[..., MLA_NOPE:], kvc3[..., MLA_NOPE:]], axis=0)
    mla_o = _make_attention("mla")(
        jnp.swapaxes(q_all, 0, 1).astype(BF16), jnp.swapaxes(k_all, 0, 1).astype(BF16),
        jnp.swapaxes(v_all, 0, 1).astype(BF16))

    w_mo = p["mix_w_out"].reshape(-1, d)
    x2 = _make_res_proj("mixo")(jnp.concatenate([ret_o, mla_o], axis=-1), w_mo, x1, m_lat[5])
    x3 = _make_ffn_block("ffn2")(x2, p["norm3_g"], m_lat[6], m_lat[7], m_lat[8], p["ffn2_w_in"], p["ffn2_w_out"])
    return _make_final_loss("loss")(x3, p["final_norm_g"], tgt)


BIG = ("ffn1_w_in", "ffn1_w_out", "mix_w_in", "mla_w_uq", "mla_w_ukv", "mix_w_out", "ffn2_w_in", "ffn2_w_out")
SMALL = ("c_ctx", "ada_b", "norm1_g", "norm2_g", "ret_decay_fwd", "ret_decay_bwd", "mla_q_norm_g",
         "mla_kv_norm_g", "norm3_g", "final_norm_g")
WEIGHTS = ("c_ctx", "ada_w", "ada_b", "norm1_g", "ffn1_w_in", "ffn1_w_out", "norm2_g", "mix_w_in", "ret_decay_fwd",
           "ret_decay_bwd", "mla_q_norm_g", "mla_w_uq", "mla_kv_norm_g", "mla_w_ukv", "mix_w_out", "norm3_g",
           "ffn2_w_in", "ffn2_w_out", "final_norm_g")


def _pack(parts):
    flat = jnp.concatenate([t.reshape(-1) for t in parts])
    pad = (-flat.shape[0]) % LANE
    return jnp.pad(flat, (0, pad)).reshape(1, -1)


def _unpack(flat, like):
    out, off = [], 0
    for t in like:
        out.append(flat[0, off:off + t.size].reshape(t.shape))
        off += t.size
    return out


def kernel(x, c, ctx, c_ctx, ada_w, ada_b, norm1_g, ffn1_w_in, ffn1_w_out, norm2_g, mix_w_in, ret_decay_fwd, ret_decay_bwd, mla_q_norm_g, mla_w_uq, mla_kv_norm_g, mla_w_ukv, mix_w_out, norm3_g, ffn2_w_in, ffn2_w_out, final_norm_g, loss_target, m_c_ctx, m_ada_w, m_ada_b, m_norm1_g, m_ffn1_w_in, m_ffn1_w_out, m_norm2_g, m_mix_w_in, m_ret_decay_fwd, m_ret_decay_bwd, m_mla_q_norm_g, m_mla_w_uq, m_mla_kv_norm_g, m_mla_w_ukv, m_mix_w_out, m_norm3_g, m_ffn2_w_in, m_ffn2_w_out, m_final_norm_g, v_c_ctx, v_ada_w, v_ada_b, v_norm1_g, v_ffn1_w_in, v_ffn1_w_out, v_norm2_g, v_mix_w_in, v_ret_decay_fwd, v_ret_decay_bwd, v_mla_q_norm_g, v_mla_w_uq, v_mla_kv_norm_g, v_mla_w_ukv, v_mix_w_out, v_norm3_g, v_ffn2_w_in, v_ffn2_w_out, v_final_norm_g):
    w = dict(c_ctx=c_ctx, ada_w=ada_w, ada_b=ada_b, norm1_g=norm1_g, ffn1_w_in=ffn1_w_in, ffn1_w_out=ffn1_w_out,
             norm2_g=norm2_g, mix_w_in=mix_w_in, ret_decay_fwd=ret_decay_fwd, ret_decay_bwd=ret_decay_bwd,
             mla_q_norm_g=mla_q_norm_g, mla_w_uq=mla_w_uq, mla_kv_norm_g=mla_kv_norm_g, mla_w_ukv=mla_w_ukv,
             mix_w_out=mix_w_out, norm3_g=norm3_g, ffn2_w_in=ffn2_w_in, ffn2_w_out=ffn2_w_out,
             final_norm_g=final_norm_g)
    mom = dict(c_ctx=m_c_ctx, ada_w=m_ada_w, ada_b=m_ada_b, norm1_g=m_norm1_g, ffn1_w_in=m_ffn1_w_in,
               ffn1_w_out=m_ffn1_w_out, norm2_g=m_norm2_g, mix_w_in=m_mix_w_in, ret_decay_fwd=m_ret_decay_fwd,
               ret_decay_bwd=m_ret_decay_bwd, mla_q_norm_g=m_mla_q_norm_g, mla_w_uq=m_mla_w_uq,
               mla_kv_norm_g=m_mla_kv_norm_g, mla_w_ukv=m_mla_w_ukv, mix_w_out=m_mix_w_out, norm3_g=m_norm3_g,
               ffn2_w_in=m_ffn2_w_in, ffn2_w_out=m_ffn2_w_out, final_norm_g=m_final_norm_g)
    var = dict(c_ctx=v_c_ctx, ada_w=v_ada_w, ada_b=v_ada_b, norm1_g=v_norm1_g, ffn1_w_in=v_ffn1_w_in,
               ffn1_w_out=v_ffn1_w_out, norm2_g=v_norm2_g, mix_w_in=v_mix_w_in, ret_decay_fwd=v_ret_decay_fwd,
               ret_decay_bwd=v_ret_decay_bwd, mla_q_norm_g=v_mla_q_norm_g, mla_w_uq=v_mla_w_uq,
               mla_kv_norm_g=v_mla_kv_norm_g, mla_w_ukv=v_mla_w_ukv, mix_w_out=v_mix_w_out, norm3_g=v_norm3_g,
               ffn2_w_in=v_ffn2_w_in, ffn2_w_out=v_ffn2_w_out, final_norm_g=v_final_norm_g)
    me = 4 * lax.axis_index("x") + 2 * lax.axis_index("y") + lax.axis_index("c")

    gathered = _gather_two_level([w[k][0].astype(BF16) for k in BIG] + [jax.nn.silu(c)], "weights_gather")
    silu_c_all = gathered[-1][:, 0, :]

    p = dict(zip(BIG, gathered[:-1]))
    p["x"] = x[0]
    p["ada_w"] = ada_w[0]
    p["c_ctx"] = c_ctx
    p["ada_b"] = ada_b
    p["final_norm_g"] = final_norm_g[None, :]
    for k in ("norm1_g", "norm2_g", "norm3_g", "mla_q_norm_g", "mla_kv_norm_g", "ret_decay_fwd", "ret_decay_bwd"):
        p[k] = w[k]

    loss_local, grads = jax.value_and_grad(_local_loss)(p, ctx[0], silu_c_all, loss_target[0], me)
    grads["final_norm_g"] = grads["final_norm_g"][0]

    core = lax.axis_index("c").astype(jnp.int32).reshape(1)
    full = [grads[k] for k in BIG]
    theirs = _swap_sibling(full, "grads_swap")
    paired = [_pair_add(f, t, core, "grads_pair_" + k) for k, f, t in zip(BIG, full, theirs)]
    exchanged = _scatter_chips(paired, "grads_scatter")
    zero1 = [jnp.zeros((1,), F32)]
    small_like = zero1 + [w[k] for k in SMALL]
    small_all = _exchange([_pack([loss_local.reshape(1)] + [grads[k] for k in SMALL])], True, "small_grads_gather")[0]
    loss = jnp.sum(small_all[:, 0, 0])

    out_g, out_d, out_m, out_v = {}, {}, {}, {}

    def update(name, gstack, shape2d):
        res = _adamw(gstack, w[name].reshape(shape2d), mom[name].reshape(shape2d), var[name].reshape(shape2d),
                     "adamw_" + name)
        out_g[name], out_d[name], out_m[name], out_v[name] = [t.reshape(w[name].shape) for t in res]

    for k, gs in zip(BIG, exchanged):
        update(k, gs, gs.shape[1:])
    update("ada_w", grads["ada_w"][None], ada_w.shape[1:])
    res = _adamw(small_all, _pack(small_like), _pack(zero1 + [mom[k] for k in SMALL]),
                 _pack(zero1 + [var[k] for k in SMALL]), "adamw_small")
    for dst, flat in zip((out_g, out_d, out_m, out_v), res):
        for k, t in zip(SMALL, _unpack(flat, small_like)[1:]):
            dst[k] = t

    return (loss, grads["x"][None], *[out_g[k] for k in WEIGHTS], *[out_d[k] for k in WEIGHTS],
            *[out_m[k] for k in WEIGHTS], *[out_v[k] for k in WEIGHTS])
```

```python
import functools

import jax
import jax.numpy as jnp
from jax import lax
from jax.experimental import pallas as pl
from jax.experimental.pallas import tpu as pltpu

F32 = jnp.float32
BF16 = jnp.bfloat16

N_DEV = 8
MESH_AXES = ("x", "y", "c")

GRID_W = 64
N_MOD = 9
RET_HEADS = 8
RET_DK = 64
RET_DV = 128
RET_CHUNK = 256
RET_ROPE_BASE = 10000.0
MLA_HEADS = 8
MLA_Q_RANK = 512
MLA_KV_RANK = 256
MLA_NOPE = 128
MLA_ROPE = 64
MLA_V = 128
AXIAL_BASE = 10000.0
RMS_EPS = 1e-6
GN_EPS = 1e-5
SPLITS = (RET_HEADS * RET_DK, RET_HEADS * RET_DK, RET_HEADS * RET_DV, RET_HEADS * RET_DV,
          MLA_Q_RANK, MLA_KV_RANK, MLA_ROPE)
MIX_IN = sum(SPLITS)
MIX_IN_PAD = 4096

ADAM_LR = 0.001
ADAM_B1 = 0.9
ADAM_B2 = 0.999
ADAM_EPS = 1e-08
ADAM_WD = 0.01
ADAM_STEP = 10

LANE = 128
VMEM_LIMIT_BYTES = 56 * 1024 * 1024

NN = ((1,), (0,))
NT = ((1,), (1,))
TN = ((0,), (0,))


def _pick(dim, target, align=LANE):
    t = min(dim, target)
    t -= t % align
    while t >= align:
        if dim % t == 0:
            return t
        t -= align
    return dim


def _params():
    return pltpu.CompilerParams(vmem_limit_bytes=VMEM_LIMIT_BYTES)


def _dot(a, b, dims):
    return lax.dot_general(a.astype(BF16), b.astype(BF16), (dims, ((), ())), preferred_element_type=F32)


def _mm_call(name, grid, ins, pairs, outs, acc_shapes, epilogue):
    n_in, n_out = len(ins), len(outs)
    k_axis = len(grid) - 1
    k_steps = grid[k_axis]

    def body(*refs):
        in_refs = refs[:n_in]
        out_refs = refs[n_in:n_in + n_out]
        accs = refs[n_in + n_out:]
        k = pl.program_id(k_axis)

        @pl.when(k == 0)
        def _():
            for acc in accs:
                acc[...] = jnp.zeros_like(acc)

        for ai, bi, dims, ci in pairs:
            accs[ci][...] += _dot(in_refs[ai][...], in_refs[bi][...], dims)

        @pl.when(k == k_steps - 1)
        def _():
            epilogue([acc[...] for acc in accs], in_refs, out_refs)

    res = pl.pallas_call(
        body, name=name, grid=grid,
        in_specs=[s for _, s in ins], out_specs=[s for _, s in outs],
        out_shape=[s for s, _ in outs],
        scratch_shapes=[pltpu.VMEM(s, F32) for s in acc_shapes],
        compiler_params=_params(),
    )(*[a for a, _ in ins])
    return res


def _matmul(a, b, mode, out_dtype, name, tm=1024, tn=1024, tk=512):
    if mode == "nn":
        (m, kd), n = a.shape, b.shape[1]
    elif mode == "nt":
        (m, kd), n = a.shape, b.shape[0]
    else:
        (kd, m), n = a.shape, b.shape[1]
    tm, tn = _pick(m, tm, 16), _pick(n, tn)
    tk = _pick(kd, tk) if mode != "tn" else _pick(kd, tk, 16)
    if mode == "nn":
        a_spec = pl.BlockSpec((tm, tk), lambda i, j, k: (i, k))
        b_spec = pl.BlockSpec((tk, tn), lambda i, j, k: (k, j))
        dims = NN
    elif mode == "nt":
        a_spec = pl.BlockSpec((tm, tk), lambda i, j, k: (i, k))
        b_spec = pl.BlockSpec((tn, tk), lambda i, j, k: (j, k))
        dims = NT
    else:
        a_spec = pl.BlockSpec((tk, tm), lambda i, j, k: (k, i))
        b_spec = pl.BlockSpec((tk, tn), lambda i, j, k: (k, j))
        dims = TN

    def epilogue(accs, in_refs, out_refs):
        out_refs[0][...] = accs[0].astype(out_dtype)

    return _mm_call(
        name, (m // tm, n // tn, kd // tk), [(a, a_spec), (b, b_spec)], [(0, 1, dims, 0)],
        [(jax.ShapeDtypeStruct((m, n), out_dtype), pl.BlockSpec((tm, tn), lambda i, j, k: (i, j)))],
        [(tm, tn)], epilogue)[0]


def _norm_mod_tile(x, ng, sc, sh):
    r = lax.rsqrt(jnp.mean(x * x, axis=-1, keepdims=True) + RMS_EPS)
    return (x * r * ng) * (1.0 + sc) + sh


def _row_spec(tm, d):
    return pl.BlockSpec((tm, d), lambda i: (i, 0))


def _vec_spec(d):
    return pl.BlockSpec((1, d), lambda i: (0, 0))


def _norm_mod_fwd(x, ng, sc, sh, name):
    t, d = x.shape
    tm = _pick(t, 512, 16)

    def body(x_ref, ng_ref, sc_ref, sh_ref, h_ref):
        h_ref[...] = _norm_mod_tile(x_ref[...], ng_ref[...], sc_ref[...], sh_ref[...]).astype(BF16)

    return pl.pallas_call(
        body, name=name, grid=(t // tm,),
        in_specs=[_row_spec(tm, d), _vec_spec(d), _vec_spec(d), _vec_spec(d)],
        out_specs=_row_spec(tm, d), out_shape=jax.ShapeDtypeStruct((t, d), BF16),
        compiler_params=_params(),
    )(x, ng, sc, sh)


def _norm_mod_bwd(x, ng, sc, sh, dh, dres, name):
    t, d = x.shape
    tm = _pick(t, 256, 16)
    has_res = dres is not None

    def body(*refs):
        if has_res:
            x_ref, ng_ref, sc_ref, sh_ref, dh_ref, dres_ref, dx_ref, dng_ref, dsc_ref, dsh_ref = refs
        else:
            x_ref, ng_ref, sc_ref, sh_ref, dh_ref, dx_ref, dng_ref, dsc_ref, dsh_ref = refs
        _, vjp = jax.vjp(_norm_mod_tile, x_ref[...], ng_ref[...], sc_ref[...], sh_ref[...])
        dx, dng, dsc, dsh = vjp(dh_ref[...].astype(F32))
        if has_res:
            dx = dx + dres_ref[...]
        dx_ref[...] = dx

        @pl.when(pl.program_id(0) == 0)
        def _():
            dng_ref[...] = jnp.zeros_like(dng_ref)
            dsc_ref[...] = jnp.zeros_like(dsc_ref)
            dsh_ref[...] = jnp.zeros_like(dsh_ref)

        dng_ref[...] += dng
        dsc_ref[...] += dsc
        dsh_ref[...] += dsh

    ins = [x, ng, sc, sh, dh] + ([dres] if has_res else [])
    in_specs = [_row_spec(tm, d), _vec_spec(d), _vec_spec(d), _vec_spec(d), _row_spec(tm, d)]
    in_specs += [_row_spec(tm, d)] if has_res else []
    vec = jax.ShapeDtypeStruct((1, d), F32)
    return pl.pallas_call(
        body, name=name, grid=(t // tm,), in_specs=in_specs,
        out_specs=[_row_spec(tm, d), _vec_spec(d), _vec_spec(d), _vec_spec(d)],
        out_shape=[jax.ShapeDtypeStruct((t, d), F32), vec, vec, vec],
        compiler_params=_params(),
    )(*ins)


def _res_mm_fwd(a, w, x, gate, coef, name):
    t, kd = a.shape
    d = w.shape[1]
    tm, tn, tk = _pick(t, 1024, 16), _pick(d, 1024), _pick(kd, 2816)

    def epilogue(accs, in_refs, out_refs):
        f = accs[0]
        out_refs[0][...] = in_refs[2][...] + (coef * in_refs[3][...]) * f
        out_refs[1][...] = f.astype(BF16)

    tile = pl.BlockSpec((tm, tn), lambda i, j, k: (i, j))
    return _mm_call(
        name, (t // tm, d // tn, kd // tk),
        [(a, pl.BlockSpec((tm, tk), lambda i, j, k: (i, k))), (w, pl.BlockSpec((tk, tn), lambda i, j, k: (k, j))),
         (x, tile), (gate, pl.BlockSpec((1, tn), lambda i, j, k: (0, j)))],
        [(0, 1, NN, 0)],
        [(jax.ShapeDtypeStruct((t, d), F32), tile), (jax.ShapeDtypeStruct((t, d), BF16), tile)],
        [(tm, tn)], epilogue)


def _gate_bwd(dxo, f, gate, coef, name):
    t, d = dxo.shape
    tm = _pick(t, 512, 16)

    def body(dxo_ref, f_ref, gate_ref, df_ref, dgate_ref):
        dxo_t = dxo_ref[...]
        df_ref[...] = ((coef * gate_ref[...]) * dxo_t).astype(BF16)

        @pl.when(pl.program_id(0) == 0)
        def _():
            dgate_ref[...] = jnp.zeros_like(dgate_ref)

        dgate_ref[...] += coef * jnp.sum(dxo_t * f_ref[...].astype(F32), axis=0, keepdims=True)

    return pl.pallas_call(
        body, name=name, grid=(t // tm,),
        in_specs=[_row_spec(tm, d), _row_spec(tm, d), _vec_spec(d)],
        out_specs=[_row_spec(tm, d), _vec_spec(d)],
        out_shape=[jax.ShapeDtypeStruct((t, d), BF16), jax.ShapeDtypeStruct((1, d), F32)],
        compiler_params=_params(),
    )(dxo, f, gate)


def _ffn_in_fwd(h, w_in, name):
    t, d = h.shape
    n = w_in.shape[2]
    half = N_DEV // 2
    f = half * n
    tm = _pick(t, 512, 16)

    def epilogue(accs, in_refs, out_refs):
        g, u = accs
        out_refs[0][...] = (g * jax.nn.sigmoid(g) * u).astype(BF16)
        out_refs[1][0] = g.astype(BF16)
        out_refs[1][1] = u.astype(BF16)

    return _mm_call(
        name, (half, t // tm, 1),
        [(h, pl.BlockSpec((tm, d), lambda j, i, k: (i, 0))),
         (w_in, pl.BlockSpec((None, d, n), lambda j, i, k: (j, 0, 0))),
         (w_in, pl.BlockSpec((None, d, n), lambda j, i, k: (j + half, 0, 0)))],
        [(0, 1, NN, 0), (0, 2, NN, 1)],
        [(jax.ShapeDtypeStruct((t, f), BF16), pl.BlockSpec((tm, n), lambda j, i, k: (i, j))),
         (jax.ShapeDtypeStruct((2, t, f), BF16), pl.BlockSpec((2, tm, n), lambda j, i, k: (0, i, j)))],
        [(tm, n), (tm, n)], epilogue)


def _ffn_da_bwd(df, w_out2d, gu, name):
    t, d = df.shape
    f = w_out2d.shape[0]
    half = N_DEV // 2
    n = f // half
    tm = _pick(t, 512, 16)

    def epilogue(accs, in_refs, out_refs):
        da = accs[0]
        g = in_refs[2][0].astype(F32)
        u = in_refs[2][1].astype(F32)
        s = jax.nn.sigmoid(g)
        out_refs[0][0] = (da * u * (s * (1.0 + g * (1.0 - s)))).astype(BF16)
        out_refs[0][1] = (da * (g * s)).astype(BF16)

    gu_spec = pl.BlockSpec((2, tm, n), lambda j, i, k: (0, i, j))
    return _mm_call(
        name, (half, t // tm, 1),
        [(df, pl.BlockSpec((tm, d), lambda j, i, k: (i, 0))),
         (w_out2d, pl.BlockSpec((n, d), lambda j, i, k: (j, 0))),
         (gu, gu_spec)],
        [(0, 1, NT, 0)],
        [(jax.ShapeDtypeStruct((2, t, f), BF16), gu_spec)],
        [(tm, n)], epilogue)[0]


def _ffn_dh_bwd(dgu, w_in, name):
    _, t, f = dgu.shape
    d, n = w_in.shape[1], w_in.shape[2]
    half = N_DEV // 2
    tm = _pick(t, 512, 16)

    def epilogue(accs, in_refs, out_refs):
        out_refs[0][...] = accs[0]

    return _mm_call(
        name, (t // tm, 1, half),
        [(dgu, pl.BlockSpec((None, tm, n), lambda i, j, k: (0, i, k))),
         (dgu, pl.BlockSpec((None, tm, n), lambda i, j, k: (1, i, k))),
         (w_in, pl.BlockSpec((None, d, n), lambda i, j, k: (k, 0, 0))),
         (w_in, pl.BlockSpec((None, d, n), lambda i, j, k: (k + half, 0, 0)))],
        [(0, 2, NT, 0), (1, 3, NT, 0)],
        [(jax.ShapeDtypeStruct((t, d), F32), pl.BlockSpec((tm, d), lambda i, j, k: (i, 0)))],
        [(tm, d)], epilogue)[0]


def _ffn_dwin_bwd(h, dgu, name):
    t, d = h.shape
    f = dgu.shape[2]
    half = N_DEV // 2
    n = f // half
    tk = _pick(t, 1024, 16)

    def epilogue(accs, in_refs, out_refs):
        out_refs[0][...] = accs[0].astype(BF16)

    return _mm_call(
        name, (N_DEV, 1, t // tk),
        [(h, pl.BlockSpec((tk, d), lambda j, i, k: (k, 0))),
         (dgu, pl.BlockSpec((None, tk, n), lambda j, i, k: (j // half, k, j % half)))],
        [(0, 1, TN, 0)],
        [(jax.ShapeDtypeStruct((N_DEV, d, n), BF16), pl.BlockSpec((None, d, n), lambda j, i, k: (j, 0, 0)))],
        [(d, n)], epilogue)[0]


def _make_ffn_block(tag):
    @jax.custom_vjp
    def ffn_block(x, ng, sh, sc, gate, w_in, w_out):
        return fwd(x, ng, sh, sc, gate, w_in, w_out)[0]

    def fwd(x, ng, sh, sc, gate, w_in, w_out):
        f = w_out.shape[0] * w_out.shape[1]
        w_out2d = w_out.reshape(f, w_out.shape[2])
        h = _norm_mod_fwd(x, ng, sc, sh, tag + "_norm")
        a, gu = _ffn_in_fwd(h, w_in, tag + "_in")
        xo, f1 = _res_mm_fwd(a, w_out2d, x, gate, 0.5, tag + "_out")
        return xo, (x, ng, sh, sc, gate, w_in, w_out, h, a, gu, f1)

    def bwd(res, dxo):
        x, ng, sh, sc, gate, w_in, w_out, h, a, gu, f1 = res
        f = w_out.shape[0] * w_out.shape[1]
        w_out2d = w_out.reshape(f, w_out.shape[2])
        df, dgate = _gate_bwd(dxo, f1, gate, 0.5, tag + "_dgate")
        dgu = _ffn_da_bwd(df, w_out2d, gu, tag + "_da")
        dw_out = _matmul(a, df, "tn", BF16, tag + "_dwout", tm=_pick(f, 1408, 16), tn=2048, tk=1024)
        dh = _ffn_dh_bwd(dgu, w_in, tag + "_dh")
        dw_in = _ffn_dwin_bwd(h, dgu, tag + "_dwin")
        dx, dng, dsc, dsh = _norm_mod_bwd(x, ng, sc, sh, dh, dxo, tag + "_dnorm")
        return dx, dng, dsh, dsc, dgate, dw_in, dw_out.reshape(w_out.shape)

    ffn_block.defvjp(fwd, bwd)
    return ffn_block


def _make_norm_proj(tag):
    @jax.custom_vjp
    def norm_proj(x, ng, sh, sc, w):
        return fwd(x, ng, sh, sc, w)[0]

    def fwd(x, ng, sh, sc, w):
        h = _norm_mod_fwd(x, ng, sc, sh, tag + "_norm")
        p = _matmul(h, w, "nn", F32, tag + "_mm", tm=1024, tn=1024, tk=w.shape[0])
        return p, (x, ng, sh, sc, w, h)

    def bwd(res, dp):
        x, ng, sh, sc, w, h = res
        dh = _matmul(dp, w, "nt", F32, tag + "_dh", tm=512, tn=w.shape[0], tk=2048)
        dw = _matmul(h, dp, "tn", BF16, tag + "_dw", tm=w.shape[0], tn=1024, tk=1024)
        dx, dng, dsc, dsh = _norm_mod_bwd(x, ng, sc, sh, dh, None, tag + "_dnorm")
        return dx, dng, dsh, dsc, dw

    norm_proj.defvjp(fwd, bwd)
    return norm_proj


def _make_res_proj(tag):
    @jax.custom_vjp
    def res_proj(a, w, x, gate):
        return fwd(a, w, x, gate)[0]

    def fwd(a, w, x, gate):
        xo, f = _res_mm_fwd(a, w, x, gate, 1.0, tag + "_mm")
        return xo, (a, w, gate, f)

    def bwd(res, dxo):
        a, w, gate, f = res
        df, dgate = _gate_bwd(dxo, f, gate, 1.0, tag + "_dgate")
        da = _matmul(df, w, "nt", BF16, tag + "_da", tm=1024, tn=1024, tk=2048)
        dw = _matmul(a, df, "tn", BF16, tag + "_dw", tm=1024, tn=2048, tk=1024)
        return da, dw, dxo, dgate

    res_proj.defvjp(fwd, bwd)
    return res_proj


def _make_small_mm(tag):
    @jax.custom_vjp
    def small_mm(a, w):
        return _matmul(a, w, "nn", F32, tag + "_mm", tm=a.shape[0], tn=768, tk=w.shape[0])

    def fwd(a, w):
        return small_mm(a, w), (a, w)

    def bwd(res, dr):
        a, w = res
        da = _matmul(dr, w, "nt", F32, tag + "_da", tm=a.shape[0], tn=w.shape[0], tk=768)
        dw = _matmul(a, dr, "tn", F32, tag + "_dw", tm=1024, tn=768, tk=a.shape[0])
        return da, dw

    small_mm.defvjp(fwd, bwd)
    return small_mm


def _ret_chunk_terms(lg, c, reverse):
    row = lax.broadcasted_iota(jnp.int32, (c, c), 0).astype(F32)
    col = lax.broadcasted_iota(jnp.int32, (c, c), 1).astype(F32)
    pos = lax.broadcasted_iota(jnp.int32, (c, 1), 0).astype(F32)
    if reverse:
        diff = col - row
        mask = diff > 0.0
        e_exp = float(c) - pos
        f_exp = pos
    else:
        diff = row - col
        mask = diff >= 0.0
        e_exp = pos + 1.0
        f_exp = float(c - 1) - pos
    diffm = jnp.where(mask, diff, 0.0)
    dm = jnp.where(mask, jnp.exp(lg * diffm), 0.0)
    return diffm, dm, e_exp, jnp.exp(lg * e_exp), f_exp, jnp.exp(lg * f_exp)


def _lane0(val):
    lane = lax.broadcasted_iota(jnp.int32, (1, LANE), 1)
    return jnp.where(lane == 0, val, 0.0)


RET_HEAD_BLOCK = 4


def _make_ret_dir(tag, reverse):
    hb = RET_HEAD_BLOCK

    def heads_spec(nc, width, flip):
        if flip:
            return pl.BlockSpec((hb, RET_CHUNK, width), lambda h, t: (h, nc - 1 - t, 0))
        return pl.BlockSpec((hb, RET_CHUNK, width), lambda h, t: (h, t, 0))

    def state_spec(nc, flip):
        if flip:
            return pl.BlockSpec((hb, None, RET_DK, RET_DV), lambda h, t: (h, nc - 1 - t, 0, 0))
        return pl.BlockSpec((hb, None, RET_DK, RET_DV), lambda h, t: (h, t, 0, 0))

    lg_spec = pl.BlockSpec((hb, 1, LANE), lambda h, t: (h, 0, 0))

    def fwd_call(q, k, v, lgb):
        hh, ll, _ = q.shape
        c = RET_CHUNK
        nc = ll // c

        def body(q_ref, k_ref, v_ref, lg_ref, y_ref, sall_ref, s_scr):
            @pl.when(pl.program_id(1) == 0)
            def _():
                s_scr[...] = jnp.zeros_like(s_scr)

            for b in range(hb):
                lg = lg_ref[b][:, :1]
                _, dm, _, xi, _, zeta = _ret_chunk_terms(lg, c, reverse)
                q_t, k_t, v_t = q_ref[b], k_ref[b], v_ref[b]
                s = s_scr[b]
                p = _dot(q_t, k_t, NT) * dm
                y_ref[b] = _dot(p, v_t, NN) + _dot(q_t * xi, s, NN)
                sall_ref[b] = s
                s_scr[b] = jnp.exp(lg * float(c)) * s + _dot(k_t * zeta, v_t, TN)

        return pl.pallas_call(
            body, name=tag + "_fwd", grid=(hh // hb, nc),
            in_specs=[heads_spec(nc, RET_DK, reverse), heads_spec(nc, RET_DK, reverse),
                      heads_spec(nc, RET_DV, reverse), lg_spec],
            out_specs=[heads_spec(nc, RET_DV, reverse), state_spec(nc, reverse)],
            out_shape=[jax.ShapeDtypeStruct((hh, ll, RET_DV), F32),
                       jax.ShapeDtypeStruct((hh, nc, RET_DK, RET_DV), F32)],
            scratch_shapes=[pltpu.VMEM((hb, RET_DK, RET_DV), F32)],
            compiler_params=_params(),
        )(q, k, v, lgb)

    def bwd_call(q, k, v, lgb, sall, dy):
        hh, ll, _ = q.shape
        c = RET_CHUNK
        nc = ll // c
        flip = not reverse

        def body(q_ref, k_ref, v_ref, lg_ref, sall_ref, dy_ref, dq_ref, dk_ref, dv_ref, dlg_ref, ds_scr):
            @pl.when(pl.program_id(1) == 0)
            def _():
                ds_scr[...] = jnp.zeros_like(ds_scr)
                dlg_ref[...] = jnp.zeros_like(dlg_ref)

            def total(m):
                return jnp.sum(jnp.sum(m, axis=1, keepdims=True), axis=0, keepdims=True)

            for b in range(hb):
                lg = lg_ref[b][:, :1]
                diffm, dm, e_exp, xi, f_exp, zeta = _ret_chunk_terms(lg, c, reverse)
                q_t, k_t, v_t, dy_t = q_ref[b], k_ref[b], v_ref[b], dy_ref[b]
                s = sall_ref[b]
                dsn = ds_scr[b]
                a = _dot(q_t, k_t, NT)
                da = _dot(dy_t, v_t, NT) * dm
                g = _dot(dy_t, s, NT)
                hm = _dot(v_t, dsn, NT)
                dq_ref[b] = _dot(da, k_t, NN) + xi * g
                dk_ref[b] = _dot(da, q_t, TN) + zeta * hm
                dv_ref[b] = _dot(a * dm, dy_t, TN) + _dot(k_t * zeta, dsn, NN)
                gc = jnp.exp(lg * float(c))
                ds_scr[b] = gc * dsn + _dot(q_t * xi, dy_t, TN)
                dl = (total(da * a * diffm) + total(e_exp * xi * q_t * g)
                      + float(c) * gc * total(s * dsn) + total(f_exp * zeta * k_t * hm))
                dlg_ref[b] += _lane0(dl)

        return pl.pallas_call(
            body, name=tag + "_bwd", grid=(hh // hb, nc),
            in_specs=[heads_spec(nc, RET_DK, flip), heads_spec(nc, RET_DK, flip), heads_spec(nc, RET_DV, flip),
                      lg_spec, state_spec(nc, flip), heads_spec(nc, RET_DV, flip)],
            out_specs=[heads_spec(nc, RET_DK, flip), heads_spec(nc, RET_DK, flip), heads_spec(nc, RET_DV, flip),
                       lg_spec],
            out_shape=[jax.ShapeDtypeStruct((hh, ll, RET_DK), F32), jax.ShapeDtypeStruct((hh, ll, RET_DK), F32),
                       jax.ShapeDtypeStruct((hh, ll, RET_DV), F32), jax.ShapeDtypeStruct((hh, 1, LANE), F32)],
            scratch_shapes=[pltpu.VMEM((hb, RET_DK, RET_DV), F32)],
            compiler_params=_params(),
        )(q, k, v, lgb, sall, dy)

    @jax.custom_vjp
    def ret_dir(q, k, v, lgb):
        return fwd_call(q, k, v, lgb)[0]

    def fwd(q, k, v, lgb):
        y, sall = fwd_call(q, k, v, lgb)
        return y, (q, k, v, lgb, sall)

    def bwd(res, dy):
        q, k, v, lgb, sall = res
        dq, dk, dv, dlg = bwd_call(q, k, v, lgb, sall, dy)
        return dq, dk, dv, dlg

    ret_dir.defvjp(fwd, bwd)
    return ret_dir


def _ret_out_tile(y, g):
    mu = jnp.mean(y, axis=-1, keepdims=True)
    var = jnp.mean(jnp.square(y - mu), axis=-1, keepdims=True)
    return (g * jax.nn.sigmoid(g)) * ((y - mu) * lax.rsqrt(var + GN_EPS))


def _make_ret_out(tag):
    def specs(tm):
        y_spec = pl.BlockSpec((None, tm, RET_DV), lambda h, i: (h, i, 0))
        g_spec = pl.BlockSpec((tm, RET_DV), lambda h, i: (i, h))
        return y_spec, g_spec

    def fwd_call(y, g):
        hh, n, _ = y.shape
        tm = _pick(n, 1024, 16)
        y_spec, g_spec = specs(tm)

        def body(y_ref, g_ref, o_ref):
            o_ref[...] = _ret_out_tile(y_ref[...], g_ref[...]).astype(BF16)

        return pl.pallas_call(
            body, name=tag + "_fwd", grid=(hh, n // tm), in_specs=[y_spec, g_spec], out_specs=g_spec,
            out_shape=jax.ShapeDtypeStruct((n, hh * RET_DV), BF16), compiler_params=_params(),
        )(y, g)

    def bwd_call(y, g, do):
        hh, n, _ = y.shape
        tm = _pick(n, 1024, 16)
        y_spec, g_spec = specs(tm)

        def body(y_ref, g_ref, do_ref, dy_ref, dg_ref):
            _, vjp = jax.vjp(_ret_out_tile, y_ref[...], g_ref[...])
            dy, dg = vjp(do_ref[...].astype(F32))
            dy_ref[...] = dy
            dg_ref[...] = dg

        return pl.pallas_call(
            body, name=tag + "_bwd", grid=(hh, n // tm), in_specs=[y_spec, g_spec, g_spec],
            out_specs=[y_spec, g_spec],
            out_shape=[jax.ShapeDtypeStruct(y.shape, F32), jax.ShapeDtypeStruct(g.shape, F32)],
            compiler_params=_params(),
        )(y, g, do)

    @jax.custom_vjp
    def ret_out(y, g):
        return fwd_call(y, g)

    def fwd(y, g):
        return fwd_call(y, g), (y, g)

    def bwd(res, do):
        y, g = res
        return tuple(bwd_call(y, g, do))

    ret_out.defvjp(fwd, bwd)
    return ret_out


def _make_attention(tag):
    neg_big = -1e30
    log2e = 1.4426950408889634
    sub = 256

    def fwd_call(q, k, vt):
        hh, n, dq = q.shape
        dv, ll = vt.shape[1], vt.shape[2]
        tq, tk = _pick(n, 1024), _pick(ll, 1408)
        sb = sub if tk % sub == 0 else tk
        scale = dq ** -0.5
        c2 = scale * log2e
        k_steps = ll // tk

        def body(q_ref, k_ref, vt_ref, o_ref, lse_ref, m_scr, l_scr, acc_scr, s_scr, p_scr):
            j = pl.program_id(2)

            @pl.when(j == 0)
            def _():
                m_scr[...] = jnp.full_like(m_scr, neg_big)
                l_scr[...] = jnp.zeros_like(l_scr)
                acc_scr[...] = jnp.zeros_like(acc_scr)

            q_t = q_ref[...]
            m_prev = m_scr[...]
            m_new = m_prev
            for kk in range(tk // sb):
                rows = slice(kk * sb, (kk + 1) * sb)
                s_t = _dot(k_ref[rows, :], q_t, NT)
                s_scr[rows, :] = s_t
                m_new = jnp.maximum(m_new, jnp.max(s_t, axis=0, keepdims=True))
            mc = m_new * c2
            l_part = jnp.zeros_like(m_new)
            for kk in range(tk // sb):
                rows = slice(kk * sb, (kk + 1) * sb)
                p_t = jnp.exp2(s_scr[rows, :] * c2 - mc)
                l_part = l_part + jnp.sum(p_t, axis=0, keepdims=True)
                p_scr[rows, :] = p_t.astype(BF16)
            alpha = jnp.exp2((m_prev - m_new) * c2)
            l_scr[...] = alpha * l_scr[...] + l_part
            acc_scr[...] = alpha * acc_scr[...] + _dot(vt_ref[...], p_scr[...], NN)
            m_scr[...] = m_new

            @pl.when(j == k_steps - 1)
            def _():
                o_ref[...] = jnp.transpose(acc_scr[...] / l_scr[...]).astype(BF16)
                lse_ref[...] = m_scr[...] * scale + jnp.log(l_scr[...])

        return pl.pallas_call(
            body, name=tag + "_fwd", grid=(hh, n // tq, k_steps),
            in_specs=[pl.BlockSpec((None, tq, dq), lambda h, i, j: (h, i, 0)),
                      pl.BlockSpec((None, tk, dq), lambda h, i, j: (h, j, 0)),
                      pl.BlockSpec((None, dv, tk), lambda h, i, j: (h, 0, j))],
            out_specs=[pl.BlockSpec((tq, dv), lambda h, i, j: (i, h)),
                       pl.BlockSpec((None, 1, tq), lambda h, i, j: (h, 0, i))],
            out_shape=[jax.ShapeDtypeStruct((n, hh * dv), BF16), jax.ShapeDtypeStruct((hh, 1, n), F32)],
            scratch_shapes=[pltpu.VMEM((1, tq), F32), pltpu.VMEM((1, tq), F32), pltpu.VMEM((dv, tq), F32),
                            pltpu.VMEM((tk, tq), F32), pltpu.VMEM((tk, tq), BF16)],
            compiler_params=_params(),
        )(q, k, vt)

    def delta_call(o, do, hh):
        n = o.shape[0]
        dv = o.shape[1] // hh
        tq = _pick(n, 1024)

        def body(o_ref, do_ref, d_ref):
            prod_t = jnp.transpose(o_ref[...].astype(F32) * do_ref[...].astype(F32))
            d_ref[...] = jnp.sum(prod_t, axis=0, keepdims=True)

        spec = pl.BlockSpec((tq, dv), lambda h, i: (i, h))
        return pl.pallas_call(
            body, name=tag + "_delta", grid=(hh, n // tq), in_specs=[spec, spec],
            out_specs=pl.BlockSpec((None, 1, tq), lambda h, i: (h, 0, i)),
            out_shape=jax.ShapeDtypeStruct((hh, 1, n), F32), compiler_params=_params(),
        )(o, do)

    def bwd_call(q, k, kt, v, do, lse, delta):
        hh, n, dq = q.shape
        ll, dv = k.shape[1], v.shape[2]
        tq, tk = _pick(n, 1024), _pick(ll, 1408)
        sb = tk
        scale = dq ** -0.5
        c2 = scale * log2e
        q_steps = n // tq

        def body(q_ref, k_ref, kt_ref, v_ref, do_ref, lse_ref, d_ref, dqt_ref, dk_ref, dv_ref, dk_scr, dv_scr):
            j = pl.program_id(1)
            i = pl.program_id(2)

            @pl.when(i == 0)
            def _():
                dk_scr[...] = jnp.zeros_like(dk_scr)
                dv_scr[...] = jnp.zeros_like(dv_scr)

            q_t, do_t = q_ref[...], do_ref[...]
            lse2 = lse_ref[...] * log2e
            delta_t = d_ref[...]
            dq_part = None
            for kk in range(tk // sb):
                rows = slice(kk * sb, (kk + 1) * sb)
                s_t = _dot(k_ref[rows, :], q_t, NT)
                p_t = jnp.exp2(s_t * c2 - lse2)
                ds_t = p_t * (_dot(v_ref[rows, :], do_t, NT) - delta_t)
                dv_scr[rows, :] += _dot(p_t, do_t, NN)
                dk_scr[rows, :] += _dot(ds_t, q_t, NN)
                part = _dot(kt_ref[:, rows], ds_t, NN)
                dq_part = part if dq_part is None else dq_part + part
            cols = pl.ds(pl.multiple_of(i * tq, tq), tq)

            @pl.when(j == 0)
            def _():
                dqt_ref[:, cols] = dq_part

            @pl.when(j > 0)
            def _():
                dqt_ref[:, cols] += dq_part

            @pl.when(i == q_steps - 1)
            def _():
                dk_ref[...] = (dk_scr[...] * scale).astype(BF16)
                dv_ref[...] = dv_scr[...].astype(BF16)

        return pl.pallas_call(
            body, name=tag + "_bwd", grid=(hh, ll // tk, q_steps),
            in_specs=[pl.BlockSpec((None, tq, dq), lambda h, j, i: (h, i, 0)),
                      pl.BlockSpec((None, tk, dq), lambda h, j, i: (h, j, 0)),
                      pl.BlockSpec((None, dq, tk), lambda h, j, i: (h, 0, j)),
                      pl.BlockSpec((None, tk, dv), lambda h, j, i: (h, j, 0)),
                      pl.BlockSpec((tq, dv), lambda h, j, i: (i, h)),
                      pl.BlockSpec((None, 1, tq), lambda h, j, i: (h, 0, i)),
                      pl.BlockSpec((None, 1, tq), lambda h, j, i: (h, 0, i))],
            out_specs=[pl.BlockSpec((None, dq, n), lambda h, j, i: (h, 0, 0)),
                       pl.BlockSpec((None, tk, dq), lambda h, j, i: (h, j, 0)),
                       pl.BlockSpec((None, tk, dv), lambda h, j, i: (h, j, 0))],
            out_shape=[jax.ShapeDtypeStruct((hh, dq, n), F32), jax.ShapeDtypeStruct((hh, ll, dq), BF16),
                       jax.ShapeDtypeStruct((hh, ll, dv), BF16)],
            scratch_shapes=[pltpu.VMEM((tk, dq), F32), pltpu.VMEM((tk, dv), F32)],
            compiler_params=_params(),
        )(q, k, kt, v, do, lse, delta)

    @jax.custom_vjp
    def attention(q, k, v):
        return fwd_call(q, k, jnp.swapaxes(v, 1, 2))[0]

    def fwd(q, k, v):
        o, lse = fwd_call(q, k, jnp.swapaxes(v, 1, 2))
        return o, (q, k, v, o, lse)

    def bwd(res, do):
        q, k, v, o, lse = res
        delta = delta_call(o, do, q.shape[0])
        dqt, dk, dv = bwd_call(q, k, jnp.swapaxes(k, 1, 2), v, do, lse, delta)
        return (jnp.swapaxes(dqt, 1, 2) * (q.shape[2] ** -0.5)).astype(BF16), dk, dv

    attention.defvjp(fwd, bwd)
    return attention


def _loss_tile(x, g, tgt):
    r = lax.rsqrt(jnp.mean(x * x, axis=-1, keepdims=True) + RMS_EPS)
    err = x * r * g - tgt
    per_tok = jnp.mean(err * err, axis=-1, keepdims=True)
    return 0.5 * jnp.sum(per_tok, axis=0, keepdims=True)


def _make_final_loss(tag):
    def fwd_call(x, g, tgt):
        t, d = x.shape
        tm = _pick(t, 512, 16)

        def body(x_ref, g_ref, t_ref, l_ref):
            l_ref[...] = jnp.broadcast_to(_loss_tile(x_ref[...], g_ref[...], t_ref[...]), (1, LANE))

        parts = pl.pallas_call(
            body, name=tag + "_fwd", grid=(t // tm,),
            in_specs=[_row_spec(tm, d), _vec_spec(d), _row_spec(tm, d)],
            out_specs=pl.BlockSpec((None, 1, LANE), lambda i: (i, 0, 0)),
            out_shape=jax.ShapeDtypeStruct((t // tm, 1, LANE), F32), compiler_params=_params(),
        )(x, g, tgt)
        return jnp.sum(parts[:, 0, 0])

    def bwd_call(x, g, tgt, dl):
        t, d = x.shape
        tm = _pick(t, 256, 16)

        def body(x_ref, g_ref, t_ref, dl_ref, dx_ref, dg_ref):
            _, vjp = jax.vjp(_loss_tile, x_ref[...], g_ref[...], t_ref[...])
            dx, dg, _ = vjp(dl_ref[...])
            dx_ref[...] = dx

            @pl.when(pl.program_id(0) == 0)
            def _():
                dg_ref[...] = jnp.zeros_like(dg_ref)

            dg_ref[...] += dg

        return pl.pallas_call(
            body, name=tag + "_bwd", grid=(t // tm,),
            in_specs=[_row_spec(tm, d), _vec_spec(d), _row_spec(tm, d), pl.BlockSpec((1, 1), lambda i: (0, 0))],
            out_specs=[_row_spec(tm, d), _vec_spec(d)],
            out_shape=[jax.ShapeDtypeStruct((t, d), F32), jax.ShapeDtypeStruct((1, d), F32)],
            compiler_params=_params(),
        )(x, g, tgt, dl)

    @jax.custom_vjp
    def final_loss(x, g, tgt):
        return fwd_call(x, g, tgt)

    def fwd(x, g, tgt):
        return fwd_call(x, g, tgt), (x, g, tgt)

    def bwd(res, dl):
        x, g, tgt = res
        dx, dg = bwd_call(x, g, tgt, dl.reshape(1, 1).astype(F32))
        return dx, dg, jnp.zeros_like(tgt)

    final_loss.defvjp(fwd, bwd)
    return final_loss


def _exchange(arrays, gather, name):
    n = len(arrays)

    def body(*refs):
        ins, outs = refs[:n], refs[n:2 * n]
        send_sems, recv_sems, local_sems = refs[2 * n:]
        me = 4 * lax.axis_index("x") + 2 * lax.axis_index("y") + lax.axis_index("c")

        def remote(a, d, wait_side=False):
            peer = (me + d) % N_DEV
            origin = (me + N_DEV - d) % N_DEV
            src = ins[a] if gather else ins[a].at[peer]
            dst = outs[a].at[origin if wait_side else me]
            return pltpu.make_async_remote_copy(
                src_ref=src, dst_ref=dst, send_sem=send_sems.at[a, d - 1], recv_sem=recv_sems.at[a, d - 1],
                device_id=(peer // 4, (peer // 2) % 2, peer % 2), device_id_type=pl.DeviceIdType.MESH)

        def local(a):
            src = ins[a] if gather else ins[a].at[me]
            return pltpu.make_async_copy(src, outs[a].at[me], local_sems.at[a])

        for a in range(n):
            for d in range(1, N_DEV):
                remote(a, d).start()
            local(a).start()
        for a in range(n):
            local(a).wait()
            for d in range(1, N_DEV):
                remote(a, d, wait_side=True).wait_recv()
                remote(a, d).wait_send()

    out_shape = []
    for arr in arrays:
        shape = (N_DEV,) + arr.shape if gather else arr.shape
        out_shape.append(jax.ShapeDtypeStruct(shape, arr.dtype))
    any_spec = pl.BlockSpec(memory_space=pl.ANY)
    return pl.pallas_call(
        body, name=name, in_specs=[any_spec] * n, out_specs=[any_spec] * n, out_shape=out_shape,
        scratch_shapes=[pltpu.SemaphoreType.DMA((n, N_DEV - 1)), pltpu.SemaphoreType.DMA((n, N_DEV - 1)),
                        pltpu.SemaphoreType.DMA((n,))],
        compiler_params=pltpu.CompilerParams(has_side_effects=True),
    )(*arrays)


def _coords():
    return lax.axis_index("x"), lax.axis_index("y"), lax.axis_index("c")


def _other_chips(x, y):
    return [(1 - x, y), (x, 1 - y), (1 - x, 1 - y)]


def _gather_two_level(arrays, name):
    n = len(arrays)

    def body(*refs):
        ins, outs = refs[:n], refs[n:2 * n]
        send_sems, recv_sems, local_sems = refs[2 * n:]
        x, y, c = _coords()
        me, sib = (x, y, c), (x, y, 1 - c)
        chips = _other_chips(x, y)

        def copy(a, k, block, to, from_input=False):
            slot = 4 * block[0] + 2 * block[1] + block[2]
            return pltpu.make_async_remote_copy(
                src_ref=ins[a] if from_input else outs[a].at[slot], dst_ref=outs[a].at[slot],
                send_sem=send_sems.at[a, k], recv_sem=recv_sems.at[a, k],
                device_id=to, device_id_type=pl.DeviceIdType.MESH)

        def local(a):
            return pltpu.make_async_copy(ins[a], outs[a].at[4 * x + 2 * y + c], local_sems.at[a])

        for a in range(n):
            for j, chip in enumerate(chips):
                copy(a, 1 + j, me, (*chip, c), True).start()
            copy(a, 0, me, sib, True).start()
            local(a).start()
        for a in range(n):
            for j, chip in enumerate(chips):
                copy(a, 1 + j, (*chip, c), me).wait_recv()
                copy(a, 4 + j, (*chip, c), sib).start()
        for a in range(n):
            copy(a, 0, sib, me).wait_recv()
            for j, chip in enumerate(chips):
                copy(a, 4 + j, (*chip, 1 - c), me).wait_recv()
            for k in range(N_DEV - 1):
                copy(a, k, me, sib, True).wait_send()
            local(a).wait()

    any_spec = pl.BlockSpec(memory_space=pl.ANY)
    return pl.pallas_call(
        body, name=name, in_specs=[any_spec] * n, out_specs=[any_spec] * n,
        out_shape=[jax.ShapeDtypeStruct((N_DEV,) + arr.shape, arr.dtype) for arr in arrays],
        scratch_shapes=[pltpu.SemaphoreType.DMA((n, N_DEV - 1)), pltpu.SemaphoreType.DMA((n, N_DEV - 1)),
                        pltpu.SemaphoreType.DMA((n,))],
        compiler_params=pltpu.CompilerParams(has_side_effects=True),
    )(*arrays)


def _swap_sibling(arrays, name):
    n = len(arrays)
    n_chip = N_DEV // 2

    def body(*refs):
        ins, outs = refs[:n], refs[n:2 * n]
        send_sems, recv_sems = refs[2 * n:]
        x, y, c = _coords()

        def copy(a, q):
            return pltpu.make_async_remote_copy(
                src_ref=ins[a].at[2 * q + (1 - c)], dst_ref=outs[a].at[q],
                send_sem=send_sems.at[a, q], recv_sem=recv_sems.at[a, q],
                device_id=(x, y, 1 - c), device_id_type=pl.DeviceIdType.MESH)

        for a in range(n):
            for q in range(n_chip):
                copy(a, q).start()
        for a in range(n):
            for q in range(n_chip):
                copy(a, q).wait_recv()
                copy(a, q).wait_send()

    any_spec = pl.BlockSpec(memory_space=pl.ANY)
    return pl.pallas_call(
        body, name=name, in_specs=[any_spec] * n, out_specs=[any_spec] * n,
        out_shape=[jax.ShapeDtypeStruct((n_chip,) + arr.shape[1:], arr.dtype) for arr in arrays],
        scratch_shapes=[pltpu.SemaphoreType.DMA((n, n_chip)), pltpu.SemaphoreType.DMA((n, n_chip))],
        compiler_params=pltpu.CompilerParams(has_side_effects=True),
    )(*arrays)


def _scatter_chips(arrays, name):
    n = len(arrays)
    n_chip = N_DEV // 2

    def body(*refs):
        ins, outs = refs[:n], refs[n:2 * n]
        send_sems, recv_sems, local_sems = refs[2 * n:]
        x, y, c = _coords()
        q_me = 2 * x + y
        chips = _other_chips(x, y)

        def copy(a, j, wait_side=False):
            q_peer = 2 * chips[j][0] + chips[j][1]
            return pltpu.make_async_remote_copy(
                src_ref=ins[a].at[q_peer], dst_ref=outs[a].at[q_peer if wait_side else q_me],
                send_sem=send_sems.at[a, j], recv_sem=recv_sems.at[a, j],
                device_id=(*chips[j], c), device_id_type=pl.DeviceIdType.MESH)

        def local(a):
            return pltpu.make_async_copy(ins[a].at[q_me], outs[a].at[q_me], local_sems.at[a])

        for a in range(n):
            for j in range(n_chip - 1):
                copy(a, j).start()
            local(a).start()
        for a in range(n):
            local(a).wait()
            for j in range(n_chip - 1):
                copy(a, j, wait_side=True).wait_recv()
                copy(a, j).wait_send()

    any_spec = pl.BlockSpec(memory_space=pl.ANY)
    return pl.pallas_call(
        body, name=name, in_specs=[any_spec] * n, out_specs=[any_spec] * n,
        out_shape=[jax.ShapeDtypeStruct(arr.shape, arr.dtype) for arr in arrays],
        scratch_shapes=[pltpu.SemaphoreType.DMA((n, n_chip - 1)), pltpu.SemaphoreType.DMA((n, n_chip - 1)),
                        pltpu.SemaphoreType.DMA((n,))],
        compiler_params=pltpu.CompilerParams(has_side_effects=True),
    )(*arrays)


def _pair_add(full, theirs, core, name):
    n_chip, r, cn = theirs.shape
    tr = _pick(r, max(16, (2 * 1024 * 1024) // (4 * cn) // 16 * 16), 16)

    def body(core_ref, mine_ref, theirs_ref, o_ref):
        o_ref[...] = (mine_ref[...].astype(F32) + theirs_ref[...].astype(F32)).astype(BF16)

    tile = pl.BlockSpec((None, tr, cn), lambda q, i, core_ref: (q, i, 0))
    return pl.pallas_call(
        body, name=name,
        grid_spec=pltpu.PrefetchScalarGridSpec(
            num_scalar_prefetch=1, grid=(n_chip, r // tr),
            in_specs=[pl.BlockSpec((None, tr, cn), lambda q, i, core_ref: (2 * q + core_ref[0], i, 0)), tile],
            out_specs=tile),
        out_shape=jax.ShapeDtypeStruct(theirs.shape, BF16), compiler_params=_params(),
    )(core, full, theirs)


def _make_gather_op(tag):
    @jax.custom_vjp
    def gather_op(xl):
        return _exchange([xl], True, tag + "_gather")[0]

    def fwd(xl):
        return gather_op(xl), None

    def bwd(_, g):
        return (jnp.sum(_exchange([g], False, tag + "_scatter")[0], axis=0),)

    gather_op.defvjp(fwd, bwd)
    return gather_op


def _adamw(gstack, w, m, v, name):
    s, r, cn = gstack.shape
    tr = _pick(r, max(8, (2 * 1024 * 1024) // (4 * cn) // 8 * 8), 8)
    c1 = 1.0 - ADAM_B1 ** ADAM_STEP
    c2 = 1.0 - ADAM_B2 ** ADAM_STEP

    def body(g_ref, w_ref, m_ref, v_ref, go_ref, d_ref, mo_ref, vo_ref):
        g = g_ref[0].astype(F32)
        for q in range(1, s):
            g = g + g_ref[q].astype(F32)
        m_new = ADAM_B1 * m_ref[...] + (1.0 - ADAM_B1) * g
        v_new = ADAM_B2 * v_ref[...] + (1.0 - ADAM_B2) * (g * g)
        go_ref[...] = g
        mo_ref[...] = m_new
        vo_ref[...] = v_new
        d_ref[...] = -ADAM_LR * ((m_new / c1) / (jnp.sqrt(v_new / c2) + ADAM_EPS) + ADAM_WD * w_ref[...])

    tile = pl.BlockSpec((tr, cn), lambda i: (i, 0))
    out = jax.ShapeDtypeStruct((r, cn), F32)
    return pl.pallas_call(
        body, name=name, grid=(r // tr,),
        in_specs=[pl.BlockSpec((s, tr, cn), lambda i: (0, i, 0)), tile, tile, tile],
        out_specs=[tile, tile, tile, tile], out_shape=[out, out, out, out],
        compiler_params=_params(),
    )(gstack, w, m, v)


def _rope_tables(pos, dim, base):
    inv = base ** (-jnp.arange(0, dim, 2, dtype=F32) / dim)
    ang = pos.astype(F32)[:, None] * inv[None, :]
    return jnp.cos(ang)[:, None, :], jnp.sin(ang)[:, None, :]


def _rotate(x, cos, sin):
    x1, x2 = jnp.split(x, 2, axis=-1)
    return jnp.concatenate([x1 * cos - x2 * sin, x2 * cos + x1 * sin], axis=-1)


def _axial_rope(x, row_tab, col_tab):
    xr, xc = jnp.split(x, 2, axis=-1)
    return jnp.concatenate([_rotate(xr, *row_tab), _rotate(xc, *col_tab)], axis=-1)


def _heads(t, h):
    return jnp.swapaxes(t.reshape(t.shape[0], h, t.shape[1] // h), 0, 1)


def _cols_from_stack(w):
    return jnp.swapaxes(w, 0, 1).reshape(w.shape[1], N_DEV * w.shape[2])


def _local_loss(p, ctx, silu_c_all, tgt, me):
    x = p["x"]
    n_lat, d = x.shape
    n_ctx = ctx.shape[0]
    n_a = p["ada_w"].shape[1]

    a_in = jnp.concatenate([silu_c_all, jax.nn.silu(p["c_ctx"])[None, :], jnp.zeros((7, d), F32)], axis=0)
    b_loc = lax.dynamic_slice(p["ada_b"], (0, me * n_a), (1, n_a))
    r_loc = _make_small_mm("ada")(a_in, p["ada_w"]) + b_loc
    r_full = _make_gather_op("ada")(r_loc)
    m_lat = lax.dynamic_index_in_dim(r_full, me, axis=1, keepdims=False).reshape(N_MOD, 1, d)
    m_ctx = r_full[:, N_DEV, :].reshape(N_MOD, 1, d)

    x1 = _make_ffn_block("ffn1")(x, p["norm1_g"], m_lat[0], m_lat[1], m_lat[2], p["ffn1_w_in"], p["ffn1_w_out"])
    c1 = _make_ffn_block("ffn1c")(ctx, p["norm1_g"], m_ctx[0], m_ctx[1], m_ctx[2], p["ffn1_w_in"], p["ffn1_w_out"])

    w_mix = jnp.pad(_cols_from_stack(p["mix_w_in"]), ((0, 0), (0, MIX_IN_PAD - MIX_IN)))
    proj = _make_norm_proj("mix")(x1, p["norm2_g"], m_lat[3], m_lat[4], w_mix)
    proj_c = _make_norm_proj("mixc")(c1, p["norm2_g"], m_ctx[3], m_ctx[4], w_mix)
    idx = [0]
    for s in SPLITS:
        idx.append(idx[-1] + s)
    rq, rk, rv, rg, cq, ckv, kr = [proj[:, idx[i]:idx[i + 1]] for i in range(7)]
    crk, crv, cckv, ckr = [proj_c[:, idx[i]:idx[i + 1]] for i in (1, 2, 5, 6)]

    zq = jnp.zeros((1, MLA_Q_RANK), F32)
    zkv = jnp.zeros((1, MLA_KV_RANK), F32)
    w_uq = _cols_from_stack(p["mla_w_uq"])
    w_ukv = _cols_from_stack(p["mla_w_ukv"])
    q = _make_norm_proj("uq")(cq, p["mla_q_norm_g"], zq, zq, w_uq)
    kv = _make_norm_proj("ukv")(ckv, p["mla_kv_norm_g"], zkv, zkv, w_ukv)
    kv_c = _make_norm_proj("ukvc")(cckv, p["mla_kv_norm_g"], zkv, zkv, w_ukv)

    lg_f = jax.nn.log_sigmoid(p["ret_decay_fwd"][0])
    lg_b = jax.nn.log_sigmoid(p["ret_decay_bwd"][0])
    ret_tab = _rope_tables(jnp.arange(n_lat), RET_DK, RET_ROPE_BASE)
    rq_h = jnp.swapaxes(_rotate(rq.reshape(n_lat, RET_HEADS, RET_DK), *ret_tab), 0, 1)
    rk_h = jnp.swapaxes(_rotate((rk * (RET_DK ** -0.5)).reshape(n_lat, RET_HEADS, RET_DK), *ret_tab), 0, 1)
    rv_h = _heads(rv, RET_HEADS)
    crk_h = _heads(crk * (RET_DK ** -0.5), RET_HEADS)
    crv_h = _heads(crv, RET_HEADS)
    zero_q = jnp.zeros((RET_HEADS, n_ctx, RET_DK), F32)

    def lanes(lg):
        return jnp.broadcast_to(lg[:, None, None], (RET_HEADS, 1, LANE))

    y_f = _make_ret_dir("retf", False)(
        jnp.concatenate([zero_q, rq_h], axis=1), jnp.concatenate([crk_h, rk_h], axis=1),
        jnp.concatenate([crv_h, rv_h], axis=1), lanes(lg_f))[:, n_ctx:]
    y_b = _make_ret_dir("retb", True)(
        jnp.concatenate([rq_h, zero_q], axis=1), jnp.concatenate([rk_h, crk_h], axis=1),
        jnp.concatenate([rv_h, crv_h], axis=1), lanes(lg_b))[:, :n_lat]
    ret_o = _make_ret_out("reto")(y_f + y_b, rg)

    n_rows = n_lat // GRID_W
    row_tab = _rope_tables(jnp.repeat(jnp.arange(n_rows), GRID_W), MLA_ROPE // 2, AXIAL_BASE)
    col_tab = _rope_tables(jnp.tile(jnp.arange(GRID_W), n_rows), MLA_ROPE // 2, AXIAL_BASE)
    q3 = q.reshape(n_lat, MLA_HEADS, MLA_NOPE + MLA_ROPE)
    q_all = jnp.concatenate([q3[..., :MLA_NOPE], _axial_rope(q3[..., MLA_NOPE:], row_tab, col_tab)], axis=-1)
    kv3 = kv.reshape(n_lat, MLA_HEADS, MLA_NOPE + MLA_V)
    kvc3 = kv_c.reshape(n_ctx, MLA_HEADS, MLA_NOPE + MLA_V)
    kr_rot = _axial_rope(kr[:, None, :], row_tab, col_tab)
    k_lat = jnp.concatenate([kv3[..., :MLA_NOPE], jnp.broadcast_to(kr_rot, (n_lat, MLA_HEADS, MLA_ROPE))], axis=-1)
    k_ctx = jnp.concatenate(
        [kvc3[..., :MLA_NOPE], jnp.broadcast_to(ckr[:, None, :], (n_ctx, MLA_HEADS, MLA_ROPE))], axis=-1)
    k_all = jnp.concatenate([k_lat, k_ctx], axis=0)
    v_all = jnp.concatenate([kv3[..., MLA_NOPE:], kvc3[..., MLA_NOPE:]], axis=0)
    mla_o = _make_attention("mla")(
        jnp.swapaxes(q_all, 0, 1).astype(BF16), jnp.swapaxes(k_all, 0, 1).astype(BF16),
        jnp.swapaxes(v_all, 0, 1).astype(BF16))

    w_mo = p["mix_w_out"].reshape(-1, d)
    x2 = _make_res_proj("mixo")(jnp.concatenate([ret_o, mla_o], axis=-1), w_mo, x1, m_lat[5])
    x3 = _make_ffn_block("ffn2")(x2, p["norm3_g"], m_lat[6], m_lat[7], m_lat[8], p["ffn2_w_in"], p["ffn2_w_out"])
    return _make_final_loss("loss")(x3, p["final_norm_g"], tgt)


BIG = ("ffn1_w_in", "ffn1_w_out", "mix_w_in", "mla_w_uq", "mla_w_ukv", "mix_w_out", "ffn2_w_in", "ffn2_w_out")
SMALL = ("c_ctx", "ada_b", "norm1_g", "norm2_g", "ret_decay_fwd", "ret_decay_bwd", "mla_q_norm_g",
         "mla_kv_norm_g", "norm3_g", "final_norm_g")
WEIGHTS = ("c_ctx", "ada_w", "ada_b", "norm1_g", "ffn1_w_in", "ffn1_w_out", "norm2_g", "mix_w_in", "ret_decay_fwd",
           "ret_decay_bwd", "mla_q_norm_g", "mla_w_uq", "mla_kv_norm_g", "mla_w_ukv", "mix_w_out", "norm3_g",
           "ffn2_w_in", "ffn2_w_out", "final_norm_g")


def _pack(parts):
    flat = jnp.concatenate([t.reshape(-1) for t in parts])
    pad = (-flat.shape[0]) % LANE
    return jnp.pad(flat, (0, pad)).reshape(1, -1)


def _unpack(flat, like):
    out, off = [], 0
    for t in like:
        out.append(flat[0, off:off + t.size].reshape(t.shape))
        off += t.size
    return out


def kernel(x, c, ctx, c_ctx, ada_w, ada_b, norm1_g, ffn1_w_in, ffn1_w_out, norm2_g, mix_w_in, ret_decay_fwd, ret_decay_bwd, mla_q_norm_g, mla_w_uq, mla_kv_norm_g, mla_w_ukv, mix_w_out, norm3_g, ffn2_w_in, ffn2_w_out, final_norm_g, loss_target, m_c_ctx, m_ada_w, m_ada_b, m_norm1_g, m_ffn1_w_in, m_ffn1_w_out, m_norm2_g, m_mix_w_in, m_ret_decay_fwd, m_ret_decay_bwd, m_mla_q_norm_g, m_mla_w_uq, m_mla_kv_norm_g, m_mla_w_ukv, m_mix_w_out, m_norm3_g, m_ffn2_w_in, m_ffn2_w_out, m_final_norm_g, v_c_ctx, v_ada_w, v_ada_b, v_norm1_g, v_ffn1_w_in, v_ffn1_w_out, v_norm2_g, v_mix_w_in, v_ret_decay_fwd, v_ret_decay_bwd, v_mla_q_norm_g, v_mla_w_uq, v_mla_kv_norm_g, v_mla_w_ukv, v_mix_w_out, v_norm3_g, v_ffn2_w_in, v_ffn2_w_out, v_final_norm_g):
    w = dict(c_ctx=c_ctx, ada_w=ada_w, ada_b=ada_b, norm1_g=norm1_g, ffn1_w_in=ffn1_w_in, ffn1_w_out=ffn1_w_out,
             norm2_g=norm2_g, mix_w_in=mix_w_in, ret_decay_fwd=ret_decay_fwd, ret_decay_bwd=ret_decay_bwd,
             mla_q_norm_g=mla_q_norm_g, mla_w_uq=mla_w_uq, mla_kv_norm_g=mla_kv_norm_g, mla_w_ukv=mla_w_ukv,
             mix_w_out=mix_w_out, norm3_g=norm3_g, ffn2_w_in=ffn2_w_in, ffn2_w_out=ffn2_w_out,
             final_norm_g=final_norm_g)
    mom = dict(c_ctx=m_c_ctx, ada_w=m_ada_w, ada_b=m_ada_b, norm1_g=m_norm1_g, ffn1_w_in=m_ffn1_w_in,
               ffn1_w_out=m_ffn1_w_out, norm2_g=m_norm2_g, mix_w_in=m_mix_w_in, ret_decay_fwd=m_ret_decay_fwd,
               ret_decay_bwd=m_ret_decay_bwd, mla_q_norm_g=m_mla_q_norm_g, mla_w_uq=m_mla_w_uq,
               mla_kv_norm_g=m_mla_kv_norm_g, mla_w_ukv=m_mla_w_ukv, mix_w_out=m_mix_w_out, norm3_g=m_norm3_g,
               ffn2_w_in=m_ffn2_w_in, ffn2_w_out=m_ffn2_w_out, final_norm_g=m_final_norm_g)
    var = dict(c_ctx=v_c_ctx, ada_w=v_ada_w, ada_b=v_ada_b, norm1_g=v_norm1_g, ffn1_w_in=v_ffn1_w_in,
               ffn1_w_out=v_ffn1_w_out, norm2_g=v_norm2_g, mix_w_in=v_mix_w_in, ret_decay_fwd=v_ret_decay_fwd,
               ret_decay_bwd=v_ret_decay_bwd, mla_q_norm_g=v_mla_q_norm_g, mla_w_uq=v_mla_w_uq,
               mla_kv_norm_g=v_mla_kv_norm_g, mla_w_ukv=v_mla_w_ukv, mix_w_out=v_mix_w_out, norm3_g=v_norm3_g,
               ffn2_w_in=v_ffn2_w_in, ffn2_w_out=v_ffn2_w_out, final_norm_g=v_final_norm_g)
    me = 4 * lax.axis_index("x") + 2 * lax.axis_index("y") + lax.axis_index("c")

    gathered = _gather_two_level([w[k][0].astype(BF16) for k in BIG] + [jax.nn.silu(c)], "weights_gather")
    silu_c_all = gathered[-1][:, 0, :]

    p = dict(zip(BIG, gathered[:-1]))
    p["x"] = x[0]
    p["ada_w"] = ada_w[0]
    p["c_ctx"] = c_ctx
    p["ada_b"] = ada_b
    p["final_norm_g"] = final_norm_g[None, :]
    for k in ("norm1_g", "norm2_g", "norm3_g", "mla_q_norm_g", "mla_kv_norm_g", "ret_decay_fwd", "ret_decay_bwd"):
        p[k] = w[k]

    loss_local, grads = jax.value_and_grad(_local_loss)(p, ctx[0], silu_c_all, loss_target[0], me)
    grads["final_norm_g"] = grads["final_norm_g"][0]

    core = lax.axis_index("c").astype(jnp.int32).reshape(1)
    full = [grads[k] for k in BIG]
    theirs = _swap_sibling(full, "grads_swap")
    paired = [_pair_add(f, t, core, "grads_pair_" + k) for k, f, t in zip(BIG, full, theirs)]
    exchanged = _scatter_chips(paired, "grads_scatter")
    zero1 = [jnp.zeros((1,), F32)]
    small_like = zero1 + [w[k] for k in SMALL]
    small_all = _exchange([_pack([loss_local.reshape(1)] + [grads[k] for k in SMALL])], True, "small_grads_gather")[0]
    loss = jnp.sum(small_all[:, 0, 0])

    out_g, out_d, out_m, out_v = {}, {}, {}, {}

    def update(name, gstack, shape2d):
        res = _adamw(gstack, w[name].reshape(shape2d), mom[name].reshape(shape2d), var[name].reshape(shape2d),
                     "adamw_" + name)
        out_g[name], out_d[name], out_m[name], out_v[name] = [t.reshape(w[name].shape) for t in res]

    for k, gs in zip(BIG, exchanged):
        update(k, gs, gs.shape[1:])
    update("ada_w", grads["ada_w"][None], ada_w.shape[1:])
    res = _adamw(small_all, _pack(small_like), _pack(zero1 + [mom[k] for k in SMALL]),
                 _pack(zero1 + [var[k] for k in SMALL]), "adamw_small")
    for dst, flat in zip((out_g, out_d, out_m, out_v), res):
        for k, t in zip(SMALL, _unpack(flat, small_like)[1:]):
            dst[k] = t

    return (loss, grads["x"][None], *[out_g[k] for k in WEIGHTS], *[out_d[k] for k in WEIGHTS],
            *[out_m[k] for k in WEIGHTS], *[out_v[k] for k in WEIGHTS])
```

```python
import functools

import jax
import jax.numpy as jnp
from jax import lax
from jax.experimental import pallas as pl
from jax.experimental.pallas import tpu as pltpu

F32 = jnp.float32
BF16 = jnp.bfloat16

N_DEV = 8
MESH_AXES = ("x", "y", "c")

GRID_W = 64
N_MOD = 9
RET_HEADS = 8
RET_DK = 64
RET_DV = 128
RET_CHUNK = 256
RET_ROPE_BASE = 10000.0
MLA_HEADS = 8
MLA_Q_RANK = 512
MLA_KV_RANK = 256
MLA_NOPE = 128
MLA_ROPE = 64
MLA_V = 128
AXIAL_BASE = 10000.0
RMS_EPS = 1e-6
GN_EPS = 1e-5
SPLITS = (RET_HEADS * RET_DK, RET_HEADS * RET_DK, RET_HEADS * RET_DV, RET_HEADS * RET_DV,
          MLA_Q_RANK, MLA_KV_RANK, MLA_ROPE)
MIX_IN = sum(SPLITS)
MIX_IN_PAD = 4096

ADAM_LR = 0.001
ADAM_B1 = 0.9
ADAM_B2 = 0.999
ADAM_EPS = 1e-08
ADAM_WD = 0.01
ADAM_STEP = 10

LANE = 128
VMEM_LIMIT_BYTES = 56 * 1024 * 1024

NN = ((1,), (0,))
NT = ((1,), (1,))
TN = ((0,), (0,))


def _pick(dim, target, align=LANE):
    t = min(dim, target)
    t -= t % align
    while t >= align:
        if dim % t == 0:
            return t
        t -= align
    return dim


def _params():
    return pltpu.CompilerParams(vmem_limit_bytes=VMEM_LIMIT_BYTES)


def _dot(a, b, dims):
    return lax.dot_general(a.astype(BF16), b.astype(BF16), (dims, ((), ())), preferred_element_type=F32)


def _mm_call(name, grid, ins, pairs, outs, acc_shapes, epilogue):
    n_in, n_out = len(ins), len(outs)
    k_axis = len(grid) - 1
    k_steps = grid[k_axis]

    def body(*refs):
        in_refs = refs[:n_in]
        out_refs = refs[n_in:n_in + n_out]
        accs = refs[n_in + n_out:]
        k = pl.program_id(k_axis)

        @pl.when(k == 0)
        def _():
            for acc in accs:
                acc[...] = jnp.zeros_like(acc)

        for ai, bi, dims, ci in pairs:
            accs[ci][...] += _dot(in_refs[ai][...], in_refs[bi][...], dims)

        @pl.when(k == k_steps - 1)
        def _():
            epilogue([acc[...] for acc in accs], in_refs, out_refs)

    res = pl.pallas_call(
        body, name=name, grid=grid,
        in_specs=[s for _, s in ins], out_specs=[s for _, s in outs],
        out_shape=[s for s, _ in outs],
        scratch_shapes=[pltpu.VMEM(s, F32) for s in acc_shapes],
        compiler_params=_params(),
    )(*[a for a, _ in ins])
    return res


def _matmul(a, b, mode, out_dtype, name, tm=1024, tn=1024, tk=512):
    if mode == "nn":
        (m, kd), n = a.shape, b.shape[1]
    elif mode == "nt":
        (m, kd), n = a.shape, b.shape[0]
    else:
        (kd, m), n = a.shape, b.shape[1]
    tm, tn = _pick(m, tm, 16), _pick(n, tn)
    tk = _pick(kd, tk) if mode != "tn" else _pick(kd, tk, 16)
    if mode == "nn":
        a_spec = pl.BlockSpec((tm, tk), lambda i, j, k: (i, k))
        b_spec = pl.BlockSpec((tk, tn), lambda i, j, k: (k, j))
        dims = NN
    elif mode == "nt":
        a_spec = pl.BlockSpec((tm, tk), lambda i, j, k: (i, k))
        b_spec = pl.BlockSpec((tn, tk), lambda i, j, k: (j, k))
        dims = NT
    else:
        a_spec = pl.BlockSpec((tk, tm), lambda i, j, k: (k, i))
        b_spec = pl.BlockSpec((tk, tn), lambda i, j, k: (k, j))
        dims = TN

    def epilogue(accs, in_refs, out_refs):
        out_refs[0][...] = accs[0].astype(out_dtype)

    return _mm_call(
        name, (m // tm, n // tn, kd // tk), [(a, a_spec), (b, b_spec)], [(0, 1, dims, 0)],
        [(jax.ShapeDtypeStruct((m, n), out_dtype), pl.BlockSpec((tm, tn), lambda i, j, k: (i, j)))],
        [(tm, tn)], epilogue)[0]


def _norm_mod_tile(x, ng, sc, sh):
    r = lax.rsqrt(jnp.mean(x * x, axis=-1, keepdims=True) + RMS_EPS)
    return (x * r * ng) * (1.0 + sc) + sh


def _row_spec(tm, d):
    return pl.BlockSpec((tm, d), lambda i: (i, 0))


def _vec_spec(d):
    return pl.BlockSpec((1, d), lambda i: (0, 0))


def _norm_mod_fwd(x, ng, sc, sh, name):
    t, d = x.shape
    tm = _pick(t, 512, 16)

    def body(x_ref, ng_ref, sc_ref, sh_ref, h_ref):
        h_ref[...] = _norm_mod_tile(x_ref[...], ng_ref[...], sc_ref[...], sh_ref[...]).astype(BF16)

    return pl.pallas_call(
        body, name=name, grid=(t // tm,),
        in_specs=[_row_spec(tm, d), _vec_spec(d), _vec_spec(d), _vec_spec(d)],
        out_specs=_row_spec(tm, d), out_shape=jax.ShapeDtypeStruct((t, d), BF16),
        compiler_params=_params(),
    )(x, ng, sc, sh)


def _norm_mod_bwd(x, ng, sc, sh, dh, dres, name):
    t, d = x.shape
    tm = _pick(t, 256, 16)
    has_res = dres is not None

    def body(*refs):
        if has_res:
            x_ref, ng_ref, sc_ref, sh_ref, dh_ref, dres_ref, dx_ref, dng_ref, dsc_ref, dsh_ref = refs
        else:
            x_ref, ng_ref, sc_ref, sh_ref, dh_ref, dx_ref, dng_ref, dsc_ref, dsh_ref = refs
        _, vjp = jax.vjp(_norm_mod_tile, x_ref[...], ng_ref[...], sc_ref[...], sh_ref[...])
        dx, dng, dsc, dsh = vjp(dh_ref[...].astype(F32))
        if has_res:
            dx = dx + dres_ref[...]
        dx_ref[...] = dx

        @pl.when(pl.program_id(0) == 0)
        def _():
            dng_ref[...] = jnp.zeros_like(dng_ref)
            dsc_ref[...] = jnp.zeros_like(dsc_ref)
            dsh_ref[...] = jnp.zeros_like(dsh_ref)

        dng_ref[...] += dng
        dsc_ref[...] += dsc
        dsh_ref[...] += dsh

    ins = [x, ng, sc, sh, dh] + ([dres] if has_res else [])
    in_specs = [_row_spec(tm, d), _vec_spec(d), _vec_spec(d), _vec_spec(d), _row_spec(tm, d)]
    in_specs += [_row_spec(tm, d)] if has_res else []
    vec = jax.ShapeDtypeStruct((1, d), F32)
    return pl.pallas_call(
        body, name=name, grid=(t // tm,), in_specs=in_specs,
        out_specs=[_row_spec(tm, d), _vec_spec(d), _vec_spec(d), _vec_spec(d)],
        out_shape=[jax.ShapeDtypeStruct((t, d), F32), vec, vec, vec],
        compiler_params=_params(),
    )(*ins)


def _res_mm_fwd(a, w, x, gate, coef, name):
    t, kd = a.shape
    d = w.shape[1]
    tm, tn, tk = _pick(t, 1024, 16), _pick(d, 1024), _pick(kd, 2816)

    def epilogue(accs, in_refs, out_refs):
        f = accs[0]
        out_refs[0][...] = in_refs[2][...] + (coef * in_refs[3][...]) * f
        out_refs[1][...] = f.astype(BF16)

    tile = pl.BlockSpec((tm, tn), lambda i, j, k: (i, j))
    return _mm_call(
        name, (t // tm, d // tn, kd // tk),
        [(a, pl.BlockSpec((tm, tk), lambda i, j, k: (i, k))), (w, pl.BlockSpec((tk, tn), lambda i, j, k: (k, j))),
         (x, tile), (gate, pl.BlockSpec((1, tn), lambda i, j, k: (0, j)))],
        [(0, 1, NN, 0)],
        [(jax.ShapeDtypeStruct((t, d), F32), tile), (jax.ShapeDtypeStruct((t, d), BF16), tile)],
        [(tm, tn)], epilogue)


def _gate_bwd(dxo, f, gate, coef, name):
    t, d = dxo.shape
    tm = _pick(t, 512, 16)

    def body(dxo_ref, f_ref, gate_ref, df_ref, dgate_ref):
        dxo_t = dxo_ref[...]
        df_ref[...] = ((coef * gate_ref[...]) * dxo_t).astype(BF16)

        @pl.when(pl.program_id(0) == 0)
        def _():
            dgate_ref[...] = jnp.zeros_like(dgate_ref)

        dgate_ref[...] += coef * jnp.sum(dxo_t * f_ref[...].astype(F32), axis=0, keepdims=True)

    return pl.pallas_call(
        body, name=name, grid=(t // tm,),
        in_specs=[_row_spec(tm, d), _row_spec(tm, d), _vec_spec(d)],
        out_specs=[_row_spec(tm, d), _vec_spec(d)],
        out_shape=[jax.ShapeDtypeStruct((t, d), BF16), jax.ShapeDtypeStruct((1, d), F32)],
        compiler_params=_params(),
    )(dxo, f, gate)


def _ffn_in_fwd(h, w_in, name):
    t, d = h.shape
    n = w_in.shape[2]
    half = N_DEV // 2
    f = half * n
    tm = _pick(t, 512, 16)

    def epilogue(accs, in_refs, out_refs):
        g, u = accs
        out_refs[0][...] = (g * jax.nn.sigmoid(g) * u).astype(BF16)
        out_refs[1][0] = g.astype(BF16)
        out_refs[1][1] = u.astype(BF16)

    return _mm_call(
        name, (half, t // tm, 1),
        [(h, pl.BlockSpec((tm, d), lambda j, i, k: (i, 0))),
         (w_in, pl.BlockSpec((None, d, n), lambda j, i, k: (j, 0, 0))),
         (w_in, pl.BlockSpec((None, d, n), lambda j, i, k: (j + half, 0, 0)))],
        [(0, 1, NN, 0), (0, 2, NN, 1)],
        [(jax.ShapeDtypeStruct((t, f), BF16), pl.BlockSpec((tm, n), lambda j, i, k: (i, j))),
         (jax.ShapeDtypeStruct((2, t, f), BF16), pl.BlockSpec((2, tm, n), lambda j, i, k: (0, i, j)))],
        [(tm, n), (tm, n)], epilogue)


def _ffn_da_bwd(df, w_out2d, gu, name):
    t, d = df.shape
    f = w_out2d.shape[0]
    half = N_DEV // 2
    n = f // half
    tm = _pick(t, 512, 16)

    def epilogue(accs, in_refs, out_refs):
        da = accs[0]
        g = in_refs[2][0].astype(F32)
        u = in_refs[2][1].astype(F32)
        s = jax.nn.sigmoid(g)
        out_refs[0][0] = (da * u * (s * (1.0 + g * (1.0 - s)))).astype(BF16)
        out_refs[0][1] = (da * (g * s)).astype(BF16)

    gu_spec = pl.BlockSpec((2, tm, n), lambda j, i, k: (0, i, j))
    return _mm_call(
        name, (half, t // tm, 1),
        [(df, pl.BlockSpec((tm, d), lambda j, i, k: (i, 0))),
         (w_out2d, pl.BlockSpec((n, d), lambda j, i, k: (j, 0))),
         (gu, gu_spec)],
        [(0, 1, NT, 0)],
        [(jax.ShapeDtypeStruct((2, t, f), BF16), gu_spec)],
        [(tm, n)], epilogue)[0]


def _ffn_dh_bwd(dgu, w_in, name):
    _, t, f = dgu.shape
    d, n = w_in.shape[1], w_in.shape[2]
    half = N_DEV // 2
    tm = _pick(t, 512, 16)

    def epilogue(accs, in_refs, out_refs):
        out_refs[0][...] = accs[0]

    return _mm_call(
        name, (t // tm, 1, half),
        [(dgu, pl.BlockSpec((None, tm, n), lambda i, j, k: (0, i, k))),
         (dgu, pl.BlockSpec((None, tm, n), lambda i, j, k: (1, i, k))),
         (w_in, pl.BlockSpec((None, d, n), lambda i, j, k: (k, 0, 0))),
         (w_in, pl.BlockSpec((None, d, n), lambda i, j, k: (k + half, 0, 0)))],
        [(0, 2, NT, 0), (1, 3, NT, 0)],
        [(jax.ShapeDtypeStruct((t, d), F32), pl.BlockSpec((tm, d), lambda i, j, k: (i, 0)))],
        [(tm, d)], epilogue)[0]


def _ffn_dwin_bwd(h, dgu, name):
    t, d = h.shape
    f = dgu.shape[2]
    half = N_DEV // 2
    n = f // half
    tk = _pick(t, 1024, 16)

    def epilogue(accs, in_refs, out_refs):
        out_refs[0][...] = accs[0].astype(BF16)

    return _mm_call(
        name, (N_DEV, 1, t // tk),
        [(h, pl.BlockSpec((tk, d), lambda j, i, k: (k, 0))),
         (dgu, pl.BlockSpec((None, tk, n), lambda j, i, k: (j // half, k, j % half)))],
        [(0, 1, TN, 0)],
        [(jax.ShapeDtypeStruct((N_DEV, d, n), BF16), pl.BlockSpec((None, d, n), lambda j, i, k: (j, 0, 0)))],
        [(d, n)], epilogue)[0]


def _make_ffn_block(tag):
    @jax.custom_vjp
    def ffn_block(x, ng, sh, sc, gate, w_in, w_out):
        return fwd(x, ng, sh, sc, gate, w_in, w_out)[0]

    def fwd(x, ng, sh, sc, gate, w_in, w_out):
        f = w_out.shape[0] * w_out.shape[1]
        w_out2d = w_out.reshape(f, w_out.shape[2])
        h = _norm_mod_fwd(x, ng, sc, sh, tag + "_norm")
        a, gu = _ffn_in_fwd(h, w_in, tag + "_in")
        xo, f1 = _res_mm_fwd(a, w_out2d, x, gate, 0.5, tag + "_out")
        return xo, (x, ng, sh, sc, gate, w_in, w_out, h, a, gu, f1)

    def bwd(res, dxo):
        x, ng, sh, sc, gate, w_in, w_out, h, a, gu, f1 = res
        f = w_out.shape[0] * w_out.shape[1]
        w_out2d = w_out.reshape(f, w_out.shape[2])
        df, dgate = _gate_bwd(dxo, f1, gate, 0.5, tag + "_dgate")
        dgu = _ffn_da_bwd(df, w_out2d, gu, tag + "_da")
        dw_out = _matmul(a, df, "tn", BF16, tag + "_dwout", tm=_pick(f, 1408, 16), tn=2048, tk=1024)
        dh = _ffn_dh_bwd(dgu, w_in, tag + "_dh")
        dw_in = _ffn_dwin_bwd(h, dgu, tag + "_dwin")
        dx, dng, dsc, dsh = _norm_mod_bwd(x, ng, sc, sh, dh, dxo, tag + "_dnorm")
        return dx, dng, dsh, dsc, dgate, dw_in, dw_out.reshape(w_out.shape)

    ffn_block.defvjp(fwd, bwd)
    return ffn_block


def _make_norm_proj(tag):
    @jax.custom_vjp
    def norm_proj(x, ng, sh, sc, w):
        return fwd(x, ng, sh, sc, w)[0]

    def fwd(x, ng, sh, sc, w):
        h = _norm_mod_fwd(x, ng, sc, sh, tag + "_norm")
        p = _matmul(h, w, "nn", F32, tag + "_mm", tm=1024, tn=1024, tk=w.shape[0])
        return p, (x, ng, sh, sc, w, h)

    def bwd(res, dp):
        x, ng, sh, sc, w, h = res
        dh = _matmul(dp, w, "nt", F32, tag + "_dh", tm=512, tn=w.shape[0], tk=2048)
        dw = _matmul(h, dp, "tn", BF16, tag + "_dw", tm=w.shape[0], tn=1024, tk=1024)
        dx, dng, dsc, dsh = _norm_mod_bwd(x, ng, sc, sh, dh, None, tag + "_dnorm")
        return dx, dng, dsh, dsc, dw

    norm_proj.defvjp(fwd, bwd)
    return norm_proj


def _make_res_proj(tag):
    @jax.custom_vjp
    def res_proj(a, w, x, gate):
        return fwd(a, w, x, gate)[0]

    def fwd(a, w, x, gate):
        xo, f = _res_mm_fwd(a, w, x, gate, 1.0, tag + "_mm")
        return xo, (a, w, gate, f)

    def bwd(res, dxo):
        a, w, gate, f = res
        df, dgate = _gate_bwd(dxo, f, gate, 1.0, tag + "_dgate")
        da = _matmul(df, w, "nt", BF16, tag + "_da", tm=1024, tn=1024, tk=2048)
        dw = _matmul(a, df, "tn", BF16, tag + "_dw", tm=1024, tn=2048, tk=1024)
        return da, dw, dxo, dgate

    res_proj.defvjp(fwd, bwd)
    return res_proj


def _make_small_mm(tag):
    @jax.custom_vjp
    def small_mm(a, w):
        return _matmul(a, w, "nn", F32, tag + "_mm", tm=a.shape[0], tn=768, tk=w.shape[0])

    def fwd(a, w):
        return small_mm(a, w), (a, w)

    def bwd(res, dr):
        a, w = res
        da = _matmul(dr, w, "nt", F32, tag + "_da", tm=a.shape[0], tn=w.shape[0], tk=768)
        dw = _matmul(a, dr, "tn", F32, tag + "_dw", tm=1024, tn=768, tk=a.shape[0])
        return da, dw

    small_mm.defvjp(fwd, bwd)
    return small_mm


def _ret_chunk_terms(lg, c, reverse):
    row = lax.broadcasted_iota(jnp.int32, (c, c), 0).astype(F32)
    col = lax.broadcasted_iota(jnp.int32, (c, c), 1).astype(F32)
    pos = lax.broadcasted_iota(jnp.int32, (c, 1), 0).astype(F32)
    if reverse:
        diff = col - row
        mask = diff > 0.0
        e_exp = float(c) - pos
        f_exp = pos
    else:
        diff = row - col
        mask = diff >= 0.0
        e_exp = pos + 1.0
        f_exp = float(c - 1) - pos
    diffm = jnp.where(mask, diff, 0.0)
    dm = jnp.where(mask, jnp.exp(lg * diffm), 0.0)
    return diffm, dm, e_exp, jnp.exp(lg * e_exp), f_exp, jnp.exp(lg * f_exp)


def _lane0(val):
    lane = lax.broadcasted_iota(jnp.int32, (1, LANE), 1)
    return jnp.where(lane == 0, val, 0.0)


RET_HEAD_BLOCK = 4


def _make_ret_dir(tag, reverse):
    hb = RET_HEAD_BLOCK

    def heads_spec(nc, width, flip):
        if flip:
            return pl.BlockSpec((hb, RET_CHUNK, width), lambda h, t: (h, nc - 1 - t, 0))
        return pl.BlockSpec((hb, RET_CHUNK, width), lambda h, t: (h, t, 0))

    def state_spec(nc, flip):
        if flip:
            return pl.BlockSpec((hb, None, RET_DK, RET_DV), lambda h, t: (h, nc - 1 - t, 0, 0))
        return pl.BlockSpec((hb, None, RET_DK, RET_DV), lambda h, t: (h, t, 0, 0))

    lg_spec = pl.BlockSpec((hb, 1, LANE), lambda h, t: (h, 0, 0))

    def fwd_call(q, k, v, lgb):
        hh, ll, _ = q.shape
        c = RET_CHUNK
        nc = ll // c

        def body(q_ref, k_ref, v_ref, lg_ref, y_ref, sall_ref, s_scr):
            @pl.when(pl.program_id(1) == 0)
            def _():
                s_scr[...] = jnp.zeros_like(s_scr)

            for b in range(hb):
                lg = lg_ref[b][:, :1]
                _, dm, _, xi, _, zeta = _ret_chunk_terms(lg, c, reverse)
                q_t, k_t, v_t = q_ref[b], k_ref[b], v_ref[b]
                s = s_scr[b]
                p = _dot(q_t, k_t, NT) * dm
                y_ref[b] = _dot(p, v_t, NN) + _dot(q_t * xi, s, NN)
                sall_ref[b] = s
                s_scr[b] = jnp.exp(lg * float(c)) * s + _dot(k_t * zeta, v_t, TN)

        return pl.pallas_call(
            body, name=tag + "_fwd", grid=(hh // hb, nc),
            in_specs=[heads_spec(nc, RET_DK, reverse), heads_spec(nc, RET_DK, reverse),
                      heads_spec(nc, RET_DV, reverse), lg_spec],
            out_specs=[heads_spec(nc, RET_DV, reverse), state_spec(nc, reverse)],
            out_shape=[jax.ShapeDtypeStruct((hh, ll, RET_DV), F32),
                       jax.ShapeDtypeStruct((hh, nc, RET_DK, RET_DV), F32)],
            scratch_shapes=[pltpu.VMEM((hb, RET_DK, RET_DV), F32)],
            compiler_params=_params(),
        )(q, k, v, lgb)

    def bwd_call(q, k, v, lgb, sall, dy):
        hh, ll, _ = q.shape
        c = RET_CHUNK
        nc = ll // c
        flip = not reverse

        def body(q_ref, k_ref, v_ref, lg_ref, sall_ref, dy_ref, dq_ref, dk_ref, dv_ref, dlg_ref, ds_scr):
            @pl.when(pl.program_id(1) == 0)
            def _():
                ds_scr[...] = jnp.zeros_like(ds_scr)
                dlg_ref[...] = jnp.zeros_like(dlg_ref)

            def total(m):
                return jnp.sum(jnp.sum(m, axis=1, keepdims=True), axis=0, keepdims=True)

            for b in range(hb):
                lg = lg_ref[b][:, :1]
                diffm, dm, e_exp, xi, f_exp, zeta = _ret_chunk_terms(lg, c, reverse)
                q_t, k_t, v_t, dy_t = q_ref[b], k_ref[b], v_ref[b], dy_ref[b]
                s = sall_ref[b]
                dsn = ds_scr[b]
                a = _dot(q_t, k_t, NT)
                da = _dot(dy_t, v_t, NT) * dm
                g = _dot(dy_t, s, NT)
                hm = _dot(v_t, dsn, NT)
                dq_ref[b] = _dot(da, k_t, NN) + xi * g
                dk_ref[b] = _dot(da, q_t, TN) + zeta * hm
                dv_ref[b] = _dot(a * dm, dy_t, TN) + _dot(k_t * zeta, dsn, NN)
                gc = jnp.exp(lg * float(c))
                ds_scr[b] = gc * dsn + _dot(q_t * xi, dy_t, TN)
                dl = (total(da * a * diffm) + total(e_exp * xi * q_t * g)
                      + float(c) * gc * total(s * dsn) + total(f_exp * zeta * k_t * hm))
                dlg_ref[b] += _lane0(dl)

        return pl.pallas_call(
            body, name=tag + "_bwd", grid=(hh // hb, nc),
            in_specs=[heads_spec(nc, RET_DK, flip), heads_spec(nc, RET_DK, flip), heads_spec(nc, RET_DV, flip),
                      lg_spec, state_spec(nc, flip), heads_spec(nc, RET_DV, flip)],
            out_specs=[heads_spec(nc, RET_DK, flip), heads_spec(nc, RET_DK, flip), heads_spec(nc, RET_DV, flip),
                       lg_spec],
            out_shape=[jax.ShapeDtypeStruct((hh, ll, RET_DK), F32), jax.ShapeDtypeStruct((hh, ll, RET_DK), F32),
                       jax.ShapeDtypeStruct((hh, ll, RET_DV), F32), jax.ShapeDtypeStruct((hh, 1, LANE), F32)],
            scratch_shapes=[pltpu.VMEM((hb, RET_DK, RET_DV), F32)],
            compiler_params=_params(),
        )(q, k, v, lgb, sall, dy)

    @jax.custom_vjp
    def ret_dir(q, k, v, lgb):
        return fwd_call(q, k, v, lgb)[0]

    def fwd(q, k, v, lgb):
        y, sall = fwd_call(q, k, v, lgb)
        return y, (q, k, v, lgb, sall)

    def bwd(res, dy):
        q, k, v, lgb, sall = res
        dq, dk, dv, dlg = bwd_call(q, k, v, lgb, sall, dy)
        return dq, dk, dv, dlg

    ret_dir.defvjp(fwd, bwd)
    return ret_dir


def _ret_out_tile(y, g):
    mu = jnp.mean(y, axis=-1, keepdims=True)
    var = jnp.mean(jnp.square(y - mu), axis=-1, keepdims=True)
    return (g * jax.nn.sigmoid(g)) * ((y - mu) * lax.rsqrt(var + GN_EPS))


def _make_ret_out(tag):
    def specs(tm):
        y_spec = pl.BlockSpec((None, tm, RET_DV), lambda h, i: (h, i, 0))
        g_spec = pl.BlockSpec((tm, RET_DV), lambda h, i: (i, h))
        return y_spec, g_spec

    def fwd_call(y, g):
        hh, n, _ = y.shape
        tm = _pick(n, 1024, 16)
        y_spec, g_spec = specs(tm)

        def body(y_ref, g_ref, o_ref):
            o_ref[...] = _ret_out_tile(y_ref[...], g_ref[...]).astype(BF16)

        return pl.pallas_call(
            body, name=tag + "_fwd", grid=(hh, n // tm), in_specs=[y_spec, g_spec], out_specs=g_spec,
            out_shape=jax.ShapeDtypeStruct((n, hh * RET_DV), BF16), compiler_params=_params(),
        )(y, g)

    def bwd_call(y, g, do):
        hh, n, _ = y.shape
        tm = _pick(n, 1024, 16)
        y_spec, g_spec = specs(tm)

        def body(y_ref, g_ref, do_ref, dy_ref, dg_ref):
            _, vjp = jax.vjp(_ret_out_tile, y_ref[...], g_ref[...])
            dy, dg = vjp(do_ref[...].astype(F32))
            dy_ref[...] = dy
            dg_ref[...] = dg

        return pl.pallas_call(
            body, name=tag + "_bwd", grid=(hh, n // tm), in_specs=[y_spec, g_spec, g_spec],
            out_specs=[y_spec, g_spec],
            out_shape=[jax.ShapeDtypeStruct(y.shape, F32), jax.ShapeDtypeStruct(g.shape, F32)],
            compiler_params=_params(),
        )(y, g, do)

    @jax.custom_vjp
    def ret_out(y, g):
        return fwd_call(y, g)

    def fwd(y, g):
        return fwd_call(y, g), (y, g)

    def bwd(res, do):
        y, g = res
        return tuple(bwd_call(y, g, do))

    ret_out.defvjp(fwd, bwd)
    return ret_out


MLA_DQ_PAD = 2 * LANE
MLA_PACK_ROWS = 256


def _rope128(x, cos, s1, s2):
    return x * cos + pltpu.roll(x, LANE - 16, 1) * s1 + pltpu.roll(x, 16, 1) * s2


def _rope128_t(g, cos, s1, s2):
    return g * cos + pltpu.roll(g * s1, 16, 1) + pltpu.roll(g * s2, LANE - 16, 1)


def _axial_tables(n_lat):
    n_rows = n_lat // GRID_W
    half = MLA_ROPE // 2
    inv = AXIAL_BASE ** (-jnp.arange(0, half, 2, dtype=F32) / half)
    ang_r = jnp.repeat(jnp.arange(n_rows), GRID_W).astype(F32)[:, None] * inv[None, :]
    ang_c = jnp.tile(jnp.arange(GRID_W), n_rows).astype(F32)[:, None] * inv[None, :]
    zero = jnp.zeros_like(ang_r)
    cos = jnp.concatenate([jnp.cos(ang_r), jnp.cos(ang_r), jnp.cos(ang_c), jnp.cos(ang_c)], axis=1)
    s1 = jnp.concatenate([-jnp.sin(ang_r), zero, -jnp.sin(ang_c), zero], axis=1)
    s2 = jnp.concatenate([zero, jnp.sin(ang_r), zero, jnp.sin(ang_c)], axis=1)
    return tuple(jnp.concatenate([t, t], axis=1) for t in (cos, s1, s2))


def _make_mla_pack(tag, n_lat, n_ctx, scale):
    hh = MLA_HEADS
    tm = MLA_PACK_ROWS
    ll = n_lat + n_ctx
    rope0 = hh * MLA_NOPE
    tabs = _axial_tables(n_lat)

    def rope_lanes():
        return lax.broadcasted_iota(jnp.int32, (1, LANE), 1) < MLA_ROPE

    def rows(width):
        return pl.BlockSpec((tm, width), lambda i: (i, 0))

    def heads(width, off):
        return pl.BlockSpec((hh, tm, width), lambda i: (0, i + off, 0))

    def put_kv(kv_ref, kr_rot, k_ref, v_ref):
        for h in range(hh):
            k_ref[h, :, :MLA_NOPE] = kv_ref[:, 2 * LANE * h:2 * LANE * h + MLA_NOPE].astype(BF16)
            k_ref[h, :, MLA_NOPE:] = kr_rot
            v_ref[h] = kv_ref[:, 2 * LANE * h + MLA_NOPE:2 * LANE * (h + 1)].astype(BF16)

    def fwd_lat(qp, kv, kr):
        def body(qp_ref, kv_ref, kr_ref, cos_ref, s1_ref, s2_ref, q_ref, k_ref, v_ref):
            cos, s1, s2 = cos_ref[...], s1_ref[...], s2_ref[...]
            keep = rope_lanes()
            for j in range(hh // 2):
                rot = _rope128(qp_ref[:, rope0 + LANE * j:rope0 + LANE * (j + 1)], cos, s1, s2)
                q_ref[2 * j, :, MLA_NOPE:] = jnp.where(keep, rot, 0.0).astype(BF16)
                q_ref[2 * j + 1, :, MLA_NOPE:] = jnp.where(keep, pltpu.roll(rot, MLA_ROPE, 1), 0.0).astype(BF16)
            for h in range(hh):
                q_ref[h, :, :MLA_NOPE] = qp_ref[:, MLA_NOPE * h:MLA_NOPE * (h + 1)].astype(BF16)
            kr_rot = jnp.where(keep, _rope128(kr_ref[...], cos, s1, s2), 0.0).astype(BF16)
            put_kv(kv_ref, kr_rot, k_ref, v_ref)

        return pl.pallas_call(
            body, name=tag + "_lat", grid=(n_lat // tm,),
            in_specs=[rows(qp.shape[1]), rows(kv.shape[1]), rows(LANE), rows(LANE), rows(LANE), rows(LANE)],
            out_specs=[heads(MLA_DQ_PAD, 0), heads(MLA_DQ_PAD, 0), heads(MLA_V, 0)],
            out_shape=[jax.ShapeDtypeStruct((hh, n_lat, MLA_DQ_PAD), BF16),
                       jax.ShapeDtypeStruct((hh, ll, MLA_DQ_PAD), BF16), jax.ShapeDtypeStruct((hh, ll, MLA_V), BF16)],
            compiler_params=_params(),
        )(qp, kv, kr, *tabs)

    def fwd_ctx(kv_c, kr_c, k_buf, v_buf):
        def body(kv_ref, kr_ref, k_in, v_in, k_ref, v_ref):
            kr_rot = jnp.where(rope_lanes(), kr_ref[...], 0.0).astype(BF16)
            put_kv(kv_ref, kr_rot, k_ref, v_ref)

        any_spec = pl.BlockSpec(memory_space=pl.ANY)
        off = n_lat // tm
        return pl.pallas_call(
            body, name=tag + "_ctx", grid=(n_ctx // tm,),
            in_specs=[rows(kv_c.shape[1]), rows(LANE), any_spec, any_spec],
            out_specs=[heads(MLA_DQ_PAD, off), heads(MLA_V, off)],
            out_shape=[jax.ShapeDtypeStruct(k_buf.shape, BF16), jax.ShapeDtypeStruct(v_buf.shape, BF16)],
            input_output_aliases={2: 0, 3: 1}, compiler_params=_params(),
        )(kv_c, kr_c, k_buf, v_buf)

    def take_kv(dk_ref, dv_ref, dkv_ref):
        dkr = jnp.zeros((tm, LANE), F32)
        for h in range(hh):
            dkv_ref[:, 2 * LANE * h:2 * LANE * h + MLA_NOPE] = dk_ref[h, :, :MLA_NOPE].astype(F32)
            dkv_ref[:, 2 * LANE * h + MLA_NOPE:2 * LANE * (h + 1)] = dv_ref[h].astype(F32)
            dkr = dkr + dk_ref[h, :, MLA_NOPE:].astype(F32)
        return jnp.where(rope_lanes(), dkr, 0.0)

    def bwd_lat(dqt, dk, dv, qp_width, kv_width):
        def body(dqt_ref, dk_ref, dv_ref, cos_ref, s1_ref, s2_ref, dqp_ref, dkv_ref, dkr_ref):
            cos, s1, s2 = cos_ref[...], s1_ref[...], s2_ref[...]
            keep = rope_lanes()
            for j in range(hh // 2):
                even = jnp.transpose(dqt_ref[2 * j]) * scale
                odd = jnp.transpose(dqt_ref[2 * j + 1]) * scale
                dqp_ref[:, MLA_NOPE * 2 * j:MLA_NOPE * (2 * j + 1)] = even[:, :MLA_NOPE]
                dqp_ref[:, MLA_NOPE * (2 * j + 1):MLA_NOPE * (2 * j + 2)] = odd[:, :MLA_NOPE]
                g = jnp.where(keep, even[:, MLA_NOPE:], pltpu.roll(odd[:, MLA_NOPE:], MLA_ROPE, 1))
                dqp_ref[:, rope0 + LANE * j:rope0 + LANE * (j + 1)] = _rope128_t(g, cos, s1, s2)
            dkr_ref[...] = jnp.where(keep, _rope128_t(take_kv(dk_ref, dv_ref, dkv_ref), cos, s1, s2), 0.0)

        return pl.pallas_call(
            body, name=tag + "_dlat", grid=(n_lat // tm,),
            in_specs=[pl.BlockSpec((hh, MLA_DQ_PAD, tm), lambda i: (0, 0, i)),
                      heads(MLA_DQ_PAD, 0), heads(MLA_V, 0), rows(LANE), rows(LANE), rows(LANE)],
            out_specs=[rows(qp_width), rows(kv_width), rows(LANE)],
            out_shape=[jax.ShapeDtypeStruct((n_lat, qp_width), F32), jax.ShapeDtypeStruct((n_lat, kv_width), F32),
                       jax.ShapeDtypeStruct((n_lat, LANE), F32)],
            compiler_params=_params(),
        )(dqt, dk, dv, *tabs)

    def bwd_ctx(dk, dv, kv_width):
        def body(dk_ref, dv_ref, dkv_ref, dkr_ref):
            dkr_ref[...] = take_kv(dk_ref, dv_ref, dkv_ref)

        off = n_lat // tm
        return pl.pallas_call(
            body, name=tag + "_dctx", grid=(n_ctx // tm,),
            in_specs=[heads(MLA_DQ_PAD, off), heads(MLA_V, off)],
            out_specs=[rows(kv_width), rows(LANE)],
            out_shape=[jax.ShapeDtypeStruct((n_ctx, kv_width), F32), jax.ShapeDtypeStruct((n_ctx, LANE), F32)],
            compiler_params=_params(),
        )(dk, dv)

    def pack(qp, kv, kr, kv_c, kr_c):
        q, k, v = fwd_lat(qp, kv, kr)
        k, v = fwd_ctx(kv_c, kr_c, k, v)
        return q, k, v

    def unpack(dqt, dk, dv):
        qp_width, kv_width = hh * (MLA_NOPE + MLA_ROPE), hh * (MLA_NOPE + MLA_V)
        dqp, dkv, dkr = bwd_lat(dqt, dk, dv, qp_width, kv_width)
        dkv_c, dkr_c = bwd_ctx(dk, dv, kv_width)
        return dqp, dkv, dkr, dkv_c, dkr_c

    return pack, unpack


def _make_mla(tag, n_lat, n_ctx):
    scale = (MLA_NOPE + MLA_ROPE) ** -0.5
    pack, unpack = _make_mla_pack(tag + "pack", n_lat, n_ctx, scale)
    attn_fwd, attn_delta, attn_bwd = _make_attention(tag, scale)

    @jax.custom_vjp
    def mla(qp, kv, kr, kv_c, kr_c):
        q, k, v = pack(qp, kv, kr, kv_c, kr_c)
        return attn_fwd(q, k, jnp.swapaxes(v, 1, 2))[0]

    def fwd(qp, kv, kr, kv_c, kr_c):
        q, k, v = pack(qp, kv, kr, kv_c, kr_c)
        o, lse = attn_fwd(q, k, jnp.swapaxes(v, 1, 2))
        return o, (q, k, v, o, lse)

    def bwd(res, do):
        q, k, v, o, lse = res
        delta = attn_delta(o, do, q.shape[0])
        dqt, dk, dv = attn_bwd(q, k, jnp.swapaxes(k, 1, 2), v, do, lse, delta)
        return unpack(dqt, dk, dv)

    mla.defvjp(fwd, bwd)
    return mla


def _make_attention(tag, scale):
    neg_big = -1e30
    log2e = 1.4426950408889634
    sub = 256

    def fwd_call(q, k, vt):
        hh, n, dq = q.shape
        dv, ll = vt.shape[1], vt.shape[2]
        tq, tk = _pick(n, 1024), _pick(ll, 1408)
        sb = sub if tk % sub == 0 else tk
        c2 = scale * log2e
        k_steps = ll // tk

        def body(q_ref, k_ref, vt_ref, o_ref, lse_ref, m_scr, l_scr, acc_scr, s_scr, p_scr):
            j = pl.program_id(2)

            @pl.when(j == 0)
            def _():
                m_scr[...] = jnp.full_like(m_scr, neg_big)
                l_scr[...] = jnp.zeros_like(l_scr)
                acc_scr[...] = jnp.zeros_like(acc_scr)

            q_t = q_ref[...]
            m_prev = m_scr[...]
            m_new = m_prev
            for kk in range(tk // sb):
                rows = slice(kk * sb, (kk + 1) * sb)
                s_t = _dot(k_ref[rows, :], q_t, NT)
                s_scr[rows, :] = s_t
                m_new = jnp.maximum(m_new, jnp.max(s_t, axis=0, keepdims=True))
            mc = m_new * c2
            l_part = jnp.zeros_like(m_new)
            for kk in range(tk // sb):
                rows = slice(kk * sb, (kk + 1) * sb)
                p_t = jnp.exp2(s_scr[rows, :] * c2 - mc)
                l_part = l_part + jnp.sum(p_t, axis=0, keepdims=True)
                p_scr[rows, :] = p_t.astype(BF16)
            alpha = jnp.exp2((m_prev - m_new) * c2)
            l_scr[...] = alpha * l_scr[...] + l_part
            acc_scr[...] = alpha * acc_scr[...] + _dot(vt_ref[...], p_scr[...], NN)
            m_scr[...] = m_new

            @pl.when(j == k_steps - 1)
            def _():
                o_ref[...] = jnp.transpose(acc_scr[...] / l_scr[...]).astype(BF16)
                lse_ref[...] = m_scr[...] * scale + jnp.log(l_scr[...])

        return pl.pallas_call(
            body, name=tag + "_fwd", grid=(hh, n // tq, k_steps),
            in_specs=[pl.BlockSpec((None, tq, dq), lambda h, i, j: (h, i, 0)),
                      pl.BlockSpec((None, tk, dq), lambda h, i, j: (h, j, 0)),
                      pl.BlockSpec((None, dv, tk), lambda h, i, j: (h, 0, j))],
            out_specs=[pl.BlockSpec((tq, dv), lambda h, i, j: (i, h)),
                       pl.BlockSpec((None, 1, tq), lambda h, i, j: (h, 0, i))],
            out_shape=[jax.ShapeDtypeStruct((n, hh * dv), BF16), jax.ShapeDtypeStruct((hh, 1, n), F32)],
            scratch_shapes=[pltpu.VMEM((1, tq), F32), pltpu.VMEM((1, tq), F32), pltpu.VMEM((dv, tq), F32),
                            pltpu.VMEM((tk, tq), F32), pltpu.VMEM((tk, tq), BF16)],
            compiler_params=_params(),
        )(q, k, vt)

    def delta_call(o, do, hh):
        n = o.shape[0]
        dv = o.shape[1] // hh
        tq = _pick(n, 1024)

        def body(o_ref, do_ref, d_ref):
            prod_t = jnp.transpose(o_ref[...].astype(F32) * do_ref[...].astype(F32))
            d_ref[...] = jnp.sum(prod_t, axis=0, keepdims=True)

        spec = pl.BlockSpec((tq, dv), lambda h, i: (i, h))
        return pl.pallas_call(
            body, name=tag + "_delta", grid=(hh, n // tq), in_specs=[spec, spec],
            out_specs=pl.BlockSpec((None, 1, tq), lambda h, i: (h, 0, i)),
            out_shape=jax.ShapeDtypeStruct((hh, 1, n), F32), compiler_params=_params(),
        )(o, do)

    def bwd_call(q, k, kt, v, do, lse, delta):
        hh, n, dq = q.shape
        ll, dv = k.shape[1], v.shape[2]
        tq, tk = _pick(n, 1024), _pick(ll, 1408)
        sb = tk
        c2 = scale * log2e
        q_steps = n // tq

        def body(q_ref, k_ref, kt_ref, v_ref, do_ref, lse_ref, d_ref, dqt_ref, dk_ref, dv_ref, dk_scr, dv_scr):
            j = pl.program_id(1)
            i = pl.program_id(2)

            @pl.when(i == 0)
            def _():
                dk_scr[...] = jnp.zeros_like(dk_scr)
                dv_scr[...] = jnp.zeros_like(dv_scr)

            q_t, do_t = q_ref[...], do_ref[...]
            lse2 = lse_ref[...] * log2e
            delta_t = d_ref[...]
            dq_part = None
            for kk in range(tk // sb):
                rows = slice(kk * sb, (kk + 1) * sb)
                s_t = _dot(k_ref[rows, :], q_t, NT)
                p_t = jnp.exp2(s_t * c2 - lse2)
                ds_t = p_t * (_dot(v_ref[rows, :], do_t, NT) - delta_t)
                dv_scr[rows, :] += _dot(p_t, do_t, NN)
                dk_scr[rows, :] += _dot(ds_t, q_t, NN)
                part = _dot(kt_ref[:, rows], ds_t, NN)
                dq_part = part if dq_part is None else dq_part + part
            cols = pl.ds(pl.multiple_of(i * tq, tq), tq)

            @pl.when(j == 0)
            def _():
                dqt_ref[:, cols] = dq_part

            @pl.when(j > 0)
            def _():
                dqt_ref[:, cols] += dq_part

            @pl.when(i == q_steps - 1)
            def _():
                dk_ref[...] = (dk_scr[...] * scale).astype(BF16)
                dv_ref[...] = dv_scr[...].astype(BF16)

        return pl.pallas_call(
            body, name=tag + "_bwd", grid=(hh, ll // tk, q_steps),
            in_specs=[pl.BlockSpec((None, tq, dq), lambda h, j, i: (h, i, 0)),
                      pl.BlockSpec((None, tk, dq), lambda h, j, i: (h, j, 0)),
                      pl.BlockSpec((None, dq, tk), lambda h, j, i: (h, 0, j)),
                      pl.BlockSpec((None, tk, dv), lambda h, j, i: (h, j, 0)),
                      pl.BlockSpec((tq, dv), lambda h, j, i: (i, h)),
                      pl.BlockSpec((None, 1, tq), lambda h, j, i: (h, 0, i)),
                      pl.BlockSpec((None, 1, tq), lambda h, j, i: (h, 0, i))],
            out_specs=[pl.BlockSpec((None, dq, n), lambda h, j, i: (h, 0, 0)),
                       pl.BlockSpec((None, tk, dq), lambda h, j, i: (h, j, 0)),
                       pl.BlockSpec((None, tk, dv), lambda h, j, i: (h, j, 0))],
            out_shape=[jax.ShapeDtypeStruct((hh, dq, n), F32), jax.ShapeDtypeStruct((hh, ll, dq), BF16),
                       jax.ShapeDtypeStruct((hh, ll, dv), BF16)],
            scratch_shapes=[pltpu.VMEM((tk, dq), F32), pltpu.VMEM((tk, dv), F32)],
            compiler_params=_params(),
        )(q, k, kt, v, do, lse, delta)

    return fwd_call, delta_call, bwd_call


def _loss_tile(x, g, tgt):
    r = lax.rsqrt(jnp.mean(x * x, axis=-1, keepdims=True) + RMS_EPS)
    err = x * r * g - tgt
    per_tok = jnp.mean(err * err, axis=-1, keepdims=True)
    return 0.5 * jnp.sum(per_tok, axis=0, keepdims=True)


def _make_final_loss(tag):
    def fwd_call(x, g, tgt):
        t, d = x.shape
        tm = _pick(t, 512, 16)

        def body(x_ref, g_ref, t_ref, l_ref):
            l_ref[...] = jnp.broadcast_to(_loss_tile(x_ref[...], g_ref[...], t_ref[...]), (1, LANE))

        parts = pl.pallas_call(
            body, name=tag + "_fwd", grid=(t // tm,),
            in_specs=[_row_spec(tm, d), _vec_spec(d), _row_spec(tm, d)],
            out_specs=pl.BlockSpec((None, 1, LANE), lambda i: (i, 0, 0)),
            out_shape=jax.ShapeDtypeStruct((t // tm, 1, LANE), F32), compiler_params=_params(),
        )(x, g, tgt)
        return jnp.sum(parts[:, 0, 0])

    def bwd_call(x, g, tgt, dl):
        t, d = x.shape
        tm = _pick(t, 256, 16)

        def body(x_ref, g_ref, t_ref, dl_ref, dx_ref, dg_ref):
            _, vjp = jax.vjp(_loss_tile, x_ref[...], g_ref[...], t_ref[...])
            dx, dg, _ = vjp(dl_ref[...])
            dx_ref[...] = dx

            @pl.when(pl.program_id(0) == 0)
            def _():
                dg_ref[...] = jnp.zeros_like(dg_ref)

            dg_ref[...] += dg

        return pl.pallas_call(
            body, name=tag + "_bwd", grid=(t // tm,),
            in_specs=[_row_spec(tm, d), _vec_spec(d), _row_spec(tm, d), pl.BlockSpec((1, 1), lambda i: (0, 0))],
            out_specs=[_row_spec(tm, d), _vec_spec(d)],
            out_shape=[jax.ShapeDtypeStruct((t, d), F32), jax.ShapeDtypeStruct((1, d), F32)],
            compiler_params=_params(),
        )(x, g, tgt, dl)

    @jax.custom_vjp
    def final_loss(x, g, tgt):
        return fwd_call(x, g, tgt)

    def fwd(x, g, tgt):
        return fwd_call(x, g, tgt), (x, g, tgt)

    def bwd(res, dl):
        x, g, tgt = res
        dx, dg = bwd_call(x, g, tgt, dl.reshape(1, 1).astype(F32))
        return dx, dg, jnp.zeros_like(tgt)

    final_loss.defvjp(fwd, bwd)
    return final_loss


def _exchange(arrays, gather, name):
    n = len(arrays)

    def body(*refs):
        ins, outs = refs[:n], refs[n:2 * n]
        send_sems, recv_sems, local_sems = refs[2 * n:]
        me = 4 * lax.axis_index("x") + 2 * lax.axis_index("y") + lax.axis_index("c")

        def remote(a, d, wait_side=False):
            peer = (me + d) % N_DEV
            origin = (me + N_DEV - d) % N_DEV
            src = ins[a] if gather else ins[a].at[peer]
            dst = outs[a].at[origin if wait_side else me]
            return pltpu.make_async_remote_copy(
                src_ref=src, dst_ref=dst, send_sem=send_sems.at[a, d - 1], recv_sem=recv_sems.at[a, d - 1],
                device_id=(peer // 4, (peer // 2) % 2, peer % 2), device_id_type=pl.DeviceIdType.MESH)

        def local(a):
            src = ins[a] if gather else ins[a].at[me]
            return pltpu.make_async_copy(src, outs[a].at[me], local_sems.at[a])

        for a in range(n):
            for d in range(1, N_DEV):
                remote(a, d).start()
            local(a).start()
        for a in range(n):
            local(a).wait()
            for d in range(1, N_DEV):
                remote(a, d, wait_side=True).wait_recv()
                remote(a, d).wait_send()

    out_shape = []
    for arr in arrays:
        shape = (N_DEV,) + arr.shape if gather else arr.shape
        out_shape.append(jax.ShapeDtypeStruct(shape, arr.dtype))
    any_spec = pl.BlockSpec(memory_space=pl.ANY)
    return pl.pallas_call(
        body, name=name, in_specs=[any_spec] * n, out_specs=[any_spec] * n, out_shape=out_shape,
        scratch_shapes=[pltpu.SemaphoreType.DMA((n, N_DEV - 1)), pltpu.SemaphoreType.DMA((n, N_DEV - 1)),
                        pltpu.SemaphoreType.DMA((n,))],
        compiler_params=pltpu.CompilerParams(has_side_effects=True),
    )(*arrays)


def _coords():
    return lax.axis_index("x"), lax.axis_index("y"), lax.axis_index("c")


def _other_chips(x, y):
    return [(1 - x, y), (x, 1 - y), (1 - x, 1 - y)]


def _gather_two_level(arrays, name):
    n = len(arrays)

    def body(*refs):
        ins, outs = refs[:n], refs[n:2 * n]
        send_sems, recv_sems, local_sems = refs[2 * n:]
        x, y, c = _coords()
        me, sib = (x, y, c), (x, y, 1 - c)
        chips = _other_chips(x, y)

        def copy(a, k, block, to, from_input=False):
            slot = 4 * block[0] + 2 * block[1] + block[2]
            return pltpu.make_async_remote_copy(
                src_ref=ins[a] if from_input else outs[a].at[slot], dst_ref=outs[a].at[slot],
                send_sem=send_sems.at[a, k], recv_sem=recv_sems.at[a, k],
                device_id=to, device_id_type=pl.DeviceIdType.MESH)

        def local(a):
            return pltpu.make_async_copy(ins[a], outs[a].at[4 * x + 2 * y + c], local_sems.at[a])

        for a in range(n):
            for j, chip in enumerate(chips):
                copy(a, 1 + j, me, (*chip, c), True).start()
            copy(a, 0, me, sib, True).start()
            local(a).start()
        for a in range(n):
            for j, chip in enumerate(chips):
                copy(a, 1 + j, (*chip, c), me).wait_recv()
                copy(a, 4 + j, (*chip, c), sib).start()
        for a in range(n):
            copy(a, 0, sib, me).wait_recv()
            for j, chip in enumerate(chips):
                copy(a, 4 + j, (*chip, 1 - c), me).wait_recv()
            for k in range(N_DEV - 1):
                copy(a, k, me, sib, True).wait_send()
            local(a).wait()

    any_spec = pl.BlockSpec(memory_space=pl.ANY)
    return pl.pallas_call(
        body, name=name, in_specs=[any_spec] * n, out_specs=[any_spec] * n,
        out_shape=[jax.ShapeDtypeStruct((N_DEV,) + arr.shape, arr.dtype) for arr in arrays],
        scratch_shapes=[pltpu.SemaphoreType.DMA((n, N_DEV - 1)), pltpu.SemaphoreType.DMA((n, N_DEV - 1)),
                        pltpu.SemaphoreType.DMA((n,))],
        compiler_params=pltpu.CompilerParams(has_side_effects=True),
    )(*arrays)


def _swap_sibling(arrays, name):
    n = len(arrays)
    n_chip = N_DEV // 2

    def body(*refs):
        ins, outs = refs[:n], refs[n:2 * n]
        send_sems, recv_sems = refs[2 * n:]
        x, y, c = _coords()

        def copy(a, q):
            return pltpu.make_async_remote_copy(
                src_ref=ins[a].at[2 * q + (1 - c)], dst_ref=outs[a].at[q],
                send_sem=send_sems.at[a, q], recv_sem=recv_sems.at[a, q],
                device_id=(x, y, 1 - c), device_id_type=pl.DeviceIdType.MESH)

        for a in range(n):
            for q in range(n_chip):
                copy(a, q).start()
        for a in range(n):
            for q in range(n_chip):
                copy(a, q).wait_recv()
                copy(a, q).wait_send()

    any_spec = pl.BlockSpec(memory_space=pl.ANY)
    return pl.pallas_call(
        body, name=name, in_specs=[any_spec] * n, out_specs=[any_spec] * n,
        out_shape=[jax.ShapeDtypeStruct((n_chip,) + arr.shape[1:], arr.dtype) for arr in arrays],
        scratch_shapes=[pltpu.SemaphoreType.DMA((n, n_chip)), pltpu.SemaphoreType.DMA((n, n_chip))],
        compiler_params=pltpu.CompilerParams(has_side_effects=True),
    )(*arrays)


def _scatter_chips(arrays, name):
    n = len(arrays)
    n_chip = N_DEV // 2

    def body(*refs):
        ins, outs = refs[:n], refs[n:2 * n]
        send_sems, recv_sems, local_sems = refs[2 * n:]
        x, y, c = _coords()
        q_me = 2 * x + y
        chips = _other_chips(x, y)

        def copy(a, j, wait_side=False):
            q_peer = 2 * chips[j][0] + chips[j][1]
            return pltpu.make_async_remote_copy(
                src_ref=ins[a].at[q_peer], dst_ref=outs[a].at[q_peer if wait_side else q_me],
                send_sem=send_sems.at[a, j], recv_sem=recv_sems.at[a, j],
                device_id=(*chips[j], c), device_id_type=pl.DeviceIdType.MESH)

        def local(a):
            return pltpu.make_async_copy(ins[a].at[q_me], outs[a].at[q_me], local_sems.at[a])

        for a in range(n):
            for j in range(n_chip - 1):
                copy(a, j).start()
            local(a).start()
        for a in range(n):
            local(a).wait()
            for j in range(n_chip - 1):
                copy(a, j, wait_side=True).wait_recv()
                copy(a, j).wait_send()

    any_spec = pl.BlockSpec(memory_space=pl.ANY)
    return pl.pallas_call(
        body, name=name, in_specs=[any_spec] * n, out_specs=[any_spec] * n,
        out_shape=[jax.ShapeDtypeStruct(arr.shape, arr.dtype) for arr in arrays],
        scratch_shapes=[pltpu.SemaphoreType.DMA((n, n_chip - 1)), pltpu.SemaphoreType.DMA((n, n_chip - 1)),
                        pltpu.SemaphoreType.DMA((n,))],
        compiler_params=pltpu.CompilerParams(has_side_effects=True),
    )(*arrays)


def _pair_add(full, theirs, core, name):
    n_chip, r, cn = theirs.shape
    tr = _pick(r, max(16, (2 * 1024 * 1024) // (4 * cn) // 16 * 16), 16)

    def body(core_ref, mine_ref, theirs_ref, o_ref):
        o_ref[...] = (mine_ref[...].astype(F32) + theirs_ref[...].astype(F32)).astype(BF16)

    tile = pl.BlockSpec((None, tr, cn), lambda q, i, core_ref: (q, i, 0))
    return pl.pallas_call(
        body, name=name,
        grid_spec=pltpu.PrefetchScalarGridSpec(
            num_scalar_prefetch=1, grid=(n_chip, r // tr),
            in_specs=[pl.BlockSpec((None, tr, cn), lambda q, i, core_ref: (2 * q + core_ref[0], i, 0)), tile],
            out_specs=tile),
        out_shape=jax.ShapeDtypeStruct(theirs.shape, BF16), compiler_params=_params(),
    )(core, full, theirs)


def _make_gather_op(tag):
    @jax.custom_vjp
    def gather_op(xl):
        return _exchange([xl], True, tag + "_gather")[0]

    def fwd(xl):
        return gather_op(xl), None

    def bwd(_, g):
        return (jnp.sum(_exchange([g], False, tag + "_scatter")[0], axis=0),)

    gather_op.defvjp(fwd, bwd)
    return gather_op


def _adamw(gstack, w, m, v, name):
    s, r, cn = gstack.shape
    tr = _pick(r, max(8, (2 * 1024 * 1024) // (4 * cn) // 8 * 8), 8)
    c1 = 1.0 - ADAM_B1 ** ADAM_STEP
    c2 = 1.0 - ADAM_B2 ** ADAM_STEP

    def body(g_ref, w_ref, m_ref, v_ref, go_ref, d_ref, mo_ref, vo_ref):
        g = g_ref[0].astype(F32)
        for q in range(1, s):
            g = g + g_ref[q].astype(F32)
        m_new = ADAM_B1 * m_ref[...] + (1.0 - ADAM_B1) * g
        v_new = ADAM_B2 * v_ref[...] + (1.0 - ADAM_B2) * (g * g)
        go_ref[...] = g
        mo_ref[...] = m_new
        vo_ref[...] = v_new
        d_ref[...] = -ADAM_LR * ((m_new / c1) / (jnp.sqrt(v_new / c2) + ADAM_EPS) + ADAM_WD * w_ref[...])

    tile = pl.BlockSpec((tr, cn), lambda i: (i, 0))
    out = jax.ShapeDtypeStruct((r, cn), F32)
    return pl.pallas_call(
        body, name=name, grid=(r // tr,),
        in_specs=[pl.BlockSpec((s, tr, cn), lambda i: (0, i, 0)), tile, tile, tile],
        out_specs=[tile, tile, tile, tile], out_shape=[out, out, out, out],
        compiler_params=_params(),
    )(gstack, w, m, v)


def _rope_tables(pos, dim, base):
    inv = base ** (-jnp.arange(0, dim, 2, dtype=F32) / dim)
    ang = pos.astype(F32)[:, None] * inv[None, :]
    return jnp.cos(ang)[:, None, :], jnp.sin(ang)[:, None, :]


def _rotate(x, cos, sin):
    x1, x2 = jnp.split(x, 2, axis=-1)
    return jnp.concatenate([x1 * cos - x2 * sin, x2 * cos + x1 * sin], axis=-1)


def _heads(t, h):
    return jnp.swapaxes(t.reshape(t.shape[0], h, t.shape[1] // h), 0, 1)


def _cols_from_stack(w):
    return jnp.swapaxes(w, 0, 1).reshape(w.shape[1], N_DEV * w.shape[2])


def _local_loss(p, ctx, silu_c_all, tgt, me):
    x = p["x"]
    n_lat, d = x.shape
    n_ctx = ctx.shape[0]
    n_a = p["ada_w"].shape[1]

    a_in = jnp.concatenate([silu_c_all, jax.nn.silu(p["c_ctx"])[None, :], jnp.zeros((7, d), F32)], axis=0)
    b_loc = lax.dynamic_slice(p["ada_b"], (0, me * n_a), (1, n_a))
    r_loc = _make_small_mm("ada")(a_in, p["ada_w"]) + b_loc
    r_full = _make_gather_op("ada")(r_loc)
    m_lat = lax.dynamic_index_in_dim(r_full, me, axis=1, keepdims=False).reshape(N_MOD, 1, d)
    m_ctx = r_full[:, N_DEV, :].reshape(N_MOD, 1, d)

    x1 = _make_ffn_block("ffn1")(x, p["norm1_g"], m_lat[0], m_lat[1], m_lat[2], p["ffn1_w_in"], p["ffn1_w_out"])
    c1 = _make_ffn_block("ffn1c")(ctx, p["norm1_g"], m_ctx[0], m_ctx[1], m_ctx[2], p["ffn1_w_in"], p["ffn1_w_out"])

    w_mix = jnp.pad(_cols_from_stack(p["mix_w_in"]), ((0, 0), (0, MIX_IN_PAD - MIX_IN)))
    proj = _make_norm_proj("mix")(x1, p["norm2_g"], m_lat[3], m_lat[4], w_mix)
    proj_c = _make_norm_proj("mixc")(c1, p["norm2_g"], m_ctx[3], m_ctx[4], w_mix)
    idx = [0]
    for s in SPLITS:
        idx.append(idx[-1] + s)
    rq, rk, rv, rg, cq, ckv = [proj[:, idx[i]:idx[i + 1]] for i in range(6)]
    crk, crv, cckv = [proj_c[:, idx[i]:idx[i + 1]] for i in (1, 2, 5)]
    kr = proj[:, idx[6]:idx[6] + LANE]
    ckr = proj_c[:, idx[6]:idx[6] + LANE]

    zq = jnp.zeros((1, MLA_Q_RANK), F32)
    zkv = jnp.zeros((1, MLA_KV_RANK), F32)
    w_uq3 = _cols_from_stack(p["mla_w_uq"]).reshape(MLA_Q_RANK, MLA_HEADS, MLA_NOPE + MLA_ROPE)
    w_uq = jnp.concatenate([w_uq3[:, :, :MLA_NOPE].reshape(MLA_Q_RANK, -1),
                            w_uq3[:, :, MLA_NOPE:].reshape(MLA_Q_RANK, -1)], axis=1)
    w_ukv = _cols_from_stack(p["mla_w_ukv"])
    q = _make_norm_proj("uq")(cq, p["mla_q_norm_g"], zq, zq, w_uq)
    kv = _make_norm_proj("ukv")(ckv, p["mla_kv_norm_g"], zkv, zkv, w_ukv)
    kv_c = _make_norm_proj("ukvc")(cckv, p["mla_kv_norm_g"], zkv, zkv, w_ukv)

    lg_f = jax.nn.log_sigmoid(p["ret_decay_fwd"][0])
    lg_b = jax.nn.log_sigmoid(p["ret_decay_bwd"][0])
    ret_tab = _rope_tables(jnp.arange(n_lat), RET_DK, RET_ROPE_BASE)
    rq_h = jnp.swapaxes(_rotate(rq.reshape(n_lat, RET_HEADS, RET_DK), *ret_tab), 0, 1)
    rk_h = jnp.swapaxes(_rotate((rk * (RET_DK ** -0.5)).reshape(n_lat, RET_HEADS, RET_DK), *ret_tab), 0, 1)
    rv_h = _heads(rv, RET_HEADS)
    crk_h = _heads(crk * (RET_DK ** -0.5), RET_HEADS)
    crv_h = _heads(crv, RET_HEADS)
    zero_q = jnp.zeros((RET_HEADS, n_ctx, RET_DK), F32)

    def lanes(lg):
        return jnp.broadcast_to(lg[:, None, None], (RET_HEADS, 1, LANE))

    y_f = _make_ret_dir("retf", False)(
        jnp.concatenate([zero_q, rq_h], axis=1), jnp.concatenate([crk_h, rk_h], axis=1),
        jnp.concatenate([crv_h, rv_h], axis=1), lanes(lg_f))[:, n_ctx:]
    y_b = _make_ret_dir("retb", True)(
        jnp.concatenate([rq_h, zero_q], axis=1), jnp.concatenate([rk_h, crk_h], axis=1),
        jnp.concatenate([rv_h, crv_h], axis=1), lanes(lg_b))[:, :n_lat]
    ret_o = _make_ret_out("reto")(y_f + y_b, rg)

    mla_o = _make_mla("mla", n_lat, n_ctx)(q, kv, kr, kv_c, ckr)

    w_mo = p["mix_w_out"].reshape(-1, d)
    x2 = _make_res_proj("mixo")(jnp.concatenate([ret_o, mla_o], axis=-1), w_mo, x1, m_lat[5])
    x3 = _make_ffn_block("ffn2")(x2, p["norm3_g"], m_lat[6], m_lat[7], m_lat[8], p["ffn2_w_in"], p["ffn2_w_out"])
    return _make_final_loss("loss")(x3, p["final_norm_g"], tgt)


BIG = ("ffn1_w_in", "ffn1_w_out", "mix_w_in", "mla_w_uq", "mla_w_ukv", "mix_w_out", "ffn2_w_in", "ffn2_w_out")
SMALL = ("c_ctx", "ada_b", "norm1_g", "norm2_g", "ret_decay_fwd", "ret_decay_bwd", "mla_q_norm_g",
         "mla_kv_norm_g", "norm3_g", "final_norm_g")
WEIGHTS = ("c_ctx", "ada_w", "ada_b", "norm1_g", "ffn1_w_in", "ffn1_w_out", "norm2_g", "mix_w_in", "ret_decay_fwd",
           "ret_decay_bwd", "mla_q_norm_g", "mla_w_uq", "mla_kv_norm_g", "mla_w_ukv", "mix_w_out", "norm3_g",
           "ffn2_w_in", "ffn2_w_out", "final_norm_g")


def _pack(parts):
    flat = jnp.concatenate([t.reshape(-1) for t in parts])
    pad = (-flat.shape[0]) % LANE
    return jnp.pad(flat, (0, pad)).reshape(1, -1)


def _unpack(flat, like):
    out, off = [], 0
    for t in like:
        out.append(flat[0, off:off + t.size].reshape(t.shape))
        off += t.size
    return out


def kernel(x, c, ctx, c_ctx, ada_w, ada_b, norm1_g, ffn1_w_in, ffn1_w_out, norm2_g, mix_w_in, ret_decay_fwd, ret_decay_bwd, mla_q_norm_g, mla_w_uq, mla_kv_norm_g, mla_w_ukv, mix_w_out, norm3_g, ffn2_w_in, ffn2_w_out, final_norm_g, loss_target, m_c_ctx, m_ada_w, m_ada_b, m_norm1_g, m_ffn1_w_in, m_ffn1_w_out, m_norm2_g, m_mix_w_in, m_ret_decay_fwd, m_ret_decay_bwd, m_mla_q_norm_g, m_mla_w_uq, m_mla_kv_norm_g, m_mla_w_ukv, m_mix_w_out, m_norm3_g, m_ffn2_w_in, m_ffn2_w_out, m_final_norm_g, v_c_ctx, v_ada_w, v_ada_b, v_norm1_g, v_ffn1_w_in, v_ffn1_w_out, v_norm2_g, v_mix_w_in, v_ret_decay_fwd, v_ret_decay_bwd, v_mla_q_norm_g, v_mla_w_uq, v_mla_kv_norm_g, v_mla_w_ukv, v_mix_w_out, v_norm3_g, v_ffn2_w_in, v_ffn2_w_out, v_final_norm_g):
    w = dict(c_ctx=c_ctx, ada_w=ada_w, ada_b=ada_b, norm1_g=norm1_g, ffn1_w_in=ffn1_w_in, ffn1_w_out=ffn1_w_out,
             norm2_g=norm2_g, mix_w_in=mix_w_in, ret_decay_fwd=ret_decay_fwd, ret_decay_bwd=ret_decay_bwd,
             mla_q_norm_g=mla_q_norm_g, mla_w_uq=mla_w_uq, mla_kv_norm_g=mla_kv_norm_g, mla_w_ukv=mla_w_ukv,
             mix_w_out=mix_w_out, norm3_g=norm3_g, ffn2_w_in=ffn2_w_in, ffn2_w_out=ffn2_w_out,
             final_norm_g=final_norm_g)
    mom = dict(c_ctx=m_c_ctx, ada_w=m_ada_w, ada_b=m_ada_b, norm1_g=m_norm1_g, ffn1_w_in=m_ffn1_w_in,
               ffn1_w_out=m_ffn1_w_out, norm2_g=m_norm2_g, mix_w_in=m_mix_w_in, ret_decay_fwd=m_ret_decay_fwd,
               ret_decay_bwd=m_ret_decay_bwd, mla_q_norm_g=m_mla_q_norm_g, mla_w_uq=m_mla_w_uq,
               mla_kv_norm_g=m_mla_kv_norm_g, mla_w_ukv=m_mla_w_ukv, mix_w_out=m_mix_w_out, norm3_g=m_norm3_g,
               ffn2_w_in=m_ffn2_w_in, ffn2_w_out=m_ffn2_w_out, final_norm_g=m_final_norm_g)
    var = dict(c_ctx=v_c_ctx, ada_w=v_ada_w, ada_b=v_ada_b, norm1_g=v_norm1_g, ffn1_w_in=v_ffn1_w_in,
               ffn1_w_out=v_ffn1_w_out, norm2_g=v_norm2_g, mix_w_in=v_mix_w_in, ret_decay_fwd=v_ret_decay_fwd,
               ret_decay_bwd=v_ret_decay_bwd, mla_q_norm_g=v_mla_q_norm_g, mla_w_uq=v_mla_w_uq,
               mla_kv_norm_g=v_mla_kv_norm_g, mla_w_ukv=v_mla_w_ukv, mix_w_out=v_mix_w_out, norm3_g=v_norm3_g,
               ffn2_w_in=v_ffn2_w_in, ffn2_w_out=v_ffn2_w_out, final_norm_g=v_final_norm_g)
    me = 4 * lax.axis_index("x") + 2 * lax.axis_index("y") + lax.axis_index("c")

    gathered = _gather_two_level([w[k][0].astype(BF16) for k in BIG] + [jax.nn.silu(c)], "weights_gather")
    silu_c_all = gathered[-1][:, 0, :]

    p = dict(zip(BIG, gathered[:-1]))
    p["x"] = x[0]
    p["ada_w"] = ada_w[0]
    p["c_ctx"] = c_ctx
    p["ada_b"] = ada_b
    p["final_norm_g"] = final_norm_g[None, :]
    for k in ("norm1_g", "norm2_g", "norm3_g", "mla_q_norm_g", "mla_kv_norm_g", "ret_decay_fwd", "ret_decay_bwd"):
        p[k] = w[k]

    loss_local, grads = jax.value_and_grad(_local_loss)(p, ctx[0], silu_c_all, loss_target[0], me)
    grads["final_norm_g"] = grads["final_norm_g"][0]

    core = lax.axis_index("c").astype(jnp.int32).reshape(1)
    full = [grads[k] for k in BIG]
    theirs = _swap_sibling(full, "grads_swap")
    paired = [_pair_add(f, t, core, "grads_pair_" + k) for k, f, t in zip(BIG, full, theirs)]
    exchanged = _scatter_chips(paired, "grads_scatter")
    zero1 = [jnp.zeros((1,), F32)]
    small_like = zero1 + [w[k] for k in SMALL]
    small_all = _exchange([_pack([loss_local.reshape(1)] + [grads[k] for k in SMALL])], True, "small_grads_gather")[0]
    loss = jnp.sum(small_all[:, 0, 0])

    out_g, out_d, out_m, out_v = {}, {}, {}, {}

    def update(name, gstack, shape2d):
        res = _adamw(gstack, w[name].reshape(shape2d), mom[name].reshape(shape2d), var[name].reshape(shape2d),
                     "adamw_" + name)
        out_g[name], out_d[name], out_m[name], out_v[name] = [t.reshape(w[name].shape) for t in res]

    for k, gs in zip(BIG, exchanged):
        update(k, gs, gs.shape[1:])
    update("ada_w", grads["ada_w"][None], ada_w.shape[1:])
    res = _adamw(small_all, _pack(small_like), _pack(zero1 + [mom[k] for k in SMALL]),
                 _pack(zero1 + [var[k] for k in SMALL]), "adamw_small")
    for dst, flat in zip((out_g, out_d, out_m, out_v), res):
        for k, t in zip(SMALL, _unpack(flat, small_like)[1:]):
            dst[k] = t

    return (loss, grads["x"][None], *[out_g[k] for k in WEIGHTS], *[out_d[k] for k in WEIGHTS],
            *[out_m[k] for k in WEIGHTS], *[out_v[k] for k in WEIGHTS])
```

```python
import functools

import jax
import jax.numpy as jnp
from jax import lax
from jax.experimental import pallas as pl
from jax.experimental.pallas import tpu as pltpu

F32 = jnp.float32
BF16 = jnp.bfloat16

N_DEV = 8
MESH_AXES = ("x", "y", "c")

GRID_W = 64
N_MOD = 9
RET_HEADS = 8
RET_DK = 64
RET_DV = 128
RET_CHUNK = 256
RET_ROPE_BASE = 10000.0
MLA_HEADS = 8
MLA_Q_RANK = 512
MLA_KV_RANK = 256
MLA_NOPE = 128
MLA_ROPE = 64
MLA_V = 128
AXIAL_BASE = 10000.0
RMS_EPS = 1e-6
GN_EPS = 1e-5
SPLITS = (RET_HEADS * RET_DK, RET_HEADS * RET_DK, RET_HEADS * RET_DV, RET_HEADS * RET_DV,
          MLA_Q_RANK, MLA_KV_RANK, MLA_ROPE)
MIX_IN = sum(SPLITS)
MIX_IN_PAD = 4096

ADAM_LR = 0.001
ADAM_B1 = 0.9
ADAM_B2 = 0.999
ADAM_EPS = 1e-08
ADAM_WD = 0.01
ADAM_STEP = 10

LANE = 128
VMEM_LIMIT_BYTES = 56 * 1024 * 1024

NN = ((1,), (0,))
NT = ((1,), (1,))
TN = ((0,), (0,))


def _pick(dim, target, align=LANE):
    t = min(dim, target)
    t -= t % align
    while t >= align:
        if dim % t == 0:
            return t
        t -= align
    return dim


def _params():
    return pltpu.CompilerParams(vmem_limit_bytes=VMEM_LIMIT_BYTES)


def _dot(a, b, dims):
    return lax.dot_general(a.astype(BF16), b.astype(BF16), (dims, ((), ())), preferred_element_type=F32)


def _mm_call(name, grid, ins, pairs, outs, acc_shapes, epilogue):
    n_in, n_out = len(ins), len(outs)
    k_axis = len(grid) - 1
    k_steps = grid[k_axis]

    def body(*refs):
        in_refs = refs[:n_in]
        out_refs = refs[n_in:n_in + n_out]
        accs = refs[n_in + n_out:]
        k = pl.program_id(k_axis)

        @pl.when(k == 0)
        def _():
            for acc in accs:
                acc[...] = jnp.zeros_like(acc)

        for ai, bi, dims, ci in pairs:
            accs[ci][...] += _dot(in_refs[ai][...], in_refs[bi][...], dims)

        @pl.when(k == k_steps - 1)
        def _():
            epilogue([acc[...] for acc in accs], in_refs, out_refs)

    res = pl.pallas_call(
        body, name=name, grid=grid,
        in_specs=[s for _, s in ins], out_specs=[s for _, s in outs],
        out_shape=[s for s, _ in outs],
        scratch_shapes=[pltpu.VMEM(s, F32) for s in acc_shapes],
        compiler_params=_params(),
    )(*[a for a, _ in ins])
    return res


def _matmul(a, b, mode, out_dtype, name, tm=1024, tn=1024, tk=512):
    if mode == "nn":
        (m, kd), n = a.shape, b.shape[1]
    elif mode == "nt":
        (m, kd), n = a.shape, b.shape[0]
    else:
        (kd, m), n = a.shape, b.shape[1]
    tm, tn = _pick(m, tm, 16), _pick(n, tn)
    tk = _pick(kd, tk) if mode != "tn" else _pick(kd, tk, 16)
    if mode == "nn":
        a_spec = pl.BlockSpec((tm, tk), lambda i, j, k: (i, k))
        b_spec = pl.BlockSpec((tk, tn), lambda i, j, k: (k, j))
        dims = NN
    elif mode == "nt":
        a_spec = pl.BlockSpec((tm, tk), lambda i, j, k: (i, k))
        b_spec = pl.BlockSpec((tn, tk), lambda i, j, k: (j, k))
        dims = NT
    else:
        a_spec = pl.BlockSpec((tk, tm), lambda i, j, k: (k, i))
        b_spec = pl.BlockSpec((tk, tn), lambda i, j, k: (k, j))
        dims = TN

    def epilogue(accs, in_refs, out_refs):
        out_refs[0][...] = accs[0].astype(out_dtype)

    return _mm_call(
        name, (m // tm, n // tn, kd // tk), [(a, a_spec), (b, b_spec)], [(0, 1, dims, 0)],
        [(jax.ShapeDtypeStruct((m, n), out_dtype), pl.BlockSpec((tm, tn), lambda i, j, k: (i, j)))],
        [(tm, tn)], epilogue)[0]


def _norm_mod_tile(x, ng, sc, sh):
    r = lax.rsqrt(jnp.mean(x * x, axis=-1, keepdims=True) + RMS_EPS)
    return (x * r * ng) * (1.0 + sc) + sh


def _row_spec(tm, d):
    return pl.BlockSpec((tm, d), lambda i: (i, 0))


def _vec_spec(d):
    return pl.BlockSpec((1, d), lambda i: (0, 0))


def _norm_mod_fwd(x, ng, sc, sh, name):
    t, d = x.shape
    tm = _pick(t, 512, 16)

    def body(x_ref, ng_ref, sc_ref, sh_ref, h_ref):
        h_ref[...] = _norm_mod_tile(x_ref[...], ng_ref[...], sc_ref[...], sh_ref[...]).astype(BF16)

    return pl.pallas_call(
        body, name=name, grid=(t // tm,),
        in_specs=[_row_spec(tm, d), _vec_spec(d), _vec_spec(d), _vec_spec(d)],
        out_specs=_row_spec(tm, d), out_shape=jax.ShapeDtypeStruct((t, d), BF16),
        compiler_params=_params(),
    )(x, ng, sc, sh)


def _norm_mod_bwd(x, ng, sc, sh, dh, dres, name):
    t, d = x.shape
    tm = _pick(t, 256, 16)
    has_res = dres is not None

    def body(*refs):
        if has_res:
            x_ref, ng_ref, sc_ref, sh_ref, dh_ref, dres_ref, dx_ref, dng_ref, dsc_ref, dsh_ref = refs
        else:
            x_ref, ng_ref, sc_ref, sh_ref, dh_ref, dx_ref, dng_ref, dsc_ref, dsh_ref = refs
        _, vjp = jax.vjp(_norm_mod_tile, x_ref[...], ng_ref[...], sc_ref[...], sh_ref[...])
        dx, dng, dsc, dsh = vjp(dh_ref[...].astype(F32))
        if has_res:
            dx = dx + dres_ref[...]
        dx_ref[...] = dx

        @pl.when(pl.program_id(0) == 0)
        def _():
            dng_ref[...] = jnp.zeros_like(dng_ref)
            dsc_ref[...] = jnp.zeros_like(dsc_ref)
            dsh_ref[...] = jnp.zeros_like(dsh_ref)

        dng_ref[...] += dng
        dsc_ref[...] += dsc
        dsh_ref[...] += dsh

    ins = [x, ng, sc, sh, dh] + ([dres] if has_res else [])
    in_specs = [_row_spec(tm, d), _vec_spec(d), _vec_spec(d), _vec_spec(d), _row_spec(tm, d)]
    in_specs += [_row_spec(tm, d)] if has_res else []
    vec = jax.ShapeDtypeStruct((1, d), F32)
    return pl.pallas_call(
        body, name=name, grid=(t // tm,), in_specs=in_specs,
        out_specs=[_row_spec(tm, d), _vec_spec(d), _vec_spec(d), _vec_spec(d)],
        out_shape=[jax.ShapeDtypeStruct((t, d), F32), vec, vec, vec],
        compiler_params=_params(),
    )(*ins)


def _res_mm_fwd(a, w, x, gate, coef, name):
    t, kd = a.shape
    d = w.shape[1]
    tm, tn, tk = _pick(t, 1024, 16), _pick(d, 1024), _pick(kd, 2816)

    def epilogue(accs, in_refs, out_refs):
        f = accs[0]
        out_refs[0][...] = in_refs[2][...] + (coef * in_refs[3][...]) * f
        out_refs[1][...] = f.astype(BF16)

    tile = pl.BlockSpec((tm, tn), lambda i, j, k: (i, j))
    return _mm_call(
        name, (t // tm, d // tn, kd // tk),
        [(a, pl.BlockSpec((tm, tk), lambda i, j, k: (i, k))), (w, pl.BlockSpec((tk, tn), lambda i, j, k: (k, j))),
         (x, tile), (gate, pl.BlockSpec((1, tn), lambda i, j, k: (0, j)))],
        [(0, 1, NN, 0)],
        [(jax.ShapeDtypeStruct((t, d), F32), tile), (jax.ShapeDtypeStruct((t, d), BF16), tile)],
        [(tm, tn)], epilogue)


def _gate_bwd(dxo, f, gate, coef, name):
    t, d = dxo.shape
    tm = _pick(t, 512, 16)

    def body(dxo_ref, f_ref, gate_ref, df_ref, dgate_ref):
        dxo_t = dxo_ref[...]
        df_ref[...] = ((coef * gate_ref[...]) * dxo_t).astype(BF16)

        @pl.when(pl.program_id(0) == 0)
        def _():
            dgate_ref[...] = jnp.zeros_like(dgate_ref)

        dgate_ref[...] += coef * jnp.sum(dxo_t * f_ref[...].astype(F32), axis=0, keepdims=True)

    return pl.pallas_call(
        body, name=name, grid=(t // tm,),
        in_specs=[_row_spec(tm, d), _row_spec(tm, d), _vec_spec(d)],
        out_specs=[_row_spec(tm, d), _vec_spec(d)],
        out_shape=[jax.ShapeDtypeStruct((t, d), BF16), jax.ShapeDtypeStruct((1, d), F32)],
        compiler_params=_params(),
    )(dxo, f, gate)


def _ffn_in_fwd(h, w_in, name):
    t, d = h.shape
    n = w_in.shape[2]
    half = N_DEV // 2
    f = half * n
    tm = _pick(t, 512, 16)

    def epilogue(accs, in_refs, out_refs):
        g, u = accs
        out_refs[0][...] = (g * jax.nn.sigmoid(g) * u).astype(BF16)
        out_refs[1][0] = g.astype(BF16)
        out_refs[1][1] = u.astype(BF16)

    return _mm_call(
        name, (half, t // tm, 1),
        [(h, pl.BlockSpec((tm, d), lambda j, i, k: (i, 0))),
         (w_in, pl.BlockSpec((None, d, n), lambda j, i, k: (j, 0, 0))),
         (w_in, pl.BlockSpec((None, d, n), lambda j, i, k: (j + half, 0, 0)))],
        [(0, 1, NN, 0), (0, 2, NN, 1)],
        [(jax.ShapeDtypeStruct((t, f), BF16), pl.BlockSpec((tm, n), lambda j, i, k: (i, j))),
         (jax.ShapeDtypeStruct((2, t, f), BF16), pl.BlockSpec((2, tm, n), lambda j, i, k: (0, i, j)))],
        [(tm, n), (tm, n)], epilogue)


def _ffn_da_bwd(df, w_out2d, gu, name):
    t, d = df.shape
    f = w_out2d.shape[0]
    half = N_DEV // 2
    n = f // half
    tm = _pick(t, 512, 16)

    def epilogue(accs, in_refs, out_refs):
        da = accs[0]
        g = in_refs[2][0].astype(F32)
        u = in_refs[2][1].astype(F32)
        s = jax.nn.sigmoid(g)
        out_refs[0][0] = (da * u * (s * (1.0 + g * (1.0 - s)))).astype(BF16)
        out_refs[0][1] = (da * (g * s)).astype(BF16)

    gu_spec = pl.BlockSpec((2, tm, n), lambda j, i, k: (0, i, j))
    return _mm_call(
        name, (half, t // tm, 1),
        [(df, pl.BlockSpec((tm, d), lambda j, i, k: (i, 0))),
         (w_out2d, pl.BlockSpec((n, d), lambda j, i, k: (j, 0))),
         (gu, gu_spec)],
        [(0, 1, NT, 0)],
        [(jax.ShapeDtypeStruct((2, t, f), BF16), gu_spec)],
        [(tm, n)], epilogue)[0]


def _ffn_dh_bwd(dgu, w_in, name):
    _, t, f = dgu.shape
    d, n = w_in.shape[1], w_in.shape[2]
    half = N_DEV // 2
    tm = _pick(t, 512, 16)

    def epilogue(accs, in_refs, out_refs):
        out_refs[0][...] = accs[0]

    return _mm_call(
        name, (t // tm, 1, half),
        [(dgu, pl.BlockSpec((None, tm, n), lambda i, j, k: (0, i, k))),
         (dgu, pl.BlockSpec((None, tm, n), lambda i, j, k: (1, i, k))),
         (w_in, pl.BlockSpec((None, d, n), lambda i, j, k: (k, 0, 0))),
         (w_in, pl.BlockSpec((None, d, n), lambda i, j, k: (k + half, 0, 0)))],
        [(0, 2, NT, 0), (1, 3, NT, 0)],
        [(jax.ShapeDtypeStruct((t, d), F32), pl.BlockSpec((tm, d), lambda i, j, k: (i, 0)))],
        [(tm, d)], epilogue)[0]


def _ffn_dwin_bwd(h, dgu, name):
    t, d = h.shape
    f = dgu.shape[2]
    half = N_DEV // 2
    n = f // half
    tk = _pick(t, 1024, 16)

    def epilogue(accs, in_refs, out_refs):
        out_refs[0][...] = accs[0].astype(BF16)

    return _mm_call(
        name, (N_DEV, 1, t // tk),
        [(h, pl.BlockSpec((tk, d), lambda j, i, k: (k, 0))),
         (dgu, pl.BlockSpec((None, tk, n), lambda j, i, k: (j // half, k, j % half)))],
        [(0, 1, TN, 0)],
        [(jax.ShapeDtypeStruct((N_DEV, d, n), BF16), pl.BlockSpec((None, d, n), lambda j, i, k: (j, 0, 0)))],
        [(d, n)], epilogue)[0]


def _make_ffn_block(tag):
    @jax.custom_vjp
    def ffn_block(x, ng, sh, sc, gate, w_in, w_out):
        return fwd(x, ng, sh, sc, gate, w_in, w_out)[0]

    def fwd(x, ng, sh, sc, gate, w_in, w_out):
        f = w_out.shape[0] * w_out.shape[1]
        w_out2d = w_out.reshape(f, w_out.shape[2])
        h = _norm_mod_fwd(x, ng, sc, sh, tag + "_norm")
        a, gu = _ffn_in_fwd(h, w_in, tag + "_in")
        xo, f1 = _res_mm_fwd(a, w_out2d, x, gate, 0.5, tag + "_out")
        return xo, (x, ng, sh, sc, gate, w_in, w_out, h, a, gu, f1)

    def bwd(res, dxo):
        x, ng, sh, sc, gate, w_in, w_out, h, a, gu, f1 = res
        f = w_out.shape[0] * w_out.shape[1]
        w_out2d = w_out.reshape(f, w_out.shape[2])
        df, dgate = _gate_bwd(dxo, f1, gate, 0.5, tag + "_dgate")
        dgu = _ffn_da_bwd(df, w_out2d, gu, tag + "_da")
        dw_out = _matmul(a, df, "tn", BF16, tag + "_dwout", tm=_pick(f, 1408, 16), tn=2048, tk=1024)
        dh = _ffn_dh_bwd(dgu, w_in, tag + "_dh")
        dw_in = _ffn_dwin_bwd(h, dgu, tag + "_dwin")
        dx, dng, dsc, dsh = _norm_mod_bwd(x, ng, sc, sh, dh, dxo, tag + "_dnorm")
        return dx, dng, dsh, dsc, dgate, dw_in, dw_out.reshape(w_out.shape)

    ffn_block.defvjp(fwd, bwd)
    return ffn_block


def _make_norm_proj(tag):
    @jax.custom_vjp
    def norm_proj(x, ng, sh, sc, w):
        return fwd(x, ng, sh, sc, w)[0]

    def fwd(x, ng, sh, sc, w):
        h = _norm_mod_fwd(x, ng, sc, sh, tag + "_norm")
        p = _matmul(h, w, "nn", F32, tag + "_mm", tm=1024, tn=1024, tk=w.shape[0])
        return p, (x, ng, sh, sc, w, h)

    def bwd(res, dp):
        x, ng, sh, sc, w, h = res
        dh = _matmul(dp, w, "nt", F32, tag + "_dh", tm=512, tn=w.shape[0], tk=2048)
        dw = _matmul(h, dp, "tn", BF16, tag + "_dw", tm=w.shape[0], tn=1024, tk=1024)
        dx, dng, dsc, dsh = _norm_mod_bwd(x, ng, sc, sh, dh, None, tag + "_dnorm")
        return dx, dng, dsh, dsc, dw

    norm_proj.defvjp(fwd, bwd)
    return norm_proj


def _make_res_proj(tag):
    @jax.custom_vjp
    def res_proj(a, w, x, gate):
        return fwd(a, w, x, gate)[0]

    def fwd(a, w, x, gate):
        xo, f = _res_mm_fwd(a, w, x, gate, 1.0, tag + "_mm")
        return xo, (a, w, gate, f)

    def bwd(res, dxo):
        a, w, gate, f = res
        df, dgate = _gate_bwd(dxo, f, gate, 1.0, tag + "_dgate")
        da = _matmul(df, w, "nt", BF16, tag + "_da", tm=1024, tn=1024, tk=2048)
        dw = _matmul(a, df, "tn", BF16, tag + "_dw", tm=1024, tn=2048, tk=1024)
        return da, dw, dxo, dgate

    res_proj.defvjp(fwd, bwd)
    return res_proj


def _make_small_mm(tag):
    @jax.custom_vjp
    def small_mm(a, w):
        return _matmul(a, w, "nn", F32, tag + "_mm", tm=a.shape[0], tn=768, tk=w.shape[0])

    def fwd(a, w):
        return small_mm(a, w), (a, w)

    def bwd(res, dr):
        a, w = res
        da = _matmul(dr, w, "nt", F32, tag + "_da", tm=a.shape[0], tn=w.shape[0], tk=768)
        dw = _matmul(a, dr, "tn", F32, tag + "_dw", tm=1024, tn=768, tk=a.shape[0])
        return da, dw

    small_mm.defvjp(fwd, bwd)
    return small_mm


def _ret_chunk_terms(lg, c, reverse):
    row = lax.broadcasted_iota(jnp.int32, (c, c), 0).astype(F32)
    col = lax.broadcasted_iota(jnp.int32, (c, c), 1).astype(F32)
    pos = lax.broadcasted_iota(jnp.int32, (c, 1), 0).astype(F32)
    if reverse:
        diff = col - row
        mask = diff > 0.0
        e_exp = float(c) - pos
        f_exp = pos
    else:
        diff = row - col
        mask = diff >= 0.0
        e_exp = pos + 1.0
        f_exp = float(c - 1) - pos
    diffm = jnp.where(mask, diff, 0.0)
    dm = jnp.where(mask, jnp.exp(lg * diffm), 0.0)
    return diffm, dm, e_exp, jnp.exp(lg * e_exp), f_exp, jnp.exp(lg * f_exp)


def _lane0(val):
    lane = lax.broadcasted_iota(jnp.int32, (1, LANE), 1)
    return jnp.where(lane == 0, val, 0.0)


RET_HEAD_BLOCK = 4


def _make_ret_dir(tag, reverse):
    hb = RET_HEAD_BLOCK

    def heads_spec(nc, width, flip):
        if flip:
            return pl.BlockSpec((hb, RET_CHUNK, width), lambda h, t: (h, nc - 1 - t, 0))
        return pl.BlockSpec((hb, RET_CHUNK, width), lambda h, t: (h, t, 0))

    def state_spec(nc, flip):
        if flip:
            return pl.BlockSpec((hb, None, RET_DK, RET_DV), lambda h, t: (h, nc - 1 - t, 0, 0))
        return pl.BlockSpec((hb, None, RET_DK, RET_DV), lambda h, t: (h, t, 0, 0))

    lg_spec = pl.BlockSpec((hb, 1, LANE), lambda h, t: (h, 0, 0))

    def fwd_call(q, k, v, lgb):
        hh, ll, _ = q.shape
        c = RET_CHUNK
        nc = ll // c

        def body(q_ref, k_ref, v_ref, lg_ref, y_ref, sall_ref, s_scr):
            @pl.when(pl.program_id(1) == 0)
            def _():
                s_scr[...] = jnp.zeros_like(s_scr)

            for b in range(hb):
                lg = lg_ref[b][:, :1]
                _, dm, _, xi, _, zeta = _ret_chunk_terms(lg, c, reverse)
                q_t, k_t, v_t = q_ref[b], k_ref[b], v_ref[b]
                s = s_scr[b]
                p = _dot(q_t, k_t, NT) * dm
                y_ref[b] = _dot(p, v_t, NN) + _dot(q_t * xi, s, NN)
                sall_ref[b] = s
                s_scr[b] = jnp.exp(lg * float(c)) * s + _dot(k_t * zeta, v_t, TN)

        return pl.pallas_call(
            body, name=tag + "_fwd", grid=(hh // hb, nc),
            in_specs=[heads_spec(nc, RET_DK, reverse), heads_spec(nc, RET_DK, reverse),
                      heads_spec(nc, RET_DV, reverse), lg_spec],
            out_specs=[heads_spec(nc, RET_DV, reverse), state_spec(nc, reverse)],
            out_shape=[jax.ShapeDtypeStruct((hh, ll, RET_DV), F32),
                       jax.ShapeDtypeStruct((hh, nc, RET_DK, RET_DV), F32)],
            scratch_shapes=[pltpu.VMEM((hb, RET_DK, RET_DV), F32)],
            compiler_params=_params(),
        )(q, k, v, lgb)

    def bwd_call(q, k, v, lgb, sall, dy):
        hh, ll, _ = q.shape
        c = RET_CHUNK
        nc = ll // c
        flip = not reverse

        def body(q_ref, k_ref, v_ref, lg_ref, sall_ref, dy_ref, dq_ref, dk_ref, dv_ref, dlg_ref, ds_scr):
            @pl.when(pl.program_id(1) == 0)
            def _():
                ds_scr[...] = jnp.zeros_like(ds_scr)
                dlg_ref[...] = jnp.zeros_like(dlg_ref)

            def total(m):
                return jnp.sum(jnp.sum(m, axis=1, keepdims=True), axis=0, keepdims=True)

            for b in range(hb):
                lg = lg_ref[b][:, :1]
                diffm, dm, e_exp, xi, f_exp, zeta = _ret_chunk_terms(lg, c, reverse)
                q_t, k_t, v_t, dy_t = q_ref[b], k_ref[b], v_ref[b], dy_ref[b]
                s = sall_ref[b]
                dsn = ds_scr[b]
                a = _dot(q_t, k_t, NT)
                da = _dot(dy_t, v_t, NT) * dm
                g = _dot(dy_t, s, NT)
                hm = _dot(v_t, dsn, NT)
                dq_ref[b] = _dot(da, k_t, NN) + xi * g
                dk_ref[b] = _dot(da, q_t, TN) + zeta * hm
                dv_ref[b] = _dot(a * dm, dy_t, TN) + _dot(k_t * zeta, dsn, NN)
                gc = jnp.exp(lg * float(c))
                ds_scr[b] = gc * dsn + _dot(q_t * xi, dy_t, TN)
                dl = (total(da * a * diffm) + total(e_exp * xi * q_t * g)
                      + float(c) * gc * total(s * dsn) + total(f_exp * zeta * k_t * hm))
                dlg_ref[b] += _lane0(dl)

        return pl.pallas_call(
            body, name=tag + "_bwd", grid=(hh // hb, nc),
            in_specs=[heads_spec(nc, RET_DK, flip), heads_spec(nc, RET_DK, flip), heads_spec(nc, RET_DV, flip),
                      lg_spec, state_spec(nc, flip), heads_spec(nc, RET_DV, flip)],
            out_specs=[heads_spec(nc, RET_DK, flip), heads_spec(nc, RET_DK, flip), heads_spec(nc, RET_DV, flip),
                       lg_spec],
            out_shape=[jax.ShapeDtypeStruct((hh, ll, RET_DK), F32), jax.ShapeDtypeStruct((hh, ll, RET_DK), F32),
                       jax.ShapeDtypeStruct((hh, ll, RET_DV), F32), jax.ShapeDtypeStruct((hh, 1, LANE), F32)],
            scratch_shapes=[pltpu.VMEM((hb, RET_DK, RET_DV), F32)],
            compiler_params=_params(),
        )(q, k, v, lgb, sall, dy)

    @jax.custom_vjp
    def ret_dir(q, k, v, lgb):
        return fwd_call(q, k, v, lgb)[0]

    def fwd(q, k, v, lgb):
        y, sall = fwd_call(q, k, v, lgb)
        return y, (q, k, v, lgb, sall)

    def bwd(res, dy):
        q, k, v, lgb, sall = res
        dq, dk, dv, dlg = bwd_call(q, k, v, lgb, sall, dy)
        return dq, dk, dv, dlg

    ret_dir.defvjp(fwd, bwd)
    return ret_dir


def _ret_out_tile(y, g):
    mu = jnp.mean(y, axis=-1, keepdims=True)
    var = jnp.mean(jnp.square(y - mu), axis=-1, keepdims=True)
    return (g * jax.nn.sigmoid(g)) * ((y - mu) * lax.rsqrt(var + GN_EPS))


def _make_ret_out(tag):
    def specs(tm):
        y_spec = pl.BlockSpec((None, tm, RET_DV), lambda h, i: (h, i, 0))
        g_spec = pl.BlockSpec((tm, RET_DV), lambda h, i: (i, h))
        return y_spec, g_spec

    def fwd_call(y, g):
        hh, n, _ = y.shape
        tm = _pick(n, 1024, 16)
        y_spec, g_spec = specs(tm)

        def body(y_ref, g_ref, o_ref):
            o_ref[...] = _ret_out_tile(y_ref[...], g_ref[...]).astype(BF16)

        return pl.pallas_call(
            body, name=tag + "_fwd", grid=(hh, n // tm), in_specs=[y_spec, g_spec], out_specs=g_spec,
            out_shape=jax.ShapeDtypeStruct((n, hh * RET_DV), BF16), compiler_params=_params(),
        )(y, g)

    def bwd_call(y, g, do):
        hh, n, _ = y.shape
        tm = _pick(n, 1024, 16)
        y_spec, g_spec = specs(tm)

        def body(y_ref, g_ref, do_ref, dy_ref, dg_ref):
            _, vjp = jax.vjp(_ret_out_tile, y_ref[...], g_ref[...])
            dy, dg = vjp(do_ref[...].astype(F32))
            dy_ref[...] = dy
            dg_ref[...] = dg

        return pl.pallas_call(
            body, name=tag + "_bwd", grid=(hh, n // tm), in_specs=[y_spec, g_spec, g_spec],
            out_specs=[y_spec, g_spec],
            out_shape=[jax.ShapeDtypeStruct(y.shape, F32), jax.ShapeDtypeStruct(g.shape, F32)],
            compiler_params=_params(),
        )(y, g, do)

    @jax.custom_vjp
    def ret_out(y, g):
        return fwd_call(y, g)

    def fwd(y, g):
        return fwd_call(y, g), (y, g)

    def bwd(res, do):
        y, g = res
        return tuple(bwd_call(y, g, do))

    ret_out.defvjp(fwd, bwd)
    return ret_out


MLA_DQ_PAD = 2 * LANE
MLA_PACK_ROWS = 256


def _rope128(x, cos, s1, s2):
    return x * cos + pltpu.roll(x, LANE - 16, 1) * s1 + pltpu.roll(x, 16, 1) * s2


def _rope128_t(g, cos, s1, s2):
    return g * cos + pltpu.roll(g * s1, 16, 1) + pltpu.roll(g * s2, LANE - 16, 1)


def _axial_tables(n_lat):
    n_rows = n_lat // GRID_W
    half = MLA_ROPE // 2
    inv = AXIAL_BASE ** (-jnp.arange(0, half, 2, dtype=F32) / half)
    ang_r = jnp.repeat(jnp.arange(n_rows), GRID_W).astype(F32)[:, None] * inv[None, :]
    ang_c = jnp.tile(jnp.arange(GRID_W), n_rows).astype(F32)[:, None] * inv[None, :]
    zero = jnp.zeros_like(ang_r)
    cos = jnp.concatenate([jnp.cos(ang_r), jnp.cos(ang_r), jnp.cos(ang_c), jnp.cos(ang_c)], axis=1)
    s1 = jnp.concatenate([-jnp.sin(ang_r), zero, -jnp.sin(ang_c), zero], axis=1)
    s2 = jnp.concatenate([zero, jnp.sin(ang_r), zero, jnp.sin(ang_c)], axis=1)
    return tuple(jnp.concatenate([t, t], axis=1) for t in (cos, s1, s2))


def _make_mla_pack(tag, n_lat, n_ctx, scale):
    hh = MLA_HEADS
    tm = MLA_PACK_ROWS
    ll = n_lat + n_ctx
    rope0 = hh * MLA_NOPE
    tabs = _axial_tables(n_lat)

    def rope_lanes():
        return lax.broadcasted_iota(jnp.int32, (1, LANE), 1) < MLA_ROPE

    def rows(width):
        return pl.BlockSpec((tm, width), lambda i: (i, 0))

    def heads(width, off):
        return pl.BlockSpec((hh, tm, width), lambda i: (0, i + off, 0))

    def put_kv(kv_ref, kr_rot, k_ref, v_ref):
        for h in range(hh):
            k_ref[h, :, :MLA_NOPE] = kv_ref[:, 2 * LANE * h:2 * LANE * h + MLA_NOPE].astype(BF16)
            k_ref[h, :, MLA_NOPE:] = kr_rot
            v_ref[h] = kv_ref[:, 2 * LANE * h + MLA_NOPE:2 * LANE * (h + 1)].astype(BF16)

    def fwd_lat(qp, kv, kr):
        def body(qp_ref, kv_ref, kr_ref, cos_ref, s1_ref, s2_ref, q_ref, k_ref, v_ref):
            cos, s1, s2 = cos_ref[...], s1_ref[...], s2_ref[...]
            keep = rope_lanes()
            for j in range(hh // 2):
                rot = _rope128(qp_ref[:, rope0 + LANE * j:rope0 + LANE * (j + 1)], cos, s1, s2)
                q_ref[2 * j, :, MLA_NOPE:] = jnp.where(keep, rot, 0.0).astype(BF16)
                q_ref[2 * j + 1, :, MLA_NOPE:] = jnp.where(keep, pltpu.roll(rot, MLA_ROPE, 1), 0.0).astype(BF16)
            for h in range(hh):
                q_ref[h, :, :MLA_NOPE] = qp_ref[:, MLA_NOPE * h:MLA_NOPE * (h + 1)].astype(BF16)
            kr_rot = jnp.where(keep, _rope128(kr_ref[...], cos, s1, s2), 0.0).astype(BF16)
            put_kv(kv_ref, kr_rot, k_ref, v_ref)

        return pl.pallas_call(
            body, name=tag + "_lat", grid=(n_lat // tm,),
            in_specs=[rows(qp.shape[1]), rows(kv.shape[1]), rows(LANE), rows(LANE), rows(LANE), rows(LANE)],
            out_specs=[heads(MLA_DQ_PAD, 0), heads(MLA_DQ_PAD, 0), heads(MLA_V, 0)],
            out_shape=[jax.ShapeDtypeStruct((hh, n_lat, MLA_DQ_PAD), BF16),
                       jax.ShapeDtypeStruct((hh, ll, MLA_DQ_PAD), BF16), jax.ShapeDtypeStruct((hh, ll, MLA_V), BF16)],
            compiler_params=_params(),
        )(qp, kv, kr, *tabs)

    def fwd_ctx(kv_c, kr_c, k_buf, v_buf):
        def body(kv_ref, kr_ref, k_in, v_in, k_ref, v_ref):
            kr_rot = jnp.where(rope_lanes(), kr_ref[...], 0.0).astype(BF16)
            put_kv(kv_ref, kr_rot, k_ref, v_ref)

        any_spec = pl.BlockSpec(memory_space=pl.ANY)
        off = n_lat // tm
        return pl.pallas_call(
            body, name=tag + "_ctx", grid=(n_ctx // tm,),
            in_specs=[rows(kv_c.shape[1]), rows(LANE), any_spec, any_spec],
            out_specs=[heads(MLA_DQ_PAD, off), heads(MLA_V, off)],
            out_shape=[jax.ShapeDtypeStruct(k_buf.shape, BF16), jax.ShapeDtypeStruct(v_buf.shape, BF16)],
            input_output_aliases={2: 0, 3: 1}, compiler_params=_params(),
        )(kv_c, kr_c, k_buf, v_buf)

    def take_kv(dk_ref, dv_ref, dkv_ref):
        dkr = jnp.zeros((tm, LANE), F32)
        for h in range(hh):
            dkv_ref[:, 2 * LANE * h:2 * LANE * h + MLA_NOPE] = dk_ref[h, :, :MLA_NOPE].astype(F32)
            dkv_ref[:, 2 * LANE * h + MLA_NOPE:2 * LANE * (h + 1)] = dv_ref[h].astype(F32)
            dkr = dkr + dk_ref[h, :, MLA_NOPE:].astype(F32)
        return jnp.where(rope_lanes(), dkr, 0.0)

    def bwd_lat(dqt, dk, dv, qp_width, kv_width):
        def body(dqt_ref, dk_ref, dv_ref, cos_ref, s1_ref, s2_ref, dqp_ref, dkv_ref, dkr_ref):
            cos, s1, s2 = cos_ref[...], s1_ref[...], s2_ref[...]
            keep = rope_lanes()
            for j in range(hh // 2):
                even = jnp.transpose(dqt_ref[2 * j]) * scale
                odd = jnp.transpose(dqt_ref[2 * j + 1]) * scale
                dqp_ref[:, MLA_NOPE * 2 * j:MLA_NOPE * (2 * j + 1)] = even[:, :MLA_NOPE]
                dqp_ref[:, MLA_NOPE * (2 * j + 1):MLA_NOPE * (2 * j + 2)] = odd[:, :MLA_NOPE]
                g = jnp.where(keep, even[:, MLA_NOPE:], pltpu.roll(odd[:, MLA_NOPE:], MLA_ROPE, 1))
                dqp_ref[:, rope0 + LANE * j:rope0 + LANE * (j + 1)] = _rope128_t(g, cos, s1, s2)
            dkr_ref[...] = jnp.where(keep, _rope128_t(take_kv(dk_ref, dv_ref, dkv_ref), cos, s1, s2), 0.0)

        return pl.pallas_call(
            body, name=tag + "_dlat", grid=(n_lat // tm,),
            in_specs=[pl.BlockSpec((hh, MLA_DQ_PAD, tm), lambda i: (0, 0, i)),
                      heads(MLA_DQ_PAD, 0), heads(MLA_V, 0), rows(LANE), rows(LANE), rows(LANE)],
            out_specs=[rows(qp_width), rows(kv_width), rows(LANE)],
            out_shape=[jax.ShapeDtypeStruct((n_lat, qp_width), F32), jax.ShapeDtypeStruct((n_lat, kv_width), F32),
                       jax.ShapeDtypeStruct((n_lat, LANE), F32)],
            compiler_params=_params(),
        )(dqt, dk, dv, *tabs)

    def bwd_ctx(dk, dv, kv_width):
        def body(dk_ref, dv_ref, dkv_ref, dkr_ref):
            dkr_ref[...] = take_kv(dk_ref, dv_ref, dkv_ref)

        off = n_lat // tm
        return pl.pallas_call(
            body, name=tag + "_dctx", grid=(n_ctx // tm,),
            in_specs=[heads(MLA_DQ_PAD, off), heads(MLA_V, off)],
            out_specs=[rows(kv_width), rows(LANE)],
            out_shape=[jax.ShapeDtypeStruct((n_ctx, kv_width), F32), jax.ShapeDtypeStruct((n_ctx, LANE), F32)],
            compiler_params=_params(),
        )(dk, dv)

    def pack(qp, kv, kr, kv_c, kr_c):
        q, k, v = fwd_lat(qp, kv, kr)
        k, v = fwd_ctx(kv_c, kr_c, k, v)
        return q, k, v

    def unpack(dqt, dk, dv):
        qp_width, kv_width = hh * (MLA_NOPE + MLA_ROPE), hh * (MLA_NOPE + MLA_V)
        dqp, dkv, dkr = bwd_lat(dqt, dk, dv, qp_width, kv_width)
        dkv_c, dkr_c = bwd_ctx(dk, dv, kv_width)
        return dqp, dkv, dkr, dkv_c, dkr_c

    return pack, unpack


def _make_mla(tag, n_lat, n_ctx):
    scale = (MLA_NOPE + MLA_ROPE) ** -0.5
    pack, unpack = _make_mla_pack(tag + "pack", n_lat, n_ctx, scale)
    attn_fwd, attn_delta, attn_bwd = _make_attention(tag, scale)

    @jax.custom_vjp
    def mla(qp, kv, kr, kv_c, kr_c):
        q, k, v = pack(qp, kv, kr, kv_c, kr_c)
        return attn_fwd(q, k, jnp.swapaxes(v, 1, 2))[0]

    def fwd(qp, kv, kr, kv_c, kr_c):
        q, k, v = pack(qp, kv, kr, kv_c, kr_c)
        o, lse = attn_fwd(q, k, jnp.swapaxes(v, 1, 2))
        return o, (q, k, v, o, lse)

    def bwd(res, do):
        q, k, v, o, lse = res
        delta = attn_delta(o, do, q.shape[0])
        dqt, dk, dv = attn_bwd(q, k, jnp.swapaxes(k, 1, 2), v, do, lse, delta)
        return unpack(dqt, dk, dv)

    mla.defvjp(fwd, bwd)
    return mla


def _make_attention(tag, scale):
    neg_big = -1e30
    log2e = 1.4426950408889634
    sub = 256

    def fwd_call(q, k, vt):
        hh, n, dq = q.shape
        dv, ll = vt.shape[1], vt.shape[2]
        tq, tk = _pick(n, 1024), _pick(ll, 1408)
        sb = sub if tk % sub == 0 else tk
        c2 = scale * log2e
        k_steps = ll // tk

        def body(q_ref, k_ref, vt_ref, o_ref, lse_ref, m_scr, l_scr, acc_scr, s_scr, p_scr):
            j = pl.program_id(2)

            @pl.when(j == 0)
            def _():
                m_scr[...] = jnp.full_like(m_scr, neg_big)
                l_scr[...] = jnp.zeros_like(l_scr)
                acc_scr[...] = jnp.zeros_like(acc_scr)

            q_t = q_ref[...]
            m_prev = m_scr[...]
            m_new = m_prev
            for kk in range(tk // sb):
                rows = slice(kk * sb, (kk + 1) * sb)
                s_t = _dot(k_ref[rows, :], q_t, NT)
                s_scr[rows, :] = s_t
                m_new = jnp.maximum(m_new, jnp.max(s_t, axis=0, keepdims=True))
            mc = m_new * c2
            l_part = jnp.zeros_like(m_new)
            for kk in range(tk // sb):
                rows = slice(kk * sb, (kk + 1) * sb)
                p_t = jnp.exp2(s_scr[rows, :] * c2 - mc)
                l_part = l_part + jnp.sum(p_t, axis=0, keepdims=True)
                p_scr[rows, :] = p_t.astype(BF16)
            alpha = jnp.exp2((m_prev - m_new) * c2)
            l_scr[...] = alpha * l_scr[...] + l_part
            acc_scr[...] = alpha * acc_scr[...] + _dot(vt_ref[...], p_scr[...], NN)
            m_scr[...] = m_new

            @pl.when(j == k_steps - 1)
            def _():
                o_ref[...] = jnp.transpose(acc_scr[...] / l_scr[...]).astype(BF16)
                lse_ref[...] = m_scr[...] * scale + jnp.log(l_scr[...])

        return pl.pallas_call(
            body, name=tag + "_fwd", grid=(hh, n // tq, k_steps),
            in_specs=[pl.BlockSpec((None, tq, dq), lambda h, i, j: (h, i, 0)),
                      pl.BlockSpec((None, tk, dq), lambda h, i, j: (h, j, 0)),
                      pl.BlockSpec((None, dv, tk), lambda h, i, j: (h, 0, j))],
            out_specs=[pl.BlockSpec((tq, dv), lambda h, i, j: (i, h)),
                       pl.BlockSpec((None, 1, tq), lambda h, i, j: (h, 0, i))],
            out_shape=[jax.ShapeDtypeStruct((n, hh * dv), BF16), jax.ShapeDtypeStruct((hh, 1, n), F32)],
            scratch_shapes=[pltpu.VMEM((1, tq), F32), pltpu.VMEM((1, tq), F32), pltpu.VMEM((dv, tq), F32),
                            pltpu.VMEM((tk, tq), F32), pltpu.VMEM((tk, tq), BF16)],
            compiler_params=_params(),
        )(q, k, vt)

    def delta_call(o, do, hh):
        n = o.shape[0]
        dv = o.shape[1] // hh
        tq = _pick(n, 1024)

        def body(o_ref, do_ref, d_ref):
            prod_t = jnp.transpose(o_ref[...].astype(F32) * do_ref[...].astype(F32))
            d_ref[...] = jnp.sum(prod_t, axis=0, keepdims=True)

        spec = pl.BlockSpec((tq, dv), lambda h, i: (i, h))
        return pl.pallas_call(
            body, name=tag + "_delta", grid=(hh, n // tq), in_specs=[spec, spec],
            out_specs=pl.BlockSpec((None, 1, tq), lambda h, i: (h, 0, i)),
            out_shape=jax.ShapeDtypeStruct((hh, 1, n), F32), compiler_params=_params(),
        )(o, do)

    def bwd_call(q, k, kt, v, do, lse, delta):
        hh, n, dq = q.shape
        ll, dv = k.shape[1], v.shape[2]
        tq, tk = _pick(n, 1024), _pick(ll, 1408)
        sb = tk
        c2 = scale * log2e
        q_steps = n // tq

        def body(q_ref, k_ref, kt_ref, v_ref, do_ref, lse_ref, d_ref, dqt_ref, dk_ref, dv_ref, dk_scr, dv_scr):
            j = pl.program_id(1)
            i = pl.program_id(2)

            @pl.when(i == 0)
            def _():
                dk_scr[...] = jnp.zeros_like(dk_scr)
                dv_scr[...] = jnp.zeros_like(dv_scr)

            q_t, do_t = q_ref[...], do_ref[...]
            lse2 = lse_ref[...] * log2e
            delta_t = d_ref[...]
            dq_part = None
            for kk in range(tk // sb):
                rows = slice(kk * sb, (kk + 1) * sb)
                s_t = _dot(k_ref[rows, :], q_t, NT)
                p_t = jnp.exp2(s_t * c2 - lse2)
                ds_t = p_t * (_dot(v_ref[rows, :], do_t, NT) - delta_t)
                dv_scr[rows, :] += _dot(p_t, do_t, NN)
                dk_scr[rows, :] += _dot(ds_t, q_t, NN)
                part = _dot(kt_ref[:, rows], ds_t, NN)
                dq_part = part if dq_part is None else dq_part + part
            cols = pl.ds(pl.multiple_of(i * tq, tq), tq)

            @pl.when(j == 0)
            def _():
                dqt_ref[:, cols] = dq_part

            @pl.when(j > 0)
            def _():
                dqt_ref[:, cols] += dq_part

            @pl.when(i == q_steps - 1)
            def _():
                dk_ref[...] = (dk_scr[...] * scale).astype(BF16)
                dv_ref[...] = dv_scr[...].astype(BF16)

        return pl.pallas_call(
            body, name=tag + "_bwd", grid=(hh, ll // tk, q_steps),
            in_specs=[pl.BlockSpec((None, tq, dq), lambda h, j, i: (h, i, 0)),
                      pl.BlockSpec((None, tk, dq), lambda h, j, i: (h, j, 0)),
                      pl.BlockSpec((None, dq, tk), lambda h, j, i: (h, 0, j)),
                      pl.BlockSpec((None, tk, dv), lambda h, j, i: (h, j, 0)),
                      pl.BlockSpec((tq, dv), lambda h, j, i: (i, h)),
                      pl.BlockSpec((None, 1, tq), lambda h, j, i: (h, 0, i)),
                      pl.BlockSpec((None, 1, tq), lambda h, j, i: (h, 0, i))],
            out_specs=[pl.BlockSpec((None, dq, n), lambda h, j, i: (h, 0, 0)),
                       pl.BlockSpec((None, tk, dq), lambda h, j, i: (h, j, 0)),
                       pl.BlockSpec((None, tk, dv), lambda h, j, i: (h, j, 0))],
            out_shape=[jax.ShapeDtypeStruct((hh, dq, n), F32), jax.ShapeDtypeStruct((hh, ll, dq), BF16),
                       jax.ShapeDtypeStruct((hh, ll, dv), BF16)],
            scratch_shapes=[pltpu.VMEM((tk, dq), F32), pltpu.VMEM((tk, dv), F32)],
            compiler_params=_params(),
        )(q, k, kt, v, do, lse, delta)

    return fwd_call, delta_call, bwd_call


def _loss_tile(x, g, tgt):
    r = lax.rsqrt(jnp.mean(x * x, axis=-1, keepdims=True) + RMS_EPS)
    err = x * r * g - tgt
    per_tok = jnp.mean(err * err, axis=-1, keepdims=True)
    return 0.5 * jnp.sum(per_tok, axis=0, keepdims=True)


def _make_final_loss(tag):
    def fwd_call(x, g, tgt):
        t, d = x.shape
        tm = _pick(t, 512, 16)

        def body(x_ref, g_ref, t_ref, l_ref):
            l_ref[...] = jnp.broadcast_to(_loss_tile(x_ref[...], g_ref[...], t_ref[...]), (1, LANE))

        parts = pl.pallas_call(
            body, name=tag + "_fwd", grid=(t // tm,),
            in_specs=[_row_spec(tm, d), _vec_spec(d), _row_spec(tm, d)],
            out_specs=pl.BlockSpec((None, 1, LANE), lambda i: (i, 0, 0)),
            out_shape=jax.ShapeDtypeStruct((t // tm, 1, LANE), F32), compiler_params=_params(),
        )(x, g, tgt)
        return jnp.sum(parts[:, 0, 0])

    def bwd_call(x, g, tgt, dl):
        t, d = x.shape
        tm = _pick(t, 256, 16)

        def body(x_ref, g_ref, t_ref, dl_ref, dx_ref, dg_ref):
            _, vjp = jax.vjp(_loss_tile, x_ref[...], g_ref[...], t_ref[...])
            dx, dg, _ = vjp(dl_ref[...])
            dx_ref[...] = dx

            @pl.when(pl.program_id(0) == 0)
            def _():
                dg_ref[...] = jnp.zeros_like(dg_ref)

            dg_ref[...] += dg

        return pl.pallas_call(
            body, name=tag + "_bwd", grid=(t // tm,),
            in_specs=[_row_spec(tm, d), _vec_spec(d), _row_spec(tm, d), pl.BlockSpec((1, 1), lambda i: (0, 0))],
            out_specs=[_row_spec(tm, d), _vec_spec(d)],
            out_shape=[jax.ShapeDtypeStruct((t, d), F32), jax.ShapeDtypeStruct((1, d), F32)],
            compiler_params=_params(),
        )(x, g, tgt, dl)

    @jax.custom_vjp
    def final_loss(x, g, tgt):
        return fwd_call(x, g, tgt)

    def fwd(x, g, tgt):
        return fwd_call(x, g, tgt), (x, g, tgt)

    def bwd(res, dl):
        x, g, tgt = res
        dx, dg = bwd_call(x, g, tgt, dl.reshape(1, 1).astype(F32))
        return dx, dg, jnp.zeros_like(tgt)

    final_loss.defvjp(fwd, bwd)
    return final_loss


def _exchange(arrays, gather, name):
    n = len(arrays)

    def body(*refs):
        ins, outs = refs[:n], refs[n:2 * n]
        send_sems, recv_sems, local_sems = refs[2 * n:]
        me = 4 * lax.axis_index("x") + 2 * lax.axis_index("y") + lax.axis_index("c")

        def remote(a, d, wait_side=False):
            peer = (me + d) % N_DEV
            origin = (me + N_DEV - d) % N_DEV
            src = ins[a] if gather else ins[a].at[peer]
            dst = outs[a].at[origin if wait_side else me]
            return pltpu.make_async_remote_copy(
                src_ref=src, dst_ref=dst, send_sem=send_sems.at[a, d - 1], recv_sem=recv_sems.at[a, d - 1],
                device_id=(peer // 4, (peer // 2) % 2, peer % 2), device_id_type=pl.DeviceIdType.MESH)

        def local(a):
            src = ins[a] if gather else ins[a].at[me]
            return pltpu.make_async_copy(src, outs[a].at[me], local_sems.at[a])

        for a in range(n):
            for d in range(1, N_DEV):
                remote(a, d).start()
            local(a).start()
        for a in range(n):
            local(a).wait()
            for d in range(1, N_DEV):
                remote(a, d, wait_side=True).wait_recv()
                remote(a, d).wait_send()

    out_shape = []
    for arr in arrays:
        shape = (N_DEV,) + arr.shape if gather else arr.shape
        out_shape.append(jax.ShapeDtypeStruct(shape, arr.dtype))
    any_spec = pl.BlockSpec(memory_space=pl.ANY)
    return pl.pallas_call(
        body, name=name, in_specs=[any_spec] * n, out_specs=[any_spec] * n, out_shape=out_shape,
        scratch_shapes=[pltpu.SemaphoreType.DMA((n, N_DEV - 1)), pltpu.SemaphoreType.DMA((n, N_DEV - 1)),
                        pltpu.SemaphoreType.DMA((n,))],
        compiler_params=pltpu.CompilerParams(has_side_effects=True),
    )(*arrays)


def _split_copy(ins, lands, send_sems, recv_sems, a, d, gather, wait_side):
    me = 4 * lax.axis_index("x") + 2 * lax.axis_index("y") + lax.axis_index("c")
    peer = (me + d) % N_DEV
    origin = (me + N_DEV - d) % N_DEV
    return pltpu.make_async_remote_copy(
        src_ref=ins[a] if gather else ins[a].at[peer], dst_ref=lands[a].at[origin if wait_side else me],
        send_sem=send_sems.at[a * (N_DEV - 1) + d - 1], recv_sem=recv_sems.at[a * (N_DEV - 1) + d - 1],
        device_id=(peer // 4, (peer // 2) % 2, peer % 2), device_id_type=pl.DeviceIdType.MESH)


def _exchange_start(srcs, lands, after, gather, name):
    n = len(srcs)

    def body(*refs):
        ins, lnd = refs[:n], refs[n:2 * n]
        send_sems, recv_sems = refs[2 * n + 1], refs[2 * n + 2]
        for a in range(n):
            for d in range(1, N_DEV):
                _split_copy(ins, lnd, send_sems, recv_sems, a, d, gather, False).start()

    hbm = pl.BlockSpec(memory_space=pltpu.HBM)
    sem = pl.BlockSpec(memory_space=pltpu.SEMAPHORE)
    bufs = [pltpu.with_memory_space_constraint(t, pltpu.HBM) for t in list(srcs) + list(lands) + [after]]
    res = pl.pallas_call(
        body, name=name,
        in_specs=[hbm] * (2 * n + 1), out_specs=[sem, sem] + [hbm] * (2 * n + 1),
        out_shape=[pltpu.SemaphoreType.DMA((n * (N_DEV - 1),)), pltpu.SemaphoreType.DMA((n * (N_DEV - 1),))]
        + [pltpu.HBM(t.shape, t.dtype) for t in bufs],
        input_output_aliases={i: 2 + i for i in range(2 * n + 1)},
        compiler_params=pltpu.CompilerParams(has_side_effects=pltpu.SideEffectType.DATAFLOW_SIDE_EFFECTING),
    )(*bufs)
    return res[0], res[1], res[2:2 + n], res[2 + n:2 + 2 * n], res[-1]


def _exchange_wait(send_sems, recv_sems, srcs, lands, after, gather, name):
    n = len(srcs)

    def body(*refs):
        ins, lnd = refs[:n], refs[n:2 * n]
        send_sems_ref, recv_sems_ref = refs[2 * n], refs[2 * n + 1]
        for a in range(n):
            for d in range(1, N_DEV):
                _split_copy(ins, lnd, send_sems_ref, recv_sems_ref, a, d, gather, False).wait_send()
                _split_copy(ins, lnd, send_sems_ref, recv_sems_ref, a, d, gather, True).wait_recv()

    hbm = pl.BlockSpec(memory_space=pltpu.HBM)
    sem = pl.BlockSpec(memory_space=pltpu.SEMAPHORE)
    bufs = list(srcs) + list(lands)
    res = pl.pallas_call(
        body, name=name,
        in_specs=[hbm] * (2 * n) + [sem, sem, pl.BlockSpec(memory_space=pl.ANY)],
        out_specs=[hbm] * (2 * n),
        out_shape=[pltpu.HBM(t.shape, t.dtype) for t in bufs],
        input_output_aliases={i: i for i in range(2 * n)},
        compiler_params=pltpu.CompilerParams(has_side_effects=pltpu.SideEffectType.DATAFLOW_SIDE_EFFECTING),
    )(*bufs, send_sems, recv_sems, after)
    return res[n:]


def _own_slot(block, me):
    empty = lax.empty((N_DEV,) + block.shape, block.dtype)
    return lax.dynamic_update_slice(empty, block[None], (me,) + (0,) * block.ndim)


def _coords():
    return lax.axis_index("x"), lax.axis_index("y"), lax.axis_index("c")


def _other_chips(x, y):
    return [(1 - x, y), (x, 1 - y), (1 - x, 1 - y)]


def _gather_two_level(arrays, name):
    n = len(arrays)

    def body(*refs):
        ins, outs = refs[:n], refs[n:2 * n]
        send_sems, recv_sems, local_sems = refs[2 * n:]
        x, y, c = _coords()
        me, sib = (x, y, c), (x, y, 1 - c)
        chips = _other_chips(x, y)

        def copy(a, k, block, to, from_input=False):
            slot = 4 * block[0] + 2 * block[1] + block[2]
            return pltpu.make_async_remote_copy(
                src_ref=ins[a] if from_input else outs[a].at[slot], dst_ref=outs[a].at[slot],
                send_sem=send_sems.at[a, k], recv_sem=recv_sems.at[a, k],
                device_id=to, device_id_type=pl.DeviceIdType.MESH)

        def local(a):
            return pltpu.make_async_copy(ins[a], outs[a].at[4 * x + 2 * y + c], local_sems.at[a])

        for a in range(n):
            for j, chip in enumerate(chips):
                copy(a, 1 + j, me, (*chip, c), True).start()
            copy(a, 0, me, sib, True).start()
            local(a).start()
        for a in range(n):
            for j, chip in enumerate(chips):
                copy(a, 1 + j, (*chip, c), me).wait_recv()
                copy(a, 4 + j, (*chip, c), sib).start()
        for a in range(n):
            copy(a, 0, sib, me).wait_recv()
            for j, chip in enumerate(chips):
                copy(a, 4 + j, (*chip, 1 - c), me).wait_recv()
            for k in range(N_DEV - 1):
                copy(a, k, me, sib, True).wait_send()
            local(a).wait()

    any_spec = pl.BlockSpec(memory_space=pl.ANY)
    return pl.pallas_call(
        body, name=name, in_specs=[any_spec] * n, out_specs=[any_spec] * n,
        out_shape=[jax.ShapeDtypeStruct((N_DEV,) + arr.shape, arr.dtype) for arr in arrays],
        scratch_shapes=[pltpu.SemaphoreType.DMA((n, N_DEV - 1)), pltpu.SemaphoreType.DMA((n, N_DEV - 1)),
                        pltpu.SemaphoreType.DMA((n,))],
        compiler_params=pltpu.CompilerParams(has_side_effects=True),
    )(*arrays)


def _swap_sibling(arrays, name):
    n = len(arrays)
    n_chip = N_DEV // 2

    def body(*refs):
        ins, outs = refs[:n], refs[n:2 * n]
        send_sems, recv_sems = refs[2 * n:]
        x, y, c = _coords()

        def copy(a, q):
            return pltpu.make_async_remote_copy(
                src_ref=ins[a].at[2 * q + (1 - c)], dst_ref=outs[a].at[q],
                send_sem=send_sems.at[a, q], recv_sem=recv_sems.at[a, q],
                device_id=(x, y, 1 - c), device_id_type=pl.DeviceIdType.MESH)

        for a in range(n):
            for q in range(n_chip):
                copy(a, q).start()
        for a in range(n):
            for q in range(n_chip):
                copy(a, q).wait_recv()
                copy(a, q).wait_send()

    any_spec = pl.BlockSpec(memory_space=pl.ANY)
    return pl.pallas_call(
        body, name=name, in_specs=[any_spec] * n, out_specs=[any_spec] * n,
        out_shape=[jax.ShapeDtypeStruct((n_chip,) + arr.shape[1:], arr.dtype) for arr in arrays],
        scratch_shapes=[pltpu.SemaphoreType.DMA((n, n_chip)), pltpu.SemaphoreType.DMA((n, n_chip))],
        compiler_params=pltpu.CompilerParams(has_side_effects=True),
    )(*arrays)


def _scatter_chips(arrays, name):
    n = len(arrays)
    n_chip = N_DEV // 2

    def body(*refs):
        ins, outs = refs[:n], refs[n:2 * n]
        send_sems, recv_sems, local_sems = refs[2 * n:]
        x, y, c = _coords()
        q_me = 2 * x + y
        chips = _other_chips(x, y)

        def copy(a, j, wait_side=False):
            q_peer = 2 * chips[j][0] + chips[j][1]
            return pltpu.make_async_remote_copy(
                src_ref=ins[a].at[q_peer], dst_ref=outs[a].at[q_peer if wait_side else q_me],
                send_sem=send_sems.at[a, j], recv_sem=recv_sems.at[a, j],
                device_id=(*chips[j], c), device_id_type=pl.DeviceIdType.MESH)

        def local(a):
            return pltpu.make_async_copy(ins[a].at[q_me], outs[a].at[q_me], local_sems.at[a])

        for a in range(n):
            for j in range(n_chip - 1):
                copy(a, j).start()
            local(a).start()
        for a in range(n):
            local(a).wait()
            for j in range(n_chip - 1):
                copy(a, j, wait_side=True).wait_recv()
                copy(a, j).wait_send()

    any_spec = pl.BlockSpec(memory_space=pl.ANY)
    return pl.pallas_call(
        body, name=name, in_specs=[any_spec] * n, out_specs=[any_spec] * n,
        out_shape=[jax.ShapeDtypeStruct(arr.shape, arr.dtype) for arr in arrays],
        scratch_shapes=[pltpu.SemaphoreType.DMA((n, n_chip - 1)), pltpu.SemaphoreType.DMA((n, n_chip - 1)),
                        pltpu.SemaphoreType.DMA((n,))],
        compiler_params=pltpu.CompilerParams(has_side_effects=True),
    )(*arrays)


def _pair_add(full, theirs, core, name):
    n_chip, r, cn = theirs.shape
    tr = _pick(r, max(16, (2 * 1024 * 1024) // (4 * cn) // 16 * 16), 16)

    def body(core_ref, mine_ref, theirs_ref, o_ref):
        o_ref[...] = (mine_ref[...].astype(F32) + theirs_ref[...].astype(F32)).astype(BF16)

    tile = pl.BlockSpec((None, tr, cn), lambda q, i, core_ref: (q, i, 0))
    return pl.pallas_call(
        body, name=name,
        grid_spec=pltpu.PrefetchScalarGridSpec(
            num_scalar_prefetch=1, grid=(n_chip, r // tr),
            in_specs=[pl.BlockSpec((None, tr, cn), lambda q, i, core_ref: (2 * q + core_ref[0], i, 0)), tile],
            out_specs=tile),
        out_shape=jax.ShapeDtypeStruct(theirs.shape, BF16), compiler_params=_params(),
    )(core, full, theirs)


def _make_gather_op(tag):
    @jax.custom_vjp
    def gather_op(xl):
        return _exchange([xl], True, tag + "_gather")[0]

    def fwd(xl):
        return gather_op(xl), None

    def bwd(_, g):
        return (jnp.sum(_exchange([g], False, tag + "_scatter")[0], axis=0),)

    gather_op.defvjp(fwd, bwd)
    return gather_op


def _adamw(gstack, w, m, v, name):
    s, r, cn = gstack.shape
    tr = _pick(r, max(8, (2 * 1024 * 1024) // (4 * cn) // 8 * 8), 8)
    c1 = 1.0 - ADAM_B1 ** ADAM_STEP
    c2 = 1.0 - ADAM_B2 ** ADAM_STEP

    def body(g_ref, w_ref, m_ref, v_ref, go_ref, d_ref, mo_ref, vo_ref):
        g = g_ref[0].astype(F32)
        for q in range(1, s):
            g = g + g_ref[q].astype(F32)
        m_new = ADAM_B1 * m_ref[...] + (1.0 - ADAM_B1) * g
        v_new = ADAM_B2 * v_ref[...] + (1.0 - ADAM_B2) * (g * g)
        go_ref[...] = g
        mo_ref[...] = m_new
        vo_ref[...] = v_new
        d_ref[...] = -ADAM_LR * ((m_new / c1) / (jnp.sqrt(v_new / c2) + ADAM_EPS) + ADAM_WD * w_ref[...])

    tile = pl.BlockSpec((tr, cn), lambda i: (i, 0))
    out = jax.ShapeDtypeStruct((r, cn), F32)
    return pl.pallas_call(
        body, name=name, grid=(r // tr,),
        in_specs=[pl.BlockSpec((s, tr, cn), lambda i: (0, i, 0)), tile, tile, tile],
        out_specs=[tile, tile, tile, tile], out_shape=[out, out, out, out],
        compiler_params=_params(),
    )(gstack, w, m, v)


def _rope_tables(pos, dim, base):
    inv = base ** (-jnp.arange(0, dim, 2, dtype=F32) / dim)
    ang = pos.astype(F32)[:, None] * inv[None, :]
    return jnp.cos(ang)[:, None, :], jnp.sin(ang)[:, None, :]


def _rotate(x, cos, sin):
    x1, x2 = jnp.split(x, 2, axis=-1)
    return jnp.concatenate([x1 * cos - x2 * sin, x2 * cos + x1 * sin], axis=-1)


def _heads(t, h):
    return jnp.swapaxes(t.reshape(t.shape[0], h, t.shape[1] // h), 0, 1)


def _cols_from_stack(w):
    return jnp.swapaxes(w, 0, 1).reshape(w.shape[1], N_DEV * w.shape[2])


def _stage_a(p, ctx, silu_c_all, me):
    x = p["x"]
    d = x.shape[1]
    n_a = p["ada_w"].shape[1]

    a_in = jnp.concatenate([silu_c_all, jax.nn.silu(p["c_ctx"])[None, :], jnp.zeros((7, d), F32)], axis=0)
    b_loc = lax.dynamic_slice(p["ada_b"], (0, me * n_a), (1, n_a))
    r_loc = _make_small_mm("ada")(a_in, p["ada_w"]) + b_loc
    r_full = _make_gather_op("ada")(r_loc)
    m_lat = lax.dynamic_index_in_dim(r_full, me, axis=1, keepdims=False).reshape(N_MOD, 1, d)
    m_ctx = r_full[:, N_DEV, :].reshape(N_MOD, 1, d)

    x1 = _make_ffn_block("ffn1")(x, p["norm1_g"], m_lat[0], m_lat[1], m_lat[2], p["ffn1_w_in"], p["ffn1_w_out"])
    c1 = _make_ffn_block("ffn1c")(ctx, p["norm1_g"], m_ctx[0], m_ctx[1], m_ctx[2], p["ffn1_w_in"], p["ffn1_w_out"])
    return x1, c1, m_lat, m_ctx


def _stage_b(p, x1, c1, m_lat, m_ctx):
    n_lat, d = x1.shape
    n_ctx = c1.shape[0]
    w_mix = jnp.pad(_cols_from_stack(p["mix_w_in"]), ((0, 0), (0, MIX_IN_PAD - MIX_IN)))
    proj = _make_norm_proj("mix")(x1, p["norm2_g"], m_lat[3], m_lat[4], w_mix)
    proj_c = _make_norm_proj("mixc")(c1, p["norm2_g"], m_ctx[3], m_ctx[4], w_mix)
    idx = [0]
    for s in SPLITS:
        idx.append(idx[-1] + s)
    rq, rk, rv, rg, cq, ckv = [proj[:, idx[i]:idx[i + 1]] for i in range(6)]
    crk, crv, cckv = [proj_c[:, idx[i]:idx[i + 1]] for i in (1, 2, 5)]
    kr = proj[:, idx[6]:idx[6] + LANE]
    ckr = proj_c[:, idx[6]:idx[6] + LANE]

    zq = jnp.zeros((1, MLA_Q_RANK), F32)
    zkv = jnp.zeros((1, MLA_KV_RANK), F32)
    w_uq3 = _cols_from_stack(p["mla_w_uq"]).reshape(MLA_Q_RANK, MLA_HEADS, MLA_NOPE + MLA_ROPE)
    w_uq = jnp.concatenate([w_uq3[:, :, :MLA_NOPE].reshape(MLA_Q_RANK, -1),
                            w_uq3[:, :, MLA_NOPE:].reshape(MLA_Q_RANK, -1)], axis=1)
    w_ukv = _cols_from_stack(p["mla_w_ukv"])
    q = _make_norm_proj("uq")(cq, p["mla_q_norm_g"], zq, zq, w_uq)
    kv = _make_norm_proj("ukv")(ckv, p["mla_kv_norm_g"], zkv, zkv, w_ukv)
    kv_c = _make_norm_proj("ukvc")(cckv, p["mla_kv_norm_g"], zkv, zkv, w_ukv)

    lg_f = jax.nn.log_sigmoid(p["ret_decay_fwd"][0])
    lg_b = jax.nn.log_sigmoid(p["ret_decay_bwd"][0])
    ret_tab = _rope_tables(jnp.arange(n_lat), RET_DK, RET_ROPE_BASE)
    rq_h = jnp.swapaxes(_rotate(rq.reshape(n_lat, RET_HEADS, RET_DK), *ret_tab), 0, 1)
    rk_h = jnp.swapaxes(_rotate((rk * (RET_DK ** -0.5)).reshape(n_lat, RET_HEADS, RET_DK), *ret_tab), 0, 1)
    rv_h = _heads(rv, RET_HEADS)
    crk_h = _heads(crk * (RET_DK ** -0.5), RET_HEADS)
    crv_h = _heads(crv, RET_HEADS)
    zero_q = jnp.zeros((RET_HEADS, n_ctx, RET_DK), F32)

    def lanes(lg):
        return jnp.broadcast_to(lg[:, None, None], (RET_HEADS, 1, LANE))

    y_f = _make_ret_dir("retf", False)(
        jnp.concatenate([zero_q, rq_h], axis=1), jnp.concatenate([crk_h, rk_h], axis=1),
        jnp.concatenate([crv_h, rv_h], axis=1), lanes(lg_f))[:, n_ctx:]
    y_b = _make_ret_dir("retb", True)(
        jnp.concatenate([rq_h, zero_q], axis=1), jnp.concatenate([rk_h, crk_h], axis=1),
        jnp.concatenate([rv_h, crv_h], axis=1), lanes(lg_b))[:, :n_lat]
    ret_o = _make_ret_out("reto")(y_f + y_b, rg)

    mla_o = _make_mla("mla", n_lat, n_ctx)(q, kv, kr, kv_c, ckr)

    w_mo = p["mix_w_out"].reshape(-1, d)
    return _make_res_proj("mixo")(jnp.concatenate([ret_o, mla_o], axis=-1), w_mo, x1, m_lat[5])


def _stage_c(p, x2, m_lat, tgt):
    x3 = _make_ffn_block("ffn2")(x2, p["norm3_g"], m_lat[6], m_lat[7], m_lat[8], p["ffn2_w_in"], p["ffn2_w_out"])
    return _make_final_loss("loss")(x3, p["final_norm_g"], tgt)


FIRST = ("ffn1_w_in", "ffn1_w_out")
MID = ("mix_w_in", "mla_w_uq", "mla_w_ukv", "mix_w_out")
LAST = ("ffn2_w_in", "ffn2_w_out")
BIG = FIRST + MID + LAST
SMALL = ("c_ctx", "ada_b", "norm1_g", "norm2_g", "ret_decay_fwd", "ret_decay_bwd", "mla_q_norm_g",
         "mla_kv_norm_g", "norm3_g", "final_norm_g")
WEIGHTS = ("c_ctx", "ada_w", "ada_b", "norm1_g", "ffn1_w_in", "ffn1_w_out", "norm2_g", "mix_w_in", "ret_decay_fwd",
           "ret_decay_bwd", "mla_q_norm_g", "mla_w_uq", "mla_kv_norm_g", "mla_w_ukv", "mix_w_out", "norm3_g",
           "ffn2_w_in", "ffn2_w_out", "final_norm_g")


def _pack(parts):
    flat = jnp.concatenate([t.reshape(-1) for t in parts])
    pad = (-flat.shape[0]) % LANE
    return jnp.pad(flat, (0, pad)).reshape(1, -1)


def _unpack(flat, like):
    out, off = [], 0
    for t in like:
        out.append(flat[0, off:off + t.size].reshape(t.shape))
        off += t.size
    return out


def kernel(x, c, ctx, c_ctx, ada_w, ada_b, norm1_g, ffn1_w_in, ffn1_w_out, norm2_g, mix_w_in, ret_decay_fwd, ret_decay_bwd, mla_q_norm_g, mla_w_uq, mla_kv_norm_g, mla_w_ukv, mix_w_out, norm3_g, ffn2_w_in, ffn2_w_out, final_norm_g, loss_target, m_c_ctx, m_ada_w, m_ada_b, m_norm1_g, m_ffn1_w_in, m_ffn1_w_out, m_norm2_g, m_mix_w_in, m_ret_decay_fwd, m_ret_decay_bwd, m_mla_q_norm_g, m_mla_w_uq, m_mla_kv_norm_g, m_mla_w_ukv, m_mix_w_out, m_norm3_g, m_ffn2_w_in, m_ffn2_w_out, m_final_norm_g, v_c_ctx, v_ada_w, v_ada_b, v_norm1_g, v_ffn1_w_in, v_ffn1_w_out, v_norm2_g, v_mix_w_in, v_ret_decay_fwd, v_ret_decay_bwd, v_mla_q_norm_g, v_mla_w_uq, v_mla_kv_norm_g, v_mla_w_ukv, v_mix_w_out, v_norm3_g, v_ffn2_w_in, v_ffn2_w_out, v_final_norm_g):
    w = dict(c_ctx=c_ctx, ada_w=ada_w, ada_b=ada_b, norm1_g=norm1_g, ffn1_w_in=ffn1_w_in, ffn1_w_out=ffn1_w_out,
             norm2_g=norm2_g, mix_w_in=mix_w_in, ret_decay_fwd=ret_decay_fwd, ret_decay_bwd=ret_decay_bwd,
             mla_q_norm_g=mla_q_norm_g, mla_w_uq=mla_w_uq, mla_kv_norm_g=mla_kv_norm_g, mla_w_ukv=mla_w_ukv,
             mix_w_out=mix_w_out, norm3_g=norm3_g, ffn2_w_in=ffn2_w_in, ffn2_w_out=ffn2_w_out,
             final_norm_g=final_norm_g)
    mom = dict(c_ctx=m_c_ctx, ada_w=m_ada_w, ada_b=m_ada_b, norm1_g=m_norm1_g, ffn1_w_in=m_ffn1_w_in,
               ffn1_w_out=m_ffn1_w_out, norm2_g=m_norm2_g, mix_w_in=m_mix_w_in, ret_decay_fwd=m_ret_decay_fwd,
               ret_decay_bwd=m_ret_decay_bwd, mla_q_norm_g=m_mla_q_norm_g, mla_w_uq=m_mla_w_uq,
               mla_kv_norm_g=m_mla_kv_norm_g, mla_w_ukv=m_mla_w_ukv, mix_w_out=m_mix_w_out, norm3_g=m_norm3_g,
               ffn2_w_in=m_ffn2_w_in, ffn2_w_out=m_ffn2_w_out, final_norm_g=m_final_norm_g)
    var = dict(c_ctx=v_c_ctx, ada_w=v_ada_w, ada_b=v_ada_b, norm1_g=v_norm1_g, ffn1_w_in=v_ffn1_w_in,
               ffn1_w_out=v_ffn1_w_out, norm2_g=v_norm2_g, mix_w_in=v_mix_w_in, ret_decay_fwd=v_ret_decay_fwd,
               ret_decay_bwd=v_ret_decay_bwd, mla_q_norm_g=v_mla_q_norm_g, mla_w_uq=v_mla_w_uq,
               mla_kv_norm_g=v_mla_kv_norm_g, mla_w_ukv=v_mla_w_ukv, mix_w_out=v_mix_w_out, norm3_g=v_norm3_g,
               ffn2_w_in=v_ffn2_w_in, ffn2_w_out=v_ffn2_w_out, final_norm_g=v_final_norm_g)
    me = 4 * lax.axis_index("x") + 2 * lax.axis_index("y") + lax.axis_index("c")

    shard = {k: w[k][0].astype(BF16) for k in BIG}
    first = _gather_two_level([shard[k] for k in FIRST] + [jax.nn.silu(c)], "weights_gather")
    silu_c_all = first[-1][:, 0, :]
    mid_start = _exchange_start([shard[k] for k in MID], [_own_slot(shard[k], me) for k in MID], first[0], True,
                                "mixer_weights_start")
    last_start = _exchange_start([shard[k] for k in LAST], [_own_slot(shard[k], me) for k in LAST], mid_start[4],
                                 True, "ffn2_weights_start")

    pa = dict(zip(FIRST, (last_start[4], first[1])), x=x[0], ada_w=ada_w[0], c_ctx=c_ctx, ada_b=ada_b,
              norm1_g=norm1_g)
    (x1, c1, m_lat, m_ctx), vjp_a = jax.vjp(lambda q: _stage_a(q, ctx[0], silu_c_all, me), pa)

    mid = _exchange_wait(mid_start[0], mid_start[1], mid_start[2], mid_start[3], x1, True, "mixer_weights_wait")
    pb = dict(zip(MID, mid))
    for k in ("norm2_g", "mla_q_norm_g", "mla_kv_norm_g", "ret_decay_fwd", "ret_decay_bwd"):
        pb[k] = w[k]
    x2, vjp_b = jax.vjp(_stage_b, pb, x1, c1, m_lat, m_ctx)

    last = _exchange_wait(last_start[0], last_start[1], last_start[2], last_start[3], x2, True, "ffn2_weights_wait")
    pc = dict(zip(LAST, last), norm3_g=norm3_g, final_norm_g=final_norm_g[None, :])
    loss_local, vjp_c = jax.vjp(lambda q, t, m: _stage_c(q, t, m, loss_target[0]), pc, x2, m_lat)

    gc, dx2, dm_c = vjp_c(jnp.ones((), F32))
    last_scat = _exchange_start([gc[k] for k in LAST],
                                [_own_slot(lax.dynamic_index_in_dim(gc[k], me, 0, False), me) for k in LAST],
                                dx2, False, "ffn2_grads_start")
    gb, dx1, dc1, dm_b, dmc_b = vjp_b(last_scat[4])
    mid_scat = _exchange_start([gb[k] for k in MID],
                               [_own_slot(lax.dynamic_index_in_dim(gb[k], me, 0, False), me) for k in MID],
                               dx1, False, "mixer_grads_start")
    (ga,) = vjp_a((mid_scat[4], dc1, dm_b + dm_c, dmc_b))
    grads = {**ga, **gb, **gc}
    grads["final_norm_g"] = grads["final_norm_g"][0]

    core = lax.axis_index("c").astype(jnp.int32).reshape(1)
    full = [grads[k] for k in FIRST]
    theirs = _swap_sibling(full, "grads_swap")
    paired = [_pair_add(f, t, core, "grads_pair_" + k) for k, f, t in zip(FIRST, full, theirs)]
    exchanged = dict(zip(FIRST, _scatter_chips(paired, "grads_scatter")))
    exchanged.update(zip(LAST, _exchange_wait(last_scat[0], last_scat[1], last_scat[2], last_scat[3], grads["x"],
                                              False, "ffn2_grads_wait")))
    exchanged.update(zip(MID, _exchange_wait(mid_scat[0], mid_scat[1], mid_scat[2], mid_scat[3], grads["x"],
                                             False, "mixer_grads_wait")))
    zero1 = [jnp.zeros((1,), F32)]
    small_like = zero1 + [w[k] for k in SMALL]
    small_all = _exchange([_pack([loss_local.reshape(1)] + [grads[k] for k in SMALL])], True, "small_grads_gather")[0]
    loss = jnp.sum(small_all[:, 0, 0])

    out_g, out_d, out_m, out_v = {}, {}, {}, {}

    def update(name, gstack, shape2d):
        res = _adamw(gstack, w[name].reshape(shape2d), mom[name].reshape(shape2d), var[name].reshape(shape2d),
                     "adamw_" + name)
        out_g[name], out_d[name], out_m[name], out_v[name] = [t.reshape(w[name].shape) for t in res]

    for k in BIG:
        update(k, exchanged[k], exchanged[k].shape[1:])
    update("ada_w", grads["ada_w"][None], ada_w.shape[1:])
    res = _adamw(small_all, _pack(small_like), _pack(zero1 + [mom[k] for k in SMALL]),
                 _pack(zero1 + [var[k] for k in SMALL]), "adamw_small")
    for dst, flat in zip((out_g, out_d, out_m, out_v), res):
        for k, t in zip(SMALL, _unpack(flat, small_like)[1:]):
            dst[k] = t

    return (loss, grads["x"][None], *[out_g[k] for k in WEIGHTS], *[out_d[k] for k in WEIGHTS],
            *[out_m[k] for k in WEIGHTS], *[out_v[k] for k in WEIGHTS])
```

```python
import functools

import jax
import jax.numpy as jnp
from jax import lax
from jax.experimental import pallas as pl
from jax.experimental.pallas import tpu as pltpu

F32 = jnp.float32
BF16 = jnp.bfloat16

N_DEV = 8
MESH_AXES = ("x", "y", "c")

GRID_W = 64
N_MOD = 9
RET_HEADS = 8
RET_DK = 64
RET_DV = 128
RET_CHUNK = 256
RET_ROPE_BASE = 10000.0
MLA_HEADS = 8
MLA_Q_RANK = 512
MLA_KV_RANK = 256
MLA_NOPE = 128
MLA_ROPE = 64
MLA_V = 128
AXIAL_BASE = 10000.0
RMS_EPS = 1e-6
GN_EPS = 1e-5
SPLITS = (RET_HEADS * RET_DK, RET_HEADS * RET_DK, RET_HEADS * RET_DV, RET_HEADS * RET_DV,
          MLA_Q_RANK, MLA_KV_RANK, MLA_ROPE)
MIX_IN = sum(SPLITS)
MIX_IN_PAD = 4096

ADAM_LR = 0.001
ADAM_B1 = 0.9
ADAM_B2 = 0.999
ADAM_EPS = 1e-08
ADAM_WD = 0.01
ADAM_STEP = 10

LANE = 128
VMEM_LIMIT_BYTES = 56 * 1024 * 1024

NN = ((1,), (0,))
NT = ((1,), (1,))
TN = ((0,), (0,))


def _pick(dim, target, align=LANE):
    t = min(dim, target)
    t -= t % align
    while t >= align:
        if dim % t == 0:
            return t
        t -= align
    return dim


def _params():
    return pltpu.CompilerParams(vmem_limit_bytes=VMEM_LIMIT_BYTES)


def _dot(a, b, dims):
    return lax.dot_general(a.astype(BF16), b.astype(BF16), (dims, ((), ())), preferred_element_type=F32)


def _mm_call(name, grid, ins, pairs, outs, acc_shapes, epilogue):
    n_in, n_out = len(ins), len(outs)
    k_axis = len(grid) - 1
    k_steps = grid[k_axis]

    def body(*refs):
        in_refs = refs[:n_in]
        out_refs = refs[n_in:n_in + n_out]
        accs = refs[n_in + n_out:]
        k = pl.program_id(k_axis)

        @pl.when(k == 0)
        def _():
            for acc in accs:
                acc[...] = jnp.zeros_like(acc)

        for ai, bi, dims, ci in pairs:
            accs[ci][...] += _dot(in_refs[ai][...], in_refs[bi][...], dims)

        @pl.when(k == k_steps - 1)
        def _():
            epilogue([acc[...] for acc in accs], in_refs, out_refs)

    res = pl.pallas_call(
        body, name=name, grid=grid,
        in_specs=[s for _, s in ins], out_specs=[s for _, s in outs],
        out_shape=[s for s, _ in outs],
        scratch_shapes=[pltpu.VMEM(s, F32) for s in acc_shapes],
        compiler_params=_params(),
    )(*[a for a, _ in ins])
    return res


def _matmul(a, b, mode, out_dtype, name, tm=1024, tn=1024, tk=512):
    if mode == "nn":
        (m, kd), n = a.shape, b.shape[1]
    elif mode == "nt":
        (m, kd), n = a.shape, b.shape[0]
    else:
        (kd, m), n = a.shape, b.shape[1]
    tm, tn = _pick(m, tm, 16), _pick(n, tn)
    tk = _pick(kd, tk) if mode != "tn" else _pick(kd, tk, 16)
    if mode == "nn":
        a_spec = pl.BlockSpec((tm, tk), lambda i, j, k: (i, k))
        b_spec = pl.BlockSpec((tk, tn), lambda i, j, k: (k, j))
        dims = NN
    elif mode == "nt":
        a_spec = pl.BlockSpec((tm, tk), lambda i, j, k: (i, k))
        b_spec = pl.BlockSpec((tn, tk), lambda i, j, k: (j, k))
        dims = NT
    else:
        a_spec = pl.BlockSpec((tk, tm), lambda i, j, k: (k, i))
        b_spec = pl.BlockSpec((tk, tn), lambda i, j, k: (k, j))
        dims = TN

    def epilogue(accs, in_refs, out_refs):
        out_refs[0][...] = accs[0].astype(out_dtype)

    return _mm_call(
        name, (m // tm, n // tn, kd // tk), [(a, a_spec), (b, b_spec)], [(0, 1, dims, 0)],
        [(jax.ShapeDtypeStruct((m, n), out_dtype), pl.BlockSpec((tm, tn), lambda i, j, k: (i, j)))],
        [(tm, tn)], epilogue)[0]


def _norm_mod_tile(x, ng, sc, sh):
    r = lax.rsqrt(jnp.mean(x * x, axis=-1, keepdims=True) + RMS_EPS)
    return (x * r * ng) * (1.0 + sc) + sh


def _row_spec(tm, d):
    return pl.BlockSpec((tm, d), lambda i: (i, 0))


def _vec_spec(d):
    return pl.BlockSpec((1, d), lambda i: (0, 0))


def _norm_mod_fwd(x, ng, sc, sh, name):
    t, d = x.shape
    tm = _pick(t, 512, 16)

    def body(x_ref, ng_ref, sc_ref, sh_ref, h_ref):
        h_ref[...] = _norm_mod_tile(x_ref[...], ng_ref[...], sc_ref[...], sh_ref[...]).astype(BF16)

    return pl.pallas_call(
        body, name=name, grid=(t // tm,),
        in_specs=[_row_spec(tm, d), _vec_spec(d), _vec_spec(d), _vec_spec(d)],
        out_specs=_row_spec(tm, d), out_shape=jax.ShapeDtypeStruct((t, d), BF16),
        compiler_params=_params(),
    )(x, ng, sc, sh)


def _norm_mod_bwd(x, ng, sc, sh, dh, dres, name):
    t, d = x.shape
    tm = _pick(t, 256, 16)
    has_res = dres is not None

    def body(*refs):
        if has_res:
            x_ref, ng_ref, sc_ref, sh_ref, dh_ref, dres_ref, dx_ref, dng_ref, dsc_ref, dsh_ref = refs
        else:
            x_ref, ng_ref, sc_ref, sh_ref, dh_ref, dx_ref, dng_ref, dsc_ref, dsh_ref = refs
        _, vjp = jax.vjp(_norm_mod_tile, x_ref[...], ng_ref[...], sc_ref[...], sh_ref[...])
        dx, dng, dsc, dsh = vjp(dh_ref[...].astype(F32))
        if has_res:
            dx = dx + dres_ref[...]
        dx_ref[...] = dx

        @pl.when(pl.program_id(0) == 0)
        def _():
            dng_ref[...] = jnp.zeros_like(dng_ref)
            dsc_ref[...] = jnp.zeros_like(dsc_ref)
            dsh_ref[...] = jnp.zeros_like(dsh_ref)

        dng_ref[...] += dng
        dsc_ref[...] += dsc
        dsh_ref[...] += dsh

    ins = [x, ng, sc, sh, dh] + ([dres] if has_res else [])
    in_specs = [_row_spec(tm, d), _vec_spec(d), _vec_spec(d), _vec_spec(d), _row_spec(tm, d)]
    in_specs += [_row_spec(tm, d)] if has_res else []
    vec = jax.ShapeDtypeStruct((1, d), F32)
    return pl.pallas_call(
        body, name=name, grid=(t // tm,), in_specs=in_specs,
        out_specs=[_row_spec(tm, d), _vec_spec(d), _vec_spec(d), _vec_spec(d)],
        out_shape=[jax.ShapeDtypeStruct((t, d), F32), vec, vec, vec],
        compiler_params=_params(),
    )(*ins)


def _res_mm_fwd(a, w, x, gate, coef, name):
    t, kd = a.shape
    d = w.shape[1]
    tm, tn, tk = _pick(t, 1024, 16), _pick(d, 1024), _pick(kd, 2816)

    def epilogue(accs, in_refs, out_refs):
        f = accs[0]
        out_refs[0][...] = in_refs[2][...] + (coef * in_refs[3][...]) * f
        out_refs[1][...] = f.astype(BF16)

    tile = pl.BlockSpec((tm, tn), lambda i, j, k: (i, j))
    return _mm_call(
        name, (t // tm, d // tn, kd // tk),
        [(a, pl.BlockSpec((tm, tk), lambda i, j, k: (i, k))), (w, pl.BlockSpec((tk, tn), lambda i, j, k: (k, j))),
         (x, tile), (gate, pl.BlockSpec((1, tn), lambda i, j, k: (0, j)))],
        [(0, 1, NN, 0)],
        [(jax.ShapeDtypeStruct((t, d), F32), tile), (jax.ShapeDtypeStruct((t, d), BF16), tile)],
        [(tm, tn)], epilogue)


def _gate_bwd(dxo, f, gate, coef, name):
    t, d = dxo.shape
    tm = _pick(t, 512, 16)

    def body(dxo_ref, f_ref, gate_ref, df_ref, dgate_ref):
        dxo_t = dxo_ref[...]
        df_ref[...] = ((coef * gate_ref[...]) * dxo_t).astype(BF16)

        @pl.when(pl.program_id(0) == 0)
        def _():
            dgate_ref[...] = jnp.zeros_like(dgate_ref)

        dgate_ref[...] += coef * jnp.sum(dxo_t * f_ref[...].astype(F32), axis=0, keepdims=True)

    return pl.pallas_call(
        body, name=name, grid=(t // tm,),
        in_specs=[_row_spec(tm, d), _row_spec(tm, d), _vec_spec(d)],
        out_specs=[_row_spec(tm, d), _vec_spec(d)],
        out_shape=[jax.ShapeDtypeStruct((t, d), BF16), jax.ShapeDtypeStruct((1, d), F32)],
        compiler_params=_params(),
    )(dxo, f, gate)


def _ffn_in_fwd(h, w_in, name):
    t, d = h.shape
    n = w_in.shape[2]
    half = N_DEV // 2
    f = half * n
    tm = _pick(t, 512, 16)

    def epilogue(accs, in_refs, out_refs):
        g, u = accs
        out_refs[0][...] = (g * jax.nn.sigmoid(g) * u).astype(BF16)
        out_refs[1][0] = g.astype(BF16)
        out_refs[1][1] = u.astype(BF16)

    return _mm_call(
        name, (half, t // tm, 1),
        [(h, pl.BlockSpec((tm, d), lambda j, i, k: (i, 0))),
         (w_in, pl.BlockSpec((None, d, n), lambda j, i, k: (j, 0, 0))),
         (w_in, pl.BlockSpec((None, d, n), lambda j, i, k: (j + half, 0, 0)))],
        [(0, 1, NN, 0), (0, 2, NN, 1)],
        [(jax.ShapeDtypeStruct((t, f), BF16), pl.BlockSpec((tm, n), lambda j, i, k: (i, j))),
         (jax.ShapeDtypeStruct((2, t, f), BF16), pl.BlockSpec((2, tm, n), lambda j, i, k: (0, i, j)))],
        [(tm, n), (tm, n)], epilogue)


def _ffn_da_bwd(df, w_out2d, gu, name):
    t, d = df.shape
    f = w_out2d.shape[0]
    half = N_DEV // 2
    n = f // half
    tm = _pick(t, 512, 16)

    def epilogue(accs, in_refs, out_refs):
        da = accs[0]
        g = in_refs[2][0].astype(F32)
        u = in_refs[2][1].astype(F32)
        s = jax.nn.sigmoid(g)
        out_refs[0][0] = (da * u * (s * (1.0 + g * (1.0 - s)))).astype(BF16)
        out_refs[0][1] = (da * (g * s)).astype(BF16)

    gu_spec = pl.BlockSpec((2, tm, n), lambda j, i, k: (0, i, j))
    return _mm_call(
        name, (half, t // tm, 1),
        [(df, pl.BlockSpec((tm, d), lambda j, i, k: (i, 0))),
         (w_out2d, pl.BlockSpec((n, d), lambda j, i, k: (j, 0))),
         (gu, gu_spec)],
        [(0, 1, NT, 0)],
        [(jax.ShapeDtypeStruct((2, t, f), BF16), gu_spec)],
        [(tm, n)], epilogue)[0]


def _ffn_dh_bwd(dgu, w_in, name):
    _, t, f = dgu.shape
    d, n = w_in.shape[1], w_in.shape[2]
    half = N_DEV // 2
    tm = _pick(t, 512, 16)

    def epilogue(accs, in_refs, out_refs):
        out_refs[0][...] = accs[0]

    return _mm_call(
        name, (t // tm, 1, half),
        [(dgu, pl.BlockSpec((None, tm, n), lambda i, j, k: (0, i, k))),
         (dgu, pl.BlockSpec((None, tm, n), lambda i, j, k: (1, i, k))),
         (w_in, pl.BlockSpec((None, d, n), lambda i, j, k: (k, 0, 0))),
         (w_in, pl.BlockSpec((None, d, n), lambda i, j, k: (k + half, 0, 0)))],
        [(0, 2, NT, 0), (1, 3, NT, 0)],
        [(jax.ShapeDtypeStruct((t, d), F32), pl.BlockSpec((tm, d), lambda i, j, k: (i, 0)))],
        [(tm, d)], epilogue)[0]


def _ffn_dwin_bwd(h, dgu, name):
    t, d = h.shape
    f = dgu.shape[2]
    half = N_DEV // 2
    n = f // half
    tk = _pick(t, 1024, 16)

    def epilogue(accs, in_refs, out_refs):
        out_refs[0][...] = accs[0].astype(BF16)

    return _mm_call(
        name, (N_DEV, 1, t // tk),
        [(h, pl.BlockSpec((tk, d), lambda j, i, k: (k, 0))),
         (dgu, pl.BlockSpec((None, tk, n), lambda j, i, k: (j // half, k, j % half)))],
        [(0, 1, TN, 0)],
        [(jax.ShapeDtypeStruct((N_DEV, d, n), BF16), pl.BlockSpec((None, d, n), lambda j, i, k: (j, 0, 0)))],
        [(d, n)], epilogue)[0]


def _make_ffn_block(tag):
    @jax.custom_vjp
    def ffn_block(x, ng, sh, sc, gate, w_in, w_out):
        return fwd(x, ng, sh, sc, gate, w_in, w_out)[0]

    def fwd(x, ng, sh, sc, gate, w_in, w_out):
        f = w_out.shape[0] * w_out.shape[1]
        w_out2d = w_out.reshape(f, w_out.shape[2])
        h = _norm_mod_fwd(x, ng, sc, sh, tag + "_norm")
        a, gu = _ffn_in_fwd(h, w_in, tag + "_in")
        xo, f1 = _res_mm_fwd(a, w_out2d, x, gate, 0.5, tag + "_out")
        return xo, (x, ng, sh, sc, gate, w_in, w_out, h, a, gu, f1)

    def bwd(res, dxo):
        x, ng, sh, sc, gate, w_in, w_out, h, a, gu, f1 = res
        f = w_out.shape[0] * w_out.shape[1]
        w_out2d = w_out.reshape(f, w_out.shape[2])
        df, dgate = _gate_bwd(dxo, f1, gate, 0.5, tag + "_dgate")
        dgu = _ffn_da_bwd(df, w_out2d, gu, tag + "_da")
        dw_out = _matmul(a, df, "tn", BF16, tag + "_dwout", tm=_pick(f, 1408, 16), tn=2048, tk=1024)
        dh = _ffn_dh_bwd(dgu, w_in, tag + "_dh")
        dw_in = _ffn_dwin_bwd(h, dgu, tag + "_dwin")
        dx, dng, dsc, dsh = _norm_mod_bwd(x, ng, sc, sh, dh, dxo, tag + "_dnorm")
        return dx, dng, dsh, dsc, dgate, dw_in, dw_out.reshape(w_out.shape)

    ffn_block.defvjp(fwd, bwd)
    return ffn_block


def _make_norm_proj(tag):
    @jax.custom_vjp
    def norm_proj(x, ng, sh, sc, w):
        return fwd(x, ng, sh, sc, w)[0]

    def fwd(x, ng, sh, sc, w):
        h = _norm_mod_fwd(x, ng, sc, sh, tag + "_norm")
        p = _matmul(h, w, "nn", F32, tag + "_mm", tm=1024, tn=1024, tk=w.shape[0])
        return p, (x, ng, sh, sc, w, h)

    def bwd(res, dp):
        x, ng, sh, sc, w, h = res
        dh = _matmul(dp, w, "nt", F32, tag + "_dh", tm=512, tn=w.shape[0], tk=2048)
        dw = _matmul(h, dp, "tn", BF16, tag + "_dw", tm=w.shape[0], tn=1024, tk=1024)
        dx, dng, dsc, dsh = _norm_mod_bwd(x, ng, sc, sh, dh, None, tag + "_dnorm")
        return dx, dng, dsh, dsc, dw

    norm_proj.defvjp(fwd, bwd)
    return norm_proj


def _make_split(tag, widths, total):
    offs = [sum(widths[:i]) for i in range(len(widths))]

    def concat_call(pieces):
        t = pieces[0].shape[0]
        tm = _pick(t, 256, 16)

        def body(*refs):
            o_ref = refs[-1]
            for ref, off, wd in zip(refs[:-1], offs, widths):
                o_ref[:, off:off + wd] = ref[...]
            end = offs[-1] + widths[-1]
            if end < total:
                o_ref[:, end:] = jnp.zeros((tm, total - end), F32)

        return pl.pallas_call(
            body, name=tag + "_concat", grid=(t // tm,),
            in_specs=[_row_spec(tm, wd) for wd in widths], out_specs=_row_spec(tm, total),
            out_shape=jax.ShapeDtypeStruct((t, total), F32), compiler_params=_params(),
        )(*pieces)

    @jax.custom_vjp
    def split(p):
        return tuple(p[:, off:off + wd] for off, wd in zip(offs, widths))

    def fwd(p):
        return split(p), None

    def bwd(_, cts):
        return (concat_call(list(cts)),)

    split.defvjp(fwd, bwd)
    return split


def _make_res_proj(tag):
    @jax.custom_vjp
    def res_proj(a, w, x, gate):
        return fwd(a, w, x, gate)[0]

    def fwd(a, w, x, gate):
        xo, f = _res_mm_fwd(a, w, x, gate, 1.0, tag + "_mm")
        return xo, (a, w, gate, f)

    def bwd(res, dxo):
        a, w, gate, f = res
        df, dgate = _gate_bwd(dxo, f, gate, 1.0, tag + "_dgate")
        da = _matmul(df, w, "nt", BF16, tag + "_da", tm=1024, tn=1024, tk=2048)
        dw = _matmul(a, df, "tn", BF16, tag + "_dw", tm=1024, tn=2048, tk=1024)
        return da, dw, dxo, dgate

    res_proj.defvjp(fwd, bwd)
    return res_proj


def _make_small_mm(tag):
    @jax.custom_vjp
    def small_mm(a, w):
        return _matmul(a, w, "nn", F32, tag + "_mm", tm=a.shape[0], tn=768, tk=w.shape[0])

    def fwd(a, w):
        return small_mm(a, w), (a, w)

    def bwd(res, dr):
        a, w = res
        da = _matmul(dr, w, "nt", F32, tag + "_da", tm=a.shape[0], tn=w.shape[0], tk=768)
        dw = _matmul(a, dr, "tn", F32, tag + "_dw", tm=1024, tn=768, tk=a.shape[0])
        return da, dw

    small_mm.defvjp(fwd, bwd)
    return small_mm


def _ret_chunk_terms(lg, c, reverse):
    row = lax.broadcasted_iota(jnp.int32, (c, c), 0).astype(F32)
    col = lax.broadcasted_iota(jnp.int32, (c, c), 1).astype(F32)
    pos = lax.broadcasted_iota(jnp.int32, (c, 1), 0).astype(F32)
    if reverse:
        diff = col - row
        mask = diff > 0.0
        e_exp = float(c) - pos
        f_exp = pos
    else:
        diff = row - col
        mask = diff >= 0.0
        e_exp = pos + 1.0
        f_exp = float(c - 1) - pos
    diffm = jnp.where(mask, diff, 0.0)
    dm = jnp.where(mask, jnp.exp(lg * diffm), 0.0)
    return diffm, dm, e_exp, jnp.exp(lg * e_exp), f_exp, jnp.exp(lg * f_exp)


def _lane0(val):
    lane = lax.broadcasted_iota(jnp.int32, (1, LANE), 1)
    return jnp.where(lane == 0, val, 0.0)


RET_HEAD_BLOCK = 4


def _make_ret_dir(tag, reverse):
    hb = RET_HEAD_BLOCK

    def heads_spec(nc, width, flip):
        if flip:
            return pl.BlockSpec((hb, RET_CHUNK, width), lambda h, t: (h, nc - 1 - t, 0))
        return pl.BlockSpec((hb, RET_CHUNK, width), lambda h, t: (h, t, 0))

    def state_spec(nc, flip):
        if flip:
            return pl.BlockSpec((hb, None, RET_DK, RET_DV), lambda h, t: (h, nc - 1 - t, 0, 0))
        return pl.BlockSpec((hb, None, RET_DK, RET_DV), lambda h, t: (h, t, 0, 0))

    lg_spec = pl.BlockSpec((hb, 1, LANE), lambda h, t: (h, 0, 0))
    s0_spec = pl.BlockSpec((hb, RET_DK, RET_DV), lambda h, t: (h, 0, 0))

    def fwd_call(q, k, v, lgb, s0):
        hh, ll, _ = q.shape
        c = RET_CHUNK
        nc = ll // c

        def body(q_ref, k_ref, v_ref, lg_ref, s0_ref, y_ref, sall_ref, s_scr):
            @pl.when(pl.program_id(1) == 0)
            def _():
                s_scr[...] = s0_ref[...]

            for b in range(hb):
                lg = lg_ref[b][:, :1]
                _, dm, _, xi, _, zeta = _ret_chunk_terms(lg, c, reverse)
                q_t, k_t, v_t = q_ref[b], k_ref[b], v_ref[b]
                s = s_scr[b]
                p = _dot(q_t, k_t, NT) * dm
                y_ref[b] = _dot(p, v_t, NN) + _dot(q_t * xi, s, NN)
                sall_ref[b] = s
                s_scr[b] = jnp.exp(lg * float(c)) * s + _dot(k_t * zeta, v_t, TN)

        return pl.pallas_call(
            body, name=tag + "_fwd", grid=(hh // hb, nc),
            in_specs=[heads_spec(nc, RET_DK, reverse), heads_spec(nc, RET_DK, reverse),
                      heads_spec(nc, RET_DV, reverse), lg_spec, s0_spec],
            out_specs=[heads_spec(nc, RET_DV, reverse), state_spec(nc, reverse)],
            out_shape=[jax.ShapeDtypeStruct((hh, ll, RET_DV), F32),
                       jax.ShapeDtypeStruct((hh, nc, RET_DK, RET_DV), F32)],
            scratch_shapes=[pltpu.VMEM((hb, RET_DK, RET_DV), F32)],
            compiler_params=_params(),
        )(q, k, v, lgb, s0)

    def bwd_call(q, k, v, lgb, sall, dy):
        hh, ll, _ = q.shape
        c = RET_CHUNK
        nc = ll // c
        flip = not reverse

        def body(q_ref, k_ref, v_ref, lg_ref, sall_ref, dy_ref, dq_ref, dk_ref, dv_ref, dlg_ref, ds0_ref, ds_scr):
            @pl.when(pl.program_id(1) == 0)
            def _():
                ds_scr[...] = jnp.zeros_like(ds_scr)
                dlg_ref[...] = jnp.zeros_like(dlg_ref)

            def total(m):
                return jnp.sum(jnp.sum(m, axis=1, keepdims=True), axis=0, keepdims=True)

            for b in range(hb):
                lg = lg_ref[b][:, :1]
                diffm, dm, e_exp, xi, f_exp, zeta = _ret_chunk_terms(lg, c, reverse)
                q_t, k_t, v_t, dy_t = q_ref[b], k_ref[b], v_ref[b], dy_ref[b]
                s = sall_ref[b]
                dsn = ds_scr[b]
                a = _dot(q_t, k_t, NT)
                da = _dot(dy_t, v_t, NT) * dm
                g = _dot(dy_t, s, NT)
                hm = _dot(v_t, dsn, NT)
                dq_ref[b] = _dot(da, k_t, NN) + xi * g
                dk_ref[b] = _dot(da, q_t, TN) + zeta * hm
                dv_ref[b] = _dot(a * dm, dy_t, TN) + _dot(k_t * zeta, dsn, NN)
                gc = jnp.exp(lg * float(c))
                ds_scr[b] = gc * dsn + _dot(q_t * xi, dy_t, TN)
                dl = (total(da * a * diffm) + total(e_exp * xi * q_t * g)
                      + float(c) * gc * total(s * dsn) + total(f_exp * zeta * k_t * hm))
                dlg_ref[b] += _lane0(dl)

            @pl.when(pl.program_id(1) == nc - 1)
            def _():
                ds0_ref[...] = ds_scr[...]

        return pl.pallas_call(
            body, name=tag + "_bwd", grid=(hh // hb, nc),
            in_specs=[heads_spec(nc, RET_DK, flip), heads_spec(nc, RET_DK, flip), heads_spec(nc, RET_DV, flip),
                      lg_spec, state_spec(nc, flip), heads_spec(nc, RET_DV, flip)],
            out_specs=[heads_spec(nc, RET_DK, flip), heads_spec(nc, RET_DK, flip), heads_spec(nc, RET_DV, flip),
                       lg_spec, s0_spec],
            out_shape=[jax.ShapeDtypeStruct((hh, ll, RET_DK), F32), jax.ShapeDtypeStruct((hh, ll, RET_DK), F32),
                       jax.ShapeDtypeStruct((hh, ll, RET_DV), F32), jax.ShapeDtypeStruct((hh, 1, LANE), F32),
                       jax.ShapeDtypeStruct((hh, RET_DK, RET_DV), F32)],
            scratch_shapes=[pltpu.VMEM((hb, RET_DK, RET_DV), F32)],
            compiler_params=_params(),
        )(q, k, v, lgb, sall, dy)

    @jax.custom_vjp
    def ret_dir(q, k, v, lgb, s0):
        return fwd_call(q, k, v, lgb, s0)[0]

    def fwd(q, k, v, lgb, s0):
        y, sall = fwd_call(q, k, v, lgb, s0)
        return y, (q, k, v, lgb, sall)

    def bwd(res, dy):
        q, k, v, lgb, sall = res
        return tuple(bwd_call(q, k, v, lgb, sall, dy))

    ret_dir.defvjp(fwd, bwd)
    return ret_dir


def _make_ctx_state(tag, reverse):
    hb = RET_HEAD_BLOCK
    c = RET_CHUNK
    k_spec = pl.BlockSpec((hb, c, RET_DK), lambda h: (h, 0, 0))
    v_spec = pl.BlockSpec((hb, c, RET_DV), lambda h: (h, 0, 0))
    lg_spec = pl.BlockSpec((hb, 1, LANE), lambda h: (h, 0, 0))
    s_spec = pl.BlockSpec((hb, RET_DK, RET_DV), lambda h: (h, 0, 0))

    def fwd_call(k, v, lgb):
        hh = k.shape[0]

        def body(k_ref, v_ref, lg_ref, s_ref):
            for b in range(hb):
                _, _, _, _, _, zeta = _ret_chunk_terms(lg_ref[b][:, :1], c, reverse)
                s_ref[b] = _dot(k_ref[b] * zeta, v_ref[b], TN)

        return pl.pallas_call(
            body, name=tag + "_fwd", grid=(hh // hb,), in_specs=[k_spec, v_spec, lg_spec], out_specs=s_spec,
            out_shape=jax.ShapeDtypeStruct((hh, RET_DK, RET_DV), F32), compiler_params=_params(),
        )(k, v, lgb)

    def bwd_call(k, v, lgb, ds):
        hh = k.shape[0]

        def body(k_ref, v_ref, lg_ref, ds_ref, dk_ref, dv_ref, dlg_ref):
            for b in range(hb):
                _, _, _, _, f_exp, zeta = _ret_chunk_terms(lg_ref[b][:, :1], c, reverse)
                k_t, v_t, ds = k_ref[b], v_ref[b], ds_ref[b]
                hm = _dot(v_t, ds, NT)
                dk_ref[b] = zeta * hm
                dv_ref[b] = _dot(k_t * zeta, ds, NN)
                tot = jnp.sum(jnp.sum(f_exp * zeta * k_t * hm, axis=1, keepdims=True), axis=0, keepdims=True)
                dlg_ref[b] = _lane0(tot)

        return pl.pallas_call(
            body, name=tag + "_bwd", grid=(hh // hb,), in_specs=[k_spec, v_spec, lg_spec, s_spec],
            out_specs=[k_spec, v_spec, lg_spec],
            out_shape=[jax.ShapeDtypeStruct(k.shape, F32), jax.ShapeDtypeStruct(v.shape, F32),
                       jax.ShapeDtypeStruct((hh, 1, LANE), F32)],
            compiler_params=_params(),
        )(k, v, lgb, ds)

    @jax.custom_vjp
    def ctx_state(k, v, lgb):
        return fwd_call(k, v, lgb)

    def fwd(k, v, lgb):
        return fwd_call(k, v, lgb), (k, v, lgb)

    def bwd(res, ds):
        return tuple(bwd_call(*res, ds))

    ctx_state.defvjp(fwd, bwd)
    return ctx_state


def _ret_out_tile(y, g):
    mu = jnp.mean(y, axis=-1, keepdims=True)
    var = jnp.mean(jnp.square(y - mu), axis=-1, keepdims=True)
    return (g * jax.nn.sigmoid(g)) * ((y - mu) * lax.rsqrt(var + GN_EPS))


def _make_ret_out(tag):
    def specs(tm):
        y_spec = pl.BlockSpec((None, tm, RET_DV), lambda h, i: (h, i, 0))
        g_spec = pl.BlockSpec((tm, RET_DV), lambda h, i: (i, h))
        return y_spec, g_spec

    def fwd_call(yf, yb, g):
        hh, n, _ = yf.shape
        tm = _pick(n, 1024, 16)
        y_spec, g_spec = specs(tm)

        def body(yf_ref, yb_ref, g_ref, o_ref):
            o_ref[...] = _ret_out_tile(yf_ref[...] + yb_ref[...], g_ref[...]).astype(BF16)

        return pl.pallas_call(
            body, name=tag + "_fwd", grid=(hh, n // tm), in_specs=[y_spec, y_spec, g_spec], out_specs=g_spec,
            out_shape=jax.ShapeDtypeStruct((n, hh * RET_DV), BF16), compiler_params=_params(),
        )(yf, yb, g)

    def bwd_call(yf, yb, g, do):
        hh, n, _ = yf.shape
        tm = _pick(n, 1024, 16)
        y_spec, g_spec = specs(tm)

        def body(yf_ref, yb_ref, g_ref, do_ref, dy_ref, dg_ref):
            _, vjp = jax.vjp(_ret_out_tile, yf_ref[...] + yb_ref[...], g_ref[...])
            dy, dg = vjp(do_ref[...].astype(F32))
            dy_ref[...] = dy
            dg_ref[...] = dg

        return pl.pallas_call(
            body, name=tag + "_bwd", grid=(hh, n // tm), in_specs=[y_spec, y_spec, g_spec, g_spec],
            out_specs=[y_spec, g_spec],
            out_shape=[jax.ShapeDtypeStruct(yf.shape, F32), jax.ShapeDtypeStruct(g.shape, F32)],
            compiler_params=_params(),
        )(yf, yb, g, do)

    @jax.custom_vjp
    def ret_out(yf, yb, g):
        return fwd_call(yf, yb, g)

    def fwd(yf, yb, g):
        return fwd_call(yf, yb, g), (yf, yb, g)

    def bwd(res, do):
        dy, dg = bwd_call(*res, do)
        return dy, dy, dg

    ret_out.defvjp(fwd, bwd)
    return ret_out


MLA_DQ_PAD = 2 * LANE
MLA_PACK_ROWS = 256


def _rope128(x, cos, s1, s2):
    return x * cos + pltpu.roll(x, LANE - 16, 1) * s1 + pltpu.roll(x, 16, 1) * s2


def _rope128_t(g, cos, s1, s2):
    return g * cos + pltpu.roll(g * s1, 16, 1) + pltpu.roll(g * s2, LANE - 16, 1)


def _axial_tables(n_lat):
    n_rows = n_lat // GRID_W
    half = MLA_ROPE // 2
    inv = AXIAL_BASE ** (-jnp.arange(0, half, 2, dtype=F32) / half)
    ang_r = jnp.repeat(jnp.arange(n_rows), GRID_W).astype(F32)[:, None] * inv[None, :]
    ang_c = jnp.tile(jnp.arange(GRID_W), n_rows).astype(F32)[:, None] * inv[None, :]
    zero = jnp.zeros_like(ang_r)
    cos = jnp.concatenate([jnp.cos(ang_r), jnp.cos(ang_r), jnp.cos(ang_c), jnp.cos(ang_c)], axis=1)
    s1 = jnp.concatenate([-jnp.sin(ang_r), zero, -jnp.sin(ang_c), zero], axis=1)
    s2 = jnp.concatenate([zero, jnp.sin(ang_r), zero, jnp.sin(ang_c)], axis=1)
    return tuple(jnp.concatenate([t, t], axis=1) for t in (cos, s1, s2))


def _make_mla_pack(tag, n_lat, n_ctx, scale):
    hh = MLA_HEADS
    tm = MLA_PACK_ROWS
    ll = n_lat + n_ctx
    rope0 = hh * MLA_NOPE
    tabs = _axial_tables(n_lat)

    def rope_lanes():
        return lax.broadcasted_iota(jnp.int32, (1, LANE), 1) < MLA_ROPE

    def rows(width):
        return pl.BlockSpec((tm, width), lambda i: (i, 0))

    def heads(width, off):
        return pl.BlockSpec((hh, tm, width), lambda i: (0, i + off, 0))

    def put_kv(kv_ref, kr_rot, k_ref, v_ref):
        for h in range(hh):
            k_ref[h, :, :MLA_NOPE] = kv_ref[:, 2 * LANE * h:2 * LANE * h + MLA_NOPE].astype(BF16)
            k_ref[h, :, MLA_NOPE:] = kr_rot
            v_ref[h] = kv_ref[:, 2 * LANE * h + MLA_NOPE:2 * LANE * (h + 1)].astype(BF16)

    def fwd_lat(qp, kv, kr):
        def body(qp_ref, kv_ref, kr_ref, cos_ref, s1_ref, s2_ref, q_ref, k_ref, v_ref):
            cos, s1, s2 = cos_ref[...], s1_ref[...], s2_ref[...]
            keep = rope_lanes()
            for j in range(hh // 2):
                rot = _rope128(qp_ref[:, rope0 + LANE * j:rope0 + LANE * (j + 1)], cos, s1, s2)
                q_ref[2 * j, :, MLA_NOPE:] = jnp.where(keep, rot, 0.0).astype(BF16)
                q_ref[2 * j + 1, :, MLA_NOPE:] = jnp.where(keep, pltpu.roll(rot, MLA_ROPE, 1), 0.0).astype(BF16)
            for h in range(hh):
                q_ref[h, :, :MLA_NOPE] = qp_ref[:, MLA_NOPE * h:MLA_NOPE * (h + 1)].astype(BF16)
            kr_rot = jnp.where(keep, _rope128(kr_ref[...], cos, s1, s2), 0.0).astype(BF16)
            put_kv(kv_ref, kr_rot, k_ref, v_ref)

        return pl.pallas_call(
            body, name=tag + "_lat", grid=(n_lat // tm,),
            in_specs=[rows(qp.shape[1]), rows(kv.shape[1]), rows(LANE), rows(LANE), rows(LANE), rows(LANE)],
            out_specs=[heads(MLA_DQ_PAD, 0), heads(MLA_DQ_PAD, 0), heads(MLA_V, 0)],
            out_shape=[jax.ShapeDtypeStruct((hh, n_lat, MLA_DQ_PAD), BF16),
                       jax.ShapeDtypeStruct((hh, ll, MLA_DQ_PAD), BF16), jax.ShapeDtypeStruct((hh, ll, MLA_V), BF16)],
            compiler_params=_params(),
        )(qp, kv, kr, *tabs)

    def fwd_ctx(kv_c, kr_c, k_buf, v_buf):
        def body(kv_ref, kr_ref, k_in, v_in, k_ref, v_ref):
            kr_rot = jnp.where(rope_lanes(), kr_ref[...], 0.0).astype(BF16)
            put_kv(kv_ref, kr_rot, k_ref, v_ref)

        any_spec = pl.BlockSpec(memory_space=pl.ANY)
        off = n_lat // tm
        return pl.pallas_call(
            body, name=tag + "_ctx", grid=(n_ctx // tm,),
            in_specs=[rows(kv_c.shape[1]), rows(LANE), any_spec, any_spec],
            out_specs=[heads(MLA_DQ_PAD, off), heads(MLA_V, off)],
            out_shape=[jax.ShapeDtypeStruct(k_buf.shape, BF16), jax.ShapeDtypeStruct(v_buf.shape, BF16)],
            input_output_aliases={2: 0, 3: 1}, compiler_params=_params(),
        )(kv_c, kr_c, k_buf, v_buf)

    def take_kv(dk_ref, dv_ref, dkv_ref):
        dkr = jnp.zeros((tm, LANE), F32)
        for h in range(hh):
            dkv_ref[:, 2 * LANE * h:2 * LANE * h + MLA_NOPE] = dk_ref[h, :, :MLA_NOPE].astype(F32)
            dkv_ref[:, 2 * LANE * h + MLA_NOPE:2 * LANE * (h + 1)] = dv_ref[h].astype(F32)
            dkr = dkr + dk_ref[h, :, MLA_NOPE:].astype(F32)
        return jnp.where(rope_lanes(), dkr, 0.0)

    def bwd_lat(dqt, dk, dv, qp_width, kv_width):
        def body(dqt_ref, dk_ref, dv_ref, cos_ref, s1_ref, s2_ref, dqp_ref, dkv_ref, dkr_ref):
            cos, s1, s2 = cos_ref[...], s1_ref[...], s2_ref[...]
            keep = rope_lanes()
            for j in range(hh // 2):
                even = jnp.transpose(dqt_ref[2 * j]) * scale
                odd = jnp.transpose(dqt_ref[2 * j + 1]) * scale
                dqp_ref[:, MLA_NOPE * 2 * j:MLA_NOPE * (2 * j + 1)] = even[:, :MLA_NOPE]
                dqp_ref[:, MLA_NOPE * (2 * j + 1):MLA_NOPE * (2 * j + 2)] = odd[:, :MLA_NOPE]
                g = jnp.where(keep, even[:, MLA_NOPE:], pltpu.roll(odd[:, MLA_NOPE:], MLA_ROPE, 1))
                dqp_ref[:, rope0 + LANE * j:rope0 + LANE * (j + 1)] = _rope128_t(g, cos, s1, s2)
            dkr_ref[...] = jnp.where(keep, _rope128_t(take_kv(dk_ref, dv_ref, dkv_ref), cos, s1, s2), 0.0)

        return pl.pallas_call(
            body, name=tag + "_dlat", grid=(n_lat // tm,),
            in_specs=[pl.BlockSpec((hh, MLA_DQ_PAD, tm), lambda i: (0, 0, i)),
                      heads(MLA_DQ_PAD, 0), heads(MLA_V, 0), rows(LANE), rows(LANE), rows(LANE)],
            out_specs=[rows(qp_width), rows(kv_width), rows(LANE)],
            out_shape=[jax.ShapeDtypeStruct((n_lat, qp_width), F32), jax.ShapeDtypeStruct((n_lat, kv_width), F32),
                       jax.ShapeDtypeStruct((n_lat, LANE), F32)],
            compiler_params=_params(),
        )(dqt, dk, dv, *tabs)

    def bwd_ctx(dk, dv, kv_width):
        def body(dk_ref, dv_ref, dkv_ref, dkr_ref):
            dkr_ref[...] = take_kv(dk_ref, dv_ref, dkv_ref)

        off = n_lat // tm
        return pl.pallas_call(
            body, name=tag + "_dctx", grid=(n_ctx // tm,),
            in_specs=[heads(MLA_DQ_PAD, off), heads(MLA_V, off)],
            out_specs=[rows(kv_width), rows(LANE)],
            out_shape=[jax.ShapeDtypeStruct((n_ctx, kv_width), F32), jax.ShapeDtypeStruct((n_ctx, LANE), F32)],
            compiler_params=_params(),
        )(dk, dv)

    def pack(qp, kv, kr, kv_c, kr_c):
        q, k, v = fwd_lat(qp, kv, kr)
        k, v = fwd_ctx(kv_c, kr_c, k, v)
        return q, k, v

    def unpack(dqt, dk, dv):
        qp_width, kv_width = hh * (MLA_NOPE + MLA_ROPE), hh * (MLA_NOPE + MLA_V)
        dqp, dkv, dkr = bwd_lat(dqt, dk, dv, qp_width, kv_width)
        dkv_c, dkr_c = bwd_ctx(dk, dv, kv_width)
        return dqp, dkv, dkr, dkv_c, dkr_c

    return pack, unpack


def _make_mla(tag, n_lat, n_ctx):
    scale = (MLA_NOPE + MLA_ROPE) ** -0.5
    pack, unpack = _make_mla_pack(tag + "pack", n_lat, n_ctx, scale)
    attn_fwd, attn_delta, attn_bwd = _make_attention(tag, scale)

    @jax.custom_vjp
    def mla(qp, kv, kr, kv_c, kr_c):
        q, k, v = pack(qp, kv, kr, kv_c, kr_c)
        return attn_fwd(q, k, jnp.swapaxes(v, 1, 2))[0]

    def fwd(qp, kv, kr, kv_c, kr_c):
        q, k, v = pack(qp, kv, kr, kv_c, kr_c)
        o, lse = attn_fwd(q, k, jnp.swapaxes(v, 1, 2))
        return o, (q, k, v, o, lse)

    def bwd(res, do):
        q, k, v, o, lse = res
        delta = attn_delta(o, do, q.shape[0])
        dqt, dk, dv = attn_bwd(q, k, jnp.swapaxes(k, 1, 2), v, do, lse, delta)
        return unpack(dqt, dk, dv)

    mla.defvjp(fwd, bwd)
    return mla


def _make_attention(tag, scale):
    neg_big = -1e30
    log2e = 1.4426950408889634
    sub = 256

    def fwd_call(q, k, vt):
        hh, n, dq = q.shape
        dv, ll = vt.shape[1], vt.shape[2]
        tq, tk = _pick(n, 1024), _pick(ll, 1408)
        sb = sub if tk % sub == 0 else tk
        c2 = scale * log2e
        k_steps = ll // tk

        def body(q_ref, k_ref, vt_ref, o_ref, lse_ref, m_scr, l_scr, acc_scr, s_scr, p_scr):
            j = pl.program_id(2)

            @pl.when(j == 0)
            def _():
                m_scr[...] = jnp.full_like(m_scr, neg_big)
                l_scr[...] = jnp.zeros_like(l_scr)
                acc_scr[...] = jnp.zeros_like(acc_scr)

            q_t = q_ref[...]
            m_prev = m_scr[...]
            m_new = m_prev
            for kk in range(tk // sb):
                rows = slice(kk * sb, (kk + 1) * sb)
                s_t = _dot(k_ref[rows, :], q_t, NT)
                s_scr[rows, :] = s_t
                m_new = jnp.maximum(m_new, jnp.max(s_t, axis=0, keepdims=True))
            mc = m_new * c2
            l_part = jnp.zeros_like(m_new)
            for kk in range(tk // sb):
                rows = slice(kk * sb, (kk + 1) * sb)
                p_t = jnp.exp2(s_scr[rows, :] * c2 - mc)
                l_part = l_part + jnp.sum(p_t, axis=0, keepdims=True)
                p_scr[rows, :] = p_t.astype(BF16)
            alpha = jnp.exp2((m_prev - m_new) * c2)
            l_scr[...] = alpha * l_scr[...] + l_part
            acc_scr[...] = alpha * acc_scr[...] + _dot(vt_ref[...], p_scr[...], NN)
            m_scr[...] = m_new

            @pl.when(j == k_steps - 1)
            def _():
                o_ref[...] = jnp.transpose(acc_scr[...] / l_scr[...]).astype(BF16)
                lse_ref[...] = m_scr[...] * scale + jnp.log(l_scr[...])

        return pl.pallas_call(
            body, name=tag + "_fwd", grid=(hh, n // tq, k_steps),
            in_specs=[pl.BlockSpec((None, tq, dq), lambda h, i, j: (h, i, 0)),
                      pl.BlockSpec((None, tk, dq), lambda h, i, j: (h, j, 0)),
                      pl.BlockSpec((None, dv, tk), lambda h, i, j: (h, 0, j))],
            out_specs=[pl.BlockSpec((tq, dv), lambda h, i, j: (i, h)),
                       pl.BlockSpec((None, 1, tq), lambda h, i, j: (h, 0, i))],
            out_shape=[jax.ShapeDtypeStruct((n, hh * dv), BF16), jax.ShapeDtypeStruct((hh, 1, n), F32)],
            scratch_shapes=[pltpu.VMEM((1, tq), F32), pltpu.VMEM((1, tq), F32), pltpu.VMEM((dv, tq), F32),
                            pltpu.VMEM((tk, tq), F32), pltpu.VMEM((tk, tq), BF16)],
            compiler_params=_params(),
        )(q, k, vt)

    def delta_call(o, do, hh):
        n = o.shape[0]
        dv = o.shape[1] // hh
        tq = _pick(n, 1024)

        def body(o_ref, do_ref, d_ref):
            prod_t = jnp.transpose(o_ref[...].astype(F32) * do_ref[...].astype(F32))
            d_ref[...] = jnp.sum(prod_t, axis=0, keepdims=True)

        spec = pl.BlockSpec((tq, dv), lambda h, i: (i, h))
        return pl.pallas_call(
            body, name=tag + "_delta", grid=(hh, n // tq), in_specs=[spec, spec],
            out_specs=pl.BlockSpec((None, 1, tq), lambda h, i: (h, 0, i)),
            out_shape=jax.ShapeDtypeStruct((hh, 1, n), F32), compiler_params=_params(),
        )(o, do)

    def bwd_call(q, k, kt, v, do, lse, delta):
        hh, n, dq = q.shape
        ll, dv = k.shape[1], v.shape[2]
        tq, tk = _pick(n, 1024), _pick(ll, 1408)
        sb = tk
        c2 = scale * log2e
        q_steps = n // tq

        def body(q_ref, k_ref, kt_ref, v_ref, do_ref, lse_ref, d_ref, dqt_ref, dk_ref, dv_ref, dk_scr, dv_scr):
            j = pl.program_id(1)
            i = pl.program_id(2)

            @pl.when(i == 0)
            def _():
                dk_scr[...] = jnp.zeros_like(dk_scr)
                dv_scr[...] = jnp.zeros_like(dv_scr)

            q_t, do_t = q_ref[...], do_ref[...]
            lse2 = lse_ref[...] * log2e
            delta_t = d_ref[...]
            dq_part = None
            for kk in range(tk // sb):
                rows = slice(kk * sb, (kk + 1) * sb)
                s_t = _dot(k_ref[rows, :], q_t, NT)
                p_t = jnp.exp2(s_t * c2 - lse2)
                ds_t = p_t * (_dot(v_ref[rows, :], do_t, NT) - delta_t)
                dv_scr[rows, :] += _dot(p_t, do_t, NN)
                dk_scr[rows, :] += _dot(ds_t, q_t, NN)
                part = _dot(kt_ref[:, rows], ds_t, NN)
                dq_part = part if dq_part is None else dq_part + part
            cols = pl.ds(pl.multiple_of(i * tq, tq), tq)

            @pl.when(j == 0)
            def _():
                dqt_ref[:, cols] = dq_part

            @pl.when(j > 0)
            def _():
                dqt_ref[:, cols] += dq_part

            @pl.when(i == q_steps - 1)
            def _():
                dk_ref[...] = (dk_scr[...] * scale).astype(BF16)
                dv_ref[...] = dv_scr[...].astype(BF16)

        return pl.pallas_call(
            body, name=tag + "_bwd", grid=(hh, ll // tk, q_steps),
            in_specs=[pl.BlockSpec((None, tq, dq), lambda h, j, i: (h, i, 0)),
                      pl.BlockSpec((None, tk, dq), lambda h, j, i: (h, j, 0)),
                      pl.BlockSpec((None, dq, tk), lambda h, j, i: (h, 0, j)),
                      pl.BlockSpec((None, tk, dv), lambda h, j, i: (h, j, 0)),
                      pl.BlockSpec((tq, dv), lambda h, j, i: (i, h)),
                      pl.BlockSpec((None, 1, tq), lambda h, j, i: (h, 0, i)),
                      pl.BlockSpec((None, 1, tq), lambda h, j, i: (h, 0, i))],
            out_specs=[pl.BlockSpec((None, dq, n), lambda h, j, i: (h, 0, 0)),
                       pl.BlockSpec((None, tk, dq), lambda h, j, i: (h, j, 0)),
                       pl.BlockSpec((None, tk, dv), lambda h, j, i: (h, j, 0))],
            out_shape=[jax.ShapeDtypeStruct((hh, dq, n), F32), jax.ShapeDtypeStruct((hh, ll, dq), BF16),
                       jax.ShapeDtypeStruct((hh, ll, dv), BF16)],
            scratch_shapes=[pltpu.VMEM((tk, dq), F32), pltpu.VMEM((tk, dv), F32)],
            compiler_params=_params(),
        )(q, k, kt, v, do, lse, delta)

    return fwd_call, delta_call, bwd_call


def _loss_tile(x, g, tgt):
    r = lax.rsqrt(jnp.mean(x * x, axis=-1, keepdims=True) + RMS_EPS)
    err = x * r * g - tgt
    per_tok = jnp.mean(err * err, axis=-1, keepdims=True)
    return 0.5 * jnp.sum(per_tok, axis=0, keepdims=True)


def _make_final_loss(tag):
    def fwd_call(x, g, tgt):
        t, d = x.shape
        tm = _pick(t, 512, 16)

        def body(x_ref, g_ref, t_ref, l_ref):
            l_ref[...] = jnp.broadcast_to(_loss_tile(x_ref[...], g_ref[...], t_ref[...]), (1, LANE))

        parts = pl.pallas_call(
            body, name=tag + "_fwd", grid=(t // tm,),
            in_specs=[_row_spec(tm, d), _vec_spec(d), _row_spec(tm, d)],
            out_specs=pl.BlockSpec((None, 1, LANE), lambda i: (i, 0, 0)),
            out_shape=jax.ShapeDtypeStruct((t // tm, 1, LANE), F32), compiler_params=_params(),
        )(x, g, tgt)
        return jnp.sum(parts[:, 0, 0])

    def bwd_call(x, g, tgt, dl):
        t, d = x.shape
        tm = _pick(t, 256, 16)

        def body(x_ref, g_ref, t_ref, dl_ref, dx_ref, dg_ref):
            _, vjp = jax.vjp(_loss_tile, x_ref[...], g_ref[...], t_ref[...])
            dx, dg, _ = vjp(dl_ref[...])
            dx_ref[...] = dx

            @pl.when(pl.program_id(0) == 0)
            def _():
                dg_ref[...] = jnp.zeros_like(dg_ref)

            dg_ref[...] += dg

        return pl.pallas_call(
            body, name=tag + "_bwd", grid=(t // tm,),
            in_specs=[_row_spec(tm, d), _vec_spec(d), _row_spec(tm, d), pl.BlockSpec((1, 1), lambda i: (0, 0))],
            out_specs=[_row_spec(tm, d), _vec_spec(d)],
            out_shape=[jax.ShapeDtypeStruct((t, d), F32), jax.ShapeDtypeStruct((1, d), F32)],
            compiler_params=_params(),
        )(x, g, tgt, dl)

    @jax.custom_vjp
    def final_loss(x, g, tgt):
        return fwd_call(x, g, tgt)

    def fwd(x, g, tgt):
        return fwd_call(x, g, tgt), (x, g, tgt)

    def bwd(res, dl):
        x, g, tgt = res
        dx, dg = bwd_call(x, g, tgt, dl.reshape(1, 1).astype(F32))
        return dx, dg, jnp.zeros_like(tgt)

    final_loss.defvjp(fwd, bwd)
    return final_loss


def _exchange(arrays, gather, name):
    n = len(arrays)

    def body(*refs):
        ins, outs = refs[:n], refs[n:2 * n]
        send_sems, recv_sems, local_sems = refs[2 * n:]
        me = 4 * lax.axis_index("x") + 2 * lax.axis_index("y") + lax.axis_index("c")

        def remote(a, d, wait_side=False):
            peer = (me + d) % N_DEV
            origin = (me + N_DEV - d) % N_DEV
            src = ins[a] if gather else ins[a].at[peer]
            dst = outs[a].at[origin if wait_side else me]
            return pltpu.make_async_remote_copy(
                src_ref=src, dst_ref=dst, send_sem=send_sems.at[a, d - 1], recv_sem=recv_sems.at[a, d - 1],
                device_id=(peer // 4, (peer // 2) % 2, peer % 2), device_id_type=pl.DeviceIdType.MESH)

        def local(a):
            src = ins[a] if gather else ins[a].at[me]
            return pltpu.make_async_copy(src, outs[a].at[me], local_sems.at[a])

        for a in range(n):
            for d in range(1, N_DEV):
                remote(a, d).start()
            local(a).start()
        for a in range(n):
            local(a).wait()
            for d in range(1, N_DEV):
                remote(a, d, wait_side=True).wait_recv()
                remote(a, d).wait_send()

    out_shape = []
    for arr in arrays:
        shape = (N_DEV,) + arr.shape if gather else arr.shape
        out_shape.append(jax.ShapeDtypeStruct(shape, arr.dtype))
    any_spec = pl.BlockSpec(memory_space=pl.ANY)
    return pl.pallas_call(
        body, name=name, in_specs=[any_spec] * n, out_specs=[any_spec] * n, out_shape=out_shape,
        scratch_shapes=[pltpu.SemaphoreType.DMA((n, N_DEV - 1)), pltpu.SemaphoreType.DMA((n, N_DEV - 1)),
                        pltpu.SemaphoreType.DMA((n,))],
        compiler_params=pltpu.CompilerParams(has_side_effects=True),
    )(*arrays)


def _split_copy(ins, lands, send_sems, recv_sems, a, d, gather, wait_side):
    me = 4 * lax.axis_index("x") + 2 * lax.axis_index("y") + lax.axis_index("c")
    peer = (me + d) % N_DEV
    origin = (me + N_DEV - d) % N_DEV
    return pltpu.make_async_remote_copy(
        src_ref=ins[a] if gather else ins[a].at[peer], dst_ref=lands[a].at[origin if wait_side else me],
        send_sem=send_sems.at[a * (N_DEV - 1) + d - 1], recv_sem=recv_sems.at[a * (N_DEV - 1) + d - 1],
        device_id=(peer // 4, (peer // 2) % 2, peer % 2), device_id_type=pl.DeviceIdType.MESH)


def _exchange_start(srcs, lands, after, gather, name):
    n = len(srcs)

    def body(*refs):
        ins, lnd = refs[:n], refs[n:2 * n]
        send_sems, recv_sems = refs[2 * n + 1], refs[2 * n + 2]
        for a in range(n):
            for d in range(1, N_DEV):
                _split_copy(ins, lnd, send_sems, recv_sems, a, d, gather, False).start()

    hbm = pl.BlockSpec(memory_space=pltpu.HBM)
    sem = pl.BlockSpec(memory_space=pltpu.SEMAPHORE)
    bufs = [pltpu.with_memory_space_constraint(t, pltpu.HBM) for t in list(srcs) + list(lands) + [after]]
    res = pl.pallas_call(
        body, name=name,
        in_specs=[hbm] * (2 * n + 1), out_specs=[sem, sem] + [hbm] * (2 * n + 1),
        out_shape=[pltpu.SemaphoreType.DMA((n * (N_DEV - 1),)), pltpu.SemaphoreType.DMA((n * (N_DEV - 1),))]
        + [pltpu.HBM(t.shape, t.dtype) for t in bufs],
        input_output_aliases={i: 2 + i for i in range(2 * n + 1)},
        compiler_params=pltpu.CompilerParams(has_side_effects=pltpu.SideEffectType.DATAFLOW_SIDE_EFFECTING),
    )(*bufs)
    return res[0], res[1], res[2:2 + n], res[2 + n:2 + 2 * n], res[-1]


def _exchange_wait(send_sems, recv_sems, srcs, lands, after, gather, name):
    n = len(srcs)

    def body(*refs):
        ins, lnd = refs[:n], refs[n:2 * n]
        send_sems_ref, recv_sems_ref = refs[2 * n], refs[2 * n + 1]
        for a in range(n):
            for d in range(1, N_DEV):
                _split_copy(ins, lnd, send_sems_ref, recv_sems_ref, a, d, gather, False).wait_send()
                _split_copy(ins, lnd, send_sems_ref, recv_sems_ref, a, d, gather, True).wait_recv()

    hbm = pl.BlockSpec(memory_space=pltpu.HBM)
    sem = pl.BlockSpec(memory_space=pltpu.SEMAPHORE)
    bufs = list(srcs) + list(lands)
    res = pl.pallas_call(
        body, name=name,
        in_specs=[hbm] * (2 * n) + [sem, sem, pl.BlockSpec(memory_space=pl.ANY)],
        out_specs=[hbm] * (2 * n),
        out_shape=[pltpu.HBM(t.shape, t.dtype) for t in bufs],
        input_output_aliases={i: i for i in range(2 * n)},
        compiler_params=pltpu.CompilerParams(has_side_effects=pltpu.SideEffectType.DATAFLOW_SIDE_EFFECTING),
    )(*bufs, send_sems, recv_sems, after)
    return res[n:]


def _own_slot(block, me):
    empty = lax.empty((N_DEV,) + block.shape, block.dtype)
    return lax.dynamic_update_slice(empty, block[None], (me,) + (0,) * block.ndim)


def _coords():
    return lax.axis_index("x"), lax.axis_index("y"), lax.axis_index("c")


def _other_chips(x, y):
    return [(1 - x, y), (x, 1 - y), (1 - x, 1 - y)]


def _gather_two_level(arrays, name):
    n = len(arrays)

    def body(*refs):
        ins, outs = refs[:n], refs[n:2 * n]
        send_sems, recv_sems, local_sems = refs[2 * n:]
        x, y, c = _coords()
        me, sib = (x, y, c), (x, y, 1 - c)
        chips = _other_chips(x, y)

        def copy(a, k, block, to, from_input=False):
            slot = 4 * block[0] + 2 * block[1] + block[2]
            return pltpu.make_async_remote_copy(
                src_ref=ins[a] if from_input else outs[a].at[slot], dst_ref=outs[a].at[slot],
                send_sem=send_sems.at[a, k], recv_sem=recv_sems.at[a, k],
                device_id=to, device_id_type=pl.DeviceIdType.MESH)

        def local(a):
            return pltpu.make_async_copy(ins[a], outs[a].at[4 * x + 2 * y + c], local_sems.at[a])

        for a in range(n):
            for j, chip in enumerate(chips):
                copy(a, 1 + j, me, (*chip, c), True).start()
            copy(a, 0, me, sib, True).start()
            local(a).start()
        for a in range(n):
            for j, chip in enumerate(chips):
                copy(a, 1 + j, (*chip, c), me).wait_recv()
                copy(a, 4 + j, (*chip, c), sib).start()
        for a in range(n):
            copy(a, 0, sib, me).wait_recv()
            for j, chip in enumerate(chips):
                copy(a, 4 + j, (*chip, 1 - c), me).wait_recv()
            for k in range(N_DEV - 1):
                copy(a, k, me, sib, True).wait_send()
            local(a).wait()

    any_spec = pl.BlockSpec(memory_space=pl.ANY)
    return pl.pallas_call(
        body, name=name, in_specs=[any_spec] * n, out_specs=[any_spec] * n,
        out_shape=[jax.ShapeDtypeStruct((N_DEV,) + arr.shape, arr.dtype) for arr in arrays],
        scratch_shapes=[pltpu.SemaphoreType.DMA((n, N_DEV - 1)), pltpu.SemaphoreType.DMA((n, N_DEV - 1)),
                        pltpu.SemaphoreType.DMA((n,))],
        compiler_params=pltpu.CompilerParams(has_side_effects=True),
    )(*arrays)


def _swap_sibling(arrays, name):
    n = len(arrays)
    n_chip = N_DEV // 2

    def body(*refs):
        ins, outs = refs[:n], refs[n:2 * n]
        send_sems, recv_sems = refs[2 * n:]
        x, y, c = _coords()

        def copy(a, q):
            return pltpu.make_async_remote_copy(
                src_ref=ins[a].at[2 * q + (1 - c)], dst_ref=outs[a].at[q],
                send_sem=send_sems.at[a, q], recv_sem=recv_sems.at[a, q],
                device_id=(x, y, 1 - c), device_id_type=pl.DeviceIdType.MESH)

        for a in range(n):
            for q in range(n_chip):
                copy(a, q).start()
        for a in range(n):
            for q in range(n_chip):
                copy(a, q).wait_recv()
                copy(a, q).wait_send()

    any_spec = pl.BlockSpec(memory_space=pl.ANY)
    return pl.pallas_call(
        body, name=name, in_specs=[any_spec] * n, out_specs=[any_spec] * n,
        out_shape=[jax.ShapeDtypeStruct((n_chip,) + arr.shape[1:], arr.dtype) for arr in arrays],
        scratch_shapes=[pltpu.SemaphoreType.DMA((n, n_chip)), pltpu.SemaphoreType.DMA((n, n_chip))],
        compiler_params=pltpu.CompilerParams(has_side_effects=True),
    )(*arrays)


def _scatter_chips(arrays, name):
    n = len(arrays)
    n_chip = N_DEV // 2

    def body(*refs):
        ins, outs = refs[:n], refs[n:2 * n]
        send_sems, recv_sems, local_sems = refs[2 * n:]
        x, y, c = _coords()
        q_me = 2 * x + y
        chips = _other_chips(x, y)

        def copy(a, j, wait_side=False):
            q_peer = 2 * chips[j][0] + chips[j][1]
            return pltpu.make_async_remote_copy(
                src_ref=ins[a].at[q_peer], dst_ref=outs[a].at[q_peer if wait_side else q_me],
                send_sem=send_sems.at[a, j], recv_sem=recv_sems.at[a, j],
                device_id=(*chips[j], c), device_id_type=pl.DeviceIdType.MESH)

        def local(a):
            return pltpu.make_async_copy(ins[a].at[q_me], outs[a].at[q_me], local_sems.at[a])

        for a in range(n):
            for j in range(n_chip - 1):
                copy(a, j).start()
            local(a).start()
        for a in range(n):
            local(a).wait()
            for j in range(n_chip - 1):
                copy(a, j, wait_side=True).wait_recv()
                copy(a, j).wait_send()

    any_spec = pl.BlockSpec(memory_space=pl.ANY)
    return pl.pallas_call(
        body, name=name, in_specs=[any_spec] * n, out_specs=[any_spec] * n,
        out_shape=[jax.ShapeDtypeStruct(arr.shape, arr.dtype) for arr in arrays],
        scratch_shapes=[pltpu.SemaphoreType.DMA((n, n_chip - 1)), pltpu.SemaphoreType.DMA((n, n_chip - 1)),
                        pltpu.SemaphoreType.DMA((n,))],
        compiler_params=pltpu.CompilerParams(has_side_effects=True),
    )(*arrays)


def _pair_add(full, theirs, core, name):
    n_chip, r, cn = theirs.shape
    tr = _pick(r, max(16, (2 * 1024 * 1024) // (4 * cn) // 16 * 16), 16)

    def body(core_ref, mine_ref, theirs_ref, o_ref):
        o_ref[...] = (mine_ref[...].astype(F32) + theirs_ref[...].astype(F32)).astype(BF16)

    tile = pl.BlockSpec((None, tr, cn), lambda q, i, core_ref: (q, i, 0))
    return pl.pallas_call(
        body, name=name,
        grid_spec=pltpu.PrefetchScalarGridSpec(
            num_scalar_prefetch=1, grid=(n_chip, r // tr),
            in_specs=[pl.BlockSpec((None, tr, cn), lambda q, i, core_ref: (2 * q + core_ref[0], i, 0)), tile],
            out_specs=tile),
        out_shape=jax.ShapeDtypeStruct(theirs.shape, BF16), compiler_params=_params(),
    )(core, full, theirs)


def _make_gather_op(tag):
    @jax.custom_vjp
    def gather_op(xl):
        return _exchange([xl], True, tag + "_gather")[0]

    def fwd(xl):
        return gather_op(xl), None

    def bwd(_, g):
        return (jnp.sum(_exchange([g], False, tag + "_scatter")[0], axis=0),)

    gather_op.defvjp(fwd, bwd)
    return gather_op


def _adamw(gstack, w, m, v, name):
    s, r, cn = gstack.shape
    tr = _pick(r, max(8, (2 * 1024 * 1024) // (4 * cn) // 8 * 8), 8)
    c1 = 1.0 - ADAM_B1 ** ADAM_STEP
    c2 = 1.0 - ADAM_B2 ** ADAM_STEP

    def body(g_ref, w_ref, m_ref, v_ref, go_ref, d_ref, mo_ref, vo_ref):
        g = g_ref[0].astype(F32)
        for q in range(1, s):
            g = g + g_ref[q].astype(F32)
        m_new = ADAM_B1 * m_ref[...] + (1.0 - ADAM_B1) * g
        v_new = ADAM_B2 * v_ref[...] + (1.0 - ADAM_B2) * (g * g)
        go_ref[...] = g
        mo_ref[...] = m_new
        vo_ref[...] = v_new
        d_ref[...] = -ADAM_LR * ((m_new / c1) / (jnp.sqrt(v_new / c2) + ADAM_EPS) + ADAM_WD * w_ref[...])

    tile = pl.BlockSpec((tr, cn), lambda i: (i, 0))
    out = jax.ShapeDtypeStruct((r, cn), F32)
    return pl.pallas_call(
        body, name=name, grid=(r // tr,),
        in_specs=[pl.BlockSpec((s, tr, cn), lambda i: (0, i, 0)), tile, tile, tile],
        out_specs=[tile, tile, tile, tile], out_shape=[out, out, out, out],
        compiler_params=_params(),
    )(gstack, w, m, v)


def _rope_tables(pos, dim, base):
    inv = base ** (-jnp.arange(0, dim, 2, dtype=F32) / dim)
    ang = pos.astype(F32)[:, None] * inv[None, :]
    return jnp.cos(ang)[:, None, :], jnp.sin(ang)[:, None, :]


def _rotate(x, cos, sin):
    x1, x2 = jnp.split(x, 2, axis=-1)
    return jnp.concatenate([x1 * cos - x2 * sin, x2 * cos + x1 * sin], axis=-1)


def _heads(t, h):
    return jnp.swapaxes(t.reshape(t.shape[0], h, t.shape[1] // h), 0, 1)


def _cols_from_stack(w):
    return jnp.swapaxes(w, 0, 1).reshape(w.shape[1], N_DEV * w.shape[2])


def _stage_a(p, ctx, silu_c_all, me):
    x = p["x"]
    d = x.shape[1]
    n_a = p["ada_w"].shape[1]

    a_in = jnp.concatenate([silu_c_all, jax.nn.silu(p["c_ctx"])[None, :], jnp.zeros((7, d), F32)], axis=0)
    b_loc = lax.dynamic_slice(p["ada_b"], (0, me * n_a), (1, n_a))
    r_loc = _make_small_mm("ada")(a_in, p["ada_w"]) + b_loc
    r_full = _make_gather_op("ada")(r_loc)
    m_lat = lax.dynamic_index_in_dim(r_full, me, axis=1, keepdims=False).reshape(N_MOD, 1, d)
    m_ctx = r_full[:, N_DEV, :].reshape(N_MOD, 1, d)

    x1 = _make_ffn_block("ffn1")(x, p["norm1_g"], m_lat[0], m_lat[1], m_lat[2], p["ffn1_w_in"], p["ffn1_w_out"])
    c1 = _make_ffn_block("ffn1c")(ctx, p["norm1_g"], m_ctx[0], m_ctx[1], m_ctx[2], p["ffn1_w_in"], p["ffn1_w_out"])
    return x1, c1, m_lat, m_ctx


def _stage_b(p, x1, c1, m_lat, m_ctx):
    n_lat, d = x1.shape
    n_ctx = c1.shape[0]
    w_mix = jnp.pad(_cols_from_stack(p["mix_w_in"]), ((0, 0), (0, MIX_IN_PAD - MIX_IN)))
    proj = _make_norm_proj("mix")(x1, p["norm2_g"], m_lat[3], m_lat[4], w_mix)
    proj_c = _make_norm_proj("mixc")(c1, p["norm2_g"], m_ctx[3], m_ctx[4], w_mix)
    widths = SPLITS[:6] + (LANE,)
    rq, rk, rv, rg, cq, ckv, kr = _make_split("mixsplit", widths, MIX_IN_PAD)(proj)
    _, crk, crv, _, _, cckv, ckr = _make_split("mixsplitc", widths, MIX_IN_PAD)(proj_c)

    zq = jnp.zeros((1, MLA_Q_RANK), F32)
    zkv = jnp.zeros((1, MLA_KV_RANK), F32)
    w_uq3 = _cols_from_stack(p["mla_w_uq"]).reshape(MLA_Q_RANK, MLA_HEADS, MLA_NOPE + MLA_ROPE)
    w_uq = jnp.concatenate([w_uq3[:, :, :MLA_NOPE].reshape(MLA_Q_RANK, -1),
                            w_uq3[:, :, MLA_NOPE:].reshape(MLA_Q_RANK, -1)], axis=1)
    w_ukv = _cols_from_stack(p["mla_w_ukv"])
    q = _make_norm_proj("uq")(cq, p["mla_q_norm_g"], zq, zq, w_uq)
    kv = _make_norm_proj("ukv")(ckv, p["mla_kv_norm_g"], zkv, zkv, w_ukv)
    kv_c = _make_norm_proj("ukvc")(cckv, p["mla_kv_norm_g"], zkv, zkv, w_ukv)

    lg_f = jax.nn.log_sigmoid(p["ret_decay_fwd"][0])
    lg_b = jax.nn.log_sigmoid(p["ret_decay_bwd"][0])
    ret_tab = _rope_tables(jnp.arange(n_lat), RET_DK, RET_ROPE_BASE)
    rq_h = jnp.swapaxes(_rotate(rq.reshape(n_lat, RET_HEADS, RET_DK), *ret_tab), 0, 1)
    rk_h = jnp.swapaxes(_rotate((rk * (RET_DK ** -0.5)).reshape(n_lat, RET_HEADS, RET_DK), *ret_tab), 0, 1)
    rv_h = _heads(rv, RET_HEADS)
    crk_h = _heads(crk * (RET_DK ** -0.5), RET_HEADS)
    crv_h = _heads(crv, RET_HEADS)
    assert n_ctx == RET_CHUNK, "the context prefix is one retention chunk"

    def lanes(lg):
        return jnp.broadcast_to(lg[:, None, None], (RET_HEADS, 1, LANE))

    s0_f = _make_ctx_state("retcf", False)(crk_h, crv_h, lanes(lg_f))
    s0_b = _make_ctx_state("retcb", True)(crk_h, crv_h, lanes(lg_b))
    y_f = _make_ret_dir("retf", False)(rq_h, rk_h, rv_h, lanes(lg_f), s0_f)
    y_b = _make_ret_dir("retb", True)(rq_h, rk_h, rv_h, lanes(lg_b), s0_b)
    ret_o = _make_ret_out("reto")(y_f, y_b, rg)

    mla_o = _make_mla("mla", n_lat, n_ctx)(q, kv, kr, kv_c, ckr)

    w_mo = p["mix_w_out"].reshape(-1, d)
    return _make_res_proj("mixo")(jnp.concatenate([ret_o, mla_o], axis=-1), w_mo, x1, m_lat[5])


def _stage_c(p, x2, m_lat, tgt):
    x3 = _make_ffn_block("ffn2")(x2, p["norm3_g"], m_lat[6], m_lat[7], m_lat[8], p["ffn2_w_in"], p["ffn2_w_out"])
    return _make_final_loss("loss")(x3, p["final_norm_g"], tgt)


FIRST = ("ffn1_w_in", "ffn1_w_out")
MID = ("mix_w_in", "mla_w_uq", "mla_w_ukv", "mix_w_out")
LAST = ("ffn2_w_in", "ffn2_w_out")
BIG = FIRST + MID + LAST
SMALL = ("c_ctx", "ada_b", "norm1_g", "norm2_g", "ret_decay_fwd", "ret_decay_bwd", "mla_q_norm_g",
         "mla_kv_norm_g", "norm3_g", "final_norm_g")
WEIGHTS = ("c_ctx", "ada_w", "ada_b", "norm1_g", "ffn1_w_in", "ffn1_w_out", "norm2_g", "mix_w_in", "ret_decay_fwd",
           "ret_decay_bwd", "mla_q_norm_g", "mla_w_uq", "mla_kv_norm_g", "mla_w_ukv", "mix_w_out", "norm3_g",
           "ffn2_w_in", "ffn2_w_out", "final_norm_g")


def _pack(parts):
    flat = jnp.concatenate([t.reshape(-1) for t in parts])
    pad = (-flat.shape[0]) % LANE
    return jnp.pad(flat, (0, pad)).reshape(1, -1)


def _unpack(flat, like):
    out, off = [], 0
    for t in like:
        out.append(flat[0, off:off + t.size].reshape(t.shape))
        off += t.size
    return out


def kernel(x, c, ctx, c_ctx, ada_w, ada_b, norm1_g, ffn1_w_in, ffn1_w_out, norm2_g, mix_w_in, ret_decay_fwd, ret_decay_bwd, mla_q_norm_g, mla_w_uq, mla_kv_norm_g, mla_w_ukv, mix_w_out, norm3_g, ffn2_w_in, ffn2_w_out, final_norm_g, loss_target, m_c_ctx, m_ada_w, m_ada_b, m_norm1_g, m_ffn1_w_in, m_ffn1_w_out, m_norm2_g, m_mix_w_in, m_ret_decay_fwd, m_ret_decay_bwd, m_mla_q_norm_g, m_mla_w_uq, m_mla_kv_norm_g, m_mla_w_ukv, m_mix_w_out, m_norm3_g, m_ffn2_w_in, m_ffn2_w_out, m_final_norm_g, v_c_ctx, v_ada_w, v_ada_b, v_norm1_g, v_ffn1_w_in, v_ffn1_w_out, v_norm2_g, v_mix_w_in, v_ret_decay_fwd, v_ret_decay_bwd, v_mla_q_norm_g, v_mla_w_uq, v_mla_kv_norm_g, v_mla_w_ukv, v_mix_w_out, v_norm3_g, v_ffn2_w_in, v_ffn2_w_out, v_final_norm_g):
    w = dict(c_ctx=c_ctx, ada_w=ada_w, ada_b=ada_b, norm1_g=norm1_g, ffn1_w_in=ffn1_w_in, ffn1_w_out=ffn1_w_out,
             norm2_g=norm2_g, mix_w_in=mix_w_in, ret_decay_fwd=ret_decay_fwd, ret_decay_bwd=ret_decay_bwd,
             mla_q_norm_g=mla_q_norm_g, mla_w_uq=mla_w_uq, mla_kv_norm_g=mla_kv_norm_g, mla_w_ukv=mla_w_ukv,
             mix_w_out=mix_w_out, norm3_g=norm3_g, ffn2_w_in=ffn2_w_in, ffn2_w_out=ffn2_w_out,
             final_norm_g=final_norm_g)
    mom = dict(c_ctx=m_c_ctx, ada_w=m_ada_w, ada_b=m_ada_b, norm1_g=m_norm1_g, ffn1_w_in=m_ffn1_w_in,
               ffn1_w_out=m_ffn1_w_out, norm2_g=m_norm2_g, mix_w_in=m_mix_w_in, ret_decay_fwd=m_ret_decay_fwd,
               ret_decay_bwd=m_ret_decay_bwd, mla_q_norm_g=m_mla_q_norm_g, mla_w_uq=m_mla_w_uq,
               mla_kv_norm_g=m_mla_kv_norm_g, mla_w_ukv=m_mla_w_ukv, mix_w_out=m_mix_w_out, norm3_g=m_norm3_g,
               ffn2_w_in=m_ffn2_w_in, ffn2_w_out=m_ffn2_w_out, final_norm_g=m_final_norm_g)
    var = dict(c_ctx=v_c_ctx, ada_w=v_ada_w, ada_b=v_ada_b, norm1_g=v_norm1_g, ffn1_w_in=v_ffn1_w_in,
               ffn1_w_out=v_ffn1_w_out, norm2_g=v_norm2_g, mix_w_in=v_mix_w_in, ret_decay_fwd=v_ret_decay_fwd,
               ret_decay_bwd=v_ret_decay_bwd, mla_q_norm_g=v_mla_q_norm_g, mla_w_uq=v_mla_w_uq,
               mla_kv_norm_g=v_mla_kv_norm_g, mla_w_ukv=v_mla_w_ukv, mix_w_out=v_mix_w_out, norm3_g=v_norm3_g,
               ffn2_w_in=v_ffn2_w_in, ffn2_w_out=v_ffn2_w_out, final_norm_g=v_final_norm_g)
    me = 4 * lax.axis_index("x") + 2 * lax.axis_index("y") + lax.axis_index("c")

    shard = {k: w[k][0].astype(BF16) for k in BIG}
    first = _gather_two_level([shard[k] for k in FIRST] + [jax.nn.silu(c)], "weights_gather")
    silu_c_all = first[-1][:, 0, :]
    mid_start = _exchange_start([shard[k] for k in MID], [_own_slot(shard[k], me) for k in MID], first[0], True,
                                "mixer_weights_start")
    last_start = _exchange_start([shard[k] for k in LAST], [_own_slot(shard[k], me) for k in LAST], mid_start[4],
                                 True, "ffn2_weights_start")

    pa = dict(zip(FIRST, (last_start[4], first[1])), x=x[0], ada_w=ada_w[0], c_ctx=c_ctx, ada_b=ada_b,
              norm1_g=norm1_g)
    (x1, c1, m_lat, m_ctx), vjp_a = jax.vjp(lambda q: _stage_a(q, ctx[0], silu_c_all, me), pa)

    mid = _exchange_wait(mid_start[0], mid_start[1], mid_start[2], mid_start[3], x1, True, "mixer_weights_wait")
    pb = dict(zip(MID, mid))
    for k in ("norm2_g", "mla_q_norm_g", "mla_kv_norm_g", "ret_decay_fwd", "ret_decay_bwd"):
        pb[k] = w[k]
    x2, vjp_b = jax.vjp(_stage_b, pb, x1, c1, m_lat, m_ctx)

    last = _exchange_wait(last_start[0], last_start[1], last_start[2], last_start[3], x2, True, "ffn2_weights_wait")
    pc = dict(zip(LAST, last), norm3_g=norm3_g, final_norm_g=final_norm_g[None, :])
    loss_local, vjp_c = jax.vjp(lambda q, t, m: _stage_c(q, t, m, loss_target[0]), pc, x2, m_lat)

    gc, dx2, dm_c = vjp_c(jnp.ones((), F32))
    last_scat = _exchange_start([gc[k] for k in LAST],
                                [_own_slot(lax.dynamic_index_in_dim(gc[k], me, 0, False), me) for k in LAST],
                                dx2, False, "ffn2_grads_start")
    gb, dx1, dc1, dm_b, dmc_b = vjp_b(last_scat[4])
    mid_scat = _exchange_start([gb[k] for k in MID],
                               [_own_slot(lax.dynamic_index_in_dim(gb[k], me, 0, False), me) for k in MID],
                               dx1, False, "mixer_grads_start")
    (ga,) = vjp_a((mid_scat[4], dc1, dm_b + dm_c, dmc_b))
    grads = {**ga, **gb, **gc}
    grads["final_norm_g"] = grads["final_norm_g"][0]

    core = lax.axis_index("c").astype(jnp.int32).reshape(1)
    full = [grads[k] for k in FIRST]
    theirs = _swap_sibling(full, "grads_swap")
    paired = [_pair_add(f, t, core, "grads_pair_" + k) for k, f, t in zip(FIRST, full, theirs)]
    exchanged = dict(zip(FIRST, _scatter_chips(paired, "grads_scatter")))
    exchanged.update(zip(LAST, _exchange_wait(last_scat[0], last_scat[1], last_scat[2], last_scat[3], grads["x"],
                                              False, "ffn2_grads_wait")))
    exchanged.update(zip(MID, _exchange_wait(mid_scat[0], mid_scat[1], mid_scat[2], mid_scat[3], grads["x"],
                                             False, "mixer_grads_wait")))
    zero1 = [jnp.zeros((1,), F32)]
    small_like = zero1 + [w[k] for k in SMALL]
    small_all = _exchange([_pack([loss_local.reshape(1)] + [grads[k] for k in SMALL])], True, "small_grads_gather")[0]
    loss = jnp.sum(small_all[:, 0, 0])

    out_g, out_d, out_m, out_v = {}, {}, {}, {}

    def update(name, gstack, shape2d):
        res = _adamw(gstack, w[name].reshape(shape2d), mom[name].reshape(shape2d), var[name].reshape(shape2d),
                     "adamw_" + name)
        out_g[name], out_d[name], out_m[name], out_v[name] = [t.reshape(w[name].shape) for t in res]

    for k in BIG:
        update(k, exchanged[k], exchanged[k].shape[1:])
    update("ada_w", grads["ada_w"][None], ada_w.shape[1:])
    res = _adamw(small_all, _pack(small_like), _pack(zero1 + [mom[k] for k in SMALL]),
                 _pack(zero1 + [var[k] for k in SMALL]), "adamw_small")
    for dst, flat in zip((out_g, out_d, out_m, out_v), res):
        for k, t in zip(SMALL, _unpack(flat, small_like)[1:]):
            dst[k] = t

    return (loss, grads["x"][None], *[out_g[k] for k in WEIGHTS], *[out_d[k] for k in WEIGHTS],
            *[out_m[k] for k in WEIGHTS], *[out_v[k] for k in WEIGHTS])
```

```python
import functools

import jax
import jax.numpy as jnp
from jax import lax
from jax.experimental import pallas as pl
from jax.experimental.pallas import tpu as pltpu

F32 = jnp.float32
BF16 = jnp.bfloat16

N_DEV = 8
MESH_AXES = ("x", "y", "c")

GRID_W = 64
N_MOD = 9
RET_HEADS = 8
RET_DK = 64
RET_DV = 128
RET_CHUNK = 256
RET_ROPE_BASE = 10000.0
MLA_HEADS = 8
MLA_Q_RANK = 512
MLA_KV_RANK = 256
MLA_NOPE = 128
MLA_ROPE = 64
MLA_V = 128
AXIAL_BASE = 10000.0
RMS_EPS = 1e-6
GN_EPS = 1e-5
SPLITS = (RET_HEADS * RET_DK, RET_HEADS * RET_DK, RET_HEADS * RET_DV, RET_HEADS * RET_DV,
          MLA_Q_RANK, MLA_KV_RANK, MLA_ROPE)
MIX_IN = sum(SPLITS)
MIX_IN_PAD = 4096

ADAM_LR = 0.001
ADAM_B1 = 0.9
ADAM_B2 = 0.999
ADAM_EPS = 1e-08
ADAM_WD = 0.01
ADAM_STEP = 10

LANE = 128
RET_DKP = LANE
VMEM_LIMIT_BYTES = 56 * 1024 * 1024

NN = ((1,), (0,))
NT = ((1,), (1,))
TN = ((0,), (0,))


def _pick(dim, target, align=LANE):
    t = min(dim, target)
    t -= t % align
    while t >= align:
        if dim % t == 0:
            return t
        t -= align
    return dim


def _params():
    return pltpu.CompilerParams(vmem_limit_bytes=VMEM_LIMIT_BYTES)


def _dot(a, b, dims):
    return lax.dot_general(a.astype(BF16), b.astype(BF16), (dims, ((), ())), preferred_element_type=F32)


def _mm_call(name, grid, ins, pairs, outs, acc_shapes, epilogue):
    n_in, n_out = len(ins), len(outs)
    k_axis = len(grid) - 1
    k_steps = grid[k_axis]

    def body(*refs):
        in_refs = refs[:n_in]
        out_refs = refs[n_in:n_in + n_out]
        accs = refs[n_in + n_out:]
        k = pl.program_id(k_axis)

        @pl.when(k == 0)
        def _():
            for acc in accs:
                acc[...] = jnp.zeros_like(acc)

        for ai, bi, dims, ci in pairs:
            accs[ci][...] += _dot(in_refs[ai][...], in_refs[bi][...], dims)

        @pl.when(k == k_steps - 1)
        def _():
            epilogue([acc[...] for acc in accs], in_refs, out_refs)

    res = pl.pallas_call(
        body, name=name, grid=grid,
        in_specs=[s for _, s in ins], out_specs=[s for _, s in outs],
        out_shape=[s for s, _ in outs],
        scratch_shapes=[pltpu.VMEM(s, F32) for s in acc_shapes],
        compiler_params=_params(),
    )(*[a for a, _ in ins])
    return res


def _matmul(a, b, mode, out_dtype, name, tm=1024, tn=1024, tk=512):
    if mode == "nn":
        (m, kd), n = a.shape, b.shape[1]
    elif mode == "nt":
        (m, kd), n = a.shape, b.shape[0]
    else:
        (kd, m), n = a.shape, b.shape[1]
    tm, tn = _pick(m, tm, 16), _pick(n, tn)
    tk = _pick(kd, tk) if mode != "tn" else _pick(kd, tk, 16)
    if mode == "nn":
        a_spec = pl.BlockSpec((tm, tk), lambda i, j, k: (i, k))
        b_spec = pl.BlockSpec((tk, tn), lambda i, j, k: (k, j))
        dims = NN
    elif mode == "nt":
        a_spec = pl.BlockSpec((tm, tk), lambda i, j, k: (i, k))
        b_spec = pl.BlockSpec((tn, tk), lambda i, j, k: (j, k))
        dims = NT
    else:
        a_spec = pl.BlockSpec((tk, tm), lambda i, j, k: (k, i))
        b_spec = pl.BlockSpec((tk, tn), lambda i, j, k: (k, j))
        dims = TN

    def epilogue(accs, in_refs, out_refs):
        out_refs[0][...] = accs[0].astype(out_dtype)

    return _mm_call(
        name, (m // tm, n // tn, kd // tk), [(a, a_spec), (b, b_spec)], [(0, 1, dims, 0)],
        [(jax.ShapeDtypeStruct((m, n), out_dtype), pl.BlockSpec((tm, tn), lambda i, j, k: (i, j)))],
        [(tm, tn)], epilogue)[0]


def _norm_mod_tile(x, ng, sc, sh):
    r = lax.rsqrt(jnp.mean(x * x, axis=-1, keepdims=True) + RMS_EPS)
    return (x * r * ng) * (1.0 + sc) + sh


def _row_spec(tm, d):
    return pl.BlockSpec((tm, d), lambda i: (i, 0))


def _vec_spec(d):
    return pl.BlockSpec((1, d), lambda i: (0, 0))


def _norm_mod_fwd(x, ng, sc, sh, name):
    t, d = x.shape
    tm = _pick(t, 512, 16)

    def body(x_ref, ng_ref, sc_ref, sh_ref, h_ref):
        h_ref[...] = _norm_mod_tile(x_ref[...], ng_ref[...], sc_ref[...], sh_ref[...]).astype(BF16)

    return pl.pallas_call(
        body, name=name, grid=(t // tm,),
        in_specs=[_row_spec(tm, d), _vec_spec(d), _vec_spec(d), _vec_spec(d)],
        out_specs=_row_spec(tm, d), out_shape=jax.ShapeDtypeStruct((t, d), BF16),
        compiler_params=_params(),
    )(x, ng, sc, sh)


def _norm_mod_bwd(x, ng, sc, sh, dh, dres, name):
    t, d = x.shape
    tm = _pick(t, 256, 16)
    has_res = dres is not None

    def body(*refs):
        if has_res:
            x_ref, ng_ref, sc_ref, sh_ref, dh_ref, dres_ref, dx_ref, dng_ref, dsc_ref, dsh_ref = refs
        else:
            x_ref, ng_ref, sc_ref, sh_ref, dh_ref, dx_ref, dng_ref, dsc_ref, dsh_ref = refs
        _, vjp = jax.vjp(_norm_mod_tile, x_ref[...], ng_ref[...], sc_ref[...], sh_ref[...])
        dx, dng, dsc, dsh = vjp(dh_ref[...].astype(F32))
        if has_res:
            dx = dx + dres_ref[...]
        dx_ref[...] = dx

        @pl.when(pl.program_id(0) == 0)
        def _():
            dng_ref[...] = jnp.zeros_like(dng_ref)
            dsc_ref[...] = jnp.zeros_like(dsc_ref)
            dsh_ref[...] = jnp.zeros_like(dsh_ref)

        dng_ref[...] += dng
        dsc_ref[...] += dsc
        dsh_ref[...] += dsh

    ins = [x, ng, sc, sh, dh] + ([dres] if has_res else [])
    in_specs = [_row_spec(tm, d), _vec_spec(d), _vec_spec(d), _vec_spec(d), _row_spec(tm, d)]
    in_specs += [_row_spec(tm, d)] if has_res else []
    vec = jax.ShapeDtypeStruct((1, d), F32)
    return pl.pallas_call(
        body, name=name, grid=(t // tm,), in_specs=in_specs,
        out_specs=[_row_spec(tm, d), _vec_spec(d), _vec_spec(d), _vec_spec(d)],
        out_shape=[jax.ShapeDtypeStruct((t, d), F32), vec, vec, vec],
        compiler_params=_params(),
    )(*ins)


def _res_mm_fwd(a, w, x, gate, coef, name):
    t, kd = a.shape
    d = w.shape[1]
    tm, tn, tk = _pick(t, 1024, 16), _pick(d, 1024), _pick(kd, 2816)

    def epilogue(accs, in_refs, out_refs):
        f = accs[0]
        out_refs[0][...] = in_refs[2][...] + (coef * in_refs[3][...]) * f
        out_refs[1][...] = f.astype(BF16)

    tile = pl.BlockSpec((tm, tn), lambda i, j, k: (i, j))
    return _mm_call(
        name, (t // tm, d // tn, kd // tk),
        [(a, pl.BlockSpec((tm, tk), lambda i, j, k: (i, k))), (w, pl.BlockSpec((tk, tn), lambda i, j, k: (k, j))),
         (x, tile), (gate, pl.BlockSpec((1, tn), lambda i, j, k: (0, j)))],
        [(0, 1, NN, 0)],
        [(jax.ShapeDtypeStruct((t, d), F32), tile), (jax.ShapeDtypeStruct((t, d), BF16), tile)],
        [(tm, tn)], epilogue)


def _gate_bwd(dxo, f, gate, coef, name):
    t, d = dxo.shape
    tm = _pick(t, 512, 16)

    def body(dxo_ref, f_ref, gate_ref, df_ref, dgate_ref):
        dxo_t = dxo_ref[...]
        df_ref[...] = ((coef * gate_ref[...]) * dxo_t).astype(BF16)

        @pl.when(pl.program_id(0) == 0)
        def _():
            dgate_ref[...] = jnp.zeros_like(dgate_ref)

        dgate_ref[...] += coef * jnp.sum(dxo_t * f_ref[...].astype(F32), axis=0, keepdims=True)

    return pl.pallas_call(
        body, name=name, grid=(t // tm,),
        in_specs=[_row_spec(tm, d), _row_spec(tm, d), _vec_spec(d)],
        out_specs=[_row_spec(tm, d), _vec_spec(d)],
        out_shape=[jax.ShapeDtypeStruct((t, d), BF16), jax.ShapeDtypeStruct((1, d), F32)],
        compiler_params=_params(),
    )(dxo, f, gate)


def _ffn_in_fwd(h, w_in, name):
    t, d = h.shape
    n = w_in.shape[2]
    half = N_DEV // 2
    f = half * n
    tm = _pick(t, 512, 16)

    def epilogue(accs, in_refs, out_refs):
        g, u = accs
        out_refs[0][...] = (g * jax.nn.sigmoid(g) * u).astype(BF16)
        out_refs[1][0] = g.astype(BF16)
        out_refs[1][1] = u.astype(BF16)

    return _mm_call(
        name, (half, t // tm, 1),
        [(h, pl.BlockSpec((tm, d), lambda j, i, k: (i, 0))),
         (w_in, pl.BlockSpec((None, d, n), lambda j, i, k: (j, 0, 0))),
         (w_in, pl.BlockSpec((None, d, n), lambda j, i, k: (j + half, 0, 0)))],
        [(0, 1, NN, 0), (0, 2, NN, 1)],
        [(jax.ShapeDtypeStruct((t, f), BF16), pl.BlockSpec((tm, n), lambda j, i, k: (i, j))),
         (jax.ShapeDtypeStruct((2, t, f), BF16), pl.BlockSpec((2, tm, n), lambda j, i, k: (0, i, j)))],
        [(tm, n), (tm, n)], epilogue)


def _ffn_da_bwd(df, w_out2d, gu, name):
    t, d = df.shape
    f = w_out2d.shape[0]
    half = N_DEV // 2
    n = f // half
    tm = _pick(t, 512, 16)

    def epilogue(accs, in_refs, out_refs):
        da = accs[0]
        g = in_refs[2][0].astype(F32)
        u = in_refs[2][1].astype(F32)
        s = jax.nn.sigmoid(g)
        out_refs[0][0] = (da * u * (s * (1.0 + g * (1.0 - s)))).astype(BF16)
        out_refs[0][1] = (da * (g * s)).astype(BF16)

    gu_spec = pl.BlockSpec((2, tm, n), lambda j, i, k: (0, i, j))
    return _mm_call(
        name, (half, t // tm, 1),
        [(df, pl.BlockSpec((tm, d), lambda j, i, k: (i, 0))),
         (w_out2d, pl.BlockSpec((n, d), lambda j, i, k: (j, 0))),
         (gu, gu_spec)],
        [(0, 1, NT, 0)],
        [(jax.ShapeDtypeStruct((2, t, f), BF16), gu_spec)],
        [(tm, n)], epilogue)[0]


def _ffn_dh_bwd(dgu, w_in, name):
    _, t, f = dgu.shape
    d, n = w_in.shape[1], w_in.shape[2]
    half = N_DEV // 2
    tm = _pick(t, 512, 16)

    def epilogue(accs, in_refs, out_refs):
        out_refs[0][...] = accs[0]

    return _mm_call(
        name, (t // tm, 1, half),
        [(dgu, pl.BlockSpec((None, tm, n), lambda i, j, k: (0, i, k))),
         (dgu, pl.BlockSpec((None, tm, n), lambda i, j, k: (1, i, k))),
         (w_in, pl.BlockSpec((None, d, n), lambda i, j, k: (k, 0, 0))),
         (w_in, pl.BlockSpec((None, d, n), lambda i, j, k: (k + half, 0, 0)))],
        [(0, 2, NT, 0), (1, 3, NT, 0)],
        [(jax.ShapeDtypeStruct((t, d), F32), pl.BlockSpec((tm, d), lambda i, j, k: (i, 0)))],
        [(tm, d)], epilogue)[0]


def _ffn_dwin_bwd(h, dgu, name):
    t, d = h.shape
    f = dgu.shape[2]
    half = N_DEV // 2
    n = f // half
    tk = _pick(t, 1024, 16)

    def epilogue(accs, in_refs, out_refs):
        out_refs[0][...] = accs[0].astype(BF16)

    return _mm_call(
        name, (N_DEV, 1, t // tk),
        [(h, pl.BlockSpec((tk, d), lambda j, i, k: (k, 0))),
         (dgu, pl.BlockSpec((None, tk, n), lambda j, i, k: (j // half, k, j % half)))],
        [(0, 1, TN, 0)],
        [(jax.ShapeDtypeStruct((N_DEV, d, n), BF16), pl.BlockSpec((None, d, n), lambda j, i, k: (j, 0, 0)))],
        [(d, n)], epilogue)[0]


def _make_ffn_block(tag):
    @jax.custom_vjp
    def ffn_block(x, ng, sh, sc, gate, w_in, w_out):
        return fwd(x, ng, sh, sc, gate, w_in, w_out)[0]

    def fwd(x, ng, sh, sc, gate, w_in, w_out):
        f = w_out.shape[0] * w_out.shape[1]
        w_out2d = w_out.reshape(f, w_out.shape[2])
        h = _norm_mod_fwd(x, ng, sc, sh, tag + "_norm")
        a, gu = _ffn_in_fwd(h, w_in, tag + "_in")
        xo, f1 = _res_mm_fwd(a, w_out2d, x, gate, 0.5, tag + "_out")
        return xo, (x, ng, sh, sc, gate, w_in, w_out, h, a, gu, f1)

    def bwd(res, dxo):
        x, ng, sh, sc, gate, w_in, w_out, h, a, gu, f1 = res
        f = w_out.shape[0] * w_out.shape[1]
        w_out2d = w_out.reshape(f, w_out.shape[2])
        df, dgate = _gate_bwd(dxo, f1, gate, 0.5, tag + "_dgate")
        dgu = _ffn_da_bwd(df, w_out2d, gu, tag + "_da")
        dw_out = _matmul(a, df, "tn", BF16, tag + "_dwout", tm=_pick(f, 1408, 16), tn=2048, tk=1024)
        dh = _ffn_dh_bwd(dgu, w_in, tag + "_dh")
        dw_in = _ffn_dwin_bwd(h, dgu, tag + "_dwin")
        dx, dng, dsc, dsh = _norm_mod_bwd(x, ng, sc, sh, dh, dxo, tag + "_dnorm")
        return dx, dng, dsh, dsc, dgate, dw_in, dw_out.reshape(w_out.shape)

    ffn_block.defvjp(fwd, bwd)
    return ffn_block


def _make_norm_proj(tag):
    @jax.custom_vjp
    def norm_proj(x, ng, sh, sc, w):
        return fwd(x, ng, sh, sc, w)[0]

    def fwd(x, ng, sh, sc, w):
        h = _norm_mod_fwd(x, ng, sc, sh, tag + "_norm")
        p = _matmul(h, w, "nn", F32, tag + "_mm", tm=1024, tn=1024, tk=w.shape[0])
        return p, (x, ng, sh, sc, w, h)

    def bwd(res, dp):
        x, ng, sh, sc, w, h = res
        dh = _matmul(dp, w, "nt", F32, tag + "_dh", tm=512, tn=w.shape[0], tk=2048)
        dw = _matmul(h, dp, "tn", BF16, tag + "_dw", tm=w.shape[0], tn=1024, tk=1024)
        dx, dng, dsc, dsh = _norm_mod_bwd(x, ng, sc, sh, dh, None, tag + "_dnorm")
        return dx, dng, dsh, dsc, dw

    norm_proj.defvjp(fwd, bwd)
    return norm_proj


def _make_split(tag, widths, total):
    offs = [sum(widths[:i]) for i in range(len(widths))]

    def concat_call(pieces):
        t = pieces[0].shape[0]
        tm = _pick(t, 256, 16)

        def body(*refs):
            o_ref = refs[-1]
            for ref, off, wd in zip(refs[:-1], offs, widths):
                o_ref[:, off:off + wd] = ref[...]
            end = offs[-1] + widths[-1]
            if end < total:
                o_ref[:, end:] = jnp.zeros((tm, total - end), F32)

        return pl.pallas_call(
            body, name=tag + "_concat", grid=(t // tm,),
            in_specs=[_row_spec(tm, wd) for wd in widths], out_specs=_row_spec(tm, total),
            out_shape=jax.ShapeDtypeStruct((t, total), F32), compiler_params=_params(),
        )(*pieces)

    @jax.custom_vjp
    def split(p):
        return tuple(p[:, off:off + wd] for off, wd in zip(offs, widths))

    def fwd(p):
        return split(p), None

    def bwd(_, cts):
        return (concat_call(list(cts)),)

    split.defvjp(fwd, bwd)
    return split


def _make_res_proj(tag):
    @jax.custom_vjp
    def res_proj(a, w, x, gate):
        return fwd(a, w, x, gate)[0]

    def fwd(a, w, x, gate):
        xo, f = _res_mm_fwd(a, w, x, gate, 1.0, tag + "_mm")
        return xo, (a, w, gate, f)

    def bwd(res, dxo):
        a, w, gate, f = res
        df, dgate = _gate_bwd(dxo, f, gate, 1.0, tag + "_dgate")
        da = _matmul(df, w, "nt", BF16, tag + "_da", tm=1024, tn=1024, tk=2048)
        dw = _matmul(a, df, "tn", BF16, tag + "_dw", tm=1024, tn=2048, tk=1024)
        return da, dw, dxo, dgate

    res_proj.defvjp(fwd, bwd)
    return res_proj


def _make_small_mm(tag):
    @jax.custom_vjp
    def small_mm(a, w):
        return _matmul(a, w, "nn", F32, tag + "_mm", tm=a.shape[0], tn=768, tk=w.shape[0])

    def fwd(a, w):
        return small_mm(a, w), (a, w)

    def bwd(res, dr):
        a, w = res
        da = _matmul(dr, w, "nt", F32, tag + "_da", tm=a.shape[0], tn=w.shape[0], tk=768)
        dw = _matmul(a, dr, "tn", F32, tag + "_dw", tm=1024, tn=768, tk=a.shape[0])
        return da, dw

    small_mm.defvjp(fwd, bwd)
    return small_mm


def _ret_chunk_terms(lg, c, reverse):
    row = lax.broadcasted_iota(jnp.int32, (c, c), 0).astype(F32)
    col = lax.broadcasted_iota(jnp.int32, (c, c), 1).astype(F32)
    pos = lax.broadcasted_iota(jnp.int32, (c, 1), 0).astype(F32)
    if reverse:
        diff = col - row
        mask = diff > 0.0
        e_exp = float(c) - pos
        f_exp = pos
    else:
        diff = row - col
        mask = diff >= 0.0
        e_exp = pos + 1.0
        f_exp = float(c - 1) - pos
    diffm = jnp.where(mask, diff, 0.0)
    dm = jnp.where(mask, jnp.exp(lg * diffm), 0.0)
    return diffm, dm, e_exp, jnp.exp(lg * e_exp), f_exp, jnp.exp(lg * f_exp)


def _lane0(val):
    lane = lax.broadcasted_iota(jnp.int32, (1, LANE), 1)
    return jnp.where(lane == 0, val, 0.0)


RET_HEAD_BLOCK = 4


def _make_ret_dir(tag, reverse):
    hb = RET_HEAD_BLOCK

    def heads_spec(nc, width, flip):
        if flip:
            return pl.BlockSpec((hb, RET_CHUNK, width), lambda h, t: (h, nc - 1 - t, 0))
        return pl.BlockSpec((hb, RET_CHUNK, width), lambda h, t: (h, t, 0))

    def state_spec(nc, flip):
        if flip:
            return pl.BlockSpec((hb, None, RET_DKP, RET_DV), lambda h, t: (h, nc - 1 - t, 0, 0))
        return pl.BlockSpec((hb, None, RET_DKP, RET_DV), lambda h, t: (h, t, 0, 0))

    lg_spec = pl.BlockSpec((hb, 1, LANE), lambda h, t: (h, 0, 0))
    s0_spec = pl.BlockSpec((hb, RET_DKP, RET_DV), lambda h, t: (h, 0, 0))

    def fwd_call(q, k, v, lgb, s0):
        hh, ll, _ = q.shape
        c = RET_CHUNK
        nc = ll // c

        def body(q_ref, k_ref, v_ref, lg_ref, s0_ref, y_ref, sall_ref, s_scr):
            @pl.when(pl.program_id(1) == 0)
            def _():
                s_scr[...] = s0_ref[...]

            for b in range(hb):
                lg = lg_ref[b][:, :1]
                _, dm, _, xi, _, zeta = _ret_chunk_terms(lg, c, reverse)
                q_t, k_t, v_t = q_ref[b], k_ref[b], v_ref[b]
                s = s_scr[b]
                p = _dot(q_t, k_t, NT) * dm
                y_ref[b] = _dot(p, v_t, NN) + _dot(q_t * xi, s, NN)
                sall_ref[b] = s
                s_scr[b] = jnp.exp(lg * float(c)) * s + _dot(k_t * zeta, v_t, TN)

        return pl.pallas_call(
            body, name=tag + "_fwd", grid=(hh // hb, nc),
            in_specs=[heads_spec(nc, RET_DKP,reverse), heads_spec(nc, RET_DKP,reverse),
                      heads_spec(nc, RET_DV, reverse), lg_spec, s0_spec],
            out_specs=[heads_spec(nc, RET_DV, reverse), state_spec(nc, reverse)],
            out_shape=[jax.ShapeDtypeStruct((hh, ll, RET_DV), F32),
                       jax.ShapeDtypeStruct((hh, nc, RET_DKP, RET_DV), F32)],
            scratch_shapes=[pltpu.VMEM((hb, RET_DKP, RET_DV), F32)],
            compiler_params=_params(),
        )(q, k, v, lgb, s0)

    def bwd_call(q, k, v, lgb, sall, dy):
        hh, ll, _ = q.shape
        c = RET_CHUNK
        nc = ll // c
        flip = not reverse

        def body(q_ref, k_ref, v_ref, lg_ref, sall_ref, dy_ref, dq_ref, dk_ref, dv_ref, dlg_ref, ds0_ref, ds_scr):
            @pl.when(pl.program_id(1) == 0)
            def _():
                ds_scr[...] = jnp.zeros_like(ds_scr)
                dlg_ref[...] = jnp.zeros_like(dlg_ref)

            def total(m):
                return jnp.sum(jnp.sum(m, axis=1, keepdims=True), axis=0, keepdims=True)

            for b in range(hb):
                lg = lg_ref[b][:, :1]
                diffm, dm, e_exp, xi, f_exp, zeta = _ret_chunk_terms(lg, c, reverse)
                q_t, k_t, v_t, dy_t = q_ref[b], k_ref[b], v_ref[b], dy_ref[b]
                s = sall_ref[b]
                dsn = ds_scr[b]
                a = _dot(q_t, k_t, NT)
                da = _dot(dy_t, v_t, NT) * dm
                g = _dot(dy_t, s, NT)
                hm = _dot(v_t, dsn, NT)
                dq_ref[b] = _dot(da, k_t, NN) + xi * g
                dk_ref[b] = _dot(da, q_t, TN) + zeta * hm
                dv_ref[b] = _dot(a * dm, dy_t, TN) + _dot(k_t * zeta, dsn, NN)
                gc = jnp.exp(lg * float(c))
                ds_scr[b] = gc * dsn + _dot(q_t * xi, dy_t, TN)
                dl = (total(da * a * diffm) + total(e_exp * xi * q_t * g)
                      + float(c) * gc * total(s * dsn) + total(f_exp * zeta * k_t * hm))
                dlg_ref[b] += _lane0(dl)

            @pl.when(pl.program_id(1) == nc - 1)
            def _():
                ds0_ref[...] = ds_scr[...]

        return pl.pallas_call(
            body, name=tag + "_bwd", grid=(hh // hb, nc),
            in_specs=[heads_spec(nc, RET_DKP,flip), heads_spec(nc, RET_DKP,flip), heads_spec(nc, RET_DV, flip),
                      lg_spec, state_spec(nc, flip), heads_spec(nc, RET_DV, flip)],
            out_specs=[heads_spec(nc, RET_DKP,flip), heads_spec(nc, RET_DKP,flip), heads_spec(nc, RET_DV, flip),
                       lg_spec, s0_spec],
            out_shape=[jax.ShapeDtypeStruct((hh, ll, RET_DKP), F32), jax.ShapeDtypeStruct((hh, ll, RET_DKP), F32),
                       jax.ShapeDtypeStruct((hh, ll, RET_DV), F32), jax.ShapeDtypeStruct((hh, 1, LANE), F32),
                       jax.ShapeDtypeStruct((hh, RET_DKP, RET_DV), F32)],
            scratch_shapes=[pltpu.VMEM((hb, RET_DKP, RET_DV), F32)],
            compiler_params=_params(),
        )(q, k, v, lgb, sall, dy)

    @jax.custom_vjp
    def ret_dir(q, k, v, lgb, s0):
        return fwd_call(q, k, v, lgb, s0)[0]

    def fwd(q, k, v, lgb, s0):
        y, sall = fwd_call(q, k, v, lgb, s0)
        return y, (q, k, v, lgb, sall)

    def bwd(res, dy):
        q, k, v, lgb, sall = res
        return tuple(bwd_call(q, k, v, lgb, sall, dy))

    ret_dir.defvjp(fwd, bwd)
    return ret_dir


def _make_ctx_state(tag, reverse):
    hb = RET_HEAD_BLOCK
    c = RET_CHUNK
    k_spec = pl.BlockSpec((hb, c, RET_DKP), lambda h: (h, 0, 0))
    v_spec = pl.BlockSpec((hb, c, RET_DV), lambda h: (h, 0, 0))
    lg_spec = pl.BlockSpec((hb, 1, LANE), lambda h: (h, 0, 0))
    s_spec = pl.BlockSpec((hb, RET_DKP, RET_DV), lambda h: (h, 0, 0))

    def fwd_call(k, v, lgb):
        hh = k.shape[0]

        def body(k_ref, v_ref, lg_ref, s_ref):
            for b in range(hb):
                _, _, _, _, _, zeta = _ret_chunk_terms(lg_ref[b][:, :1], c, reverse)
                s_ref[b] = _dot(k_ref[b] * zeta, v_ref[b], TN)

        return pl.pallas_call(
            body, name=tag + "_fwd", grid=(hh // hb,), in_specs=[k_spec, v_spec, lg_spec], out_specs=s_spec,
            out_shape=jax.ShapeDtypeStruct((hh, RET_DKP, RET_DV), F32), compiler_params=_params(),
        )(k, v, lgb)

    def bwd_call(k, v, lgb, ds):
        hh = k.shape[0]

        def body(k_ref, v_ref, lg_ref, ds_ref, dk_ref, dv_ref, dlg_ref):
            for b in range(hb):
                _, _, _, _, f_exp, zeta = _ret_chunk_terms(lg_ref[b][:, :1], c, reverse)
                k_t, v_t, ds = k_ref[b], v_ref[b], ds_ref[b]
                hm = _dot(v_t, ds, NT)
                dk_ref[b] = zeta * hm
                dv_ref[b] = _dot(k_t * zeta, ds, NN)
                tot = jnp.sum(jnp.sum(f_exp * zeta * k_t * hm, axis=1, keepdims=True), axis=0, keepdims=True)
                dlg_ref[b] = _lane0(tot)

        return pl.pallas_call(
            body, name=tag + "_bwd", grid=(hh // hb,), in_specs=[k_spec, v_spec, lg_spec, s_spec],
            out_specs=[k_spec, v_spec, lg_spec],
            out_shape=[jax.ShapeDtypeStruct(k.shape, F32), jax.ShapeDtypeStruct(v.shape, F32),
                       jax.ShapeDtypeStruct((hh, 1, LANE), F32)],
            compiler_params=_params(),
        )(k, v, lgb, ds)

    @jax.custom_vjp
    def ctx_state(k, v, lgb):
        return fwd_call(k, v, lgb)

    def fwd(k, v, lgb):
        return fwd_call(k, v, lgb), (k, v, lgb)

    def bwd(res, ds):
        return tuple(bwd_call(*res, ds))

    ctx_state.defvjp(fwd, bwd)
    return ctx_state


def _rope_tables_call(name, n, inv, shift, axial):
    tm = _pick(n, 1024, 8)
    inv_lane = jnp.tile(inv, LANE // inv.shape[0])[None, :]

    def body(inv_ref, cos_ref, s1_ref, s2_ref):
        t = lax.broadcasted_iota(jnp.int32, (tm, LANE), 0) + pl.program_id(0) * tm
        lane = lax.broadcasted_iota(jnp.int32, (tm, LANE), 1)
        if axial:
            pos = jnp.where(lane % (2 * MLA_ROPE // 2) < MLA_ROPE // 2, t // GRID_W, t % GRID_W)
        else:
            pos = t
        ang = pos.astype(F32) * inv_ref[...]
        sin = jnp.sin(ang)
        first = lane % (2 * shift) < shift
        cos_ref[...] = jnp.cos(ang)
        s1_ref[...] = jnp.where(first, -sin, 0.0)
        s2_ref[...] = jnp.where(first, 0.0, sin)

    tab = jax.ShapeDtypeStruct((n, LANE), F32)
    return tuple(pl.pallas_call(
        body, name=name, grid=(n // tm,), in_specs=[_vec_spec(LANE)], out_specs=[_row_spec(tm, LANE)] * 3,
        out_shape=[tab, tab, tab], compiler_params=_params(),
    )(inv_lane))


def _ret_tables(n_lat):
    inv = RET_ROPE_BASE ** (-jnp.arange(0, RET_DK, 2, dtype=F32) / RET_DK)
    return _rope_tables_call("ret_tables", n_lat, inv, RET_DK // 2, False)


def _make_ret_pack(tag, n_lat, n_ctx):
    hh = RET_HEADS
    tm = MLA_PACK_ROWS
    k_scale = RET_DK ** -0.5
    shift = RET_DK // 2
    tabs = _ret_tables(n_lat)

    def low_lanes():
        return lax.broadcasted_iota(jnp.int32, (1, LANE), 1) < RET_DK

    def rows(width):
        return pl.BlockSpec((tm, width), lambda i: (i, 0))

    def heads(width):
        return pl.BlockSpec((hh, tm, width), lambda i: (0, i, 0))

    def split_pairs(src_ref, dst_ref, scale, rope):
        keep = low_lanes()
        for j in range(hh // 2):
            blk = src_ref[:, LANE * j:LANE * (j + 1)]
            if scale != 1.0:
                blk = blk * scale
            if rope is not None:
                blk = _rope128(blk, *rope, shift=shift)
            dst_ref[2 * j] = jnp.where(keep, blk, 0.0)
            dst_ref[2 * j + 1] = jnp.where(keep, pltpu.roll(blk, RET_DK, 1), 0.0)

    def merge_pairs(src_ref, dst_ref, scale, rope):
        keep = low_lanes()
        for j in range(hh // 2):
            g = jnp.where(keep, src_ref[2 * j], pltpu.roll(src_ref[2 * j + 1], RET_DK, 1))
            if rope is not None:
                g = _rope128_t(g, *rope, shift=shift)
            dst_ref[:, LANE * j:LANE * (j + 1)] = g * scale if scale != 1.0 else g

    def pack_call(name, n, q, k, v, rope):
        with_q = q is not None

        def body(*refs):
            refs = list(refs)
            q_ref = refs.pop(0) if with_q else None
            k_ref, v_ref = refs.pop(0), refs.pop(0)
            tab = tuple(r[...] for r in refs[:3]) if rope else None
            outs = refs[3:] if rope else refs
            if with_q:
                split_pairs(q_ref, outs[0], 1.0, tab)
                outs = outs[1:]
            split_pairs(k_ref, outs[0], k_scale, tab)
            for h in range(hh):
                outs[1][h] = v_ref[:, RET_DV * h:RET_DV * (h + 1)]

        ins = ([q] if with_q else []) + [k, v] + (list(tabs) if rope else [])
        in_specs = ([rows(q.shape[1])] if with_q else []) + [rows(k.shape[1]), rows(v.shape[1])]
        in_specs += [rows(LANE)] * 3 if rope else []
        n_out = 3 if with_q else 2
        return pl.pallas_call(
            body, name=name, grid=(n // tm,), in_specs=in_specs,
            out_specs=[heads(RET_DKP)] * (n_out - 1) + [heads(RET_DV)],
            out_shape=[jax.ShapeDtypeStruct((hh, n, RET_DKP), F32)] * (n_out - 1)
            + [jax.ShapeDtypeStruct((hh, n, RET_DV), F32)],
            compiler_params=_params(),
        )(*ins)

    def unpack_call(name, n, dq, dk, dv, rope):
        with_q = dq is not None

        def body(*refs):
            refs = list(refs)
            dq_ref = refs.pop(0) if with_q else None
            dk_ref, dv_ref = refs.pop(0), refs.pop(0)
            tab = tuple(r[...] for r in refs[:3]) if rope else None
            outs = refs[3:] if rope else refs
            if with_q:
                merge_pairs(dq_ref, outs[0], 1.0, tab)
                outs = outs[1:]
            merge_pairs(dk_ref, outs[0], k_scale, tab)
            for h in range(hh):
                outs[1][:, RET_DV * h:RET_DV * (h + 1)] = dv_ref[h]

        ins = ([dq] if with_q else []) + [dk, dv] + (list(tabs) if rope else [])
        in_specs = [heads(RET_DKP)] * (2 if with_q else 1) + [heads(RET_DV)] + ([rows(LANE)] * 3 if rope else [])
        n_out = 3 if with_q else 2
        return pl.pallas_call(
            body, name=name, grid=(n // tm,), in_specs=in_specs,
            out_specs=[rows(hh * RET_DK)] * (n_out - 1) + [rows(hh * RET_DV)],
            out_shape=[jax.ShapeDtypeStruct((n, hh * RET_DK), F32)] * (n_out - 1)
            + [jax.ShapeDtypeStruct((n, hh * RET_DV), F32)],
            compiler_params=_params(),
        )(*ins)

    @jax.custom_vjp
    def ret_pack(rq, rk, rv, crk, crv):
        q, k, v = pack_call(tag + "_lat", n_lat, rq, rk, rv, True)
        k_c, v_c = pack_call(tag + "_ctx", n_ctx, None, crk, crv, False)
        return q, k, v, k_c, v_c

    def fwd(rq, rk, rv, crk, crv):
        return ret_pack(rq, rk, rv, crk, crv), None

    def bwd(_, cts):
        dq, dk, dv, dk_c, dv_c = cts
        drq, drk, drv = unpack_call(tag + "_dlat", n_lat, dq, dk, dv, True)
        dcrk, dcrv = unpack_call(tag + "_dctx", n_ctx, None, dk_c, dv_c, False)
        return drq, drk, drv, dcrk, dcrv

    ret_pack.defvjp(fwd, bwd)
    return ret_pack


def _ret_out_tile(y, g):
    mu = jnp.mean(y, axis=-1, keepdims=True)
    var = jnp.mean(jnp.square(y - mu), axis=-1, keepdims=True)
    return (g * jax.nn.sigmoid(g)) * ((y - mu) * lax.rsqrt(var + GN_EPS))


def _make_ret_out(tag):
    def specs(tm):
        y_spec = pl.BlockSpec((None, tm, RET_DV), lambda h, i: (h, i, 0))
        g_spec = pl.BlockSpec((tm, RET_DV), lambda h, i: (i, h))
        return y_spec, g_spec

    def fwd_call(yf, yb, g):
        hh, n, _ = yf.shape
        tm = _pick(n, 1024, 16)
        y_spec, g_spec = specs(tm)

        def body(yf_ref, yb_ref, g_ref, o_ref):
            o_ref[...] = _ret_out_tile(yf_ref[...] + yb_ref[...], g_ref[...]).astype(BF16)

        return pl.pallas_call(
            body, name=tag + "_fwd", grid=(hh, n // tm), in_specs=[y_spec, y_spec, g_spec], out_specs=g_spec,
            out_shape=jax.ShapeDtypeStruct((n, hh * RET_DV), BF16), compiler_params=_params(),
        )(yf, yb, g)

    def bwd_call(yf, yb, g, do):
        hh, n, _ = yf.shape
        tm = _pick(n, 1024, 16)
        y_spec, g_spec = specs(tm)

        def body(yf_ref, yb_ref, g_ref, do_ref, dy_ref, dg_ref):
            _, vjp = jax.vjp(_ret_out_tile, yf_ref[...] + yb_ref[...], g_ref[...])
            dy, dg = vjp(do_ref[...].astype(F32))
            dy_ref[...] = dy
            dg_ref[...] = dg

        return pl.pallas_call(
            body, name=tag + "_bwd", grid=(hh, n // tm), in_specs=[y_spec, y_spec, g_spec, g_spec],
            out_specs=[y_spec, g_spec],
            out_shape=[jax.ShapeDtypeStruct(yf.shape, F32), jax.ShapeDtypeStruct(g.shape, F32)],
            compiler_params=_params(),
        )(yf, yb, g, do)

    @jax.custom_vjp
    def ret_out(yf, yb, g):
        return fwd_call(yf, yb, g)

    def fwd(yf, yb, g):
        return fwd_call(yf, yb, g), (yf, yb, g)

    def bwd(res, do):
        dy, dg = bwd_call(*res, do)
        return dy, dy, dg

    ret_out.defvjp(fwd, bwd)
    return ret_out


MLA_DQ_PAD = 2 * LANE
MLA_PACK_ROWS = 256


def _rope128(x, cos, s1, s2, shift=16):
    return x * cos + pltpu.roll(x, LANE - shift, 1) * s1 + pltpu.roll(x, shift, 1) * s2


def _rope128_t(g, cos, s1, s2, shift=16):
    return g * cos + pltpu.roll(g * s1, shift, 1) + pltpu.roll(g * s2, LANE - shift, 1)


def _axial_tables(n_lat):
    half = MLA_ROPE // 2
    inv = AXIAL_BASE ** (-jnp.arange(0, half, 2, dtype=F32) / half)
    return _rope_tables_call("mla_tables", n_lat, inv, half // 2, True)


def _make_mla_pack(tag, n_lat, n_ctx, scale):
    hh = MLA_HEADS
    tm = MLA_PACK_ROWS
    ll = n_lat + n_ctx
    rope0 = hh * MLA_NOPE
    tabs = _axial_tables(n_lat)

    def rope_lanes():
        return lax.broadcasted_iota(jnp.int32, (1, LANE), 1) < MLA_ROPE

    def rows(width):
        return pl.BlockSpec((tm, width), lambda i: (i, 0))

    def heads(width, off):
        return pl.BlockSpec((hh, tm, width), lambda i: (0, i + off, 0))

    def put_kv(kv_ref, kr_rot, k_ref, v_ref):
        for h in range(hh):
            k_ref[h, :, :MLA_NOPE] = kv_ref[:, 2 * LANE * h:2 * LANE * h + MLA_NOPE].astype(BF16)
            k_ref[h, :, MLA_NOPE:] = kr_rot
            v_ref[h] = kv_ref[:, 2 * LANE * h + MLA_NOPE:2 * LANE * (h + 1)].astype(BF16)

    def fwd_lat(qp, kv, kr):
        def body(qp_ref, kv_ref, kr_ref, cos_ref, s1_ref, s2_ref, q_ref, k_ref, v_ref):
            cos, s1, s2 = cos_ref[...], s1_ref[...], s2_ref[...]
            keep = rope_lanes()
            for j in range(hh // 2):
                rot = _rope128(qp_ref[:, rope0 + LANE * j:rope0 + LANE * (j + 1)], cos, s1, s2)
                q_ref[2 * j, :, MLA_NOPE:] = jnp.where(keep, rot, 0.0).astype(BF16)
                q_ref[2 * j + 1, :, MLA_NOPE:] = jnp.where(keep, pltpu.roll(rot, MLA_ROPE, 1), 0.0).astype(BF16)
            for h in range(hh):
                q_ref[h, :, :MLA_NOPE] = qp_ref[:, MLA_NOPE * h:MLA_NOPE * (h + 1)].astype(BF16)
            kr_rot = jnp.where(keep, _rope128(kr_ref[...], cos, s1, s2), 0.0).astype(BF16)
            put_kv(kv_ref, kr_rot, k_ref, v_ref)

        return pl.pallas_call(
            body, name=tag + "_lat", grid=(n_lat // tm,),
            in_specs=[rows(qp.shape[1]), rows(kv.shape[1]), rows(LANE), rows(LANE), rows(LANE), rows(LANE)],
            out_specs=[heads(MLA_DQ_PAD, 0), heads(MLA_DQ_PAD, 0), heads(MLA_V, 0)],
            out_shape=[jax.ShapeDtypeStruct((hh, n_lat, MLA_DQ_PAD), BF16),
                       jax.ShapeDtypeStruct((hh, ll, MLA_DQ_PAD), BF16), jax.ShapeDtypeStruct((hh, ll, MLA_V), BF16)],
            compiler_params=_params(),
        )(qp, kv, kr, *tabs)

    def fwd_ctx(kv_c, kr_c, k_buf, v_buf):
        def body(kv_ref, kr_ref, k_in, v_in, k_ref, v_ref):
            kr_rot = jnp.where(rope_lanes(), kr_ref[...], 0.0).astype(BF16)
            put_kv(kv_ref, kr_rot, k_ref, v_ref)

        any_spec = pl.BlockSpec(memory_space=pl.ANY)
        off = n_lat // tm
        return pl.pallas_call(
            body, name=tag + "_ctx", grid=(n_ctx // tm,),
            in_specs=[rows(kv_c.shape[1]), rows(LANE), any_spec, any_spec],
            out_specs=[heads(MLA_DQ_PAD, off), heads(MLA_V, off)],
            out_shape=[jax.ShapeDtypeStruct(k_buf.shape, BF16), jax.ShapeDtypeStruct(v_buf.shape, BF16)],
            input_output_aliases={2: 0, 3: 1}, compiler_params=_params(),
        )(kv_c, kr_c, k_buf, v_buf)

    def take_kv(dk_ref, dv_ref, dkv_ref):
        dkr = jnp.zeros((tm, LANE), F32)
        for h in range(hh):
            dkv_ref[:, 2 * LANE * h:2 * LANE * h + MLA_NOPE] = dk_ref[h, :, :MLA_NOPE].astype(F32)
            dkv_ref[:, 2 * LANE * h + MLA_NOPE:2 * LANE * (h + 1)] = dv_ref[h].astype(F32)
            dkr = dkr + dk_ref[h, :, MLA_NOPE:].astype(F32)
        return jnp.where(rope_lanes(), dkr, 0.0)

    def bwd_lat(dqt, dk, dv, qp_width, kv_width):
        def body(dqt_ref, dk_ref, dv_ref, cos_ref, s1_ref, s2_ref, dqp_ref, dkv_ref, dkr_ref):
            cos, s1, s2 = cos_ref[...], s1_ref[...], s2_ref[...]
            keep = rope_lanes()
            for j in range(hh // 2):
                even = jnp.transpose(dqt_ref[2 * j]) * scale
                odd = jnp.transpose(dqt_ref[2 * j + 1]) * scale
                dqp_ref[:, MLA_NOPE * 2 * j:MLA_NOPE * (2 * j + 1)] = even[:, :MLA_NOPE]
                dqp_ref[:, MLA_NOPE * (2 * j + 1):MLA_NOPE * (2 * j + 2)] = odd[:, :MLA_NOPE]
                g = jnp.where(keep, even[:, MLA_NOPE:], pltpu.roll(odd[:, MLA_NOPE:], MLA_ROPE, 1))
                dqp_ref[:, rope0 + LANE * j:rope0 + LANE * (j + 1)] = _rope128_t(g, cos, s1, s2)
            dkr_ref[...] = jnp.where(keep, _rope128_t(take_kv(dk_ref, dv_ref, dkv_ref), cos, s1, s2), 0.0)

        return pl.pallas_call(
            body, name=tag + "_dlat", grid=(n_lat // tm,),
            in_specs=[pl.BlockSpec((hh, MLA_DQ_PAD, tm), lambda i: (0, 0, i)),
                      heads(MLA_DQ_PAD, 0), heads(MLA_V, 0), rows(LANE), rows(LANE), rows(LANE)],
            out_specs=[rows(qp_width), rows(kv_width), rows(LANE)],
            out_shape=[jax.ShapeDtypeStruct((n_lat, qp_width), F32), jax.ShapeDtypeStruct((n_lat, kv_width), F32),
                       jax.ShapeDtypeStruct((n_lat, LANE), F32)],
            compiler_params=_params(),
        )(dqt, dk, dv, *tabs)

    def bwd_ctx(dk, dv, kv_width):
        def body(dk_ref, dv_ref, dkv_ref, dkr_ref):
            dkr_ref[...] = take_kv(dk_ref, dv_ref, dkv_ref)

        off = n_lat // tm
        return pl.pallas_call(
            body, name=tag + "_dctx", grid=(n_ctx // tm,),
            in_specs=[heads(MLA_DQ_PAD, off), heads(MLA_V, off)],
            out_specs=[rows(kv_width), rows(LANE)],
            out_shape=[jax.ShapeDtypeStruct((n_ctx, kv_width), F32), jax.ShapeDtypeStruct((n_ctx, LANE), F32)],
            compiler_params=_params(),
        )(dk, dv)

    def pack(qp, kv, kr, kv_c, kr_c):
        q, k, v = fwd_lat(qp, kv, kr)
        k, v = fwd_ctx(kv_c, kr_c, k, v)
        return q, k, v

    def unpack(dqt, dk, dv):
        qp_width, kv_width = hh * (MLA_NOPE + MLA_ROPE), hh * (MLA_NOPE + MLA_V)
        dqp, dkv, dkr = bwd_lat(dqt, dk, dv, qp_width, kv_width)
        dkv_c, dkr_c = bwd_ctx(dk, dv, kv_width)
        return dqp, dkv, dkr, dkv_c, dkr_c

    return pack, unpack


def _make_mla(tag, n_lat, n_ctx):
    scale = (MLA_NOPE + MLA_ROPE) ** -0.5
    pack, unpack = _make_mla_pack(tag + "pack", n_lat, n_ctx, scale)
    attn_fwd, attn_delta, attn_bwd = _make_attention(tag, scale)

    @jax.custom_vjp
    def mla(qp, kv, kr, kv_c, kr_c):
        q, k, v = pack(qp, kv, kr, kv_c, kr_c)
        return attn_fwd(q, k, jnp.swapaxes(v, 1, 2))[0]

    def fwd(qp, kv, kr, kv_c, kr_c):
        q, k, v = pack(qp, kv, kr, kv_c, kr_c)
        o, lse = attn_fwd(q, k, jnp.swapaxes(v, 1, 2))
        return o, (q, k, v, o, lse)

    def bwd(res, do):
        q, k, v, o, lse = res
        delta = attn_delta(o, do, q.shape[0])
        dqt, dk, dv = attn_bwd(q, k, jnp.swapaxes(k, 1, 2), v, do, lse, delta)
        return unpack(dqt, dk, dv)

    mla.defvjp(fwd, bwd)
    return mla


def _make_attention(tag, scale):
    neg_big = -1e30
    log2e = 1.4426950408889634
    sub = 256

    def fwd_call(q, k, vt):
        hh, n, dq = q.shape
        dv, ll = vt.shape[1], vt.shape[2]
        tq, tk = _pick(n, 1024), _pick(ll, 1408)
        sb = sub if tk % sub == 0 else tk
        c2 = scale * log2e
        k_steps = ll // tk

        def body(q_ref, k_ref, vt_ref, o_ref, lse_ref, m_scr, l_scr, acc_scr, s_scr, p_scr):
            j = pl.program_id(2)

            @pl.when(j == 0)
            def _():
                m_scr[...] = jnp.full_like(m_scr, neg_big)
                l_scr[...] = jnp.zeros_like(l_scr)
                acc_scr[...] = jnp.zeros_like(acc_scr)

            q_t = q_ref[...]
            m_prev = m_scr[...]
            m_new = m_prev
            for kk in range(tk // sb):
                rows = slice(kk * sb, (kk + 1) * sb)
                s_t = _dot(k_ref[rows, :], q_t, NT)
                s_scr[rows, :] = s_t
                m_new = jnp.maximum(m_new, jnp.max(s_t, axis=0, keepdims=True))
            mc = m_new * c2
            l_part = jnp.zeros_like(m_new)
            for kk in range(tk // sb):
                rows = slice(kk * sb, (kk + 1) * sb)
                p_t = jnp.exp2(s_scr[rows, :] * c2 - mc)
                l_part = l_part + jnp.sum(p_t, axis=0, keepdims=True)
                p_scr[rows, :] = p_t.astype(BF16)
            alpha = jnp.exp2((m_prev - m_new) * c2)
            l_scr[...] = alpha * l_scr[...] + l_part
            acc_scr[...] = alpha * acc_scr[...] + _dot(vt_ref[...], p_scr[...], NN)
            m_scr[...] = m_new

            @pl.when(j == k_steps - 1)
            def _():
                o_ref[...] = jnp.transpose(acc_scr[...] / l_scr[...]).astype(BF16)
                lse_ref[...] = m_scr[...] * scale + jnp.log(l_scr[...])

        return pl.pallas_call(
            body, name=tag + "_fwd", grid=(hh, n // tq, k_steps),
            in_specs=[pl.BlockSpec((None, tq, dq), lambda h, i, j: (h, i, 0)),
                      pl.BlockSpec((None, tk, dq), lambda h, i, j: (h, j, 0)),
                      pl.BlockSpec((None, dv, tk), lambda h, i, j: (h, 0, j))],
            out_specs=[pl.BlockSpec((tq, dv), lambda h, i, j: (i, h)),
                       pl.BlockSpec((None, 1, tq), lambda h, i, j: (h, 0, i))],
            out_shape=[jax.ShapeDtypeStruct((n, hh * dv), BF16), jax.ShapeDtypeStruct((hh, 1, n), F32)],
            scratch_shapes=[pltpu.VMEM((1, tq), F32), pltpu.VMEM((1, tq), F32), pltpu.VMEM((dv, tq), F32),
                            pltpu.VMEM((tk, tq), F32), pltpu.VMEM((tk, tq), BF16)],
            compiler_params=_params(),
        )(q, k, vt)

    def delta_call(o, do, hh):
        n = o.shape[0]
        dv = o.shape[1] // hh
        tq = _pick(n, 1024)

        def body(o_ref, do_ref, d_ref):
            prod_t = jnp.transpose(o_ref[...].astype(F32) * do_ref[...].astype(F32))
            d_ref[...] = jnp.sum(prod_t, axis=0, keepdims=True)

        spec = pl.BlockSpec((tq, dv), lambda h, i: (i, h))
        return pl.pallas_call(
            body, name=tag + "_delta", grid=(hh, n // tq), in_specs=[spec, spec],
            out_specs=pl.BlockSpec((None, 1, tq), lambda h, i: (h, 0, i)),
            out_shape=jax.ShapeDtypeStruct((hh, 1, n), F32), compiler_params=_params(),
        )(o, do)

    def bwd_call(q, k, kt, v, do, lse, delta):
        hh, n, dq = q.shape
        ll, dv = k.shape[1], v.shape[2]
        tq, tk = _pick(n, 1024), _pick(ll, 1408)
        sb = tk
        c2 = scale * log2e
        q_steps = n // tq

        def body(q_ref, k_ref, kt_ref, v_ref, do_ref, lse_ref, d_ref, dqt_ref, dk_ref, dv_ref, dk_scr, dv_scr):
            j = pl.program_id(1)
            i = pl.program_id(2)

            @pl.when(i == 0)
            def _():
                dk_scr[...] = jnp.zeros_like(dk_scr)
                dv_scr[...] = jnp.zeros_like(dv_scr)

            q_t, do_t = q_ref[...], do_ref[...]
            lse2 = lse_ref[...] * log2e
            delta_t = d_ref[...]
            dq_part = None
            for kk in range(tk // sb):
                rows = slice(kk * sb, (kk + 1) * sb)
                s_t = _dot(k_ref[rows, :], q_t, NT)
                p_t = jnp.exp2(s_t * c2 - lse2)
                ds_t = p_t * (_dot(v_ref[rows, :], do_t, NT) - delta_t)
                dv_scr[rows, :] += _dot(p_t, do_t, NN)
                dk_scr[rows, :] += _dot(ds_t, q_t, NN)
                part = _dot(kt_ref[:, rows], ds_t, NN)
                dq_part = part if dq_part is None else dq_part + part
            cols = pl.ds(pl.multiple_of(i * tq, tq), tq)

            @pl.when(j == 0)
            def _():
                dqt_ref[:, cols] = dq_part

            @pl.when(j > 0)
            def _():
                dqt_ref[:, cols] += dq_part

            @pl.when(i == q_steps - 1)
            def _():
                dk_ref[...] = (dk_scr[...] * scale).astype(BF16)
                dv_ref[...] = dv_scr[...].astype(BF16)

        return pl.pallas_call(
            body, name=tag + "_bwd", grid=(hh, ll // tk, q_steps),
            in_specs=[pl.BlockSpec((None, tq, dq), lambda h, j, i: (h, i, 0)),
                      pl.BlockSpec((None, tk, dq), lambda h, j, i: (h, j, 0)),
                      pl.BlockSpec((None, dq, tk), lambda h, j, i: (h, 0, j)),
                      pl.BlockSpec((None, tk, dv), lambda h, j, i: (h, j, 0)),
                      pl.BlockSpec((tq, dv), lambda h, j, i: (i, h)),
                      pl.BlockSpec((None, 1, tq), lambda h, j, i: (h, 0, i)),
                      pl.BlockSpec((None, 1, tq), lambda h, j, i: (h, 0, i))],
            out_specs=[pl.BlockSpec((None, dq, n), lambda h, j, i: (h, 0, 0)),
                       pl.BlockSpec((None, tk, dq), lambda h, j, i: (h, j, 0)),
                       pl.BlockSpec((None, tk, dv), lambda h, j, i: (h, j, 0))],
            out_shape=[jax.ShapeDtypeStruct((hh, dq, n), F32), jax.ShapeDtypeStruct((hh, ll, dq), BF16),
                       jax.ShapeDtypeStruct((hh, ll, dv), BF16)],
            scratch_shapes=[pltpu.VMEM((tk, dq), F32), pltpu.VMEM((tk, dv), F32)],
            compiler_params=_params(),
        )(q, k, kt, v, do, lse, delta)

    return fwd_call, delta_call, bwd_call


def _loss_tile(x, g, tgt):
    r = lax.rsqrt(jnp.mean(x * x, axis=-1, keepdims=True) + RMS_EPS)
    err = x * r * g - tgt
    per_tok = jnp.mean(err * err, axis=-1, keepdims=True)
    return 0.5 * jnp.sum(per_tok, axis=0, keepdims=True)


def _make_final_loss(tag):
    def fwd_call(x, g, tgt):
        t, d = x.shape
        tm = _pick(t, 512, 16)

        def body(x_ref, g_ref, t_ref, l_ref):
            l_ref[...] = jnp.broadcast_to(_loss_tile(x_ref[...], g_ref[...], t_ref[...]), (1, LANE))

        parts = pl.pallas_call(
            body, name=tag + "_fwd", grid=(t // tm,),
            in_specs=[_row_spec(tm, d), _vec_spec(d), _row_spec(tm, d)],
            out_specs=pl.BlockSpec((None, 1, LANE), lambda i: (i, 0, 0)),
            out_shape=jax.ShapeDtypeStruct((t // tm, 1, LANE), F32), compiler_params=_params(),
        )(x, g, tgt)
        return jnp.sum(parts[:, 0, 0])

    def bwd_call(x, g, tgt, dl):
        t, d = x.shape
        tm = _pick(t, 256, 16)

        def body(x_ref, g_ref, t_ref, dl_ref, dx_ref, dg_ref):
            _, vjp = jax.vjp(_loss_tile, x_ref[...], g_ref[...], t_ref[...])
            dx, dg, _ = vjp(dl_ref[...])
            dx_ref[...] = dx

            @pl.when(pl.program_id(0) == 0)
            def _():
                dg_ref[...] = jnp.zeros_like(dg_ref)

            dg_ref[...] += dg

        return pl.pallas_call(
            body, name=tag + "_bwd", grid=(t // tm,),
            in_specs=[_row_spec(tm, d), _vec_spec(d), _row_spec(tm, d), pl.BlockSpec((1, 1), lambda i: (0, 0))],
            out_specs=[_row_spec(tm, d), _vec_spec(d)],
            out_shape=[jax.ShapeDtypeStruct((t, d), F32), jax.ShapeDtypeStruct((1, d), F32)],
            compiler_params=_params(),
        )(x, g, tgt, dl)

    @jax.custom_vjp
    def final_loss(x, g, tgt):
        return fwd_call(x, g, tgt)

    def fwd(x, g, tgt):
        return fwd_call(x, g, tgt), (x, g, tgt)

    def bwd(res, dl):
        x, g, tgt = res
        dx, dg = bwd_call(x, g, tgt, dl.reshape(1, 1).astype(F32))
        return dx, dg, jnp.zeros_like(tgt)

    final_loss.defvjp(fwd, bwd)
    return final_loss


def _exchange(arrays, gather, name):
    n = len(arrays)

    def body(*refs):
        ins, outs = refs[:n], refs[n:2 * n]
        send_sems, recv_sems, local_sems = refs[2 * n:]
        me = 4 * lax.axis_index("x") + 2 * lax.axis_index("y") + lax.axis_index("c")

        def remote(a, d, wait_side=False):
            peer = (me + d) % N_DEV
            origin = (me + N_DEV - d) % N_DEV
            src = ins[a] if gather else ins[a].at[peer]
            dst = outs[a].at[origin if wait_side else me]
            return pltpu.make_async_remote_copy(
                src_ref=src, dst_ref=dst, send_sem=send_sems.at[a, d - 1], recv_sem=recv_sems.at[a, d - 1],
                device_id=(peer // 4, (peer // 2) % 2, peer % 2), device_id_type=pl.DeviceIdType.MESH)

        def local(a):
            src = ins[a] if gather else ins[a].at[me]
            return pltpu.make_async_copy(src, outs[a].at[me], local_sems.at[a])

        for a in range(n):
            for d in range(1, N_DEV):
                remote(a, d).start()
            local(a).start()
        for a in range(n):
            local(a).wait()
            for d in range(1, N_DEV):
                remote(a, d, wait_side=True).wait_recv()
                remote(a, d).wait_send()

    out_shape = []
    for arr in arrays:
        shape = (N_DEV,) + arr.shape if gather else arr.shape
        out_shape.append(jax.ShapeDtypeStruct(shape, arr.dtype))
    any_spec = pl.BlockSpec(memory_space=pl.ANY)
    return pl.pallas_call(
        body, name=name, in_specs=[any_spec] * n, out_specs=[any_spec] * n, out_shape=out_shape,
        scratch_shapes=[pltpu.SemaphoreType.DMA((n, N_DEV - 1)), pltpu.SemaphoreType.DMA((n, N_DEV - 1)),
                        pltpu.SemaphoreType.DMA((n,))],
        compiler_params=pltpu.CompilerParams(has_side_effects=True),
    )(*arrays)


def _split_copy(ins, lands, send_sems, recv_sems, a, d, gather, wait_side):
    me = 4 * lax.axis_index("x") + 2 * lax.axis_index("y") + lax.axis_index("c")
    peer = (me + d) % N_DEV
    origin = (me + N_DEV - d) % N_DEV
    return pltpu.make_async_remote_copy(
        src_ref=ins[a] if gather else ins[a].at[peer], dst_ref=lands[a].at[origin if wait_side else me],
        send_sem=send_sems.at[a * (N_DEV - 1) + d - 1], recv_sem=recv_sems.at[a * (N_DEV - 1) + d - 1],
        device_id=(peer // 4, (peer // 2) % 2, peer % 2), device_id_type=pl.DeviceIdType.MESH)


def _exchange_start(srcs, lands, after, gather, name):
    n = len(srcs)

    def body(*refs):
        ins, lnd = refs[:n], refs[n:2 * n]
        send_sems, recv_sems = refs[2 * n + 1], refs[2 * n + 2]
        for a in range(n):
            for d in range(1, N_DEV):
                _split_copy(ins, lnd, send_sems, recv_sems, a, d, gather, False).start()

    hbm = pl.BlockSpec(memory_space=pltpu.HBM)
    sem = pl.BlockSpec(memory_space=pltpu.SEMAPHORE)
    bufs = [pltpu.with_memory_space_constraint(t, pltpu.HBM) for t in list(srcs) + list(lands) + [after]]
    res = pl.pallas_call(
        body, name=name,
        in_specs=[hbm] * (2 * n + 1), out_specs=[sem, sem] + [hbm] * (2 * n + 1),
        out_shape=[pltpu.SemaphoreType.DMA((n * (N_DEV - 1),)), pltpu.SemaphoreType.DMA((n * (N_DEV - 1),))]
        + [pltpu.HBM(t.shape, t.dtype) for t in bufs],
        input_output_aliases={i: 2 + i for i in range(2 * n + 1)},
        compiler_params=pltpu.CompilerParams(has_side_effects=pltpu.SideEffectType.DATAFLOW_SIDE_EFFECTING),
    )(*bufs)
    return res[0], res[1], res[2:2 + n], res[2 + n:2 + 2 * n], res[-1]


def _exchange_wait(send_sems, recv_sems, srcs, lands, after, gather, name):
    n = len(srcs)

    def body(*refs):
        ins, lnd = refs[:n], refs[n:2 * n]
        send_sems_ref, recv_sems_ref = refs[2 * n], refs[2 * n + 1]
        for a in range(n):
            for d in range(1, N_DEV):
                _split_copy(ins, lnd, send_sems_ref, recv_sems_ref, a, d, gather, False).wait_send()
                _split_copy(ins, lnd, send_sems_ref, recv_sems_ref, a, d, gather, True).wait_recv()

    hbm = pl.BlockSpec(memory_space=pltpu.HBM)
    sem = pl.BlockSpec(memory_space=pltpu.SEMAPHORE)
    bufs = list(srcs) + list(lands)
    res = pl.pallas_call(
        body, name=name,
        in_specs=[hbm] * (2 * n) + [sem, sem, pl.BlockSpec(memory_space=pl.ANY)],
        out_specs=[hbm] * (2 * n),
        out_shape=[pltpu.HBM(t.shape, t.dtype) for t in bufs],
        input_output_aliases={i: i for i in range(2 * n)},
        compiler_params=pltpu.CompilerParams(has_side_effects=pltpu.SideEffectType.DATAFLOW_SIDE_EFFECTING),
    )(*bufs, send_sems, recv_sems, after)
    return res[n:]


def _own_slot(block, me):
    empty = lax.empty((N_DEV,) + block.shape, block.dtype)
    return lax.dynamic_update_slice(empty, block[None], (me,) + (0,) * block.ndim)


def _coords():
    return lax.axis_index("x"), lax.axis_index("y"), lax.axis_index("c")


def _other_chips(x, y):
    return [(1 - x, y), (x, 1 - y), (1 - x, 1 - y)]


def _gather_two_level(arrays, name):
    n = len(arrays)

    def body(*refs):
        ins, outs = refs[:n], refs[n:2 * n]
        send_sems, recv_sems, local_sems = refs[2 * n:]
        x, y, c = _coords()
        me, sib = (x, y, c), (x, y, 1 - c)
        chips = _other_chips(x, y)

        def copy(a, k, block, to, from_input=False):
            slot = 4 * block[0] + 2 * block[1] + block[2]
            return pltpu.make_async_remote_copy(
                src_ref=ins[a] if from_input else outs[a].at[slot], dst_ref=outs[a].at[slot],
                send_sem=send_sems.at[a, k], recv_sem=recv_sems.at[a, k],
                device_id=to, device_id_type=pl.DeviceIdType.MESH)

        def local(a):
            return pltpu.make_async_copy(ins[a], outs[a].at[4 * x + 2 * y + c], local_sems.at[a])

        for a in range(n):
            for j, chip in enumerate(chips):
                copy(a, 1 + j, me, (*chip, c), True).start()
            copy(a, 0, me, sib, True).start()
            local(a).start()
        for a in range(n):
            for j, chip in enumerate(chips):
                copy(a, 1 + j, (*chip, c), me).wait_recv()
                copy(a, 4 + j, (*chip, c), sib).start()
        for a in range(n):
            copy(a, 0, sib, me).wait_recv()
            for j, chip in enumerate(chips):
                copy(a, 4 + j, (*chip, 1 - c), me).wait_recv()
            for k in range(N_DEV - 1):
                copy(a, k, me, sib, True).wait_send()
            local(a).wait()

    any_spec = pl.BlockSpec(memory_space=pl.ANY)
    return pl.pallas_call(
        body, name=name, in_specs=[any_spec] * n, out_specs=[any_spec] * n,
        out_shape=[jax.ShapeDtypeStruct((N_DEV,) + arr.shape, arr.dtype) for arr in arrays],
        scratch_shapes=[pltpu.SemaphoreType.DMA((n, N_DEV - 1)), pltpu.SemaphoreType.DMA((n, N_DEV - 1)),
                        pltpu.SemaphoreType.DMA((n,))],
        compiler_params=pltpu.CompilerParams(has_side_effects=True),
    )(*arrays)


def _swap_sibling(arrays, name):
    n = len(arrays)
    n_chip = N_DEV // 2

    def body(*refs):
        ins, outs = refs[:n], refs[n:2 * n]
        send_sems, recv_sems = refs[2 * n:]
        x, y, c = _coords()

        def copy(a, q):
            return pltpu.make_async_remote_copy(
                src_ref=ins[a].at[2 * q + (1 - c)], dst_ref=outs[a].at[q],
                send_sem=send_sems.at[a, q], recv_sem=recv_sems.at[a, q],
                device_id=(x, y, 1 - c), device_id_type=pl.DeviceIdType.MESH)

        for a in range(n):
            for q in range(n_chip):
                copy(a, q).start()
        for a in range(n):
            for q in range(n_chip):
                copy(a, q).wait_recv()
                copy(a, q).wait_send()

    any_spec = pl.BlockSpec(memory_space=pl.ANY)
    return pl.pallas_call(
        body, name=name, in_specs=[any_spec] * n, out_specs=[any_spec] * n,
        out_shape=[jax.ShapeDtypeStruct((n_chip,) + arr.shape[1:], arr.dtype) for arr in arrays],
        scratch_shapes=[pltpu.SemaphoreType.DMA((n, n_chip)), pltpu.SemaphoreType.DMA((n, n_chip))],
        compiler_params=pltpu.CompilerParams(has_side_effects=True),
    )(*arrays)


def _scatter_chips(arrays, name):
    n = len(arrays)
    n_chip = N_DEV // 2

    def body(*refs):
        ins, outs = refs[:n], refs[n:2 * n]
        send_sems, recv_sems, local_sems = refs[2 * n:]
        x, y, c = _coords()
        q_me = 2 * x + y
        chips = _other_chips(x, y)

        def copy(a, j, wait_side=False):
            q_peer = 2 * chips[j][0] + chips[j][1]
            return pltpu.make_async_remote_copy(
                src_ref=ins[a].at[q_peer], dst_ref=outs[a].at[q_peer if wait_side else q_me],
                send_sem=send_sems.at[a, j], recv_sem=recv_sems.at[a, j],
                device_id=(*chips[j], c), device_id_type=pl.DeviceIdType.MESH)

        def local(a):
            return pltpu.make_async_copy(ins[a].at[q_me], outs[a].at[q_me], local_sems.at[a])

        for a in range(n):
            for j in range(n_chip - 1):
                copy(a, j).start()
            local(a).start()
        for a in range(n):
            local(a).wait()
            for j in range(n_chip - 1):
                copy(a, j, wait_side=True).wait_recv()
                copy(a, j).wait_send()

    any_spec = pl.BlockSpec(memory_space=pl.ANY)
    return pl.pallas_call(
        body, name=name, in_specs=[any_spec] * n, out_specs=[any_spec] * n,
        out_shape=[jax.ShapeDtypeStruct(arr.shape, arr.dtype) for arr in arrays],
        scratch_shapes=[pltpu.SemaphoreType.DMA((n, n_chip - 1)), pltpu.SemaphoreType.DMA((n, n_chip - 1)),
                        pltpu.SemaphoreType.DMA((n,))],
        compiler_params=pltpu.CompilerParams(has_side_effects=True),
    )(*arrays)


def _pair_add(full, theirs, core, name):
    n_chip, r, cn = theirs.shape
    tr = _pick(r, max(16, (2 * 1024 * 1024) // (4 * cn) // 16 * 16), 16)

    def body(core_ref, mine_ref, theirs_ref, o_ref):
        o_ref[...] = (mine_ref[...].astype(F32) + theirs_ref[...].astype(F32)).astype(BF16)

    tile = pl.BlockSpec((None, tr, cn), lambda q, i, core_ref: (q, i, 0))
    return pl.pallas_call(
        body, name=name,
        grid_spec=pltpu.PrefetchScalarGridSpec(
            num_scalar_prefetch=1, grid=(n_chip, r // tr),
            in_specs=[pl.BlockSpec((None, tr, cn), lambda q, i, core_ref: (2 * q + core_ref[0], i, 0)), tile],
            out_specs=tile),
        out_shape=jax.ShapeDtypeStruct(theirs.shape, BF16), compiler_params=_params(),
    )(core, full, theirs)


def _make_gather_op(tag):
    @jax.custom_vjp
    def gather_op(xl):
        return _exchange([xl], True, tag + "_gather")[0]

    def fwd(xl):
        return gather_op(xl), None

    def bwd(_, g):
        return (jnp.sum(_exchange([g], False, tag + "_scatter")[0], axis=0),)

    gather_op.defvjp(fwd, bwd)
    return gather_op


def _adamw(gstack, w, m, v, name):
    s, r, cn = gstack.shape
    tr = _pick(r, max(8, (2 * 1024 * 1024) // (4 * cn) // 8 * 8), 8)
    c1 = 1.0 - ADAM_B1 ** ADAM_STEP
    c2 = 1.0 - ADAM_B2 ** ADAM_STEP

    def body(g_ref, w_ref, m_ref, v_ref, go_ref, d_ref, mo_ref, vo_ref):
        g = g_ref[0].astype(F32)
        for q in range(1, s):
            g = g + g_ref[q].astype(F32)
        m_new = ADAM_B1 * m_ref[...] + (1.0 - ADAM_B1) * g
        v_new = ADAM_B2 * v_ref[...] + (1.0 - ADAM_B2) * (g * g)
        go_ref[...] = g
        mo_ref[...] = m_new
        vo_ref[...] = v_new
        d_ref[...] = -ADAM_LR * ((m_new / c1) / (jnp.sqrt(v_new / c2) + ADAM_EPS) + ADAM_WD * w_ref[...])

    tile = pl.BlockSpec((tr, cn), lambda i: (i, 0))
    out = jax.ShapeDtypeStruct((r, cn), F32)
    return pl.pallas_call(
        body, name=name, grid=(r // tr,),
        in_specs=[pl.BlockSpec((s, tr, cn), lambda i: (0, i, 0)), tile, tile, tile],
        out_specs=[tile, tile, tile, tile], out_shape=[out, out, out, out],
        compiler_params=_params(),
    )(gstack, w, m, v)


def _cols_from_stack(w):
    return jnp.swapaxes(w, 0, 1).reshape(w.shape[1], N_DEV * w.shape[2])


def _stage_a(p, ctx, silu_c_all, me):
    x = p["x"]
    d = x.shape[1]
    n_a = p["ada_w"].shape[1]

    a_in = jnp.concatenate([silu_c_all, jax.nn.silu(p["c_ctx"])[None, :], jnp.zeros((7, d), F32)], axis=0)
    b_loc = lax.dynamic_slice(p["ada_b"], (0, me * n_a), (1, n_a))
    r_loc = _make_small_mm("ada")(a_in, p["ada_w"]) + b_loc
    r_full = _make_gather_op("ada")(r_loc)
    m_lat = lax.dynamic_index_in_dim(r_full, me, axis=1, keepdims=False).reshape(N_MOD, 1, d)
    m_ctx = r_full[:, N_DEV, :].reshape(N_MOD, 1, d)

    x1 = _make_ffn_block("ffn1")(x, p["norm1_g"], m_lat[0], m_lat[1], m_lat[2], p["ffn1_w_in"], p["ffn1_w_out"])
    c1 = _make_ffn_block("ffn1c")(ctx, p["norm1_g"], m_ctx[0], m_ctx[1], m_ctx[2], p["ffn1_w_in"], p["ffn1_w_out"])
    return x1, c1, m_lat, m_ctx


def _stage_b(p, x1, c1, m_lat, m_ctx):
    n_lat, d = x1.shape
    n_ctx = c1.shape[0]
    w_mix = jnp.pad(_cols_from_stack(p["mix_w_in"]), ((0, 0), (0, MIX_IN_PAD - MIX_IN)))
    proj = _make_norm_proj("mix")(x1, p["norm2_g"], m_lat[3], m_lat[4], w_mix)
    proj_c = _make_norm_proj("mixc")(c1, p["norm2_g"], m_ctx[3], m_ctx[4], w_mix)
    widths = SPLITS[:6] + (LANE,)
    rq, rk, rv, rg, cq, ckv, kr = _make_split("mixsplit", widths, MIX_IN_PAD)(proj)
    _, crk, crv, _, _, cckv, ckr = _make_split("mixsplitc", widths, MIX_IN_PAD)(proj_c)

    zq = jnp.zeros((1, MLA_Q_RANK), F32)
    zkv = jnp.zeros((1, MLA_KV_RANK), F32)
    w_uq3 = _cols_from_stack(p["mla_w_uq"]).reshape(MLA_Q_RANK, MLA_HEADS, MLA_NOPE + MLA_ROPE)
    w_uq = jnp.concatenate([w_uq3[:, :, :MLA_NOPE].reshape(MLA_Q_RANK, -1),
                            w_uq3[:, :, MLA_NOPE:].reshape(MLA_Q_RANK, -1)], axis=1)
    w_ukv = _cols_from_stack(p["mla_w_ukv"])
    q = _make_norm_proj("uq")(cq, p["mla_q_norm_g"], zq, zq, w_uq)
    kv = _make_norm_proj("ukv")(ckv, p["mla_kv_norm_g"], zkv, zkv, w_ukv)
    kv_c = _make_norm_proj("ukvc")(cckv, p["mla_kv_norm_g"], zkv, zkv, w_ukv)

    lg_f = jax.nn.log_sigmoid(p["ret_decay_fwd"][0])
    lg_b = jax.nn.log_sigmoid(p["ret_decay_bwd"][0])
    rq_h, rk_h, rv_h, crk_h, crv_h = _make_ret_pack("retpack", n_lat, n_ctx)(rq, rk, rv, crk, crv)
    assert n_ctx == RET_CHUNK, "the context prefix is one retention chunk"

    def lanes(lg):
        return jnp.broadcast_to(lg[:, None, None], (RET_HEADS, 1, LANE))

    s0_f = _make_ctx_state("retcf", False)(crk_h, crv_h, lanes(lg_f))
    s0_b = _make_ctx_state("retcb", True)(crk_h, crv_h, lanes(lg_b))
    y_f = _make_ret_dir("retf", False)(rq_h, rk_h, rv_h, lanes(lg_f), s0_f)
    y_b = _make_ret_dir("retb", True)(rq_h, rk_h, rv_h, lanes(lg_b), s0_b)
    ret_o = _make_ret_out("reto")(y_f, y_b, rg)

    mla_o = _make_mla("mla", n_lat, n_ctx)(q, kv, kr, kv_c, ckr)

    w_mo = p["mix_w_out"].reshape(-1, d)
    return _make_res_proj("mixo")(jnp.concatenate([ret_o, mla_o], axis=-1), w_mo, x1, m_lat[5])


def _stage_c(p, x2, m_lat, tgt):
    x3 = _make_ffn_block("ffn2")(x2, p["norm3_g"], m_lat[6], m_lat[7], m_lat[8], p["ffn2_w_in"], p["ffn2_w_out"])
    return _make_final_loss("loss")(x3, p["final_norm_g"], tgt)


FIRST = ("ffn1_w_in", "ffn1_w_out")
MID = ("mix_w_in", "mla_w_uq", "mla_w_ukv", "mix_w_out")
LAST = ("ffn2_w_in", "ffn2_w_out")
BIG = FIRST + MID + LAST
SMALL = ("c_ctx", "ada_b", "norm1_g", "norm2_g", "ret_decay_fwd", "ret_decay_bwd", "mla_q_norm_g",
         "mla_kv_norm_g", "norm3_g", "final_norm_g")
WEIGHTS = ("c_ctx", "ada_w", "ada_b", "norm1_g", "ffn1_w_in", "ffn1_w_out", "norm2_g", "mix_w_in", "ret_decay_fwd",
           "ret_decay_bwd", "mla_q_norm_g", "mla_w_uq", "mla_kv_norm_g", "mla_w_ukv", "mix_w_out", "norm3_g",
           "ffn2_w_in", "ffn2_w_out", "final_norm_g")


def _pack(parts):
    flat = jnp.concatenate([t.reshape(-1) for t in parts])
    pad = (-flat.shape[0]) % LANE
    return jnp.pad(flat, (0, pad)).reshape(1, -1)


def _unpack(flat, like):
    out, off = [], 0
    for t in like:
        out.append(flat[0, off:off + t.size].reshape(t.shape))
        off += t.size
    return out


def kernel(x, c, ctx, c_ctx, ada_w, ada_b, norm1_g, ffn1_w_in, ffn1_w_out, norm2_g, mix_w_in, ret_decay_fwd, ret_decay_bwd, mla_q_norm_g, mla_w_uq, mla_kv_norm_g, mla_w_ukv, mix_w_out, norm3_g, ffn2_w_in, ffn2_w_out, final_norm_g, loss_target, m_c_ctx, m_ada_w, m_ada_b, m_norm1_g, m_ffn1_w_in, m_ffn1_w_out, m_norm2_g, m_mix_w_in, m_ret_decay_fwd, m_ret_decay_bwd, m_mla_q_norm_g, m_mla_w_uq, m_mla_kv_norm_g, m_mla_w_ukv, m_mix_w_out, m_norm3_g, m_ffn2_w_in, m_ffn2_w_out, m_final_norm_g, v_c_ctx, v_ada_w, v_ada_b, v_norm1_g, v_ffn1_w_in, v_ffn1_w_out, v_norm2_g, v_mix_w_in, v_ret_decay_fwd, v_ret_decay_bwd, v_mla_q_norm_g, v_mla_w_uq, v_mla_kv_norm_g, v_mla_w_ukv, v_mix_w_out, v_norm3_g, v_ffn2_w_in, v_ffn2_w_out, v_final_norm_g):
    w = dict(c_ctx=c_ctx, ada_w=ada_w, ada_b=ada_b, norm1_g=norm1_g, ffn1_w_in=ffn1_w_in, ffn1_w_out=ffn1_w_out,
             norm2_g=norm2_g, mix_w_in=mix_w_in, ret_decay_fwd=ret_decay_fwd, ret_decay_bwd=ret_decay_bwd,
             mla_q_norm_g=mla_q_norm_g, mla_w_uq=mla_w_uq, mla_kv_norm_g=mla_kv_norm_g, mla_w_ukv=mla_w_ukv,
             mix_w_out=mix_w_out, norm3_g=norm3_g, ffn2_w_in=ffn2_w_in, ffn2_w_out=ffn2_w_out,
             final_norm_g=final_norm_g)
    mom = dict(c_ctx=m_c_ctx, ada_w=m_ada_w, ada_b=m_ada_b, norm1_g=m_norm1_g, ffn1_w_in=m_ffn1_w_in,
               ffn1_w_out=m_ffn1_w_out, norm2_g=m_norm2_g, mix_w_in=m_mix_w_in, ret_decay_fwd=m_ret_decay_fwd,
               ret_decay_bwd=m_ret_decay_bwd, mla_q_norm_g=m_mla_q_norm_g, mla_w_uq=m_mla_w_uq,
               mla_kv_norm_g=m_mla_kv_norm_g, mla_w_ukv=m_mla_w_ukv, mix_w_out=m_mix_w_out, norm3_g=m_norm3_g,
               ffn2_w_in=m_ffn2_w_in, ffn2_w_out=m_ffn2_w_out, final_norm_g=m_final_norm_g)
    var = dict(c_ctx=v_c_ctx, ada_w=v_ada_w, ada_b=v_ada_b, norm1_g=v_norm1_g, ffn1_w_in=v_ffn1_w_in,
               ffn1_w_out=v_ffn1_w_out, norm2_g=v_norm2_g, mix_w_in=v_mix_w_in, ret_decay_fwd=v_ret_decay_fwd,
               ret_decay_bwd=v_ret_decay_bwd, mla_q_norm_g=v_mla_q_norm_g, mla_w_uq=v_mla_w_uq,
               mla_kv_norm_g=v_mla_kv_norm_g, mla_w_ukv=v_mla_w_ukv, mix_w_out=v_mix_w_out, norm3_g=v_norm3_g,
               ffn2_w_in=v_ffn2_w_in, ffn2_w_out=v_ffn2_w_out, final_norm_g=v_final_norm_g)
    me = 4 * lax.axis_index("x") + 2 * lax.axis_index("y") + lax.axis_index("c")

    shard = {k: w[k][0].astype(BF16) for k in BIG}
    first = _gather_two_level([shard[k] for k in FIRST] + [jax.nn.silu(c)], "weights_gather")
    silu_c_all = first[-1][:, 0, :]
    mid_start = _exchange_start([shard[k] for k in MID], [_own_slot(shard[k], me) for k in MID], first[0], True,
                                "mixer_weights_start")
    last_start = _exchange_start([shard[k] for k in LAST], [_own_slot(shard[k], me) for k in LAST], mid_start[4],
                                 True, "ffn2_weights_start")

    pa = dict(zip(FIRST, (last_start[4], first[1])), x=x[0], ada_w=ada_w[0], c_ctx=c_ctx, ada_b=ada_b,
              norm1_g=norm1_g)
    (x1, c1, m_lat, m_ctx), vjp_a = jax.vjp(lambda q: _stage_a(q, ctx[0], silu_c_all, me), pa)

    mid = _exchange_wait(mid_start[0], mid_start[1], mid_start[2], mid_start[3], x1, True, "mixer_weights_wait")
    pb = dict(zip(MID, mid))
    for k in ("norm2_g", "mla_q_norm_g", "mla_kv_norm_g", "ret_decay_fwd", "ret_decay_bwd"):
        pb[k] = w[k]
    x2, vjp_b = jax.vjp(_stage_b, pb, x1, c1, m_lat, m_ctx)

    last = _exchange_wait(last_start[0], last_start[1], last_start[2], last_start[3], x2, True, "ffn2_weights_wait")
    pc = dict(zip(LAST, last), norm3_g=norm3_g, final_norm_g=final_norm_g[None, :])
    loss_local, vjp_c = jax.vjp(lambda q, t, m: _stage_c(q, t, m, loss_target[0]), pc, x2, m_lat)

    gc, dx2, dm_c = vjp_c(jnp.ones((), F32))
    last_scat = _exchange_start([gc[k] for k in LAST],
                                [_own_slot(lax.dynamic_index_in_dim(gc[k], me, 0, False), me) for k in LAST],
                                dx2, False, "ffn2_grads_start")
    gb, dx1, dc1, dm_b, dmc_b = vjp_b(last_scat[4])
    mid_scat = _exchange_start([gb[k] for k in MID],
                               [_own_slot(lax.dynamic_index_in_dim(gb[k], me, 0, False), me) for k in MID],
                               dx1, False, "mixer_grads_start")
    (ga,) = vjp_a((mid_scat[4], dc1, dm_b + dm_c, dmc_b))
    grads = {**ga, **gb, **gc}
    grads["final_norm_g"] = grads["final_norm_g"][0]

    core = lax.axis_index("c").astype(jnp.int32).reshape(1)
    full = [grads[k] for k in FIRST]
    theirs = _swap_sibling(full, "grads_swap")
    paired = [_pair_add(f, t, core, "grads_pair_" + k) for k, f, t in zip(FIRST, full, theirs)]
    exchanged = dict(zip(FIRST, _scatter_chips(paired, "grads_scatter")))
    exchanged.update(zip(LAST, _exchange_wait(last_scat[0], last_scat[1], last_scat[2], last_scat[3], grads["x"],
                                              False, "ffn2_grads_wait")))
    exchanged.update(zip(MID, _exchange_wait(mid_scat[0], mid_scat[1], mid_scat[2], mid_scat[3], grads["x"],
                                             False, "mixer_grads_wait")))
    zero1 = [jnp.zeros((1,), F32)]
    small_like = zero1 + [w[k] for k in SMALL]
    small_all = _exchange([_pack([loss_local.reshape(1)] + [grads[k] for k in SMALL])], True, "small_grads_gather")[0]
    loss = jnp.sum(small_all[:, 0, 0])

    out_g, out_d, out_m, out_v = {}, {}, {}, {}

    def update(name, gstack, shape2d):
        res = _adamw(gstack, w[name].reshape(shape2d), mom[name].reshape(shape2d), var[name].reshape(shape2d),
                     "adamw_" + name)
        out_g[name], out_d[name], out_m[name], out_v[name] = [t.reshape(w[name].shape) for t in res]

    for k in BIG:
        update(k, exchanged[k], exchanged[k].shape[1:])
    update("ada_w", grads["ada_w"][None], ada_w.shape[1:])
    res = _adamw(small_all, _pack(small_like), _pack(zero1 + [mom[k] for k in SMALL]),
                 _pack(zero1 + [var[k] for k in SMALL]), "adamw_small")
    for dst, flat in zip((out_g, out_d, out_m, out_v), res):
        for k, t in zip(SMALL, _unpack(flat, small_like)[1:]):
            dst[k] = t

    return (loss, grads["x"][None], *[out_g[k] for k in WEIGHTS], *[out_d[k] for k in WEIGHTS],
            *[out_m[k] for k in WEIGHTS], *[out_v[k] for k in WEIGHTS])
```

```python
import functools

import jax
import jax.numpy as jnp
from jax import lax
from jax.experimental import pallas as pl
from jax.experimental.pallas import tpu as pltpu

F32 = jnp.float32
BF16 = jnp.bfloat16

N_DEV = 8
MESH_AXES = ("x", "y", "c")

GRID_W = 64
N_MOD = 9
RET_HEADS = 8
RET_DK = 64
RET_DV = 128
RET_CHUNK = 256
RET_ROPE_BASE = 10000.0
MLA_HEADS = 8
MLA_Q_RANK = 512
MLA_KV_RANK = 256
MLA_NOPE = 128
MLA_ROPE = 64
MLA_V = 128
AXIAL_BASE = 10000.0
RMS_EPS = 1e-6
GN_EPS = 1e-5
SPLITS = (RET_HEADS * RET_DK, RET_HEADS * RET_DK, RET_HEADS * RET_DV, RET_HEADS * RET_DV,
          MLA_Q_RANK, MLA_KV_RANK, MLA_ROPE)
MIX_IN = sum(SPLITS)
MIX_IN_PAD = 4096

ADAM_LR = 0.001
ADAM_B1 = 0.9
ADAM_B2 = 0.999
ADAM_EPS = 1e-08
ADAM_WD = 0.01
ADAM_STEP = 10

LANE = 128
RET_DKP = LANE
VMEM_LIMIT_BYTES = 56 * 1024 * 1024

NN = ((1,), (0,))
NT = ((1,), (1,))
TN = ((0,), (0,))


def _pick(dim, target, align=LANE):
    t = min(dim, target)
    t -= t % align
    while t >= align:
        if dim % t == 0:
            return t
        t -= align
    return dim


def _params():
    return pltpu.CompilerParams(vmem_limit_bytes=VMEM_LIMIT_BYTES)


def _dot(a, b, dims):
    return lax.dot_general(a.astype(BF16), b.astype(BF16), (dims, ((), ())), preferred_element_type=F32)


def _mm_call(name, grid, ins, pairs, outs, acc_shapes, epilogue):
    n_in, n_out = len(ins), len(outs)
    k_axis = len(grid) - 1
    k_steps = grid[k_axis]

    def body(*refs):
        in_refs = refs[:n_in]
        out_refs = refs[n_in:n_in + n_out]
        accs = refs[n_in + n_out:]
        k = pl.program_id(k_axis)

        @pl.when(k == 0)
        def _():
            for acc in accs:
                acc[...] = jnp.zeros_like(acc)

        for ai, bi, dims, ci in pairs:
            accs[ci][...] += _dot(in_refs[ai][...], in_refs[bi][...], dims)

        @pl.when(k == k_steps - 1)
        def _():
            epilogue([acc[...] for acc in accs], in_refs, out_refs)

    res = pl.pallas_call(
        body, name=name, grid=grid,
        in_specs=[s for _, s in ins], out_specs=[s for _, s in outs],
        out_shape=[s for s, _ in outs],
        scratch_shapes=[pltpu.VMEM(s, F32) for s in acc_shapes],
        compiler_params=_params(),
    )(*[a for a, _ in ins])
    return res


def _matmul(a, b, mode, out_dtype, name, tm=1024, tn=1024, tk=512):
    if mode == "nn":
        (m, kd), n = a.shape, b.shape[1]
    elif mode == "nt":
        (m, kd), n = a.shape, b.shape[0]
    else:
        (kd, m), n = a.shape, b.shape[1]
    tm, tn = _pick(m, tm, 16), _pick(n, tn)
    tk = _pick(kd, tk) if mode != "tn" else _pick(kd, tk, 16)
    if mode == "nn":
        a_spec = pl.BlockSpec((tm, tk), lambda i, j, k: (i, k))
        b_spec = pl.BlockSpec((tk, tn), lambda i, j, k: (k, j))
        dims = NN
    elif mode == "nt":
        a_spec = pl.BlockSpec((tm, tk), lambda i, j, k: (i, k))
        b_spec = pl.BlockSpec((tn, tk), lambda i, j, k: (j, k))
        dims = NT
    else:
        a_spec = pl.BlockSpec((tk, tm), lambda i, j, k: (k, i))
        b_spec = pl.BlockSpec((tk, tn), lambda i, j, k: (k, j))
        dims = TN

    def epilogue(accs, in_refs, out_refs):
        out_refs[0][...] = accs[0].astype(out_dtype)

    return _mm_call(
        name, (m // tm, n // tn, kd // tk), [(a, a_spec), (b, b_spec)], [(0, 1, dims, 0)],
        [(jax.ShapeDtypeStruct((m, n), out_dtype), pl.BlockSpec((tm, tn), lambda i, j, k: (i, j)))],
        [(tm, tn)], epilogue)[0]


def _norm_mod_tile(x, ng, sc, sh):
    r = lax.rsqrt(jnp.mean(x * x, axis=-1, keepdims=True) + RMS_EPS)
    return (x * r * ng) * (1.0 + sc) + sh


def _row_spec(tm, d):
    return pl.BlockSpec((tm, d), lambda i: (i, 0))


def _vec_spec(d):
    return pl.BlockSpec((1, d), lambda i: (0, 0))


def _norm_mod_fwd(x, ng, sc, sh, name):
    t, d = x.shape
    tm = _pick(t, 512, 16)

    def body(x_ref, ng_ref, sc_ref, sh_ref, h_ref):
        h_ref[...] = _norm_mod_tile(x_ref[...], ng_ref[...], sc_ref[...], sh_ref[...]).astype(BF16)

    return pl.pallas_call(
        body, name=name, grid=(t // tm,),
        in_specs=[_row_spec(tm, d), _vec_spec(d), _vec_spec(d), _vec_spec(d)],
        out_specs=_row_spec(tm, d), out_shape=jax.ShapeDtypeStruct((t, d), BF16),
        compiler_params=_params(),
    )(x, ng, sc, sh)


def _norm_mod_bwd(x, ng, sc, sh, dh, dres, name):
    t, d = x.shape
    tm = _pick(t, 256, 16)
    has_res = dres is not None

    def body(*refs):
        if has_res:
            x_ref, ng_ref, sc_ref, sh_ref, dh_ref, dres_ref, dx_ref, dng_ref, dsc_ref, dsh_ref = refs
        else:
            x_ref, ng_ref, sc_ref, sh_ref, dh_ref, dx_ref, dng_ref, dsc_ref, dsh_ref = refs
        _, vjp = jax.vjp(_norm_mod_tile, x_ref[...], ng_ref[...], sc_ref[...], sh_ref[...])
        dx, dng, dsc, dsh = vjp(dh_ref[...].astype(F32))
        if has_res:
            dx = dx + dres_ref[...]
        dx_ref[...] = dx

        @pl.when(pl.program_id(0) == 0)
        def _():
            dng_ref[...] = jnp.zeros_like(dng_ref)
            dsc_ref[...] = jnp.zeros_like(dsc_ref)
            dsh_ref[...] = jnp.zeros_like(dsh_ref)

        dng_ref[...] += dng
        dsc_ref[...] += dsc
        dsh_ref[...] += dsh

    ins = [x, ng, sc, sh, dh] + ([dres] if has_res else [])
    in_specs = [_row_spec(tm, d), _vec_spec(d), _vec_spec(d), _vec_spec(d), _row_spec(tm, d)]
    in_specs += [_row_spec(tm, d)] if has_res else []
    vec = jax.ShapeDtypeStruct((1, d), F32)
    return pl.pallas_call(
        body, name=name, grid=(t // tm,), in_specs=in_specs,
        out_specs=[_row_spec(tm, d), _vec_spec(d), _vec_spec(d), _vec_spec(d)],
        out_shape=[jax.ShapeDtypeStruct((t, d), F32), vec, vec, vec],
        compiler_params=_params(),
    )(*ins)


def _res_mm_fwd(a, w, x, gate, coef, name):
    t, kd = a.shape
    d = w.shape[1]
    tm, tn, tk = _pick(t, 1024, 16), _pick(d, 1024), _pick(kd, 2816)

    def epilogue(accs, in_refs, out_refs):
        f = accs[0]
        out_refs[0][...] = in_refs[2][...] + (coef * in_refs[3][...]) * f
        out_refs[1][...] = f.astype(BF16)

    tile = pl.BlockSpec((tm, tn), lambda i, j, k: (i, j))
    return _mm_call(
        name, (t // tm, d // tn, kd // tk),
        [(a, pl.BlockSpec((tm, tk), lambda i, j, k: (i, k))), (w, pl.BlockSpec((tk, tn), lambda i, j, k: (k, j))),
         (x, tile), (gate, pl.BlockSpec((1, tn), lambda i, j, k: (0, j)))],
        [(0, 1, NN, 0)],
        [(jax.ShapeDtypeStruct((t, d), F32), tile), (jax.ShapeDtypeStruct((t, d), BF16), tile)],
        [(tm, tn)], epilogue)


def _gate_bwd(dxo, f, gate, coef, name):
    t, d = dxo.shape
    tm = _pick(t, 512, 16)

    def body(dxo_ref, f_ref, gate_ref, df_ref, dgate_ref):
        dxo_t = dxo_ref[...]
        df_ref[...] = ((coef * gate_ref[...]) * dxo_t).astype(BF16)

        @pl.when(pl.program_id(0) == 0)
        def _():
            dgate_ref[...] = jnp.zeros_like(dgate_ref)

        dgate_ref[...] += coef * jnp.sum(dxo_t * f_ref[...].astype(F32), axis=0, keepdims=True)

    return pl.pallas_call(
        body, name=name, grid=(t // tm,),
        in_specs=[_row_spec(tm, d), _row_spec(tm, d), _vec_spec(d)],
        out_specs=[_row_spec(tm, d), _vec_spec(d)],
        out_shape=[jax.ShapeDtypeStruct((t, d), BF16), jax.ShapeDtypeStruct((1, d), F32)],
        compiler_params=_params(),
    )(dxo, f, gate)


def _ffn_in_fwd(h, w_in, name):
    t, d = h.shape
    n = w_in.shape[2]
    half = N_DEV // 2
    f = half * n
    tm = _pick(t, 512, 16)

    def epilogue(accs, in_refs, out_refs):
        g, u = accs
        out_refs[0][...] = (g * jax.nn.sigmoid(g) * u).astype(BF16)
        out_refs[1][0] = g.astype(BF16)
        out_refs[1][1] = u.astype(BF16)

    return _mm_call(
        name, (half, t // tm, 1),
        [(h, pl.BlockSpec((tm, d), lambda j, i, k: (i, 0))),
         (w_in, pl.BlockSpec((None, d, n), lambda j, i, k: (j, 0, 0))),
         (w_in, pl.BlockSpec((None, d, n), lambda j, i, k: (j + half, 0, 0)))],
        [(0, 1, NN, 0), (0, 2, NN, 1)],
        [(jax.ShapeDtypeStruct((t, f), BF16), pl.BlockSpec((tm, n), lambda j, i, k: (i, j))),
         (jax.ShapeDtypeStruct((2, t, f), BF16), pl.BlockSpec((2, tm, n), lambda j, i, k: (0, i, j)))],
        [(tm, n), (tm, n)], epilogue)


def _ffn_da_bwd(df, w_out2d, gu, name):
    t, d = df.shape
    f = w_out2d.shape[0]
    half = N_DEV // 2
    n = f // half
    tm = _pick(t, 512, 16)
    step = 4 * LANE
    chunks = [(c, min(c + step, n)) for c in range(0, n, step)]

    def body(df_ref, w_ref, gu_ref, o_ref):
        df_t = df_ref[...]
        for c0, c1 in chunks:
            da = _dot(df_t, w_ref[c0:c1, :], NT)
            g = gu_ref[0, :, c0:c1].astype(F32)
            u = gu_ref[1, :, c0:c1].astype(F32)
            s = jax.nn.sigmoid(g)
            o_ref[0, :, c0:c1] = (da * u * (s * (1.0 + g * (1.0 - s)))).astype(BF16)
            o_ref[1, :, c0:c1] = (da * (g * s)).astype(BF16)

    gu_spec = pl.BlockSpec((2, tm, n), lambda j, i: (0, i, j))
    return pl.pallas_call(
        body, name=name, grid=(half, t // tm),
        in_specs=[pl.BlockSpec((tm, d), lambda j, i: (i, 0)), pl.BlockSpec((n, d), lambda j, i: (j, 0)), gu_spec],
        out_specs=gu_spec, out_shape=jax.ShapeDtypeStruct((2, t, f), BF16), compiler_params=_params(),
    )(df, w_out2d, gu)


def _ffn_dh_bwd(dgu, w_in, name):
    _, t, f = dgu.shape
    d, n = w_in.shape[1], w_in.shape[2]
    half = N_DEV // 2
    tm = _pick(t, 512, 16)

    def epilogue(accs, in_refs, out_refs):
        out_refs[0][...] = accs[0]

    return _mm_call(
        name, (t // tm, 1, half),
        [(dgu, pl.BlockSpec((None, tm, n), lambda i, j, k: (0, i, k))),
         (dgu, pl.BlockSpec((None, tm, n), lambda i, j, k: (1, i, k))),
         (w_in, pl.BlockSpec((None, d, n), lambda i, j, k: (k, 0, 0))),
         (w_in, pl.BlockSpec((None, d, n), lambda i, j, k: (k + half, 0, 0)))],
        [(0, 2, NT, 0), (1, 3, NT, 0)],
        [(jax.ShapeDtypeStruct((t, d), F32), pl.BlockSpec((tm, d), lambda i, j, k: (i, 0)))],
        [(tm, d)], epilogue)[0]


def _ffn_dwin_bwd(h, dgu, name):
    t, d = h.shape
    f = dgu.shape[2]
    half = N_DEV // 2
    n = f // half
    tk = _pick(t, 1024, 16)

    def epilogue(accs, in_refs, out_refs):
        out_refs[0][...] = accs[0].astype(BF16)

    return _mm_call(
        name, (N_DEV, 1, t // tk),
        [(h, pl.BlockSpec((tk, d), lambda j, i, k: (k, 0))),
         (dgu, pl.BlockSpec((None, tk, n), lambda j, i, k: (j // half, k, j % half)))],
        [(0, 1, TN, 0)],
        [(jax.ShapeDtypeStruct((N_DEV, d, n), BF16), pl.BlockSpec((None, d, n), lambda j, i, k: (j, 0, 0)))],
        [(d, n)], epilogue)[0]


def _make_ffn_block(tag):
    @jax.custom_vjp
    def ffn_block(x, ng, sh, sc, gate, w_in, w_out):
        return fwd(x, ng, sh, sc, gate, w_in, w_out)[0]

    def fwd(x, ng, sh, sc, gate, w_in, w_out):
        f = w_out.shape[0] * w_out.shape[1]
        w_out2d = w_out.reshape(f, w_out.shape[2])
        h = _norm_mod_fwd(x, ng, sc, sh, tag + "_norm")
        a, gu = _ffn_in_fwd(h, w_in, tag + "_in")
        xo, f1 = _res_mm_fwd(a, w_out2d, x, gate, 0.5, tag + "_out")
        return xo, (x, ng, sh, sc, gate, w_in, w_out, h, a, gu, f1)

    def bwd(res, dxo):
        x, ng, sh, sc, gate, w_in, w_out, h, a, gu, f1 = res
        f = w_out.shape[0] * w_out.shape[1]
        w_out2d = w_out.reshape(f, w_out.shape[2])
        df, dgate = _gate_bwd(dxo, f1, gate, 0.5, tag + "_dgate")
        dgu = _ffn_da_bwd(df, w_out2d, gu, tag + "_da")
        dw_out = _matmul(a, df, "tn", BF16, tag + "_dwout", tm=_pick(f, 1408, 16), tn=2048, tk=1024)
        dh = _ffn_dh_bwd(dgu, w_in, tag + "_dh")
        dw_in = _ffn_dwin_bwd(h, dgu, tag + "_dwin")
        dx, dng, dsc, dsh = _norm_mod_bwd(x, ng, sc, sh, dh, dxo, tag + "_dnorm")
        return dx, dng, dsh, dsc, dgate, dw_in, dw_out.reshape(w_out.shape)

    ffn_block.defvjp(fwd, bwd)
    return ffn_block


def _make_norm_proj(tag):
    @jax.custom_vjp
    def norm_proj(x, ng, sh, sc, w):
        return fwd(x, ng, sh, sc, w)[0]

    def fwd(x, ng, sh, sc, w):
        h = _norm_mod_fwd(x, ng, sc, sh, tag + "_norm")
        p = _matmul(h, w, "nn", F32, tag + "_mm", tm=1024, tn=1024, tk=w.shape[0])
        return p, (x, ng, sh, sc, w, h)

    def bwd(res, dp):
        x, ng, sh, sc, w, h = res
        dh = _matmul(dp, w, "nt", F32, tag + "_dh", tm=512, tn=w.shape[0], tk=2048)
        dw = _matmul(h, dp, "tn", BF16, tag + "_dw", tm=w.shape[0], tn=1024, tk=1024)
        dx, dng, dsc, dsh = _norm_mod_bwd(x, ng, sc, sh, dh, None, tag + "_dnorm")
        return dx, dng, dsh, dsc, dw

    norm_proj.defvjp(fwd, bwd)
    return norm_proj


def _make_norm_proj_carry(tag):
    @jax.custom_vjp
    def norm_proj(x, ng, sh, sc, w):
        return fwd(x, ng, sh, sc, w)[0]

    def fwd(x, ng, sh, sc, w):
        h = _norm_mod_fwd(x, ng, sc, sh, tag + "_norm")
        p = _matmul(h, w, "nn", F32, tag + "_mm", tm=1024, tn=1024, tk=w.shape[0])
        return (p, x), (x, ng, sh, sc, w, h)

    def bwd(res, cts):
        x, ng, sh, sc, w, h = res
        dp, dx_carry = cts
        dh = _matmul(dp, w, "nt", F32, tag + "_dh", tm=512, tn=w.shape[0], tk=2048)
        dw = _matmul(h, dp, "tn", BF16, tag + "_dw", tm=w.shape[0], tn=1024, tk=1024)
        dx, dng, dsc, dsh = _norm_mod_bwd(x, ng, sc, sh, dh, dx_carry, tag + "_dnorm")
        return dx, dng, dsh, dsc, dw

    norm_proj.defvjp(fwd, bwd)
    return norm_proj


def _make_split(tag, widths, total):
    offs = [sum(widths[:i]) for i in range(len(widths))]

    def concat_call(pieces):
        t = pieces[0].shape[0]
        tm = _pick(t, 256, 16)

        def body(*refs):
            o_ref = refs[-1]
            for ref, off, wd in zip(refs[:-1], offs, widths):
                o_ref[:, off:off + wd] = ref[...]
            end = offs[-1] + widths[-1]
            if end < total:
                o_ref[:, end:] = jnp.zeros((tm, total - end), F32)

        return pl.pallas_call(
            body, name=tag + "_concat", grid=(t // tm,),
            in_specs=[_row_spec(tm, wd) for wd in widths], out_specs=_row_spec(tm, total),
            out_shape=jax.ShapeDtypeStruct((t, total), F32), compiler_params=_params(),
        )(*pieces)

    @jax.custom_vjp
    def split(p):
        return tuple(p[:, off:off + wd] for off, wd in zip(offs, widths))

    def fwd(p):
        return split(p), None

    def bwd(_, cts):
        return (concat_call(list(cts)),)

    split.defvjp(fwd, bwd)
    return split


def _make_res_proj(tag):
    @jax.custom_vjp
    def res_proj(a, w, x, gate):
        return fwd(a, w, x, gate)[0]

    def fwd(a, w, x, gate):
        xo, f = _res_mm_fwd(a, w, x, gate, 1.0, tag + "_mm")
        return xo, (a, w, gate, f)

    def bwd(res, dxo):
        a, w, gate, f = res
        df, dgate = _gate_bwd(dxo, f, gate, 1.0, tag + "_dgate")
        da = _matmul(df, w, "nt", BF16, tag + "_da", tm=1024, tn=1024, tk=2048)
        dw = _matmul(a, df, "tn", BF16, tag + "_dw", tm=1024, tn=2048, tk=1024)
        return da, dw, dxo, dgate

    res_proj.defvjp(fwd, bwd)
    return res_proj


def _make_small_mm(tag):
    @jax.custom_vjp
    def small_mm(a, w):
        return _matmul(a, w, "nn", F32, tag + "_mm", tm=a.shape[0], tn=768, tk=w.shape[0])

    def fwd(a, w):
        return small_mm(a, w), (a, w)

    def bwd(res, dr):
        a, w = res
        da = _matmul(dr, w, "nt", F32, tag + "_da", tm=a.shape[0], tn=w.shape[0], tk=768)
        dw = _matmul(a, dr, "tn", F32, tag + "_dw", tm=1024, tn=768, tk=a.shape[0])
        return da, dw

    small_mm.defvjp(fwd, bwd)
    return small_mm


def _ret_chunk_terms(lg, c, reverse):
    row = lax.broadcasted_iota(jnp.int32, (c, c), 0).astype(F32)
    col = lax.broadcasted_iota(jnp.int32, (c, c), 1).astype(F32)
    pos = lax.broadcasted_iota(jnp.int32, (c, 1), 0).astype(F32)
    if reverse:
        diff = col - row
        mask = diff > 0.0
        e_exp = float(c) - pos
        f_exp = pos
    else:
        diff = row - col
        mask = diff >= 0.0
        e_exp = pos + 1.0
        f_exp = float(c - 1) - pos
    diffm = jnp.where(mask, diff, 0.0)
    dm = jnp.where(mask, jnp.exp(lg * diffm), 0.0)
    return diffm, dm, e_exp, jnp.exp(lg * e_exp), f_exp, jnp.exp(lg * f_exp)


def _lane0(val):
    lane = lax.broadcasted_iota(jnp.int32, (1, LANE), 1)
    return jnp.where(lane == 0, val, 0.0)


RET_HEAD_BLOCK = 4


def _make_ret_dir(tag, reverse):
    hb = RET_HEAD_BLOCK

    def heads_spec(nc, width, flip):
        if flip:
            return pl.BlockSpec((hb, RET_CHUNK, width), lambda h, t: (h, nc - 1 - t, 0))
        return pl.BlockSpec((hb, RET_CHUNK, width), lambda h, t: (h, t, 0))

    def state_spec(nc, flip):
        if flip:
            return pl.BlockSpec((hb, None, RET_DKP, RET_DV), lambda h, t: (h, nc - 1 - t, 0, 0))
        return pl.BlockSpec((hb, None, RET_DKP, RET_DV), lambda h, t: (h, t, 0, 0))

    lg_spec = pl.BlockSpec((hb, 1, LANE), lambda h, t: (h, 0, 0))
    s0_spec = pl.BlockSpec((hb, RET_DKP, RET_DV), lambda h, t: (h, 0, 0))

    def fwd_call(q, k, v, lgb, s0):
        hh, ll, _ = q.shape
        c = RET_CHUNK
        nc = ll // c

        def body(q_ref, k_ref, v_ref, lg_ref, s0_ref, y_ref, sall_ref, s_scr):
            @pl.when(pl.program_id(1) == 0)
            def _():
                s_scr[...] = s0_ref[...]

            for b in range(hb):
                lg = lg_ref[b][:, :1]
                _, dm, _, xi, _, zeta = _ret_chunk_terms(lg, c, reverse)
                q_t, k_t, v_t = q_ref[b], k_ref[b], v_ref[b]
                s = s_scr[b]
                p = _dot(q_t, k_t, NT) * dm
                y_ref[b] = _dot(p, v_t, NN) + _dot(q_t * xi, s, NN)
                sall_ref[b] = s
                s_scr[b] = jnp.exp(lg * float(c)) * s + _dot(k_t * zeta, v_t, TN)

        return pl.pallas_call(
            body, name=tag + "_fwd", grid=(hh // hb, nc),
            in_specs=[heads_spec(nc, RET_DKP,reverse), heads_spec(nc, RET_DKP,reverse),
                      heads_spec(nc, RET_DV, reverse), lg_spec, s0_spec],
            out_specs=[heads_spec(nc, RET_DV, reverse), state_spec(nc, reverse)],
            out_shape=[jax.ShapeDtypeStruct((hh, ll, RET_DV), F32),
                       jax.ShapeDtypeStruct((hh, nc, RET_DKP, RET_DV), F32)],
            scratch_shapes=[pltpu.VMEM((hb, RET_DKP, RET_DV), F32)],
            compiler_params=_params(),
        )(q, k, v, lgb, s0)

    def bwd_call(q, k, v, lgb, sall, dy):
        hh, ll, _ = q.shape
        c = RET_CHUNK
        nc = ll // c
        flip = not reverse

        def body(q_ref, k_ref, v_ref, lg_ref, sall_ref, dy_ref, dq_ref, dk_ref, dv_ref, dlg_ref, ds0_ref, ds_scr):
            @pl.when(pl.program_id(1) == 0)
            def _():
                ds_scr[...] = jnp.zeros_like(ds_scr)
                dlg_ref[...] = jnp.zeros_like(dlg_ref)

            def total(m):
                return jnp.sum(jnp.sum(m, axis=1, keepdims=True), axis=0, keepdims=True)

            for b in range(hb):
                lg = lg_ref[b][:, :1]
                diffm, dm, e_exp, xi, f_exp, zeta = _ret_chunk_terms(lg, c, reverse)
                q_t, k_t, v_t, dy_t = q_ref[b], k_ref[b], v_ref[b], dy_ref[b]
                s = sall_ref[b]
                dsn = ds_scr[b]
                a = _dot(q_t, k_t, NT)
                da = _dot(dy_t, v_t, NT) * dm
                g = _dot(dy_t, s, NT)
                hm = _dot(v_t, dsn, NT)
                dq_ref[b] = _dot(da, k_t, NN) + xi * g
                dk_ref[b] = _dot(da, q_t, TN) + zeta * hm
                dv_ref[b] = _dot(a * dm, dy_t, TN) + _dot(k_t * zeta, dsn, NN)
                gc = jnp.exp(lg * float(c))
                ds_scr[b] = gc * dsn + _dot(q_t * xi, dy_t, TN)
                dl = (total(da * a * diffm) + total(e_exp * xi * q_t * g)
                      + float(c) * gc * total(s * dsn) + total(f_exp * zeta * k_t * hm))
                dlg_ref[b] += _lane0(dl)

            @pl.when(pl.program_id(1) == nc - 1)
            def _():
                ds0_ref[...] = ds_scr[...]

        return pl.pallas_call(
            body, name=tag + "_bwd", grid=(hh // hb, nc),
            in_specs=[heads_spec(nc, RET_DKP,flip), heads_spec(nc, RET_DKP,flip), heads_spec(nc, RET_DV, flip),
                      lg_spec, state_spec(nc, flip), heads_spec(nc, RET_DV, flip)],
            out_specs=[heads_spec(nc, RET_DKP,flip), heads_spec(nc, RET_DKP,flip), heads_spec(nc, RET_DV, flip),
                       lg_spec, s0_spec],
            out_shape=[jax.ShapeDtypeStruct((hh, ll, RET_DKP), F32), jax.ShapeDtypeStruct((hh, ll, RET_DKP), F32),
                       jax.ShapeDtypeStruct((hh, ll, RET_DV), F32), jax.ShapeDtypeStruct((hh, 1, LANE), F32),
                       jax.ShapeDtypeStruct((hh, RET_DKP, RET_DV), F32)],
            scratch_shapes=[pltpu.VMEM((hb, RET_DKP, RET_DV), F32)],
            compiler_params=_params(),
        )(q, k, v, lgb, sall, dy)

    @jax.custom_vjp
    def ret_dir(q, k, v, lgb, s0):
        return fwd_call(q, k, v, lgb, s0)[0]

    def fwd(q, k, v, lgb, s0):
        y, sall = fwd_call(q, k, v, lgb, s0)
        return y, (q, k, v, lgb, sall)

    def bwd(res, dy):
        q, k, v, lgb, sall = res
        return tuple(bwd_call(q, k, v, lgb, sall, dy))

    ret_dir.defvjp(fwd, bwd)
    return ret_dir


def _make_ctx_state(tag, reverse):
    hb = RET_HEAD_BLOCK
    c = RET_CHUNK
    k_spec = pl.BlockSpec((hb, c, RET_DKP), lambda h: (h, 0, 0))
    v_spec = pl.BlockSpec((hb, c, RET_DV), lambda h: (h, 0, 0))
    lg_spec = pl.BlockSpec((hb, 1, LANE), lambda h: (h, 0, 0))
    s_spec = pl.BlockSpec((hb, RET_DKP, RET_DV), lambda h: (h, 0, 0))

    def fwd_call(k, v, lgb):
        hh = k.shape[0]

        def body(k_ref, v_ref, lg_ref, s_ref):
            for b in range(hb):
                _, _, _, _, _, zeta = _ret_chunk_terms(lg_ref[b][:, :1], c, reverse)
                s_ref[b] = _dot(k_ref[b] * zeta, v_ref[b], TN)

        return pl.pallas_call(
            body, name=tag + "_fwd", grid=(hh // hb,), in_specs=[k_spec, v_spec, lg_spec], out_specs=s_spec,
            out_shape=jax.ShapeDtypeStruct((hh, RET_DKP, RET_DV), F32), compiler_params=_params(),
        )(k, v, lgb)

    def bwd_call(k, v, lgb, ds):
        hh = k.shape[0]

        def body(k_ref, v_ref, lg_ref, ds_ref, dk_ref, dv_ref, dlg_ref):
            for b in range(hb):
                _, _, _, _, f_exp, zeta = _ret_chunk_terms(lg_ref[b][:, :1], c, reverse)
                k_t, v_t, ds = k_ref[b], v_ref[b], ds_ref[b]
                hm = _dot(v_t, ds, NT)
                dk_ref[b] = zeta * hm
                dv_ref[b] = _dot(k_t * zeta, ds, NN)
                tot = jnp.sum(jnp.sum(f_exp * zeta * k_t * hm, axis=1, keepdims=True), axis=0, keepdims=True)
                dlg_ref[b] = _lane0(tot)

        return pl.pallas_call(
            body, name=tag + "_bwd", grid=(hh // hb,), in_specs=[k_spec, v_spec, lg_spec, s_spec],
            out_specs=[k_spec, v_spec, lg_spec],
            out_shape=[jax.ShapeDtypeStruct(k.shape, F32), jax.ShapeDtypeStruct(v.shape, F32),
                       jax.ShapeDtypeStruct((hh, 1, LANE), F32)],
            compiler_params=_params(),
        )(k, v, lgb, ds)

    @jax.custom_vjp
    def ctx_state(k, v, lgb):
        return fwd_call(k, v, lgb)

    def fwd(k, v, lgb):
        return fwd_call(k, v, lgb), (k, v, lgb)

    def bwd(res, ds):
        return tuple(bwd_call(*res, ds))

    ctx_state.defvjp(fwd, bwd)
    return ctx_state


def _rope_tables_call(name, n, inv, shift, axial):
    tm = _pick(n, 1024, 8)
    inv_lane = jnp.tile(inv, LANE // inv.shape[0])[None, :]

    def body(inv_ref, cos_ref, s1_ref, s2_ref):
        t = lax.broadcasted_iota(jnp.int32, (tm, LANE), 0) + pl.program_id(0) * tm
        lane = lax.broadcasted_iota(jnp.int32, (tm, LANE), 1)
        if axial:
            pos = jnp.where(lane % (2 * MLA_ROPE // 2) < MLA_ROPE // 2, t // GRID_W, t % GRID_W)
        else:
            pos = t
        ang = pos.astype(F32) * inv_ref[...]
        sin = jnp.sin(ang)
        first = lane % (2 * shift) < shift
        cos_ref[...] = jnp.cos(ang)
        s1_ref[...] = jnp.where(first, -sin, 0.0)
        s2_ref[...] = jnp.where(first, 0.0, sin)

    tab = jax.ShapeDtypeStruct((n, LANE), F32)
    return tuple(pl.pallas_call(
        body, name=name, grid=(n // tm,), in_specs=[_vec_spec(LANE)], out_specs=[_row_spec(tm, LANE)] * 3,
        out_shape=[tab, tab, tab], compiler_params=_params(),
    )(inv_lane))


def _ret_tables(n_lat):
    inv = RET_ROPE_BASE ** (-jnp.arange(0, RET_DK, 2, dtype=F32) / RET_DK)
    return _rope_tables_call("ret_tables", n_lat, inv, RET_DK // 2, False)


def _make_ret_pack(tag, n_lat, n_ctx):
    hh = RET_HEADS
    tm = MLA_PACK_ROWS
    k_scale = RET_DK ** -0.5
    shift = RET_DK // 2
    tabs = _ret_tables(n_lat)

    def low_lanes():
        return lax.broadcasted_iota(jnp.int32, (1, LANE), 1) < RET_DK

    def rows(width):
        return pl.BlockSpec((tm, width), lambda i: (i, 0))

    def heads(width):
        return pl.BlockSpec((hh, tm, width), lambda i: (0, i, 0))

    def split_pairs(src_ref, dst_ref, scale, rope):
        keep = low_lanes()
        for j in range(hh // 2):
            blk = src_ref[:, LANE * j:LANE * (j + 1)]
            if scale != 1.0:
                blk = blk * scale
            if rope is not None:
                blk = _rope128(blk, *rope, shift=shift)
            dst_ref[2 * j] = jnp.where(keep, blk, 0.0)
            dst_ref[2 * j + 1] = jnp.where(keep, pltpu.roll(blk, RET_DK, 1), 0.0)

    def merge_pairs(src_refs, dst_ref, scale, rope):
        keep = low_lanes()
        for j in range(hh // 2):
            even = sum(r[2 * j] for r in src_refs)
            odd = sum(r[2 * j + 1] for r in src_refs)
            g = jnp.where(keep, even, pltpu.roll(odd, RET_DK, 1))
            if rope is not None:
                g = _rope128_t(g, *rope, shift=shift)
            dst_ref[:, LANE * j:LANE * (j + 1)] = g * scale if scale != 1.0 else g

    def pack_call(name, n, q, k, v, rope):
        with_q = q is not None

        def body(*refs):
            refs = list(refs)
            q_ref = refs.pop(0) if with_q else None
            k_ref, v_ref = refs.pop(0), refs.pop(0)
            tab = tuple(r[...] for r in refs[:3]) if rope else None
            outs = refs[3:] if rope else refs
            if with_q:
                split_pairs(q_ref, outs[0], 1.0, tab)
                outs = outs[1:]
            split_pairs(k_ref, outs[0], k_scale, tab)
            for h in range(hh):
                outs[1][h] = v_ref[:, RET_DV * h:RET_DV * (h + 1)]

        ins = ([q] if with_q else []) + [k, v] + (list(tabs) if rope else [])
        in_specs = ([rows(q.shape[1])] if with_q else []) + [rows(k.shape[1]), rows(v.shape[1])]
        in_specs += [rows(LANE)] * 3 if rope else []
        n_out = 3 if with_q else 2
        return pl.pallas_call(
            body, name=name, grid=(n // tm,), in_specs=in_specs,
            out_specs=[heads(RET_DKP)] * (n_out - 1) + [heads(RET_DV)],
            out_shape=[jax.ShapeDtypeStruct((hh, n, RET_DKP), F32)] * (n_out - 1)
            + [jax.ShapeDtypeStruct((hh, n, RET_DV), F32)],
            compiler_params=_params(),
        )(*ins)

    def unpack_call(name, n, dqs, dks, dvs, rope):
        with_q = len(dqs) > 0
        uses = len(dks)

        def body(*refs):
            refs = list(refs)
            dq_refs = [refs.pop(0) for _ in range(len(dqs))]
            dk_refs = [refs.pop(0) for _ in range(uses)]
            dv_refs = [refs.pop(0) for _ in range(uses)]
            tab = tuple(r[...] for r in refs[:3]) if rope else None
            outs = refs[3:] if rope else refs
            if with_q:
                merge_pairs(dq_refs, outs[0], 1.0, tab)
                outs = outs[1:]
            merge_pairs(dk_refs, outs[0], k_scale, tab)
            for h in range(hh):
                outs[1][:, RET_DV * h:RET_DV * (h + 1)] = sum(r[h] for r in dv_refs)

        ins = list(dqs) + list(dks) + list(dvs) + (list(tabs) if rope else [])
        in_specs = [heads(RET_DKP)] * (len(dqs) + uses) + [heads(RET_DV)] * uses + ([rows(LANE)] * 3 if rope else [])
        n_out = 3 if with_q else 2
        return pl.pallas_call(
            body, name=name, grid=(n // tm,), in_specs=in_specs,
            out_specs=[rows(hh * RET_DK)] * (n_out - 1) + [rows(hh * RET_DV)],
            out_shape=[jax.ShapeDtypeStruct((n, hh * RET_DK), F32)] * (n_out - 1)
            + [jax.ShapeDtypeStruct((n, hh * RET_DV), F32)],
            compiler_params=_params(),
        )(*ins)

    @jax.custom_vjp
    def ret_pack(rq, rk, rv, crk, crv):
        q, k, v = pack_call(tag + "_lat", n_lat, rq, rk, rv, True)
        k_c, v_c = pack_call(tag + "_ctx", n_ctx, None, crk, crv, False)
        return (q, k, v), (q, k, v), (k_c, v_c), (k_c, v_c)

    def fwd(rq, rk, rv, crk, crv):
        return ret_pack(rq, rk, rv, crk, crv), None

    def bwd(_, cts):
        lat_f, lat_b, ctx_f, ctx_b = cts
        drq, drk, drv = unpack_call(tag + "_dlat", n_lat, [lat_f[0], lat_b[0]], [lat_f[1], lat_b[1]],
                                    [lat_f[2], lat_b[2]], True)
        dcrk, dcrv = unpack_call(tag + "_dctx", n_ctx, [], [ctx_f[0], ctx_b[0]], [ctx_f[1], ctx_b[1]], False)
        return drq, drk, drv, dcrk, dcrv

    ret_pack.defvjp(fwd, bwd)
    return ret_pack


def _ret_out_tile(y, g):
    mu = jnp.mean(y, axis=-1, keepdims=True)
    var = jnp.mean(jnp.square(y - mu), axis=-1, keepdims=True)
    return (g * jax.nn.sigmoid(g)) * ((y - mu) * lax.rsqrt(var + GN_EPS))


def _make_ret_out(tag):
    def specs(tm):
        y_spec = pl.BlockSpec((None, tm, RET_DV), lambda h, i: (h, i, 0))
        g_spec = pl.BlockSpec((tm, RET_DV), lambda h, i: (i, h))
        return y_spec, g_spec

    def fwd_call(yf, yb, g):
        hh, n, _ = yf.shape
        tm = _pick(n, 1024, 16)
        y_spec, g_spec = specs(tm)

        def body(yf_ref, yb_ref, g_ref, o_ref):
            o_ref[...] = _ret_out_tile(yf_ref[...] + yb_ref[...], g_ref[...]).astype(BF16)

        return pl.pallas_call(
            body, name=tag + "_fwd", grid=(hh, n // tm), in_specs=[y_spec, y_spec, g_spec], out_specs=g_spec,
            out_shape=jax.ShapeDtypeStruct((n, hh * RET_DV), BF16), compiler_params=_params(),
        )(yf, yb, g)

    def bwd_call(yf, yb, g, do):
        hh, n, _ = yf.shape
        tm = _pick(n, 1024, 16)
        y_spec, g_spec = specs(tm)

        def body(yf_ref, yb_ref, g_ref, do_ref, dy_ref, dg_ref):
            _, vjp = jax.vjp(_ret_out_tile, yf_ref[...] + yb_ref[...], g_ref[...])
            dy, dg = vjp(do_ref[...].astype(F32))
            dy_ref[...] = dy
            dg_ref[...] = dg

        return pl.pallas_call(
            body, name=tag + "_bwd", grid=(hh, n // tm), in_specs=[y_spec, y_spec, g_spec, g_spec],
            out_specs=[y_spec, g_spec],
            out_shape=[jax.ShapeDtypeStruct(yf.shape, F32), jax.ShapeDtypeStruct(g.shape, F32)],
            compiler_params=_params(),
        )(yf, yb, g, do)

    @jax.custom_vjp
    def ret_out(yf, yb, g):
        return fwd_call(yf, yb, g)

    def fwd(yf, yb, g):
        return fwd_call(yf, yb, g), (yf, yb, g)

    def bwd(res, do):
        dy, dg = bwd_call(*res, do)
        return dy, dy, dg

    ret_out.defvjp(fwd, bwd)
    return ret_out


MLA_DQ_PAD = 2 * LANE
MLA_PACK_ROWS = 256


def _rope128(x, cos, s1, s2, shift=16):
    return x * cos + pltpu.roll(x, LANE - shift, 1) * s1 + pltpu.roll(x, shift, 1) * s2


def _rope128_t(g, cos, s1, s2, shift=16):
    return g * cos + pltpu.roll(g * s1, shift, 1) + pltpu.roll(g * s2, LANE - shift, 1)


def _axial_tables(n_lat):
    half = MLA_ROPE // 2
    inv = AXIAL_BASE ** (-jnp.arange(0, half, 2, dtype=F32) / half)
    return _rope_tables_call("mla_tables", n_lat, inv, half // 2, True)


def _make_mla_pack(tag, n_lat, n_ctx, scale):
    hh = MLA_HEADS
    tm = MLA_PACK_ROWS
    ll = n_lat + n_ctx
    rope0 = hh * MLA_NOPE
    tabs = _axial_tables(n_lat)

    def rope_lanes():
        return lax.broadcasted_iota(jnp.int32, (1, LANE), 1) < MLA_ROPE

    def rows(width):
        return pl.BlockSpec((tm, width), lambda i: (i, 0))

    def heads(width, off):
        return pl.BlockSpec((hh, tm, width), lambda i: (0, i + off, 0))

    def heads_t(width, off):
        return pl.BlockSpec((hh, width, tm), lambda i: (0, 0, i + off))

    def put_kv(kv_ref, kr_rot, k_ref, v_ref, kt_ref, vt_ref):
        kr_b = kr_rot.astype(BF16)
        kr_t = jnp.transpose(kr_rot).astype(BF16)
        for h in range(hh):
            k_nope = kv_ref[:, 2 * LANE * h:2 * LANE * h + MLA_NOPE]
            val = kv_ref[:, 2 * LANE * h + MLA_NOPE:2 * LANE * (h + 1)]
            k_ref[h, :, :MLA_NOPE] = k_nope.astype(BF16)
            k_ref[h, :, MLA_NOPE:] = kr_b
            v_ref[h] = val.astype(BF16)
            kt_ref[h, :MLA_NOPE, :] = jnp.transpose(k_nope).astype(BF16)
            kt_ref[h, MLA_NOPE:, :] = kr_t
            vt_ref[h] = jnp.transpose(val).astype(BF16)

    def fwd_lat(qp, kv, kr):
        def body(qp_ref, kv_ref, kr_ref, cos_ref, s1_ref, s2_ref, q_ref, k_ref, v_ref, kt_ref, vt_ref):
            cos, s1, s2 = cos_ref[...], s1_ref[...], s2_ref[...]
            keep = rope_lanes()
            for j in range(hh // 2):
                rot = _rope128(qp_ref[:, rope0 + LANE * j:rope0 + LANE * (j + 1)], cos, s1, s2)
                q_ref[2 * j, :, MLA_NOPE:] = jnp.where(keep, rot, 0.0).astype(BF16)
                q_ref[2 * j + 1, :, MLA_NOPE:] = jnp.where(keep, pltpu.roll(rot, MLA_ROPE, 1), 0.0).astype(BF16)
            for h in range(hh):
                q_ref[h, :, :MLA_NOPE] = qp_ref[:, MLA_NOPE * h:MLA_NOPE * (h + 1)].astype(BF16)
            kr_rot = jnp.where(keep, _rope128(kr_ref[...], cos, s1, s2), 0.0)
            put_kv(kv_ref, kr_rot, k_ref, v_ref, kt_ref, vt_ref)

        return pl.pallas_call(
            body, name=tag + "_lat", grid=(n_lat // tm,),
            in_specs=[rows(qp.shape[1]), rows(kv.shape[1]), rows(LANE), rows(LANE), rows(LANE), rows(LANE)],
            out_specs=[heads(MLA_DQ_PAD, 0), heads(MLA_DQ_PAD, 0), heads(MLA_V, 0), heads_t(MLA_DQ_PAD, 0),
                       heads_t(MLA_V, 0)],
            out_shape=[jax.ShapeDtypeStruct((hh, n_lat, MLA_DQ_PAD), BF16),
                       jax.ShapeDtypeStruct((hh, ll, MLA_DQ_PAD), BF16), jax.ShapeDtypeStruct((hh, ll, MLA_V), BF16),
                       jax.ShapeDtypeStruct((hh, MLA_DQ_PAD, ll), BF16), jax.ShapeDtypeStruct((hh, MLA_V, ll), BF16)],
            compiler_params=_params(),
        )(qp, kv, kr, *tabs)

    def fwd_ctx(kv_c, kr_c, bufs):
        def body(kv_ref, kr_ref, k_in, v_in, kt_in, vt_in, k_ref, v_ref, kt_ref, vt_ref):
            kr_rot = jnp.where(rope_lanes(), kr_ref[...], 0.0)
            put_kv(kv_ref, kr_rot, k_ref, v_ref, kt_ref, vt_ref)

        any_spec = pl.BlockSpec(memory_space=pl.ANY)
        off = n_lat // tm
        return pl.pallas_call(
            body, name=tag + "_ctx", grid=(n_ctx // tm,),
            in_specs=[rows(kv_c.shape[1]), rows(LANE)] + [any_spec] * 4,
            out_specs=[heads(MLA_DQ_PAD, off), heads(MLA_V, off), heads_t(MLA_DQ_PAD, off), heads_t(MLA_V, off)],
            out_shape=[jax.ShapeDtypeStruct(b.shape, BF16) for b in bufs],
            input_output_aliases={2: 0, 3: 1, 4: 2, 5: 3}, compiler_params=_params(),
        )(kv_c, kr_c, *bufs)

    def take_kv(dk_ref, dv_ref, dkv_ref):
        dkr = jnp.zeros((tm, LANE), F32)
        for h in range(hh):
            dkv_ref[:, 2 * LANE * h:2 * LANE * h + MLA_NOPE] = dk_ref[h, :, :MLA_NOPE].astype(F32)
            dkv_ref[:, 2 * LANE * h + MLA_NOPE:2 * LANE * (h + 1)] = dv_ref[h].astype(F32)
            dkr = dkr + dk_ref[h, :, MLA_NOPE:].astype(F32)
        return jnp.where(rope_lanes(), dkr, 0.0)

    def bwd_lat(dqt, dk, dv, qp_width, kv_width):
        def body(dqt_ref, dk_ref, dv_ref, cos_ref, s1_ref, s2_ref, dqp_ref, dkv_ref, dkr_ref):
            cos, s1, s2 = cos_ref[...], s1_ref[...], s2_ref[...]
            keep = rope_lanes()
            for j in range(hh // 2):
                even = jnp.transpose(dqt_ref[2 * j]) * scale
                odd = jnp.transpose(dqt_ref[2 * j + 1]) * scale
                dqp_ref[:, MLA_NOPE * 2 * j:MLA_NOPE * (2 * j + 1)] = even[:, :MLA_NOPE]
                dqp_ref[:, MLA_NOPE * (2 * j + 1):MLA_NOPE * (2 * j + 2)] = odd[:, :MLA_NOPE]
                g = jnp.where(keep, even[:, MLA_NOPE:], pltpu.roll(odd[:, MLA_NOPE:], MLA_ROPE, 1))
                dqp_ref[:, rope0 + LANE * j:rope0 + LANE * (j + 1)] = _rope128_t(g, cos, s1, s2)
            dkr_ref[...] = jnp.where(keep, _rope128_t(take_kv(dk_ref, dv_ref, dkv_ref), cos, s1, s2), 0.0)

        return pl.pallas_call(
            body, name=tag + "_dlat", grid=(n_lat // tm,),
            in_specs=[pl.BlockSpec((hh, MLA_DQ_PAD, tm), lambda i: (0, 0, i)),
                      heads(MLA_DQ_PAD, 0), heads(MLA_V, 0), rows(LANE), rows(LANE), rows(LANE)],
            out_specs=[rows(qp_width), rows(kv_width), rows(LANE)],
            out_shape=[jax.ShapeDtypeStruct((n_lat, qp_width), F32), jax.ShapeDtypeStruct((n_lat, kv_width), F32),
                       jax.ShapeDtypeStruct((n_lat, LANE), F32)],
            compiler_params=_params(),
        )(dqt, dk, dv, *tabs)

    def bwd_ctx(dk, dv, kv_width):
        def body(dk_ref, dv_ref, dkv_ref, dkr_ref):
            dkr_ref[...] = take_kv(dk_ref, dv_ref, dkv_ref)

        off = n_lat // tm
        return pl.pallas_call(
            body, name=tag + "_dctx", grid=(n_ctx // tm,),
            in_specs=[heads(MLA_DQ_PAD, off), heads(MLA_V, off)],
            out_specs=[rows(kv_width), rows(LANE)],
            out_shape=[jax.ShapeDtypeStruct((n_ctx, kv_width), F32), jax.ShapeDtypeStruct((n_ctx, LANE), F32)],
            compiler_params=_params(),
        )(dk, dv)

    def pack(qp, kv, kr, kv_c, kr_c):
        q, *bufs = fwd_lat(qp, kv, kr)
        return (q, *fwd_ctx(kv_c, kr_c, bufs))

    def unpack(dqt, dk, dv):
        qp_width, kv_width = hh * (MLA_NOPE + MLA_ROPE), hh * (MLA_NOPE + MLA_V)
        dqp, dkv, dkr = bwd_lat(dqt, dk, dv, qp_width, kv_width)
        dkv_c, dkr_c = bwd_ctx(dk, dv, kv_width)
        return dqp, dkv, dkr, dkv_c, dkr_c

    return pack, unpack


def _make_mla(tag, n_lat, n_ctx):
    scale = (MLA_NOPE + MLA_ROPE) ** -0.5
    pack, unpack = _make_mla_pack(tag + "pack", n_lat, n_ctx, scale)
    attn_fwd, attn_delta, attn_bwd = _make_attention(tag, scale)

    @jax.custom_vjp
    def mla(qp, kv, kr, kv_c, kr_c):
        q, k, _, _, vt = pack(qp, kv, kr, kv_c, kr_c)
        return attn_fwd(q, k, vt)[0]

    def fwd(qp, kv, kr, kv_c, kr_c):
        q, k, v, kt, vt = pack(qp, kv, kr, kv_c, kr_c)
        o, lse = attn_fwd(q, k, vt)
        return o, (q, k, kt, v, o, lse)

    def bwd(res, do):
        q, k, kt, v, o, lse = res
        delta = attn_delta(o, do, q.shape[0])
        dqt, dk, dv = attn_bwd(q, k, kt, v, do, lse, delta)
        return unpack(dqt, dk, dv)

    mla.defvjp(fwd, bwd)
    return mla


def _make_attention(tag, scale):
    neg_big = -1e30
    log2e = 1.4426950408889634
    sub = 256

    def fwd_call(q, k, vt):
        hh, n, dq = q.shape
        dv, ll = vt.shape[1], vt.shape[2]
        tq, tk = _pick(n, 1024), _pick(ll, 1408)
        sb = sub if tk % sub == 0 else tk
        c2 = scale * log2e
        k_steps = ll // tk

        def body(q_ref, k_ref, vt_ref, o_ref, lse_ref, m_scr, l_scr, acc_scr, s_scr, p_scr):
            j = pl.program_id(2)

            @pl.when(j == 0)
            def _():
                m_scr[...] = jnp.full_like(m_scr, neg_big)
                l_scr[...] = jnp.zeros_like(l_scr)
                acc_scr[...] = jnp.zeros_like(acc_scr)

            q_t = q_ref[...]
            m_prev = m_scr[...]
            m_new = m_prev
            for kk in range(tk // sb):
                rows = slice(kk * sb, (kk + 1) * sb)
                s_t = _dot(k_ref[rows, :], q_t, NT)
                s_scr[rows, :] = s_t
                m_new = jnp.maximum(m_new, jnp.max(s_t, axis=0, keepdims=True))
            mc = m_new * c2
            l_part = jnp.zeros_like(m_new)
            for kk in range(tk // sb):
                rows = slice(kk * sb, (kk + 1) * sb)
                p_t = jnp.exp2(s_scr[rows, :] * c2 - mc)
                l_part = l_part + jnp.sum(p_t, axis=0, keepdims=True)
                p_scr[rows, :] = p_t.astype(BF16)
            alpha = jnp.exp2((m_prev - m_new) * c2)
            l_scr[...] = alpha * l_scr[...] + l_part
            acc_scr[...] = alpha * acc_scr[...] + _dot(vt_ref[...], p_scr[...], NN)
            m_scr[...] = m_new

            @pl.when(j == k_steps - 1)
            def _():
                o_ref[...] = jnp.transpose(acc_scr[...] / l_scr[...]).astype(BF16)
                lse_ref[...] = m_scr[...] * scale + jnp.log(l_scr[...])

        return pl.pallas_call(
            body, name=tag + "_fwd", grid=(hh, n // tq, k_steps),
            in_specs=[pl.BlockSpec((None, tq, dq), lambda h, i, j: (h, i, 0)),
                      pl.BlockSpec((None, tk, dq), lambda h, i, j: (h, j, 0)),
                      pl.BlockSpec((None, dv, tk), lambda h, i, j: (h, 0, j))],
            out_specs=[pl.BlockSpec((tq, dv), lambda h, i, j: (i, h)),
                       pl.BlockSpec((None, 1, tq), lambda h, i, j: (h, 0, i))],
            out_shape=[jax.ShapeDtypeStruct((n, hh * dv), BF16), jax.ShapeDtypeStruct((hh, 1, n), F32)],
            scratch_shapes=[pltpu.VMEM((1, tq), F32), pltpu.VMEM((1, tq), F32), pltpu.VMEM((dv, tq), F32),
                            pltpu.VMEM((tk, tq), F32), pltpu.VMEM((tk, tq), BF16)],
            compiler_params=_params(),
        )(q, k, vt)

    def delta_call(o, do, hh):
        n = o.shape[0]
        dv = o.shape[1] // hh
        tq = _pick(n, 1024)

        def body(o_ref, do_ref, d_ref):
            prod_t = jnp.transpose(o_ref[...].astype(F32) * do_ref[...].astype(F32))
            d_ref[...] = jnp.sum(prod_t, axis=0, keepdims=True)

        spec = pl.BlockSpec((tq, dv), lambda h, i: (i, h))
        return pl.pallas_call(
            body, name=tag + "_delta", grid=(hh, n // tq), in_specs=[spec, spec],
            out_specs=pl.BlockSpec((None, 1, tq), lambda h, i: (h, 0, i)),
            out_shape=jax.ShapeDtypeStruct((hh, 1, n), F32), compiler_params=_params(),
        )(o, do)

    def bwd_call(q, k, kt, v, do, lse, delta):
        hh, n, dq = q.shape
        ll, dv = k.shape[1], v.shape[2]
        tq, tk = _pick(n, 1024), _pick(ll, 1408)
        sb = tk
        c2 = scale * log2e
        q_steps = n // tq

        def body(q_ref, k_ref, kt_ref, v_ref, do_ref, lse_ref, d_ref, dqt_ref, dk_ref, dv_ref, dk_scr, dv_scr):
            j = pl.program_id(1)
            i = pl.program_id(2)

            @pl.when(i == 0)
            def _():
                dk_scr[...] = jnp.zeros_like(dk_scr)
                dv_scr[...] = jnp.zeros_like(dv_scr)

            q_t, do_t = q_ref[...], do_ref[...]
            lse2 = lse_ref[...] * log2e
            delta_t = d_ref[...]
            dq_part = None
            for kk in range(tk // sb):
                rows = slice(kk * sb, (kk + 1) * sb)
                s_t = _dot(k_ref[rows, :], q_t, NT)
                p_t = jnp.exp2(s_t * c2 - lse2)
                ds_t = p_t * (_dot(v_ref[rows, :], do_t, NT) - delta_t)
                dv_scr[rows, :] += _dot(p_t, do_t, NN)
                dk_scr[rows, :] += _dot(ds_t, q_t, NN)
                part = _dot(kt_ref[:, rows], ds_t, NN)
                dq_part = part if dq_part is None else dq_part + part
            cols = pl.ds(pl.multiple_of(i * tq, tq), tq)

            @pl.when(j == 0)
            def _():
                dqt_ref[:, cols] = dq_part

            @pl.when(j > 0)
            def _():
                dqt_ref[:, cols] += dq_part

            @pl.when(i == q_steps - 1)
            def _():
                dk_ref[...] = (dk_scr[...] * scale).astype(BF16)
                dv_ref[...] = dv_scr[...].astype(BF16)

        return pl.pallas_call(
            body, name=tag + "_bwd", grid=(hh, ll // tk, q_steps),
            in_specs=[pl.BlockSpec((None, tq, dq), lambda h, j, i: (h, i, 0)),
                      pl.BlockSpec((None, tk, dq), lambda h, j, i: (h, j, 0)),
                      pl.BlockSpec((None, dq, tk), lambda h, j, i: (h, 0, j)),
                      pl.BlockSpec((None, tk, dv), lambda h, j, i: (h, j, 0)),
                      pl.BlockSpec((tq, dv), lambda h, j, i: (i, h)),
                      pl.BlockSpec((None, 1, tq), lambda h, j, i: (h, 0, i)),
                      pl.BlockSpec((None, 1, tq), lambda h, j, i: (h, 0, i))],
            out_specs=[pl.BlockSpec((None, dq, n), lambda h, j, i: (h, 0, 0)),
                       pl.BlockSpec((None, tk, dq), lambda h, j, i: (h, j, 0)),
                       pl.BlockSpec((None, tk, dv), lambda h, j, i: (h, j, 0))],
            out_shape=[jax.ShapeDtypeStruct((hh, dq, n), F32), jax.ShapeDtypeStruct((hh, ll, dq), BF16),
                       jax.ShapeDtypeStruct((hh, ll, dv), BF16)],
            scratch_shapes=[pltpu.VMEM((tk, dq), F32), pltpu.VMEM((tk, dv), F32)],
            compiler_params=_params(),
        )(q, k, kt, v, do, lse, delta)

    return fwd_call, delta_call, bwd_call


def _loss_tile(x, g, tgt):
    r = lax.rsqrt(jnp.mean(x * x, axis=-1, keepdims=True) + RMS_EPS)
    err = x * r * g - tgt
    per_tok = jnp.mean(err * err, axis=-1, keepdims=True)
    return 0.5 * jnp.sum(per_tok, axis=0, keepdims=True)


def _make_final_loss(tag):
    def fwd_call(x, g, tgt):
        t, d = x.shape
        tm = _pick(t, 512, 16)

        def body(x_ref, g_ref, t_ref, l_ref):
            l_ref[...] = jnp.broadcast_to(_loss_tile(x_ref[...], g_ref[...], t_ref[...]), (1, LANE))

        parts = pl.pallas_call(
            body, name=tag + "_fwd", grid=(t // tm,),
            in_specs=[_row_spec(tm, d), _vec_spec(d), _row_spec(tm, d)],
            out_specs=pl.BlockSpec((None, 1, LANE), lambda i: (i, 0, 0)),
            out_shape=jax.ShapeDtypeStruct((t // tm, 1, LANE), F32), compiler_params=_params(),
        )(x, g, tgt)
        return jnp.sum(parts[:, 0, 0])

    def bwd_call(x, g, tgt, dl):
        t, d = x.shape
        tm = _pick(t, 256, 16)

        def body(x_ref, g_ref, t_ref, dl_ref, dx_ref, dg_ref):
            _, vjp = jax.vjp(_loss_tile, x_ref[...], g_ref[...], t_ref[...])
            dx, dg, _ = vjp(dl_ref[...])
            dx_ref[...] = dx

            @pl.when(pl.program_id(0) == 0)
            def _():
                dg_ref[...] = jnp.zeros_like(dg_ref)

            dg_ref[...] += dg

        return pl.pallas_call(
            body, name=tag + "_bwd", grid=(t // tm,),
            in_specs=[_row_spec(tm, d), _vec_spec(d), _row_spec(tm, d), pl.BlockSpec((1, 1), lambda i: (0, 0))],
            out_specs=[_row_spec(tm, d), _vec_spec(d)],
            out_shape=[jax.ShapeDtypeStruct((t, d), F32), jax.ShapeDtypeStruct((1, d), F32)],
            compiler_params=_params(),
        )(x, g, tgt, dl)

    @jax.custom_vjp
    def final_loss(x, g, tgt):
        return fwd_call(x, g, tgt)

    def fwd(x, g, tgt):
        return fwd_call(x, g, tgt), (x, g, tgt)

    def bwd(res, dl):
        x, g, tgt = res
        dx, dg = bwd_call(x, g, tgt, dl.reshape(1, 1).astype(F32))
        return dx, dg, jnp.zeros_like(tgt)

    final_loss.defvjp(fwd, bwd)
    return final_loss


def _exchange(arrays, gather, name):
    n = len(arrays)

    def body(*refs):
        ins, outs = refs[:n], refs[n:2 * n]
        send_sems, recv_sems, local_sems = refs[2 * n:]
        me = 4 * lax.axis_index("x") + 2 * lax.axis_index("y") + lax.axis_index("c")

        def remote(a, d, wait_side=False):
            peer = (me + d) % N_DEV
            origin = (me + N_DEV - d) % N_DEV
            src = ins[a] if gather else ins[a].at[peer]
            dst = outs[a].at[origin if wait_side else me]
            return pltpu.make_async_remote_copy(
                src_ref=src, dst_ref=dst, send_sem=send_sems.at[a, d - 1], recv_sem=recv_sems.at[a, d - 1],
                device_id=(peer // 4, (peer // 2) % 2, peer % 2), device_id_type=pl.DeviceIdType.MESH)

        def local(a):
            src = ins[a] if gather else ins[a].at[me]
            return pltpu.make_async_copy(src, outs[a].at[me], local_sems.at[a])

        for a in range(n):
            for d in range(1, N_DEV):
                remote(a, d).start()
            local(a).start()
        for a in range(n):
            local(a).wait()
            for d in range(1, N_DEV):
                remote(a, d, wait_side=True).wait_recv()
                remote(a, d).wait_send()

    out_shape = []
    for arr in arrays:
        shape = (N_DEV,) + arr.shape if gather else arr.shape
        out_shape.append(jax.ShapeDtypeStruct(shape, arr.dtype))
    any_spec = pl.BlockSpec(memory_space=pl.ANY)
    return pl.pallas_call(
        body, name=name, in_specs=[any_spec] * n, out_specs=[any_spec] * n, out_shape=out_shape,
        scratch_shapes=[pltpu.SemaphoreType.DMA((n, N_DEV - 1)), pltpu.SemaphoreType.DMA((n, N_DEV - 1)),
                        pltpu.SemaphoreType.DMA((n,))],
        compiler_params=pltpu.CompilerParams(has_side_effects=True),
    )(*arrays)


def _split_copy(ins, lands, send_sems, recv_sems, a, d, gather, wait_side):
    me = 4 * lax.axis_index("x") + 2 * lax.axis_index("y") + lax.axis_index("c")
    peer = (me + d) % N_DEV
    origin = (me + N_DEV - d) % N_DEV
    return pltpu.make_async_remote_copy(
        src_ref=ins[a] if gather else ins[a].at[peer], dst_ref=lands[a].at[origin if wait_side else me],
        send_sem=send_sems.at[a * (N_DEV - 1) + d - 1], recv_sem=recv_sems.at[a * (N_DEV - 1) + d - 1],
        device_id=(peer // 4, (peer // 2) % 2, peer % 2), device_id_type=pl.DeviceIdType.MESH)


def _exchange_start(srcs, lands, after, gather, name):
    n = len(srcs)

    def body(*refs):
        ins, lnd = refs[:n], refs[n:2 * n]
        send_sems, recv_sems = refs[2 * n + 1], refs[2 * n + 2]
        for a in range(n):
            for d in range(1, N_DEV):
                _split_copy(ins, lnd, send_sems, recv_sems, a, d, gather, False).start()

    hbm = pl.BlockSpec(memory_space=pltpu.HBM)
    sem = pl.BlockSpec(memory_space=pltpu.SEMAPHORE)
    bufs = [pltpu.with_memory_space_constraint(t, pltpu.HBM) for t in list(srcs) + list(lands) + [after]]
    res = pl.pallas_call(
        body, name=name,
        in_specs=[hbm] * (2 * n + 1), out_specs=[sem, sem] + [hbm] * (2 * n + 1),
        out_shape=[pltpu.SemaphoreType.DMA((n * (N_DEV - 1),)), pltpu.SemaphoreType.DMA((n * (N_DEV - 1),))]
        + [pltpu.HBM(t.shape, t.dtype) for t in bufs],
        input_output_aliases={i: 2 + i for i in range(2 * n + 1)},
        compiler_params=pltpu.CompilerParams(has_side_effects=pltpu.SideEffectType.DATAFLOW_SIDE_EFFECTING),
    )(*bufs)
    return res[0], res[1], res[2:2 + n], res[2 + n:2 + 2 * n], res[-1]


def _exchange_wait(send_sems, recv_sems, srcs, lands, after, gather, name):
    n = len(srcs)

    def body(*refs):
        ins, lnd = refs[:n], refs[n:2 * n]
        send_sems_ref, recv_sems_ref = refs[2 * n], refs[2 * n + 1]
        for a in range(n):
            for d in range(1, N_DEV):
                _split_copy(ins, lnd, send_sems_ref, recv_sems_ref, a, d, gather, False).wait_send()
                _split_copy(ins, lnd, send_sems_ref, recv_sems_ref, a, d, gather, True).wait_recv()

    hbm = pl.BlockSpec(memory_space=pltpu.HBM)
    sem = pl.BlockSpec(memory_space=pltpu.SEMAPHORE)
    bufs = list(srcs) + list(lands)
    res = pl.pallas_call(
        body, name=name,
        in_specs=[hbm] * (2 * n) + [sem, sem, pl.BlockSpec(memory_space=pl.ANY)],
        out_specs=[hbm] * (2 * n),
        out_shape=[pltpu.HBM(t.shape, t.dtype) for t in bufs],
        input_output_aliases={i: i for i in range(2 * n)},
        compiler_params=pltpu.CompilerParams(has_side_effects=pltpu.SideEffectType.DATAFLOW_SIDE_EFFECTING),
    )(*bufs, send_sems, recv_sems, after)
    return res[n:]


def _own_slot(block, me):
    empty = lax.empty((N_DEV,) + block.shape, block.dtype)
    return lax.dynamic_update_slice(empty, block[None], (me,) + (0,) * block.ndim)


def _coords():
    return lax.axis_index("x"), lax.axis_index("y"), lax.axis_index("c")


def _other_chips(x, y):
    return [(1 - x, y), (x, 1 - y), (1 - x, 1 - y)]


def _gather_two_level(arrays, name):
    n = len(arrays)

    def body(*refs):
        ins, outs = refs[:n], refs[n:2 * n]
        send_sems, recv_sems, local_sems = refs[2 * n:]
        x, y, c = _coords()
        me, sib = (x, y, c), (x, y, 1 - c)
        chips = _other_chips(x, y)

        def copy(a, k, block, to, from_input=False):
            slot = 4 * block[0] + 2 * block[1] + block[2]
            return pltpu.make_async_remote_copy(
                src_ref=ins[a] if from_input else outs[a].at[slot], dst_ref=outs[a].at[slot],
                send_sem=send_sems.at[a, k], recv_sem=recv_sems.at[a, k],
                device_id=to, device_id_type=pl.DeviceIdType.MESH)

        def local(a):
            return pltpu.make_async_copy(ins[a], outs[a].at[4 * x + 2 * y + c], local_sems.at[a])

        for a in range(n):
            for j, chip in enumerate(chips):
                copy(a, 1 + j, me, (*chip, c), True).start()
            copy(a, 0, me, sib, True).start()
            local(a).start()
        for a in range(n):
            for j, chip in enumerate(chips):
                copy(a, 1 + j, (*chip, c), me).wait_recv()
                copy(a, 4 + j, (*chip, c), sib).start()
        for a in range(n):
            copy(a, 0, sib, me).wait_recv()
            for j, chip in enumerate(chips):
                copy(a, 4 + j, (*chip, 1 - c), me).wait_recv()
            for k in range(N_DEV - 1):
                copy(a, k, me, sib, True).wait_send()
            local(a).wait()

    any_spec = pl.BlockSpec(memory_space=pl.ANY)
    return pl.pallas_call(
        body, name=name, in_specs=[any_spec] * n, out_specs=[any_spec] * n,
        out_shape=[jax.ShapeDtypeStruct((N_DEV,) + arr.shape, arr.dtype) for arr in arrays],
        scratch_shapes=[pltpu.SemaphoreType.DMA((n, N_DEV - 1)), pltpu.SemaphoreType.DMA((n, N_DEV - 1)),
                        pltpu.SemaphoreType.DMA((n,))],
        compiler_params=pltpu.CompilerParams(has_side_effects=True),
    )(*arrays)


def _swap_sibling(arrays, name):
    n = len(arrays)
    n_chip = N_DEV // 2

    def body(*refs):
        ins, outs = refs[:n], refs[n:2 * n]
        send_sems, recv_sems = refs[2 * n:]
        x, y, c = _coords()

        def copy(a, q):
            return pltpu.make_async_remote_copy(
                src_ref=ins[a].at[2 * q + (1 - c)], dst_ref=outs[a].at[q],
                send_sem=send_sems.at[a, q], recv_sem=recv_sems.at[a, q],
                device_id=(x, y, 1 - c), device_id_type=pl.DeviceIdType.MESH)

        for a in range(n):
            for q in range(n_chip):
                copy(a, q).start()
        for a in range(n):
            for q in range(n_chip):
                copy(a, q).wait_recv()
                copy(a, q).wait_send()

    any_spec = pl.BlockSpec(memory_space=pl.ANY)
    return pl.pallas_call(
        body, name=name, in_specs=[any_spec] * n, out_specs=[any_spec] * n,
        out_shape=[jax.ShapeDtypeStruct((n_chip,) + arr.shape[1:], arr.dtype) for arr in arrays],
        scratch_shapes=[pltpu.SemaphoreType.DMA((n, n_chip)), pltpu.SemaphoreType.DMA((n, n_chip))],
        compiler_params=pltpu.CompilerParams(has_side_effects=True),
    )(*arrays)


def _scatter_chips(arrays, name):
    n = len(arrays)
    n_chip = N_DEV // 2

    def body(*refs):
        ins, outs = refs[:n], refs[n:2 * n]
        send_sems, recv_sems, local_sems = refs[2 * n:]
        x, y, c = _coords()
        q_me = 2 * x + y
        chips = _other_chips(x, y)

        def copy(a, j, wait_side=False):
            q_peer = 2 * chips[j][0] + chips[j][1]
            return pltpu.make_async_remote_copy(
                src_ref=ins[a].at[q_peer], dst_ref=outs[a].at[q_peer if wait_side else q_me],
                send_sem=send_sems.at[a, j], recv_sem=recv_sems.at[a, j],
                device_id=(*chips[j], c), device_id_type=pl.DeviceIdType.MESH)

        def local(a):
            return pltpu.make_async_copy(ins[a].at[q_me], outs[a].at[q_me], local_sems.at[a])

        for a in range(n):
            for j in range(n_chip - 1):
                copy(a, j).start()
            local(a).start()
        for a in range(n):
            local(a).wait()
            for j in range(n_chip - 1):
                copy(a, j, wait_side=True).wait_recv()
                copy(a, j).wait_send()

    any_spec = pl.BlockSpec(memory_space=pl.ANY)
    return pl.pallas_call(
        body, name=name, in_specs=[any_spec] * n, out_specs=[any_spec] * n,
        out_shape=[jax.ShapeDtypeStruct(arr.shape, arr.dtype) for arr in arrays],
        scratch_shapes=[pltpu.SemaphoreType.DMA((n, n_chip - 1)), pltpu.SemaphoreType.DMA((n, n_chip - 1)),
                        pltpu.SemaphoreType.DMA((n,))],
        compiler_params=pltpu.CompilerParams(has_side_effects=True),
    )(*arrays)


def _pair_add(full, theirs, core, name):
    n_chip, r, cn = theirs.shape
    tr = _pick(r, max(16, (2 * 1024 * 1024) // (4 * cn) // 16 * 16), 16)

    def body(core_ref, mine_ref, theirs_ref, o_ref):
        o_ref[...] = (mine_ref[...].astype(F32) + theirs_ref[...].astype(F32)).astype(BF16)

    tile = pl.BlockSpec((None, tr, cn), lambda q, i, core_ref: (q, i, 0))
    return pl.pallas_call(
        body, name=name,
        grid_spec=pltpu.PrefetchScalarGridSpec(
            num_scalar_prefetch=1, grid=(n_chip, r // tr),
            in_specs=[pl.BlockSpec((None, tr, cn), lambda q, i, core_ref: (2 * q + core_ref[0], i, 0)), tile],
            out_specs=tile),
        out_shape=jax.ShapeDtypeStruct(theirs.shape, BF16), compiler_params=_params(),
    )(core, full, theirs)


def _make_gather_op(tag):
    @jax.custom_vjp
    def gather_op(xl):
        return _exchange([xl], True, tag + "_gather")[0]

    def fwd(xl):
        return gather_op(xl), None

    def bwd(_, g):
        return (jnp.sum(_exchange([g], False, tag + "_scatter")[0], axis=0),)

    gather_op.defvjp(fwd, bwd)
    return gather_op


def _adamw(gstack, w, m, v, name):
    s, r, cn = gstack.shape
    tr = _pick(r, max(8, (2 * 1024 * 1024) // (4 * cn) // 8 * 8), 8)
    c1 = 1.0 - ADAM_B1 ** ADAM_STEP
    c2 = 1.0 - ADAM_B2 ** ADAM_STEP

    def body(g_ref, w_ref, m_ref, v_ref, go_ref, d_ref, mo_ref, vo_ref):
        g = g_ref[0].astype(F32)
        for q in range(1, s):
            g = g + g_ref[q].astype(F32)
        m_new = ADAM_B1 * m_ref[...] + (1.0 - ADAM_B1) * g
        v_new = ADAM_B2 * v_ref[...] + (1.0 - ADAM_B2) * (g * g)
        go_ref[...] = g
        mo_ref[...] = m_new
        vo_ref[...] = v_new
        d_ref[...] = -ADAM_LR * ((m_new / c1) / (jnp.sqrt(v_new / c2) + ADAM_EPS) + ADAM_WD * w_ref[...])

    tile = pl.BlockSpec((tr, cn), lambda i: (i, 0))
    out = jax.ShapeDtypeStruct((r, cn), F32)
    return pl.pallas_call(
        body, name=name, grid=(r // tr,),
        in_specs=[pl.BlockSpec((s, tr, cn), lambda i: (0, i, 0)), tile, tile, tile],
        out_specs=[tile, tile, tile, tile], out_shape=[out, out, out, out],
        compiler_params=_params(),
    )(gstack, w, m, v)


def _cols_from_stack(w):
    return jnp.swapaxes(w, 0, 1).reshape(w.shape[1], N_DEV * w.shape[2])


def _stage_a(p, ctx, silu_c_all, me):
    x = p["x"]
    d = x.shape[1]
    n_a = p["ada_w"].shape[1]

    a_in = jnp.concatenate([silu_c_all, jax.nn.silu(p["c_ctx"])[None, :], jnp.zeros((7, d), F32)], axis=0)
    b_loc = lax.dynamic_slice(p["ada_b"], (0, me * n_a), (1, n_a))
    r_loc = _make_small_mm("ada")(a_in, p["ada_w"]) + b_loc
    r_full = _make_gather_op("ada")(r_loc)
    m_lat = lax.dynamic_index_in_dim(r_full, me, axis=1, keepdims=False).reshape(N_MOD, 1, d)
    m_ctx = r_full[:, N_DEV, :].reshape(N_MOD, 1, d)

    x1 = _make_ffn_block("ffn1")(x, p["norm1_g"], m_lat[0], m_lat[1], m_lat[2], p["ffn1_w_in"], p["ffn1_w_out"])
    c1 = _make_ffn_block("ffn1c")(ctx, p["norm1_g"], m_ctx[0], m_ctx[1], m_ctx[2], p["ffn1_w_in"], p["ffn1_w_out"])
    return x1, c1, m_lat, m_ctx


def _stage_b(p, x1, c1, m_lat, m_ctx):
    n_lat, d = x1.shape
    n_ctx = c1.shape[0]
    w_mix = jnp.pad(_cols_from_stack(p["mix_w_in"]), ((0, 0), (0, MIX_IN_PAD - MIX_IN)))
    proj, x1 = _make_norm_proj_carry("mix")(x1, p["norm2_g"], m_lat[3], m_lat[4], w_mix)
    proj_c = _make_norm_proj("mixc")(c1, p["norm2_g"], m_ctx[3], m_ctx[4], w_mix)
    widths = SPLITS[:6] + (LANE,)
    rq, rk, rv, rg, cq, ckv, kr = _make_split("mixsplit", widths, MIX_IN_PAD)(proj)
    _, crk, crv, _, _, cckv, ckr = _make_split("mixsplitc", widths, MIX_IN_PAD)(proj_c)

    zq = jnp.zeros((1, MLA_Q_RANK), F32)
    zkv = jnp.zeros((1, MLA_KV_RANK), F32)
    w_uq3 = _cols_from_stack(p["mla_w_uq"]).reshape(MLA_Q_RANK, MLA_HEADS, MLA_NOPE + MLA_ROPE)
    w_uq = jnp.concatenate([w_uq3[:, :, :MLA_NOPE].reshape(MLA_Q_RANK, -1),
                            w_uq3[:, :, MLA_NOPE:].reshape(MLA_Q_RANK, -1)], axis=1)
    w_ukv = _cols_from_stack(p["mla_w_ukv"])
    q = _make_norm_proj("uq")(cq, p["mla_q_norm_g"], zq, zq, w_uq)
    kv = _make_norm_proj("ukv")(ckv, p["mla_kv_norm_g"], zkv, zkv, w_ukv)
    kv_c = _make_norm_proj("ukvc")(cckv, p["mla_kv_norm_g"], zkv, zkv, w_ukv)

    lg_f = jax.nn.log_sigmoid(p["ret_decay_fwd"][0])
    lg_b = jax.nn.log_sigmoid(p["ret_decay_bwd"][0])
    lat_f, lat_b, ctx_f, ctx_b = _make_ret_pack("retpack", n_lat, n_ctx)(rq, rk, rv, crk, crv)
    assert n_ctx == RET_CHUNK, "the context prefix is one retention chunk"

    def lanes(lg):
        return jnp.broadcast_to(lg[:, None, None], (RET_HEADS, 1, LANE))

    s0_f = _make_ctx_state("retcf", False)(*ctx_f, lanes(lg_f))
    s0_b = _make_ctx_state("retcb", True)(*ctx_b, lanes(lg_b))
    y_f = _make_ret_dir("retf", False)(*lat_f, lanes(lg_f), s0_f)
    y_b = _make_ret_dir("retb", True)(*lat_b, lanes(lg_b), s0_b)
    ret_o = _make_ret_out("reto")(y_f, y_b, rg)

    mla_o = _make_mla("mla", n_lat, n_ctx)(q, kv, kr, kv_c, ckr)

    w_mo = p["mix_w_out"].reshape(-1, d)
    return _make_res_proj("mixo")(jnp.concatenate([ret_o, mla_o], axis=-1), w_mo, x1, m_lat[5])


def _stage_c(p, x2, m_lat, tgt):
    x3 = _make_ffn_block("ffn2")(x2, p["norm3_g"], m_lat[6], m_lat[7], m_lat[8], p["ffn2_w_in"], p["ffn2_w_out"])
    return _make_final_loss("loss")(x3, p["final_norm_g"], tgt)


FIRST = ("ffn1_w_in", "ffn1_w_out")
MID = ("mix_w_in", "mla_w_uq", "mla_w_ukv", "mix_w_out")
LAST = ("ffn2_w_in", "ffn2_w_out")
BIG = FIRST + MID + LAST
SMALL = ("c_ctx", "ada_b", "norm1_g", "norm2_g", "ret_decay_fwd", "ret_decay_bwd", "mla_q_norm_g",
         "mla_kv_norm_g", "norm3_g", "final_norm_g")
WEIGHTS = ("c_ctx", "ada_w", "ada_b", "norm1_g", "ffn1_w_in", "ffn1_w_out", "norm2_g", "mix_w_in", "ret_decay_fwd",
           "ret_decay_bwd", "mla_q_norm_g", "mla_w_uq", "mla_kv_norm_g", "mla_w_ukv", "mix_w_out", "norm3_g",
           "ffn2_w_in", "ffn2_w_out", "final_norm_g")


def _pack(parts):
    flat = jnp.concatenate([t.reshape(-1) for t in parts])
    pad = (-flat.shape[0]) % LANE
    return jnp.pad(flat, (0, pad)).reshape(1, -1)


def _unpack(flat, like):
    out, off = [], 0
    for t in like:
        out.append(flat[0, off:off + t.size].reshape(t.shape))
        off += t.size
    return out


def kernel(x, c, ctx, c_ctx, ada_w, ada_b, norm1_g, ffn1_w_in, ffn1_w_out, norm2_g, mix_w_in, ret_decay_fwd, ret_decay_bwd, mla_q_norm_g, mla_w_uq, mla_kv_norm_g, mla_w_ukv, mix_w_out, norm3_g, ffn2_w_in, ffn2_w_out, final_norm_g, loss_target, m_c_ctx, m_ada_w, m_ada_b, m_norm1_g, m_ffn1_w_in, m_ffn1_w_out, m_norm2_g, m_mix_w_in, m_ret_decay_fwd, m_ret_decay_bwd, m_mla_q_norm_g, m_mla_w_uq, m_mla_kv_norm_g, m_mla_w_ukv, m_mix_w_out, m_norm3_g, m_ffn2_w_in, m_ffn2_w_out, m_final_norm_g, v_c_ctx, v_ada_w, v_ada_b, v_norm1_g, v_ffn1_w_in, v_ffn1_w_out, v_norm2_g, v_mix_w_in, v_ret_decay_fwd, v_ret_decay_bwd, v_mla_q_norm_g, v_mla_w_uq, v_mla_kv_norm_g, v_mla_w_ukv, v_mix_w_out, v_norm3_g, v_ffn2_w_in, v_ffn2_w_out, v_final_norm_g):
    w = dict(c_ctx=c_ctx, ada_w=ada_w, ada_b=ada_b, norm1_g=norm1_g, ffn1_w_in=ffn1_w_in, ffn1_w_out=ffn1_w_out,
             norm2_g=norm2_g, mix_w_in=mix_w_in, ret_decay_fwd=ret_decay_fwd, ret_decay_bwd=ret_decay_bwd,
             mla_q_norm_g=mla_q_norm_g, mla_w_uq=mla_w_uq, mla_kv_norm_g=mla_kv_norm_g, mla_w_ukv=mla_w_ukv,
             mix_w_out=mix_w_out, norm3_g=norm3_g, ffn2_w_in=ffn2_w_in, ffn2_w_out=ffn2_w_out,
             final_norm_g=final_norm_g)
    mom = dict(c_ctx=m_c_ctx, ada_w=m_ada_w, ada_b=m_ada_b, norm1_g=m_norm1_g, ffn1_w_in=m_ffn1_w_in,
               ffn1_w_out=m_ffn1_w_out, norm2_g=m_norm2_g, mix_w_in=m_mix_w_in, ret_decay_fwd=m_ret_decay_fwd,
               ret_decay_bwd=m_ret_decay_bwd, mla_q_norm_g=m_mla_q_norm_g, mla_w_uq=m_mla_w_uq,
               mla_kv_norm_g=m_mla_kv_norm_g, mla_w_ukv=m_mla_w_ukv, mix_w_out=m_mix_w_out, norm3_g=m_norm3_g,
               ffn2_w_in=m_ffn2_w_in, ffn2_w_out=m_ffn2_w_out, final_norm_g=m_final_norm_g)
    var = dict(c_ctx=v_c_ctx, ada_w=v_ada_w, ada_b=v_ada_b, norm1_g=v_norm1_g, ffn1_w_in=v_ffn1_w_in,
               ffn1_w_out=v_ffn1_w_out, norm2_g=v_norm2_g, mix_w_in=v_mix_w_in, ret_decay_fwd=v_ret_decay_fwd,
               ret_decay_bwd=v_ret_decay_bwd, mla_q_norm_g=v_mla_q_norm_g, mla_w_uq=v_mla_w_uq,
               mla_kv_norm_g=v_mla_kv_norm_g, mla_w_ukv=v_mla_w_ukv, mix_w_out=v_mix_w_out, norm3_g=v_norm3_g,
               ffn2_w_in=v_ffn2_w_in, ffn2_w_out=v_ffn2_w_out, final_norm_g=v_final_norm_g)
    me = 4 * lax.axis_index("x") + 2 * lax.axis_index("y") + lax.axis_index("c")

    shard = {k: w[k][0].astype(BF16) for k in BIG}
    first = _gather_two_level([shard[k] for k in FIRST] + [jax.nn.silu(c)], "weights_gather")
    silu_c_all = first[-1][:, 0, :]
    mid_start = _exchange_start([shard[k] for k in MID], [_own_slot(shard[k], me) for k in MID], first[0], True,
                                "mixer_weights_start")
    last_start = _exchange_start([shard[k] for k in LAST], [_own_slot(shard[k], me) for k in LAST], mid_start[4],
                                 True, "ffn2_weights_start")

    pa = dict(zip(FIRST, (last_start[4], first[1])), x=x[0], ada_w=ada_w[0], c_ctx=c_ctx, ada_b=ada_b,
              norm1_g=norm1_g)
    (x1, c1, m_lat, m_ctx), vjp_a = jax.vjp(lambda q: _stage_a(q, ctx[0], silu_c_all, me), pa)

    mid = _exchange_wait(mid_start[0], mid_start[1], mid_start[2], mid_start[3], x1, True, "mixer_weights_wait")
    pb = dict(zip(MID, mid))
    for k in ("norm2_g", "mla_q_norm_g", "mla_kv_norm_g", "ret_decay_fwd", "ret_decay_bwd"):
        pb[k] = w[k]
    x2, vjp_b = jax.vjp(_stage_b, pb, x1, c1, m_lat, m_ctx)

    last = _exchange_wait(last_start[0], last_start[1], last_start[2], last_start[3], x2, True, "ffn2_weights_wait")
    pc = dict(zip(LAST, last), norm3_g=norm3_g, final_norm_g=final_norm_g[None, :])
    loss_local, vjp_c = jax.vjp(lambda q, t, m: _stage_c(q, t, m, loss_target[0]), pc, x2, m_lat)

    gc, dx2, dm_c = vjp_c(jnp.ones((), F32))
    last_scat = _exchange_start([gc[k] for k in LAST],
                                [_own_slot(lax.dynamic_index_in_dim(gc[k], me, 0, False), me) for k in LAST],
                                dx2, False, "ffn2_grads_start")
    gb, dx1, dc1, dm_b, dmc_b = vjp_b(last_scat[4])
    mid_scat = _exchange_start([gb[k] for k in MID],
                               [_own_slot(lax.dynamic_index_in_dim(gb[k], me, 0, False), me) for k in MID],
                               dx1, False, "mixer_grads_start")
    (ga,) = vjp_a((mid_scat[4], dc1, dm_b + dm_c, dmc_b))
    grads = {**ga, **gb, **gc}
    grads["final_norm_g"] = grads["final_norm_g"][0]

    core = lax.axis_index("c").astype(jnp.int32).reshape(1)
    full = [grads[k] for k in FIRST]
    theirs = _swap_sibling(full, "grads_swap")
    paired = [_pair_add(f, t, core, "grads_pair_" + k) for k, f, t in zip(FIRST, full, theirs)]
    exchanged = dict(zip(FIRST, _scatter_chips(paired, "grads_scatter")))
    exchanged.update(zip(LAST, _exchange_wait(last_scat[0], last_scat[1], last_scat[2], last_scat[3], grads["x"],
                                              False, "ffn2_grads_wait")))
    exchanged.update(zip(MID, _exchange_wait(mid_scat[0], mid_scat[1], mid_scat[2], mid_scat[3], grads["x"],
                                             False, "mixer_grads_wait")))
    zero1 = [jnp.zeros((1,), F32)]
    small_like = zero1 + [w[k] for k in SMALL]
    small_all = _exchange([_pack([loss_local.reshape(1)] + [grads[k] for k in SMALL])], True, "small_grads_gather")[0]
    loss = jnp.sum(small_all[:, 0, 0])

    out_g, out_d, out_m, out_v = {}, {}, {}, {}

    def update(name, gstack, shape2d):
        res = _adamw(gstack, w[name].reshape(shape2d), mom[name].reshape(shape2d), var[name].reshape(shape2d),
                     "adamw_" + name)
        out_g[name], out_d[name], out_m[name], out_v[name] = [t.reshape(w[name].shape) for t in res]

    for k in BIG:
        update(k, exchanged[k], exchanged[k].shape[1:])
    update("ada_w", grads["ada_w"][None], ada_w.shape[1:])
    res = _adamw(small_all, _pack(small_like), _pack(zero1 + [mom[k] for k in SMALL]),
                 _pack(zero1 + [var[k] for k in SMALL]), "adamw_small")
    for dst, flat in zip((out_g, out_d, out_m, out_v), res):
        for k, t in zip(SMALL, _unpack(flat, small_like)[1:]):
            dst[k] = t

    return (loss, grads["x"][None], *[out_g[k] for k in WEIGHTS], *[out_d[k] for k in WEIGHTS],
            *[out_m[k] for k in WEIGHTS], *[out_v[k] for k in WEIGHTS])
```

```python
import functools

import jax
import jax.numpy as jnp
from jax import lax
from jax.experimental import pallas as pl
from jax.experimental.pallas import tpu as pltpu

F32 = jnp.float32
BF16 = jnp.bfloat16

N_DEV = 8
MESH_AXES = ("x", "y", "c")

GRID_W = 64
N_MOD = 9
RET_HEADS = 8
RET_DK = 64
RET_DV = 128
RET_CHUNK = 256
RET_ROPE_BASE = 10000.0
MLA_HEADS = 8
MLA_Q_RANK = 512
MLA_KV_RANK = 256
MLA_NOPE = 128
MLA_ROPE = 64
MLA_V = 128
AXIAL_BASE = 10000.0
RMS_EPS = 1e-6
GN_EPS = 1e-5
SPLITS = (RET_HEADS * RET_DK, RET_HEADS * RET_DK, RET_HEADS * RET_DV, RET_HEADS * RET_DV,
          MLA_Q_RANK, MLA_KV_RANK, MLA_ROPE)
MIX_IN = sum(SPLITS)
MIX_IN_PAD = 4096

ADAM_LR = 0.001
ADAM_B1 = 0.9
ADAM_B2 = 0.999
ADAM_EPS = 1e-08
ADAM_WD = 0.01
ADAM_STEP = 10

LANE = 128
RET_DKP = LANE
VMEM_LIMIT_BYTES = 56 * 1024 * 1024

NN = ((1,), (0,))
NT = ((1,), (1,))
TN = ((0,), (0,))


def _pick(dim, target, align=LANE):
    t = min(dim, target)
    t -= t % align
    while t >= align:
        if dim % t == 0:
            return t
        t -= align
    return dim


def _params():
    return pltpu.CompilerParams(vmem_limit_bytes=VMEM_LIMIT_BYTES)


def _dot(a, b, dims):
    return lax.dot_general(a.astype(BF16), b.astype(BF16), (dims, ((), ())), preferred_element_type=F32)


def _mm_call(name, grid, ins, pairs, outs, acc_shapes, epilogue):
    n_in, n_out = len(ins), len(outs)
    k_axis = len(grid) - 1
    k_steps = grid[k_axis]

    def body(*refs):
        in_refs = refs[:n_in]
        out_refs = refs[n_in:n_in + n_out]
        accs = refs[n_in + n_out:]
        k = pl.program_id(k_axis)

        @pl.when(k == 0)
        def _():
            for acc in accs:
                acc[...] = jnp.zeros_like(acc)

        for ai, bi, dims, ci in pairs:
            accs[ci][...] += _dot(in_refs[ai][...], in_refs[bi][...], dims)

        @pl.when(k == k_steps - 1)
        def _():
            epilogue([acc[...] for acc in accs], in_refs, out_refs)

    res = pl.pallas_call(
        body, name=name, grid=grid,
        in_specs=[s for _, s in ins], out_specs=[s for _, s in outs],
        out_shape=[s for s, _ in outs],
        scratch_shapes=[pltpu.VMEM(s, F32) for s in acc_shapes],
        compiler_params=_params(),
    )(*[a for a, _ in ins])
    return res


def _matmul(a, b, mode, out_dtype, name, tm=1024, tn=1024, tk=512):
    if mode == "nn":
        (m, kd), n = a.shape, b.shape[1]
    elif mode == "nt":
        (m, kd), n = a.shape, b.shape[0]
    else:
        (kd, m), n = a.shape, b.shape[1]
    tm, tn = _pick(m, tm, 16), _pick(n, tn)
    tk = _pick(kd, tk) if mode != "tn" else _pick(kd, tk, 16)
    if mode == "nn":
        a_spec = pl.BlockSpec((tm, tk), lambda i, j, k: (i, k))
        b_spec = pl.BlockSpec((tk, tn), lambda i, j, k: (k, j))
        dims = NN
    elif mode == "nt":
        a_spec = pl.BlockSpec((tm, tk), lambda i, j, k: (i, k))
        b_spec = pl.BlockSpec((tn, tk), lambda i, j, k: (j, k))
        dims = NT
    else:
        a_spec = pl.BlockSpec((tk, tm), lambda i, j, k: (k, i))
        b_spec = pl.BlockSpec((tk, tn), lambda i, j, k: (k, j))
        dims = TN

    def epilogue(accs, in_refs, out_refs):
        out_refs[0][...] = accs[0].astype(out_dtype)

    return _mm_call(
        name, (m // tm, n // tn, kd // tk), [(a, a_spec), (b, b_spec)], [(0, 1, dims, 0)],
        [(jax.ShapeDtypeStruct((m, n), out_dtype), pl.BlockSpec((tm, tn), lambda i, j, k: (i, j)))],
        [(tm, tn)], epilogue)[0]


def _norm_mod_tile(x, ng, sc, sh):
    r = lax.rsqrt(jnp.mean(x * x, axis=-1, keepdims=True) + RMS_EPS)
    return (x * r * ng) * (1.0 + sc) + sh


def _row_spec(tm, d):
    return pl.BlockSpec((tm, d), lambda i: (i, 0))


def _vec_spec(d):
    return pl.BlockSpec((1, d), lambda i: (0, 0))


def _norm_mod_fwd(x, ng, sc, sh, name):
    t, d = x.shape
    tm = _pick(t, 512, 16)

    def body(x_ref, ng_ref, sc_ref, sh_ref, h_ref):
        h_ref[...] = _norm_mod_tile(x_ref[...], ng_ref[...], sc_ref[...], sh_ref[...]).astype(BF16)

    return pl.pallas_call(
        body, name=name, grid=(t // tm,),
        in_specs=[_row_spec(tm, d), _vec_spec(d), _vec_spec(d), _vec_spec(d)],
        out_specs=_row_spec(tm, d), out_shape=jax.ShapeDtypeStruct((t, d), BF16),
        compiler_params=_params(),
    )(x, ng, sc, sh)


def _norm_mod_bwd(x, ng, sc, sh, dh, dres, name):
    t, d = x.shape
    tm = _pick(t, 256, 16)
    has_res = dres is not None

    def body(*refs):
        if has_res:
            x_ref, ng_ref, sc_ref, sh_ref, dh_ref, dres_ref, dx_ref, dng_ref, dsc_ref, dsh_ref = refs
        else:
            x_ref, ng_ref, sc_ref, sh_ref, dh_ref, dx_ref, dng_ref, dsc_ref, dsh_ref = refs
        _, vjp = jax.vjp(_norm_mod_tile, x_ref[...], ng_ref[...], sc_ref[...], sh_ref[...])
        dx, dng, dsc, dsh = vjp(dh_ref[...].astype(F32))
        if has_res:
            dx = dx + dres_ref[...]
        dx_ref[...] = dx

        @pl.when(pl.program_id(0) == 0)
        def _():
            dng_ref[...] = jnp.zeros_like(dng_ref)
            dsc_ref[...] = jnp.zeros_like(dsc_ref)
            dsh_ref[...] = jnp.zeros_like(dsh_ref)

        dng_ref[...] += dng
        dsc_ref[...] += dsc
        dsh_ref[...] += dsh

    ins = [x, ng, sc, sh, dh] + ([dres] if has_res else [])
    in_specs = [_row_spec(tm, d), _vec_spec(d), _vec_spec(d), _vec_spec(d), _row_spec(tm, d)]
    in_specs += [_row_spec(tm, d)] if has_res else []
    vec = jax.ShapeDtypeStruct((1, d), F32)
    return pl.pallas_call(
        body, name=name, grid=(t // tm,), in_specs=in_specs,
        out_specs=[_row_spec(tm, d), _vec_spec(d), _vec_spec(d), _vec_spec(d)],
        out_shape=[jax.ShapeDtypeStruct((t, d), F32), vec, vec, vec],
        compiler_params=_params(),
    )(*ins)


def _res_mm_fwd(a, w, x, gate, coef, name):
    t, kd = a.shape
    d = w.shape[1]
    tm, tn, tk = _pick(t, 1024, 16), _pick(d, 1024), _pick(kd, 2816)

    def epilogue(accs, in_refs, out_refs):
        f = accs[0]
        out_refs[0][...] = in_refs[2][...] + (coef * in_refs[3][...]) * f
        out_refs[1][...] = f.astype(BF16)

    tile = pl.BlockSpec((tm, tn), lambda i, j, k: (i, j))
    return _mm_call(
        name, (t // tm, d // tn, kd // tk),
        [(a, pl.BlockSpec((tm, tk), lambda i, j, k: (i, k))), (w, pl.BlockSpec((tk, tn), lambda i, j, k: (k, j))),
         (x, tile), (gate, pl.BlockSpec((1, tn), lambda i, j, k: (0, j)))],
        [(0, 1, NN, 0)],
        [(jax.ShapeDtypeStruct((t, d), F32), tile), (jax.ShapeDtypeStruct((t, d), BF16), tile)],
        [(tm, tn)], epilogue)


def _gate_bwd(dxo, f, gate, coef, name):
    t, d = dxo.shape
    tm = _pick(t, 512, 16)

    def body(dxo_ref, f_ref, gate_ref, df_ref, dgate_ref):
        dxo_t = dxo_ref[...]
        df_ref[...] = ((coef * gate_ref[...]) * dxo_t).astype(BF16)

        @pl.when(pl.program_id(0) == 0)
        def _():
            dgate_ref[...] = jnp.zeros_like(dgate_ref)

        dgate_ref[...] += coef * jnp.sum(dxo_t * f_ref[...].astype(F32), axis=0, keepdims=True)

    return pl.pallas_call(
        body, name=name, grid=(t // tm,),
        in_specs=[_row_spec(tm, d), _row_spec(tm, d), _vec_spec(d)],
        out_specs=[_row_spec(tm, d), _vec_spec(d)],
        out_shape=[jax.ShapeDtypeStruct((t, d), BF16), jax.ShapeDtypeStruct((1, d), F32)],
        compiler_params=_params(),
    )(dxo, f, gate)


def _ffn_in_fwd(h, w_in, name):
    t, d = h.shape
    n = w_in.shape[2]
    half = N_DEV // 2
    f = half * n
    tm = _pick(t, 512, 16)

    def epilogue(accs, in_refs, out_refs):
        g, u = accs
        out_refs[0][...] = (g * jax.nn.sigmoid(g) * u).astype(BF16)
        out_refs[1][0] = g.astype(BF16)
        out_refs[1][1] = u.astype(BF16)

    return _mm_call(
        name, (half, t // tm, 1),
        [(h, pl.BlockSpec((tm, d), lambda j, i, k: (i, 0))),
         (w_in, pl.BlockSpec((None, d, n), lambda j, i, k: (j, 0, 0))),
         (w_in, pl.BlockSpec((None, d, n), lambda j, i, k: (j + half, 0, 0)))],
        [(0, 1, NN, 0), (0, 2, NN, 1)],
        [(jax.ShapeDtypeStruct((t, f), BF16), pl.BlockSpec((tm, n), lambda j, i, k: (i, j))),
         (jax.ShapeDtypeStruct((2, t, f), BF16), pl.BlockSpec((2, tm, n), lambda j, i, k: (0, i, j)))],
        [(tm, n), (tm, n)], epilogue)


def _ffn_da_bwd(df, w_out2d, gu, name):
    t, d = df.shape
    f = w_out2d.shape[0]
    half = N_DEV // 2
    n = f // half
    tm = _pick(t, 512, 16)
    step = 4 * LANE
    chunks = [(c, min(c + step, n)) for c in range(0, n, step)]

    def body(df_ref, w_ref, gu_ref, o_ref):
        df_t = df_ref[...]
        for c0, c1 in chunks:
            da = _dot(df_t, w_ref[c0:c1, :], NT)
            g = gu_ref[0, :, c0:c1].astype(F32)
            u = gu_ref[1, :, c0:c1].astype(F32)
            s = jax.nn.sigmoid(g)
            o_ref[0, :, c0:c1] = (da * u * (s * (1.0 + g * (1.0 - s)))).astype(BF16)
            o_ref[1, :, c0:c1] = (da * (g * s)).astype(BF16)

    gu_spec = pl.BlockSpec((2, tm, n), lambda j, i: (0, i, j))
    return pl.pallas_call(
        body, name=name, grid=(half, t // tm),
        in_specs=[pl.BlockSpec((tm, d), lambda j, i: (i, 0)), pl.BlockSpec((n, d), lambda j, i: (j, 0)), gu_spec],
        out_specs=gu_spec, out_shape=jax.ShapeDtypeStruct((2, t, f), BF16), compiler_params=_params(),
    )(df, w_out2d, gu)


def _ffn_dh_bwd(dgu, w_in, name):
    _, t, f = dgu.shape
    d, n = w_in.shape[1], w_in.shape[2]
    half = N_DEV // 2
    tm = _pick(t, 512, 16)

    def epilogue(accs, in_refs, out_refs):
        out_refs[0][...] = accs[0]

    return _mm_call(
        name, (t // tm, 1, half),
        [(dgu, pl.BlockSpec((None, tm, n), lambda i, j, k: (0, i, k))),
         (dgu, pl.BlockSpec((None, tm, n), lambda i, j, k: (1, i, k))),
         (w_in, pl.BlockSpec((None, d, n), lambda i, j, k: (k, 0, 0))),
         (w_in, pl.BlockSpec((None, d, n), lambda i, j, k: (k + half, 0, 0)))],
        [(0, 2, NT, 0), (1, 3, NT, 0)],
        [(jax.ShapeDtypeStruct((t, d), F32), pl.BlockSpec((tm, d), lambda i, j, k: (i, 0)))],
        [(tm, d)], epilogue)[0]


def _ffn_dwin_bwd(h, dgu, name):
    t, d = h.shape
    f = dgu.shape[2]
    half = N_DEV // 2
    n = f // half
    tk = _pick(t, 1024, 16)

    def epilogue(accs, in_refs, out_refs):
        out_refs[0][...] = accs[0].astype(BF16)

    return _mm_call(
        name, (N_DEV, 1, t // tk),
        [(h, pl.BlockSpec((tk, d), lambda j, i, k: (k, 0))),
         (dgu, pl.BlockSpec((None, tk, n), lambda j, i, k: (j // half, k, j % half)))],
        [(0, 1, TN, 0)],
        [(jax.ShapeDtypeStruct((N_DEV, d, n), BF16), pl.BlockSpec((None, d, n), lambda j, i, k: (j, 0, 0)))],
        [(d, n)], epilogue)[0]


def _ffn_parts(tag):
    def w2d(w_out):
        return w_out.reshape(w_out.shape[0] * w_out.shape[1], w_out.shape[2])

    def fwd_in(x, ng, sh, sc, w_in):
        h = _norm_mod_fwd(x, ng, sc, sh, tag + "_norm")
        a, gu = _ffn_in_fwd(h, w_in, tag + "_in")
        return h, a, gu

    def fwd_out(a, w_out, x, gate):
        return _res_mm_fwd(a, w2d(w_out), x, gate, 0.5, tag + "_out")

    def bwd_out(dxo, f1, gate, w_out, gu, a):
        df, dgate = _gate_bwd(dxo, f1, gate, 0.5, tag + "_dgate")
        dgu = _ffn_da_bwd(df, w2d(w_out), gu, tag + "_da")
        f = w_out.shape[0] * w_out.shape[1]
        dw_out = _matmul(a, df, "tn", BF16, tag + "_dwout", tm=_pick(f, 1408, 16), tn=2048, tk=1024)
        return dgate, dgu, dw_out.reshape(w_out.shape)

    def bwd_in(x, ng, sh, sc, w_in, h, dgu, dxo):
        dh = _ffn_dh_bwd(dgu, w_in, tag + "_dh")
        dw_in = _ffn_dwin_bwd(h, dgu, tag + "_dwin")
        dx, dng, dsc, dsh = _norm_mod_bwd(x, ng, sc, sh, dh, dxo, tag + "_dnorm")
        return dx, dng, dsh, dsc, dw_in

    return fwd_in, fwd_out, bwd_out, bwd_in


def _make_ffn_block(tag):
    fwd_in, fwd_out, bwd_out, bwd_in = _ffn_parts(tag)

    @jax.custom_vjp
    def ffn_block(x, ng, sh, sc, gate, w_in, w_out):
        return fwd(x, ng, sh, sc, gate, w_in, w_out)[0]

    def fwd(x, ng, sh, sc, gate, w_in, w_out):
        h, a, gu = fwd_in(x, ng, sh, sc, w_in)
        xo, f1 = fwd_out(a, w_out, x, gate)
        return xo, (x, ng, sh, sc, gate, w_in, w_out, h, a, gu, f1)

    def bwd(res, dxo):
        x, ng, sh, sc, gate, w_in, w_out, h, a, gu, f1 = res
        dgate, dgu, dw_out = bwd_out(dxo, f1, gate, w_out, gu, a)
        dx, dng, dsh, dsc, dw_in = bwd_in(x, ng, sh, sc, w_in, h, dgu, dxo)
        return dx, dng, dsh, dsc, dgate, dw_in, dw_out

    ffn_block.defvjp(fwd, bwd)
    return ffn_block


def _make_norm_proj(tag):
    @jax.custom_vjp
    def norm_proj(x, ng, sh, sc, w):
        return fwd(x, ng, sh, sc, w)[0]

    def fwd(x, ng, sh, sc, w):
        h = _norm_mod_fwd(x, ng, sc, sh, tag + "_norm")
        p = _matmul(h, w, "nn", F32, tag + "_mm", tm=1024, tn=1024, tk=w.shape[0])
        return p, (x, ng, sh, sc, w, h)

    def bwd(res, dp):
        x, ng, sh, sc, w, h = res
        dh = _matmul(dp, w, "nt", F32, tag + "_dh", tm=512, tn=w.shape[0], tk=2048)
        dw = _matmul(h, dp, "tn", BF16, tag + "_dw", tm=w.shape[0], tn=1024, tk=1024)
        dx, dng, dsc, dsh = _norm_mod_bwd(x, ng, sc, sh, dh, None, tag + "_dnorm")
        return dx, dng, dsh, dsc, dw

    norm_proj.defvjp(fwd, bwd)
    return norm_proj


def _make_norm_proj_carry(tag):
    @jax.custom_vjp
    def norm_proj(x, ng, sh, sc, w):
        return fwd(x, ng, sh, sc, w)[0]

    def fwd(x, ng, sh, sc, w):
        h = _norm_mod_fwd(x, ng, sc, sh, tag + "_norm")
        p = _matmul(h, w, "nn", F32, tag + "_mm", tm=1024, tn=1024, tk=w.shape[0])
        return (p, x), (x, ng, sh, sc, w, h)

    def bwd(res, cts):
        x, ng, sh, sc, w, h = res
        dp, dx_carry = cts
        dh = _matmul(dp, w, "nt", F32, tag + "_dh", tm=512, tn=w.shape[0], tk=2048)
        dw = _matmul(h, dp, "tn", BF16, tag + "_dw", tm=w.shape[0], tn=1024, tk=1024)
        dx, dng, dsc, dsh = _norm_mod_bwd(x, ng, sc, sh, dh, dx_carry, tag + "_dnorm")
        return dx, dng, dsh, dsc, dw

    norm_proj.defvjp(fwd, bwd)
    return norm_proj


def _make_split(tag, widths, total):
    offs = [sum(widths[:i]) for i in range(len(widths))]

    def concat_call(pieces):
        t = pieces[0].shape[0]
        tm = _pick(t, 256, 16)

        def body(*refs):
            o_ref = refs[-1]
            for ref, off, wd in zip(refs[:-1], offs, widths):
                o_ref[:, off:off + wd] = ref[...]
            end = offs[-1] + widths[-1]
            if end < total:
                o_ref[:, end:] = jnp.zeros((tm, total - end), F32)

        return pl.pallas_call(
            body, name=tag + "_concat", grid=(t // tm,),
            in_specs=[_row_spec(tm, wd) for wd in widths], out_specs=_row_spec(tm, total),
            out_shape=jax.ShapeDtypeStruct((t, total), F32), compiler_params=_params(),
        )(*pieces)

    @jax.custom_vjp
    def split(p):
        return tuple(p[:, off:off + wd] for off, wd in zip(offs, widths))

    def fwd(p):
        return split(p), None

    def bwd(_, cts):
        return (concat_call(list(cts)),)

    split.defvjp(fwd, bwd)
    return split


def _make_res_proj(tag):
    @jax.custom_vjp
    def res_proj(a, w, x, gate):
        return fwd(a, w, x, gate)[0]

    def fwd(a, w, x, gate):
        xo, f = _res_mm_fwd(a, w, x, gate, 1.0, tag + "_mm")
        return xo, (a, w, gate, f)

    def bwd(res, dxo):
        a, w, gate, f = res
        df, dgate = _gate_bwd(dxo, f, gate, 1.0, tag + "_dgate")
        da = _matmul(df, w, "nt", BF16, tag + "_da", tm=1024, tn=1024, tk=2048)
        dw = _matmul(a, df, "tn", BF16, tag + "_dw", tm=1024, tn=2048, tk=1024)
        return da, dw, dxo, dgate

    res_proj.defvjp(fwd, bwd)
    return res_proj


def _make_small_mm(tag):
    @jax.custom_vjp
    def small_mm(a, w):
        return _matmul(a, w, "nn", F32, tag + "_mm", tm=a.shape[0], tn=768, tk=w.shape[0])

    def fwd(a, w):
        return small_mm(a, w), (a, w)

    def bwd(res, dr):
        a, w = res
        da = _matmul(dr, w, "nt", F32, tag + "_da", tm=a.shape[0], tn=w.shape[0], tk=768)
        dw = _matmul(a, dr, "tn", F32, tag + "_dw", tm=1024, tn=768, tk=a.shape[0])
        return da, dw

    small_mm.defvjp(fwd, bwd)
    return small_mm


def _ret_chunk_terms(lg, c, reverse):
    row = lax.broadcasted_iota(jnp.int32, (c, c), 0).astype(F32)
    col = lax.broadcasted_iota(jnp.int32, (c, c), 1).astype(F32)
    pos = lax.broadcasted_iota(jnp.int32, (c, 1), 0).astype(F32)
    if reverse:
        diff = col - row
        mask = diff > 0.0
        e_exp = float(c) - pos
        f_exp = pos
    else:
        diff = row - col
        mask = diff >= 0.0
        e_exp = pos + 1.0
        f_exp = float(c - 1) - pos
    diffm = jnp.where(mask, diff, 0.0)
    dm = jnp.where(mask, jnp.exp(lg * diffm), 0.0)
    return diffm, dm, e_exp, jnp.exp(lg * e_exp), f_exp, jnp.exp(lg * f_exp)


def _lane0(val):
    lane = lax.broadcasted_iota(jnp.int32, (1, LANE), 1)
    return jnp.where(lane == 0, val, 0.0)


RET_HEAD_BLOCK = 4


def _make_ret_dir(tag, reverse):
    hb = RET_HEAD_BLOCK

    def heads_spec(nc, width, flip):
        if flip:
            return pl.BlockSpec((hb, RET_CHUNK, width), lambda h, t: (h, nc - 1 - t, 0))
        return pl.BlockSpec((hb, RET_CHUNK, width), lambda h, t: (h, t, 0))

    def state_spec(nc, flip):
        if flip:
            return pl.BlockSpec((hb, None, RET_DKP, RET_DV), lambda h, t: (h, nc - 1 - t, 0, 0))
        return pl.BlockSpec((hb, None, RET_DKP, RET_DV), lambda h, t: (h, t, 0, 0))

    lg_spec = pl.BlockSpec((hb, 1, LANE), lambda h, t: (h, 0, 0))
    s0_spec = pl.BlockSpec((hb, RET_DKP, RET_DV), lambda h, t: (h, 0, 0))

    def fwd_call(q, k, v, lgb, s0):
        hh, ll, _ = q.shape
        c = RET_CHUNK
        nc = ll // c

        def body(q_ref, k_ref, v_ref, lg_ref, s0_ref, y_ref, sall_ref, s_scr):
            @pl.when(pl.program_id(1) == 0)
            def _():
                s_scr[...] = s0_ref[...]

            for b in range(hb):
                lg = lg_ref[b][:, :1]
                _, dm, _, xi, _, zeta = _ret_chunk_terms(lg, c, reverse)
                q_t, k_t, v_t = q_ref[b], k_ref[b], v_ref[b]
                s = s_scr[b]
                p = _dot(q_t, k_t, NT) * dm
                y_ref[b] = _dot(p, v_t, NN) + _dot(q_t * xi, s, NN)
                sall_ref[b] = s
                s_scr[b] = jnp.exp(lg * float(c)) * s + _dot(k_t * zeta, v_t, TN)

        return pl.pallas_call(
            body, name=tag + "_fwd", grid=(hh // hb, nc),
            in_specs=[heads_spec(nc, RET_DKP,reverse), heads_spec(nc, RET_DKP,reverse),
                      heads_spec(nc, RET_DV, reverse), lg_spec, s0_spec],
            out_specs=[heads_spec(nc, RET_DV, reverse), state_spec(nc, reverse)],
            out_shape=[jax.ShapeDtypeStruct((hh, ll, RET_DV), F32),
                       jax.ShapeDtypeStruct((hh, nc, RET_DKP, RET_DV), F32)],
            scratch_shapes=[pltpu.VMEM((hb, RET_DKP, RET_DV), F32)],
            compiler_params=_params(),
        )(q, k, v, lgb, s0)

    def bwd_call(q, k, v, lgb, sall, dy):
        hh, ll, _ = q.shape
        c = RET_CHUNK
        nc = ll // c
        flip = not reverse

        def body(q_ref, k_ref, v_ref, lg_ref, sall_ref, dy_ref, dq_ref, dk_ref, dv_ref, dlg_ref, ds0_ref, ds_scr):
            @pl.when(pl.program_id(1) == 0)
            def _():
                ds_scr[...] = jnp.zeros_like(ds_scr)
                dlg_ref[...] = jnp.zeros_like(dlg_ref)

            def total(m):
                return jnp.sum(jnp.sum(m, axis=1, keepdims=True), axis=0, keepdims=True)

            for b in range(hb):
                lg = lg_ref[b][:, :1]
                diffm, dm, e_exp, xi, f_exp, zeta = _ret_chunk_terms(lg, c, reverse)
                q_t, k_t, v_t, dy_t = q_ref[b], k_ref[b], v_ref[b], dy_ref[b]
                s = sall_ref[b]
                dsn = ds_scr[b]
                a = _dot(q_t, k_t, NT)
                da = _dot(dy_t, v_t, NT) * dm
                g = _dot(dy_t, s, NT)
                hm = _dot(v_t, dsn, NT)
                dq_ref[b] = _dot(da, k_t, NN) + xi * g
                dk_ref[b] = _dot(da, q_t, TN) + zeta * hm
                dv_ref[b] = _dot(a * dm, dy_t, TN) + _dot(k_t * zeta, dsn, NN)
                gc = jnp.exp(lg * float(c))
                ds_scr[b] = gc * dsn + _dot(q_t * xi, dy_t, TN)
                dl = (total(da * a * diffm) + total(e_exp * xi * q_t * g)
                      + float(c) * gc * total(s * dsn) + total(f_exp * zeta * k_t * hm))
                dlg_ref[b] += _lane0(dl)

            @pl.when(pl.program_id(1) == nc - 1)
            def _():
                ds0_ref[...] = ds_scr[...]

        return pl.pallas_call(
            body, name=tag + "_bwd", grid=(hh // hb, nc),
            in_specs=[heads_spec(nc, RET_DKP,flip), heads_spec(nc, RET_DKP,flip), heads_spec(nc, RET_DV, flip),
                      lg_spec, state_spec(nc, flip), heads_spec(nc, RET_DV, flip)],
            out_specs=[heads_spec(nc, RET_DKP,flip), heads_spec(nc, RET_DKP,flip), heads_spec(nc, RET_DV, flip),
                       lg_spec, s0_spec],
            out_shape=[jax.ShapeDtypeStruct((hh, ll, RET_DKP), F32), jax.ShapeDtypeStruct((hh, ll, RET_DKP), F32),
                       jax.ShapeDtypeStruct((hh, ll, RET_DV), F32), jax.ShapeDtypeStruct((hh, 1, LANE), F32),
                       jax.ShapeDtypeStruct((hh, RET_DKP, RET_DV), F32)],
            scratch_shapes=[pltpu.VMEM((hb, RET_DKP, RET_DV), F32)],
            compiler_params=_params(),
        )(q, k, v, lgb, sall, dy)

    @jax.custom_vjp
    def ret_dir(q, k, v, lgb, s0):
        return fwd_call(q, k, v, lgb, s0)[0]

    def fwd(q, k, v, lgb, s0):
        y, sall = fwd_call(q, k, v, lgb, s0)
        return y, (q, k, v, lgb, sall)

    def bwd(res, dy):
        q, k, v, lgb, sall = res
        return tuple(bwd_call(q, k, v, lgb, sall, dy))

    ret_dir.defvjp(fwd, bwd)
    return ret_dir


def _make_ctx_state(tag, reverse):
    hb = RET_HEAD_BLOCK
    c = RET_CHUNK
    k_spec = pl.BlockSpec((hb, c, RET_DKP), lambda h: (h, 0, 0))
    v_spec = pl.BlockSpec((hb, c, RET_DV), lambda h: (h, 0, 0))
    lg_spec = pl.BlockSpec((hb, 1, LANE), lambda h: (h, 0, 0))
    s_spec = pl.BlockSpec((hb, RET_DKP, RET_DV), lambda h: (h, 0, 0))

    def fwd_call(k, v, lgb):
        hh = k.shape[0]

        def body(k_ref, v_ref, lg_ref, s_ref):
            for b in range(hb):
                _, _, _, _, _, zeta = _ret_chunk_terms(lg_ref[b][:, :1], c, reverse)
                s_ref[b] = _dot(k_ref[b] * zeta, v_ref[b], TN)

        return pl.pallas_call(
            body, name=tag + "_fwd", grid=(hh // hb,), in_specs=[k_spec, v_spec, lg_spec], out_specs=s_spec,
            out_shape=jax.ShapeDtypeStruct((hh, RET_DKP, RET_DV), F32), compiler_params=_params(),
        )(k, v, lgb)

    def bwd_call(k, v, lgb, ds):
        hh = k.shape[0]

        def body(k_ref, v_ref, lg_ref, ds_ref, dk_ref, dv_ref, dlg_ref):
            for b in range(hb):
                _, _, _, _, f_exp, zeta = _ret_chunk_terms(lg_ref[b][:, :1], c, reverse)
                k_t, v_t, ds = k_ref[b], v_ref[b], ds_ref[b]
                hm = _dot(v_t, ds, NT)
                dk_ref[b] = zeta * hm
                dv_ref[b] = _dot(k_t * zeta, ds, NN)
                tot = jnp.sum(jnp.sum(f_exp * zeta * k_t * hm, axis=1, keepdims=True), axis=0, keepdims=True)
                dlg_ref[b] = _lane0(tot)

        return pl.pallas_call(
            body, name=tag + "_bwd", grid=(hh // hb,), in_specs=[k_spec, v_spec, lg_spec, s_spec],
            out_specs=[k_spec, v_spec, lg_spec],
            out_shape=[jax.ShapeDtypeStruct(k.shape, F32), jax.ShapeDtypeStruct(v.shape, F32),
                       jax.ShapeDtypeStruct((hh, 1, LANE), F32)],
            compiler_params=_params(),
        )(k, v, lgb, ds)

    @jax.custom_vjp
    def ctx_state(k, v, lgb):
        return fwd_call(k, v, lgb)

    def fwd(k, v, lgb):
        return fwd_call(k, v, lgb), (k, v, lgb)

    def bwd(res, ds):
        return tuple(bwd_call(*res, ds))

    ctx_state.defvjp(fwd, bwd)
    return ctx_state


def _rope_tables_call(name, n, inv, shift, axial):
    tm = _pick(n, 1024, 8)
    inv_lane = jnp.tile(inv, LANE // inv.shape[0])[None, :]

    def body(inv_ref, cos_ref, s1_ref, s2_ref):
        t = lax.broadcasted_iota(jnp.int32, (tm, LANE), 0) + pl.program_id(0) * tm
        lane = lax.broadcasted_iota(jnp.int32, (tm, LANE), 1)
        if axial:
            pos = jnp.where(lane % (2 * MLA_ROPE // 2) < MLA_ROPE // 2, t // GRID_W, t % GRID_W)
        else:
            pos = t
        ang = pos.astype(F32) * inv_ref[...]
        sin = jnp.sin(ang)
        first = lane % (2 * shift) < shift
        cos_ref[...] = jnp.cos(ang)
        s1_ref[...] = jnp.where(first, -sin, 0.0)
        s2_ref[...] = jnp.where(first, 0.0, sin)

    tab = jax.ShapeDtypeStruct((n, LANE), F32)
    return tuple(pl.pallas_call(
        body, name=name, grid=(n // tm,), in_specs=[_vec_spec(LANE)], out_specs=[_row_spec(tm, LANE)] * 3,
        out_shape=[tab, tab, tab], compiler_params=_params(),
    )(inv_lane))


def _ret_tables(n_lat):
    inv = RET_ROPE_BASE ** (-jnp.arange(0, RET_DK, 2, dtype=F32) / RET_DK)
    return _rope_tables_call("ret_tables", n_lat, inv, RET_DK // 2, False)


def _make_ret_pack(tag, n_lat, n_ctx):
    hh = RET_HEADS
    tm = MLA_PACK_ROWS
    k_scale = RET_DK ** -0.5
    shift = RET_DK // 2
    tabs = _ret_tables(n_lat)

    def low_lanes():
        return lax.broadcasted_iota(jnp.int32, (1, LANE), 1) < RET_DK

    def rows(width):
        return pl.BlockSpec((tm, width), lambda i: (i, 0))

    def heads(width):
        return pl.BlockSpec((hh, tm, width), lambda i: (0, i, 0))

    def split_pairs(src_ref, dst_ref, scale, rope):
        keep = low_lanes()
        for j in range(hh // 2):
            blk = src_ref[:, LANE * j:LANE * (j + 1)]
            if scale != 1.0:
                blk = blk * scale
            if rope is not None:
                blk = _rope128(blk, *rope, shift=shift)
            dst_ref[2 * j] = jnp.where(keep, blk, 0.0)
            dst_ref[2 * j + 1] = jnp.where(keep, pltpu.roll(blk, RET_DK, 1), 0.0)

    def merge_pairs(src_refs, dst_ref, scale, rope):
        keep = low_lanes()
        for j in range(hh // 2):
            even = sum(r[2 * j] for r in src_refs)
            odd = sum(r[2 * j + 1] for r in src_refs)
            g = jnp.where(keep, even, pltpu.roll(odd, RET_DK, 1))
            if rope is not None:
                g = _rope128_t(g, *rope, shift=shift)
            dst_ref[:, LANE * j:LANE * (j + 1)] = g * scale if scale != 1.0 else g

    def pack_call(name, n, q, k, v, rope):
        with_q = q is not None

        def body(*refs):
            refs = list(refs)
            q_ref = refs.pop(0) if with_q else None
            k_ref, v_ref = refs.pop(0), refs.pop(0)
            tab = tuple(r[...] for r in refs[:3]) if rope else None
            outs = refs[3:] if rope else refs
            if with_q:
                split_pairs(q_ref, outs[0], 1.0, tab)
                outs = outs[1:]
            split_pairs(k_ref, outs[0], k_scale, tab)
            for h in range(hh):
                outs[1][h] = v_ref[:, RET_DV * h:RET_DV * (h + 1)]

        ins = ([q] if with_q else []) + [k, v] + (list(tabs) if rope else [])
        in_specs = ([rows(q.shape[1])] if with_q else []) + [rows(k.shape[1]), rows(v.shape[1])]
        in_specs += [rows(LANE)] * 3 if rope else []
        n_out = 3 if with_q else 2
        return pl.pallas_call(
            body, name=name, grid=(n // tm,), in_specs=in_specs,
            out_specs=[heads(RET_DKP)] * (n_out - 1) + [heads(RET_DV)],
            out_shape=[jax.ShapeDtypeStruct((hh, n, RET_DKP), F32)] * (n_out - 1)
            + [jax.ShapeDtypeStruct((hh, n, RET_DV), F32)],
            compiler_params=_params(),
        )(*ins)

    def unpack_call(name, n, dqs, dks, dvs, rope):
        with_q = len(dqs) > 0
        uses = len(dks)

        def body(*refs):
            refs = list(refs)
            dq_refs = [refs.pop(0) for _ in range(len(dqs))]
            dk_refs = [refs.pop(0) for _ in range(uses)]
            dv_refs = [refs.pop(0) for _ in range(uses)]
            tab = tuple(r[...] for r in refs[:3]) if rope else None
            outs = refs[3:] if rope else refs
            if with_q:
                merge_pairs(dq_refs, outs[0], 1.0, tab)
                outs = outs[1:]
            merge_pairs(dk_refs, outs[0], k_scale, tab)
            for h in range(hh):
                outs[1][:, RET_DV * h:RET_DV * (h + 1)] = sum(r[h] for r in dv_refs)

        ins = list(dqs) + list(dks) + list(dvs) + (list(tabs) if rope else [])
        in_specs = [heads(RET_DKP)] * (len(dqs) + uses) + [heads(RET_DV)] * uses + ([rows(LANE)] * 3 if rope else [])
        n_out = 3 if with_q else 2
        return pl.pallas_call(
            body, name=name, grid=(n // tm,), in_specs=in_specs,
            out_specs=[rows(hh * RET_DK)] * (n_out - 1) + [rows(hh * RET_DV)],
            out_shape=[jax.ShapeDtypeStruct((n, hh * RET_DK), F32)] * (n_out - 1)
            + [jax.ShapeDtypeStruct((n, hh * RET_DV), F32)],
            compiler_params=_params(),
        )(*ins)

    @jax.custom_vjp
    def ret_pack(rq, rk, rv, crk, crv):
        q, k, v = pack_call(tag + "_lat", n_lat, rq, rk, rv, True)
        k_c, v_c = pack_call(tag + "_ctx", n_ctx, None, crk, crv, False)
        return (q, k, v), (q, k, v), (k_c, v_c), (k_c, v_c)

    def fwd(rq, rk, rv, crk, crv):
        return ret_pack(rq, rk, rv, crk, crv), None

    def bwd(_, cts):
        lat_f, lat_b, ctx_f, ctx_b = cts
        drq, drk, drv = unpack_call(tag + "_dlat", n_lat, [lat_f[0], lat_b[0]], [lat_f[1], lat_b[1]],
                                    [lat_f[2], lat_b[2]], True)
        dcrk, dcrv = unpack_call(tag + "_dctx", n_ctx, [], [ctx_f[0], ctx_b[0]], [ctx_f[1], ctx_b[1]], False)
        return drq, drk, drv, dcrk, dcrv

    ret_pack.defvjp(fwd, bwd)
    return ret_pack


def _ret_out_tile(y, g):
    mu = jnp.mean(y, axis=-1, keepdims=True)
    var = jnp.mean(jnp.square(y - mu), axis=-1, keepdims=True)
    return (g * jax.nn.sigmoid(g)) * ((y - mu) * lax.rsqrt(var + GN_EPS))


def _make_ret_out(tag):
    def specs(tm):
        y_spec = pl.BlockSpec((None, tm, RET_DV), lambda h, i: (h, i, 0))
        g_spec = pl.BlockSpec((tm, RET_DV), lambda h, i: (i, h))
        return y_spec, g_spec

    def fwd_call(yf, yb, g):
        hh, n, _ = yf.shape
        tm = _pick(n, 1024, 16)
        y_spec, g_spec = specs(tm)

        def body(yf_ref, yb_ref, g_ref, o_ref):
            o_ref[...] = _ret_out_tile(yf_ref[...] + yb_ref[...], g_ref[...]).astype(BF16)

        return pl.pallas_call(
            body, name=tag + "_fwd", grid=(hh, n // tm), in_specs=[y_spec, y_spec, g_spec], out_specs=g_spec,
            out_shape=jax.ShapeDtypeStruct((n, hh * RET_DV), BF16), compiler_params=_params(),
        )(yf, yb, g)

    def bwd_call(yf, yb, g, do):
        hh, n, _ = yf.shape
        tm = _pick(n, 1024, 16)
        y_spec, g_spec = specs(tm)

        def body(yf_ref, yb_ref, g_ref, do_ref, dy_ref, dg_ref):
            _, vjp = jax.vjp(_ret_out_tile, yf_ref[...] + yb_ref[...], g_ref[...])
            dy, dg = vjp(do_ref[...].astype(F32))
            dy_ref[...] = dy
            dg_ref[...] = dg

        return pl.pallas_call(
            body, name=tag + "_bwd", grid=(hh, n // tm), in_specs=[y_spec, y_spec, g_spec, g_spec],
            out_specs=[y_spec, g_spec],
            out_shape=[jax.ShapeDtypeStruct(yf.shape, F32), jax.ShapeDtypeStruct(g.shape, F32)],
            compiler_params=_params(),
        )(yf, yb, g, do)

    @jax.custom_vjp
    def ret_out(yf, yb, g):
        return fwd_call(yf, yb, g)

    def fwd(yf, yb, g):
        return fwd_call(yf, yb, g), (yf, yb, g)

    def bwd(res, do):
        dy, dg = bwd_call(*res, do)
        return dy, dy, dg

    ret_out.defvjp(fwd, bwd)
    return ret_out


MLA_DQ_PAD = 2 * LANE
MLA_PACK_ROWS = 256


def _rope128(x, cos, s1, s2, shift=16):
    return x * cos + pltpu.roll(x, LANE - shift, 1) * s1 + pltpu.roll(x, shift, 1) * s2


def _rope128_t(g, cos, s1, s2, shift=16):
    return g * cos + pltpu.roll(g * s1, shift, 1) + pltpu.roll(g * s2, LANE - shift, 1)


def _axial_tables(n_lat):
    half = MLA_ROPE // 2
    inv = AXIAL_BASE ** (-jnp.arange(0, half, 2, dtype=F32) / half)
    return _rope_tables_call("mla_tables", n_lat, inv, half // 2, True)


def _make_mla_pack(tag, n_lat, n_ctx, scale):
    hh = MLA_HEADS
    tm = MLA_PACK_ROWS
    ll = n_lat + n_ctx
    rope0 = hh * MLA_NOPE
    tabs = _axial_tables(n_lat)

    def rope_lanes():
        return lax.broadcasted_iota(jnp.int32, (1, LANE), 1) < MLA_ROPE

    def rows(width):
        return pl.BlockSpec((tm, width), lambda i: (i, 0))

    def heads(width, off):
        return pl.BlockSpec((hh, tm, width), lambda i: (0, i + off, 0))

    def heads_t(width, off):
        return pl.BlockSpec((hh, width, tm), lambda i: (0, 0, i + off))

    def put_kv(kv_ref, kr_rot, k_ref, v_ref, kt_ref, vt_ref):
        kr_b = kr_rot.astype(BF16)
        kr_t = jnp.transpose(kr_rot).astype(BF16)
        for h in range(hh):
            k_nope = kv_ref[:, 2 * LANE * h:2 * LANE * h + MLA_NOPE]
            val = kv_ref[:, 2 * LANE * h + MLA_NOPE:2 * LANE * (h + 1)]
            k_ref[h, :, :MLA_NOPE] = k_nope.astype(BF16)
            k_ref[h, :, MLA_NOPE:] = kr_b
            v_ref[h] = val.astype(BF16)
            kt_ref[h, :MLA_NOPE, :] = jnp.transpose(k_nope).astype(BF16)
            kt_ref[h, MLA_NOPE:, :] = kr_t
            vt_ref[h] = jnp.transpose(val).astype(BF16)

    def fwd_lat(qp, kv, kr):
        def body(qp_ref, kv_ref, kr_ref, cos_ref, s1_ref, s2_ref, q_ref, k_ref, v_ref, kt_ref, vt_ref):
            cos, s1, s2 = cos_ref[...], s1_ref[...], s2_ref[...]
            keep = rope_lanes()
            for j in range(hh // 2):
                rot = _rope128(qp_ref[:, rope0 + LANE * j:rope0 + LANE * (j + 1)], cos, s1, s2)
                q_ref[2 * j, :, MLA_NOPE:] = jnp.where(keep, rot, 0.0).astype(BF16)
                q_ref[2 * j + 1, :, MLA_NOPE:] = jnp.where(keep, pltpu.roll(rot, MLA_ROPE, 1), 0.0).astype(BF16)
            for h in range(hh):
                q_ref[h, :, :MLA_NOPE] = qp_ref[:, MLA_NOPE * h:MLA_NOPE * (h + 1)].astype(BF16)
            kr_rot = jnp.where(keep, _rope128(kr_ref[...], cos, s1, s2), 0.0)
            put_kv(kv_ref, kr_rot, k_ref, v_ref, kt_ref, vt_ref)

        return pl.pallas_call(
            body, name=tag + "_lat", grid=(n_lat // tm,),
            in_specs=[rows(qp.shape[1]), rows(kv.shape[1]), rows(LANE), rows(LANE), rows(LANE), rows(LANE)],
            out_specs=[heads(MLA_DQ_PAD, 0), heads(MLA_DQ_PAD, 0), heads(MLA_V, 0), heads_t(MLA_DQ_PAD, 0),
                       heads_t(MLA_V, 0)],
            out_shape=[jax.ShapeDtypeStruct((hh, n_lat, MLA_DQ_PAD), BF16),
                       jax.ShapeDtypeStruct((hh, ll, MLA_DQ_PAD), BF16), jax.ShapeDtypeStruct((hh, ll, MLA_V), BF16),
                       jax.ShapeDtypeStruct((hh, MLA_DQ_PAD, ll), BF16), jax.ShapeDtypeStruct((hh, MLA_V, ll), BF16)],
            compiler_params=_params(),
        )(qp, kv, kr, *tabs)

    def fwd_ctx(kv_c, kr_c, bufs):
        def body(kv_ref, kr_ref, k_in, v_in, kt_in, vt_in, k_ref, v_ref, kt_ref, vt_ref):
            kr_rot = jnp.where(rope_lanes(), kr_ref[...], 0.0)
            put_kv(kv_ref, kr_rot, k_ref, v_ref, kt_ref, vt_ref)

        any_spec = pl.BlockSpec(memory_space=pl.ANY)
        off = n_lat // tm
        return pl.pallas_call(
            body, name=tag + "_ctx", grid=(n_ctx // tm,),
            in_specs=[rows(kv_c.shape[1]), rows(LANE)] + [any_spec] * 4,
            out_specs=[heads(MLA_DQ_PAD, off), heads(MLA_V, off), heads_t(MLA_DQ_PAD, off), heads_t(MLA_V, off)],
            out_shape=[jax.ShapeDtypeStruct(b.shape, BF16) for b in bufs],
            input_output_aliases={2: 0, 3: 1, 4: 2, 5: 3}, compiler_params=_params(),
        )(kv_c, kr_c, *bufs)

    def take_kv(dk_ref, dv_ref, dkv_ref):
        dkr = jnp.zeros((tm, LANE), F32)
        for h in range(hh):
            dkv_ref[:, 2 * LANE * h:2 * LANE * h + MLA_NOPE] = dk_ref[h, :, :MLA_NOPE].astype(F32)
            dkv_ref[:, 2 * LANE * h + MLA_NOPE:2 * LANE * (h + 1)] = dv_ref[h].astype(F32)
            dkr = dkr + dk_ref[h, :, MLA_NOPE:].astype(F32)
        return jnp.where(rope_lanes(), dkr, 0.0)

    def bwd_lat(dqt, dk, dv, qp_width, kv_width):
        def body(dqt_ref, dk_ref, dv_ref, cos_ref, s1_ref, s2_ref, dqp_ref, dkv_ref, dkr_ref):
            cos, s1, s2 = cos_ref[...], s1_ref[...], s2_ref[...]
            keep = rope_lanes()
            for j in range(hh // 2):
                even = jnp.transpose(dqt_ref[2 * j]) * scale
                odd = jnp.transpose(dqt_ref[2 * j + 1]) * scale
                dqp_ref[:, MLA_NOPE * 2 * j:MLA_NOPE * (2 * j + 1)] = even[:, :MLA_NOPE]
                dqp_ref[:, MLA_NOPE * (2 * j + 1):MLA_NOPE * (2 * j + 2)] = odd[:, :MLA_NOPE]
                g = jnp.where(keep, even[:, MLA_NOPE:], pltpu.roll(odd[:, MLA_NOPE:], MLA_ROPE, 1))
                dqp_ref[:, rope0 + LANE * j:rope0 + LANE * (j + 1)] = _rope128_t(g, cos, s1, s2)
            dkr_ref[...] = jnp.where(keep, _rope128_t(take_kv(dk_ref, dv_ref, dkv_ref), cos, s1, s2), 0.0)

        return pl.pallas_call(
            body, name=tag + "_dlat", grid=(n_lat // tm,),
            in_specs=[pl.BlockSpec((hh, MLA_DQ_PAD, tm), lambda i: (0, 0, i)),
                      heads(MLA_DQ_PAD, 0), heads(MLA_V, 0), rows(LANE), rows(LANE), rows(LANE)],
            out_specs=[rows(qp_width), rows(kv_width), rows(LANE)],
            out_shape=[jax.ShapeDtypeStruct((n_lat, qp_width), F32), jax.ShapeDtypeStruct((n_lat, kv_width), F32),
                       jax.ShapeDtypeStruct((n_lat, LANE), F32)],
            compiler_params=_params(),
        )(dqt, dk, dv, *tabs)

    def bwd_ctx(dk, dv, kv_width):
        def body(dk_ref, dv_ref, dkv_ref, dkr_ref):
            dkr_ref[...] = take_kv(dk_ref, dv_ref, dkv_ref)

        off = n_lat // tm
        return pl.pallas_call(
            body, name=tag + "_dctx", grid=(n_ctx // tm,),
            in_specs=[heads(MLA_DQ_PAD, off), heads(MLA_V, off)],
            out_specs=[rows(kv_width), rows(LANE)],
            out_shape=[jax.ShapeDtypeStruct((n_ctx, kv_width), F32), jax.ShapeDtypeStruct((n_ctx, LANE), F32)],
            compiler_params=_params(),
        )(dk, dv)

    def pack(qp, kv, kr, kv_c, kr_c):
        q, *bufs = fwd_lat(qp, kv, kr)
        return (q, *fwd_ctx(kv_c, kr_c, bufs))

    def unpack(dqt, dk, dv):
        qp_width, kv_width = hh * (MLA_NOPE + MLA_ROPE), hh * (MLA_NOPE + MLA_V)
        dqp, dkv, dkr = bwd_lat(dqt, dk, dv, qp_width, kv_width)
        dkv_c, dkr_c = bwd_ctx(dk, dv, kv_width)
        return dqp, dkv, dkr, dkv_c, dkr_c

    return pack, unpack


def _make_mla(tag, n_lat, n_ctx):
    scale = (MLA_NOPE + MLA_ROPE) ** -0.5
    pack, unpack = _make_mla_pack(tag + "pack", n_lat, n_ctx, scale)
    attn_fwd, attn_delta, attn_bwd = _make_attention(tag, scale)

    @jax.custom_vjp
    def mla(qp, kv, kr, kv_c, kr_c):
        q, k, _, _, vt = pack(qp, kv, kr, kv_c, kr_c)
        return attn_fwd(q, k, vt)[0]

    def fwd(qp, kv, kr, kv_c, kr_c):
        q, k, v, kt, vt = pack(qp, kv, kr, kv_c, kr_c)
        o, lse = attn_fwd(q, k, vt)
        return o, (q, k, kt, v, o, lse)

    def bwd(res, do):
        q, k, kt, v, o, lse = res
        delta = attn_delta(o, do, q.shape[0])
        dqt, dk, dv = attn_bwd(q, k, kt, v, do, lse, delta)
        return unpack(dqt, dk, dv)

    mla.defvjp(fwd, bwd)
    return mla


def _make_attention(tag, scale):
    neg_big = -1e30
    log2e = 1.4426950408889634
    sub = 256

    def fwd_call(q, k, vt):
        hh, n, dq = q.shape
        dv, ll = vt.shape[1], vt.shape[2]
        tq, tk = _pick(n, 1024), _pick(ll, 1408)
        sb = sub if tk % sub == 0 else tk
        c2 = scale * log2e
        k_steps = ll // tk

        def body(q_ref, k_ref, vt_ref, o_ref, lse_ref, m_scr, l_scr, acc_scr, s_scr, p_scr):
            j = pl.program_id(2)

            @pl.when(j == 0)
            def _():
                m_scr[...] = jnp.full_like(m_scr, neg_big)
                l_scr[...] = jnp.zeros_like(l_scr)
                acc_scr[...] = jnp.zeros_like(acc_scr)

            q_t = q_ref[...]
            m_prev = m_scr[...]
            m_new = m_prev
            for kk in range(tk // sb):
                rows = slice(kk * sb, (kk + 1) * sb)
                s_t = _dot(k_ref[rows, :], q_t, NT)
                s_scr[rows, :] = s_t
                m_new = jnp.maximum(m_new, jnp.max(s_t, axis=0, keepdims=True))
            mc = m_new * c2
            l_part = jnp.zeros_like(m_new)
            for kk in range(tk // sb):
                rows = slice(kk * sb, (kk + 1) * sb)
                p_t = jnp.exp2(s_scr[rows, :] * c2 - mc)
                l_part = l_part + jnp.sum(p_t, axis=0, keepdims=True)
                p_scr[rows, :] = p_t.astype(BF16)
            alpha = jnp.exp2((m_prev - m_new) * c2)
            l_scr[...] = alpha * l_scr[...] + l_part
            acc_scr[...] = alpha * acc_scr[...] + _dot(vt_ref[...], p_scr[...], NN)
            m_scr[...] = m_new

            @pl.when(j == k_steps - 1)
            def _():
                o_ref[...] = jnp.transpose(acc_scr[...] / l_scr[...]).astype(BF16)
                lse_ref[...] = m_scr[...] * scale + jnp.log(l_scr[...])

        return pl.pallas_call(
            body, name=tag + "_fwd", grid=(hh, n // tq, k_steps),
            in_specs=[pl.BlockSpec((None, tq, dq), lambda h, i, j: (h, i, 0)),
                      pl.BlockSpec((None, tk, dq), lambda h, i, j: (h, j, 0)),
                      pl.BlockSpec((None, dv, tk), lambda h, i, j: (h, 0, j))],
            out_specs=[pl.BlockSpec((tq, dv), lambda h, i, j: (i, h)),
                       pl.BlockSpec((None, 1, tq), lambda h, i, j: (h, 0, i))],
            out_shape=[jax.ShapeDtypeStruct((n, hh * dv), BF16), jax.ShapeDtypeStruct((hh, 1, n), F32)],
            scratch_shapes=[pltpu.VMEM((1, tq), F32), pltpu.VMEM((1, tq), F32), pltpu.VMEM((dv, tq), F32),
                            pltpu.VMEM((tk, tq), F32), pltpu.VMEM((tk, tq), BF16)],
            compiler_params=_params(),
        )(q, k, vt)

    def delta_call(o, do, hh):
        n = o.shape[0]
        dv = o.shape[1] // hh
        tq = _pick(n, 1024)

        def body(o_ref, do_ref, d_ref):
            prod_t = jnp.transpose(o_ref[...].astype(F32) * do_ref[...].astype(F32))
            d_ref[...] = jnp.sum(prod_t, axis=0, keepdims=True)

        spec = pl.BlockSpec((tq, dv), lambda h, i: (i, h))
        return pl.pallas_call(
            body, name=tag + "_delta", grid=(hh, n // tq), in_specs=[spec, spec],
            out_specs=pl.BlockSpec((None, 1, tq), lambda h, i: (h, 0, i)),
            out_shape=jax.ShapeDtypeStruct((hh, 1, n), F32), compiler_params=_params(),
        )(o, do)

    def bwd_call(q, k, kt, v, do, lse, delta):
        hh, n, dq = q.shape
        ll, dv = k.shape[1], v.shape[2]
        tq, tk = _pick(n, 1024), _pick(ll, 1408)
        sb = tk
        c2 = scale * log2e
        q_steps = n // tq

        def body(q_ref, k_ref, kt_ref, v_ref, do_ref, lse_ref, d_ref, dqt_ref, dk_ref, dv_ref, dk_scr, dv_scr):
            j = pl.program_id(1)
            i = pl.program_id(2)

            @pl.when(i == 0)
            def _():
                dk_scr[...] = jnp.zeros_like(dk_scr)
                dv_scr[...] = jnp.zeros_like(dv_scr)

            q_t, do_t = q_ref[...], do_ref[...]
            lse2 = lse_ref[...] * log2e
            delta_t = d_ref[...]
            dq_part = None
            for kk in range(tk // sb):
                rows = slice(kk * sb, (kk + 1) * sb)
                s_t = _dot(k_ref[rows, :], q_t, NT)
                p_t = jnp.exp2(s_t * c2 - lse2)
                ds_t = p_t * (_dot(v_ref[rows, :], do_t, NT) - delta_t)
                dv_scr[rows, :] += _dot(p_t, do_t, NN)
                dk_scr[rows, :] += _dot(ds_t, q_t, NN)
                part = _dot(kt_ref[:, rows], ds_t, NN)
                dq_part = part if dq_part is None else dq_part + part
            cols = pl.ds(pl.multiple_of(i * tq, tq), tq)

            @pl.when(j == 0)
            def _():
                dqt_ref[:, cols] = dq_part

            @pl.when(j > 0)
            def _():
                dqt_ref[:, cols] += dq_part

            @pl.when(i == q_steps - 1)
            def _():
                dk_ref[...] = (dk_scr[...] * scale).astype(BF16)
                dv_ref[...] = dv_scr[...].astype(BF16)

        return pl.pallas_call(
            body, name=tag + "_bwd", grid=(hh, ll // tk, q_steps),
            in_specs=[pl.BlockSpec((None, tq, dq), lambda h, j, i: (h, i, 0)),
                      pl.BlockSpec((None, tk, dq), lambda h, j, i: (h, j, 0)),
                      pl.BlockSpec((None, dq, tk), lambda h, j, i: (h, 0, j)),
                      pl.BlockSpec((None, tk, dv), lambda h, j, i: (h, j, 0)),
                      pl.BlockSpec((tq, dv), lambda h, j, i: (i, h)),
                      pl.BlockSpec((None, 1, tq), lambda h, j, i: (h, 0, i)),
                      pl.BlockSpec((None, 1, tq), lambda h, j, i: (h, 0, i))],
            out_specs=[pl.BlockSpec((None, dq, n), lambda h, j, i: (h, 0, 0)),
                       pl.BlockSpec((None, tk, dq), lambda h, j, i: (h, j, 0)),
                       pl.BlockSpec((None, tk, dv), lambda h, j, i: (h, j, 0))],
            out_shape=[jax.ShapeDtypeStruct((hh, dq, n), F32), jax.ShapeDtypeStruct((hh, ll, dq), BF16),
                       jax.ShapeDtypeStruct((hh, ll, dv), BF16)],
            scratch_shapes=[pltpu.VMEM((tk, dq), F32), pltpu.VMEM((tk, dv), F32)],
            compiler_params=_params(),
        )(q, k, kt, v, do, lse, delta)

    return fwd_call, delta_call, bwd_call


def _loss_tile(x, g, tgt):
    r = lax.rsqrt(jnp.mean(x * x, axis=-1, keepdims=True) + RMS_EPS)
    err = x * r * g - tgt
    per_tok = jnp.mean(err * err, axis=-1, keepdims=True)
    return 0.5 * jnp.sum(per_tok, axis=0, keepdims=True)


def _make_final_loss(tag):
    def fwd_call(x, g, tgt):
        t, d = x.shape
        tm = _pick(t, 512, 16)

        def body(x_ref, g_ref, t_ref, l_ref):
            l_ref[...] = jnp.broadcast_to(_loss_tile(x_ref[...], g_ref[...], t_ref[...]), (1, LANE))

        parts = pl.pallas_call(
            body, name=tag + "_fwd", grid=(t // tm,),
            in_specs=[_row_spec(tm, d), _vec_spec(d), _row_spec(tm, d)],
            out_specs=pl.BlockSpec((None, 1, LANE), lambda i: (i, 0, 0)),
            out_shape=jax.ShapeDtypeStruct((t // tm, 1, LANE), F32), compiler_params=_params(),
        )(x, g, tgt)
        return jnp.sum(parts[:, 0, 0])

    def bwd_call(x, g, tgt, dl):
        t, d = x.shape
        tm = _pick(t, 256, 16)

        def body(x_ref, g_ref, t_ref, dl_ref, dx_ref, dg_ref):
            _, vjp = jax.vjp(_loss_tile, x_ref[...], g_ref[...], t_ref[...])
            dx, dg, _ = vjp(dl_ref[...])
            dx_ref[...] = dx

            @pl.when(pl.program_id(0) == 0)
            def _():
                dg_ref[...] = jnp.zeros_like(dg_ref)

            dg_ref[...] += dg

        return pl.pallas_call(
            body, name=tag + "_bwd", grid=(t // tm,),
            in_specs=[_row_spec(tm, d), _vec_spec(d), _row_spec(tm, d), pl.BlockSpec((1, 1), lambda i: (0, 0))],
            out_specs=[_row_spec(tm, d), _vec_spec(d)],
            out_shape=[jax.ShapeDtypeStruct((t, d), F32), jax.ShapeDtypeStruct((1, d), F32)],
            compiler_params=_params(),
        )(x, g, tgt, dl)

    @jax.custom_vjp
    def final_loss(x, g, tgt):
        return fwd_call(x, g, tgt)

    def fwd(x, g, tgt):
        return fwd_call(x, g, tgt), (x, g, tgt)

    def bwd(res, dl):
        x, g, tgt = res
        dx, dg = bwd_call(x, g, tgt, dl.reshape(1, 1).astype(F32))
        return dx, dg, jnp.zeros_like(tgt)

    final_loss.defvjp(fwd, bwd)
    return final_loss


def _exchange(arrays, gather, name):
    n = len(arrays)

    def body(*refs):
        ins, outs = refs[:n], refs[n:2 * n]
        send_sems, recv_sems, local_sems = refs[2 * n:]
        me = 4 * lax.axis_index("x") + 2 * lax.axis_index("y") + lax.axis_index("c")

        def remote(a, d, wait_side=False):
            peer = (me + d) % N_DEV
            origin = (me + N_DEV - d) % N_DEV
            src = ins[a] if gather else ins[a].at[peer]
            dst = outs[a].at[origin if wait_side else me]
            return pltpu.make_async_remote_copy(
                src_ref=src, dst_ref=dst, send_sem=send_sems.at[a, d - 1], recv_sem=recv_sems.at[a, d - 1],
                device_id=(peer // 4, (peer // 2) % 2, peer % 2), device_id_type=pl.DeviceIdType.MESH)

        def local(a):
            src = ins[a] if gather else ins[a].at[me]
            return pltpu.make_async_copy(src, outs[a].at[me], local_sems.at[a])

        for a in range(n):
            for d in range(1, N_DEV):
                remote(a, d).start()
            local(a).start()
        for a in range(n):
            local(a).wait()
            for d in range(1, N_DEV):
                remote(a, d, wait_side=True).wait_recv()
                remote(a, d).wait_send()

    out_shape = []
    for arr in arrays:
        shape = (N_DEV,) + arr.shape if gather else arr.shape
        out_shape.append(jax.ShapeDtypeStruct(shape, arr.dtype))
    any_spec = pl.BlockSpec(memory_space=pl.ANY)
    return pl.pallas_call(
        body, name=name, in_specs=[any_spec] * n, out_specs=[any_spec] * n, out_shape=out_shape,
        scratch_shapes=[pltpu.SemaphoreType.DMA((n, N_DEV - 1)), pltpu.SemaphoreType.DMA((n, N_DEV - 1)),
                        pltpu.SemaphoreType.DMA((n,))],
        compiler_params=pltpu.CompilerParams(has_side_effects=True),
    )(*arrays)


def _split_copy(ins, lands, send_sems, recv_sems, a, d, gather, wait_side):
    me = 4 * lax.axis_index("x") + 2 * lax.axis_index("y") + lax.axis_index("c")
    peer = (me + d) % N_DEV
    origin = (me + N_DEV - d) % N_DEV
    return pltpu.make_async_remote_copy(
        src_ref=ins[a] if gather else ins[a].at[peer], dst_ref=lands[a].at[origin if wait_side else me],
        send_sem=send_sems.at[a * (N_DEV - 1) + d - 1], recv_sem=recv_sems.at[a * (N_DEV - 1) + d - 1],
        device_id=(peer // 4, (peer // 2) % 2, peer % 2), device_id_type=pl.DeviceIdType.MESH)


def _exchange_start(srcs, lands, after, gather, name):
    n = len(srcs)

    def body(*refs):
        ins, lnd = refs[:n], refs[n:2 * n]
        send_sems, recv_sems = refs[2 * n + 1], refs[2 * n + 2]
        for a in range(n):
            for d in range(1, N_DEV):
                _split_copy(ins, lnd, send_sems, recv_sems, a, d, gather, False).start()

    hbm = pl.BlockSpec(memory_space=pltpu.HBM)
    sem = pl.BlockSpec(memory_space=pltpu.SEMAPHORE)
    bufs = [pltpu.with_memory_space_constraint(t, pltpu.HBM) for t in list(srcs) + list(lands) + [after]]
    res = pl.pallas_call(
        body, name=name,
        in_specs=[hbm] * (2 * n + 1), out_specs=[sem, sem] + [hbm] * (2 * n + 1),
        out_shape=[pltpu.SemaphoreType.DMA((n * (N_DEV - 1),)), pltpu.SemaphoreType.DMA((n * (N_DEV - 1),))]
        + [pltpu.HBM(t.shape, t.dtype) for t in bufs],
        input_output_aliases={i: 2 + i for i in range(2 * n + 1)},
        compiler_params=pltpu.CompilerParams(has_side_effects=pltpu.SideEffectType.DATAFLOW_SIDE_EFFECTING),
    )(*bufs)
    return res[0], res[1], res[2:2 + n], res[2 + n:2 + 2 * n], res[-1]


def _exchange_wait(send_sems, recv_sems, srcs, lands, after, gather, name):
    n = len(srcs)

    def body(*refs):
        ins, lnd = refs[:n], refs[n:2 * n]
        send_sems_ref, recv_sems_ref = refs[2 * n], refs[2 * n + 1]
        for a in range(n):
            for d in range(1, N_DEV):
                _split_copy(ins, lnd, send_sems_ref, recv_sems_ref, a, d, gather, False).wait_send()
                _split_copy(ins, lnd, send_sems_ref, recv_sems_ref, a, d, gather, True).wait_recv()

    hbm = pl.BlockSpec(memory_space=pltpu.HBM)
    sem = pl.BlockSpec(memory_space=pltpu.SEMAPHORE)
    bufs = list(srcs) + list(lands)
    res = pl.pallas_call(
        body, name=name,
        in_specs=[hbm] * (2 * n) + [sem, sem, pl.BlockSpec(memory_space=pl.ANY)],
        out_specs=[hbm] * (2 * n),
        out_shape=[pltpu.HBM(t.shape, t.dtype) for t in bufs],
        input_output_aliases={i: i for i in range(2 * n)},
        compiler_params=pltpu.CompilerParams(has_side_effects=pltpu.SideEffectType.DATAFLOW_SIDE_EFFECTING),
    )(*bufs, send_sems, recv_sems, after)
    return res[n:]


def _own_slot(block, me):
    empty = lax.empty((N_DEV,) + block.shape, block.dtype)
    return lax.dynamic_update_slice(empty, block[None], (me,) + (0,) * block.ndim)


def _coords():
    return lax.axis_index("x"), lax.axis_index("y"), lax.axis_index("c")


def _other_chips(x, y):
    return [(1 - x, y), (x, 1 - y), (1 - x, 1 - y)]


def _gather_two_level(arrays, name):
    n = len(arrays)

    def body(*refs):
        ins, outs = refs[:n], refs[n:2 * n]
        send_sems, recv_sems, local_sems = refs[2 * n:]
        x, y, c = _coords()
        me, sib = (x, y, c), (x, y, 1 - c)
        chips = _other_chips(x, y)

        def copy(a, k, block, to, from_input=False):
            slot = 4 * block[0] + 2 * block[1] + block[2]
            return pltpu.make_async_remote_copy(
                src_ref=ins[a] if from_input else outs[a].at[slot], dst_ref=outs[a].at[slot],
                send_sem=send_sems.at[a, k], recv_sem=recv_sems.at[a, k],
                device_id=to, device_id_type=pl.DeviceIdType.MESH)

        def local(a):
            return pltpu.make_async_copy(ins[a], outs[a].at[4 * x + 2 * y + c], local_sems.at[a])

        for a in range(n):
            for j, chip in enumerate(chips):
                copy(a, 1 + j, me, (*chip, c), True).start()
            copy(a, 0, me, sib, True).start()
            local(a).start()
        for a in range(n):
            for j, chip in enumerate(chips):
                copy(a, 1 + j, (*chip, c), me).wait_recv()
                copy(a, 4 + j, (*chip, c), sib).start()
        for a in range(n):
            copy(a, 0, sib, me).wait_recv()
            for j, chip in enumerate(chips):
                copy(a, 4 + j, (*chip, 1 - c), me).wait_recv()
            for k in range(N_DEV - 1):
                copy(a, k, me, sib, True).wait_send()
            local(a).wait()

    any_spec = pl.BlockSpec(memory_space=pl.ANY)
    return pl.pallas_call(
        body, name=name, in_specs=[any_spec] * n, out_specs=[any_spec] * n,
        out_shape=[jax.ShapeDtypeStruct((N_DEV,) + arr.shape, arr.dtype) for arr in arrays],
        scratch_shapes=[pltpu.SemaphoreType.DMA((n, N_DEV - 1)), pltpu.SemaphoreType.DMA((n, N_DEV - 1)),
                        pltpu.SemaphoreType.DMA((n,))],
        compiler_params=pltpu.CompilerParams(has_side_effects=True),
    )(*arrays)


def _swap_sibling(arrays, name):
    n = len(arrays)
    n_chip = N_DEV // 2

    def body(*refs):
        ins, outs = refs[:n], refs[n:2 * n]
        send_sems, recv_sems = refs[2 * n:]
        x, y, c = _coords()

        def copy(a, q):
            return pltpu.make_async_remote_copy(
                src_ref=ins[a].at[2 * q + (1 - c)], dst_ref=outs[a].at[q],
                send_sem=send_sems.at[a, q], recv_sem=recv_sems.at[a, q],
                device_id=(x, y, 1 - c), device_id_type=pl.DeviceIdType.MESH)

        for a in range(n):
            for q in range(n_chip):
                copy(a, q).start()
        for a in range(n):
            for q in range(n_chip):
                copy(a, q).wait_recv()
                copy(a, q).wait_send()

    any_spec = pl.BlockSpec(memory_space=pl.ANY)
    return pl.pallas_call(
        body, name=name, in_specs=[any_spec] * n, out_specs=[any_spec] * n,
        out_shape=[jax.ShapeDtypeStruct((n_chip,) + arr.shape[1:], arr.dtype) for arr in arrays],
        scratch_shapes=[pltpu.SemaphoreType.DMA((n, n_chip)), pltpu.SemaphoreType.DMA((n, n_chip))],
        compiler_params=pltpu.CompilerParams(has_side_effects=True),
    )(*arrays)


def _scatter_chips(arrays, name):
    n = len(arrays)
    n_chip = N_DEV // 2

    def body(*refs):
        ins, outs = refs[:n], refs[n:2 * n]
        send_sems, recv_sems, local_sems = refs[2 * n:]
        x, y, c = _coords()
        q_me = 2 * x + y
        chips = _other_chips(x, y)

        def copy(a, j, wait_side=False):
            q_peer = 2 * chips[j][0] + chips[j][1]
            return pltpu.make_async_remote_copy(
                src_ref=ins[a].at[q_peer], dst_ref=outs[a].at[q_peer if wait_side else q_me],
                send_sem=send_sems.at[a, j], recv_sem=recv_sems.at[a, j],
                device_id=(*chips[j], c), device_id_type=pl.DeviceIdType.MESH)

        def local(a):
            return pltpu.make_async_copy(ins[a].at[q_me], outs[a].at[q_me], local_sems.at[a])

        for a in range(n):
            for j in range(n_chip - 1):
                copy(a, j).start()
            local(a).start()
        for a in range(n):
            local(a).wait()
            for j in range(n_chip - 1):
                copy(a, j, wait_side=True).wait_recv()
                copy(a, j).wait_send()

    any_spec = pl.BlockSpec(memory_space=pl.ANY)
    return pl.pallas_call(
        body, name=name, in_specs=[any_spec] * n, out_specs=[any_spec] * n,
        out_shape=[jax.ShapeDtypeStruct(arr.shape, arr.dtype) for arr in arrays],
        scratch_shapes=[pltpu.SemaphoreType.DMA((n, n_chip - 1)), pltpu.SemaphoreType.DMA((n, n_chip - 1)),
                        pltpu.SemaphoreType.DMA((n,))],
        compiler_params=pltpu.CompilerParams(has_side_effects=True),
    )(*arrays)


def _pair_add(full, theirs, core, name):
    n_chip, r, cn = theirs.shape
    tr = _pick(r, max(16, (2 * 1024 * 1024) // (4 * cn) // 16 * 16), 16)

    def body(core_ref, mine_ref, theirs_ref, o_ref):
        o_ref[...] = (mine_ref[...].astype(F32) + theirs_ref[...].astype(F32)).astype(BF16)

    tile = pl.BlockSpec((None, tr, cn), lambda q, i, core_ref: (q, i, 0))
    return pl.pallas_call(
        body, name=name,
        grid_spec=pltpu.PrefetchScalarGridSpec(
            num_scalar_prefetch=1, grid=(n_chip, r // tr),
            in_specs=[pl.BlockSpec((None, tr, cn), lambda q, i, core_ref: (2 * q + core_ref[0], i, 0)), tile],
            out_specs=tile),
        out_shape=jax.ShapeDtypeStruct(theirs.shape, BF16), compiler_params=_params(),
    )(core, full, theirs)


def _make_gather_op(tag):
    @jax.custom_vjp
    def gather_op(xl):
        return _exchange([xl], True, tag + "_gather")[0]

    def fwd(xl):
        return gather_op(xl), None

    def bwd(_, g):
        return (jnp.sum(_exchange([g], False, tag + "_scatter")[0], axis=0),)

    gather_op.defvjp(fwd, bwd)
    return gather_op


def _adamw(gstack, w, m, v, name):
    s, r, cn = gstack.shape
    tr = _pick(r, max(8, (2 * 1024 * 1024) // (4 * cn) // 8 * 8), 8)
    c1 = 1.0 - ADAM_B1 ** ADAM_STEP
    c2 = 1.0 - ADAM_B2 ** ADAM_STEP

    def body(g_ref, w_ref, m_ref, v_ref, go_ref, d_ref, mo_ref, vo_ref):
        g = g_ref[0].astype(F32)
        for q in range(1, s):
            g = g + g_ref[q].astype(F32)
        m_new = ADAM_B1 * m_ref[...] + (1.0 - ADAM_B1) * g
        v_new = ADAM_B2 * v_ref[...] + (1.0 - ADAM_B2) * (g * g)
        go_ref[...] = g
        mo_ref[...] = m_new
        vo_ref[...] = v_new
        d_ref[...] = -ADAM_LR * ((m_new / c1) / (jnp.sqrt(v_new / c2) + ADAM_EPS) + ADAM_WD * w_ref[...])

    tile = pl.BlockSpec((tr, cn), lambda i: (i, 0))
    out = jax.ShapeDtypeStruct((r, cn), F32)
    return pl.pallas_call(
        body, name=name, grid=(r // tr,),
        in_specs=[pl.BlockSpec((s, tr, cn), lambda i: (0, i, 0)), tile, tile, tile],
        out_specs=[tile, tile, tile, tile], out_shape=[out, out, out, out],
        compiler_params=_params(),
    )(gstack, w, m, v)


def _cols_from_stack(w):
    return jnp.swapaxes(w, 0, 1).reshape(w.shape[1], N_DEV * w.shape[2])


def _ada_vectors(p, silu_c_all, me):
    d = p["c_ctx"].shape[0]
    n_a = p["ada_w"].shape[1]
    a_in = jnp.concatenate([silu_c_all, jax.nn.silu(p["c_ctx"])[None, :], jnp.zeros((7, d), F32)], axis=0)
    b_loc = lax.dynamic_slice(p["ada_b"], (0, me * n_a), (1, n_a))
    r_loc = _make_small_mm("ada")(a_in, p["ada_w"]) + b_loc
    r_full = _make_gather_op("ada")(r_loc)
    m_lat = lax.dynamic_index_in_dim(r_full, me, axis=1, keepdims=False).reshape(N_MOD, 1, d)
    m_ctx = r_full[:, N_DEV, :].reshape(N_MOD, 1, d)
    return m_lat, m_ctx


def _stage_a_fwd(p_ada, x, ctx, ng, w_in, wait_w_out, silu_c_all, me):
    (m_lat, m_ctx), vjp_ada = jax.vjp(lambda q: _ada_vectors(q, silu_c_all, me), p_ada)
    lat, cx = _ffn_parts("ffn1"), _ffn_parts("ffn1c")
    h, a, gu = lat[0](x, ng, m_lat[0], m_lat[1], w_in)
    hc, ac, guc = cx[0](ctx, ng, m_ctx[0], m_ctx[1], w_in)
    w_out = wait_w_out(a)
    x1, f1 = lat[1](a, w_out, x, m_lat[2])
    c1, f1c = cx[1](ac, w_out, ctx, m_ctx[2])
    res = dict(vjp_ada=vjp_ada, m_lat=m_lat, m_ctx=m_ctx, x=x, ctx=ctx, ng=ng, w_in=w_in, w_out=w_out,
               lat=(h, a, gu, f1), cx=(hc, ac, guc, f1c))
    return (x1, c1, m_lat, m_ctx), res


def _stage_a_bwd(res, dx1, dc1, dm_lat, dm_ctx, start_w_out_grads):
    lat, cx = _ffn_parts("ffn1"), _ffn_parts("ffn1c")
    m_lat, m_ctx, ng, w_in, w_out = res["m_lat"], res["m_ctx"], res["ng"], res["w_in"], res["w_out"]
    hc, ac, guc, f1c = res["cx"]
    dgate_c, dgu_c, dw_out_c = cx[2](dc1, f1c, m_ctx[2], w_out, guc, ac)
    _, dng_c, dsh_c, dsc_c, dw_in_c = cx[3](res["ctx"], ng, m_ctx[0], m_ctx[1], w_in, hc, dgu_c, dc1)
    h, a, gu, f1 = res["lat"]
    dgate, dgu, dw_out = lat[2](dx1, f1, m_lat[2], w_out, gu, a)
    dgu = start_w_out_grads(dw_out + dw_out_c, dgu)
    dx, dng, dsh, dsc, dw_in = lat[3](res["x"], ng, m_lat[0], m_lat[1], w_in, h, dgu, dx1)

    def rows(dsh_, dsc_, dgate_):
        return jnp.concatenate([dsh_, dsc_, dgate_, jnp.zeros((N_MOD - 3,) + dsh_.shape, F32)[:, 0]], axis=0)[:, None, :]

    (g_ada,) = res["vjp_ada"]((dm_lat + rows(dsh, dsc, dgate), dm_ctx + rows(dsh_c, dsc_c, dgate_c)))
    return g_ada, dx, dng + dng_c, dw_in + dw_in_c


def _stage_b(p, x1, c1, m_lat, m_ctx):
    n_lat, d = x1.shape
    n_ctx = c1.shape[0]
    w_mix = jnp.pad(_cols_from_stack(p["mix_w_in"]), ((0, 0), (0, MIX_IN_PAD - MIX_IN)))
    proj, x1 = _make_norm_proj_carry("mix")(x1, p["norm2_g"], m_lat[3], m_lat[4], w_mix)
    proj_c = _make_norm_proj("mixc")(c1, p["norm2_g"], m_ctx[3], m_ctx[4], w_mix)
    widths = SPLITS[:6] + (LANE,)
    rq, rk, rv, rg, cq, ckv, kr = _make_split("mixsplit", widths, MIX_IN_PAD)(proj)
    _, crk, crv, _, _, cckv, ckr = _make_split("mixsplitc", widths, MIX_IN_PAD)(proj_c)

    zq = jnp.zeros((1, MLA_Q_RANK), F32)
    zkv = jnp.zeros((1, MLA_KV_RANK), F32)
    w_uq3 = _cols_from_stack(p["mla_w_uq"]).reshape(MLA_Q_RANK, MLA_HEADS, MLA_NOPE + MLA_ROPE)
    w_uq = jnp.concatenate([w_uq3[:, :, :MLA_NOPE].reshape(MLA_Q_RANK, -1),
                            w_uq3[:, :, MLA_NOPE:].reshape(MLA_Q_RANK, -1)], axis=1)
    w_ukv = _cols_from_stack(p["mla_w_ukv"])
    q = _make_norm_proj("uq")(cq, p["mla_q_norm_g"], zq, zq, w_uq)
    kv = _make_norm_proj("ukv")(ckv, p["mla_kv_norm_g"], zkv, zkv, w_ukv)
    kv_c = _make_norm_proj("ukvc")(cckv, p["mla_kv_norm_g"], zkv, zkv, w_ukv)

    lg_f = jax.nn.log_sigmoid(p["ret_decay_fwd"][0])
    lg_b = jax.nn.log_sigmoid(p["ret_decay_bwd"][0])
    lat_f, lat_b, ctx_f, ctx_b = _make_ret_pack("retpack", n_lat, n_ctx)(rq, rk, rv, crk, crv)
    assert n_ctx == RET_CHUNK, "the context prefix is one retention chunk"

    def lanes(lg):
        return jnp.broadcast_to(lg[:, None, None], (RET_HEADS, 1, LANE))

    s0_f = _make_ctx_state("retcf", False)(*ctx_f, lanes(lg_f))
    s0_b = _make_ctx_state("retcb", True)(*ctx_b, lanes(lg_b))
    y_f = _make_ret_dir("retf", False)(*lat_f, lanes(lg_f), s0_f)
    y_b = _make_ret_dir("retb", True)(*lat_b, lanes(lg_b), s0_b)
    ret_o = _make_ret_out("reto")(y_f, y_b, rg)

    mla_o = _make_mla("mla", n_lat, n_ctx)(q, kv, kr, kv_c, ckr)

    w_mo = p["mix_w_out"].reshape(-1, d)
    return _make_res_proj("mixo")(jnp.concatenate([ret_o, mla_o], axis=-1), w_mo, x1, m_lat[5])


def _stage_c(p, x2, m_lat, tgt):
    x3 = _make_ffn_block("ffn2")(x2, p["norm3_g"], m_lat[6], m_lat[7], m_lat[8], p["ffn2_w_in"], p["ffn2_w_out"])
    return _make_final_loss("loss")(x3, p["final_norm_g"], tgt)


FIRST = ("ffn1_w_in", "ffn1_w_out")
MID = ("mix_w_in", "mla_w_uq", "mla_w_ukv", "mix_w_out")
LAST = ("ffn2_w_in", "ffn2_w_out")
BIG = FIRST + MID + LAST
SMALL = ("c_ctx", "ada_b", "norm1_g", "norm2_g", "ret_decay_fwd", "ret_decay_bwd", "mla_q_norm_g",
         "mla_kv_norm_g", "norm3_g", "final_norm_g")
WEIGHTS = ("c_ctx", "ada_w", "ada_b", "norm1_g", "ffn1_w_in", "ffn1_w_out", "norm2_g", "mix_w_in", "ret_decay_fwd",
           "ret_decay_bwd", "mla_q_norm_g", "mla_w_uq", "mla_kv_norm_g", "mla_w_ukv", "mix_w_out", "norm3_g",
           "ffn2_w_in", "ffn2_w_out", "final_norm_g")


def _pack(parts):
    flat = jnp.concatenate([t.reshape(-1) for t in parts])
    pad = (-flat.shape[0]) % LANE
    return jnp.pad(flat, (0, pad)).reshape(1, -1)


def _unpack(flat, like):
    out, off = [], 0
    for t in like:
        out.append(flat[0, off:off + t.size].reshape(t.shape))
        off += t.size
    return out


def kernel(x, c, ctx, c_ctx, ada_w, ada_b, norm1_g, ffn1_w_in, ffn1_w_out, norm2_g, mix_w_in, ret_decay_fwd, ret_decay_bwd, mla_q_norm_g, mla_w_uq, mla_kv_norm_g, mla_w_ukv, mix_w_out, norm3_g, ffn2_w_in, ffn2_w_out, final_norm_g, loss_target, m_c_ctx, m_ada_w, m_ada_b, m_norm1_g, m_ffn1_w_in, m_ffn1_w_out, m_norm2_g, m_mix_w_in, m_ret_decay_fwd, m_ret_decay_bwd, m_mla_q_norm_g, m_mla_w_uq, m_mla_kv_norm_g, m_mla_w_ukv, m_mix_w_out, m_norm3_g, m_ffn2_w_in, m_ffn2_w_out, m_final_norm_g, v_c_ctx, v_ada_w, v_ada_b, v_norm1_g, v_ffn1_w_in, v_ffn1_w_out, v_norm2_g, v_mix_w_in, v_ret_decay_fwd, v_ret_decay_bwd, v_mla_q_norm_g, v_mla_w_uq, v_mla_kv_norm_g, v_mla_w_ukv, v_mix_w_out, v_norm3_g, v_ffn2_w_in, v_ffn2_w_out, v_final_norm_g):
    w = dict(c_ctx=c_ctx, ada_w=ada_w, ada_b=ada_b, norm1_g=norm1_g, ffn1_w_in=ffn1_w_in, ffn1_w_out=ffn1_w_out,
             norm2_g=norm2_g, mix_w_in=mix_w_in, ret_decay_fwd=ret_decay_fwd, ret_decay_bwd=ret_decay_bwd,
             mla_q_norm_g=mla_q_norm_g, mla_w_uq=mla_w_uq, mla_kv_norm_g=mla_kv_norm_g, mla_w_ukv=mla_w_ukv,
             mix_w_out=mix_w_out, norm3_g=norm3_g, ffn2_w_in=ffn2_w_in, ffn2_w_out=ffn2_w_out,
             final_norm_g=final_norm_g)
    mom = dict(c_ctx=m_c_ctx, ada_w=m_ada_w, ada_b=m_ada_b, norm1_g=m_norm1_g, ffn1_w_in=m_ffn1_w_in,
               ffn1_w_out=m_ffn1_w_out, norm2_g=m_norm2_g, mix_w_in=m_mix_w_in, ret_decay_fwd=m_ret_decay_fwd,
               ret_decay_bwd=m_ret_decay_bwd, mla_q_norm_g=m_mla_q_norm_g, mla_w_uq=m_mla_w_uq,
               mla_kv_norm_g=m_mla_kv_norm_g, mla_w_ukv=m_mla_w_ukv, mix_w_out=m_mix_w_out, norm3_g=m_norm3_g,
               ffn2_w_in=m_ffn2_w_in, ffn2_w_out=m_ffn2_w_out, final_norm_g=m_final_norm_g)
    var = dict(c_ctx=v_c_ctx, ada_w=v_ada_w, ada_b=v_ada_b, norm1_g=v_norm1_g, ffn1_w_in=v_ffn1_w_in,
               ffn1_w_out=v_ffn1_w_out, norm2_g=v_norm2_g, mix_w_in=v_mix_w_in, ret_decay_fwd=v_ret_decay_fwd,
               ret_decay_bwd=v_ret_decay_bwd, mla_q_norm_g=v_mla_q_norm_g, mla_w_uq=v_mla_w_uq,
               mla_kv_norm_g=v_mla_kv_norm_g, mla_w_ukv=v_mla_w_ukv, mix_w_out=v_mix_w_out, norm3_g=v_norm3_g,
               ffn2_w_in=v_ffn2_w_in, ffn2_w_out=v_ffn2_w_out, final_norm_g=v_final_norm_g)
    me = 4 * lax.axis_index("x") + 2 * lax.axis_index("y") + lax.axis_index("c")

    shard = {k: w[k][0].astype(BF16) for k in BIG}
    first = _gather_two_level([shard["ffn1_w_in"], jax.nn.silu(c)], "weights_gather")
    silu_c_all = first[-1][:, 0, :]

    def start_gather(names, after, name):
        return _exchange_start([shard[k] for k in names], [_own_slot(shard[k], me) for k in names], after, True, name)

    wout_start = start_gather(FIRST[1:], first[0], "ffn1_wout_start")
    mid_start = start_gather(MID, wout_start[4], "mixer_weights_start")
    last_start = start_gather(LAST, mid_start[4], "ffn2_weights_start")

    def wait_w_out(after):
        return _exchange_wait(*wout_start[:4], after, True, "ffn1_wout_wait")[0]

    pa = dict(ada_w=ada_w[0], c_ctx=c_ctx, ada_b=ada_b)
    (x1, c1, m_lat, m_ctx), res_a = _stage_a_fwd(pa, x[0], ctx[0], norm1_g, last_start[4], wait_w_out, silu_c_all, me)

    mid = _exchange_wait(mid_start[0], mid_start[1], mid_start[2], mid_start[3], x1, True, "mixer_weights_wait")
    pb = dict(zip(MID, mid))
    for k in ("norm2_g", "mla_q_norm_g", "mla_kv_norm_g", "ret_decay_fwd", "ret_decay_bwd"):
        pb[k] = w[k]
    x2, vjp_b = jax.vjp(_stage_b, pb, x1, c1, m_lat, m_ctx)

    last = _exchange_wait(last_start[0], last_start[1], last_start[2], last_start[3], x2, True, "ffn2_weights_wait")
    pc = dict(zip(LAST, last), norm3_g=norm3_g, final_norm_g=final_norm_g[None, :])
    loss_local, vjp_c = jax.vjp(lambda q, t, m: _stage_c(q, t, m, loss_target[0]), pc, x2, m_lat)

    gc, dx2, dm_c = vjp_c(jnp.ones((), F32))
    last_scat = _exchange_start([gc[k] for k in LAST],
                                [_own_slot(lax.dynamic_index_in_dim(gc[k], me, 0, False), me) for k in LAST],
                                dx2, False, "ffn2_grads_start")
    gb, dx1, dc1, dm_b, dmc_b = vjp_b(last_scat[4])
    mid_scat = _exchange_start([gb[k] for k in MID],
                               [_own_slot(lax.dynamic_index_in_dim(gb[k], me, 0, False), me) for k in MID],
                               dx1, False, "mixer_grads_start")
    wout_scat = []

    def start_w_out_grads(dw_out, after):
        wout_scat.append(_exchange_start(
            [dw_out], [_own_slot(lax.dynamic_index_in_dim(dw_out, me, 0, False), me)], after, False,
            "ffn1_wout_grads_start"))
        return wout_scat[0][4]

    g_ada, dx, dng1, dw1_in = _stage_a_bwd(res_a, mid_scat[4], dc1, dm_b + dm_c, dmc_b, start_w_out_grads)
    grads = {**g_ada, **gb, **gc, "x": dx, "norm1_g": dng1, "ffn1_w_in": dw1_in}
    grads["final_norm_g"] = grads["final_norm_g"][0]

    core = lax.axis_index("c").astype(jnp.int32).reshape(1)
    theirs = _swap_sibling([dw1_in], "grads_swap")
    paired = [_pair_add(dw1_in, theirs[0], core, "grads_pair_ffn1_w_in")]
    exchanged = {"ffn1_w_in": _scatter_chips(paired, "grads_scatter")[0]}
    exchanged["ffn1_w_out"] = _exchange_wait(*wout_scat[0][:4], dx, False, "ffn1_wout_grads_wait")[0]
    exchanged.update(zip(LAST, _exchange_wait(*last_scat[:4], dx, False, "ffn2_grads_wait")))
    exchanged.update(zip(MID, _exchange_wait(*mid_scat[:4], dx, False, "mixer_grads_wait")))
    zero1 = [jnp.zeros((1,), F32)]
    small_like = zero1 + [w[k] for k in SMALL]
    small_all = _exchange([_pack([loss_local.reshape(1)] + [grads[k] for k in SMALL])], True, "small_grads_gather")[0]
    loss = jnp.sum(small_all[:, 0, 0])

    out_g, out_d, out_m, out_v = {}, {}, {}, {}

    def update(name, gstack, shape2d):
        res = _adamw(gstack, w[name].reshape(shape2d), mom[name].reshape(shape2d), var[name].reshape(shape2d),
                     "adamw_" + name)
        out_g[name], out_d[name], out_m[name], out_v[name] = [t.reshape(w[name].shape) for t in res]

    for k in BIG:
        update(k, exchanged[k], exchanged[k].shape[1:])
    update("ada_w", grads["ada_w"][None], ada_w.shape[1:])
    res = _adamw(small_all, _pack(small_like), _pack(zero1 + [mom[k] for k in SMALL]),
                 _pack(zero1 + [var[k] for k in SMALL]), "adamw_small")
    for dst, flat in zip((out_g, out_d, out_m, out_v), res):
        for k, t in zip(SMALL, _unpack(flat, small_like)[1:]):
            dst[k] = t

    return (loss, grads["x"][None], *[out_g[k] for k in WEIGHTS], *[out_d[k] for k in WEIGHTS],
            *[out_m[k] for k in WEIGHTS], *[out_v[k] for k in WEIGHTS])
```

```python
import functools

import jax
import jax.numpy as jnp
from jax import lax
from jax.experimental import pallas as pl
from jax.experimental.pallas import tpu as pltpu

F32 = jnp.float32
BF16 = jnp.bfloat16

N_DEV = 8
MESH_AXES = ("x", "y", "c")

GRID_W = 64
N_MOD = 9
RET_HEADS = 8
RET_DK = 64
RET_DV = 128
RET_CHUNK = 256
RET_ROPE_BASE = 10000.0
MLA_HEADS = 8
MLA_Q_RANK = 512
MLA_KV_RANK = 256
MLA_NOPE = 128
MLA_ROPE = 64
MLA_V = 128
AXIAL_BASE = 10000.0
RMS_EPS = 1e-6
GN_EPS = 1e-5
SPLITS = (RET_HEADS * RET_DK, RET_HEADS * RET_DK, RET_HEADS * RET_DV, RET_HEADS * RET_DV,
          MLA_Q_RANK, MLA_KV_RANK, MLA_ROPE)
MIX_IN = sum(SPLITS)
MIX_IN_PAD = 4096

ADAM_LR = 0.001
ADAM_B1 = 0.9
ADAM_B2 = 0.999
ADAM_EPS = 1e-08
ADAM_WD = 0.01
ADAM_STEP = 10

LANE = 128
RET_DKP = LANE
VMEM_LIMIT_BYTES = 56 * 1024 * 1024

NN = ((1,), (0,))
NT = ((1,), (1,))
TN = ((0,), (0,))


def _pick(dim, target, align=LANE):
    t = min(dim, target)
    t -= t % align
    while t >= align:
        if dim % t == 0:
            return t
        t -= align
    return dim


def _params():
    return pltpu.CompilerParams(vmem_limit_bytes=VMEM_LIMIT_BYTES)


def _dot(a, b, dims):
    return lax.dot_general(a.astype(BF16), b.astype(BF16), (dims, ((), ())), preferred_element_type=F32)


def _mm_call(name, grid, ins, pairs, outs, acc_shapes, epilogue):
    n_in, n_out = len(ins), len(outs)
    k_axis = len(grid) - 1
    k_steps = grid[k_axis]

    def body(*refs):
        in_refs = refs[:n_in]
        out_refs = refs[n_in:n_in + n_out]
        accs = refs[n_in + n_out:]
        k = pl.program_id(k_axis)

        @pl.when(k == 0)
        def _():
            for acc in accs:
                acc[...] = jnp.zeros_like(acc)

        for ai, bi, dims, ci in pairs:
            accs[ci][...] += _dot(in_refs[ai][...], in_refs[bi][...], dims)

        @pl.when(k == k_steps - 1)
        def _():
            epilogue([acc[...] for acc in accs], in_refs, out_refs)

    res = pl.pallas_call(
        body, name=name, grid=grid,
        in_specs=[s for _, s in ins], out_specs=[s for _, s in outs],
        out_shape=[s for s, _ in outs],
        scratch_shapes=[pltpu.VMEM(s, F32) for s in acc_shapes],
        compiler_params=_params(),
    )(*[a for a, _ in ins])
    return res


def _matmul(a, b, mode, out_dtype, name, tm=1024, tn=1024, tk=512):
    if mode == "nn":
        (m, kd), n = a.shape, b.shape[1]
    elif mode == "nt":
        (m, kd), n = a.shape, b.shape[0]
    else:
        (kd, m), n = a.shape, b.shape[1]
    tm, tn = _pick(m, tm, 16), _pick(n, tn)
    tk = _pick(kd, tk) if mode != "tn" else _pick(kd, tk, 16)
    if mode == "nn":
        a_spec = pl.BlockSpec((tm, tk), lambda i, j, k: (i, k))
        b_spec = pl.BlockSpec((tk, tn), lambda i, j, k: (k, j))
        dims = NN
    elif mode == "nt":
        a_spec = pl.BlockSpec((tm, tk), lambda i, j, k: (i, k))
        b_spec = pl.BlockSpec((tn, tk), lambda i, j, k: (j, k))
        dims = NT
    else:
        a_spec = pl.BlockSpec((tk, tm), lambda i, j, k: (k, i))
        b_spec = pl.BlockSpec((tk, tn), lambda i, j, k: (k, j))
        dims = TN

    def epilogue(accs, in_refs, out_refs):
        out_refs[0][...] = accs[0].astype(out_dtype)

    return _mm_call(
        name, (m // tm, n // tn, kd // tk), [(a, a_spec), (b, b_spec)], [(0, 1, dims, 0)],
        [(jax.ShapeDtypeStruct((m, n), out_dtype), pl.BlockSpec((tm, tn), lambda i, j, k: (i, j)))],
        [(tm, tn)], epilogue)[0]


def _norm_mod_tile(x, ng, sc, sh):
    r = lax.rsqrt(jnp.mean(x * x, axis=-1, keepdims=True) + RMS_EPS)
    return (x * r * ng) * (1.0 + sc) + sh


def _row_spec(tm, d):
    return pl.BlockSpec((tm, d), lambda i: (i, 0))


def _vec_spec(d):
    return pl.BlockSpec((1, d), lambda i: (0, 0))


def _norm_mod_fwd(x, ng, sc, sh, name):
    t, d = x.shape
    tm = _pick(t, 512, 16)

    def body(x_ref, ng_ref, sc_ref, sh_ref, h_ref):
        h_ref[...] = _norm_mod_tile(x_ref[...], ng_ref[...], sc_ref[...], sh_ref[...]).astype(BF16)

    return pl.pallas_call(
        body, name=name, grid=(t // tm,),
        in_specs=[_row_spec(tm, d), _vec_spec(d), _vec_spec(d), _vec_spec(d)],
        out_specs=_row_spec(tm, d), out_shape=jax.ShapeDtypeStruct((t, d), BF16),
        compiler_params=_params(),
    )(x, ng, sc, sh)


def _norm_mod_bwd(x, ng, sc, sh, dh, dres, name):
    t, d = x.shape
    tm = _pick(t, 256, 16)
    has_res = dres is not None

    def body(*refs):
        if has_res:
            x_ref, ng_ref, sc_ref, sh_ref, dh_ref, dres_ref, dx_ref, dng_ref, dsc_ref, dsh_ref = refs
        else:
            x_ref, ng_ref, sc_ref, sh_ref, dh_ref, dx_ref, dng_ref, dsc_ref, dsh_ref = refs
        _, vjp = jax.vjp(_norm_mod_tile, x_ref[...], ng_ref[...], sc_ref[...], sh_ref[...])
        dx, dng, dsc, dsh = vjp(dh_ref[...].astype(F32))
        if has_res:
            dx = dx + dres_ref[...]
        dx_ref[...] = dx

        @pl.when(pl.program_id(0) == 0)
        def _():
            dng_ref[...] = jnp.zeros_like(dng_ref)
            dsc_ref[...] = jnp.zeros_like(dsc_ref)
            dsh_ref[...] = jnp.zeros_like(dsh_ref)

        dng_ref[...] += dng
        dsc_ref[...] += dsc
        dsh_ref[...] += dsh

    ins = [x, ng, sc, sh, dh] + ([dres] if has_res else [])
    in_specs = [_row_spec(tm, d), _vec_spec(d), _vec_spec(d), _vec_spec(d), _row_spec(tm, d)]
    in_specs += [_row_spec(tm, d)] if has_res else []
    vec = jax.ShapeDtypeStruct((1, d), F32)
    return pl.pallas_call(
        body, name=name, grid=(t // tm,), in_specs=in_specs,
        out_specs=[_row_spec(tm, d), _vec_spec(d), _vec_spec(d), _vec_spec(d)],
        out_shape=[jax.ShapeDtypeStruct((t, d), F32), vec, vec, vec],
        compiler_params=_params(),
    )(*ins)


def _res_mm_fwd(a, w, x, gate, coef, name):
    t, kd = a.shape
    d = w.shape[1]
    tm, tn, tk = _pick(t, 1024, 16), _pick(d, 1024), _pick(kd, 2816)

    def epilogue(accs, in_refs, out_refs):
        f = accs[0]
        out_refs[0][...] = in_refs[2][...] + (coef * in_refs[3][...]) * f
        out_refs[1][...] = f.astype(BF16)

    tile = pl.BlockSpec((tm, tn), lambda i, j, k: (i, j))
    return _mm_call(
        name, (t // tm, d // tn, kd // tk),
        [(a, pl.BlockSpec((tm, tk), lambda i, j, k: (i, k))), (w, pl.BlockSpec((tk, tn), lambda i, j, k: (k, j))),
         (x, tile), (gate, pl.BlockSpec((1, tn), lambda i, j, k: (0, j)))],
        [(0, 1, NN, 0)],
        [(jax.ShapeDtypeStruct((t, d), F32), tile), (jax.ShapeDtypeStruct((t, d), BF16), tile)],
        [(tm, tn)], epilogue)


def _gate_bwd(dxo, f, gate, coef, name):
    t, d = dxo.shape
    tm = _pick(t, 512, 16)

    def body(dxo_ref, f_ref, gate_ref, df_ref, dgate_ref):
        dxo_t = dxo_ref[...]
        df_ref[...] = ((coef * gate_ref[...]) * dxo_t).astype(BF16)

        @pl.when(pl.program_id(0) == 0)
        def _():
            dgate_ref[...] = jnp.zeros_like(dgate_ref)

        dgate_ref[...] += coef * jnp.sum(dxo_t * f_ref[...].astype(F32), axis=0, keepdims=True)

    return pl.pallas_call(
        body, name=name, grid=(t // tm,),
        in_specs=[_row_spec(tm, d), _row_spec(tm, d), _vec_spec(d)],
        out_specs=[_row_spec(tm, d), _vec_spec(d)],
        out_shape=[jax.ShapeDtypeStruct((t, d), BF16), jax.ShapeDtypeStruct((1, d), F32)],
        compiler_params=_params(),
    )(dxo, f, gate)


def _ffn_in_fwd(h, w_in, name):
    t, d = h.shape
    n = w_in.shape[2]
    half = N_DEV // 2
    f = half * n
    tm = _pick(t, 512, 16)

    def epilogue(accs, in_refs, out_refs):
        g, u = accs
        out_refs[0][...] = (g * jax.nn.sigmoid(g) * u).astype(BF16)
        out_refs[1][0] = g.astype(BF16)
        out_refs[1][1] = u.astype(BF16)

    return _mm_call(
        name, (half, t // tm, 1),
        [(h, pl.BlockSpec((tm, d), lambda j, i, k: (i, 0))),
         (w_in, pl.BlockSpec((None, d, n), lambda j, i, k: (j, 0, 0))),
         (w_in, pl.BlockSpec((None, d, n), lambda j, i, k: (j + half, 0, 0)))],
        [(0, 1, NN, 0), (0, 2, NN, 1)],
        [(jax.ShapeDtypeStruct((t, f), BF16), pl.BlockSpec((tm, n), lambda j, i, k: (i, j))),
         (jax.ShapeDtypeStruct((2, t, f), BF16), pl.BlockSpec((2, tm, n), lambda j, i, k: (0, i, j)))],
        [(tm, n), (tm, n)], epilogue)


def _ffn_da_bwd(df, w_out2d, gu, name):
    t, d = df.shape
    f = w_out2d.shape[0]
    half = N_DEV // 2
    n = f // half
    tm = _pick(t, 512, 16)
    step = 4 * LANE
    chunks = [(c, min(c + step, n)) for c in range(0, n, step)]

    def body(df_ref, w_ref, gu_ref, o_ref):
        df_t = df_ref[...]
        for c0, c1 in chunks:
            da = _dot(df_t, w_ref[c0:c1, :], NT)
            g = gu_ref[0, :, c0:c1].astype(F32)
            u = gu_ref[1, :, c0:c1].astype(F32)
            s = jax.nn.sigmoid(g)
            o_ref[0, :, c0:c1] = (da * u * (s * (1.0 + g * (1.0 - s)))).astype(BF16)
            o_ref[1, :, c0:c1] = (da * (g * s)).astype(BF16)

    gu_spec = pl.BlockSpec((2, tm, n), lambda j, i: (0, i, j))
    return pl.pallas_call(
        body, name=name, grid=(half, t // tm),
        in_specs=[pl.BlockSpec((tm, d), lambda j, i: (i, 0)), pl.BlockSpec((n, d), lambda j, i: (j, 0)), gu_spec],
        out_specs=gu_spec, out_shape=jax.ShapeDtypeStruct((2, t, f), BF16), compiler_params=_params(),
    )(df, w_out2d, gu)


def _ffn_dh_bwd(dgu, w_in, name):
    _, t, f = dgu.shape
    d, n = w_in.shape[1], w_in.shape[2]
    half = N_DEV // 2
    tm = _pick(t, 512, 16)

    def epilogue(accs, in_refs, out_refs):
        out_refs[0][...] = accs[0]

    return _mm_call(
        name, (t // tm, 1, half),
        [(dgu, pl.BlockSpec((None, tm, n), lambda i, j, k: (0, i, k))),
         (dgu, pl.BlockSpec((None, tm, n), lambda i, j, k: (1, i, k))),
         (w_in, pl.BlockSpec((None, d, n), lambda i, j, k: (k, 0, 0))),
         (w_in, pl.BlockSpec((None, d, n), lambda i, j, k: (k + half, 0, 0)))],
        [(0, 2, NT, 0), (1, 3, NT, 0)],
        [(jax.ShapeDtypeStruct((t, d), F32), pl.BlockSpec((tm, d), lambda i, j, k: (i, 0)))],
        [(tm, d)], epilogue)[0]


def _ffn_dwin_bwd(h, dgu, name):
    t, d = h.shape
    f = dgu.shape[2]
    half = N_DEV // 2
    n = f // half
    tk = _pick(t, 1024, 16)

    def epilogue(accs, in_refs, out_refs):
        out_refs[0][...] = accs[0].astype(BF16)

    return _mm_call(
        name, (N_DEV, 1, t // tk),
        [(h, pl.BlockSpec((tk, d), lambda j, i, k: (k, 0))),
         (dgu, pl.BlockSpec((None, tk, n), lambda j, i, k: (j // half, k, j % half)))],
        [(0, 1, TN, 0)],
        [(jax.ShapeDtypeStruct((N_DEV, d, n), BF16), pl.BlockSpec((None, d, n), lambda j, i, k: (j, 0, 0)))],
        [(d, n)], epilogue)[0]


def _ffn_parts(tag):
    def w2d(w_out):
        return w_out.reshape(w_out.shape[0] * w_out.shape[1], w_out.shape[2])

    def fwd_in(x, ng, sh, sc, w_in):
        h = _norm_mod_fwd(x, ng, sc, sh, tag + "_norm")
        a, gu = _ffn_in_fwd(h, w_in, tag + "_in")
        return h, a, gu

    def fwd_out(a, w_out, x, gate):
        return _res_mm_fwd(a, w2d(w_out), x, gate, 0.5, tag + "_out")

    def bwd_out(dxo, f1, gate, w_out, gu, a):
        df, dgate = _gate_bwd(dxo, f1, gate, 0.5, tag + "_dgate")
        dgu = _ffn_da_bwd(df, w2d(w_out), gu, tag + "_da")
        f = w_out.shape[0] * w_out.shape[1]
        dw_out = _matmul(a, df, "tn", BF16, tag + "_dwout", tm=_pick(f, 1408, 16), tn=2048, tk=1024)
        return dgate, dgu, dw_out.reshape(w_out.shape)

    def bwd_in(x, ng, sh, sc, w_in, h, dgu, dxo):
        dh = _ffn_dh_bwd(dgu, w_in, tag + "_dh")
        dw_in = _ffn_dwin_bwd(h, dgu, tag + "_dwin")
        dx, dng, dsc, dsh = _norm_mod_bwd(x, ng, sc, sh, dh, dxo, tag + "_dnorm")
        return dx, dng, dsh, dsc, dw_in

    return fwd_in, fwd_out, bwd_out, bwd_in


def _make_ffn_block(tag):
    fwd_in, fwd_out, bwd_out, bwd_in = _ffn_parts(tag)

    @jax.custom_vjp
    def ffn_block(x, ng, sh, sc, gate, w_in, w_out):
        return fwd(x, ng, sh, sc, gate, w_in, w_out)[0]

    def fwd(x, ng, sh, sc, gate, w_in, w_out):
        h, a, gu = fwd_in(x, ng, sh, sc, w_in)
        xo, f1 = fwd_out(a, w_out, x, gate)
        return xo, (x, ng, sh, sc, gate, w_in, w_out, h, a, gu, f1)

    def bwd(res, dxo):
        x, ng, sh, sc, gate, w_in, w_out, h, a, gu, f1 = res
        dgate, dgu, dw_out = bwd_out(dxo, f1, gate, w_out, gu, a)
        dx, dng, dsh, dsc, dw_in = bwd_in(x, ng, sh, sc, w_in, h, dgu, dxo)
        return dx, dng, dsh, dsc, dgate, dw_in, dw_out

    ffn_block.defvjp(fwd, bwd)
    return ffn_block


def _make_norm_proj(tag):
    @jax.custom_vjp
    def norm_proj(x, ng, sh, sc, w):
        return fwd(x, ng, sh, sc, w)[0]

    def fwd(x, ng, sh, sc, w):
        h = _norm_mod_fwd(x, ng, sc, sh, tag + "_norm")
        p = _matmul(h, w, "nn", F32, tag + "_mm", tm=1024, tn=1024, tk=w.shape[0])
        return p, (x, ng, sh, sc, w, h)

    def bwd(res, dp):
        x, ng, sh, sc, w, h = res
        dh = _matmul(dp, w, "nt", F32, tag + "_dh", tm=512, tn=w.shape[0], tk=2048)
        dw = _matmul(h, dp, "tn", BF16, tag + "_dw", tm=w.shape[0], tn=1024, tk=1024)
        dx, dng, dsc, dsh = _norm_mod_bwd(x, ng, sc, sh, dh, None, tag + "_dnorm")
        return dx, dng, dsh, dsc, dw

    norm_proj.defvjp(fwd, bwd)
    return norm_proj


def _make_norm_proj_carry(tag):
    @jax.custom_vjp
    def norm_proj(x, ng, sh, sc, w):
        return fwd(x, ng, sh, sc, w)[0]

    def fwd(x, ng, sh, sc, w):
        h = _norm_mod_fwd(x, ng, sc, sh, tag + "_norm")
        p = _matmul(h, w, "nn", F32, tag + "_mm", tm=1024, tn=1024, tk=w.shape[0])
        return (p, x), (x, ng, sh, sc, w, h)

    def bwd(res, cts):
        x, ng, sh, sc, w, h = res
        dp, dx_carry = cts
        dh = _matmul(dp, w, "nt", F32, tag + "_dh", tm=512, tn=w.shape[0], tk=2048)
        dw = _matmul(h, dp, "tn", BF16, tag + "_dw", tm=w.shape[0], tn=1024, tk=1024)
        dx, dng, dsc, dsh = _norm_mod_bwd(x, ng, sc, sh, dh, dx_carry, tag + "_dnorm")
        return dx, dng, dsh, dsc, dw

    norm_proj.defvjp(fwd, bwd)
    return norm_proj


def _make_split(tag, widths, total):
    offs = [sum(widths[:i]) for i in range(len(widths))]

    def concat_call(pieces):
        t = pieces[0].shape[0]
        tm = _pick(t, 256, 16)

        def body(*refs):
            o_ref = refs[-1]
            for ref, off, wd in zip(refs[:-1], offs, widths):
                o_ref[:, off:off + wd] = ref[...]
            end = offs[-1] + widths[-1]
            if end < total:
                o_ref[:, end:] = jnp.zeros((tm, total - end), F32)

        return pl.pallas_call(
            body, name=tag + "_concat", grid=(t // tm,),
            in_specs=[_row_spec(tm, wd) for wd in widths], out_specs=_row_spec(tm, total),
            out_shape=jax.ShapeDtypeStruct((t, total), F32), compiler_params=_params(),
        )(*pieces)

    @jax.custom_vjp
    def split(p):
        return tuple(p[:, off:off + wd] for off, wd in zip(offs, widths))

    def fwd(p):
        return split(p), None

    def bwd(_, cts):
        return (concat_call(list(cts)),)

    split.defvjp(fwd, bwd)
    return split


def _make_res_proj(tag):
    @jax.custom_vjp
    def res_proj(a, w, x, gate):
        return fwd(a, w, x, gate)[0]

    def fwd(a, w, x, gate):
        xo, f = _res_mm_fwd(a, w, x, gate, 1.0, tag + "_mm")
        return xo, (a, w, gate, f)

    def bwd(res, dxo):
        a, w, gate, f = res
        df, dgate = _gate_bwd(dxo, f, gate, 1.0, tag + "_dgate")
        da = _matmul(df, w, "nt", BF16, tag + "_da", tm=1024, tn=1024, tk=2048)
        dw = _matmul(a, df, "tn", BF16, tag + "_dw", tm=1024, tn=2048, tk=1024)
        return da, dw, dxo, dgate

    res_proj.defvjp(fwd, bwd)
    return res_proj


def _make_small_mm(tag):
    @jax.custom_vjp
    def small_mm(a, w):
        return _matmul(a, w, "nn", F32, tag + "_mm", tm=a.shape[0], tn=768, tk=w.shape[0])

    def fwd(a, w):
        return small_mm(a, w), (a, w)

    def bwd(res, dr):
        a, w = res
        da = _matmul(dr, w, "nt", F32, tag + "_da", tm=a.shape[0], tn=w.shape[0], tk=768)
        dw = _matmul(a, dr, "tn", F32, tag + "_dw", tm=1024, tn=768, tk=a.shape[0])
        return da, dw

    small_mm.defvjp(fwd, bwd)
    return small_mm


def _ret_chunk_terms(lg, c, reverse):
    row = lax.broadcasted_iota(jnp.int32, (c, c), 0).astype(F32)
    col = lax.broadcasted_iota(jnp.int32, (c, c), 1).astype(F32)
    pos = lax.broadcasted_iota(jnp.int32, (c, 1), 0).astype(F32)
    if reverse:
        diff = col - row
        mask = diff > 0.0
        e_exp = float(c) - pos
        f_exp = pos
    else:
        diff = row - col
        mask = diff >= 0.0
        e_exp = pos + 1.0
        f_exp = float(c - 1) - pos
    diffm = jnp.where(mask, diff, 0.0)
    dm = jnp.where(mask, jnp.exp(lg * diffm), 0.0)
    return diffm, dm, e_exp, jnp.exp(lg * e_exp), f_exp, jnp.exp(lg * f_exp)


def _lane0(val):
    lane = lax.broadcasted_iota(jnp.int32, (1, LANE), 1)
    return jnp.where(lane == 0, val, 0.0)


RET_HEAD_BLOCK = 4


def _make_ret_dir(tag, reverse):
    hb = RET_HEAD_BLOCK

    def heads_spec(nc, width, flip):
        if flip:
            return pl.BlockSpec((hb, RET_CHUNK, width), lambda h, t: (h, nc - 1 - t, 0))
        return pl.BlockSpec((hb, RET_CHUNK, width), lambda h, t: (h, t, 0))

    def state_spec(nc, flip):
        if flip:
            return pl.BlockSpec((hb, None, RET_DKP, RET_DV), lambda h, t: (h, nc - 1 - t, 0, 0))
        return pl.BlockSpec((hb, None, RET_DKP, RET_DV), lambda h, t: (h, t, 0, 0))

    lg_spec = pl.BlockSpec((hb, 1, LANE), lambda h, t: (h, 0, 0))
    s0_spec = pl.BlockSpec((hb, RET_DKP, RET_DV), lambda h, t: (h, 0, 0))

    def fwd_call(q, k, v, lgb, s0):
        hh, ll, _ = q.shape
        c = RET_CHUNK
        nc = ll // c

        def body(q_ref, k_ref, v_ref, lg_ref, s0_ref, y_ref, sall_ref, s_scr):
            @pl.when(pl.program_id(1) == 0)
            def _():
                s_scr[...] = s0_ref[...]

            for b in range(hb):
                lg = lg_ref[b][:, :1]
                _, dm, _, xi, _, zeta = _ret_chunk_terms(lg, c, reverse)
                q_t, k_t, v_t = q_ref[b], k_ref[b], v_ref[b]
                s = s_scr[b]
                p = _dot(q_t, k_t, NT) * dm
                y_ref[b] = _dot(p, v_t, NN) + _dot(q_t * xi, s, NN)
                sall_ref[b] = s
                s_scr[b] = jnp.exp(lg * float(c)) * s + _dot(k_t * zeta, v_t, TN)

        return pl.pallas_call(
            body, name=tag + "_fwd", grid=(hh // hb, nc),
            in_specs=[heads_spec(nc, RET_DKP,reverse), heads_spec(nc, RET_DKP,reverse),
                      heads_spec(nc, RET_DV, reverse), lg_spec, s0_spec],
            out_specs=[heads_spec(nc, RET_DV, reverse), state_spec(nc, reverse)],
            out_shape=[jax.ShapeDtypeStruct((hh, ll, RET_DV), F32),
                       jax.ShapeDtypeStruct((hh, nc, RET_DKP, RET_DV), F32)],
            scratch_shapes=[pltpu.VMEM((hb, RET_DKP, RET_DV), F32)],
            compiler_params=_params(),
        )(q, k, v, lgb, s0)

    def bwd_call(q, k, v, lgb, sall, dy):
        hh, ll, _ = q.shape
        c = RET_CHUNK
        nc = ll // c
        flip = not reverse

        def body(q_ref, k_ref, v_ref, lg_ref, sall_ref, dy_ref, dq_ref, dk_ref, dv_ref, dlg_ref, ds0_ref, ds_scr):
            @pl.when(pl.program_id(1) == 0)
            def _():
                ds_scr[...] = jnp.zeros_like(ds_scr)
                dlg_ref[...] = jnp.zeros_like(dlg_ref)

            def total(m):
                return jnp.sum(jnp.sum(m, axis=1, keepdims=True), axis=0, keepdims=True)

            for b in range(hb):
                lg = lg_ref[b][:, :1]
                diffm, dm, e_exp, xi, f_exp, zeta = _ret_chunk_terms(lg, c, reverse)
                q_t, k_t, v_t, dy_t = q_ref[b], k_ref[b], v_ref[b], dy_ref[b]
                s = sall_ref[b]
                dsn = ds_scr[b]
                a = _dot(q_t, k_t, NT)
                da = _dot(dy_t, v_t, NT) * dm
                g = _dot(dy_t, s, NT)
                hm = _dot(v_t, dsn, NT)
                dq_ref[b] = _dot(da, k_t, NN) + xi * g
                dk_ref[b] = _dot(da, q_t, TN) + zeta * hm
                dv_ref[b] = _dot(a * dm, dy_t, TN) + _dot(k_t * zeta, dsn, NN)
                gc = jnp.exp(lg * float(c))
                ds_scr[b] = gc * dsn + _dot(q_t * xi, dy_t, TN)
                dl = (total(da * a * diffm) + total(e_exp * xi * q_t * g)
                      + float(c) * gc * total(s * dsn) + total(f_exp * zeta * k_t * hm))
                dlg_ref[b] += _lane0(dl)

            @pl.when(pl.program_id(1) == nc - 1)
            def _():
                ds0_ref[...] = ds_scr[...]

        return pl.pallas_call(
            body, name=tag + "_bwd", grid=(hh // hb, nc),
            in_specs=[heads_spec(nc, RET_DKP,flip), heads_spec(nc, RET_DKP,flip), heads_spec(nc, RET_DV, flip),
                      lg_spec, state_spec(nc, flip), heads_spec(nc, RET_DV, flip)],
            out_specs=[heads_spec(nc, RET_DKP,flip), heads_spec(nc, RET_DKP,flip), heads_spec(nc, RET_DV, flip),
                       lg_spec, s0_spec],
            out_shape=[jax.ShapeDtypeStruct((hh, ll, RET_DKP), F32), jax.ShapeDtypeStruct((hh, ll, RET_DKP), F32),
                       jax.ShapeDtypeStruct((hh, ll, RET_DV), F32), jax.ShapeDtypeStruct((hh, 1, LANE), F32),
                       jax.ShapeDtypeStruct((hh, RET_DKP, RET_DV), F32)],
            scratch_shapes=[pltpu.VMEM((hb, RET_DKP, RET_DV), F32)],
            compiler_params=_params(),
        )(q, k, v, lgb, sall, dy)

    @jax.custom_vjp
    def ret_dir(q, k, v, lgb, s0):
        return fwd_call(q, k, v, lgb, s0)[0]

    def fwd(q, k, v, lgb, s0):
        y, sall = fwd_call(q, k, v, lgb, s0)
        return y, (q, k, v, lgb, sall)

    def bwd(res, dy):
        q, k, v, lgb, sall = res
        return tuple(bwd_call(q, k, v, lgb, sall, dy))

    ret_dir.defvjp(fwd, bwd)
    return ret_dir


def _make_ctx_state(tag, reverse):
    hb = RET_HEAD_BLOCK
    c = RET_CHUNK
    k_spec = pl.BlockSpec((hb, c, RET_DKP), lambda h: (h, 0, 0))
    v_spec = pl.BlockSpec((hb, c, RET_DV), lambda h: (h, 0, 0))
    lg_spec = pl.BlockSpec((hb, 1, LANE), lambda h: (h, 0, 0))
    s_spec = pl.BlockSpec((hb, RET_DKP, RET_DV), lambda h: (h, 0, 0))

    def fwd_call(k, v, lgb):
        hh = k.shape[0]

        def body(k_ref, v_ref, lg_ref, s_ref):
            for b in range(hb):
                _, _, _, _, _, zeta = _ret_chunk_terms(lg_ref[b][:, :1], c, reverse)
                s_ref[b] = _dot(k_ref[b] * zeta, v_ref[b], TN)

        return pl.pallas_call(
            body, name=tag + "_fwd", grid=(hh // hb,), in_specs=[k_spec, v_spec, lg_spec], out_specs=s_spec,
            out_shape=jax.ShapeDtypeStruct((hh, RET_DKP, RET_DV), F32), compiler_params=_params(),
        )(k, v, lgb)

    def bwd_call(k, v, lgb, ds):
        hh = k.shape[0]

        def body(k_ref, v_ref, lg_ref, ds_ref, dk_ref, dv_ref, dlg_ref):
            for b in range(hb):
                _, _, _, _, f_exp, zeta = _ret_chunk_terms(lg_ref[b][:, :1], c, reverse)
                k_t, v_t, ds = k_ref[b], v_ref[b], ds_ref[b]
                hm = _dot(v_t, ds, NT)
                dk_ref[b] = zeta * hm
                dv_ref[b] = _dot(k_t * zeta, ds, NN)
                tot = jnp.sum(jnp.sum(f_exp * zeta * k_t * hm, axis=1, keepdims=True), axis=0, keepdims=True)
                dlg_ref[b] = _lane0(tot)

        return pl.pallas_call(
            body, name=tag + "_bwd", grid=(hh // hb,), in_specs=[k_spec, v_spec, lg_spec, s_spec],
            out_specs=[k_spec, v_spec, lg_spec],
            out_shape=[jax.ShapeDtypeStruct(k.shape, F32), jax.ShapeDtypeStruct(v.shape, F32),
                       jax.ShapeDtypeStruct((hh, 1, LANE), F32)],
            compiler_params=_params(),
        )(k, v, lgb, ds)

    @jax.custom_vjp
    def ctx_state(k, v, lgb):
        return fwd_call(k, v, lgb)

    def fwd(k, v, lgb):
        return fwd_call(k, v, lgb), (k, v, lgb)

    def bwd(res, ds):
        return tuple(bwd_call(*res, ds))

    ctx_state.defvjp(fwd, bwd)
    return ctx_state


def _rope_tables_call(name, n, inv, shift, axial):
    tm = _pick(n, 1024, 8)
    inv_lane = jnp.tile(inv, LANE // inv.shape[0])[None, :]

    def body(inv_ref, cos_ref, s1_ref, s2_ref):
        t = lax.broadcasted_iota(jnp.int32, (tm, LANE), 0) + pl.program_id(0) * tm
        lane = lax.broadcasted_iota(jnp.int32, (tm, LANE), 1)
        if axial:
            pos = jnp.where(lane % (2 * MLA_ROPE // 2) < MLA_ROPE // 2, t // GRID_W, t % GRID_W)
        else:
            pos = t
        ang = pos.astype(F32) * inv_ref[...]
        sin = jnp.sin(ang)
        first = lane % (2 * shift) < shift
        cos_ref[...] = jnp.cos(ang)
        s1_ref[...] = jnp.where(first, -sin, 0.0)
        s2_ref[...] = jnp.where(first, 0.0, sin)

    tab = jax.ShapeDtypeStruct((n, LANE), F32)
    return tuple(pl.pallas_call(
        body, name=name, grid=(n // tm,), in_specs=[_vec_spec(LANE)], out_specs=[_row_spec(tm, LANE)] * 3,
        out_shape=[tab, tab, tab], compiler_params=_params(),
    )(inv_lane))


def _ret_tables(n_lat):
    inv = RET_ROPE_BASE ** (-jnp.arange(0, RET_DK, 2, dtype=F32) / RET_DK)
    return _rope_tables_call("ret_tables", n_lat, inv, RET_DK // 2, False)


def _make_ret_pack(tag, n_lat, n_ctx):
    hh = RET_HEADS
    tm = MLA_PACK_ROWS
    k_scale = RET_DK ** -0.5
    shift = RET_DK // 2
    tabs = _ret_tables(n_lat)

    def low_lanes():
        return lax.broadcasted_iota(jnp.int32, (1, LANE), 1) < RET_DK

    def rows(width):
        return pl.BlockSpec((tm, width), lambda i: (i, 0))

    def heads(width):
        return pl.BlockSpec((hh, tm, width), lambda i: (0, i, 0))

    def split_pairs(src_ref, dst_ref, scale, rope):
        keep = low_lanes()
        for j in range(hh // 2):
            blk = src_ref[:, LANE * j:LANE * (j + 1)]
            if scale != 1.0:
                blk = blk * scale
            if rope is not None:
                blk = _rope128(blk, *rope, shift=shift)
            dst_ref[2 * j] = jnp.where(keep, blk, 0.0)
            dst_ref[2 * j + 1] = jnp.where(keep, pltpu.roll(blk, RET_DK, 1), 0.0)

    def merge_pairs(src_refs, dst_ref, scale, rope):
        keep = low_lanes()
        for j in range(hh // 2):
            even = sum(r[2 * j] for r in src_refs)
            odd = sum(r[2 * j + 1] for r in src_refs)
            g = jnp.where(keep, even, pltpu.roll(odd, RET_DK, 1))
            if rope is not None:
                g = _rope128_t(g, *rope, shift=shift)
            dst_ref[:, LANE * j:LANE * (j + 1)] = g * scale if scale != 1.0 else g

    def pack_call(name, n, q, k, v, rope):
        with_q = q is not None

        def body(*refs):
            refs = list(refs)
            q_ref = refs.pop(0) if with_q else None
            k_ref, v_ref = refs.pop(0), refs.pop(0)
            tab = tuple(r[...] for r in refs[:3]) if rope else None
            outs = refs[3:] if rope else refs
            if with_q:
                split_pairs(q_ref, outs[0], 1.0, tab)
                outs = outs[1:]
            split_pairs(k_ref, outs[0], k_scale, tab)
            for h in range(hh):
                outs[1][h] = v_ref[:, RET_DV * h:RET_DV * (h + 1)]

        ins = ([q] if with_q else []) + [k, v] + (list(tabs) if rope else [])
        in_specs = ([rows(q.shape[1])] if with_q else []) + [rows(k.shape[1]), rows(v.shape[1])]
        in_specs += [rows(LANE)] * 3 if rope else []
        n_out = 3 if with_q else 2
        return pl.pallas_call(
            body, name=name, grid=(n // tm,), in_specs=in_specs,
            out_specs=[heads(RET_DKP)] * (n_out - 1) + [heads(RET_DV)],
            out_shape=[jax.ShapeDtypeStruct((hh, n, RET_DKP), F32)] * (n_out - 1)
            + [jax.ShapeDtypeStruct((hh, n, RET_DV), F32)],
            compiler_params=_params(),
        )(*ins)

    def unpack_call(name, n, dqs, dks, dvs, rope):
        with_q = len(dqs) > 0
        uses = len(dks)

        def body(*refs):
            refs = list(refs)
            dq_refs = [refs.pop(0) for _ in range(len(dqs))]
            dk_refs = [refs.pop(0) for _ in range(uses)]
            dv_refs = [refs.pop(0) for _ in range(uses)]
            tab = tuple(r[...] for r in refs[:3]) if rope else None
            outs = refs[3:] if rope else refs
            if with_q:
                merge_pairs(dq_refs, outs[0], 1.0, tab)
                outs = outs[1:]
            merge_pairs(dk_refs, outs[0], k_scale, tab)
            for h in range(hh):
                outs[1][:, RET_DV * h:RET_DV * (h + 1)] = sum(r[h] for r in dv_refs)

        ins = list(dqs) + list(dks) + list(dvs) + (list(tabs) if rope else [])
        in_specs = [heads(RET_DKP)] * (len(dqs) + uses) + [heads(RET_DV)] * uses + ([rows(LANE)] * 3 if rope else [])
        n_out = 3 if with_q else 2
        return pl.pallas_call(
            body, name=name, grid=(n // tm,), in_specs=in_specs,
            out_specs=[rows(hh * RET_DK)] * (n_out - 1) + [rows(hh * RET_DV)],
            out_shape=[jax.ShapeDtypeStruct((n, hh * RET_DK), F32)] * (n_out - 1)
            + [jax.ShapeDtypeStruct((n, hh * RET_DV), F32)],
            compiler_params=_params(),
        )(*ins)

    @jax.custom_vjp
    def ret_pack(rq, rk, rv, crk, crv):
        q, k, v = pack_call(tag + "_lat", n_lat, rq, rk, rv, True)
        k_c, v_c = pack_call(tag + "_ctx", n_ctx, None, crk, crv, False)
        return (q, k, v), (q, k, v), (k_c, v_c), (k_c, v_c)

    def fwd(rq, rk, rv, crk, crv):
        return ret_pack(rq, rk, rv, crk, crv), None

    def bwd(_, cts):
        lat_f, lat_b, ctx_f, ctx_b = cts
        drq, drk, drv = unpack_call(tag + "_dlat", n_lat, [lat_f[0], lat_b[0]], [lat_f[1], lat_b[1]],
                                    [lat_f[2], lat_b[2]], True)
        dcrk, dcrv = unpack_call(tag + "_dctx", n_ctx, [], [ctx_f[0], ctx_b[0]], [ctx_f[1], ctx_b[1]], False)
        return drq, drk, drv, dcrk, dcrv

    ret_pack.defvjp(fwd, bwd)
    return ret_pack


def _ret_out_tile(y, g):
    mu = jnp.mean(y, axis=-1, keepdims=True)
    var = jnp.mean(jnp.square(y - mu), axis=-1, keepdims=True)
    return (g * jax.nn.sigmoid(g)) * ((y - mu) * lax.rsqrt(var + GN_EPS))


def _make_ret_out(tag):
    def specs(tm):
        y_spec = pl.BlockSpec((None, tm, RET_DV), lambda h, i: (h, i, 0))
        g_spec = pl.BlockSpec((tm, RET_DV), lambda h, i: (i, h))
        return y_spec, g_spec

    def fwd_call(yf, yb, g):
        hh, n, _ = yf.shape
        tm = _pick(n, 1024, 16)
        y_spec, g_spec = specs(tm)

        def body(yf_ref, yb_ref, g_ref, o_ref):
            o_ref[...] = _ret_out_tile(yf_ref[...] + yb_ref[...], g_ref[...]).astype(BF16)

        return pl.pallas_call(
            body, name=tag + "_fwd", grid=(hh, n // tm), in_specs=[y_spec, y_spec, g_spec], out_specs=g_spec,
            out_shape=jax.ShapeDtypeStruct((n, hh * RET_DV), BF16), compiler_params=_params(),
        )(yf, yb, g)

    def bwd_call(yf, yb, g, do):
        hh, n, _ = yf.shape
        tm = _pick(n, 1024, 16)
        y_spec, g_spec = specs(tm)

        def body(yf_ref, yb_ref, g_ref, do_ref, dy_ref, dg_ref):
            _, vjp = jax.vjp(_ret_out_tile, yf_ref[...] + yb_ref[...], g_ref[...])
            dy, dg = vjp(do_ref[...].astype(F32))
            dy_ref[...] = dy
            dg_ref[...] = dg

        return pl.pallas_call(
            body, name=tag + "_bwd", grid=(hh, n // tm), in_specs=[y_spec, y_spec, g_spec, g_spec],
            out_specs=[y_spec, g_spec],
            out_shape=[jax.ShapeDtypeStruct(yf.shape, F32), jax.ShapeDtypeStruct(g.shape, F32)],
            compiler_params=_params(),
        )(yf, yb, g, do)

    @jax.custom_vjp
    def ret_out(yf, yb, g):
        return fwd_call(yf, yb, g)

    def fwd(yf, yb, g):
        return fwd_call(yf, yb, g), (yf, yb, g)

    def bwd(res, do):
        dy, dg = bwd_call(*res, do)
        return dy, dy, dg

    ret_out.defvjp(fwd, bwd)
    return ret_out


MLA_DQ_PAD = 2 * LANE
MLA_PACK_ROWS = 256


def _rope128(x, cos, s1, s2, shift=16):
    return x * cos + pltpu.roll(x, LANE - shift, 1) * s1 + pltpu.roll(x, shift, 1) * s2


def _rope128_t(g, cos, s1, s2, shift=16):
    return g * cos + pltpu.roll(g * s1, shift, 1) + pltpu.roll(g * s2, LANE - shift, 1)


def _axial_tables(n_lat):
    half = MLA_ROPE // 2
    inv = AXIAL_BASE ** (-jnp.arange(0, half, 2, dtype=F32) / half)
    return _rope_tables_call("mla_tables", n_lat, inv, half // 2, True)


def _make_mla_pack(tag, n_lat, n_ctx, scale):
    hh = MLA_HEADS
    tm = MLA_PACK_ROWS
    ll = n_lat + n_ctx
    rope0 = hh * MLA_NOPE
    tabs = _axial_tables(n_lat)

    def rope_lanes():
        return lax.broadcasted_iota(jnp.int32, (1, LANE), 1) < MLA_ROPE

    def rows(width):
        return pl.BlockSpec((tm, width), lambda i: (i, 0))

    def heads(width, off):
        return pl.BlockSpec((hh, tm, width), lambda i: (0, i + off, 0))

    def heads_t(width, off):
        return pl.BlockSpec((hh, width, tm), lambda i: (0, 0, i + off))

    def put_kv(kv_ref, kr_rot, k_ref, v_ref, kt_ref, vt_ref):
        kr_b = kr_rot.astype(BF16)
        kr_t = jnp.transpose(kr_rot).astype(BF16)
        for h in range(hh):
            k_nope = kv_ref[:, 2 * LANE * h:2 * LANE * h + MLA_NOPE]
            val = kv_ref[:, 2 * LANE * h + MLA_NOPE:2 * LANE * (h + 1)]
            k_ref[h, :, :MLA_NOPE] = k_nope.astype(BF16)
            k_ref[h, :, MLA_NOPE:] = kr_b
            v_ref[h] = val.astype(BF16)
            kt_ref[h, :MLA_NOPE, :] = jnp.transpose(k_nope).astype(BF16)
            kt_ref[h, MLA_NOPE:, :] = kr_t
            vt_ref[h] = jnp.transpose(val).astype(BF16)

    def fwd_lat(qp, kv, kr):
        def body(qp_ref, kv_ref, kr_ref, cos_ref, s1_ref, s2_ref, q_ref, k_ref, v_ref, kt_ref, vt_ref):
            cos, s1, s2 = cos_ref[...], s1_ref[...], s2_ref[...]
            keep = rope_lanes()
            for j in range(hh // 2):
                rot = _rope128(qp_ref[:, rope0 + LANE * j:rope0 + LANE * (j + 1)], cos, s1, s2)
                q_ref[2 * j, :, MLA_NOPE:] = jnp.where(keep, rot, 0.0).astype(BF16)
                q_ref[2 * j + 1, :, MLA_NOPE:] = jnp.where(keep, pltpu.roll(rot, MLA_ROPE, 1), 0.0).astype(BF16)
            for h in range(hh):
                q_ref[h, :, :MLA_NOPE] = qp_ref[:, MLA_NOPE * h:MLA_NOPE * (h + 1)].astype(BF16)
            kr_rot = jnp.where(keep, _rope128(kr_ref[...], cos, s1, s2), 0.0)
            put_kv(kv_ref, kr_rot, k_ref, v_ref, kt_ref, vt_ref)

        return pl.pallas_call(
            body, name=tag + "_lat", grid=(n_lat // tm,),
            in_specs=[rows(qp.shape[1]), rows(kv.shape[1]), rows(LANE), rows(LANE), rows(LANE), rows(LANE)],
            out_specs=[heads(MLA_DQ_PAD, 0), heads(MLA_DQ_PAD, 0), heads(MLA_V, 0), heads_t(MLA_DQ_PAD, 0),
                       heads_t(MLA_V, 0)],
            out_shape=[jax.ShapeDtypeStruct((hh, n_lat, MLA_DQ_PAD), BF16),
                       jax.ShapeDtypeStruct((hh, ll, MLA_DQ_PAD), BF16), jax.ShapeDtypeStruct((hh, ll, MLA_V), BF16),
                       jax.ShapeDtypeStruct((hh, MLA_DQ_PAD, ll), BF16), jax.ShapeDtypeStruct((hh, MLA_V, ll), BF16)],
            compiler_params=_params(),
        )(qp, kv, kr, *tabs)

    def fwd_ctx(kv_c, kr_c, bufs):
        def body(kv_ref, kr_ref, k_in, v_in, kt_in, vt_in, k_ref, v_ref, kt_ref, vt_ref):
            kr_rot = jnp.where(rope_lanes(), kr_ref[...], 0.0)
            put_kv(kv_ref, kr_rot, k_ref, v_ref, kt_ref, vt_ref)

        any_spec = pl.BlockSpec(memory_space=pl.ANY)
        off = n_lat // tm
        return pl.pallas_call(
            body, name=tag + "_ctx", grid=(n_ctx // tm,),
            in_specs=[rows(kv_c.shape[1]), rows(LANE)] + [any_spec] * 4,
            out_specs=[heads(MLA_DQ_PAD, off), heads(MLA_V, off), heads_t(MLA_DQ_PAD, off), heads_t(MLA_V, off)],
            out_shape=[jax.ShapeDtypeStruct(b.shape, BF16) for b in bufs],
            input_output_aliases={2: 0, 3: 1, 4: 2, 5: 3}, compiler_params=_params(),
        )(kv_c, kr_c, *bufs)

    def take_kv(dk_ref, dv_ref, dkv_ref):
        dkr = jnp.zeros((tm, LANE), F32)
        for h in range(hh):
            dkv_ref[:, 2 * LANE * h:2 * LANE * h + MLA_NOPE] = dk_ref[h, :, :MLA_NOPE].astype(F32)
            dkv_ref[:, 2 * LANE * h + MLA_NOPE:2 * LANE * (h + 1)] = dv_ref[h].astype(F32)
            dkr = dkr + dk_ref[h, :, MLA_NOPE:].astype(F32)
        return jnp.where(rope_lanes(), dkr, 0.0)

    def bwd_lat(dqt, dk, dv, qp_width, kv_width):
        def body(dqt_ref, dk_ref, dv_ref, cos_ref, s1_ref, s2_ref, dqp_ref, dkv_ref, dkr_ref):
            cos, s1, s2 = cos_ref[...], s1_ref[...], s2_ref[...]
            keep = rope_lanes()
            for j in range(hh // 2):
                even = jnp.transpose(dqt_ref[2 * j]) * scale
                odd = jnp.transpose(dqt_ref[2 * j + 1]) * scale
                dqp_ref[:, MLA_NOPE * 2 * j:MLA_NOPE * (2 * j + 1)] = even[:, :MLA_NOPE]
                dqp_ref[:, MLA_NOPE * (2 * j + 1):MLA_NOPE * (2 * j + 2)] = odd[:, :MLA_NOPE]
                g = jnp.where(keep, even[:, MLA_NOPE:], pltpu.roll(odd[:, MLA_NOPE:], MLA_ROPE, 1))
                dqp_ref[:, rope0 + LANE * j:rope0 + LANE * (j + 1)] = _rope128_t(g, cos, s1, s2)
            dkr_ref[...] = jnp.where(keep, _rope128_t(take_kv(dk_ref, dv_ref, dkv_ref), cos, s1, s2), 0.0)

        return pl.pallas_call(
            body, name=tag + "_dlat", grid=(n_lat // tm,),
            in_specs=[pl.BlockSpec((hh, MLA_DQ_PAD, tm), lambda i: (0, 0, i)),
                      heads(MLA_DQ_PAD, 0), heads(MLA_V, 0), rows(LANE), rows(LANE), rows(LANE)],
            out_specs=[rows(qp_width), rows(kv_width), rows(LANE)],
            out_shape=[jax.ShapeDtypeStruct((n_lat, qp_width), F32), jax.ShapeDtypeStruct((n_lat, kv_width), F32),
                       jax.ShapeDtypeStruct((n_lat, LANE), F32)],
            compiler_params=_params(),
        )(dqt, dk, dv, *tabs)

    def bwd_ctx(dk, dv, kv_width):
        def body(dk_ref, dv_ref, dkv_ref, dkr_ref):
            dkr_ref[...] = take_kv(dk_ref, dv_ref, dkv_ref)

        off = n_lat // tm
        return pl.pallas_call(
            body, name=tag + "_dctx", grid=(n_ctx // tm,),
            in_specs=[heads(MLA_DQ_PAD, off), heads(MLA_V, off)],
            out_specs=[rows(kv_width), rows(LANE)],
            out_shape=[jax.ShapeDtypeStruct((n_ctx, kv_width), F32), jax.ShapeDtypeStruct((n_ctx, LANE), F32)],
            compiler_params=_params(),
        )(dk, dv)

    def pack(qp, kv, kr, kv_c, kr_c):
        q, *bufs = fwd_lat(qp, kv, kr)
        return (q, *fwd_ctx(kv_c, kr_c, bufs))

    def unpack(dqt, dk, dv):
        qp_width, kv_width = hh * (MLA_NOPE + MLA_ROPE), hh * (MLA_NOPE + MLA_V)
        dqp, dkv, dkr = bwd_lat(dqt, dk, dv, qp_width, kv_width)
        dkv_c, dkr_c = bwd_ctx(dk, dv, kv_width)
        return dqp, dkv, dkr, dkv_c, dkr_c

    return pack, unpack


def _make_mla(tag, n_lat, n_ctx):
    scale = (MLA_NOPE + MLA_ROPE) ** -0.5
    pack, unpack = _make_mla_pack(tag + "pack", n_lat, n_ctx, scale)
    attn_fwd, attn_delta, attn_bwd = _make_attention(tag, scale)

    @jax.custom_vjp
    def mla(qp, kv, kr, kv_c, kr_c):
        q, k, _, _, vt = pack(qp, kv, kr, kv_c, kr_c)
        return attn_fwd(q, k, vt)[0]

    def fwd(qp, kv, kr, kv_c, kr_c):
        q, k, v, kt, vt = pack(qp, kv, kr, kv_c, kr_c)
        o, lse = attn_fwd(q, k, vt)
        return o, (q, k, kt, v, o, lse)

    def bwd(res, do):
        q, k, kt, v, o, lse = res
        delta = attn_delta(o, do, q.shape[0])
        dqt, dk, dv = attn_bwd(q, k, kt, v, do, lse, delta)
        return unpack(dqt, dk, dv)

    mla.defvjp(fwd, bwd)
    return mla


def _make_attention(tag, scale):
    neg_big = -1e30
    log2e = 1.4426950408889634
    sub = 256

    def fwd_call(q, k, vt):
        hh, n, dq = q.shape
        dv, ll = vt.shape[1], vt.shape[2]
        tq, tk = _pick(n, 1024), _pick(ll, 1408)
        sb = sub if tk % sub == 0 else tk
        c2 = scale * log2e
        k_steps = ll // tk

        def body(q_ref, k_ref, vt_ref, o_ref, lse_ref, m_scr, l_scr, acc_scr, s_scr, p_scr):
            j = pl.program_id(2)

            @pl.when(j == 0)
            def _():
                m_scr[...] = jnp.full_like(m_scr, neg_big)
                l_scr[...] = jnp.zeros_like(l_scr)
                acc_scr[...] = jnp.zeros_like(acc_scr)

            q_t = q_ref[...]
            m_prev = m_scr[...]
            m_new = m_prev
            for kk in range(tk // sb):
                rows = slice(kk * sb, (kk + 1) * sb)
                s_t = _dot(k_ref[rows, :], q_t, NT)
                s_scr[rows, :] = s_t
                m_new = jnp.maximum(m_new, jnp.max(s_t, axis=0, keepdims=True))
            mc = m_new * c2
            l_part = jnp.zeros_like(m_new)
            for kk in range(tk // sb):
                rows = slice(kk * sb, (kk + 1) * sb)
                p_t = jnp.exp2(s_scr[rows, :] * c2 - mc)
                l_part = l_part + jnp.sum(p_t, axis=0, keepdims=True)
                p_scr[rows, :] = p_t.astype(BF16)
            alpha = jnp.exp2((m_prev - m_new) * c2)
            l_scr[...] = alpha * l_scr[...] + l_part
            acc_scr[...] = alpha * acc_scr[...] + _dot(vt_ref[...], p_scr[...], NN)
            m_scr[...] = m_new

            @pl.when(j == k_steps - 1)
            def _():
                o_ref[...] = jnp.transpose(acc_scr[...] / l_scr[...]).astype(BF16)
                lse_ref[...] = m_scr[...] * scale + jnp.log(l_scr[...])

        return pl.pallas_call(
            body, name=tag + "_fwd", grid=(hh, n // tq, k_steps),
            in_specs=[pl.BlockSpec((None, tq, dq), lambda h, i, j: (h, i, 0)),
                      pl.BlockSpec((None, tk, dq), lambda h, i, j: (h, j, 0)),
                      pl.BlockSpec((None, dv, tk), lambda h, i, j: (h, 0, j))],
            out_specs=[pl.BlockSpec((tq, dv), lambda h, i, j: (i, h)),
                       pl.BlockSpec((None, 1, tq), lambda h, i, j: (h, 0, i))],
            out_shape=[jax.ShapeDtypeStruct((n, hh * dv), BF16), jax.ShapeDtypeStruct((hh, 1, n), F32)],
            scratch_shapes=[pltpu.VMEM((1, tq), F32), pltpu.VMEM((1, tq), F32), pltpu.VMEM((dv, tq), F32),
                            pltpu.VMEM((tk, tq), F32), pltpu.VMEM((tk, tq), BF16)],
            compiler_params=_params(),
        )(q, k, vt)

    def delta_call(o, do, hh):
        n = o.shape[0]
        dv = o.shape[1] // hh
        tq = _pick(n, 1024)

        def body(o_ref, do_ref, d_ref):
            prod_t = jnp.transpose(o_ref[...].astype(F32) * do_ref[...].astype(F32))
            d_ref[...] = jnp.sum(prod_t, axis=0, keepdims=True)

        spec = pl.BlockSpec((tq, dv), lambda h, i: (i, h))
        return pl.pallas_call(
            body, name=tag + "_delta", grid=(hh, n // tq), in_specs=[spec, spec],
            out_specs=pl.BlockSpec((None, 1, tq), lambda h, i: (h, 0, i)),
            out_shape=jax.ShapeDtypeStruct((hh, 1, n), F32), compiler_params=_params(),
        )(o, do)

    def bwd_call(q, k, kt, v, do, lse, delta):
        hh, n, dq = q.shape
        ll, dv = k.shape[1], v.shape[2]
        tq, tk = _pick(n, 1024), _pick(ll, 1408)
        sb = tk
        c2 = scale * log2e
        q_steps = n // tq

        def body(q_ref, k_ref, kt_ref, v_ref, do_ref, lse_ref, d_ref, dqt_ref, dk_ref, dv_ref, dk_scr, dv_scr):
            j = pl.program_id(1)
            i = pl.program_id(2)

            @pl.when(i == 0)
            def _():
                dk_scr[...] = jnp.zeros_like(dk_scr)
                dv_scr[...] = jnp.zeros_like(dv_scr)

            q_t, do_t = q_ref[...], do_ref[...]
            lse2 = lse_ref[...] * log2e
            delta_t = d_ref[...]
            dq_part = None
            for kk in range(tk // sb):
                rows = slice(kk * sb, (kk + 1) * sb)
                s_t = _dot(k_ref[rows, :], q_t, NT)
                p_t = jnp.exp2(s_t * c2 - lse2)
                ds_t = p_t * (_dot(v_ref[rows, :], do_t, NT) - delta_t)
                dv_scr[rows, :] += _dot(p_t, do_t, NN)
                dk_scr[rows, :] += _dot(ds_t, q_t, NN)
                part = _dot(kt_ref[:, rows], ds_t, NN)
                dq_part = part if dq_part is None else dq_part + part
            cols = pl.ds(pl.multiple_of(i * tq, tq), tq)

            @pl.when(j == 0)
            def _():
                dqt_ref[:, cols] = dq_part

            @pl.when(j > 0)
            def _():
                dqt_ref[:, cols] += dq_part

            @pl.when(i == q_steps - 1)
            def _():
                dk_ref[...] = (dk_scr[...] * scale).astype(BF16)
                dv_ref[...] = dv_scr[...].astype(BF16)

        return pl.pallas_call(
            body, name=tag + "_bwd", grid=(hh, ll // tk, q_steps),
            in_specs=[pl.BlockSpec((None, tq, dq), lambda h, j, i: (h, i, 0)),
                      pl.BlockSpec((None, tk, dq), lambda h, j, i: (h, j, 0)),
                      pl.BlockSpec((None, dq, tk), lambda h, j, i: (h, 0, j)),
                      pl.BlockSpec((None, tk, dv), lambda h, j, i: (h, j, 0)),
                      pl.BlockSpec((tq, dv), lambda h, j, i: (i, h)),
                      pl.BlockSpec((None, 1, tq), lambda h, j, i: (h, 0, i)),
                      pl.BlockSpec((None, 1, tq), lambda h, j, i: (h, 0, i))],
            out_specs=[pl.BlockSpec((None, dq, n), lambda h, j, i: (h, 0, 0)),
                       pl.BlockSpec((None, tk, dq), lambda h, j, i: (h, j, 0)),
                       pl.BlockSpec((None, tk, dv), lambda h, j, i: (h, j, 0))],
            out_shape=[jax.ShapeDtypeStruct((hh, dq, n), F32), jax.ShapeDtypeStruct((hh, ll, dq), BF16),
                       jax.ShapeDtypeStruct((hh, ll, dv), BF16)],
            scratch_shapes=[pltpu.VMEM((tk, dq), F32), pltpu.VMEM((tk, dv), F32)],
            compiler_params=_params(),
        )(q, k, kt, v, do, lse, delta)

    return fwd_call, delta_call, bwd_call


def _loss_tile(x, g, tgt):
    r = lax.rsqrt(jnp.mean(x * x, axis=-1, keepdims=True) + RMS_EPS)
    err = x * r * g - tgt
    per_tok = jnp.mean(err * err, axis=-1, keepdims=True)
    return 0.5 * jnp.sum(per_tok, axis=0, keepdims=True)


def _make_final_loss(tag):
    def fwd_call(x, g, tgt):
        t, d = x.shape
        tm = _pick(t, 512, 16)

        def body(x_ref, g_ref, t_ref, l_ref):
            l_ref[...] = jnp.broadcast_to(_loss_tile(x_ref[...], g_ref[...], t_ref[...]), (1, LANE))

        parts = pl.pallas_call(
            body, name=tag + "_fwd", grid=(t // tm,),
            in_specs=[_row_spec(tm, d), _vec_spec(d), _row_spec(tm, d)],
            out_specs=pl.BlockSpec((None, 1, LANE), lambda i: (i, 0, 0)),
            out_shape=jax.ShapeDtypeStruct((t // tm, 1, LANE), F32), compiler_params=_params(),
        )(x, g, tgt)
        return jnp.sum(parts[:, 0, 0])

    def bwd_call(x, g, tgt, dl):
        t, d = x.shape
        tm = _pick(t, 256, 16)

        def body(x_ref, g_ref, t_ref, dl_ref, dx_ref, dg_ref):
            _, vjp = jax.vjp(_loss_tile, x_ref[...], g_ref[...], t_ref[...])
            dx, dg, _ = vjp(dl_ref[...])
            dx_ref[...] = dx

            @pl.when(pl.program_id(0) == 0)
            def _():
                dg_ref[...] = jnp.zeros_like(dg_ref)

            dg_ref[...] += dg

        return pl.pallas_call(
            body, name=tag + "_bwd", grid=(t // tm,),
            in_specs=[_row_spec(tm, d), _vec_spec(d), _row_spec(tm, d), pl.BlockSpec((1, 1), lambda i: (0, 0))],
            out_specs=[_row_spec(tm, d), _vec_spec(d)],
            out_shape=[jax.ShapeDtypeStruct((t, d), F32), jax.ShapeDtypeStruct((1, d), F32)],
            compiler_params=_params(),
        )(x, g, tgt, dl)

    @jax.custom_vjp
    def final_loss(x, g, tgt):
        return fwd_call(x, g, tgt)

    def fwd(x, g, tgt):
        return fwd_call(x, g, tgt), (x, g, tgt)

    def bwd(res, dl):
        x, g, tgt = res
        dx, dg = bwd_call(x, g, tgt, dl.reshape(1, 1).astype(F32))
        return dx, dg, jnp.zeros_like(tgt)

    final_loss.defvjp(fwd, bwd)
    return final_loss


def _exchange(arrays, gather, name):
    n = len(arrays)

    def body(*refs):
        ins, outs = refs[:n], refs[n:2 * n]
        send_sems, recv_sems, local_sems = refs[2 * n:]
        me = 4 * lax.axis_index("x") + 2 * lax.axis_index("y") + lax.axis_index("c")

        def remote(a, d, wait_side=False):
            peer = (me + d) % N_DEV
            origin = (me + N_DEV - d) % N_DEV
            src = ins[a] if gather else ins[a].at[peer]
            dst = outs[a].at[origin if wait_side else me]
            return pltpu.make_async_remote_copy(
                src_ref=src, dst_ref=dst, send_sem=send_sems.at[a, d - 1], recv_sem=recv_sems.at[a, d - 1],
                device_id=(peer // 4, (peer // 2) % 2, peer % 2), device_id_type=pl.DeviceIdType.MESH)

        def local(a):
            src = ins[a] if gather else ins[a].at[me]
            return pltpu.make_async_copy(src, outs[a].at[me], local_sems.at[a])

        for a in range(n):
            for d in range(1, N_DEV):
                remote(a, d).start()
            local(a).start()
        for a in range(n):
            local(a).wait()
            for d in range(1, N_DEV):
                remote(a, d, wait_side=True).wait_recv()
                remote(a, d).wait_send()

    out_shape = []
    for arr in arrays:
        shape = (N_DEV,) + arr.shape if gather else arr.shape
        out_shape.append(jax.ShapeDtypeStruct(shape, arr.dtype))
    any_spec = pl.BlockSpec(memory_space=pl.ANY)
    return pl.pallas_call(
        body, name=name, in_specs=[any_spec] * n, out_specs=[any_spec] * n, out_shape=out_shape,
        scratch_shapes=[pltpu.SemaphoreType.DMA((n, N_DEV - 1)), pltpu.SemaphoreType.DMA((n, N_DEV - 1)),
                        pltpu.SemaphoreType.DMA((n,))],
        compiler_params=pltpu.CompilerParams(has_side_effects=True),
    )(*arrays)


def _split_copy(ins, lands, send_sems, recv_sems, a, d, gather, wait_side):
    me = 4 * lax.axis_index("x") + 2 * lax.axis_index("y") + lax.axis_index("c")
    peer = (me + d) % N_DEV
    origin = (me + N_DEV - d) % N_DEV
    return pltpu.make_async_remote_copy(
        src_ref=ins[a] if gather else ins[a].at[peer], dst_ref=lands[a].at[origin if wait_side else me],
        send_sem=send_sems.at[a * (N_DEV - 1) + d - 1], recv_sem=recv_sems.at[a * (N_DEV - 1) + d - 1],
        device_id=(peer // 4, (peer // 2) % 2, peer % 2), device_id_type=pl.DeviceIdType.MESH)


def _exchange_start(srcs, lands, after, gather, name):
    n = len(srcs)

    def body(*refs):
        ins, lnd = refs[:n], refs[n:2 * n]
        send_sems, recv_sems = refs[2 * n + 1], refs[2 * n + 2]
        for a in range(n):
            for d in range(1, N_DEV):
                _split_copy(ins, lnd, send_sems, recv_sems, a, d, gather, False).start()

    hbm = pl.BlockSpec(memory_space=pltpu.HBM)
    sem = pl.BlockSpec(memory_space=pltpu.SEMAPHORE)
    bufs = [pltpu.with_memory_space_constraint(t, pltpu.HBM) for t in list(srcs) + list(lands) + [after]]
    res = pl.pallas_call(
        body, name=name,
        in_specs=[hbm] * (2 * n + 1), out_specs=[sem, sem] + [hbm] * (2 * n + 1),
        out_shape=[pltpu.SemaphoreType.DMA((n * (N_DEV - 1),)), pltpu.SemaphoreType.DMA((n * (N_DEV - 1),))]
        + [pltpu.HBM(t.shape, t.dtype) for t in bufs],
        input_output_aliases={i: 2 + i for i in range(2 * n + 1)},
        compiler_params=pltpu.CompilerParams(has_side_effects=pltpu.SideEffectType.DATAFLOW_SIDE_EFFECTING),
    )(*bufs)
    return res[0], res[1], res[2:2 + n], res[2 + n:2 + 2 * n], res[-1]


def _exchange_wait(send_sems, recv_sems, srcs, lands, after, gather, name):
    n = len(srcs)

    def body(*refs):
        ins, lnd = refs[:n], refs[n:2 * n]
        send_sems_ref, recv_sems_ref = refs[2 * n], refs[2 * n + 1]
        for a in range(n):
            for d in range(1, N_DEV):
                _split_copy(ins, lnd, send_sems_ref, recv_sems_ref, a, d, gather, False).wait_send()
                _split_copy(ins, lnd, send_sems_ref, recv_sems_ref, a, d, gather, True).wait_recv()

    hbm = pl.BlockSpec(memory_space=pltpu.HBM)
    sem = pl.BlockSpec(memory_space=pltpu.SEMAPHORE)
    bufs = list(srcs) + list(lands)
    res = pl.pallas_call(
        body, name=name,
        in_specs=[hbm] * (2 * n) + [sem, sem, pl.BlockSpec(memory_space=pl.ANY)],
        out_specs=[hbm] * (2 * n),
        out_shape=[pltpu.HBM(t.shape, t.dtype) for t in bufs],
        input_output_aliases={i: i for i in range(2 * n)},
        compiler_params=pltpu.CompilerParams(has_side_effects=pltpu.SideEffectType.DATAFLOW_SIDE_EFFECTING),
    )(*bufs, send_sems, recv_sems, after)
    return res[n:]


def _own_slot(block, me):
    empty = lax.empty((N_DEV,) + block.shape, block.dtype)
    return lax.dynamic_update_slice(empty, block[None], (me,) + (0,) * block.ndim)


def _coords():
    return lax.axis_index("x"), lax.axis_index("y"), lax.axis_index("c")


def _other_chips(x, y):
    return [(1 - x, y), (x, 1 - y), (1 - x, 1 - y)]


def _gather_two_level(arrays, name):
    n = len(arrays)

    def body(*refs):
        ins, outs = refs[:n], refs[n:2 * n]
        send_sems, recv_sems, local_sems = refs[2 * n:]
        x, y, c = _coords()
        me, sib = (x, y, c), (x, y, 1 - c)
        chips = _other_chips(x, y)

        def copy(a, k, block, to, from_input=False):
            slot = 4 * block[0] + 2 * block[1] + block[2]
            return pltpu.make_async_remote_copy(
                src_ref=ins[a] if from_input else outs[a].at[slot], dst_ref=outs[a].at[slot],
                send_sem=send_sems.at[a, k], recv_sem=recv_sems.at[a, k],
                device_id=to, device_id_type=pl.DeviceIdType.MESH)

        def local(a):
            return pltpu.make_async_copy(ins[a], outs[a].at[4 * x + 2 * y + c], local_sems.at[a])

        for a in range(n):
            for j, chip in enumerate(chips):
                copy(a, 1 + j, me, (*chip, c), True).start()
            copy(a, 0, me, sib, True).start()
            local(a).start()
        for a in range(n):
            for j, chip in enumerate(chips):
                copy(a, 1 + j, (*chip, c), me).wait_recv()
                copy(a, 4 + j, (*chip, c), sib).start()
        for a in range(n):
            copy(a, 0, sib, me).wait_recv()
            for j, chip in enumerate(chips):
                copy(a, 4 + j, (*chip, 1 - c), me).wait_recv()
            for k in range(N_DEV - 1):
                copy(a, k, me, sib, True).wait_send()
            local(a).wait()

    any_spec = pl.BlockSpec(memory_space=pl.ANY)
    return pl.pallas_call(
        body, name=name, in_specs=[any_spec] * n, out_specs=[any_spec] * n,
        out_shape=[jax.ShapeDtypeStruct((N_DEV,) + arr.shape, arr.dtype) for arr in arrays],
        scratch_shapes=[pltpu.SemaphoreType.DMA((n, N_DEV - 1)), pltpu.SemaphoreType.DMA((n, N_DEV - 1)),
                        pltpu.SemaphoreType.DMA((n,))],
        compiler_params=pltpu.CompilerParams(has_side_effects=True),
    )(*arrays)


def _make_gather_op(tag):
    @jax.custom_vjp
    def gather_op(xl):
        return _exchange([xl], True, tag + "_gather")[0]

    def fwd(xl):
        return gather_op(xl), None

    def bwd(_, g):
        return (jnp.sum(_exchange([g], False, tag + "_scatter")[0], axis=0),)

    gather_op.defvjp(fwd, bwd)
    return gather_op


def _adamw(gstack, w, m, v, name):
    s, r, cn = gstack.shape
    tr = _pick(r, max(8, (2 * 1024 * 1024) // (4 * cn) // 8 * 8), 8)
    c1 = 1.0 - ADAM_B1 ** ADAM_STEP
    c2 = 1.0 - ADAM_B2 ** ADAM_STEP

    def body(g_ref, w_ref, m_ref, v_ref, go_ref, d_ref, mo_ref, vo_ref):
        g = g_ref[0].astype(F32)
        for q in range(1, s):
            g = g + g_ref[q].astype(F32)
        m_new = ADAM_B1 * m_ref[...] + (1.0 - ADAM_B1) * g
        v_new = ADAM_B2 * v_ref[...] + (1.0 - ADAM_B2) * (g * g)
        go_ref[...] = g
        mo_ref[...] = m_new
        vo_ref[...] = v_new
        d_ref[...] = -ADAM_LR * ((m_new / c1) / (jnp.sqrt(v_new / c2) + ADAM_EPS) + ADAM_WD * w_ref[...])

    tile = pl.BlockSpec((tr, cn), lambda i: (i, 0))
    out = jax.ShapeDtypeStruct((r, cn), F32)
    return pl.pallas_call(
        body, name=name, grid=(r // tr,),
        in_specs=[pl.BlockSpec((s, tr, cn), lambda i: (0, i, 0)), tile, tile, tile],
        out_specs=[tile, tile, tile, tile], out_shape=[out, out, out, out],
        compiler_params=_params(),
    )(gstack, w, m, v)


def _cols_from_stack(w):
    return jnp.swapaxes(w, 0, 1).reshape(w.shape[1], N_DEV * w.shape[2])


def _ada_vectors(p, silu_c_all, me):
    d = p["c_ctx"].shape[0]
    n_a = p["ada_w"].shape[1]
    a_in = jnp.concatenate([silu_c_all, jax.nn.silu(p["c_ctx"])[None, :], jnp.zeros((7, d), F32)], axis=0)
    b_loc = lax.dynamic_slice(p["ada_b"], (0, me * n_a), (1, n_a))
    r_loc = _make_small_mm("ada")(a_in, p["ada_w"]) + b_loc
    r_full = _make_gather_op("ada")(r_loc)
    m_lat = lax.dynamic_index_in_dim(r_full, me, axis=1, keepdims=False).reshape(N_MOD, 1, d)
    m_ctx = r_full[:, N_DEV, :].reshape(N_MOD, 1, d)
    return m_lat, m_ctx


def _stage_a_fwd(p_ada, x, ctx, ng, w_in, wait_w_out, silu_c_all, me):
    (m_lat, m_ctx), vjp_ada = jax.vjp(lambda q: _ada_vectors(q, silu_c_all, me), p_ada)
    lat, cx = _ffn_parts("ffn1"), _ffn_parts("ffn1c")
    h, a, gu = lat[0](x, ng, m_lat[0], m_lat[1], w_in)
    hc, ac, guc = cx[0](ctx, ng, m_ctx[0], m_ctx[1], w_in)
    w_out = wait_w_out(a)
    x1, f1 = lat[1](a, w_out, x, m_lat[2])
    c1, f1c = cx[1](ac, w_out, ctx, m_ctx[2])
    res = dict(vjp_ada=vjp_ada, m_lat=m_lat, m_ctx=m_ctx, x=x, ctx=ctx, ng=ng, w_in=w_in, w_out=w_out,
               lat=(h, a, gu, f1), cx=(hc, ac, guc, f1c))
    return (x1, c1, m_lat, m_ctx), res


def _stage_a_bwd(res, dx1, dc1, dm_lat, dm_ctx, start_grads):
    lat, cx = _ffn_parts("ffn1"), _ffn_parts("ffn1c")
    m_lat, m_ctx, ng, w_in, w_out = res["m_lat"], res["m_ctx"], res["ng"], res["w_in"], res["w_out"]
    hc, ac, guc, f1c = res["cx"]
    dgate_c, dgu_c, dw_out_c = cx[2](dc1, f1c, m_ctx[2], w_out, guc, ac)
    _, dng_c, dsh_c, dsc_c, dw_in_c = cx[3](res["ctx"], ng, m_ctx[0], m_ctx[1], w_in, hc, dgu_c, dc1)
    h, a, gu, f1 = res["lat"]
    dgate, dgu, dw_out = lat[2](dx1, f1, m_lat[2], w_out, gu, a)
    dgu = start_grads("ffn1_w_out", dw_out + dw_out_c, dgu)
    dw_in = _ffn_dwin_bwd(h, dgu, "ffn1_dwin") + dw_in_c
    dgu = start_grads("ffn1_w_in", dw_in, dgu)
    dh = _ffn_dh_bwd(dgu, w_in, "ffn1_dh")
    dx, dng, dsc, dsh = _norm_mod_bwd(res["x"], ng, m_lat[1], m_lat[0], dh, dx1, "ffn1_dnorm")

    def rows(dsh_, dsc_, dgate_):
        return jnp.concatenate([dsh_, dsc_, dgate_, jnp.zeros((N_MOD - 3,) + dsh_.shape, F32)[:, 0]], axis=0)[:, None, :]

    (g_ada,) = res["vjp_ada"]((dm_lat + rows(dsh, dsc, dgate), dm_ctx + rows(dsh_c, dsc_c, dgate_c)))
    return g_ada, dx, dng + dng_c


def _stage_b(p, x1, c1, m_lat, m_ctx):
    n_lat, d = x1.shape
    n_ctx = c1.shape[0]
    w_mix = jnp.pad(_cols_from_stack(p["mix_w_in"]), ((0, 0), (0, MIX_IN_PAD - MIX_IN)))
    proj, x1 = _make_norm_proj_carry("mix")(x1, p["norm2_g"], m_lat[3], m_lat[4], w_mix)
    proj_c = _make_norm_proj("mixc")(c1, p["norm2_g"], m_ctx[3], m_ctx[4], w_mix)
    widths = SPLITS[:6] + (LANE,)
    rq, rk, rv, rg, cq, ckv, kr = _make_split("mixsplit", widths, MIX_IN_PAD)(proj)
    _, crk, crv, _, _, cckv, ckr = _make_split("mixsplitc", widths, MIX_IN_PAD)(proj_c)

    zq = jnp.zeros((1, MLA_Q_RANK), F32)
    zkv = jnp.zeros((1, MLA_KV_RANK), F32)
    w_uq3 = _cols_from_stack(p["mla_w_uq"]).reshape(MLA_Q_RANK, MLA_HEADS, MLA_NOPE + MLA_ROPE)
    w_uq = jnp.concatenate([w_uq3[:, :, :MLA_NOPE].reshape(MLA_Q_RANK, -1),
                            w_uq3[:, :, MLA_NOPE:].reshape(MLA_Q_RANK, -1)], axis=1)
    w_ukv = _cols_from_stack(p["mla_w_ukv"])
    q = _make_norm_proj("uq")(cq, p["mla_q_norm_g"], zq, zq, w_uq)
    kv = _make_norm_proj("ukv")(ckv, p["mla_kv_norm_g"], zkv, zkv, w_ukv)
    kv_c = _make_norm_proj("ukvc")(cckv, p["mla_kv_norm_g"], zkv, zkv, w_ukv)

    lg_f = jax.nn.log_sigmoid(p["ret_decay_fwd"][0])
    lg_b = jax.nn.log_sigmoid(p["ret_decay_bwd"][0])
    lat_f, lat_b, ctx_f, ctx_b = _make_ret_pack("retpack", n_lat, n_ctx)(rq, rk, rv, crk, crv)
    assert n_ctx == RET_CHUNK, "the context prefix is one retention chunk"

    def lanes(lg):
        return jnp.broadcast_to(lg[:, None, None], (RET_HEADS, 1, LANE))

    s0_f = _make_ctx_state("retcf", False)(*ctx_f, lanes(lg_f))
    s0_b = _make_ctx_state("retcb", True)(*ctx_b, lanes(lg_b))
    y_f = _make_ret_dir("retf", False)(*lat_f, lanes(lg_f), s0_f)
    y_b = _make_ret_dir("retb", True)(*lat_b, lanes(lg_b), s0_b)
    ret_o = _make_ret_out("reto")(y_f, y_b, rg)

    mla_o = _make_mla("mla", n_lat, n_ctx)(q, kv, kr, kv_c, ckr)

    w_mo = p["mix_w_out"].reshape(-1, d)
    return _make_res_proj("mixo")(jnp.concatenate([ret_o, mla_o], axis=-1), w_mo, x1, m_lat[5])


def _stage_c(p, x2, m_lat, tgt):
    x3 = _make_ffn_block("ffn2")(x2, p["norm3_g"], m_lat[6], m_lat[7], m_lat[8], p["ffn2_w_in"], p["ffn2_w_out"])
    return _make_final_loss("loss")(x3, p["final_norm_g"], tgt)


FIRST = ("ffn1_w_in", "ffn1_w_out")
MID = ("mix_w_in", "mla_w_uq", "mla_w_ukv", "mix_w_out")
LAST = ("ffn2_w_in", "ffn2_w_out")
BIG = FIRST + MID + LAST
SMALL = ("c_ctx", "ada_b", "norm1_g", "norm2_g", "ret_decay_fwd", "ret_decay_bwd", "mla_q_norm_g",
         "mla_kv_norm_g", "norm3_g", "final_norm_g")
WEIGHTS = ("c_ctx", "ada_w", "ada_b", "norm1_g", "ffn1_w_in", "ffn1_w_out", "norm2_g", "mix_w_in", "ret_decay_fwd",
           "ret_decay_bwd", "mla_q_norm_g", "mla_w_uq", "mla_kv_norm_g", "mla_w_ukv", "mix_w_out", "norm3_g",
           "ffn2_w_in", "ffn2_w_out", "final_norm_g")


def _pack(parts):
    flat = jnp.concatenate([t.reshape(-1) for t in parts])
    pad = (-flat.shape[0]) % LANE
    return jnp.pad(flat, (0, pad)).reshape(1, -1)


def _unpack(flat, like):
    out, off = [], 0
    for t in like:
        out.append(flat[0, off:off + t.size].reshape(t.shape))
        off += t.size
    return out


def kernel(x, c, ctx, c_ctx, ada_w, ada_b, norm1_g, ffn1_w_in, ffn1_w_out, norm2_g, mix_w_in, ret_decay_fwd, ret_decay_bwd, mla_q_norm_g, mla_w_uq, mla_kv_norm_g, mla_w_ukv, mix_w_out, norm3_g, ffn2_w_in, ffn2_w_out, final_norm_g, loss_target, m_c_ctx, m_ada_w, m_ada_b, m_norm1_g, m_ffn1_w_in, m_ffn1_w_out, m_norm2_g, m_mix_w_in, m_ret_decay_fwd, m_ret_decay_bwd, m_mla_q_norm_g, m_mla_w_uq, m_mla_kv_norm_g, m_mla_w_ukv, m_mix_w_out, m_norm3_g, m_ffn2_w_in, m_ffn2_w_out, m_final_norm_g, v_c_ctx, v_ada_w, v_ada_b, v_norm1_g, v_ffn1_w_in, v_ffn1_w_out, v_norm2_g, v_mix_w_in, v_ret_decay_fwd, v_ret_decay_bwd, v_mla_q_norm_g, v_mla_w_uq, v_mla_kv_norm_g, v_mla_w_ukv, v_mix_w_out, v_norm3_g, v_ffn2_w_in, v_ffn2_w_out, v_final_norm_g):
    w = dict(c_ctx=c_ctx, ada_w=ada_w, ada_b=ada_b, norm1_g=norm1_g, ffn1_w_in=ffn1_w_in, ffn1_w_out=ffn1_w_out,
             norm2_g=norm2_g, mix_w_in=mix_w_in, ret_decay_fwd=ret_decay_fwd, ret_decay_bwd=ret_decay_bwd,
             mla_q_norm_g=mla_q_norm_g, mla_w_uq=mla_w_uq, mla_kv_norm_g=mla_kv_norm_g, mla_w_ukv=mla_w_ukv,
             mix_w_out=mix_w_out, norm3_g=norm3_g, ffn2_w_in=ffn2_w_in, ffn2_w_out=ffn2_w_out,
             final_norm_g=final_norm_g)
    mom = dict(c_ctx=m_c_ctx, ada_w=m_ada_w, ada_b=m_ada_b, norm1_g=m_norm1_g, ffn1_w_in=m_ffn1_w_in,
               ffn1_w_out=m_ffn1_w_out, norm2_g=m_norm2_g, mix_w_in=m_mix_w_in, ret_decay_fwd=m_ret_decay_fwd,
               ret_decay_bwd=m_ret_decay_bwd, mla_q_norm_g=m_mla_q_norm_g, mla_w_uq=m_mla_w_uq,
               mla_kv_norm_g=m_mla_kv_norm_g, mla_w_ukv=m_mla_w_ukv, mix_w_out=m_mix_w_out, norm3_g=m_norm3_g,
               ffn2_w_in=m_ffn2_w_in, ffn2_w_out=m_ffn2_w_out, final_norm_g=m_final_norm_g)
    var = dict(c_ctx=v_c_ctx, ada_w=v_ada_w, ada_b=v_ada_b, norm1_g=v_norm1_g, ffn1_w_in=v_ffn1_w_in,
               ffn1_w_out=v_ffn1_w_out, norm2_g=v_norm2_g, mix_w_in=v_mix_w_in, ret_decay_fwd=v_ret_decay_fwd,
               ret_decay_bwd=v_ret_decay_bwd, mla_q_norm_g=v_mla_q_norm_g, mla_w_uq=v_mla_w_uq,
               mla_kv_norm_g=v_mla_kv_norm_g, mla_w_ukv=v_mla_w_ukv, mix_w_out=v_mix_w_out, norm3_g=v_norm3_g,
               ffn2_w_in=v_ffn2_w_in, ffn2_w_out=v_ffn2_w_out, final_norm_g=v_final_norm_g)
    me = 4 * lax.axis_index("x") + 2 * lax.axis_index("y") + lax.axis_index("c")

    shard = {k: w[k][0].astype(BF16) for k in BIG}
    first = _gather_two_level([shard["ffn1_w_in"], jax.nn.silu(c)], "weights_gather")
    silu_c_all = first[-1][:, 0, :]

    def start_gather(names, after, name):
        return _exchange_start([shard[k] for k in names], [_own_slot(shard[k], me) for k in names], after, True, name)

    wout_start = start_gather(FIRST[1:], first[0], "ffn1_wout_start")
    mid_start = start_gather(MID, wout_start[4], "mixer_weights_start")
    last_start = start_gather(LAST, mid_start[4], "ffn2_weights_start")

    def wait_w_out(after):
        return _exchange_wait(*wout_start[:4], after, True, "ffn1_wout_wait")[0]

    pa = dict(ada_w=ada_w[0], c_ctx=c_ctx, ada_b=ada_b)
    (x1, c1, m_lat, m_ctx), res_a = _stage_a_fwd(pa, x[0], ctx[0], norm1_g, last_start[4], wait_w_out, silu_c_all, me)

    mid = _exchange_wait(mid_start[0], mid_start[1], mid_start[2], mid_start[3], x1, True, "mixer_weights_wait")
    pb = dict(zip(MID, mid))
    for k in ("norm2_g", "mla_q_norm_g", "mla_kv_norm_g", "ret_decay_fwd", "ret_decay_bwd"):
        pb[k] = w[k]
    x2, vjp_b = jax.vjp(_stage_b, pb, x1, c1, m_lat, m_ctx)

    last = _exchange_wait(last_start[0], last_start[1], last_start[2], last_start[3], x2, True, "ffn2_weights_wait")
    pc = dict(zip(LAST, last), norm3_g=norm3_g, final_norm_g=final_norm_g[None, :])
    loss_local, vjp_c = jax.vjp(lambda q, t, m: _stage_c(q, t, m, loss_target[0]), pc, x2, m_lat)

    gc, dx2, dm_c = vjp_c(jnp.ones((), F32))
    last_scat = _exchange_start([gc[k] for k in LAST],
                                [_own_slot(lax.dynamic_index_in_dim(gc[k], me, 0, False), me) for k in LAST],
                                dx2, False, "ffn2_grads_start")
    gb, dx1, dc1, dm_b, dmc_b = vjp_b(last_scat[4])
    mid_scat = _exchange_start([gb[k] for k in MID],
                               [_own_slot(lax.dynamic_index_in_dim(gb[k], me, 0, False), me) for k in MID],
                               dx1, False, "mixer_grads_start")
    first_scat = {}

    def start_grads(name, dw, after):
        first_scat[name] = _exchange_start(
            [dw], [_own_slot(lax.dynamic_index_in_dim(dw, me, 0, False), me)], after, False, name + "_grads_start")
        return first_scat[name][4]

    g_ada, dx, dng1 = _stage_a_bwd(res_a, mid_scat[4], dc1, dm_b + dm_c, dmc_b, start_grads)
    grads = {**g_ada, **gb, **gc, "x": dx, "norm1_g": dng1}
    grads["final_norm_g"] = grads["final_norm_g"][0]

    exchanged = {k: _exchange_wait(*first_scat[k][:4], dx, False, k + "_grads_wait")[0] for k in FIRST}
    exchanged.update(zip(LAST, _exchange_wait(*last_scat[:4], dx, False, "ffn2_grads_wait")))
    exchanged.update(zip(MID, _exchange_wait(*mid_scat[:4], dx, False, "mixer_grads_wait")))
    zero1 = [jnp.zeros((1,), F32)]
    small_like = zero1 + [w[k] for k in SMALL]
    small_all = _exchange([_pack([loss_local.reshape(1)] + [grads[k] for k in SMALL])], True, "small_grads_gather")[0]
    loss = jnp.sum(small_all[:, 0, 0])

    out_g, out_d, out_m, out_v = {}, {}, {}, {}

    def update(name, gstack, shape2d):
        res = _adamw(gstack, w[name].reshape(shape2d), mom[name].reshape(shape2d), var[name].reshape(shape2d),
                     "adamw_" + name)
        out_g[name], out_d[name], out_m[name], out_v[name] = [t.reshape(w[name].shape) for t in res]

    for k in BIG:
        update(k, exchanged[k], exchanged[k].shape[1:])
    update("ada_w", grads["ada_w"][None], ada_w.shape[1:])
    res = _adamw(small_all, _pack(small_like), _pack(zero1 + [mom[k] for k in SMALL]),
                 _pack(zero1 + [var[k] for k in SMALL]), "adamw_small")
    for dst, flat in zip((out_g, out_d, out_m, out_v), res):
        for k, t in zip(SMALL, _unpack(flat, small_like)[1:]):
            dst[k] = t

    return (loss, grads["x"][None], *[out_g[k] for k in WEIGHTS], *[out_d[k] for k in WEIGHTS],
            *[out_m[k] for k in WEIGHTS], *[out_v[k] for k in WEIGHTS])
```

```python
import functools

import jax
import jax.numpy as jnp
from jax import lax
from jax.experimental import pallas as pl
from jax.experimental.pallas import tpu as pltpu

F32 = jnp.float32
BF16 = jnp.bfloat16

N_DEV = 8
MESH_AXES = ("x", "y", "c")

GRID_W = 64
N_MOD = 9
RET_HEADS = 8
RET_DK = 64
RET_DV = 128
RET_CHUNK = 256
RET_ROPE_BASE = 10000.0
MLA_HEADS = 8
MLA_Q_RANK = 512
MLA_KV_RANK = 256
MLA_NOPE = 128
MLA_ROPE = 64
MLA_V = 128
AXIAL_BASE = 10000.0
RMS_EPS = 1e-6
GN_EPS = 1e-5
SPLITS = (RET_HEADS * RET_DK, RET_HEADS * RET_DK, RET_HEADS * RET_DV, RET_HEADS * RET_DV,
          MLA_Q_RANK, MLA_KV_RANK, MLA_ROPE)
MIX_IN = sum(SPLITS)
MIX_IN_PAD = 4096

ADAM_LR = 0.001
ADAM_B1 = 0.9
ADAM_B2 = 0.999
ADAM_EPS = 1e-08
ADAM_WD = 0.01
ADAM_STEP = 10

LANE = 128
RET_DKP = LANE
VMEM_LIMIT_BYTES = 56 * 1024 * 1024

NN = ((1,), (0,))
NT = ((1,), (1,))
TN = ((0,), (0,))


def _pick(dim, target, align=LANE):
    t = min(dim, target)
    t -= t % align
    while t >= align:
        if dim % t == 0:
            return t
        t -= align
    return dim


def _params():
    return pltpu.CompilerParams(vmem_limit_bytes=VMEM_LIMIT_BYTES)


def _dot(a, b, dims):
    return lax.dot_general(a.astype(BF16), b.astype(BF16), (dims, ((), ())), preferred_element_type=F32)


def _mm_call(name, grid, ins, pairs, outs, acc_shapes, epilogue):
    n_in, n_out = len(ins), len(outs)
    k_axis = len(grid) - 1
    k_steps = grid[k_axis]

    def body(*refs):
        in_refs = refs[:n_in]
        out_refs = refs[n_in:n_in + n_out]
        accs = refs[n_in + n_out:]
        k = pl.program_id(k_axis)

        @pl.when(k == 0)
        def _():
            for acc in accs:
                acc[...] = jnp.zeros_like(acc)

        for ai, bi, dims, ci in pairs:
            accs[ci][...] += _dot(in_refs[ai][...], in_refs[bi][...], dims)

        @pl.when(k == k_steps - 1)
        def _():
            epilogue([acc[...] for acc in accs], in_refs, out_refs)

    res = pl.pallas_call(
        body, name=name, grid=grid,
        in_specs=[s for _, s in ins], out_specs=[s for _, s in outs],
        out_shape=[s for s, _ in outs],
        scratch_shapes=[pltpu.VMEM(s, F32) for s in acc_shapes],
        compiler_params=_params(),
    )(*[a for a, _ in ins])
    return res


def _matmul(a, b, mode, out_dtype, name, tm=1024, tn=1024, tk=512):
    if mode == "nn":
        (m, kd), n = a.shape, b.shape[1]
    elif mode == "nt":
        (m, kd), n = a.shape, b.shape[0]
    else:
        (kd, m), n = a.shape, b.shape[1]
    tm, tn = _pick(m, tm, 16), _pick(n, tn)
    tk = _pick(kd, tk) if mode != "tn" else _pick(kd, tk, 16)
    if mode == "nn":
        a_spec = pl.BlockSpec((tm, tk), lambda i, j, k: (i, k))
        b_spec = pl.BlockSpec((tk, tn), lambda i, j, k: (k, j))
        dims = NN
    elif mode == "nt":
        a_spec = pl.BlockSpec((tm, tk), lambda i, j, k: (i, k))
        b_spec = pl.BlockSpec((tn, tk), lambda i, j, k: (j, k))
        dims = NT
    else:
        a_spec = pl.BlockSpec((tk, tm), lambda i, j, k: (k, i))
        b_spec = pl.BlockSpec((tk, tn), lambda i, j, k: (k, j))
        dims = TN

    def epilogue(accs, in_refs, out_refs):
        out_refs[0][...] = accs[0].astype(out_dtype)

    return _mm_call(
        name, (m // tm, n // tn, kd // tk), [(a, a_spec), (b, b_spec)], [(0, 1, dims, 0)],
        [(jax.ShapeDtypeStruct((m, n), out_dtype), pl.BlockSpec((tm, tn), lambda i, j, k: (i, j)))],
        [(tm, tn)], epilogue)[0]


def _norm_mod_tile(x, ng, sc, sh):
    r = lax.rsqrt(jnp.mean(x * x, axis=-1, keepdims=True) + RMS_EPS)
    return (x * r * ng) * (1.0 + sc) + sh


def _row_spec(tm, d):
    return pl.BlockSpec((tm, d), lambda i: (i, 0))


def _vec_spec(d):
    return pl.BlockSpec((1, d), lambda i: (0, 0))


def _norm_mod_fwd(x, ng, sc, sh, name):
    t, d = x.shape
    tm = _pick(t, 512, 16)

    def body(x_ref, ng_ref, sc_ref, sh_ref, h_ref):
        h_ref[...] = _norm_mod_tile(x_ref[...], ng_ref[...], sc_ref[...], sh_ref[...]).astype(BF16)

    return pl.pallas_call(
        body, name=name, grid=(t // tm,),
        in_specs=[_row_spec(tm, d), _vec_spec(d), _vec_spec(d), _vec_spec(d)],
        out_specs=_row_spec(tm, d), out_shape=jax.ShapeDtypeStruct((t, d), BF16),
        compiler_params=_params(),
    )(x, ng, sc, sh)


def _norm_mod_bwd(x, ng, sc, sh, dh, dres, name):
    t, d = x.shape
    tm = _pick(t, 256, 16)
    has_res = dres is not None

    def body(*refs):
        if has_res:
            x_ref, ng_ref, sc_ref, sh_ref, dh_ref, dres_ref, dx_ref, dng_ref, dsc_ref, dsh_ref = refs
        else:
            x_ref, ng_ref, sc_ref, sh_ref, dh_ref, dx_ref, dng_ref, dsc_ref, dsh_ref = refs
        _, vjp = jax.vjp(_norm_mod_tile, x_ref[...], ng_ref[...], sc_ref[...], sh_ref[...])
        dx, dng, dsc, dsh = vjp(dh_ref[...].astype(F32))
        if has_res:
            dx = dx + dres_ref[...]
        dx_ref[...] = dx

        @pl.when(pl.program_id(0) == 0)
        def _():
            dng_ref[...] = jnp.zeros_like(dng_ref)
            dsc_ref[...] = jnp.zeros_like(dsc_ref)
            dsh_ref[...] = jnp.zeros_like(dsh_ref)

        dng_ref[...] += dng
        dsc_ref[...] += dsc
        dsh_ref[...] += dsh

    ins = [x, ng, sc, sh, dh] + ([dres] if has_res else [])
    in_specs = [_row_spec(tm, d), _vec_spec(d), _vec_spec(d), _vec_spec(d), _row_spec(tm, d)]
    in_specs += [_row_spec(tm, d)] if has_res else []
    vec = jax.ShapeDtypeStruct((1, d), F32)
    return pl.pallas_call(
        body, name=name, grid=(t // tm,), in_specs=in_specs,
        out_specs=[_row_spec(tm, d), _vec_spec(d), _vec_spec(d), _vec_spec(d)],
        out_shape=[jax.ShapeDtypeStruct((t, d), F32), vec, vec, vec],
        compiler_params=_params(),
    )(*ins)


def _res_mm_fwd(a, w, x, gate, coef, name):
    t, kd = a.shape
    d = w.shape[1]
    tm, tn, tk = _pick(t, 1024, 16), _pick(d, 1024), _pick(kd, 2816)

    def epilogue(accs, in_refs, out_refs):
        f = accs[0]
        out_refs[0][...] = in_refs[2][...] + (coef * in_refs[3][...]) * f
        out_refs[1][...] = f.astype(BF16)

    tile = pl.BlockSpec((tm, tn), lambda i, j, k: (i, j))
    return _mm_call(
        name, (t // tm, d // tn, kd // tk),
        [(a, pl.BlockSpec((tm, tk), lambda i, j, k: (i, k))), (w, pl.BlockSpec((tk, tn), lambda i, j, k: (k, j))),
         (x, tile), (gate, pl.BlockSpec((1, tn), lambda i, j, k: (0, j)))],
        [(0, 1, NN, 0)],
        [(jax.ShapeDtypeStruct((t, d), F32), tile), (jax.ShapeDtypeStruct((t, d), BF16), tile)],
        [(tm, tn)], epilogue)


def _gate_bwd(dxo, f, gate, coef, name):
    t, d = dxo.shape
    tm = _pick(t, 512, 16)

    def body(dxo_ref, f_ref, gate_ref, df_ref, dgate_ref):
        dxo_t = dxo_ref[...]
        df_ref[...] = ((coef * gate_ref[...]) * dxo_t).astype(BF16)

        @pl.when(pl.program_id(0) == 0)
        def _():
            dgate_ref[...] = jnp.zeros_like(dgate_ref)

        dgate_ref[...] += coef * jnp.sum(dxo_t * f_ref[...].astype(F32), axis=0, keepdims=True)

    return pl.pallas_call(
        body, name=name, grid=(t // tm,),
        in_specs=[_row_spec(tm, d), _row_spec(tm, d), _vec_spec(d)],
        out_specs=[_row_spec(tm, d), _vec_spec(d)],
        out_shape=[jax.ShapeDtypeStruct((t, d), BF16), jax.ShapeDtypeStruct((1, d), F32)],
        compiler_params=_params(),
    )(dxo, f, gate)


def _ffn_in_fwd(h, w_in, name):
    t, d = h.shape
    n = w_in.shape[2]
    half = N_DEV // 2
    f = half * n
    tm = _pick(t, 512, 16)

    def epilogue(accs, in_refs, out_refs):
        g, u = accs
        s = jax.nn.sigmoid(g)
        silu = g * s
        out_refs[0][...] = (silu * u).astype(BF16)
        out_refs[1][0] = (u * (s * (1.0 + g * (1.0 - s)))).astype(BF16)
        out_refs[1][1] = silu.astype(BF16)

    return _mm_call(
        name, (half, t // tm, 1),
        [(h, pl.BlockSpec((tm, d), lambda j, i, k: (i, 0))),
         (w_in, pl.BlockSpec((None, d, n), lambda j, i, k: (j, 0, 0))),
         (w_in, pl.BlockSpec((None, d, n), lambda j, i, k: (j + half, 0, 0)))],
        [(0, 1, NN, 0), (0, 2, NN, 1)],
        [(jax.ShapeDtypeStruct((t, f), BF16), pl.BlockSpec((tm, n), lambda j, i, k: (i, j))),
         (jax.ShapeDtypeStruct((2, t, f), BF16), pl.BlockSpec((2, tm, n), lambda j, i, k: (0, i, j)))],
        [(tm, n), (tm, n)], epilogue)


def _ffn_da_bwd(df, w_out2d, gu, name):
    t, d = df.shape
    f = w_out2d.shape[0]
    half = N_DEV // 2
    n = f // half
    tm = _pick(t, 512, 16)
    step = 4 * LANE
    chunks = [(c, min(c + step, n)) for c in range(0, n, step)]

    def body(df_ref, w_ref, gu_ref, o_ref):
        df_t = df_ref[...]
        for c0, c1 in chunks:
            da = _dot(df_t, w_ref[c0:c1, :], NT)
            o_ref[0, :, c0:c1] = (da * gu_ref[0, :, c0:c1].astype(F32)).astype(BF16)
            o_ref[1, :, c0:c1] = (da * gu_ref[1, :, c0:c1].astype(F32)).astype(BF16)

    gu_spec = pl.BlockSpec((2, tm, n), lambda j, i: (0, i, j))
    return pl.pallas_call(
        body, name=name, grid=(half, t // tm),
        in_specs=[pl.BlockSpec((tm, d), lambda j, i: (i, 0)), pl.BlockSpec((n, d), lambda j, i: (j, 0)), gu_spec],
        out_specs=gu_spec, out_shape=jax.ShapeDtypeStruct((2, t, f), BF16), compiler_params=_params(),
    )(df, w_out2d, gu)


def _ffn_dh_bwd(dgu, w_in, name):
    _, t, f = dgu.shape
    d, n = w_in.shape[1], w_in.shape[2]
    half = N_DEV // 2
    tm = _pick(t, 512, 16)

    def epilogue(accs, in_refs, out_refs):
        out_refs[0][...] = accs[0]

    return _mm_call(
        name, (t // tm, 1, half),
        [(dgu, pl.BlockSpec((None, tm, n), lambda i, j, k: (0, i, k))),
         (dgu, pl.BlockSpec((None, tm, n), lambda i, j, k: (1, i, k))),
         (w_in, pl.BlockSpec((None, d, n), lambda i, j, k: (k, 0, 0))),
         (w_in, pl.BlockSpec((None, d, n), lambda i, j, k: (k + half, 0, 0)))],
        [(0, 2, NT, 0), (1, 3, NT, 0)],
        [(jax.ShapeDtypeStruct((t, d), F32), pl.BlockSpec((tm, d), lambda i, j, k: (i, 0)))],
        [(tm, d)], epilogue)[0]


def _ffn_dwin_bwd(h, dgu, name):
    t, d = h.shape
    f = dgu.shape[2]
    half = N_DEV // 2
    n = f // half
    tk = _pick(t, 1024, 16)

    def epilogue(accs, in_refs, out_refs):
        out_refs[0][...] = accs[0].astype(BF16)

    return _mm_call(
        name, (N_DEV, 1, t // tk),
        [(h, pl.BlockSpec((tk, d), lambda j, i, k: (k, 0))),
         (dgu, pl.BlockSpec((None, tk, n), lambda j, i, k: (j // half, k, j % half)))],
        [(0, 1, TN, 0)],
        [(jax.ShapeDtypeStruct((N_DEV, d, n), BF16), pl.BlockSpec((None, d, n), lambda j, i, k: (j, 0, 0)))],
        [(d, n)], epilogue)[0]


def _ffn_parts(tag):
    def w2d(w_out):
        return w_out.reshape(w_out.shape[0] * w_out.shape[1], w_out.shape[2])

    def fwd_in(x, ng, sh, sc, w_in):
        h = _norm_mod_fwd(x, ng, sc, sh, tag + "_norm")
        a, gu = _ffn_in_fwd(h, w_in, tag + "_in")
        return h, a, gu

    def fwd_out(a, w_out, x, gate):
        return _res_mm_fwd(a, w2d(w_out), x, gate, 0.5, tag + "_out")

    def bwd_out(dxo, f1, gate, w_out, gu, a):
        df, dgate = _gate_bwd(dxo, f1, gate, 0.5, tag + "_dgate")
        dgu = _ffn_da_bwd(df, w2d(w_out), gu, tag + "_da")
        f = w_out.shape[0] * w_out.shape[1]
        dw_out = _matmul(a, df, "tn", BF16, tag + "_dwout", tm=_pick(f, 1408, 16), tn=2048, tk=1024)
        return dgate, dgu, dw_out.reshape(w_out.shape)

    def bwd_in(x, ng, sh, sc, w_in, h, dgu, dxo):
        dh = _ffn_dh_bwd(dgu, w_in, tag + "_dh")
        dw_in = _ffn_dwin_bwd(h, dgu, tag + "_dwin")
        dx, dng, dsc, dsh = _norm_mod_bwd(x, ng, sc, sh, dh, dxo, tag + "_dnorm")
        return dx, dng, dsh, dsc, dw_in

    return fwd_in, fwd_out, bwd_out, bwd_in


def _make_ffn_block(tag):
    fwd_in, fwd_out, bwd_out, bwd_in = _ffn_parts(tag)

    @jax.custom_vjp
    def ffn_block(x, ng, sh, sc, gate, w_in, w_out):
        return fwd(x, ng, sh, sc, gate, w_in, w_out)[0]

    def fwd(x, ng, sh, sc, gate, w_in, w_out):
        h, a, gu = fwd_in(x, ng, sh, sc, w_in)
        xo, f1 = fwd_out(a, w_out, x, gate)
        return xo, (x, ng, sh, sc, gate, w_in, w_out, h, a, gu, f1)

    def bwd(res, dxo):
        x, ng, sh, sc, gate, w_in, w_out, h, a, gu, f1 = res
        dgate, dgu, dw_out = bwd_out(dxo, f1, gate, w_out, gu, a)
        dx, dng, dsh, dsc, dw_in = bwd_in(x, ng, sh, sc, w_in, h, dgu, dxo)
        return dx, dng, dsh, dsc, dgate, dw_in, dw_out

    ffn_block.defvjp(fwd, bwd)
    return ffn_block


def _make_norm_proj(tag):
    @jax.custom_vjp
    def norm_proj(x, ng, sh, sc, w):
        return fwd(x, ng, sh, sc, w)[0]

    def fwd(x, ng, sh, sc, w):
        h = _norm_mod_fwd(x, ng, sc, sh, tag + "_norm")
        p = _matmul(h, w, "nn", F32, tag + "_mm", tm=1024, tn=1024, tk=w.shape[0])
        return p, (x, ng, sh, sc, w, h)

    def bwd(res, dp):
        x, ng, sh, sc, w, h = res
        dh = _matmul(dp, w, "nt", F32, tag + "_dh", tm=512, tn=w.shape[0], tk=2048)
        dw = _matmul(h, dp, "tn", BF16, tag + "_dw", tm=w.shape[0], tn=1024, tk=1024)
        dx, dng, dsc, dsh = _norm_mod_bwd(x, ng, sc, sh, dh, None, tag + "_dnorm")
        return dx, dng, dsh, dsc, dw

    norm_proj.defvjp(fwd, bwd)
    return norm_proj


def _make_norm_proj_carry(tag):
    @jax.custom_vjp
    def norm_proj(x, ng, sh, sc, w):
        return fwd(x, ng, sh, sc, w)[0]

    def fwd(x, ng, sh, sc, w):
        h = _norm_mod_fwd(x, ng, sc, sh, tag + "_norm")
        p = _matmul(h, w, "nn", F32, tag + "_mm", tm=1024, tn=1024, tk=w.shape[0])
        return (p, x), (x, ng, sh, sc, w, h)

    def bwd(res, cts):
        x, ng, sh, sc, w, h = res
        dp, dx_carry = cts
        dh = _matmul(dp, w, "nt", F32, tag + "_dh", tm=512, tn=w.shape[0], tk=2048)
        dw = _matmul(h, dp, "tn", BF16, tag + "_dw", tm=w.shape[0], tn=1024, tk=1024)
        dx, dng, dsc, dsh = _norm_mod_bwd(x, ng, sc, sh, dh, dx_carry, tag + "_dnorm")
        return dx, dng, dsh, dsc, dw

    norm_proj.defvjp(fwd, bwd)
    return norm_proj


def _make_split(tag, widths, total):
    offs = [sum(widths[:i]) for i in range(len(widths))]

    def concat_call(pieces):
        t = pieces[0].shape[0]
        tm = _pick(t, 256, 16)

        def body(*refs):
            o_ref = refs[-1]
            for ref, off, wd in zip(refs[:-1], offs, widths):
                o_ref[:, off:off + wd] = ref[...]
            end = offs[-1] + widths[-1]
            if end < total:
                o_ref[:, end:] = jnp.zeros((tm, total - end), F32)

        return pl.pallas_call(
            body, name=tag + "_concat", grid=(t // tm,),
            in_specs=[_row_spec(tm, wd) for wd in widths], out_specs=_row_spec(tm, total),
            out_shape=jax.ShapeDtypeStruct((t, total), F32), compiler_params=_params(),
        )(*pieces)

    @jax.custom_vjp
    def split(p):
        return tuple(p[:, off:off + wd] for off, wd in zip(offs, widths))

    def fwd(p):
        return split(p), None

    def bwd(_, cts):
        return (concat_call(list(cts)),)

    split.defvjp(fwd, bwd)
    return split


def _make_res_proj(tag):
    @jax.custom_vjp
    def res_proj(a, w, x, gate):
        return fwd(a, w, x, gate)[0]

    def fwd(a, w, x, gate):
        xo, f = _res_mm_fwd(a, w, x, gate, 1.0, tag + "_mm")
        return xo, (a, w, gate, f)

    def bwd(res, dxo):
        a, w, gate, f = res
        df, dgate = _gate_bwd(dxo, f, gate, 1.0, tag + "_dgate")
        da = _matmul(df, w, "nt", BF16, tag + "_da", tm=1024, tn=1024, tk=2048)
        dw = _matmul(a, df, "tn", BF16, tag + "_dw", tm=1024, tn=2048, tk=1024)
        return da, dw, dxo, dgate

    res_proj.defvjp(fwd, bwd)
    return res_proj


def _make_small_mm(tag):
    @jax.custom_vjp
    def small_mm(a, w):
        return _matmul(a, w, "nn", F32, tag + "_mm", tm=a.shape[0], tn=768, tk=w.shape[0])

    def fwd(a, w):
        return small_mm(a, w), (a, w)

    def bwd(res, dr):
        a, w = res
        da = _matmul(dr, w, "nt", F32, tag + "_da", tm=a.shape[0], tn=w.shape[0], tk=768)
        dw = _matmul(a, dr, "tn", F32, tag + "_dw", tm=1024, tn=768, tk=a.shape[0])
        return da, dw

    small_mm.defvjp(fwd, bwd)
    return small_mm


def _ret_chunk_terms(lg, c, reverse):
    row = lax.broadcasted_iota(jnp.int32, (c, c), 0).astype(F32)
    col = lax.broadcasted_iota(jnp.int32, (c, c), 1).astype(F32)
    pos = lax.broadcasted_iota(jnp.int32, (c, 1), 0).astype(F32)
    if reverse:
        diff = col - row
        mask = diff > 0.0
        e_exp = float(c) - pos
        f_exp = pos
    else:
        diff = row - col
        mask = diff >= 0.0
        e_exp = pos + 1.0
        f_exp = float(c - 1) - pos
    diffm = jnp.where(mask, diff, 0.0)
    dm = jnp.where(mask, jnp.exp(lg * diffm), 0.0)
    return diffm, dm, e_exp, jnp.exp(lg * e_exp), f_exp, jnp.exp(lg * f_exp)


def _lane0(val):
    lane = lax.broadcasted_iota(jnp.int32, (1, LANE), 1)
    return jnp.where(lane == 0, val, 0.0)


RET_HEAD_BLOCK = 4


def _make_ret_dir(tag, reverse):
    hb = RET_HEAD_BLOCK

    def heads_spec(nc, width, flip):
        if flip:
            return pl.BlockSpec((hb, RET_CHUNK, width), lambda h, t: (h, nc - 1 - t, 0))
        return pl.BlockSpec((hb, RET_CHUNK, width), lambda h, t: (h, t, 0))

    def state_spec(nc, flip):
        if flip:
            return pl.BlockSpec((hb, None, RET_DKP, RET_DV), lambda h, t: (h, nc - 1 - t, 0, 0))
        return pl.BlockSpec((hb, None, RET_DKP, RET_DV), lambda h, t: (h, t, 0, 0))

    lg_spec = pl.BlockSpec((hb, 1, LANE), lambda h, t: (h, 0, 0))
    s0_spec = pl.BlockSpec((hb, RET_DKP, RET_DV), lambda h, t: (h, 0, 0))

    def fwd_call(q, k, v, lgb, s0):
        hh, ll, _ = q.shape
        c = RET_CHUNK
        nc = ll // c

        def body(q_ref, k_ref, v_ref, lg_ref, s0_ref, y_ref, sall_ref, s_scr):
            @pl.when(pl.program_id(1) == 0)
            def _():
                s_scr[...] = s0_ref[...]

            for b in range(hb):
                lg = lg_ref[b][:, :1]
                _, dm, _, xi, _, zeta = _ret_chunk_terms(lg, c, reverse)
                q_t, k_t, v_t = q_ref[b], k_ref[b], v_ref[b]
                s = s_scr[b]
                p = _dot(q_t, k_t, NT) * dm
                y_ref[b] = _dot(p, v_t, NN) + _dot(q_t * xi, s, NN)
                sall_ref[b] = s
                s_scr[b] = jnp.exp(lg * float(c)) * s + _dot(k_t * zeta, v_t, TN)

        return pl.pallas_call(
            body, name=tag + "_fwd", grid=(hh // hb, nc),
            in_specs=[heads_spec(nc, RET_DKP,reverse), heads_spec(nc, RET_DKP,reverse),
                      heads_spec(nc, RET_DV, reverse), lg_spec, s0_spec],
            out_specs=[heads_spec(nc, RET_DV, reverse), state_spec(nc, reverse)],
            out_shape=[jax.ShapeDtypeStruct((hh, ll, RET_DV), F32),
                       jax.ShapeDtypeStruct((hh, nc, RET_DKP, RET_DV), F32)],
            scratch_shapes=[pltpu.VMEM((hb, RET_DKP, RET_DV), F32)],
            compiler_params=_params(),
        )(q, k, v, lgb, s0)

    def bwd_call(q, k, v, lgb, sall, dy):
        hh, ll, _ = q.shape
        c = RET_CHUNK
        nc = ll // c
        flip = not reverse

        def body(q_ref, k_ref, v_ref, lg_ref, sall_ref, dy_ref, dq_ref, dk_ref, dv_ref, dlg_ref, ds0_ref, ds_scr):
            @pl.when(pl.program_id(1) == 0)
            def _():
                ds_scr[...] = jnp.zeros_like(ds_scr)
                dlg_ref[...] = jnp.zeros_like(dlg_ref)

            def total(m):
                return jnp.sum(jnp.sum(m, axis=1, keepdims=True), axis=0, keepdims=True)

            for b in range(hb):
                lg = lg_ref[b][:, :1]
                diffm, dm, e_exp, xi, f_exp, zeta = _ret_chunk_terms(lg, c, reverse)
                q_t, k_t, v_t, dy_t = q_ref[b], k_ref[b], v_ref[b], dy_ref[b]
                s = sall_ref[b]
                dsn = ds_scr[b]
                a = _dot(q_t, k_t, NT)
                da = _dot(dy_t, v_t, NT) * dm
                g = _dot(dy_t, s, NT)
                hm = _dot(v_t, dsn, NT)
                dq_ref[b] = _dot(da, k_t, NN) + xi * g
                dk_ref[b] = _dot(da, q_t, TN) + zeta * hm
                dv_ref[b] = _dot(a * dm, dy_t, TN) + _dot(k_t * zeta, dsn, NN)
                gc = jnp.exp(lg * float(c))
                ds_scr[b] = gc * dsn + _dot(q_t * xi, dy_t, TN)
                dl = (total(da * a * diffm) + total(e_exp * xi * q_t * g)
                      + float(c) * gc * total(s * dsn) + total(f_exp * zeta * k_t * hm))
                dlg_ref[b] += _lane0(dl)

            @pl.when(pl.program_id(1) == nc - 1)
            def _():
                ds0_ref[...] = ds_scr[...]

        return pl.pallas_call(
            body, name=tag + "_bwd", grid=(hh // hb, nc),
            in_specs=[heads_spec(nc, RET_DKP,flip), heads_spec(nc, RET_DKP,flip), heads_spec(nc, RET_DV, flip),
                      lg_spec, state_spec(nc, flip), heads_spec(nc, RET_DV, flip)],
            out_specs=[heads_spec(nc, RET_DKP,flip), heads_spec(nc, RET_DKP,flip), heads_spec(nc, RET_DV, flip),
                       lg_spec, s0_spec],
            out_shape=[jax.ShapeDtypeStruct((hh, ll, RET_DKP), F32), jax.ShapeDtypeStruct((hh, ll, RET_DKP), F32),
                       jax.ShapeDtypeStruct((hh, ll, RET_DV), F32), jax.ShapeDtypeStruct((hh, 1, LANE), F32),
                       jax.ShapeDtypeStruct((hh, RET_DKP, RET_DV), F32)],
            scratch_shapes=[pltpu.VMEM((hb, RET_DKP, RET_DV), F32)],
            compiler_params=_params(),
        )(q, k, v, lgb, sall, dy)

    @jax.custom_vjp
    def ret_dir(q, k, v, lgb, s0):
        return fwd_call(q, k, v, lgb, s0)[0]

    def fwd(q, k, v, lgb, s0):
        y, sall = fwd_call(q, k, v, lgb, s0)
        return y, (q, k, v, lgb, sall)

    def bwd(res, dy):
        q, k, v, lgb, sall = res
        return tuple(bwd_call(q, k, v, lgb, sall, dy))

    ret_dir.defvjp(fwd, bwd)
    return ret_dir


def _make_ctx_state(tag, reverse):
    hb = RET_HEAD_BLOCK
    c = RET_CHUNK
    k_spec = pl.BlockSpec((hb, c, RET_DKP), lambda h: (h, 0, 0))
    v_spec = pl.BlockSpec((hb, c, RET_DV), lambda h: (h, 0, 0))
    lg_spec = pl.BlockSpec((hb, 1, LANE), lambda h: (h, 0, 0))
    s_spec = pl.BlockSpec((hb, RET_DKP, RET_DV), lambda h: (h, 0, 0))

    def fwd_call(k, v, lgb):
        hh = k.shape[0]

        def body(k_ref, v_ref, lg_ref, s_ref):
            for b in range(hb):
                _, _, _, _, _, zeta = _ret_chunk_terms(lg_ref[b][:, :1], c, reverse)
                s_ref[b] = _dot(k_ref[b] * zeta, v_ref[b], TN)

        return pl.pallas_call(
            body, name=tag + "_fwd", grid=(hh // hb,), in_specs=[k_spec, v_spec, lg_spec], out_specs=s_spec,
            out_shape=jax.ShapeDtypeStruct((hh, RET_DKP, RET_DV), F32), compiler_params=_params(),
        )(k, v, lgb)

    def bwd_call(k, v, lgb, ds):
        hh = k.shape[0]

        def body(k_ref, v_ref, lg_ref, ds_ref, dk_ref, dv_ref, dlg_ref):
            for b in range(hb):
                _, _, _, _, f_exp, zeta = _ret_chunk_terms(lg_ref[b][:, :1], c, reverse)
                k_t, v_t, ds = k_ref[b], v_ref[b], ds_ref[b]
                hm = _dot(v_t, ds, NT)
                dk_ref[b] = zeta * hm
                dv_ref[b] = _dot(k_t * zeta, ds, NN)
                tot = jnp.sum(jnp.sum(f_exp * zeta * k_t * hm, axis=1, keepdims=True), axis=0, keepdims=True)
                dlg_ref[b] = _lane0(tot)

        return pl.pallas_call(
            body, name=tag + "_bwd", grid=(hh // hb,), in_specs=[k_spec, v_spec, lg_spec, s_spec],
            out_specs=[k_spec, v_spec, lg_spec],
            out_shape=[jax.ShapeDtypeStruct(k.shape, F32), jax.ShapeDtypeStruct(v.shape, F32),
                       jax.ShapeDtypeStruct((hh, 1, LANE), F32)],
            compiler_params=_params(),
        )(k, v, lgb, ds)

    @jax.custom_vjp
    def ctx_state(k, v, lgb):
        return fwd_call(k, v, lgb)

    def fwd(k, v, lgb):
        return fwd_call(k, v, lgb), (k, v, lgb)

    def bwd(res, ds):
        return tuple(bwd_call(*res, ds))

    ctx_state.defvjp(fwd, bwd)
    return ctx_state


def _rope_tables_call(name, n, inv, shift, axial):
    tm = _pick(n, 1024, 8)
    inv_lane = jnp.tile(inv, LANE // inv.shape[0])[None, :]

    def body(inv_ref, cos_ref, s1_ref, s2_ref):
        t = lax.broadcasted_iota(jnp.int32, (tm, LANE), 0) + pl.program_id(0) * tm
        lane = lax.broadcasted_iota(jnp.int32, (tm, LANE), 1)
        if axial:
            pos = jnp.where(lane % (2 * MLA_ROPE // 2) < MLA_ROPE // 2, t // GRID_W, t % GRID_W)
        else:
            pos = t
        ang = pos.astype(F32) * inv_ref[...]
        sin = jnp.sin(ang)
        first = lane % (2 * shift) < shift
        cos_ref[...] = jnp.cos(ang)
        s1_ref[...] = jnp.where(first, -sin, 0.0)
        s2_ref[...] = jnp.where(first, 0.0, sin)

    tab = jax.ShapeDtypeStruct((n, LANE), F32)
    return tuple(pl.pallas_call(
        body, name=name, grid=(n // tm,), in_specs=[_vec_spec(LANE)], out_specs=[_row_spec(tm, LANE)] * 3,
        out_shape=[tab, tab, tab], compiler_params=_params(),
    )(inv_lane))


def _ret_tables(n_lat):
    inv = RET_ROPE_BASE ** (-jnp.arange(0, RET_DK, 2, dtype=F32) / RET_DK)
    return _rope_tables_call("ret_tables", n_lat, inv, RET_DK // 2, False)


def _make_ret_pack(tag, n_lat, n_ctx):
    hh = RET_HEADS
    tm = MLA_PACK_ROWS
    k_scale = RET_DK ** -0.5
    shift = RET_DK // 2
    tabs = _ret_tables(n_lat)

    def low_lanes():
        return lax.broadcasted_iota(jnp.int32, (1, LANE), 1) < RET_DK

    def rows(width):
        return pl.BlockSpec((tm, width), lambda i: (i, 0))

    def heads(width):
        return pl.BlockSpec((hh, tm, width), lambda i: (0, i, 0))

    def split_pairs(src_ref, dst_ref, scale, rope):
        keep = low_lanes()
        for j in range(hh // 2):
            blk = src_ref[:, LANE * j:LANE * (j + 1)]
            if scale != 1.0:
                blk = blk * scale
            if rope is not None:
                blk = _rope128(blk, *rope, shift=shift)
            dst_ref[2 * j] = jnp.where(keep, blk, 0.0)
            dst_ref[2 * j + 1] = jnp.where(keep, pltpu.roll(blk, RET_DK, 1), 0.0)

    def merge_pairs(src_refs, dst_ref, scale, rope):
        keep = low_lanes()
        for j in range(hh // 2):
            even = sum(r[2 * j] for r in src_refs)
            odd = sum(r[2 * j + 1] for r in src_refs)
            g = jnp.where(keep, even, pltpu.roll(odd, RET_DK, 1))
            if rope is not None:
                g = _rope128_t(g, *rope, shift=shift)
            dst_ref[:, LANE * j:LANE * (j + 1)] = g * scale if scale != 1.0 else g

    def pack_call(name, n, q, k, v, rope):
        with_q = q is not None

        def body(*refs):
            refs = list(refs)
            q_ref = refs.pop(0) if with_q else None
            k_ref, v_ref = refs.pop(0), refs.pop(0)
            tab = tuple(r[...] for r in refs[:3]) if rope else None
            outs = refs[3:] if rope else refs
            if with_q:
                split_pairs(q_ref, outs[0], 1.0, tab)
                outs = outs[1:]
            split_pairs(k_ref, outs[0], k_scale, tab)
            for h in range(hh):
                outs[1][h] = v_ref[:, RET_DV * h:RET_DV * (h + 1)]

        ins = ([q] if with_q else []) + [k, v] + (list(tabs) if rope else [])
        in_specs = ([rows(q.shape[1])] if with_q else []) + [rows(k.shape[1]), rows(v.shape[1])]
        in_specs += [rows(LANE)] * 3 if rope else []
        n_out = 3 if with_q else 2
        return pl.pallas_call(
            body, name=name, grid=(n // tm,), in_specs=in_specs,
            out_specs=[heads(RET_DKP)] * (n_out - 1) + [heads(RET_DV)],
            out_shape=[jax.ShapeDtypeStruct((hh, n, RET_DKP), F32)] * (n_out - 1)
            + [jax.ShapeDtypeStruct((hh, n, RET_DV), F32)],
            compiler_params=_params(),
        )(*ins)

    def unpack_call(name, n, dqs, dks, dvs, rope):
        with_q = len(dqs) > 0
        uses = len(dks)

        def body(*refs):
            refs = list(refs)
            dq_refs = [refs.pop(0) for _ in range(len(dqs))]
            dk_refs = [refs.pop(0) for _ in range(uses)]
            dv_refs = [refs.pop(0) for _ in range(uses)]
            tab = tuple(r[...] for r in refs[:3]) if rope else None
            outs = refs[3:] if rope else refs
            if with_q:
                merge_pairs(dq_refs, outs[0], 1.0, tab)
                outs = outs[1:]
            merge_pairs(dk_refs, outs[0], k_scale, tab)
            for h in range(hh):
                outs[1][:, RET_DV * h:RET_DV * (h + 1)] = sum(r[h] for r in dv_refs)

        ins = list(dqs) + list(dks) + list(dvs) + (list(tabs) if rope else [])
        in_specs = [heads(RET_DKP)] * (len(dqs) + uses) + [heads(RET_DV)] * uses + ([rows(LANE)] * 3 if rope else [])
        n_out = 3 if with_q else 2
        return pl.pallas_call(
            body, name=name, grid=(n // tm,), in_specs=in_specs,
            out_specs=[rows(hh * RET_DK)] * (n_out - 1) + [rows(hh * RET_DV)],
            out_shape=[jax.ShapeDtypeStruct((n, hh * RET_DK), F32)] * (n_out - 1)
            + [jax.ShapeDtypeStruct((n, hh * RET_DV), F32)],
            compiler_params=_params(),
        )(*ins)

    @jax.custom_vjp
    def ret_pack(rq, rk, rv, crk, crv):
        q, k, v = pack_call(tag + "_lat", n_lat, rq, rk, rv, True)
        k_c, v_c = pack_call(tag + "_ctx", n_ctx, None, crk, crv, False)
        return (q, k, v), (q, k, v), (k_c, v_c), (k_c, v_c)

    def fwd(rq, rk, rv, crk, crv):
        return ret_pack(rq, rk, rv, crk, crv), None

    def bwd(_, cts):
        lat_f, lat_b, ctx_f, ctx_b = cts
        drq, drk, drv = unpack_call(tag + "_dlat", n_lat, [lat_f[0], lat_b[0]], [lat_f[1], lat_b[1]],
                                    [lat_f[2], lat_b[2]], True)
        dcrk, dcrv = unpack_call(tag + "_dctx", n_ctx, [], [ctx_f[0], ctx_b[0]], [ctx_f[1], ctx_b[1]], False)
        return drq, drk, drv, dcrk, dcrv

    ret_pack.defvjp(fwd, bwd)
    return ret_pack


def _ret_out_tile(y, g):
    mu = jnp.mean(y, axis=-1, keepdims=True)
    var = jnp.mean(jnp.square(y - mu), axis=-1, keepdims=True)
    return (g * jax.nn.sigmoid(g)) * ((y - mu) * lax.rsqrt(var + GN_EPS))


def _make_ret_out(tag):
    def specs(tm):
        y_spec = pl.BlockSpec((None, tm, RET_DV), lambda h, i: (h, i, 0))
        g_spec = pl.BlockSpec((tm, RET_DV), lambda h, i: (i, h))
        return y_spec, g_spec

    def fwd_call(yf, yb, g):
        hh, n, _ = yf.shape
        tm = _pick(n, 1024, 16)
        y_spec, g_spec = specs(tm)

        def body(yf_ref, yb_ref, g_ref, o_ref):
            o_ref[...] = _ret_out_tile(yf_ref[...] + yb_ref[...], g_ref[...]).astype(BF16)

        return pl.pallas_call(
            body, name=tag + "_fwd", grid=(hh, n // tm), in_specs=[y_spec, y_spec, g_spec], out_specs=g_spec,
            out_shape=jax.ShapeDtypeStruct((n, hh * RET_DV), BF16), compiler_params=_params(),
        )(yf, yb, g)

    def bwd_call(yf, yb, g, do):
        hh, n, _ = yf.shape
        tm = _pick(n, 1024, 16)
        y_spec, g_spec = specs(tm)

        def body(yf_ref, yb_ref, g_ref, do_ref, dy_ref, dg_ref):
            _, vjp = jax.vjp(_ret_out_tile, yf_ref[...] + yb_ref[...], g_ref[...])
            dy, dg = vjp(do_ref[...].astype(F32))
            dy_ref[...] = dy
            dg_ref[...] = dg

        return pl.pallas_call(
            body, name=tag + "_bwd", grid=(hh, n // tm), in_specs=[y_spec, y_spec, g_spec, g_spec],
            out_specs=[y_spec, g_spec],
            out_shape=[jax.ShapeDtypeStruct(yf.shape, F32), jax.ShapeDtypeStruct(g.shape, F32)],
            compiler_params=_params(),
        )(yf, yb, g, do)

    @jax.custom_vjp
    def ret_out(yf, yb, g):
        return fwd_call(yf, yb, g)

    def fwd(yf, yb, g):
        return fwd_call(yf, yb, g), (yf, yb, g)

    def bwd(res, do):
        dy, dg = bwd_call(*res, do)
        return dy, dy, dg

    ret_out.defvjp(fwd, bwd)
    return ret_out


MLA_DQ_PAD = 2 * LANE
MLA_PACK_ROWS = 256


def _rope128(x, cos, s1, s2, shift=16):
    return x * cos + pltpu.roll(x, LANE - shift, 1) * s1 + pltpu.roll(x, shift, 1) * s2


def _rope128_t(g, cos, s1, s2, shift=16):
    return g * cos + pltpu.roll(g * s1, shift, 1) + pltpu.roll(g * s2, LANE - shift, 1)


def _axial_tables(n_lat):
    half = MLA_ROPE // 2
    inv = AXIAL_BASE ** (-jnp.arange(0, half, 2, dtype=F32) / half)
    return _rope_tables_call("mla_tables", n_lat, inv, half // 2, True)


def _make_mla_pack(tag, n_lat, n_ctx, scale):
    hh = MLA_HEADS
    tm = MLA_PACK_ROWS
    ll = n_lat + n_ctx
    rope0 = hh * MLA_NOPE
    tabs = _axial_tables(n_lat)

    def rope_lanes():
        return lax.broadcasted_iota(jnp.int32, (1, LANE), 1) < MLA_ROPE

    def rows(width):
        return pl.BlockSpec((tm, width), lambda i: (i, 0))

    def heads(width, off):
        return pl.BlockSpec((hh, tm, width), lambda i: (0, i + off, 0))

    def heads_t(width, off):
        return pl.BlockSpec((hh, width, tm), lambda i: (0, 0, i + off))

    def put_kv(kv_ref, kr_rot, k_ref, v_ref, kt_ref, vt_ref):
        kr_b = kr_rot.astype(BF16)
        kr_t = jnp.transpose(kr_rot).astype(BF16)
        for h in range(hh):
            k_nope = kv_ref[:, 2 * LANE * h:2 * LANE * h + MLA_NOPE]
            val = kv_ref[:, 2 * LANE * h + MLA_NOPE:2 * LANE * (h + 1)]
            k_ref[h, :, :MLA_NOPE] = k_nope.astype(BF16)
            k_ref[h, :, MLA_NOPE:] = kr_b
            v_ref[h] = val.astype(BF16)
            kt_ref[h, :MLA_NOPE, :] = jnp.transpose(k_nope).astype(BF16)
            kt_ref[h, MLA_NOPE:, :] = kr_t
            vt_ref[h] = jnp.transpose(val).astype(BF16)

    def fwd_lat(qp, kv, kr):
        def body(qp_ref, kv_ref, kr_ref, cos_ref, s1_ref, s2_ref, q_ref, k_ref, v_ref, kt_ref, vt_ref):
            cos, s1, s2 = cos_ref[...], s1_ref[...], s2_ref[...]
            keep = rope_lanes()
            for j in range(hh // 2):
                rot = _rope128(qp_ref[:, rope0 + LANE * j:rope0 + LANE * (j + 1)], cos, s1, s2)
                q_ref[2 * j, :, MLA_NOPE:] = jnp.where(keep, rot, 0.0).astype(BF16)
                q_ref[2 * j + 1, :, MLA_NOPE:] = jnp.where(keep, pltpu.roll(rot, MLA_ROPE, 1), 0.0).astype(BF16)
            for h in range(hh):
                q_ref[h, :, :MLA_NOPE] = qp_ref[:, MLA_NOPE * h:MLA_NOPE * (h + 1)].astype(BF16)
            kr_rot = jnp.where(keep, _rope128(kr_ref[...], cos, s1, s2), 0.0)
            put_kv(kv_ref, kr_rot, k_ref, v_ref, kt_ref, vt_ref)

        return pl.pallas_call(
            body, name=tag + "_lat", grid=(n_lat // tm,),
            in_specs=[rows(qp.shape[1]), rows(kv.shape[1]), rows(LANE), rows(LANE), rows(LANE), rows(LANE)],
            out_specs=[heads(MLA_DQ_PAD, 0), heads(MLA_DQ_PAD, 0), heads(MLA_V, 0), heads_t(MLA_DQ_PAD, 0),
                       heads_t(MLA_V, 0)],
            out_shape=[jax.ShapeDtypeStruct((hh, n_lat, MLA_DQ_PAD), BF16),
                       jax.ShapeDtypeStruct((hh, ll, MLA_DQ_PAD), BF16), jax.ShapeDtypeStruct((hh, ll, MLA_V), BF16),
                       jax.ShapeDtypeStruct((hh, MLA_DQ_PAD, ll), BF16), jax.ShapeDtypeStruct((hh, MLA_V, ll), BF16)],
            compiler_params=_params(),
        )(qp, kv, kr, *tabs)

    def fwd_ctx(kv_c, kr_c, bufs):
        def body(kv_ref, kr_ref, k_in, v_in, kt_in, vt_in, k_ref, v_ref, kt_ref, vt_ref):
            kr_rot = jnp.where(rope_lanes(), kr_ref[...], 0.0)
            put_kv(kv_ref, kr_rot, k_ref, v_ref, kt_ref, vt_ref)

        any_spec = pl.BlockSpec(memory_space=pl.ANY)
        off = n_lat // tm
        return pl.pallas_call(
            body, name=tag + "_ctx", grid=(n_ctx // tm,),
            in_specs=[rows(kv_c.shape[1]), rows(LANE)] + [any_spec] * 4,
            out_specs=[heads(MLA_DQ_PAD, off), heads(MLA_V, off), heads_t(MLA_DQ_PAD, off), heads_t(MLA_V, off)],
            out_shape=[jax.ShapeDtypeStruct(b.shape, BF16) for b in bufs],
            input_output_aliases={2: 0, 3: 1, 4: 2, 5: 3}, compiler_params=_params(),
        )(kv_c, kr_c, *bufs)

    def take_kv(dk_ref, dv_ref, dkv_ref):
        dkr = jnp.zeros((tm, LANE), F32)
        for h in range(hh):
            dkv_ref[:, 2 * LANE * h:2 * LANE * h + MLA_NOPE] = dk_ref[h, :, :MLA_NOPE].astype(F32)
            dkv_ref[:, 2 * LANE * h + MLA_NOPE:2 * LANE * (h + 1)] = dv_ref[h].astype(F32)
            dkr = dkr + dk_ref[h, :, MLA_NOPE:].astype(F32)
        return jnp.where(rope_lanes(), dkr, 0.0)

    def bwd_lat(dqt, dk, dv, qp_width, kv_width):
        def body(dqt_ref, dk_ref, dv_ref, cos_ref, s1_ref, s2_ref, dqp_ref, dkv_ref, dkr_ref):
            cos, s1, s2 = cos_ref[...], s1_ref[...], s2_ref[...]
            keep = rope_lanes()
            for j in range(hh // 2):
                even = jnp.transpose(dqt_ref[2 * j]) * scale
                odd = jnp.transpose(dqt_ref[2 * j + 1]) * scale
                dqp_ref[:, MLA_NOPE * 2 * j:MLA_NOPE * (2 * j + 1)] = even[:, :MLA_NOPE]
                dqp_ref[:, MLA_NOPE * (2 * j + 1):MLA_NOPE * (2 * j + 2)] = odd[:, :MLA_NOPE]
                g = jnp.where(keep, even[:, MLA_NOPE:], pltpu.roll(odd[:, MLA_NOPE:], MLA_ROPE, 1))
                dqp_ref[:, rope0 + LANE * j:rope0 + LANE * (j + 1)] = _rope128_t(g, cos, s1, s2)
            dkr_ref[...] = jnp.where(keep, _rope128_t(take_kv(dk_ref, dv_ref, dkv_ref), cos, s1, s2), 0.0)

        return pl.pallas_call(
            body, name=tag + "_dlat", grid=(n_lat // tm,),
            in_specs=[pl.BlockSpec((hh, MLA_DQ_PAD, tm), lambda i: (0, 0, i)),
                      heads(MLA_DQ_PAD, 0), heads(MLA_V, 0), rows(LANE), rows(LANE), rows(LANE)],
            out_specs=[rows(qp_width), rows(kv_width), rows(LANE)],
            out_shape=[jax.ShapeDtypeStruct((n_lat, qp_width), F32), jax.ShapeDtypeStruct((n_lat, kv_width), F32),
                       jax.ShapeDtypeStruct((n_lat, LANE), F32)],
            compiler_params=_params(),
        )(dqt, dk, dv, *tabs)

    def bwd_ctx(dk, dv, kv_width):
        def body(dk_ref, dv_ref, dkv_ref, dkr_ref):
            dkr_ref[...] = take_kv(dk_ref, dv_ref, dkv_ref)

        off = n_lat // tm
        return pl.pallas_call(
            body, name=tag + "_dctx", grid=(n_ctx // tm,),
            in_specs=[heads(MLA_DQ_PAD, off), heads(MLA_V, off)],
            out_specs=[rows(kv_width), rows(LANE)],
            out_shape=[jax.ShapeDtypeStruct((n_ctx, kv_width), F32), jax.ShapeDtypeStruct((n_ctx, LANE), F32)],
            compiler_params=_params(),
        )(dk, dv)

    def pack(qp, kv, kr, kv_c, kr_c):
        q, *bufs = fwd_lat(qp, kv, kr)
        return (q, *fwd_ctx(kv_c, kr_c, bufs))

    def unpack(dqt, dk, dv):
        qp_width, kv_width = hh * (MLA_NOPE + MLA_ROPE), hh * (MLA_NOPE + MLA_V)
        dqp, dkv, dkr = bwd_lat(dqt, dk, dv, qp_width, kv_width)
        dkv_c, dkr_c = bwd_ctx(dk, dv, kv_width)
        return dqp, dkv, dkr, dkv_c, dkr_c

    return pack, unpack


def _make_mla(tag, n_lat, n_ctx):
    scale = (MLA_NOPE + MLA_ROPE) ** -0.5
    pack, unpack = _make_mla_pack(tag + "pack", n_lat, n_ctx, scale)
    attn_fwd, attn_delta, attn_bwd = _make_attention(tag, scale, MLA_NOPE + MLA_ROPE)

    @jax.custom_vjp
    def mla(qp, kv, kr, kv_c, kr_c):
        q, k, _, _, vt = pack(qp, kv, kr, kv_c, kr_c)
        return attn_fwd(q, k, vt)[0]

    def fwd(qp, kv, kr, kv_c, kr_c):
        q, k, v, kt, vt = pack(qp, kv, kr, kv_c, kr_c)
        o, lse = attn_fwd(q, k, vt)
        return o, (q, k, kt, v, o, lse)

    def bwd(res, do):
        q, k, kt, v, o, lse = res
        delta = attn_delta(o, do, q.shape[0])
        dqt, dk, dv = attn_bwd(q, k, kt, v, do, lse, delta)
        return unpack(dqt, dk, dv)

    mla.defvjp(fwd, bwd)
    return mla


def _make_attention(tag, scale, dq_live=None):
    neg_big = -1e30
    log2e = 1.4426950408889634
    sub = 256

    def fwd_call(q, k, vt):
        hh, n, dq = q.shape
        dv, ll = vt.shape[1], vt.shape[2]
        tq, tk = _pick(n, 1024), _pick(ll, 1408)
        sb = sub if tk % sub == 0 else tk
        c2 = scale * log2e
        k_steps = ll // tk

        def body(q_ref, k_ref, vt_ref, o_ref, lse_ref, m_scr, l_scr, acc_scr, s_scr, p_scr):
            j = pl.program_id(2)

            @pl.when(j == 0)
            def _():
                m_scr[...] = jnp.full_like(m_scr, neg_big)
                l_scr[...] = jnp.zeros_like(l_scr)
                acc_scr[...] = jnp.zeros_like(acc_scr)

            q_t = q_ref[...]
            m_prev = m_scr[...]
            m_new = m_prev
            for kk in range(tk // sb):
                rows = slice(kk * sb, (kk + 1) * sb)
                s_t = _dot(k_ref[rows, :], q_t, NT)
                s_scr[rows, :] = s_t
                m_new = jnp.maximum(m_new, jnp.max(s_t, axis=0, keepdims=True))
            mc = m_new * c2
            l_part = jnp.zeros_like(m_new)
            for kk in range(tk // sb):
                rows = slice(kk * sb, (kk + 1) * sb)
                p_t = jnp.exp2(s_scr[rows, :] * c2 - mc)
                l_part = l_part + jnp.sum(p_t, axis=0, keepdims=True)
                p_scr[rows, :] = p_t.astype(BF16)
            alpha = jnp.exp2((m_prev - m_new) * c2)
            l_scr[...] = alpha * l_scr[...] + l_part
            acc_scr[...] = alpha * acc_scr[...] + _dot(vt_ref[...], p_scr[...], NN)
            m_scr[...] = m_new

            @pl.when(j == k_steps - 1)
            def _():
                o_ref[...] = jnp.transpose(acc_scr[...] / l_scr[...]).astype(BF16)
                lse_ref[...] = m_scr[...] * scale + jnp.log(l_scr[...])

        return pl.pallas_call(
            body, name=tag + "_fwd", grid=(hh, n // tq, k_steps),
            in_specs=[pl.BlockSpec((None, tq, dq), lambda h, i, j: (h, i, 0)),
                      pl.BlockSpec((None, tk, dq), lambda h, i, j: (h, j, 0)),
                      pl.BlockSpec((None, dv, tk), lambda h, i, j: (h, 0, j))],
            out_specs=[pl.BlockSpec((tq, dv), lambda h, i, j: (i, h)),
                       pl.BlockSpec((None, 1, tq), lambda h, i, j: (h, 0, i))],
            out_shape=[jax.ShapeDtypeStruct((n, hh * dv), BF16), jax.ShapeDtypeStruct((hh, 1, n), F32)],
            scratch_shapes=[pltpu.VMEM((1, tq), F32), pltpu.VMEM((1, tq), F32), pltpu.VMEM((dv, tq), F32),
                            pltpu.VMEM((tk, tq), F32), pltpu.VMEM((tk, tq), BF16)],
            compiler_params=_params(),
        )(q, k, vt)

    def delta_call(o, do, hh):
        n = o.shape[0]
        dv = o.shape[1] // hh
        tq = _pick(n, 1024)

        def body(o_ref, do_ref, d_ref):
            prod_t = jnp.transpose(o_ref[...].astype(F32) * do_ref[...].astype(F32))
            d_ref[...] = jnp.sum(prod_t, axis=0, keepdims=True)

        spec = pl.BlockSpec((tq, dv), lambda h, i: (i, h))
        return pl.pallas_call(
            body, name=tag + "_delta", grid=(hh, n // tq), in_specs=[spec, spec],
            out_specs=pl.BlockSpec((None, 1, tq), lambda h, i: (h, 0, i)),
            out_shape=jax.ShapeDtypeStruct((hh, 1, n), F32), compiler_params=_params(),
        )(o, do)

    def bwd_call(q, k, kt, v, do, lse, delta):
        hh, n, dq = q.shape
        ll, dv = k.shape[1], v.shape[2]
        tq, tk = _pick(n, 1024), _pick(ll, 1408)
        sb = tk
        c2 = scale * log2e
        q_steps = n // tq
        live = dq_live or dq

        def body(q_ref, k_ref, kt_ref, v_ref, do_ref, lse_ref, d_ref, dqt_ref, dk_ref, dv_ref, dk_scr, dv_scr):
            j = pl.program_id(1)
            i = pl.program_id(2)

            @pl.when(i == 0)
            def _():
                dk_scr[...] = jnp.zeros_like(dk_scr)
                dv_scr[...] = jnp.zeros_like(dv_scr)

            q_t, do_t = q_ref[...], do_ref[...]
            lse2 = lse_ref[...] * log2e
            delta_t = d_ref[...]
            dq_part = None
            for kk in range(tk // sb):
                rows = slice(kk * sb, (kk + 1) * sb)
                s_t = _dot(k_ref[rows, :], q_t, NT)
                p_t = jnp.exp2(s_t * c2 - lse2)
                ds_t = p_t * (_dot(v_ref[rows, :], do_t, NT) - delta_t)
                dv_scr[rows, :] += _dot(p_t, do_t, NN)
                dk_scr[rows, :] += _dot(ds_t, q_t, NN)
                part = _dot(kt_ref[:live, rows], ds_t, NN)
                dq_part = part if dq_part is None else dq_part + part
            cols = pl.ds(pl.multiple_of(i * tq, tq), tq)

            @pl.when(j == 0)
            def _():
                dqt_ref[:live, cols] = dq_part
                if live < dq:
                    dqt_ref[live:, cols] = jnp.zeros((dq - live, tq), F32)

            @pl.when(j > 0)
            def _():
                dqt_ref[:live, cols] += dq_part

            @pl.when(i == q_steps - 1)
            def _():
                dk_ref[...] = (dk_scr[...] * scale).astype(BF16)
                dv_ref[...] = dv_scr[...].astype(BF16)

        return pl.pallas_call(
            body, name=tag + "_bwd", grid=(hh, ll // tk, q_steps),
            in_specs=[pl.BlockSpec((None, tq, dq), lambda h, j, i: (h, i, 0)),
                      pl.BlockSpec((None, tk, dq), lambda h, j, i: (h, j, 0)),
                      pl.BlockSpec((None, dq, tk), lambda h, j, i: (h, 0, j)),
                      pl.BlockSpec((None, tk, dv), lambda h, j, i: (h, j, 0)),
                      pl.BlockSpec((tq, dv), lambda h, j, i: (i, h)),
                      pl.BlockSpec((None, 1, tq), lambda h, j, i: (h, 0, i)),
                      pl.BlockSpec((None, 1, tq), lambda h, j, i: (h, 0, i))],
            out_specs=[pl.BlockSpec((None, dq, n), lambda h, j, i: (h, 0, 0)),
                       pl.BlockSpec((None, tk, dq), lambda h, j, i: (h, j, 0)),
                       pl.BlockSpec((None, tk, dv), lambda h, j, i: (h, j, 0))],
            out_shape=[jax.ShapeDtypeStruct((hh, dq, n), F32), jax.ShapeDtypeStruct((hh, ll, dq), BF16),
                       jax.ShapeDtypeStruct((hh, ll, dv), BF16)],
            scratch_shapes=[pltpu.VMEM((tk, dq), F32), pltpu.VMEM((tk, dv), F32)],
            compiler_params=_params(),
        )(q, k, kt, v, do, lse, delta)

    return fwd_call, delta_call, bwd_call


def _loss_tile(x, g, tgt):
    r = lax.rsqrt(jnp.mean(x * x, axis=-1, keepdims=True) + RMS_EPS)
    err = x * r * g - tgt
    per_tok = jnp.mean(err * err, axis=-1, keepdims=True)
    return 0.5 * jnp.sum(per_tok, axis=0, keepdims=True)


def _make_final_loss(tag):
    def fwd_call(x, g, tgt):
        t, d = x.shape
        tm = _pick(t, 512, 16)

        def body(x_ref, g_ref, t_ref, l_ref):
            l_ref[...] = jnp.broadcast_to(_loss_tile(x_ref[...], g_ref[...], t_ref[...]), (1, LANE))

        parts = pl.pallas_call(
            body, name=tag + "_fwd", grid=(t // tm,),
            in_specs=[_row_spec(tm, d), _vec_spec(d), _row_spec(tm, d)],
            out_specs=pl.BlockSpec((None, 1, LANE), lambda i: (i, 0, 0)),
            out_shape=jax.ShapeDtypeStruct((t // tm, 1, LANE), F32), compiler_params=_params(),
        )(x, g, tgt)
        return jnp.sum(parts[:, 0, 0])

    def bwd_call(x, g, tgt, dl):
        t, d = x.shape
        tm = _pick(t, 256, 16)

        def body(x_ref, g_ref, t_ref, dl_ref, dx_ref, dg_ref):
            _, vjp = jax.vjp(_loss_tile, x_ref[...], g_ref[...], t_ref[...])
            dx, dg, _ = vjp(dl_ref[...])
            dx_ref[...] = dx

            @pl.when(pl.program_id(0) == 0)
            def _():
                dg_ref[...] = jnp.zeros_like(dg_ref)

            dg_ref[...] += dg

        return pl.pallas_call(
            body, name=tag + "_bwd", grid=(t // tm,),
            in_specs=[_row_spec(tm, d), _vec_spec(d), _row_spec(tm, d), pl.BlockSpec((1, 1), lambda i: (0, 0))],
            out_specs=[_row_spec(tm, d), _vec_spec(d)],
            out_shape=[jax.ShapeDtypeStruct((t, d), F32), jax.ShapeDtypeStruct((1, d), F32)],
            compiler_params=_params(),
        )(x, g, tgt, dl)

    @jax.custom_vjp
    def final_loss(x, g, tgt):
        return fwd_call(x, g, tgt)

    def fwd(x, g, tgt):
        return fwd_call(x, g, tgt), (x, g, tgt)

    def bwd(res, dl):
        x, g, tgt = res
        dx, dg = bwd_call(x, g, tgt, dl.reshape(1, 1).astype(F32))
        return dx, dg, jnp.zeros_like(tgt)

    final_loss.defvjp(fwd, bwd)
    return final_loss


def _exchange(arrays, gather, name):
    n = len(arrays)

    def body(*refs):
        ins, outs = refs[:n], refs[n:2 * n]
        send_sems, recv_sems, local_sems = refs[2 * n:]
        me = 4 * lax.axis_index("x") + 2 * lax.axis_index("y") + lax.axis_index("c")

        def remote(a, d, wait_side=False):
            peer = (me + d) % N_DEV
            origin = (me + N_DEV - d) % N_DEV
            src = ins[a] if gather else ins[a].at[peer]
            dst = outs[a].at[origin if wait_side else me]
            return pltpu.make_async_remote_copy(
                src_ref=src, dst_ref=dst, send_sem=send_sems.at[a, d - 1], recv_sem=recv_sems.at[a, d - 1],
                device_id=(peer // 4, (peer // 2) % 2, peer % 2), device_id_type=pl.DeviceIdType.MESH)

        def local(a):
            src = ins[a] if gather else ins[a].at[me]
            return pltpu.make_async_copy(src, outs[a].at[me], local_sems.at[a])

        for a in range(n):
            for d in range(1, N_DEV):
                remote(a, d).start()
            local(a).start()
        for a in range(n):
            local(a).wait()
            for d in range(1, N_DEV):
                remote(a, d, wait_side=True).wait_recv()
                remote(a, d).wait_send()

    out_shape = []
    for arr in arrays:
        shape = (N_DEV,) + arr.shape if gather else arr.shape
        out_shape.append(jax.ShapeDtypeStruct(shape, arr.dtype))
    any_spec = pl.BlockSpec(memory_space=pl.ANY)
    return pl.pallas_call(
        body, name=name, in_specs=[any_spec] * n, out_specs=[any_spec] * n, out_shape=out_shape,
        scratch_shapes=[pltpu.SemaphoreType.DMA((n, N_DEV - 1)), pltpu.SemaphoreType.DMA((n, N_DEV - 1)),
                        pltpu.SemaphoreType.DMA((n,))],
        compiler_params=pltpu.CompilerParams(has_side_effects=True),
    )(*arrays)


def _split_copy(ins, lands, send_sems, recv_sems, a, d, gather, wait_side):
    me = 4 * lax.axis_index("x") + 2 * lax.axis_index("y") + lax.axis_index("c")
    peer = (me + d) % N_DEV
    origin = (me + N_DEV - d) % N_DEV
    return pltpu.make_async_remote_copy(
        src_ref=ins[a] if gather else ins[a].at[peer], dst_ref=lands[a].at[origin if wait_side else me],
        send_sem=send_sems.at[a * (N_DEV - 1) + d - 1], recv_sem=recv_sems.at[a * (N_DEV - 1) + d - 1],
        device_id=(peer // 4, (peer // 2) % 2, peer % 2), device_id_type=pl.DeviceIdType.MESH)


def _exchange_start(srcs, lands, after, gather, name):
    n = len(srcs)

    def body(*refs):
        ins, lnd = refs[:n], refs[n:2 * n]
        send_sems, recv_sems = refs[2 * n + 1], refs[2 * n + 2]
        for a in range(n):
            for d in range(1, N_DEV):
                _split_copy(ins, lnd, send_sems, recv_sems, a, d, gather, False).start()

    hbm = pl.BlockSpec(memory_space=pltpu.HBM)
    sem = pl.BlockSpec(memory_space=pltpu.SEMAPHORE)
    bufs = [pltpu.with_memory_space_constraint(t, pltpu.HBM) for t in list(srcs) + list(lands) + [after]]
    res = pl.pallas_call(
        body, name=name,
        in_specs=[hbm] * (2 * n + 1), out_specs=[sem, sem] + [hbm] * (2 * n + 1),
        out_shape=[pltpu.SemaphoreType.DMA((n * (N_DEV - 1),)), pltpu.SemaphoreType.DMA((n * (N_DEV - 1),))]
        + [pltpu.HBM(t.shape, t.dtype) for t in bufs],
        input_output_aliases={i: 2 + i for i in range(2 * n + 1)},
        compiler_params=pltpu.CompilerParams(has_side_effects=pltpu.SideEffectType.DATAFLOW_SIDE_EFFECTING),
    )(*bufs)
    return res[0], res[1], res[2:2 + n], res[2 + n:2 + 2 * n], res[-1]


def _exchange_wait(send_sems, recv_sems, srcs, lands, after, gather, name):
    n = len(srcs)

    def body(*refs):
        ins, lnd = refs[:n], refs[n:2 * n]
        send_sems_ref, recv_sems_ref = refs[2 * n], refs[2 * n + 1]
        for a in range(n):
            for d in range(1, N_DEV):
                _split_copy(ins, lnd, send_sems_ref, recv_sems_ref, a, d, gather, False).wait_send()
                _split_copy(ins, lnd, send_sems_ref, recv_sems_ref, a, d, gather, True).wait_recv()

    hbm = pl.BlockSpec(memory_space=pltpu.HBM)
    sem = pl.BlockSpec(memory_space=pltpu.SEMAPHORE)
    bufs = list(srcs) + list(lands)
    res = pl.pallas_call(
        body, name=name,
        in_specs=[hbm] * (2 * n) + [sem, sem, pl.BlockSpec(memory_space=pl.ANY)],
        out_specs=[hbm] * (2 * n),
        out_shape=[pltpu.HBM(t.shape, t.dtype) for t in bufs],
        input_output_aliases={i: i for i in range(2 * n)},
        compiler_params=pltpu.CompilerParams(has_side_effects=pltpu.SideEffectType.DATAFLOW_SIDE_EFFECTING),
    )(*bufs, send_sems, recv_sems, after)
    return res[n:]


def _own_slot(block, me):
    empty = lax.empty((N_DEV,) + block.shape, block.dtype)
    return lax.dynamic_update_slice(empty, block[None], (me,) + (0,) * block.ndim)


def _coords():
    return lax.axis_index("x"), lax.axis_index("y"), lax.axis_index("c")


def _other_chips(x, y):
    return [(1 - x, y), (x, 1 - y), (1 - x, 1 - y)]


def _gather_two_level(arrays, name):
    n = len(arrays)

    def body(*refs):
        ins, outs = refs[:n], refs[n:2 * n]
        send_sems, recv_sems, local_sems = refs[2 * n:]
        x, y, c = _coords()
        me, sib = (x, y, c), (x, y, 1 - c)
        chips = _other_chips(x, y)

        def copy(a, k, block, to, from_input=False):
            slot = 4 * block[0] + 2 * block[1] + block[2]
            return pltpu.make_async_remote_copy(
                src_ref=ins[a] if from_input else outs[a].at[slot], dst_ref=outs[a].at[slot],
                send_sem=send_sems.at[a, k], recv_sem=recv_sems.at[a, k],
                device_id=to, device_id_type=pl.DeviceIdType.MESH)

        def local(a):
            return pltpu.make_async_copy(ins[a], outs[a].at[4 * x + 2 * y + c], local_sems.at[a])

        for a in range(n):
            for j, chip in enumerate(chips):
                copy(a, 1 + j, me, (*chip, c), True).start()
            copy(a, 0, me, sib, True).start()
            local(a).start()
        for a in range(n):
            for j, chip in enumerate(chips):
                copy(a, 1 + j, (*chip, c), me).wait_recv()
                copy(a, 4 + j, (*chip, c), sib).start()
        for a in range(n):
            copy(a, 0, sib, me).wait_recv()
            for j, chip in enumerate(chips):
                copy(a, 4 + j, (*chip, 1 - c), me).wait_recv()
            for k in range(N_DEV - 1):
                copy(a, k, me, sib, True).wait_send()
            local(a).wait()

    any_spec = pl.BlockSpec(memory_space=pl.ANY)
    return pl.pallas_call(
        body, name=name, in_specs=[any_spec] * n, out_specs=[any_spec] * n,
        out_shape=[jax.ShapeDtypeStruct((N_DEV,) + arr.shape, arr.dtype) for arr in arrays],
        scratch_shapes=[pltpu.SemaphoreType.DMA((n, N_DEV - 1)), pltpu.SemaphoreType.DMA((n, N_DEV - 1)),
                        pltpu.SemaphoreType.DMA((n,))],
        compiler_params=pltpu.CompilerParams(has_side_effects=True),
    )(*arrays)


def _make_gather_op(tag):
    @jax.custom_vjp
    def gather_op(xl):
        return _exchange([xl], True, tag + "_gather")[0]

    def fwd(xl):
        return gather_op(xl), None

    def bwd(_, g):
        return (jnp.sum(_exchange([g], False, tag + "_scatter")[0], axis=0),)

    gather_op.defvjp(fwd, bwd)
    return gather_op


def _adamw(gstack, w, m, v, name):
    s, r, cn = gstack.shape
    tr = _pick(r, max(8, (2 * 1024 * 1024) // (4 * cn) // 8 * 8), 8)
    c1 = 1.0 - ADAM_B1 ** ADAM_STEP
    c2 = 1.0 - ADAM_B2 ** ADAM_STEP

    def body(g_ref, w_ref, m_ref, v_ref, go_ref, d_ref, mo_ref, vo_ref):
        g = g_ref[0].astype(F32)
        for q in range(1, s):
            g = g + g_ref[q].astype(F32)
        m_new = ADAM_B1 * m_ref[...] + (1.0 - ADAM_B1) * g
        v_new = ADAM_B2 * v_ref[...] + (1.0 - ADAM_B2) * (g * g)
        go_ref[...] = g
        mo_ref[...] = m_new
        vo_ref[...] = v_new
        d_ref[...] = -ADAM_LR * ((m_new / c1) / (jnp.sqrt(v_new / c2) + ADAM_EPS) + ADAM_WD * w_ref[...])

    tile = pl.BlockSpec((tr, cn), lambda i: (i, 0))
    out = jax.ShapeDtypeStruct((r, cn), F32)
    return pl.pallas_call(
        body, name=name, grid=(r // tr,),
        in_specs=[pl.BlockSpec((s, tr, cn), lambda i: (0, i, 0)), tile, tile, tile],
        out_specs=[tile, tile, tile, tile], out_shape=[out, out, out, out],
        compiler_params=_params(),
    )(gstack, w, m, v)


def _cols_from_stack(w):
    return jnp.swapaxes(w, 0, 1).reshape(w.shape[1], N_DEV * w.shape[2])


def _ada_vectors(p, silu_c_all, me):
    d = p["c_ctx"].shape[0]
    n_a = p["ada_w"].shape[1]
    a_in = jnp.concatenate([silu_c_all, jax.nn.silu(p["c_ctx"])[None, :], jnp.zeros((7, d), F32)], axis=0)
    b_loc = lax.dynamic_slice(p["ada_b"], (0, me * n_a), (1, n_a))
    r_loc = _make_small_mm("ada")(a_in, p["ada_w"]) + b_loc
    r_full = _make_gather_op("ada")(r_loc)
    m_lat = lax.dynamic_index_in_dim(r_full, me, axis=1, keepdims=False).reshape(N_MOD, 1, d)
    m_ctx = r_full[:, N_DEV, :].reshape(N_MOD, 1, d)
    return m_lat, m_ctx


def _stage_a_fwd(p_ada, x, ctx, ng, w_in, wait_w_out, silu_c_all, me):
    (m_lat, m_ctx), vjp_ada = jax.vjp(lambda q: _ada_vectors(q, silu_c_all, me), p_ada)
    lat, cx = _ffn_parts("ffn1"), _ffn_parts("ffn1c")
    h, a, gu = lat[0](x, ng, m_lat[0], m_lat[1], w_in)
    hc, ac, guc = cx[0](ctx, ng, m_ctx[0], m_ctx[1], w_in)
    w_out = wait_w_out(a)
    x1, f1 = lat[1](a, w_out, x, m_lat[2])
    c1, f1c = cx[1](ac, w_out, ctx, m_ctx[2])
    res = dict(vjp_ada=vjp_ada, m_lat=m_lat, m_ctx=m_ctx, x=x, ctx=ctx, ng=ng, w_in=w_in, w_out=w_out,
               lat=(h, a, gu, f1), cx=(hc, ac, guc, f1c))
    return (x1, c1, m_lat, m_ctx), res


def _stage_a_bwd(res, dx1, dc1, dm_lat, dm_ctx, start_grads):
    lat, cx = _ffn_parts("ffn1"), _ffn_parts("ffn1c")
    m_lat, m_ctx, ng, w_in, w_out = res["m_lat"], res["m_ctx"], res["ng"], res["w_in"], res["w_out"]
    hc, ac, guc, f1c = res["cx"]
    dgate_c, dgu_c, dw_out_c = cx[2](dc1, f1c, m_ctx[2], w_out, guc, ac)
    _, dng_c, dsh_c, dsc_c, dw_in_c = cx[3](res["ctx"], ng, m_ctx[0], m_ctx[1], w_in, hc, dgu_c, dc1)
    h, a, gu, f1 = res["lat"]
    dgate, dgu, dw_out = lat[2](dx1, f1, m_lat[2], w_out, gu, a)
    dgu = start_grads("ffn1_w_out", dw_out + dw_out_c, dgu)
    dw_in = _ffn_dwin_bwd(h, dgu, "ffn1_dwin") + dw_in_c
    dgu = start_grads("ffn1_w_in", dw_in, dgu)
    dh = _ffn_dh_bwd(dgu, w_in, "ffn1_dh")
    dx, dng, dsc, dsh = _norm_mod_bwd(res["x"], ng, m_lat[1], m_lat[0], dh, dx1, "ffn1_dnorm")

    def rows(dsh_, dsc_, dgate_):
        return jnp.concatenate([dsh_, dsc_, dgate_, jnp.zeros((N_MOD - 3,) + dsh_.shape, F32)[:, 0]], axis=0)[:, None, :]

    (g_ada,) = res["vjp_ada"]((dm_lat + rows(dsh, dsc, dgate), dm_ctx + rows(dsh_c, dsc_c, dgate_c)))
    return g_ada, dx, dng + dng_c


def _stage_b(p, x1, c1, m_lat, m_ctx):
    n_lat, d = x1.shape
    n_ctx = c1.shape[0]
    w_mix = jnp.pad(_cols_from_stack(p["mix_w_in"]), ((0, 0), (0, MIX_IN_PAD - MIX_IN)))
    proj, x1 = _make_norm_proj_carry("mix")(x1, p["norm2_g"], m_lat[3], m_lat[4], w_mix)
    proj_c = _make_norm_proj("mixc")(c1, p["norm2_g"], m_ctx[3], m_ctx[4], w_mix)
    widths = SPLITS[:6] + (LANE,)
    rq, rk, rv, rg, cq, ckv, kr = _make_split("mixsplit", widths, MIX_IN_PAD)(proj)
    _, crk, crv, _, _, cckv, ckr = _make_split("mixsplitc", widths, MIX_IN_PAD)(proj_c)

    zq = jnp.zeros((1, MLA_Q_RANK), F32)
    zkv = jnp.zeros((1, MLA_KV_RANK), F32)
    w_uq3 = _cols_from_stack(p["mla_w_uq"]).reshape(MLA_Q_RANK, MLA_HEADS, MLA_NOPE + MLA_ROPE)
    w_uq = jnp.concatenate([w_uq3[:, :, :MLA_NOPE].reshape(MLA_Q_RANK, -1),
                            w_uq3[:, :, MLA_NOPE:].reshape(MLA_Q_RANK, -1)], axis=1)
    w_ukv = _cols_from_stack(p["mla_w_ukv"])
    q = _make_norm_proj("uq")(cq, p["mla_q_norm_g"], zq, zq, w_uq)
    kv = _make_norm_proj("ukv")(ckv, p["mla_kv_norm_g"], zkv, zkv, w_ukv)
    kv_c = _make_norm_proj("ukvc")(cckv, p["mla_kv_norm_g"], zkv, zkv, w_ukv)

    lg_f = jax.nn.log_sigmoid(p["ret_decay_fwd"][0])
    lg_b = jax.nn.log_sigmoid(p["ret_decay_bwd"][0])
    lat_f, lat_b, ctx_f, ctx_b = _make_ret_pack("retpack", n_lat, n_ctx)(rq, rk, rv, crk, crv)
    assert n_ctx == RET_CHUNK, "the context prefix is one retention chunk"

    def lanes(lg):
        return jnp.broadcast_to(lg[:, None, None], (RET_HEADS, 1, LANE))

    s0_f = _make_ctx_state("retcf", False)(*ctx_f, lanes(lg_f))
    s0_b = _make_ctx_state("retcb", True)(*ctx_b, lanes(lg_b))
    y_f = _make_ret_dir("retf", False)(*lat_f, lanes(lg_f), s0_f)
    y_b = _make_ret_dir("retb", True)(*lat_b, lanes(lg_b), s0_b)
    ret_o = _make_ret_out("reto")(y_f, y_b, rg)

    mla_o = _make_mla("mla", n_lat, n_ctx)(q, kv, kr, kv_c, ckr)

    w_mo = p["mix_w_out"].reshape(-1, d)
    return _make_res_proj("mixo")(jnp.concatenate([ret_o, mla_o], axis=-1), w_mo, x1, m_lat[5])


def _stage_c(p, x2, m_lat, tgt):
    x3 = _make_ffn_block("ffn2")(x2, p["norm3_g"], m_lat[6], m_lat[7], m_lat[8], p["ffn2_w_in"], p["ffn2_w_out"])
    return _make_final_loss("loss")(x3, p["final_norm_g"], tgt)


FIRST = ("ffn1_w_in", "ffn1_w_out")
MID = ("mix_w_in", "mla_w_uq", "mla_w_ukv", "mix_w_out")
LAST = ("ffn2_w_in", "ffn2_w_out")
BIG = FIRST + MID + LAST
SMALL = ("c_ctx", "ada_b", "norm1_g", "norm2_g", "ret_decay_fwd", "ret_decay_bwd", "mla_q_norm_g",
         "mla_kv_norm_g", "norm3_g", "final_norm_g")
WEIGHTS = ("c_ctx", "ada_w", "ada_b", "norm1_g", "ffn1_w_in", "ffn1_w_out", "norm2_g", "mix_w_in", "ret_decay_fwd",
           "ret_decay_bwd", "mla_q_norm_g", "mla_w_uq", "mla_kv_norm_g", "mla_w_ukv", "mix_w_out", "norm3_g",
           "ffn2_w_in", "ffn2_w_out", "final_norm_g")


def _pack(parts):
    flat = jnp.concatenate([t.reshape(-1) for t in parts])
    pad = (-flat.shape[0]) % LANE
    return jnp.pad(flat, (0, pad)).reshape(1, -1)


def _unpack(flat, like):
    out, off = [], 0
    for t in like:
        out.append(flat[0, off:off + t.size].reshape(t.shape))
        off += t.size
    return out


def kernel(x, c, ctx, c_ctx, ada_w, ada_b, norm1_g, ffn1_w_in, ffn1_w_out, norm2_g, mix_w_in, ret_decay_fwd, ret_decay_bwd, mla_q_norm_g, mla_w_uq, mla_kv_norm_g, mla_w_ukv, mix_w_out, norm3_g, ffn2_w_in, ffn2_w_out, final_norm_g, loss_target, m_c_ctx, m_ada_w, m_ada_b, m_norm1_g, m_ffn1_w_in, m_ffn1_w_out, m_norm2_g, m_mix_w_in, m_ret_decay_fwd, m_ret_decay_bwd, m_mla_q_norm_g, m_mla_w_uq, m_mla_kv_norm_g, m_mla_w_ukv, m_mix_w_out, m_norm3_g, m_ffn2_w_in, m_ffn2_w_out, m_final_norm_g, v_c_ctx, v_ada_w, v_ada_b, v_norm1_g, v_ffn1_w_in, v_ffn1_w_out, v_norm2_g, v_mix_w_in, v_ret_decay_fwd, v_ret_decay_bwd, v_mla_q_norm_g, v_mla_w_uq, v_mla_kv_norm_g, v_mla_w_ukv, v_mix_w_out, v_norm3_g, v_ffn2_w_in, v_ffn2_w_out, v_final_norm_g):
    w = dict(c_ctx=c_ctx, ada_w=ada_w, ada_b=ada_b, norm1_g=norm1_g, ffn1_w_in=ffn1_w_in, ffn1_w_out=ffn1_w_out,
             norm2_g=norm2_g, mix_w_in=mix_w_in, ret_decay_fwd=ret_decay_fwd, ret_decay_bwd=ret_decay_bwd,
             mla_q_norm_g=mla_q_norm_g, mla_w_uq=mla_w_uq, mla_kv_norm_g=mla_kv_norm_g, mla_w_ukv=mla_w_ukv,
             mix_w_out=mix_w_out, norm3_g=norm3_g, ffn2_w_in=ffn2_w_in, ffn2_w_out=ffn2_w_out,
             final_norm_g=final_norm_g)
    mom = dict(c_ctx=m_c_ctx, ada_w=m_ada_w, ada_b=m_ada_b, norm1_g=m_norm1_g, ffn1_w_in=m_ffn1_w_in,
               ffn1_w_out=m_ffn1_w_out, norm2_g=m_norm2_g, mix_w_in=m_mix_w_in, ret_decay_fwd=m_ret_decay_fwd,
               ret_decay_bwd=m_ret_decay_bwd, mla_q_norm_g=m_mla_q_norm_g, mla_w_uq=m_mla_w_uq,
               mla_kv_norm_g=m_mla_kv_norm_g, mla_w_ukv=m_mla_w_ukv, mix_w_out=m_mix_w_out, norm3_g=m_norm3_g,
               ffn2_w_in=m_ffn2_w_in, ffn2_w_out=m_ffn2_w_out, final_norm_g=m_final_norm_g)
    var = dict(c_ctx=v_c_ctx, ada_w=v_ada_w, ada_b=v_ada_b, norm1_g=v_norm1_g, ffn1_w_in=v_ffn1_w_in,
               ffn1_w_out=v_ffn1_w_out, norm2_g=v_norm2_g, mix_w_in=v_mix_w_in, ret_decay_fwd=v_ret_decay_fwd,
               ret_decay_bwd=v_ret_decay_bwd, mla_q_norm_g=v_mla_q_norm_g, mla_w_uq=v_mla_w_uq,
               mla_kv_norm_g=v_mla_kv_norm_g, mla_w_ukv=v_mla_w_ukv, mix_w_out=v_mix_w_out, norm3_g=v_norm3_g,
               ffn2_w_in=v_ffn2_w_in, ffn2_w_out=v_ffn2_w_out, final_norm_g=v_final_norm_g)
    me = 4 * lax.axis_index("x") + 2 * lax.axis_index("y") + lax.axis_index("c")

    shard = {k: w[k][0].astype(BF16) for k in BIG}
    first = _gather_two_level([shard["ffn1_w_in"], jax.nn.silu(c)], "weights_gather")
    silu_c_all = first[-1][:, 0, :]

    def start_gather(names, after, name):
        return _exchange_start([shard[k] for k in names], [_own_slot(shard[k], me) for k in names], after, True, name)

    wout_start = start_gather(FIRST[1:], first[0], "ffn1_wout_start")
    mid_start = start_gather(MID, wout_start[4], "mixer_weights_start")
    last_start = start_gather(LAST, mid_start[4], "ffn2_weights_start")

    def wait_w_out(after):
        return _exchange_wait(*wout_start[:4], after, True, "ffn1_wout_wait")[0]

    pa = dict(ada_w=ada_w[0], c_ctx=c_ctx, ada_b=ada_b)
    (x1, c1, m_lat, m_ctx), res_a = _stage_a_fwd(pa, x[0], ctx[0], norm1_g, last_start[4], wait_w_out, silu_c_all, me)

    mid = _exchange_wait(mid_start[0], mid_start[1], mid_start[2], mid_start[3], x1, True, "mixer_weights_wait")
    pb = dict(zip(MID, mid))
    for k in ("norm2_g", "mla_q_norm_g", "mla_kv_norm_g", "ret_decay_fwd", "ret_decay_bwd"):
        pb[k] = w[k]
    x2, vjp_b = jax.vjp(_stage_b, pb, x1, c1, m_lat, m_ctx)

    last = _exchange_wait(last_start[0], last_start[1], last_start[2], last_start[3], x2, True, "ffn2_weights_wait")
    pc = dict(zip(LAST, last), norm3_g=norm3_g, final_norm_g=final_norm_g[None, :])
    loss_local, vjp_c = jax.vjp(lambda q, t, m: _stage_c(q, t, m, loss_target[0]), pc, x2, m_lat)

    gc, dx2, dm_c = vjp_c(jnp.ones((), F32))
    last_scat = _exchange_start([gc[k] for k in LAST],
                                [_own_slot(lax.dynamic_index_in_dim(gc[k], me, 0, False), me) for k in LAST],
                                dx2, False, "ffn2_grads_start")
    gb, dx1, dc1, dm_b, dmc_b = vjp_b(last_scat[4])
    mid_scat = _exchange_start([gb[k] for k in MID],
                               [_own_slot(lax.dynamic_index_in_dim(gb[k], me, 0, False), me) for k in MID],
                               dx1, False, "mixer_grads_start")
    first_scat = {}

    def start_grads(name, dw, after):
        first_scat[name] = _exchange_start(
            [dw], [_own_slot(lax.dynamic_index_in_dim(dw, me, 0, False), me)], after, False, name + "_grads_start")
        return first_scat[name][4]

    g_ada, dx, dng1 = _stage_a_bwd(res_a, mid_scat[4], dc1, dm_b + dm_c, dmc_b, start_grads)
    grads = {**g_ada, **gb, **gc, "x": dx, "norm1_g": dng1}
    grads["final_norm_g"] = grads["final_norm_g"][0]

    exchanged = {k: _exchange_wait(*first_scat[k][:4], dx, False, k + "_grads_wait")[0] for k in FIRST}
    exchanged.update(zip(LAST, _exchange_wait(*last_scat[:4], dx, False, "ffn2_grads_wait")))
    exchanged.update(zip(MID, _exchange_wait(*mid_scat[:4], dx, False, "mixer_grads_wait")))
    zero1 = [jnp.zeros((1,), F32)]
    small_like = zero1 + [w[k] for k in SMALL]
    small_all = _exchange([_pack([loss_local.reshape(1)] + [grads[k] for k in SMALL])], True, "small_grads_gather")[0]
    loss = jnp.sum(small_all[:, 0, 0])

    out_g, out_d, out_m, out_v = {}, {}, {}, {}

    def update(name, gstack, shape2d):
        res = _adamw(gstack, w[name].reshape(shape2d), mom[name].reshape(shape2d), var[name].reshape(shape2d),
                     "adamw_" + name)
        out_g[name], out_d[name], out_m[name], out_v[name] = [t.reshape(w[name].shape) for t in res]

    for k in BIG:
        update(k, exchanged[k], exchanged[k].shape[1:])
    update("ada_w", grads["ada_w"][None], ada_w.shape[1:])
    res = _adamw(small_all, _pack(small_like), _pack(zero1 + [mom[k] for k in SMALL]),
                 _pack(zero1 + [var[k] for k in SMALL]), "adamw_small")
    for dst, flat in zip((out_g, out_d, out_m, out_v), res):
        for k, t in zip(SMALL, _unpack(flat, small_like)[1:]):
            dst[k] = t

    return (loss, grads["x"][None], *[out_g[k] for k in WEIGHTS], *[out_d[k] for k in WEIGHTS],
            *[out_m[k] for k in WEIGHTS], *[out_v[k] for k in WEIGHTS])
```

```python
import functools

import jax
import jax.numpy as jnp
from jax import lax
from jax.experimental import pallas as pl
from jax.experimental.pallas import tpu as pltpu

F32 = jnp.float32
BF16 = jnp.bfloat16

N_DEV = 8
MESH_AXES = ("x", "y", "c")

GRID_W = 64
N_MOD = 9
RET_HEADS = 8
RET_DK = 64
RET_DV = 128
RET_CHUNK = 256
RET_ROPE_BASE = 10000.0
MLA_HEADS = 8
MLA_Q_RANK = 512
MLA_KV_RANK = 256
MLA_NOPE = 128
MLA_ROPE = 64
MLA_V = 128
AXIAL_BASE = 10000.0
RMS_EPS = 1e-6
GN_EPS = 1e-5
SPLITS = (RET_HEADS * RET_DK, RET_HEADS * RET_DK, RET_HEADS * RET_DV, RET_HEADS * RET_DV,
          MLA_Q_RANK, MLA_KV_RANK, MLA_ROPE)
MIX_IN = sum(SPLITS)
MIX_IN_PAD = 4096

ADAM_LR = 0.001
ADAM_B1 = 0.9
ADAM_B2 = 0.999
ADAM_EPS = 1e-08
ADAM_WD = 0.01
ADAM_STEP = 10

LANE = 128
RET_DKP = LANE
VMEM_LIMIT_BYTES = 56 * 1024 * 1024

NN = ((1,), (0,))
NT = ((1,), (1,))
TN = ((0,), (0,))


def _pick(dim, target, align=LANE):
    t = min(dim, target)
    t -= t % align
    while t >= align:
        if dim % t == 0:
            return t
        t -= align
    return dim


def _params():
    return pltpu.CompilerParams(vmem_limit_bytes=VMEM_LIMIT_BYTES)


def _dot(a, b, dims):
    return lax.dot_general(a.astype(BF16), b.astype(BF16), (dims, ((), ())), preferred_element_type=F32)


def _mm_call(name, grid, ins, pairs, outs, acc_shapes, epilogue):
    n_in, n_out = len(ins), len(outs)
    k_axis = len(grid) - 1
    k_steps = grid[k_axis]

    def body(*refs):
        in_refs = refs[:n_in]
        out_refs = refs[n_in:n_in + n_out]
        accs = refs[n_in + n_out:]
        k = pl.program_id(k_axis)

        @pl.when(k == 0)
        def _():
            for acc in accs:
                acc[...] = jnp.zeros_like(acc)

        for ai, bi, dims, ci in pairs:
            accs[ci][...] += _dot(in_refs[ai][...], in_refs[bi][...], dims)

        @pl.when(k == k_steps - 1)
        def _():
            epilogue([acc[...] for acc in accs], in_refs, out_refs)

    res = pl.pallas_call(
        body, name=name, grid=grid,
        in_specs=[s for _, s in ins], out_specs=[s for _, s in outs],
        out_shape=[s for s, _ in outs],
        scratch_shapes=[pltpu.VMEM(s, F32) for s in acc_shapes],
        compiler_params=_params(),
    )(*[a for a, _ in ins])
    return res


def _matmul(a, b, mode, out_dtype, name, tm=1024, tn=1024, tk=512):
    if mode == "nn":
        (m, kd), n = a.shape, b.shape[1]
    elif mode == "nt":
        (m, kd), n = a.shape, b.shape[0]
    else:
        (kd, m), n = a.shape, b.shape[1]
    tm, tn = _pick(m, tm, 16), _pick(n, tn)
    tk = _pick(kd, tk) if mode != "tn" else _pick(kd, tk, 16)
    if mode == "nn":
        a_spec = pl.BlockSpec((tm, tk), lambda i, j, k: (i, k))
        b_spec = pl.BlockSpec((tk, tn), lambda i, j, k: (k, j))
        dims = NN
    elif mode == "nt":
        a_spec = pl.BlockSpec((tm, tk), lambda i, j, k: (i, k))
        b_spec = pl.BlockSpec((tn, tk), lambda i, j, k: (j, k))
        dims = NT
    else:
        a_spec = pl.BlockSpec((tk, tm), lambda i, j, k: (k, i))
        b_spec = pl.BlockSpec((tk, tn), lambda i, j, k: (k, j))
        dims = TN

    def epilogue(accs, in_refs, out_refs):
        out_refs[0][...] = accs[0].astype(out_dtype)

    return _mm_call(
        name, (m // tm, n // tn, kd // tk), [(a, a_spec), (b, b_spec)], [(0, 1, dims, 0)],
        [(jax.ShapeDtypeStruct((m, n), out_dtype), pl.BlockSpec((tm, tn), lambda i, j, k: (i, j)))],
        [(tm, tn)], epilogue)[0]


def _norm_mod_tile(x, ng, sc, sh):
    r = lax.rsqrt(jnp.mean(x * x, axis=-1, keepdims=True) + RMS_EPS)
    return (x * r * ng) * (1.0 + sc) + sh


def _row_spec(tm, d):
    return pl.BlockSpec((tm, d), lambda i: (i, 0))


def _vec_spec(d):
    return pl.BlockSpec((1, d), lambda i: (0, 0))


def _norm_mod_fwd(x, ng, sc, sh, name):
    t, d = x.shape
    tm = _pick(t, 512, 16)

    def body(x_ref, ng_ref, sc_ref, sh_ref, h_ref):
        h_ref[...] = _norm_mod_tile(x_ref[...], ng_ref[...], sc_ref[...], sh_ref[...]).astype(BF16)

    return pl.pallas_call(
        body, name=name, grid=(t // tm,),
        in_specs=[_row_spec(tm, d), _vec_spec(d), _vec_spec(d), _vec_spec(d)],
        out_specs=_row_spec(tm, d), out_shape=jax.ShapeDtypeStruct((t, d), BF16),
        compiler_params=_params(),
    )(x, ng, sc, sh)


def _norm_mod_bwd(x, ng, sc, sh, dh, dres, name):
    t, d = x.shape
    tm = _pick(t, 256, 16)
    has_res = dres is not None

    def body(*refs):
        if has_res:
            x_ref, ng_ref, sc_ref, sh_ref, dh_ref, dres_ref, dx_ref, dng_ref, dsc_ref, dsh_ref = refs
        else:
            x_ref, ng_ref, sc_ref, sh_ref, dh_ref, dx_ref, dng_ref, dsc_ref, dsh_ref = refs
        _, vjp = jax.vjp(_norm_mod_tile, x_ref[...], ng_ref[...], sc_ref[...], sh_ref[...])
        dx, dng, dsc, dsh = vjp(dh_ref[...].astype(F32))
        if has_res:
            dx = dx + dres_ref[...]
        dx_ref[...] = dx

        @pl.when(pl.program_id(0) == 0)
        def _():
            dng_ref[...] = jnp.zeros_like(dng_ref)
            dsc_ref[...] = jnp.zeros_like(dsc_ref)
            dsh_ref[...] = jnp.zeros_like(dsh_ref)

        dng_ref[...] += dng
        dsc_ref[...] += dsc
        dsh_ref[...] += dsh

    ins = [x, ng, sc, sh, dh] + ([dres] if has_res else [])
    in_specs = [_row_spec(tm, d), _vec_spec(d), _vec_spec(d), _vec_spec(d), _row_spec(tm, d)]
    in_specs += [_row_spec(tm, d)] if has_res else []
    vec = jax.ShapeDtypeStruct((1, d), F32)
    return pl.pallas_call(
        body, name=name, grid=(t // tm,), in_specs=in_specs,
        out_specs=[_row_spec(tm, d), _vec_spec(d), _vec_spec(d), _vec_spec(d)],
        out_shape=[jax.ShapeDtypeStruct((t, d), F32), vec, vec, vec],
        compiler_params=_params(),
    )(*ins)


def _res_mm_fwd(a, w, x, gate, coef, name):
    t, kd = a.shape
    d = w.shape[1]
    tm, tn, tk = _pick(t, 1024, 16), _pick(d, 1024), _pick(kd, 2816)

    def epilogue(accs, in_refs, out_refs):
        f = accs[0]
        out_refs[0][...] = in_refs[2][...] + (coef * in_refs[3][...]) * f
        out_refs[1][...] = f.astype(BF16)

    tile = pl.BlockSpec((tm, tn), lambda i, j, k: (i, j))
    return _mm_call(
        name, (t // tm, d // tn, kd // tk),
        [(a, pl.BlockSpec((tm, tk), lambda i, j, k: (i, k))), (w, pl.BlockSpec((tk, tn), lambda i, j, k: (k, j))),
         (x, tile), (gate, pl.BlockSpec((1, tn), lambda i, j, k: (0, j)))],
        [(0, 1, NN, 0)],
        [(jax.ShapeDtypeStruct((t, d), F32), tile), (jax.ShapeDtypeStruct((t, d), BF16), tile)],
        [(tm, tn)], epilogue)


def _gate_bwd(dxo, f, gate, coef, name):
    t, d = dxo.shape
    tm = _pick(t, 512, 16)

    def body(dxo_ref, f_ref, gate_ref, df_ref, dgate_ref):
        dxo_t = dxo_ref[...]
        df_ref[...] = ((coef * gate_ref[...]) * dxo_t).astype(BF16)

        @pl.when(pl.program_id(0) == 0)
        def _():
            dgate_ref[...] = jnp.zeros_like(dgate_ref)

        dgate_ref[...] += coef * jnp.sum(dxo_t * f_ref[...].astype(F32), axis=0, keepdims=True)

    return pl.pallas_call(
        body, name=name, grid=(t // tm,),
        in_specs=[_row_spec(tm, d), _row_spec(tm, d), _vec_spec(d)],
        out_specs=[_row_spec(tm, d), _vec_spec(d)],
        out_shape=[jax.ShapeDtypeStruct((t, d), BF16), jax.ShapeDtypeStruct((1, d), F32)],
        compiler_params=_params(),
    )(dxo, f, gate)


def _ffn_in_fwd(h, w_in, name):
    t, d = h.shape
    n = w_in.shape[2]
    half = N_DEV // 2
    f = half * n
    tm = _pick(t, 512, 16)

    def epilogue(accs, in_refs, out_refs):
        g, u = accs
        s = jax.nn.sigmoid(g)
        silu = g * s
        out_refs[0][...] = (silu * u).astype(BF16)
        out_refs[1][0] = (u * (s * (1.0 + g * (1.0 - s)))).astype(BF16)
        out_refs[1][1] = silu.astype(BF16)

    return _mm_call(
        name, (half, t // tm, 1),
        [(h, pl.BlockSpec((tm, d), lambda j, i, k: (i, 0))),
         (w_in, pl.BlockSpec((None, d, n), lambda j, i, k: (j, 0, 0))),
         (w_in, pl.BlockSpec((None, d, n), lambda j, i, k: (j + half, 0, 0)))],
        [(0, 1, NN, 0), (0, 2, NN, 1)],
        [(jax.ShapeDtypeStruct((t, f), BF16), pl.BlockSpec((tm, n), lambda j, i, k: (i, j))),
         (jax.ShapeDtypeStruct((2, t, f), BF16), pl.BlockSpec((2, tm, n), lambda j, i, k: (0, i, j)))],
        [(tm, n), (tm, n)], epilogue)


def _ffn_da_bwd(df, w_out2d, gu, name):
    t, d = df.shape
    f = w_out2d.shape[0]
    half = N_DEV // 2
    n = f // half
    tm = _pick(t, 512, 16)
    step = 4 * LANE
    chunks = [(c, min(c + step, n)) for c in range(0, n, step)]

    def body(df_ref, w_ref, gu_ref, o_ref):
        df_t = df_ref[...]
        for c0, c1 in chunks:
            da = _dot(df_t, w_ref[c0:c1, :], NT)
            o_ref[0, :, c0:c1] = (da * gu_ref[0, :, c0:c1].astype(F32)).astype(BF16)
            o_ref[1, :, c0:c1] = (da * gu_ref[1, :, c0:c1].astype(F32)).astype(BF16)

    gu_spec = pl.BlockSpec((2, tm, n), lambda j, i: (0, i, j))
    return pl.pallas_call(
        body, name=name, grid=(half, t // tm),
        in_specs=[pl.BlockSpec((tm, d), lambda j, i: (i, 0)), pl.BlockSpec((n, d), lambda j, i: (j, 0)), gu_spec],
        out_specs=gu_spec, out_shape=jax.ShapeDtypeStruct((2, t, f), BF16), compiler_params=_params(),
    )(df, w_out2d, gu)


def _ffn_dh_bwd(dgu, w_in, name):
    _, t, f = dgu.shape
    d, n = w_in.shape[1], w_in.shape[2]
    half = N_DEV // 2
    tm = _pick(t, 512, 16)

    def epilogue(accs, in_refs, out_refs):
        out_refs[0][...] = accs[0]

    return _mm_call(
        name, (t // tm, 1, half),
        [(dgu, pl.BlockSpec((None, tm, n), lambda i, j, k: (0, i, k))),
         (dgu, pl.BlockSpec((None, tm, n), lambda i, j, k: (1, i, k))),
         (w_in, pl.BlockSpec((None, d, n), lambda i, j, k: (k, 0, 0))),
         (w_in, pl.BlockSpec((None, d, n), lambda i, j, k: (k + half, 0, 0)))],
        [(0, 2, NT, 0), (1, 3, NT, 0)],
        [(jax.ShapeDtypeStruct((t, d), F32), pl.BlockSpec((tm, d), lambda i, j, k: (i, 0)))],
        [(tm, d)], epilogue)[0]


def _ffn_dwin_bwd(h, dgu, name):
    t, d = h.shape
    f = dgu.shape[2]
    half = N_DEV // 2
    n = f // half
    tk = _pick(t, 1024, 16)

    def epilogue(accs, in_refs, out_refs):
        out_refs[0][...] = accs[0].astype(BF16)

    return _mm_call(
        name, (N_DEV, 1, t // tk),
        [(h, pl.BlockSpec((tk, d), lambda j, i, k: (k, 0))),
         (dgu, pl.BlockSpec((None, tk, n), lambda j, i, k: (j // half, k, j % half)))],
        [(0, 1, TN, 0)],
        [(jax.ShapeDtypeStruct((N_DEV, d, n), BF16), pl.BlockSpec((None, d, n), lambda j, i, k: (j, 0, 0)))],
        [(d, n)], epilogue)[0]


def _ffn_parts(tag):
    def w2d(w_out):
        return w_out.reshape(w_out.shape[0] * w_out.shape[1], w_out.shape[2])

    def fwd_in(x, ng, sh, sc, w_in):
        h = _norm_mod_fwd(x, ng, sc, sh, tag + "_norm")
        a, gu = _ffn_in_fwd(h, w_in, tag + "_in")
        return h, a, gu

    def fwd_out(a, w_out, x, gate):
        return _res_mm_fwd(a, w2d(w_out), x, gate, 0.5, tag + "_out")

    def bwd_out(dxo, f1, gate, w_out, gu, a):
        df, dgate = _gate_bwd(dxo, f1, gate, 0.5, tag + "_dgate")
        dgu = _ffn_da_bwd(df, w2d(w_out), gu, tag + "_da")
        f = w_out.shape[0] * w_out.shape[1]
        dw_out = _matmul(a, df, "tn", BF16, tag + "_dwout", tm=_pick(f, 1408, 16), tn=2048, tk=1024)
        return dgate, dgu, dw_out.reshape(w_out.shape)

    def bwd_in(x, ng, sh, sc, w_in, h, dgu, dxo):
        dh = _ffn_dh_bwd(dgu, w_in, tag + "_dh")
        dw_in = _ffn_dwin_bwd(h, dgu, tag + "_dwin")
        dx, dng, dsc, dsh = _norm_mod_bwd(x, ng, sc, sh, dh, dxo, tag + "_dnorm")
        return dx, dng, dsh, dsc, dw_in

    return fwd_in, fwd_out, bwd_out, bwd_in


def _make_ffn_block(tag):
    fwd_in, fwd_out, bwd_out, bwd_in = _ffn_parts(tag)

    @jax.custom_vjp
    def ffn_block(x, ng, sh, sc, gate, w_in, w_out):
        return fwd(x, ng, sh, sc, gate, w_in, w_out)[0]

    def fwd(x, ng, sh, sc, gate, w_in, w_out):
        h, a, gu = fwd_in(x, ng, sh, sc, w_in)
        xo, f1 = fwd_out(a, w_out, x, gate)
        return xo, (x, ng, sh, sc, gate, w_in, w_out, h, a, gu, f1)

    def bwd(res, dxo):
        x, ng, sh, sc, gate, w_in, w_out, h, a, gu, f1 = res
        dgate, dgu, dw_out = bwd_out(dxo, f1, gate, w_out, gu, a)
        dx, dng, dsh, dsc, dw_in = bwd_in(x, ng, sh, sc, w_in, h, dgu, dxo)
        return dx, dng, dsh, dsc, dgate, dw_in, dw_out

    ffn_block.defvjp(fwd, bwd)
    return ffn_block


def _make_norm_proj(tag):
    @jax.custom_vjp
    def norm_proj(x, ng, sh, sc, w):
        return fwd(x, ng, sh, sc, w)[0]

    def fwd(x, ng, sh, sc, w):
        h = _norm_mod_fwd(x, ng, sc, sh, tag + "_norm")
        p = _matmul(h, w, "nn", F32, tag + "_mm", tm=1024, tn=1024, tk=w.shape[0])
        return p, (x, ng, sh, sc, w, h)

    def bwd(res, dp):
        x, ng, sh, sc, w, h = res
        dh = _matmul(dp, w, "nt", F32, tag + "_dh", tm=512, tn=w.shape[0], tk=2048)
        dw = _matmul(h, dp, "tn", BF16, tag + "_dw", tm=w.shape[0], tn=1024, tk=1024)
        dx, dng, dsc, dsh = _norm_mod_bwd(x, ng, sc, sh, dh, None, tag + "_dnorm")
        return dx, dng, dsh, dsc, dw

    norm_proj.defvjp(fwd, bwd)
    return norm_proj


def _make_norm_proj_carry(tag):
    @jax.custom_vjp
    def norm_proj(x, ng, sh, sc, w):
        return fwd(x, ng, sh, sc, w)[0]

    def fwd(x, ng, sh, sc, w):
        h = _norm_mod_fwd(x, ng, sc, sh, tag + "_norm")
        p = _matmul(h, w, "nn", F32, tag + "_mm", tm=1024, tn=1024, tk=w.shape[0])
        return (p, x), (x, ng, sh, sc, w, h)

    def bwd(res, cts):
        x, ng, sh, sc, w, h = res
        dp, dx_carry = cts
        dh = _matmul(dp, w, "nt", F32, tag + "_dh", tm=512, tn=w.shape[0], tk=2048)
        dw = _matmul(h, dp, "tn", BF16, tag + "_dw", tm=w.shape[0], tn=1024, tk=1024)
        dx, dng, dsc, dsh = _norm_mod_bwd(x, ng, sc, sh, dh, dx_carry, tag + "_dnorm")
        return dx, dng, dsh, dsc, dw

    norm_proj.defvjp(fwd, bwd)
    return norm_proj


def _make_split(tag, widths, total):
    offs = [sum(widths[:i]) for i in range(len(widths))]

    def concat_call(pieces):
        t = pieces[0].shape[0]
        tm = _pick(t, 256, 16)

        def body(*refs):
            o_ref = refs[-1]
            for ref, off, wd in zip(refs[:-1], offs, widths):
                o_ref[:, off:off + wd] = ref[...]
            end = offs[-1] + widths[-1]
            if end < total:
                o_ref[:, end:] = jnp.zeros((tm, total - end), F32)

        return pl.pallas_call(
            body, name=tag + "_concat", grid=(t // tm,),
            in_specs=[_row_spec(tm, wd) for wd in widths], out_specs=_row_spec(tm, total),
            out_shape=jax.ShapeDtypeStruct((t, total), F32), compiler_params=_params(),
        )(*pieces)

    @jax.custom_vjp
    def split(p):
        return tuple(p[:, off:off + wd] for off, wd in zip(offs, widths))

    def fwd(p):
        return split(p), None

    def bwd(_, cts):
        return (concat_call(list(cts)),)

    split.defvjp(fwd, bwd)
    return split


def _make_res_proj(tag):
    @jax.custom_vjp
    def res_proj(a, w, x, gate):
        return fwd(a, w, x, gate)[0]

    def fwd(a, w, x, gate):
        xo, f = _res_mm_fwd(a, w, x, gate, 1.0, tag + "_mm")
        return xo, (a, w, gate, f)

    def bwd(res, dxo):
        a, w, gate, f = res
        df, dgate = _gate_bwd(dxo, f, gate, 1.0, tag + "_dgate")
        da = _matmul(df, w, "nt", BF16, tag + "_da", tm=1024, tn=1024, tk=2048)
        dw = _matmul(a, df, "tn", BF16, tag + "_dw", tm=1024, tn=2048, tk=1024)
        return da, dw, dxo, dgate

    res_proj.defvjp(fwd, bwd)
    return res_proj


def _make_small_mm(tag):
    @jax.custom_vjp
    def small_mm(a, w):
        return _matmul(a, w, "nn", F32, tag + "_mm", tm=a.shape[0], tn=768, tk=w.shape[0])

    def fwd(a, w):
        return small_mm(a, w), (a, w)

    def bwd(res, dr):
        a, w = res
        da = _matmul(dr, w, "nt", F32, tag + "_da", tm=a.shape[0], tn=w.shape[0], tk=768)
        dw = _matmul(a, dr, "tn", F32, tag + "_dw", tm=1024, tn=768, tk=a.shape[0])
        return da, dw

    small_mm.defvjp(fwd, bwd)
    return small_mm


def _ret_chunk_terms(lg, c, reverse):
    row = lax.broadcasted_iota(jnp.int32, (c, c), 0).astype(F32)
    col = lax.broadcasted_iota(jnp.int32, (c, c), 1).astype(F32)
    pos = lax.broadcasted_iota(jnp.int32, (c, 1), 0).astype(F32)
    if reverse:
        diff = col - row
        mask = diff > 0.0
        e_exp = float(c) - pos
        f_exp = pos
    else:
        diff = row - col
        mask = diff >= 0.0
        e_exp = pos + 1.0
        f_exp = float(c - 1) - pos
    diffm = jnp.where(mask, diff, 0.0)
    dm = jnp.where(mask, jnp.exp(lg * diffm), 0.0)
    return diffm, dm, e_exp, jnp.exp(lg * e_exp), f_exp, jnp.exp(lg * f_exp)


def _ret_mask_t(lg, c, reverse):
    row = lax.broadcasted_iota(jnp.int32, (c, c), 0).astype(F32)
    col = lax.broadcasted_iota(jnp.int32, (c, c), 1).astype(F32)
    diff = row - col if reverse else col - row
    mask = diff > 0.0 if reverse else diff >= 0.0
    return jnp.where(mask, jnp.exp(lg * jnp.where(mask, diff, 0.0)), 0.0)


def _lane0(val):
    lane = lax.broadcasted_iota(jnp.int32, (1, LANE), 1)
    return jnp.where(lane == 0, val, 0.0)


RET_HEAD_BLOCK = 4


def _make_ret_dir(tag, reverse):
    hb = RET_HEAD_BLOCK

    def heads_spec(nc, width, flip):
        if flip:
            return pl.BlockSpec((hb, RET_CHUNK, width), lambda h, t: (h, nc - 1 - t, 0))
        return pl.BlockSpec((hb, RET_CHUNK, width), lambda h, t: (h, t, 0))

    def state_spec(nc, flip):
        if flip:
            return pl.BlockSpec((hb, None, RET_DKP, RET_DV), lambda h, t: (h, nc - 1 - t, 0, 0))
        return pl.BlockSpec((hb, None, RET_DKP, RET_DV), lambda h, t: (h, t, 0, 0))

    lg_spec = pl.BlockSpec((hb, 1, LANE), lambda h, t: (h, 0, 0))
    s0_spec = pl.BlockSpec((hb, RET_DKP, RET_DV), lambda h, t: (h, 0, 0))

    def fwd_call(q, k, v, lgb, s0):
        hh, ll, _ = q.shape
        c = RET_CHUNK
        nc = ll // c

        def body(q_ref, k_ref, v_ref, lg_ref, s0_ref, y_ref, sall_ref, s_scr):
            @pl.when(pl.program_id(1) == 0)
            def _():
                s_scr[...] = s0_ref[...]

            for b in range(hb):
                lg = lg_ref[b][:, :1]
                _, dm, _, xi, _, zeta = _ret_chunk_terms(lg, c, reverse)
                q_t, k_t, v_t = q_ref[b], k_ref[b], v_ref[b]
                s = s_scr[b]
                p = _dot(q_t, k_t, NT) * dm
                y_ref[b] = _dot(p, v_t, NN) + _dot(q_t * xi, s, NN)
                sall_ref[b] = s
                s_scr[b] = jnp.exp(lg * float(c)) * s + _dot(k_t * zeta, v_t, TN)

        return pl.pallas_call(
            body, name=tag + "_fwd", grid=(hh // hb, nc),
            in_specs=[heads_spec(nc, RET_DKP,reverse), heads_spec(nc, RET_DKP,reverse),
                      heads_spec(nc, RET_DV, reverse), lg_spec, s0_spec],
            out_specs=[heads_spec(nc, RET_DV, reverse), state_spec(nc, reverse)],
            out_shape=[jax.ShapeDtypeStruct((hh, ll, RET_DV), F32),
                       jax.ShapeDtypeStruct((hh, nc, RET_DKP, RET_DV), F32)],
            scratch_shapes=[pltpu.VMEM((hb, RET_DKP, RET_DV), F32)],
            compiler_params=_params(),
        )(q, k, v, lgb, s0)

    def bwd_call(q, k, v, lgb, sall, dy):
        hh, ll, _ = q.shape
        c = RET_CHUNK
        nc = ll // c
        flip = not reverse

        def body(q_ref, k_ref, v_ref, lg_ref, sall_ref, dy_ref, dq_ref, dk_ref, dv_ref, dlg_ref, ds0_ref, ds_scr):
            @pl.when(pl.program_id(1) == 0)
            def _():
                ds_scr[...] = jnp.zeros_like(ds_scr)
                dlg_ref[...] = jnp.zeros_like(dlg_ref)

            def total(m):
                return jnp.sum(jnp.sum(m, axis=1, keepdims=True), axis=0, keepdims=True)

            for b in range(hb):
                lg = lg_ref[b][:, :1]
                diffm, dm, e_exp, xi, f_exp, zeta = _ret_chunk_terms(lg, c, reverse)
                q_t, k_t, v_t, dy_t = q_ref[b], k_ref[b], v_ref[b], dy_ref[b]
                s = sall_ref[b]
                dsn = ds_scr[b]
                a = _dot(q_t, k_t, NT)
                da = _dot(dy_t, v_t, NT) * dm
                dm_t = _ret_mask_t(lg, c, reverse)
                a_t = _dot(k_t, q_t, NT)
                da_t = _dot(v_t, dy_t, NT) * dm_t
                g = _dot(dy_t, s, NT)
                hm = _dot(v_t, dsn, NT)
                dq_ref[b] = _dot(da, k_t, NN) + xi * g
                dk_ref[b] = _dot(da_t, q_t, NN) + zeta * hm
                dv_ref[b] = _dot(a_t * dm_t, dy_t, NN) + _dot(k_t * zeta, dsn, NN)
                gc = jnp.exp(lg * float(c))
                ds_scr[b] = gc * dsn + _dot(q_t * xi, dy_t, TN)
                dl = (total(da * a * diffm) + total(e_exp * xi * q_t * g)
                      + float(c) * gc * total(s * dsn) + total(f_exp * zeta * k_t * hm))
                dlg_ref[b] += _lane0(dl)

            @pl.when(pl.program_id(1) == nc - 1)
            def _():
                ds0_ref[...] = ds_scr[...]

        return pl.pallas_call(
            body, name=tag + "_bwd", grid=(hh // hb, nc),
            in_specs=[heads_spec(nc, RET_DKP,flip), heads_spec(nc, RET_DKP,flip), heads_spec(nc, RET_DV, flip),
                      lg_spec, state_spec(nc, flip), heads_spec(nc, RET_DV, flip)],
            out_specs=[heads_spec(nc, RET_DKP,flip), heads_spec(nc, RET_DKP,flip), heads_spec(nc, RET_DV, flip),
                       lg_spec, s0_spec],
            out_shape=[jax.ShapeDtypeStruct((hh, ll, RET_DKP), F32), jax.ShapeDtypeStruct((hh, ll, RET_DKP), F32),
                       jax.ShapeDtypeStruct((hh, ll, RET_DV), F32), jax.ShapeDtypeStruct((hh, 1, LANE), F32),
                       jax.ShapeDtypeStruct((hh, RET_DKP, RET_DV), F32)],
            scratch_shapes=[pltpu.VMEM((hb, RET_DKP, RET_DV), F32)],
            compiler_params=_params(),
        )(q, k, v, lgb, sall, dy)

    @jax.custom_vjp
    def ret_dir(q, k, v, lgb, s0):
        return fwd_call(q, k, v, lgb, s0)[0]

    def fwd(q, k, v, lgb, s0):
        y, sall = fwd_call(q, k, v, lgb, s0)
        return y, (q, k, v, lgb, sall)

    def bwd(res, dy):
        q, k, v, lgb, sall = res
        return tuple(bwd_call(q, k, v, lgb, sall, dy))

    ret_dir.defvjp(fwd, bwd)
    return ret_dir


def _make_ctx_state(tag, reverse):
    hb = RET_HEAD_BLOCK
    c = RET_CHUNK
    k_spec = pl.BlockSpec((hb, c, RET_DKP), lambda h: (h, 0, 0))
    v_spec = pl.BlockSpec((hb, c, RET_DV), lambda h: (h, 0, 0))
    lg_spec = pl.BlockSpec((hb, 1, LANE), lambda h: (h, 0, 0))
    s_spec = pl.BlockSpec((hb, RET_DKP, RET_DV), lambda h: (h, 0, 0))

    def fwd_call(k, v, lgb):
        hh = k.shape[0]

        def body(k_ref, v_ref, lg_ref, s_ref):
            for b in range(hb):
                _, _, _, _, _, zeta = _ret_chunk_terms(lg_ref[b][:, :1], c, reverse)
                s_ref[b] = _dot(k_ref[b] * zeta, v_ref[b], TN)

        return pl.pallas_call(
            body, name=tag + "_fwd", grid=(hh // hb,), in_specs=[k_spec, v_spec, lg_spec], out_specs=s_spec,
            out_shape=jax.ShapeDtypeStruct((hh, RET_DKP, RET_DV), F32), compiler_params=_params(),
        )(k, v, lgb)

    def bwd_call(k, v, lgb, ds):
        hh = k.shape[0]

        def body(k_ref, v_ref, lg_ref, ds_ref, dk_ref, dv_ref, dlg_ref):
            for b in range(hb):
                _, _, _, _, f_exp, zeta = _ret_chunk_terms(lg_ref[b][:, :1], c, reverse)
                k_t, v_t, ds = k_ref[b], v_ref[b], ds_ref[b]
                hm = _dot(v_t, ds, NT)
                dk_ref[b] = zeta * hm
                dv_ref[b] = _dot(k_t * zeta, ds, NN)
                tot = jnp.sum(jnp.sum(f_exp * zeta * k_t * hm, axis=1, keepdims=True), axis=0, keepdims=True)
                dlg_ref[b] = _lane0(tot)

        return pl.pallas_call(
            body, name=tag + "_bwd", grid=(hh // hb,), in_specs=[k_spec, v_spec, lg_spec, s_spec],
            out_specs=[k_spec, v_spec, lg_spec],
            out_shape=[jax.ShapeDtypeStruct(k.shape, F32), jax.ShapeDtypeStruct(v.shape, F32),
                       jax.ShapeDtypeStruct((hh, 1, LANE), F32)],
            compiler_params=_params(),
        )(k, v, lgb, ds)

    @jax.custom_vjp
    def ctx_state(k, v, lgb):
        return fwd_call(k, v, lgb)

    def fwd(k, v, lgb):
        return fwd_call(k, v, lgb), (k, v, lgb)

    def bwd(res, ds):
        return tuple(bwd_call(*res, ds))

    ctx_state.defvjp(fwd, bwd)
    return ctx_state


def _rope_tables_call(name, n, inv, shift, axial):
    tm = _pick(n, 1024, 8)
    inv_lane = jnp.tile(inv, LANE // inv.shape[0])[None, :]

    def body(inv_ref, cos_ref, s1_ref, s2_ref):
        t = lax.broadcasted_iota(jnp.int32, (tm, LANE), 0) + pl.program_id(0) * tm
        lane = lax.broadcasted_iota(jnp.int32, (tm, LANE), 1)
        if axial:
            pos = jnp.where(lane % (2 * MLA_ROPE // 2) < MLA_ROPE // 2, t // GRID_W, t % GRID_W)
        else:
            pos = t
        ang = pos.astype(F32) * inv_ref[...]
        sin = jnp.sin(ang)
        first = lane % (2 * shift) < shift
        cos_ref[...] = jnp.cos(ang)
        s1_ref[...] = jnp.where(first, -sin, 0.0)
        s2_ref[...] = jnp.where(first, 0.0, sin)

    tab = jax.ShapeDtypeStruct((n, LANE), F32)
    return tuple(pl.pallas_call(
        body, name=name, grid=(n // tm,), in_specs=[_vec_spec(LANE)], out_specs=[_row_spec(tm, LANE)] * 3,
        out_shape=[tab, tab, tab], compiler_params=_params(),
    )(inv_lane))


def _ret_tables(n_lat):
    inv = RET_ROPE_BASE ** (-jnp.arange(0, RET_DK, 2, dtype=F32) / RET_DK)
    return _rope_tables_call("ret_tables", n_lat, inv, RET_DK // 2, False)


def _make_ret_pack(tag, n_lat, n_ctx):
    hh = RET_HEADS
    tm = MLA_PACK_ROWS
    k_scale = RET_DK ** -0.5
    shift = RET_DK // 2
    tabs = _ret_tables(n_lat)

    def low_lanes():
        return lax.broadcasted_iota(jnp.int32, (1, LANE), 1) < RET_DK

    def rows(width):
        return pl.BlockSpec((tm, width), lambda i: (i, 0))

    def heads(width):
        return pl.BlockSpec((hh, tm, width), lambda i: (0, i, 0))

    def split_pairs(src_ref, dst_ref, scale, rope):
        keep = low_lanes()
        for j in range(hh // 2):
            blk = src_ref[:, LANE * j:LANE * (j + 1)]
            if scale != 1.0:
                blk = blk * scale
            if rope is not None:
                blk = _rope128(blk, *rope, shift=shift)
            dst_ref[2 * j] = jnp.where(keep, blk, 0.0)
            dst_ref[2 * j + 1] = jnp.where(keep, pltpu.roll(blk, RET_DK, 1), 0.0)

    def merge_pairs(src_refs, dst_ref, scale, rope):
        keep = low_lanes()
        for j in range(hh // 2):
            even = sum(r[2 * j] for r in src_refs)
            odd = sum(r[2 * j + 1] for r in src_refs)
            g = jnp.where(keep, even, pltpu.roll(odd, RET_DK, 1))
            if rope is not None:
                g = _rope128_t(g, *rope, shift=shift)
            dst_ref[:, LANE * j:LANE * (j + 1)] = g * scale if scale != 1.0 else g

    def pack_call(name, n, q, k, v, rope):
        with_q = q is not None

        def body(*refs):
            refs = list(refs)
            q_ref = refs.pop(0) if with_q else None
            k_ref, v_ref = refs.pop(0), refs.pop(0)
            tab = tuple(r[...] for r in refs[:3]) if rope else None
            outs = refs[3:] if rope else refs
            if with_q:
                split_pairs(q_ref, outs[0], 1.0, tab)
                outs = outs[1:]
            split_pairs(k_ref, outs[0], k_scale, tab)
            for h in range(hh):
                outs[1][h] = v_ref[:, RET_DV * h:RET_DV * (h + 1)]

        ins = ([q] if with_q else []) + [k, v] + (list(tabs) if rope else [])
        in_specs = ([rows(q.shape[1])] if with_q else []) + [rows(k.shape[1]), rows(v.shape[1])]
        in_specs += [rows(LANE)] * 3 if rope else []
        n_out = 3 if with_q else 2
        return pl.pallas_call(
            body, name=name, grid=(n // tm,), in_specs=in_specs,
            out_specs=[heads(RET_DKP)] * (n_out - 1) + [heads(RET_DV)],
            out_shape=[jax.ShapeDtypeStruct((hh, n, RET_DKP), F32)] * (n_out - 1)
            + [jax.ShapeDtypeStruct((hh, n, RET_DV), F32)],
            compiler_params=_params(),
        )(*ins)

    def unpack_call(name, n, dqs, dks, dvs, rope):
        with_q = len(dqs) > 0
        uses = len(dks)

        def body(*refs):
            refs = list(refs)
            dq_refs = [refs.pop(0) for _ in range(len(dqs))]
            dk_refs = [refs.pop(0) for _ in range(uses)]
            dv_refs = [refs.pop(0) for _ in range(uses)]
            tab = tuple(r[...] for r in refs[:3]) if rope else None
            outs = refs[3:] if rope else refs
            if with_q:
                merge_pairs(dq_refs, outs[0], 1.0, tab)
                outs = outs[1:]
            merge_pairs(dk_refs, outs[0], k_scale, tab)
            for h in range(hh):
                outs[1][:, RET_DV * h:RET_DV * (h + 1)] = sum(r[h] for r in dv_refs)

        ins = list(dqs) + list(dks) + list(dvs) + (list(tabs) if rope else [])
        in_specs = [heads(RET_DKP)] * (len(dqs) + uses) + [heads(RET_DV)] * uses + ([rows(LANE)] * 3 if rope else [])
        n_out = 3 if with_q else 2
        return pl.pallas_call(
            body, name=name, grid=(n // tm,), in_specs=in_specs,
            out_specs=[rows(hh * RET_DK)] * (n_out - 1) + [rows(hh * RET_DV)],
            out_shape=[jax.ShapeDtypeStruct((n, hh * RET_DK), F32)] * (n_out - 1)
            + [jax.ShapeDtypeStruct((n, hh * RET_DV), F32)],
            compiler_params=_params(),
        )(*ins)

    @jax.custom_vjp
    def ret_pack(rq, rk, rv, crk, crv):
        q, k, v = pack_call(tag + "_lat", n_lat, rq, rk, rv, True)
        k_c, v_c = pack_call(tag + "_ctx", n_ctx, None, crk, crv, False)
        return (q, k, v), (q, k, v), (k_c, v_c), (k_c, v_c)

    def fwd(rq, rk, rv, crk, crv):
        return ret_pack(rq, rk, rv, crk, crv), None

    def bwd(_, cts):
        lat_f, lat_b, ctx_f, ctx_b = cts
        drq, drk, drv = unpack_call(tag + "_dlat", n_lat, [lat_f[0], lat_b[0]], [lat_f[1], lat_b[1]],
                                    [lat_f[2], lat_b[2]], True)
        dcrk, dcrv = unpack_call(tag + "_dctx", n_ctx, [], [ctx_f[0], ctx_b[0]], [ctx_f[1], ctx_b[1]], False)
        return drq, drk, drv, dcrk, dcrv

    ret_pack.defvjp(fwd, bwd)
    return ret_pack


def _ret_out_tile(y, g):
    mu = jnp.mean(y, axis=-1, keepdims=True)
    var = jnp.mean(jnp.square(y - mu), axis=-1, keepdims=True)
    return (g * jax.nn.sigmoid(g)) * ((y - mu) * lax.rsqrt(var + GN_EPS))


def _make_ret_out(tag):
    def specs(tm):
        y_spec = pl.BlockSpec((None, tm, RET_DV), lambda h, i: (h, i, 0))
        g_spec = pl.BlockSpec((tm, RET_DV), lambda h, i: (i, h))
        return y_spec, g_spec

    def fwd_call(yf, yb, g):
        hh, n, _ = yf.shape
        tm = _pick(n, 1024, 16)
        y_spec, g_spec = specs(tm)

        def body(yf_ref, yb_ref, g_ref, o_ref):
            o_ref[...] = _ret_out_tile(yf_ref[...] + yb_ref[...], g_ref[...]).astype(BF16)

        return pl.pallas_call(
            body, name=tag + "_fwd", grid=(hh, n // tm), in_specs=[y_spec, y_spec, g_spec], out_specs=g_spec,
            out_shape=jax.ShapeDtypeStruct((n, hh * RET_DV), BF16), compiler_params=_params(),
        )(yf, yb, g)

    def bwd_call(yf, yb, g, do):
        hh, n, _ = yf.shape
        tm = _pick(n, 1024, 16)
        y_spec, g_spec = specs(tm)

        def body(yf_ref, yb_ref, g_ref, do_ref, dy_ref, dg_ref):
            _, vjp = jax.vjp(_ret_out_tile, yf_ref[...] + yb_ref[...], g_ref[...])
            dy, dg = vjp(do_ref[...].astype(F32))
            dy_ref[...] = dy
            dg_ref[...] = dg

        return pl.pallas_call(
            body, name=tag + "_bwd", grid=(hh, n // tm), in_specs=[y_spec, y_spec, g_spec, g_spec],
            out_specs=[y_spec, g_spec],
            out_shape=[jax.ShapeDtypeStruct(yf.shape, F32), jax.ShapeDtypeStruct(g.shape, F32)],
            compiler_params=_params(),
        )(yf, yb, g, do)

    @jax.custom_vjp
    def ret_out(yf, yb, g):
        return fwd_call(yf, yb, g)

    def fwd(yf, yb, g):
        return fwd_call(yf, yb, g), (yf, yb, g)

    def bwd(res, do):
        dy, dg = bwd_call(*res, do)
        return dy, dy, dg

    ret_out.defvjp(fwd, bwd)
    return ret_out


MLA_DQ_PAD = 2 * LANE
MLA_PACK_ROWS = 256


def _rope128(x, cos, s1, s2, shift=16):
    return x * cos + pltpu.roll(x, LANE - shift, 1) * s1 + pltpu.roll(x, shift, 1) * s2


def _rope128_t(g, cos, s1, s2, shift=16):
    return g * cos + pltpu.roll(g * s1, shift, 1) + pltpu.roll(g * s2, LANE - shift, 1)


def _axial_tables(n_lat):
    half = MLA_ROPE // 2
    inv = AXIAL_BASE ** (-jnp.arange(0, half, 2, dtype=F32) / half)
    return _rope_tables_call("mla_tables", n_lat, inv, half // 2, True)


def _make_mla_pack(tag, n_lat, n_ctx, scale):
    hh = MLA_HEADS
    tm = MLA_PACK_ROWS
    ll = n_lat + n_ctx
    rope0 = hh * MLA_NOPE
    tabs = _axial_tables(n_lat)

    def rope_lanes():
        return lax.broadcasted_iota(jnp.int32, (1, LANE), 1) < MLA_ROPE

    def rows(width):
        return pl.BlockSpec((tm, width), lambda i: (i, 0))

    def heads(width, off):
        return pl.BlockSpec((hh, tm, width), lambda i: (0, i + off, 0))

    def heads_t(width, off):
        return pl.BlockSpec((hh, width, tm), lambda i: (0, 0, i + off))

    def put_kv(kv_ref, kr_rot, k_ref, v_ref, kt_ref, vt_ref):
        kr_b = kr_rot.astype(BF16)
        kr_t = jnp.transpose(kr_rot).astype(BF16)
        for h in range(hh):
            k_nope = kv_ref[:, 2 * LANE * h:2 * LANE * h + MLA_NOPE]
            val = kv_ref[:, 2 * LANE * h + MLA_NOPE:2 * LANE * (h + 1)]
            k_ref[h, :, :MLA_NOPE] = k_nope.astype(BF16)
            k_ref[h, :, MLA_NOPE:] = kr_b
            v_ref[h] = val.astype(BF16)
            kt_ref[h, :MLA_NOPE, :] = jnp.transpose(k_nope).astype(BF16)
            kt_ref[h, MLA_NOPE:, :] = kr_t
            vt_ref[h] = jnp.transpose(val).astype(BF16)

    def fwd_lat(qp, kv, kr):
        def body(qp_ref, kv_ref, kr_ref, cos_ref, s1_ref, s2_ref, q_ref, k_ref, v_ref, kt_ref, vt_ref):
            cos, s1, s2 = cos_ref[...], s1_ref[...], s2_ref[...]
            keep = rope_lanes()
            for j in range(hh // 2):
                rot = _rope128(qp_ref[:, rope0 + LANE * j:rope0 + LANE * (j + 1)], cos, s1, s2)
                q_ref[2 * j, :, MLA_NOPE:] = jnp.where(keep, rot, 0.0).astype(BF16)
                q_ref[2 * j + 1, :, MLA_NOPE:] = jnp.where(keep, pltpu.roll(rot, MLA_ROPE, 1), 0.0).astype(BF16)
            for h in range(hh):
                q_ref[h, :, :MLA_NOPE] = qp_ref[:, MLA_NOPE * h:MLA_NOPE * (h + 1)].astype(BF16)
            kr_rot = jnp.where(keep, _rope128(kr_ref[...], cos, s1, s2), 0.0)
            put_kv(kv_ref, kr_rot, k_ref, v_ref, kt_ref, vt_ref)

        return pl.pallas_call(
            body, name=tag + "_lat", grid=(n_lat // tm,),
            in_specs=[rows(qp.shape[1]), rows(kv.shape[1]), rows(LANE), rows(LANE), rows(LANE), rows(LANE)],
            out_specs=[heads(MLA_DQ_PAD, 0), heads(MLA_DQ_PAD, 0), heads(MLA_V, 0), heads_t(MLA_DQ_PAD, 0),
                       heads_t(MLA_V, 0)],
            out_shape=[jax.ShapeDtypeStruct((hh, n_lat, MLA_DQ_PAD), BF16),
                       jax.ShapeDtypeStruct((hh, ll, MLA_DQ_PAD), BF16), jax.ShapeDtypeStruct((hh, ll, MLA_V), BF16),
                       jax.ShapeDtypeStruct((hh, MLA_DQ_PAD, ll), BF16), jax.ShapeDtypeStruct((hh, MLA_V, ll), BF16)],
            compiler_params=_params(),
        )(qp, kv, kr, *tabs)

    def fwd_ctx(kv_c, kr_c, bufs):
        def body(kv_ref, kr_ref, k_in, v_in, kt_in, vt_in, k_ref, v_ref, kt_ref, vt_ref):
            kr_rot = jnp.where(rope_lanes(), kr_ref[...], 0.0)
            put_kv(kv_ref, kr_rot, k_ref, v_ref, kt_ref, vt_ref)

        any_spec = pl.BlockSpec(memory_space=pl.ANY)
        off = n_lat // tm
        return pl.pallas_call(
            body, name=tag + "_ctx", grid=(n_ctx // tm,),
            in_specs=[rows(kv_c.shape[1]), rows(LANE)] + [any_spec] * 4,
            out_specs=[heads(MLA_DQ_PAD, off), heads(MLA_V, off), heads_t(MLA_DQ_PAD, off), heads_t(MLA_V, off)],
            out_shape=[jax.ShapeDtypeStruct(b.shape, BF16) for b in bufs],
            input_output_aliases={2: 0, 3: 1, 4: 2, 5: 3}, compiler_params=_params(),
        )(kv_c, kr_c, *bufs)

    def take_kv(dk_ref, dv_ref, dkv_ref):
        dkr = jnp.zeros((tm, LANE), F32)
        for h in range(hh):
            dkv_ref[:, 2 * LANE * h:2 * LANE * h + MLA_NOPE] = dk_ref[h, :, :MLA_NOPE].astype(F32)
            dkv_ref[:, 2 * LANE * h + MLA_NOPE:2 * LANE * (h + 1)] = dv_ref[h].astype(F32)
            dkr = dkr + dk_ref[h, :, MLA_NOPE:].astype(F32)
        return jnp.where(rope_lanes(), dkr, 0.0)

    def bwd_lat(dqt, dk, dv, qp_width, kv_width):
        def body(dqt_ref, dk_ref, dv_ref, cos_ref, s1_ref, s2_ref, dqp_ref, dkv_ref, dkr_ref):
            cos, s1, s2 = cos_ref[...], s1_ref[...], s2_ref[...]
            keep = rope_lanes()
            for j in range(hh // 2):
                even = jnp.transpose(dqt_ref[2 * j]) * scale
                odd = jnp.transpose(dqt_ref[2 * j + 1]) * scale
                dqp_ref[:, MLA_NOPE * 2 * j:MLA_NOPE * (2 * j + 1)] = even[:, :MLA_NOPE]
                dqp_ref[:, MLA_NOPE * (2 * j + 1):MLA_NOPE * (2 * j + 2)] = odd[:, :MLA_NOPE]
                g = jnp.where(keep, even[:, MLA_NOPE:], pltpu.roll(odd[:, MLA_NOPE:], MLA_ROPE, 1))
                dqp_ref[:, rope0 + LANE * j:rope0 + LANE * (j + 1)] = _rope128_t(g, cos, s1, s2)
            dkr_ref[...] = jnp.where(keep, _rope128_t(take_kv(dk_ref, dv_ref, dkv_ref), cos, s1, s2), 0.0)

        return pl.pallas_call(
            body, name=tag + "_dlat", grid=(n_lat // tm,),
            in_specs=[pl.BlockSpec((hh, MLA_DQ_PAD, tm), lambda i: (0, 0, i)),
                      heads(MLA_DQ_PAD, 0), heads(MLA_V, 0), rows(LANE), rows(LANE), rows(LANE)],
            out_specs=[rows(qp_width), rows(kv_width), rows(LANE)],
            out_shape=[jax.ShapeDtypeStruct((n_lat, qp_width), F32), jax.ShapeDtypeStruct((n_lat, kv_width), F32),
                       jax.ShapeDtypeStruct((n_lat, LANE), F32)],
            compiler_params=_params(),
        )(dqt, dk, dv, *tabs)

    def bwd_ctx(dk, dv, kv_width):
        def body(dk_ref, dv_ref, dkv_ref, dkr_ref):
            dkr_ref[...] = take_kv(dk_ref, dv_ref, dkv_ref)

        off = n_lat // tm
        return pl.pallas_call(
            body, name=tag + "_dctx", grid=(n_ctx // tm,),
            in_specs=[heads(MLA_DQ_PAD, off), heads(MLA_V, off)],
            out_specs=[rows(kv_width), rows(LANE)],
            out_shape=[jax.ShapeDtypeStruct((n_ctx, kv_width), F32), jax.ShapeDtypeStruct((n_ctx, LANE), F32)],
            compiler_params=_params(),
        )(dk, dv)

    def pack(qp, kv, kr, kv_c, kr_c):
        q, *bufs = fwd_lat(qp, kv, kr)
        return (q, *fwd_ctx(kv_c, kr_c, bufs))

    def unpack(dqt, dk, dv):
        qp_width, kv_width = hh * (MLA_NOPE + MLA_ROPE), hh * (MLA_NOPE + MLA_V)
        dqp, dkv, dkr = bwd_lat(dqt, dk, dv, qp_width, kv_width)
        dkv_c, dkr_c = bwd_ctx(dk, dv, kv_width)
        return dqp, dkv, dkr, dkv_c, dkr_c

    return pack, unpack


def _make_mla(tag, n_lat, n_ctx):
    scale = (MLA_NOPE + MLA_ROPE) ** -0.5
    pack, unpack = _make_mla_pack(tag + "pack", n_lat, n_ctx, scale)
    attn_fwd, attn_delta, attn_bwd = _make_attention(tag, scale, MLA_NOPE + MLA_ROPE)

    @jax.custom_vjp
    def mla(qp, kv, kr, kv_c, kr_c):
        q, k, _, _, vt = pack(qp, kv, kr, kv_c, kr_c)
        return attn_fwd(q, k, vt)[0]

    def fwd(qp, kv, kr, kv_c, kr_c):
        q, k, v, kt, vt = pack(qp, kv, kr, kv_c, kr_c)
        o, lse = attn_fwd(q, k, vt)
        return o, (q, k, kt, v, o, lse)

    def bwd(res, do):
        q, k, kt, v, o, lse = res
        delta = attn_delta(o, do, q.shape[0])
        dqt, dk, dv = attn_bwd(q, k, kt, v, do, lse, delta)
        return unpack(dqt, dk, dv)

    mla.defvjp(fwd, bwd)
    return mla


def _make_attention(tag, scale, dq_live=None):
    neg_big = -1e30
    log2e = 1.4426950408889634
    sub = 256

    def fwd_call(q, k, vt):
        hh, n, dq = q.shape
        dv, ll = vt.shape[1], vt.shape[2]
        tq, tk = _pick(n, 1024), _pick(ll, 1408)
        sb = sub if tk % sub == 0 else tk
        c2 = scale * log2e
        k_steps = ll // tk

        def body(q_ref, k_ref, vt_ref, o_ref, lse_ref, m_scr, l_scr, acc_scr, s_scr, p_scr):
            j = pl.program_id(2)

            @pl.when(j == 0)
            def _():
                m_scr[...] = jnp.full_like(m_scr, neg_big)
                l_scr[...] = jnp.zeros_like(l_scr)
                acc_scr[...] = jnp.zeros_like(acc_scr)

            q_t = q_ref[...]
            m_prev = m_scr[...]
            m_new = m_prev
            for kk in range(tk // sb):
                rows = slice(kk * sb, (kk + 1) * sb)
                s_t = _dot(k_ref[rows, :], q_t, NT)
                s_scr[rows, :] = s_t
                m_new = jnp.maximum(m_new, jnp.max(s_t, axis=0, keepdims=True))
            mc = m_new * c2
            l_part = jnp.zeros_like(m_new)
            for kk in range(tk // sb):
                rows = slice(kk * sb, (kk + 1) * sb)
                p_t = jnp.exp2(s_scr[rows, :] * c2 - mc)
                l_part = l_part + jnp.sum(p_t, axis=0, keepdims=True)
                p_scr[rows, :] = p_t.astype(BF16)
            alpha = jnp.exp2((m_prev - m_new) * c2)
            l_scr[...] = alpha * l_scr[...] + l_part
            acc_scr[...] = alpha * acc_scr[...] + _dot(vt_ref[...], p_scr[...], NN)
            m_scr[...] = m_new

            @pl.when(j == k_steps - 1)
            def _():
                o_ref[...] = jnp.transpose(acc_scr[...] / l_scr[...]).astype(BF16)
                lse_ref[...] = m_scr[...] * scale + jnp.log(l_scr[...])

        return pl.pallas_call(
            body, name=tag + "_fwd", grid=(hh, n // tq, k_steps),
            in_specs=[pl.BlockSpec((None, tq, dq), lambda h, i, j: (h, i, 0)),
                      pl.BlockSpec((None, tk, dq), lambda h, i, j: (h, j, 0)),
                      pl.BlockSpec((None, dv, tk), lambda h, i, j: (h, 0, j))],
            out_specs=[pl.BlockSpec((tq, dv), lambda h, i, j: (i, h)),
                       pl.BlockSpec((None, 1, tq), lambda h, i, j: (h, 0, i))],
            out_shape=[jax.ShapeDtypeStruct((n, hh * dv), BF16), jax.ShapeDtypeStruct((hh, 1, n), F32)],
            scratch_shapes=[pltpu.VMEM((1, tq), F32), pltpu.VMEM((1, tq), F32), pltpu.VMEM((dv, tq), F32),
                            pltpu.VMEM((tk, tq), F32), pltpu.VMEM((tk, tq), BF16)],
            compiler_params=_params(),
        )(q, k, vt)

    def delta_call(o, do, hh):
        n = o.shape[0]
        dv = o.shape[1] // hh
        tq = _pick(n, 1024)

        def body(o_ref, do_ref, d_ref):
            prod_t = jnp.transpose(o_ref[...].astype(F32) * do_ref[...].astype(F32))
            d_ref[...] = jnp.sum(prod_t, axis=0, keepdims=True)

        spec = pl.BlockSpec((tq, dv), lambda h, i: (i, h))
        return pl.pallas_call(
            body, name=tag + "_delta", grid=(hh, n // tq), in_specs=[spec, spec],
            out_specs=pl.BlockSpec((None, 1, tq), lambda h, i: (h, 0, i)),
            out_shape=jax.ShapeDtypeStruct((hh, 1, n), F32), compiler_params=_params(),
        )(o, do)

    def bwd_call(q, k, kt, v, do, lse, delta):
        hh, n, dq = q.shape
        ll, dv = k.shape[1], v.shape[2]
        tq, tk = _pick(n, 1024), _pick(ll, 1408)
        sb = tk
        c2 = scale * log2e
        q_steps = n // tq
        live = dq_live or dq

        def body(q_ref, k_ref, kt_ref, v_ref, do_ref, lse_ref, d_ref, dqt_ref, dk_ref, dv_ref, dk_scr, dv_scr):
            j = pl.program_id(1)
            i = pl.program_id(2)

            @pl.when(i == 0)
            def _():
                dk_scr[...] = jnp.zeros_like(dk_scr)
                dv_scr[...] = jnp.zeros_like(dv_scr)

            q_t, do_t = q_ref[...], do_ref[...]
            lse2 = lse_ref[...] * log2e
            delta_t = d_ref[...]
            dq_part = None
            for kk in range(tk // sb):
                rows = slice(kk * sb, (kk + 1) * sb)
                s_t = _dot(k_ref[rows, :], q_t, NT)
                p_t = jnp.exp2(s_t * c2 - lse2)
                ds_t = p_t * (_dot(v_ref[rows, :], do_t, NT) - delta_t)
                dv_scr[rows, :] += _dot(p_t, do_t, NN)
                dk_scr[rows, :] += _dot(ds_t, q_t, NN)
                part = _dot(kt_ref[:live, rows], ds_t, NN)
                dq_part = part if dq_part is None else dq_part + part
            cols = pl.ds(pl.multiple_of(i * tq, tq), tq)

            @pl.when(j == 0)
            def _():
                dqt_ref[:live, cols] = dq_part
                if live < dq:
                    dqt_ref[live:, cols] = jnp.zeros((dq - live, tq), F32)

            @pl.when(j > 0)
            def _():
                dqt_ref[:live, cols] += dq_part

            @pl.when(i == q_steps - 1)
            def _():
                dk_ref[...] = (dk_scr[...] * scale).astype(BF16)
                dv_ref[...] = dv_scr[...].astype(BF16)

        return pl.pallas_call(
            body, name=tag + "_bwd", grid=(hh, ll // tk, q_steps),
            in_specs=[pl.BlockSpec((None, tq, dq), lambda h, j, i: (h, i, 0)),
                      pl.BlockSpec((None, tk, dq), lambda h, j, i: (h, j, 0)),
                      pl.BlockSpec((None, dq, tk), lambda h, j, i: (h, 0, j)),
                      pl.BlockSpec((None, tk, dv), lambda h, j, i: (h, j, 0)),
                      pl.BlockSpec((tq, dv), lambda h, j, i: (i, h)),
                      pl.BlockSpec((None, 1, tq), lambda h, j, i: (h, 0, i)),
                      pl.BlockSpec((None, 1, tq), lambda h, j, i: (h, 0, i))],
            out_specs=[pl.BlockSpec((None, dq, n), lambda h, j, i: (h, 0, 0)),
                       pl.BlockSpec((None, tk, dq), lambda h, j, i: (h, j, 0)),
                       pl.BlockSpec((None, tk, dv), lambda h, j, i: (h, j, 0))],
            out_shape=[jax.ShapeDtypeStruct((hh, dq, n), F32), jax.ShapeDtypeStruct((hh, ll, dq), BF16),
                       jax.ShapeDtypeStruct((hh, ll, dv), BF16)],
            scratch_shapes=[pltpu.VMEM((tk, dq), F32), pltpu.VMEM((tk, dv), F32)],
            compiler_params=_params(),
        )(q, k, kt, v, do, lse, delta)

    return fwd_call, delta_call, bwd_call


def _loss_tile(x, g, tgt):
    r = lax.rsqrt(jnp.mean(x * x, axis=-1, keepdims=True) + RMS_EPS)
    err = x * r * g - tgt
    per_tok = jnp.mean(err * err, axis=-1, keepdims=True)
    return 0.5 * jnp.sum(per_tok, axis=0, keepdims=True)


def _make_final_loss(tag):
    def fwd_call(x, g, tgt):
        t, d = x.shape
        tm = _pick(t, 512, 16)

        def body(x_ref, g_ref, t_ref, l_ref):
            l_ref[...] = jnp.broadcast_to(_loss_tile(x_ref[...], g_ref[...], t_ref[...]), (1, LANE))

        parts = pl.pallas_call(
            body, name=tag + "_fwd", grid=(t // tm,),
            in_specs=[_row_spec(tm, d), _vec_spec(d), _row_spec(tm, d)],
            out_specs=pl.BlockSpec((None, 1, LANE), lambda i: (i, 0, 0)),
            out_shape=jax.ShapeDtypeStruct((t // tm, 1, LANE), F32), compiler_params=_params(),
        )(x, g, tgt)
        return jnp.sum(parts[:, 0, 0])

    def bwd_call(x, g, tgt, dl):
        t, d = x.shape
        tm = _pick(t, 256, 16)

        def body(x_ref, g_ref, t_ref, dl_ref, dx_ref, dg_ref):
            _, vjp = jax.vjp(_loss_tile, x_ref[...], g_ref[...], t_ref[...])
            dx, dg, _ = vjp(dl_ref[...])
            dx_ref[...] = dx

            @pl.when(pl.program_id(0) == 0)
            def _():
                dg_ref[...] = jnp.zeros_like(dg_ref)

            dg_ref[...] += dg

        return pl.pallas_call(
            body, name=tag + "_bwd", grid=(t // tm,),
            in_specs=[_row_spec(tm, d), _vec_spec(d), _row_spec(tm, d), pl.BlockSpec((1, 1), lambda i: (0, 0))],
            out_specs=[_row_spec(tm, d), _vec_spec(d)],
            out_shape=[jax.ShapeDtypeStruct((t, d), F32), jax.ShapeDtypeStruct((1, d), F32)],
            compiler_params=_params(),
        )(x, g, tgt, dl)

    @jax.custom_vjp
    def final_loss(x, g, tgt):
        return fwd_call(x, g, tgt)

    def fwd(x, g, tgt):
        return fwd_call(x, g, tgt), (x, g, tgt)

    def bwd(res, dl):
        x, g, tgt = res
        dx, dg = bwd_call(x, g, tgt, dl.reshape(1, 1).astype(F32))
        return dx, dg, jnp.zeros_like(tgt)

    final_loss.defvjp(fwd, bwd)
    return final_loss


def _exchange(arrays, gather, name):
    n = len(arrays)

    def body(*refs):
        ins, outs = refs[:n], refs[n:2 * n]
        send_sems, recv_sems, local_sems = refs[2 * n:]
        me = 4 * lax.axis_index("x") + 2 * lax.axis_index("y") + lax.axis_index("c")

        def remote(a, d, wait_side=False):
            peer = (me + d) % N_DEV
            origin = (me + N_DEV - d) % N_DEV
            src = ins[a] if gather else ins[a].at[peer]
            dst = outs[a].at[origin if wait_side else me]
            return pltpu.make_async_remote_copy(
                src_ref=src, dst_ref=dst, send_sem=send_sems.at[a, d - 1], recv_sem=recv_sems.at[a, d - 1],
                device_id=(peer // 4, (peer // 2) % 2, peer % 2), device_id_type=pl.DeviceIdType.MESH)

        def local(a):
            src = ins[a] if gather else ins[a].at[me]
            return pltpu.make_async_copy(src, outs[a].at[me], local_sems.at[a])

        for a in range(n):
            for d in range(1, N_DEV):
                remote(a, d).start()
            local(a).start()
        for a in range(n):
            local(a).wait()
            for d in range(1, N_DEV):
                remote(a, d, wait_side=True).wait_recv()
                remote(a, d).wait_send()

    out_shape = []
    for arr in arrays:
        shape = (N_DEV,) + arr.shape if gather else arr.shape
        out_shape.append(jax.ShapeDtypeStruct(shape, arr.dtype))
    any_spec = pl.BlockSpec(memory_space=pl.ANY)
    return pl.pallas_call(
        body, name=name, in_specs=[any_spec] * n, out_specs=[any_spec] * n, out_shape=out_shape,
        scratch_shapes=[pltpu.SemaphoreType.DMA((n, N_DEV - 1)), pltpu.SemaphoreType.DMA((n, N_DEV - 1)),
                        pltpu.SemaphoreType.DMA((n,))],
        compiler_params=pltpu.CompilerParams(has_side_effects=True),
    )(*arrays)


def _split_copy(ins, lands, send_sems, recv_sems, a, d, gather, wait_side):
    me = 4 * lax.axis_index("x") + 2 * lax.axis_index("y") + lax.axis_index("c")
    peer = (me + d) % N_DEV
    origin = (me + N_DEV - d) % N_DEV
    return pltpu.make_async_remote_copy(
        src_ref=ins[a] if gather else ins[a].at[peer], dst_ref=lands[a].at[origin if wait_side else me],
        send_sem=send_sems.at[a * (N_DEV - 1) + d - 1], recv_sem=recv_sems.at[a * (N_DEV - 1) + d - 1],
        device_id=(peer // 4, (peer // 2) % 2, peer % 2), device_id_type=pl.DeviceIdType.MESH)


def _exchange_start(srcs, lands, after, gather, name):
    n = len(srcs)

    def body(*refs):
        ins, lnd = refs[:n], refs[n:2 * n]
        send_sems, recv_sems = refs[2 * n + 1], refs[2 * n + 2]
        for a in range(n):
            for d in range(1, N_DEV):
                _split_copy(ins, lnd, send_sems, recv_sems, a, d, gather, False).start()

    hbm = pl.BlockSpec(memory_space=pltpu.HBM)
    sem = pl.BlockSpec(memory_space=pltpu.SEMAPHORE)
    bufs = [pltpu.with_memory_space_constraint(t, pltpu.HBM) for t in list(srcs) + list(lands) + [after]]
    res = pl.pallas_call(
        body, name=name,
        in_specs=[hbm] * (2 * n + 1), out_specs=[sem, sem] + [hbm] * (2 * n + 1),
        out_shape=[pltpu.SemaphoreType.DMA((n * (N_DEV - 1),)), pltpu.SemaphoreType.DMA((n * (N_DEV - 1),))]
        + [pltpu.HBM(t.shape, t.dtype) for t in bufs],
        input_output_aliases={i: 2 + i for i in range(2 * n + 1)},
        compiler_params=pltpu.CompilerParams(has_side_effects=pltpu.SideEffectType.DATAFLOW_SIDE_EFFECTING),
    )(*bufs)
    return res[0], res[1], res[2:2 + n], res[2 + n:2 + 2 * n], res[-1]


def _exchange_wait(send_sems, recv_sems, srcs, lands, after, gather, name):
    n = len(srcs)

    def body(*refs):
        ins, lnd = refs[:n], refs[n:2 * n]
        send_sems_ref, recv_sems_ref = refs[2 * n], refs[2 * n + 1]
        for a in range(n):
            for d in range(1, N_DEV):
                _split_copy(ins, lnd, send_sems_ref, recv_sems_ref, a, d, gather, False).wait_send()
                _split_copy(ins, lnd, send_sems_ref, recv_sems_ref, a, d, gather, True).wait_recv()

    hbm = pl.BlockSpec(memory_space=pltpu.HBM)
    sem = pl.BlockSpec(memory_space=pltpu.SEMAPHORE)
    bufs = list(srcs) + list(lands)
    res = pl.pallas_call(
        body, name=name,
        in_specs=[hbm] * (2 * n) + [sem, sem, pl.BlockSpec(memory_space=pl.ANY)],
        out_specs=[hbm] * (2 * n),
        out_shape=[pltpu.HBM(t.shape, t.dtype) for t in bufs],
        input_output_aliases={i: i for i in range(2 * n)},
        compiler_params=pltpu.CompilerParams(has_side_effects=pltpu.SideEffectType.DATAFLOW_SIDE_EFFECTING),
    )(*bufs, send_sems, recv_sems, after)
    return res[n:]


def _own_slot(block, me):
    empty = lax.empty((N_DEV,) + block.shape, block.dtype)
    return lax.dynamic_update_slice(empty, block[None], (me,) + (0,) * block.ndim)


def _coords():
    return lax.axis_index("x"), lax.axis_index("y"), lax.axis_index("c")


def _other_chips(x, y):
    return [(1 - x, y), (x, 1 - y), (1 - x, 1 - y)]


def _gather_two_level(arrays, name):
    n = len(arrays)

    def body(*refs):
        ins, outs = refs[:n], refs[n:2 * n]
        send_sems, recv_sems, local_sems = refs[2 * n:]
        x, y, c = _coords()
        me, sib = (x, y, c), (x, y, 1 - c)
        chips = _other_chips(x, y)

        def copy(a, k, block, to, from_input=False):
            slot = 4 * block[0] + 2 * block[1] + block[2]
            return pltpu.make_async_remote_copy(
                src_ref=ins[a] if from_input else outs[a].at[slot], dst_ref=outs[a].at[slot],
                send_sem=send_sems.at[a, k], recv_sem=recv_sems.at[a, k],
                device_id=to, device_id_type=pl.DeviceIdType.MESH)

        def local(a):
            return pltpu.make_async_copy(ins[a], outs[a].at[4 * x + 2 * y + c], local_sems.at[a])

        for a in range(n):
            for j, chip in enumerate(chips):
                copy(a, 1 + j, me, (*chip, c), True).start()
            copy(a, 0, me, sib, True).start()
            local(a).start()
        for a in range(n):
            for j, chip in enumerate(chips):
                copy(a, 1 + j, (*chip, c), me).wait_recv()
                copy(a, 4 + j, (*chip, c), sib).start()
        for a in range(n):
            copy(a, 0, sib, me).wait_recv()
            for j, chip in enumerate(chips):
                copy(a, 4 + j, (*chip, 1 - c), me).wait_recv()
            for k in range(N_DEV - 1):
                copy(a, k, me, sib, True).wait_send()
            local(a).wait()

    any_spec = pl.BlockSpec(memory_space=pl.ANY)
    return pl.pallas_call(
        body, name=name, in_specs=[any_spec] * n, out_specs=[any_spec] * n,
        out_shape=[jax.ShapeDtypeStruct((N_DEV,) + arr.shape, arr.dtype) for arr in arrays],
        scratch_shapes=[pltpu.SemaphoreType.DMA((n, N_DEV - 1)), pltpu.SemaphoreType.DMA((n, N_DEV - 1)),
                        pltpu.SemaphoreType.DMA((n,))],
        compiler_params=pltpu.CompilerParams(has_side_effects=True),
    )(*arrays)


def _make_gather_op(tag):
    @jax.custom_vjp
    def gather_op(xl):
        return _exchange([xl], True, tag + "_gather")[0]

    def fwd(xl):
        return gather_op(xl), None

    def bwd(_, g):
        return (jnp.sum(_exchange([g], False, tag + "_scatter")[0], axis=0),)

    gather_op.defvjp(fwd, bwd)
    return gather_op


def _adamw(gstack, w, m, v, name):
    s, r, cn = gstack.shape
    tr = _pick(r, max(8, (2 * 1024 * 1024) // (4 * cn) // 8 * 8), 8)
    c1 = 1.0 - ADAM_B1 ** ADAM_STEP
    c2 = 1.0 - ADAM_B2 ** ADAM_STEP

    def body(g_ref, w_ref, m_ref, v_ref, go_ref, d_ref, mo_ref, vo_ref):
        g = g_ref[0].astype(F32)
        for q in range(1, s):
            g = g + g_ref[q].astype(F32)
        m_new = ADAM_B1 * m_ref[...] + (1.0 - ADAM_B1) * g
        v_new = ADAM_B2 * v_ref[...] + (1.0 - ADAM_B2) * (g * g)
        go_ref[...] = g
        mo_ref[...] = m_new
        vo_ref[...] = v_new
        d_ref[...] = -ADAM_LR * ((m_new / c1) / (jnp.sqrt(v_new / c2) + ADAM_EPS) + ADAM_WD * w_ref[...])

    tile = pl.BlockSpec((tr, cn), lambda i: (i, 0))
    out = jax.ShapeDtypeStruct((r, cn), F32)
    return pl.pallas_call(
        body, name=name, grid=(r // tr,),
        in_specs=[pl.BlockSpec((s, tr, cn), lambda i: (0, i, 0)), tile, tile, tile],
        out_specs=[tile, tile, tile, tile], out_shape=[out, out, out, out],
        compiler_params=_params(),
    )(gstack, w, m, v)


def _cols_from_stack(w):
    return jnp.swapaxes(w, 0, 1).reshape(w.shape[1], N_DEV * w.shape[2])


def _ada_vectors(p, silu_c_all, me):
    d = p["c_ctx"].shape[0]
    n_a = p["ada_w"].shape[1]
    a_in = jnp.concatenate([silu_c_all, jax.nn.silu(p["c_ctx"])[None, :], jnp.zeros((7, d), F32)], axis=0)
    b_loc = lax.dynamic_slice(p["ada_b"], (0, me * n_a), (1, n_a))
    r_loc = _make_small_mm("ada")(a_in, p["ada_w"]) + b_loc
    r_full = _make_gather_op("ada")(r_loc)
    m_lat = lax.dynamic_index_in_dim(r_full, me, axis=1, keepdims=False).reshape(N_MOD, 1, d)
    m_ctx = r_full[:, N_DEV, :].reshape(N_MOD, 1, d)
    return m_lat, m_ctx


def _stage_a_fwd(p_ada, x, ctx, ng, w_in, wait_w_out, silu_c_all, me):
    (m_lat, m_ctx), vjp_ada = jax.vjp(lambda q: _ada_vectors(q, silu_c_all, me), p_ada)
    lat, cx = _ffn_parts("ffn1"), _ffn_parts("ffn1c")
    h, a, gu = lat[0](x, ng, m_lat[0], m_lat[1], w_in)
    hc, ac, guc = cx[0](ctx, ng, m_ctx[0], m_ctx[1], w_in)
    w_out = wait_w_out(a)
    x1, f1 = lat[1](a, w_out, x, m_lat[2])
    c1, f1c = cx[1](ac, w_out, ctx, m_ctx[2])
    res = dict(vjp_ada=vjp_ada, m_lat=m_lat, m_ctx=m_ctx, x=x, ctx=ctx, ng=ng, w_in=w_in, w_out=w_out,
               lat=(h, a, gu, f1), cx=(hc, ac, guc, f1c))
    return (x1, c1, m_lat, m_ctx), res


def _stage_a_bwd(res, dx1, dc1, dm_lat, dm_ctx, start_grads):
    lat, cx = _ffn_parts("ffn1"), _ffn_parts("ffn1c")
    m_lat, m_ctx, ng, w_in, w_out = res["m_lat"], res["m_ctx"], res["ng"], res["w_in"], res["w_out"]
    hc, ac, guc, f1c = res["cx"]
    dgate_c, dgu_c, dw_out_c = cx[2](dc1, f1c, m_ctx[2], w_out, guc, ac)
    _, dng_c, dsh_c, dsc_c, dw_in_c = cx[3](res["ctx"], ng, m_ctx[0], m_ctx[1], w_in, hc, dgu_c, dc1)
    h, a, gu, f1 = res["lat"]
    dgate, dgu, dw_out = lat[2](dx1, f1, m_lat[2], w_out, gu, a)
    dgu = start_grads("ffn1_w_out", dw_out + dw_out_c, dgu)
    dw_in = _ffn_dwin_bwd(h, dgu, "ffn1_dwin") + dw_in_c
    dgu = start_grads("ffn1_w_in", dw_in, dgu)
    dh = _ffn_dh_bwd(dgu, w_in, "ffn1_dh")
    dx, dng, dsc, dsh = _norm_mod_bwd(res["x"], ng, m_lat[1], m_lat[0], dh, dx1, "ffn1_dnorm")

    def rows(dsh_, dsc_, dgate_):
        return jnp.concatenate([dsh_, dsc_, dgate_, jnp.zeros((N_MOD - 3,) + dsh_.shape, F32)[:, 0]], axis=0)[:, None, :]

    (g_ada,) = res["vjp_ada"]((dm_lat + rows(dsh, dsc, dgate), dm_ctx + rows(dsh_c, dsc_c, dgate_c)))
    return g_ada, dx, dng + dng_c


def _stage_b(p, x1, c1, m_lat, m_ctx):
    n_lat, d = x1.shape
    n_ctx = c1.shape[0]
    w_mix = jnp.pad(_cols_from_stack(p["mix_w_in"]), ((0, 0), (0, MIX_IN_PAD - MIX_IN)))
    proj, x1 = _make_norm_proj_carry("mix")(x1, p["norm2_g"], m_lat[3], m_lat[4], w_mix)
    proj_c = _make_norm_proj("mixc")(c1, p["norm2_g"], m_ctx[3], m_ctx[4], w_mix)
    widths = SPLITS[:6] + (LANE,)
    rq, rk, rv, rg, cq, ckv, kr = _make_split("mixsplit", widths, MIX_IN_PAD)(proj)
    _, crk, crv, _, _, cckv, ckr = _make_split("mixsplitc", widths, MIX_IN_PAD)(proj_c)

    zq = jnp.zeros((1, MLA_Q_RANK), F32)
    zkv = jnp.zeros((1, MLA_KV_RANK), F32)
    w_uq3 = _cols_from_stack(p["mla_w_uq"]).reshape(MLA_Q_RANK, MLA_HEADS, MLA_NOPE + MLA_ROPE)
    w_uq = jnp.concatenate([w_uq3[:, :, :MLA_NOPE].reshape(MLA_Q_RANK, -1),
                            w_uq3[:, :, MLA_NOPE:].reshape(MLA_Q_RANK, -1)], axis=1)
    w_ukv = _cols_from_stack(p["mla_w_ukv"])
    q = _make_norm_proj("uq")(cq, p["mla_q_norm_g"], zq, zq, w_uq)
    kv = _make_norm_proj("ukv")(ckv, p["mla_kv_norm_g"], zkv, zkv, w_ukv)
    kv_c = _make_norm_proj("ukvc")(cckv, p["mla_kv_norm_g"], zkv, zkv, w_ukv)

    lg_f = jax.nn.log_sigmoid(p["ret_decay_fwd"][0])
    lg_b = jax.nn.log_sigmoid(p["ret_decay_bwd"][0])
    lat_f, lat_b, ctx_f, ctx_b = _make_ret_pack("retpack", n_lat, n_ctx)(rq, rk, rv, crk, crv)
    assert n_ctx == RET_CHUNK, "the context prefix is one retention chunk"

    def lanes(lg):
        return jnp.broadcast_to(lg[:, None, None], (RET_HEADS, 1, LANE))

    s0_f = _make_ctx_state("retcf", False)(*ctx_f, lanes(lg_f))
    s0_b = _make_ctx_state("retcb", True)(*ctx_b, lanes(lg_b))
    y_f = _make_ret_dir("retf", False)(*lat_f, lanes(lg_f), s0_f)
    y_b = _make_ret_dir("retb", True)(*lat_b, lanes(lg_b), s0_b)
    ret_o = _make_ret_out("reto")(y_f, y_b, rg)

    mla_o = _make_mla("mla", n_lat, n_ctx)(q, kv, kr, kv_c, ckr)

    w_mo = p["mix_w_out"].reshape(-1, d)
    return _make_res_proj("mixo")(jnp.concatenate([ret_o, mla_o], axis=-1), w_mo, x1, m_lat[5])


def _stage_c(p, x2, m_lat, tgt):
    x3 = _make_ffn_block("ffn2")(x2, p["norm3_g"], m_lat[6], m_lat[7], m_lat[8], p["ffn2_w_in"], p["ffn2_w_out"])
    return _make_final_loss("loss")(x3, p["final_norm_g"], tgt)


FIRST = ("ffn1_w_in", "ffn1_w_out")
MID = ("mix_w_in", "mla_w_uq", "mla_w_ukv", "mix_w_out")
LAST = ("ffn2_w_in", "ffn2_w_out")
BIG = FIRST + MID + LAST
SMALL = ("c_ctx", "ada_b", "norm1_g", "norm2_g", "ret_decay_fwd", "ret_decay_bwd", "mla_q_norm_g",
         "mla_kv_norm_g", "norm3_g", "final_norm_g")
WEIGHTS = ("c_ctx", "ada_w", "ada_b", "norm1_g", "ffn1_w_in", "ffn1_w_out", "norm2_g", "mix_w_in", "ret_decay_fwd",
           "ret_decay_bwd", "mla_q_norm_g", "mla_w_uq", "mla_kv_norm_g", "mla_w_ukv", "mix_w_out", "norm3_g",
           "ffn2_w_in", "ffn2_w_out", "final_norm_g")


def _pack(parts):
    flat = jnp.concatenate([t.reshape(-1) for t in parts])
    pad = (-flat.shape[0]) % LANE
    return jnp.pad(flat, (0, pad)).reshape(1, -1)


def _unpack(flat, like):
    out, off = [], 0
    for t in like:
        out.append(flat[0, off:off + t.size].reshape(t.shape))
        off += t.size
    return out


def kernel(x, c, ctx, c_ctx, ada_w, ada_b, norm1_g, ffn1_w_in, ffn1_w_out, norm2_g, mix_w_in, ret_decay_fwd, ret_decay_bwd, mla_q_norm_g, mla_w_uq, mla_kv_norm_g, mla_w_ukv, mix_w_out, norm3_g, ffn2_w_in, ffn2_w_out, final_norm_g, loss_target, m_c_ctx, m_ada_w, m_ada_b, m_norm1_g, m_ffn1_w_in, m_ffn1_w_out, m_norm2_g, m_mix_w_in, m_ret_decay_fwd, m_ret_decay_bwd, m_mla_q_norm_g, m_mla_w_uq, m_mla_kv_norm_g, m_mla_w_ukv, m_mix_w_out, m_norm3_g, m_ffn2_w_in, m_ffn2_w_out, m_final_norm_g, v_c_ctx, v_ada_w, v_ada_b, v_norm1_g, v_ffn1_w_in, v_ffn1_w_out, v_norm2_g, v_mix_w_in, v_ret_decay_fwd, v_ret_decay_bwd, v_mla_q_norm_g, v_mla_w_uq, v_mla_kv_norm_g, v_mla_w_ukv, v_mix_w_out, v_norm3_g, v_ffn2_w_in, v_ffn2_w_out, v_final_norm_g):
    w = dict(c_ctx=c_ctx, ada_w=ada_w, ada_b=ada_b, norm1_g=norm1_g, ffn1_w_in=ffn1_w_in, ffn1_w_out=ffn1_w_out,
             norm2_g=norm2_g, mix_w_in=mix_w_in, ret_decay_fwd=ret_decay_fwd, ret_decay_bwd=ret_decay_bwd,
             mla_q_norm_g=mla_q_norm_g, mla_w_uq=mla_w_uq, mla_kv_norm_g=mla_kv_norm_g, mla_w_ukv=mla_w_ukv,
             mix_w_out=mix_w_out, norm3_g=norm3_g, ffn2_w_in=ffn2_w_in, ffn2_w_out=ffn2_w_out,
             final_norm_g=final_norm_g)
    mom = dict(c_ctx=m_c_ctx, ada_w=m_ada_w, ada_b=m_ada_b, norm1_g=m_norm1_g, ffn1_w_in=m_ffn1_w_in,
               ffn1_w_out=m_ffn1_w_out, norm2_g=m_norm2_g, mix_w_in=m_mix_w_in, ret_decay_fwd=m_ret_decay_fwd,
               ret_decay_bwd=m_ret_decay_bwd, mla_q_norm_g=m_mla_q_norm_g, mla_w_uq=m_mla_w_uq,
               mla_kv_norm_g=m_mla_kv_norm_g, mla_w_ukv=m_mla_w_ukv, mix_w_out=m_mix_w_out, norm3_g=m_norm3_g,
               ffn2_w_in=m_ffn2_w_in, ffn2_w_out=m_ffn2_w_out, final_norm_g=m_final_norm_g)
    var = dict(c_ctx=v_c_ctx, ada_w=v_ada_w, ada_b=v_ada_b, norm1_g=v_norm1_g, ffn1_w_in=v_ffn1_w_in,
               ffn1_w_out=v_ffn1_w_out, norm2_g=v_norm2_g, mix_w_in=v_mix_w_in, ret_decay_fwd=v_ret_decay_fwd,
               ret_decay_bwd=v_ret_decay_bwd, mla_q_norm_g=v_mla_q_norm_g, mla_w_uq=v_mla_w_uq,
               mla_kv_norm_g=v_mla_kv_norm_g, mla_w_ukv=v_mla_w_ukv, mix_w_out=v_mix_w_out, norm3_g=v_norm3_g,
               ffn2_w_in=v_ffn2_w_in, ffn2_w_out=v_ffn2_w_out, final_norm_g=v_final_norm_g)
    me = 4 * lax.axis_index("x") + 2 * lax.axis_index("y") + lax.axis_index("c")

    shard = {k: w[k][0].astype(BF16) for k in BIG}
    first = _gather_two_level([shard["ffn1_w_in"], jax.nn.silu(c)], "weights_gather")
    silu_c_all = first[-1][:, 0, :]

    def start_gather(names, after, name):
        return _exchange_start([shard[k] for k in names], [_own_slot(shard[k], me) for k in names], after, True, name)

    wout_start = start_gather(FIRST[1:], first[0], "ffn1_wout_start")
    mid_start = start_gather(MID, wout_start[4], "mixer_weights_start")
    last_start = start_gather(LAST, mid_start[4], "ffn2_weights_start")

    def wait_w_out(after):
        return _exchange_wait(*wout_start[:4], after, True, "ffn1_wout_wait")[0]

    pa = dict(ada_w=ada_w[0], c_ctx=c_ctx, ada_b=ada_b)
    (x1, c1, m_lat, m_ctx), res_a = _stage_a_fwd(pa, x[0], ctx[0], norm1_g, last_start[4], wait_w_out, silu_c_all, me)

    mid = _exchange_wait(mid_start[0], mid_start[1], mid_start[2], mid_start[3], x1, True, "mixer_weights_wait")
    pb = dict(zip(MID, mid))
    for k in ("norm2_g", "mla_q_norm_g", "mla_kv_norm_g", "ret_decay_fwd", "ret_decay_bwd"):
        pb[k] = w[k]
    x2, vjp_b = jax.vjp(_stage_b, pb, x1, c1, m_lat, m_ctx)

    last = _exchange_wait(last_start[0], last_start[1], last_start[2], last_start[3], x2, True, "ffn2_weights_wait")
    pc = dict(zip(LAST, last), norm3_g=norm3_g, final_norm_g=final_norm_g[None, :])
    loss_local, vjp_c = jax.vjp(lambda q, t, m: _stage_c(q, t, m, loss_target[0]), pc, x2, m_lat)

    gc, dx2, dm_c = vjp_c(jnp.ones((), F32))
    last_scat = _exchange_start([gc[k] for k in LAST],
                                [_own_slot(lax.dynamic_index_in_dim(gc[k], me, 0, False), me) for k in LAST],
                                dx2, False, "ffn2_grads_start")
    gb, dx1, dc1, dm_b, dmc_b = vjp_b(last_scat[4])
    mid_scat = _exchange_start([gb[k] for k in MID],
                               [_own_slot(lax.dynamic_index_in_dim(gb[k], me, 0, False), me) for k in MID],
                               dx1, False, "mixer_grads_start")
    first_scat = {}

    def start_grads(name, dw, after):
        first_scat[name] = _exchange_start(
            [dw], [_own_slot(lax.dynamic_index_in_dim(dw, me, 0, False), me)], after, False, name + "_grads_start")
        return first_scat[name][4]

    g_ada, dx, dng1 = _stage_a_bwd(res_a, mid_scat[4], dc1, dm_b + dm_c, dmc_b, start_grads)
    grads = {**g_ada, **gb, **gc, "x": dx, "norm1_g": dng1}
    grads["final_norm_g"] = grads["final_norm_g"][0]

    exchanged = {k: _exchange_wait(*first_scat[k][:4], dx, False, k + "_grads_wait")[0] for k in FIRST}
    exchanged.update(zip(LAST, _exchange_wait(*last_scat[:4], dx, False, "ffn2_grads_wait")))
    exchanged.update(zip(MID, _exchange_wait(*mid_scat[:4], dx, False, "mixer_grads_wait")))
    zero1 = [jnp.zeros((1,), F32)]
    small_like = zero1 + [w[k] for k in SMALL]
    small_all = _exchange([_pack([loss_local.reshape(1)] + [grads[k] for k in SMALL])], True, "small_grads_gather")[0]
    loss = jnp.sum(small_all[:, 0, 0])

    out_g, out_d, out_m, out_v = {}, {}, {}, {}

    def update(name, gstack, shape2d):
        res = _adamw(gstack, w[name].reshape(shape2d), mom[name].reshape(shape2d), var[name].reshape(shape2d),
                     "adamw_" + name)
        out_g[name], out_d[name], out_m[name], out_v[name] = [t.reshape(w[name].shape) for t in res]

    for k in BIG:
        update(k, exchanged[k], exchanged[k].shape[1:])
    update("ada_w", grads["ada_w"][None], ada_w.shape[1:])
    res = _adamw(small_all, _pack(small_like), _pack(zero1 + [mom[k] for k in SMALL]),
                 _pack(zero1 + [var[k] for k in SMALL]), "adamw_small")
    for dst, flat in zip((out_g, out_d, out_m, out_v), res):
        for k, t in zip(SMALL, _unpack(flat, small_like)[1:]):
            dst[k] = t

    return (loss, grads["x"][None], *[out_g[k] for k in WEIGHTS], *[out_d[k] for k in WEIGHTS],
            *[out_m[k] for k in WEIGHTS], *[out_v[k] for k in WEIGHTS])
```

```python
import functools

import jax
import jax.numpy as jnp
from jax import lax
from jax.experimental import pallas as pl
from jax.experimental.pallas import tpu as pltpu

F32 = jnp.float32
BF16 = jnp.bfloat16

N_DEV = 8
MESH_AXES = ("x", "y", "c")

GRID_W = 64
N_MOD = 9
RET_HEADS = 8
RET_DK = 64
RET_DV = 128
RET_CHUNK = 256
RET_ROPE_BASE = 10000.0
MLA_HEADS = 8
MLA_Q_RANK = 512
MLA_KV_RANK = 256
MLA_NOPE = 128
MLA_ROPE = 64
MLA_V = 128
AXIAL_BASE = 10000.0
RMS_EPS = 1e-6
GN_EPS = 1e-5
SPLITS = (RET_HEADS * RET_DK, RET_HEADS * RET_DK, RET_HEADS * RET_DV, RET_HEADS * RET_DV,
          MLA_Q_RANK, MLA_KV_RANK, MLA_ROPE)
MIX_IN = sum(SPLITS)
MIX_IN_PAD = 4096

ADAM_LR = 0.001
ADAM_B1 = 0.9
ADAM_B2 = 0.999
ADAM_EPS = 1e-08
ADAM_WD = 0.01
ADAM_STEP = 10

LANE = 128
RET_DKP = LANE
VMEM_LIMIT_BYTES = 56 * 1024 * 1024

NN = ((1,), (0,))
NT = ((1,), (1,))
TN = ((0,), (0,))


def _pick(dim, target, align=LANE):
    t = min(dim, target)
    t -= t % align
    while t >= align:
        if dim % t == 0:
            return t
        t -= align
    return dim


def _params():
    return pltpu.CompilerParams(vmem_limit_bytes=VMEM_LIMIT_BYTES)


def _dot(a, b, dims):
    return lax.dot_general(a.astype(BF16), b.astype(BF16), (dims, ((), ())), preferred_element_type=F32)


def _mm_call(name, grid, ins, pairs, outs, acc_shapes, epilogue):
    n_in, n_out = len(ins), len(outs)
    k_axis = len(grid) - 1
    k_steps = grid[k_axis]

    def body(*refs):
        in_refs = refs[:n_in]
        out_refs = refs[n_in:n_in + n_out]
        accs = refs[n_in + n_out:]
        k = pl.program_id(k_axis)

        @pl.when(k == 0)
        def _():
            for acc in accs:
                acc[...] = jnp.zeros_like(acc)

        for ai, bi, dims, ci in pairs:
            accs[ci][...] += _dot(in_refs[ai][...], in_refs[bi][...], dims)

        @pl.when(k == k_steps - 1)
        def _():
            epilogue([acc[...] for acc in accs], in_refs, out_refs)

    res = pl.pallas_call(
        body, name=name, grid=grid,
        in_specs=[s for _, s in ins], out_specs=[s for _, s in outs],
        out_shape=[s for s, _ in outs],
        scratch_shapes=[pltpu.VMEM(s, F32) for s in acc_shapes],
        compiler_params=_params(),
    )(*[a for a, _ in ins])
    return res


def _matmul(a, b, mode, out_dtype, name, tm=1024, tn=1024, tk=512):
    if mode == "nn":
        (m, kd), n = a.shape, b.shape[1]
    elif mode == "nt":
        (m, kd), n = a.shape, b.shape[0]
    else:
        (kd, m), n = a.shape, b.shape[1]
    tm, tn = _pick(m, tm, 16), _pick(n, tn)
    tk = _pick(kd, tk) if mode != "tn" else _pick(kd, tk, 16)
    if mode == "nn":
        a_spec = pl.BlockSpec((tm, tk), lambda i, j, k: (i, k))
        b_spec = pl.BlockSpec((tk, tn), lambda i, j, k: (k, j))
        dims = NN
    elif mode == "nt":
        a_spec = pl.BlockSpec((tm, tk), lambda i, j, k: (i, k))
        b_spec = pl.BlockSpec((tn, tk), lambda i, j, k: (j, k))
        dims = NT
    else:
        a_spec = pl.BlockSpec((tk, tm), lambda i, j, k: (k, i))
        b_spec = pl.BlockSpec((tk, tn), lambda i, j, k: (k, j))
        dims = TN

    def epilogue(accs, in_refs, out_refs):
        out_refs[0][...] = accs[0].astype(out_dtype)

    return _mm_call(
        name, (m // tm, n // tn, kd // tk), [(a, a_spec), (b, b_spec)], [(0, 1, dims, 0)],
        [(jax.ShapeDtypeStruct((m, n), out_dtype), pl.BlockSpec((tm, tn), lambda i, j, k: (i, j)))],
        [(tm, tn)], epilogue)[0]


def _norm_mod_tile(x, ng, sc, sh):
    r = lax.rsqrt(jnp.mean(x * x, axis=-1, keepdims=True) + RMS_EPS)
    return (x * r * ng) * (1.0 + sc) + sh


def _row_spec(tm, d):
    return pl.BlockSpec((tm, d), lambda i: (i, 0))


def _vec_spec(d):
    return pl.BlockSpec((1, d), lambda i: (0, 0))


def _norm_mod_fwd(x, ng, sc, sh, name):
    t, d = x.shape
    tm = _pick(t, 512, 16)

    def body(x_ref, ng_ref, sc_ref, sh_ref, h_ref):
        h_ref[...] = _norm_mod_tile(x_ref[...], ng_ref[...], sc_ref[...], sh_ref[...]).astype(BF16)

    return pl.pallas_call(
        body, name=name, grid=(t // tm,),
        in_specs=[_row_spec(tm, d), _vec_spec(d), _vec_spec(d), _vec_spec(d)],
        out_specs=_row_spec(tm, d), out_shape=jax.ShapeDtypeStruct((t, d), BF16),
        compiler_params=_params(),
    )(x, ng, sc, sh)


def _norm_mod_bwd(x, ng, sc, sh, dh, dres, name):
    t, d = x.shape
    tm = _pick(t, 256, 16)
    has_res = dres is not None

    def body(*refs):
        if has_res:
            x_ref, ng_ref, sc_ref, sh_ref, dh_ref, dres_ref, dx_ref, dng_ref, dsc_ref, dsh_ref = refs
        else:
            x_ref, ng_ref, sc_ref, sh_ref, dh_ref, dx_ref, dng_ref, dsc_ref, dsh_ref = refs
        _, vjp = jax.vjp(_norm_mod_tile, x_ref[...], ng_ref[...], sc_ref[...], sh_ref[...])
        dx, dng, dsc, dsh = vjp(dh_ref[...].astype(F32))
        if has_res:
            dx = dx + dres_ref[...]
        dx_ref[...] = dx

        @pl.when(pl.program_id(0) == 0)
        def _():
            dng_ref[...] = jnp.zeros_like(dng_ref)
            dsc_ref[...] = jnp.zeros_like(dsc_ref)
            dsh_ref[...] = jnp.zeros_like(dsh_ref)

        dng_ref[...] += dng
        dsc_ref[...] += dsc
        dsh_ref[...] += dsh

    ins = [x, ng, sc, sh, dh] + ([dres] if has_res else [])
    in_specs = [_row_spec(tm, d), _vec_spec(d), _vec_spec(d), _vec_spec(d), _row_spec(tm, d)]
    in_specs += [_row_spec(tm, d)] if has_res else []
    vec = jax.ShapeDtypeStruct((1, d), F32)
    return pl.pallas_call(
        body, name=name, grid=(t // tm,), in_specs=in_specs,
        out_specs=[_row_spec(tm, d), _vec_spec(d), _vec_spec(d), _vec_spec(d)],
        out_shape=[jax.ShapeDtypeStruct((t, d), F32), vec, vec, vec],
        compiler_params=_params(),
    )(*ins)


def _res_mm_fwd(a, w, x, gate, coef, name):
    t, kd = a.shape
    d = w.shape[1]
    tm, tn, tk = _pick(t, 1024, 16), _pick(d, 1024), _pick(kd, 2816)

    def epilogue(accs, in_refs, out_refs):
        f = accs[0]
        out_refs[0][...] = in_refs[2][...] + (coef * in_refs[3][...]) * f
        out_refs[1][...] = f.astype(BF16)

    tile = pl.BlockSpec((tm, tn), lambda i, j, k: (i, j))
    return _mm_call(
        name, (t // tm, d // tn, kd // tk),
        [(a, pl.BlockSpec((tm, tk), lambda i, j, k: (i, k))), (w, pl.BlockSpec((tk, tn), lambda i, j, k: (k, j))),
         (x, tile), (gate, pl.BlockSpec((1, tn), lambda i, j, k: (0, j)))],
        [(0, 1, NN, 0)],
        [(jax.ShapeDtypeStruct((t, d), F32), tile), (jax.ShapeDtypeStruct((t, d), BF16), tile)],
        [(tm, tn)], epilogue)


def _gate_bwd(dxo, f, gate, coef, name):
    t, d = dxo.shape
    tm = _pick(t, 512, 16)

    def body(dxo_ref, f_ref, gate_ref, df_ref, dgate_ref):
        dxo_t = dxo_ref[...]
        df_ref[...] = ((coef * gate_ref[...]) * dxo_t).astype(BF16)

        @pl.when(pl.program_id(0) == 0)
        def _():
            dgate_ref[...] = jnp.zeros_like(dgate_ref)

        dgate_ref[...] += coef * jnp.sum(dxo_t * f_ref[...].astype(F32), axis=0, keepdims=True)

    return pl.pallas_call(
        body, name=name, grid=(t // tm,),
        in_specs=[_row_spec(tm, d), _row_spec(tm, d), _vec_spec(d)],
        out_specs=[_row_spec(tm, d), _vec_spec(d)],
        out_shape=[jax.ShapeDtypeStruct((t, d), BF16), jax.ShapeDtypeStruct((1, d), F32)],
        compiler_params=_params(),
    )(dxo, f, gate)


def _ffn_in_fwd(h, w_in, name):
    t, d = h.shape
    n = w_in.shape[2]
    half = N_DEV // 2
    f = half * n
    tm = _pick(t, 512, 16)

    def epilogue(accs, in_refs, out_refs):
        g, u = accs
        s = jax.nn.sigmoid(g)
        silu = g * s
        out_refs[0][...] = (silu * u).astype(BF16)
        out_refs[1][0] = (u * (s * (1.0 + g * (1.0 - s)))).astype(BF16)
        out_refs[1][1] = silu.astype(BF16)

    return _mm_call(
        name, (half, t // tm, 1),
        [(h, pl.BlockSpec((tm, d), lambda j, i, k: (i, 0))),
         (w_in, pl.BlockSpec((None, d, n), lambda j, i, k: (j, 0, 0))),
         (w_in, pl.BlockSpec((None, d, n), lambda j, i, k: (j + half, 0, 0)))],
        [(0, 1, NN, 0), (0, 2, NN, 1)],
        [(jax.ShapeDtypeStruct((t, f), BF16), pl.BlockSpec((tm, n), lambda j, i, k: (i, j))),
         (jax.ShapeDtypeStruct((2, t, f), BF16), pl.BlockSpec((2, tm, n), lambda j, i, k: (0, i, j)))],
        [(tm, n), (tm, n)], epilogue)


def _ffn_da_bwd(df, w_out2d, gu, name):
    t, d = df.shape
    f = w_out2d.shape[0]
    half = N_DEV // 2
    n = f // half
    tm = _pick(t, 512, 16)
    step = 4 * LANE
    chunks = [(c, min(c + step, n)) for c in range(0, n, step)]

    def body(df_ref, w_ref, gu_ref, o_ref):
        df_t = df_ref[...]
        for c0, c1 in chunks:
            da = _dot(df_t, w_ref[c0:c1, :], NT)
            o_ref[0, :, c0:c1] = (da * gu_ref[0, :, c0:c1].astype(F32)).astype(BF16)
            o_ref[1, :, c0:c1] = (da * gu_ref[1, :, c0:c1].astype(F32)).astype(BF16)

    gu_spec = pl.BlockSpec((2, tm, n), lambda j, i: (0, i, j))
    return pl.pallas_call(
        body, name=name, grid=(half, t // tm),
        in_specs=[pl.BlockSpec((tm, d), lambda j, i: (i, 0)), pl.BlockSpec((n, d), lambda j, i: (j, 0)), gu_spec],
        out_specs=gu_spec, out_shape=jax.ShapeDtypeStruct((2, t, f), BF16), compiler_params=_params(),
    )(df, w_out2d, gu)


def _ffn_dh_bwd(dgu, w_in, name):
    _, t, f = dgu.shape
    d, n = w_in.shape[1], w_in.shape[2]
    half = N_DEV // 2
    tm = _pick(t, 512, 16)

    def epilogue(accs, in_refs, out_refs):
        out_refs[0][...] = accs[0]

    return _mm_call(
        name, (t // tm, 1, half),
        [(dgu, pl.BlockSpec((None, tm, n), lambda i, j, k: (0, i, k))),
         (dgu, pl.BlockSpec((None, tm, n), lambda i, j, k: (1, i, k))),
         (w_in, pl.BlockSpec((None, d, n), lambda i, j, k: (k, 0, 0))),
         (w_in, pl.BlockSpec((None, d, n), lambda i, j, k: (k + half, 0, 0)))],
        [(0, 2, NT, 0), (1, 3, NT, 0)],
        [(jax.ShapeDtypeStruct((t, d), F32), pl.BlockSpec((tm, d), lambda i, j, k: (i, 0)))],
        [(tm, d)], epilogue)[0]


def _ffn_dwin_bwd(h, dgu, name):
    t, d = h.shape
    f = dgu.shape[2]
    half = N_DEV // 2
    n = f // half
    tk = _pick(t, 1024, 16)

    def epilogue(accs, in_refs, out_refs):
        out_refs[0][...] = accs[0].astype(BF16)

    return _mm_call(
        name, (N_DEV, 1, t // tk),
        [(h, pl.BlockSpec((tk, d), lambda j, i, k: (k, 0))),
         (dgu, pl.BlockSpec((None, tk, n), lambda j, i, k: (j // half, k, j % half)))],
        [(0, 1, TN, 0)],
        [(jax.ShapeDtypeStruct((N_DEV, d, n), BF16), pl.BlockSpec((None, d, n), lambda j, i, k: (j, 0, 0)))],
        [(d, n)], epilogue)[0]


def _ffn_parts(tag):
    def w2d(w_out):
        return w_out.reshape(w_out.shape[0] * w_out.shape[1], w_out.shape[2])

    def fwd_in(x, ng, sh, sc, w_in):
        h = _norm_mod_fwd(x, ng, sc, sh, tag + "_norm")
        a, gu = _ffn_in_fwd(h, w_in, tag + "_in")
        return h, a, gu

    def fwd_out(a, w_out, x, gate):
        return _res_mm_fwd(a, w2d(w_out), x, gate, 0.5, tag + "_out")

    def bwd_out(dxo, f1, gate, w_out, gu, a):
        df, dgate = _gate_bwd(dxo, f1, gate, 0.5, tag + "_dgate")
        dgu = _ffn_da_bwd(df, w2d(w_out), gu, tag + "_da")
        f = w_out.shape[0] * w_out.shape[1]
        dw_out = _matmul(a, df, "tn", BF16, tag + "_dwout", tm=_pick(f, 1408, 16), tn=2048, tk=1024)
        return dgate, dgu, dw_out.reshape(w_out.shape)

    def bwd_in(x, ng, sh, sc, w_in, h, dgu, dxo):
        dh = _ffn_dh_bwd(dgu, w_in, tag + "_dh")
        dw_in = _ffn_dwin_bwd(h, dgu, tag + "_dwin")
        dx, dng, dsc, dsh = _norm_mod_bwd(x, ng, sc, sh, dh, dxo, tag + "_dnorm")
        return dx, dng, dsh, dsc, dw_in

    return fwd_in, fwd_out, bwd_out, bwd_in


def _make_ffn_block(tag):
    fwd_in, fwd_out, bwd_out, bwd_in = _ffn_parts(tag)

    @jax.custom_vjp
    def ffn_block(x, ng, sh, sc, gate, w_in, w_out):
        return fwd(x, ng, sh, sc, gate, w_in, w_out)[0]

    def fwd(x, ng, sh, sc, gate, w_in, w_out):
        h, a, gu = fwd_in(x, ng, sh, sc, w_in)
        xo, f1 = fwd_out(a, w_out, x, gate)
        return xo, (x, ng, sh, sc, gate, w_in, w_out, h, a, gu, f1)

    def bwd(res, dxo):
        x, ng, sh, sc, gate, w_in, w_out, h, a, gu, f1 = res
        dgate, dgu, dw_out = bwd_out(dxo, f1, gate, w_out, gu, a)
        dx, dng, dsh, dsc, dw_in = bwd_in(x, ng, sh, sc, w_in, h, dgu, dxo)
        return dx, dng, dsh, dsc, dgate, dw_in, dw_out

    ffn_block.defvjp(fwd, bwd)
    return ffn_block


def _make_norm_proj(tag):
    @jax.custom_vjp
    def norm_proj(x, ng, sh, sc, w):
        return fwd(x, ng, sh, sc, w)[0]

    def fwd(x, ng, sh, sc, w):
        h = _norm_mod_fwd(x, ng, sc, sh, tag + "_norm")
        p = _matmul(h, w, "nn", F32, tag + "_mm", tm=1024, tn=1024, tk=w.shape[0])
        return p, (x, ng, sh, sc, w, h)

    def bwd(res, dp):
        x, ng, sh, sc, w, h = res
        dh = _matmul(dp, w, "nt", F32, tag + "_dh", tm=512, tn=w.shape[0], tk=2048)
        dw = _matmul(h, dp, "tn", BF16, tag + "_dw", tm=w.shape[0], tn=1024, tk=1024)
        dx, dng, dsc, dsh = _norm_mod_bwd(x, ng, sc, sh, dh, None, tag + "_dnorm")
        return dx, dng, dsh, dsc, dw

    norm_proj.defvjp(fwd, bwd)
    return norm_proj


def _make_norm_proj_carry(tag):
    @jax.custom_vjp
    def norm_proj(x, ng, sh, sc, w):
        return fwd(x, ng, sh, sc, w)[0]

    def fwd(x, ng, sh, sc, w):
        h = _norm_mod_fwd(x, ng, sc, sh, tag + "_norm")
        p = _matmul(h, w, "nn", F32, tag + "_mm", tm=1024, tn=1024, tk=w.shape[0])
        return (p, x), (x, ng, sh, sc, w, h)

    def bwd(res, cts):
        x, ng, sh, sc, w, h = res
        dp, dx_carry = cts
        dh = _matmul(dp, w, "nt", F32, tag + "_dh", tm=512, tn=w.shape[0], tk=2048)
        dw = _matmul(h, dp, "tn", BF16, tag + "_dw", tm=w.shape[0], tn=1024, tk=1024)
        dx, dng, dsc, dsh = _norm_mod_bwd(x, ng, sc, sh, dh, dx_carry, tag + "_dnorm")
        return dx, dng, dsh, dsc, dw

    norm_proj.defvjp(fwd, bwd)
    return norm_proj


def _make_split(tag, widths, total):
    offs = [sum(widths[:i]) for i in range(len(widths))]

    def concat_call(pieces):
        t = pieces[0].shape[0]
        tm = _pick(t, 256, 16)

        def body(*refs):
            o_ref = refs[-1]
            for ref, off, wd in zip(refs[:-1], offs, widths):
                o_ref[:, off:off + wd] = ref[...]
            end = offs[-1] + widths[-1]
            if end < total:
                o_ref[:, end:] = jnp.zeros((tm, total - end), F32)

        return pl.pallas_call(
            body, name=tag + "_concat", grid=(t // tm,),
            in_specs=[_row_spec(tm, wd) for wd in widths], out_specs=_row_spec(tm, total),
            out_shape=jax.ShapeDtypeStruct((t, total), F32), compiler_params=_params(),
        )(*pieces)

    @jax.custom_vjp
    def split(p):
        return tuple(p[:, off:off + wd] for off, wd in zip(offs, widths))

    def fwd(p):
        return split(p), None

    def bwd(_, cts):
        return (concat_call(list(cts)),)

    split.defvjp(fwd, bwd)
    return split


def _make_res_proj(tag):
    @jax.custom_vjp
    def res_proj(a, w, x, gate):
        return fwd(a, w, x, gate)[0]

    def fwd(a, w, x, gate):
        xo, f = _res_mm_fwd(a, w, x, gate, 1.0, tag + "_mm")
        return xo, (a, w, gate, f)

    def bwd(res, dxo):
        a, w, gate, f = res
        df, dgate = _gate_bwd(dxo, f, gate, 1.0, tag + "_dgate")
        da = _matmul(df, w, "nt", BF16, tag + "_da", tm=1024, tn=1024, tk=2048)
        dw = _matmul(a, df, "tn", BF16, tag + "_dw", tm=1024, tn=2048, tk=1024)
        return da, dw, dxo, dgate

    res_proj.defvjp(fwd, bwd)
    return res_proj


def _make_small_mm(tag):
    @jax.custom_vjp
    def small_mm(a, w):
        return _matmul(a, w, "nn", F32, tag + "_mm", tm=a.shape[0], tn=768, tk=w.shape[0])

    def fwd(a, w):
        return small_mm(a, w), (a, w)

    def bwd(res, dr):
        a, w = res
        da = _matmul(dr, w, "nt", F32, tag + "_da", tm=a.shape[0], tn=w.shape[0], tk=768)
        dw = _matmul(a, dr, "tn", F32, tag + "_dw", tm=1024, tn=768, tk=a.shape[0])
        return da, dw

    small_mm.defvjp(fwd, bwd)
    return small_mm


def _ret_chunk_terms(lg, c, reverse):
    row = lax.broadcasted_iota(jnp.int32, (c, c), 0).astype(F32)
    col = lax.broadcasted_iota(jnp.int32, (c, c), 1).astype(F32)
    pos = lax.broadcasted_iota(jnp.int32, (c, 1), 0).astype(F32)
    if reverse:
        diff = col - row
        mask = diff > 0.0
        e_exp = float(c) - pos
        f_exp = pos
    else:
        diff = row - col
        mask = diff >= 0.0
        e_exp = pos + 1.0
        f_exp = float(c - 1) - pos
    diffm = jnp.where(mask, diff, 0.0)
    dm = jnp.where(mask, jnp.exp(lg * diffm), 0.0)
    return diffm, dm, e_exp, jnp.exp(lg * e_exp), f_exp, jnp.exp(lg * f_exp)


def _ret_mask_t(lg, c, reverse):
    row = lax.broadcasted_iota(jnp.int32, (c, c), 0).astype(F32)
    col = lax.broadcasted_iota(jnp.int32, (c, c), 1).astype(F32)
    diff = row - col if reverse else col - row
    mask = diff > 0.0 if reverse else diff >= 0.0
    return jnp.where(mask, jnp.exp(lg * jnp.where(mask, diff, 0.0)), 0.0)


def _lane0(val):
    lane = lax.broadcasted_iota(jnp.int32, (1, LANE), 1)
    return jnp.where(lane == 0, val, 0.0)


RET_HEAD_BLOCK = 4


def _make_ret_dir(tag, reverse):
    hb = RET_HEAD_BLOCK

    def heads_spec(nc, width, flip):
        if flip:
            return pl.BlockSpec((hb, RET_CHUNK, width), lambda h, t: (h, nc - 1 - t, 0))
        return pl.BlockSpec((hb, RET_CHUNK, width), lambda h, t: (h, t, 0))

    def state_spec(nc, flip):
        if flip:
            return pl.BlockSpec((hb, None, RET_DKP, RET_DV), lambda h, t: (h, nc - 1 - t, 0, 0))
        return pl.BlockSpec((hb, None, RET_DKP, RET_DV), lambda h, t: (h, t, 0, 0))

    lg_spec = pl.BlockSpec((hb, 1, LANE), lambda h, t: (h, 0, 0))
    s0_spec = pl.BlockSpec((hb, RET_DKP, RET_DV), lambda h, t: (h, 0, 0))

    def fwd_call(q, k, v, lgb, s0):
        hh, ll, _ = q.shape
        c = RET_CHUNK
        nc = ll // c

        def body(q_ref, k_ref, v_ref, lg_ref, s0_ref, y_ref, sall_ref, s_scr):
            @pl.when(pl.program_id(1) == 0)
            def _():
                s_scr[...] = s0_ref[...]

            for b in range(hb):
                lg = lg_ref[b][:, :1]
                _, dm, _, xi, _, zeta = _ret_chunk_terms(lg, c, reverse)
                q_t, k_t, v_t = q_ref[b], k_ref[b], v_ref[b]
                s = s_scr[b]
                p = _dot(q_t, k_t, NT) * dm
                y_ref[b] = _dot(p, v_t, NN) + _dot(q_t * xi, s, NN)
                sall_ref[b] = s
                s_scr[b] = jnp.exp(lg * float(c)) * s + _dot(k_t * zeta, v_t, TN)

        return pl.pallas_call(
            body, name=tag + "_fwd", grid=(hh // hb, nc),
            in_specs=[heads_spec(nc, RET_DKP,reverse), heads_spec(nc, RET_DKP,reverse),
                      heads_spec(nc, RET_DV, reverse), lg_spec, s0_spec],
            out_specs=[heads_spec(nc, RET_DV, reverse), state_spec(nc, reverse)],
            out_shape=[jax.ShapeDtypeStruct((hh, ll, RET_DV), F32),
                       jax.ShapeDtypeStruct((hh, nc, RET_DKP, RET_DV), F32)],
            scratch_shapes=[pltpu.VMEM((hb, RET_DKP, RET_DV), F32)],
            compiler_params=_params(),
        )(q, k, v, lgb, s0)

    def bwd_call(q, k, v, lgb, sall, dy):
        hh, ll, _ = q.shape
        c = RET_CHUNK
        nc = ll // c
        flip = not reverse

        def body(q_ref, k_ref, v_ref, lg_ref, sall_ref, dy_ref, dq_ref, dk_ref, dv_ref, dlg_ref, ds0_ref, ds_scr):
            @pl.when(pl.program_id(1) == 0)
            def _():
                ds_scr[...] = jnp.zeros_like(ds_scr)
                dlg_ref[...] = jnp.zeros_like(dlg_ref)

            def total(m):
                return jnp.sum(jnp.sum(m, axis=1, keepdims=True), axis=0, keepdims=True)

            for b in range(hb):
                lg = lg_ref[b][:, :1]
                diffm, dm, e_exp, xi, f_exp, zeta = _ret_chunk_terms(lg, c, reverse)
                q_t, k_t, v_t, dy_t = q_ref[b], k_ref[b], v_ref[b], dy_ref[b]
                s = sall_ref[b]
                dsn = ds_scr[b]
                a = _dot(q_t, k_t, NT)
                da = _dot(dy_t, v_t, NT) * dm
                dm_t = _ret_mask_t(lg, c, reverse)
                a_t = _dot(k_t, q_t, NT)
                da_t = _dot(v_t, dy_t, NT) * dm_t
                g = _dot(dy_t, s, NT)
                hm = _dot(v_t, dsn, NT)
                dq_ref[b] = _dot(da, k_t, NN) + xi * g
                dk_ref[b] = _dot(da_t, q_t, NN) + zeta * hm
                dv_ref[b] = _dot(a_t * dm_t, dy_t, NN) + _dot(k_t * zeta, dsn, NN)
                gc = jnp.exp(lg * float(c))
                ds_scr[b] = gc * dsn + _dot(q_t * xi, dy_t, TN)
                dl = (total(da * a * diffm) + total(e_exp * xi * q_t * g)
                      + float(c) * gc * total(s * dsn) + total(f_exp * zeta * k_t * hm))
                dlg_ref[b] += _lane0(dl)

            @pl.when(pl.program_id(1) == nc - 1)
            def _():
                ds0_ref[...] = ds_scr[...]

        return pl.pallas_call(
            body, name=tag + "_bwd", grid=(hh // hb, nc),
            in_specs=[heads_spec(nc, RET_DKP,flip), heads_spec(nc, RET_DKP,flip), heads_spec(nc, RET_DV, flip),
                      lg_spec, state_spec(nc, flip), heads_spec(nc, RET_DV, flip)],
            out_specs=[heads_spec(nc, RET_DKP,flip), heads_spec(nc, RET_DKP,flip), heads_spec(nc, RET_DV, flip),
                       lg_spec, s0_spec],
            out_shape=[jax.ShapeDtypeStruct((hh, ll, RET_DKP), F32), jax.ShapeDtypeStruct((hh, ll, RET_DKP), F32),
                       jax.ShapeDtypeStruct((hh, ll, RET_DV), F32), jax.ShapeDtypeStruct((hh, 1, LANE), F32),
                       jax.ShapeDtypeStruct((hh, RET_DKP, RET_DV), F32)],
            scratch_shapes=[pltpu.VMEM((hb, RET_DKP, RET_DV), F32)],
            compiler_params=_params(),
        )(q, k, v, lgb, sall, dy)

    @jax.custom_vjp
    def ret_dir(q, k, v, lgb, s0):
        return fwd_call(q, k, v, lgb, s0)[0]

    def fwd(q, k, v, lgb, s0):
        y, sall = fwd_call(q, k, v, lgb, s0)
        return y, (q, k, v, lgb, sall)

    def bwd(res, dy):
        q, k, v, lgb, sall = res
        return tuple(bwd_call(q, k, v, lgb, sall, dy))

    ret_dir.defvjp(fwd, bwd)
    return ret_dir


def _make_ctx_state(tag, reverse):
    hb = RET_HEAD_BLOCK
    c = RET_CHUNK
    k_spec = pl.BlockSpec((hb, c, RET_DKP), lambda h: (h, 0, 0))
    v_spec = pl.BlockSpec((hb, c, RET_DV), lambda h: (h, 0, 0))
    lg_spec = pl.BlockSpec((hb, 1, LANE), lambda h: (h, 0, 0))
    s_spec = pl.BlockSpec((hb, RET_DKP, RET_DV), lambda h: (h, 0, 0))

    def fwd_call(k, v, lgb):
        hh = k.shape[0]

        def body(k_ref, v_ref, lg_ref, s_ref):
            for b in range(hb):
                _, _, _, _, _, zeta = _ret_chunk_terms(lg_ref[b][:, :1], c, reverse)
                s_ref[b] = _dot(k_ref[b] * zeta, v_ref[b], TN)

        return pl.pallas_call(
            body, name=tag + "_fwd", grid=(hh // hb,), in_specs=[k_spec, v_spec, lg_spec], out_specs=s_spec,
            out_shape=jax.ShapeDtypeStruct((hh, RET_DKP, RET_DV), F32), compiler_params=_params(),
        )(k, v, lgb)

    def bwd_call(k, v, lgb, ds):
        hh = k.shape[0]

        def body(k_ref, v_ref, lg_ref, ds_ref, dk_ref, dv_ref, dlg_ref):
            for b in range(hb):
                _, _, _, _, f_exp, zeta = _ret_chunk_terms(lg_ref[b][:, :1], c, reverse)
                k_t, v_t, ds = k_ref[b], v_ref[b], ds_ref[b]
                hm = _dot(v_t, ds, NT)
                dk_ref[b] = zeta * hm
                dv_ref[b] = _dot(k_t * zeta, ds, NN)
                tot = jnp.sum(jnp.sum(f_exp * zeta * k_t * hm, axis=1, keepdims=True), axis=0, keepdims=True)
                dlg_ref[b] = _lane0(tot)

        return pl.pallas_call(
            body, name=tag + "_bwd", grid=(hh // hb,), in_specs=[k_spec, v_spec, lg_spec, s_spec],
            out_specs=[k_spec, v_spec, lg_spec],
            out_shape=[jax.ShapeDtypeStruct(k.shape, F32), jax.ShapeDtypeStruct(v.shape, F32),
                       jax.ShapeDtypeStruct((hh, 1, LANE), F32)],
            compiler_params=_params(),
        )(k, v, lgb, ds)

    @jax.custom_vjp
    def ctx_state(k, v, lgb):
        return fwd_call(k, v, lgb)

    def fwd(k, v, lgb):
        return fwd_call(k, v, lgb), (k, v, lgb)

    def bwd(res, ds):
        return tuple(bwd_call(*res, ds))

    ctx_state.defvjp(fwd, bwd)
    return ctx_state


def _rope_tables_call(name, n, inv, shift, axial):
    tm = _pick(n, 1024, 8)
    inv_lane = jnp.tile(inv, LANE // inv.shape[0])[None, :]

    def body(inv_ref, cos_ref, s1_ref, s2_ref):
        t = lax.broadcasted_iota(jnp.int32, (tm, LANE), 0) + pl.program_id(0) * tm
        lane = lax.broadcasted_iota(jnp.int32, (tm, LANE), 1)
        if axial:
            pos = jnp.where(lane % (2 * MLA_ROPE // 2) < MLA_ROPE // 2, t // GRID_W, t % GRID_W)
        else:
            pos = t
        ang = pos.astype(F32) * inv_ref[...]
        sin = jnp.sin(ang)
        first = lane % (2 * shift) < shift
        cos_ref[...] = jnp.cos(ang)
        s1_ref[...] = jnp.where(first, -sin, 0.0)
        s2_ref[...] = jnp.where(first, 0.0, sin)

    tab = jax.ShapeDtypeStruct((n, LANE), F32)
    return tuple(pl.pallas_call(
        body, name=name, grid=(n // tm,), in_specs=[_vec_spec(LANE)], out_specs=[_row_spec(tm, LANE)] * 3,
        out_shape=[tab, tab, tab], compiler_params=_params(),
    )(inv_lane))


def _ret_tables(n_lat):
    inv = RET_ROPE_BASE ** (-jnp.arange(0, RET_DK, 2, dtype=F32) / RET_DK)
    return _rope_tables_call("ret_tables", n_lat, inv, RET_DK // 2, False)


def _make_ret_pack(tag, n_lat, n_ctx):
    hh = RET_HEADS
    tm = MLA_PACK_ROWS
    k_scale = RET_DK ** -0.5
    shift = RET_DK // 2
    tabs = _ret_tables(n_lat)

    def low_lanes():
        return lax.broadcasted_iota(jnp.int32, (1, LANE), 1) < RET_DK

    def rows(width):
        return pl.BlockSpec((tm, width), lambda i: (i, 0))

    def heads(width):
        return pl.BlockSpec((hh, tm, width), lambda i: (0, i, 0))

    def split_pairs(src_ref, dst_ref, scale, rope):
        keep = low_lanes()
        for j in range(hh // 2):
            blk = src_ref[:, LANE * j:LANE * (j + 1)]
            if scale != 1.0:
                blk = blk * scale
            if rope is not None:
                blk = _rope128(blk, *rope, shift=shift)
            dst_ref[2 * j] = jnp.where(keep, blk, 0.0)
            dst_ref[2 * j + 1] = jnp.where(keep, pltpu.roll(blk, RET_DK, 1), 0.0)

    def merge_pairs(src_refs, dst_ref, scale, rope):
        keep = low_lanes()
        for j in range(hh // 2):
            even = sum(r[2 * j] for r in src_refs)
            odd = sum(r[2 * j + 1] for r in src_refs)
            g = jnp.where(keep, even, pltpu.roll(odd, RET_DK, 1))
            if rope is not None:
                g = _rope128_t(g, *rope, shift=shift)
            dst_ref[:, LANE * j:LANE * (j + 1)] = g * scale if scale != 1.0 else g

    def pack_call(name, n, q, k, v, rope):
        with_q = q is not None

        def body(*refs):
            refs = list(refs)
            q_ref = refs.pop(0) if with_q else None
            k_ref, v_ref = refs.pop(0), refs.pop(0)
            tab = tuple(r[...] for r in refs[:3]) if rope else None
            outs = refs[3:] if rope else refs
            if with_q:
                split_pairs(q_ref, outs[0], 1.0, tab)
                outs = outs[1:]
            split_pairs(k_ref, outs[0], k_scale, tab)
            for h in range(hh):
                outs[1][h] = v_ref[:, RET_DV * h:RET_DV * (h + 1)]

        ins = ([q] if with_q else []) + [k, v] + (list(tabs) if rope else [])
        in_specs = ([rows(q.shape[1])] if with_q else []) + [rows(k.shape[1]), rows(v.shape[1])]
        in_specs += [rows(LANE)] * 3 if rope else []
        n_out = 3 if with_q else 2
        return pl.pallas_call(
            body, name=name, grid=(n // tm,), in_specs=in_specs,
            out_specs=[heads(RET_DKP)] * (n_out - 1) + [heads(RET_DV)],
            out_shape=[jax.ShapeDtypeStruct((hh, n, RET_DKP), F32)] * (n_out - 1)
            + [jax.ShapeDtypeStruct((hh, n, RET_DV), F32)],
            compiler_params=_params(),
        )(*ins)

    def unpack_call(name, n, dqs, dks, dvs, rope):
        with_q = len(dqs) > 0
        uses = len(dks)

        def body(*refs):
            refs = list(refs)
            dq_refs = [refs.pop(0) for _ in range(len(dqs))]
            dk_refs = [refs.pop(0) for _ in range(uses)]
            dv_refs = [refs.pop(0) for _ in range(uses)]
            tab = tuple(r[...] for r in refs[:3]) if rope else None
            outs = refs[3:] if rope else refs
            if with_q:
                merge_pairs(dq_refs, outs[0], 1.0, tab)
                outs = outs[1:]
            merge_pairs(dk_refs, outs[0], k_scale, tab)
            for h in range(hh):
                outs[1][:, RET_DV * h:RET_DV * (h + 1)] = sum(r[h] for r in dv_refs)

        ins = list(dqs) + list(dks) + list(dvs) + (list(tabs) if rope else [])
        in_specs = [heads(RET_DKP)] * (len(dqs) + uses) + [heads(RET_DV)] * uses + ([rows(LANE)] * 3 if rope else [])
        n_out = 3 if with_q else 2
        return pl.pallas_call(
            body, name=name, grid=(n // tm,), in_specs=in_specs,
            out_specs=[rows(hh * RET_DK)] * (n_out - 1) + [rows(hh * RET_DV)],
            out_shape=[jax.ShapeDtypeStruct((n, hh * RET_DK), F32)] * (n_out - 1)
            + [jax.ShapeDtypeStruct((n, hh * RET_DV), F32)],
            compiler_params=_params(),
        )(*ins)

    @jax.custom_vjp
    def ret_pack(rq, rk, rv, crk, crv):
        q, k, v = pack_call(tag + "_lat", n_lat, rq, rk, rv, True)
        k_c, v_c = pack_call(tag + "_ctx", n_ctx, None, crk, crv, False)
        return (q, k, v), (q, k, v), (k_c, v_c), (k_c, v_c)

    def fwd(rq, rk, rv, crk, crv):
        return ret_pack(rq, rk, rv, crk, crv), None

    def bwd(_, cts):
        lat_f, lat_b, ctx_f, ctx_b = cts
        drq, drk, drv = unpack_call(tag + "_dlat", n_lat, [lat_f[0], lat_b[0]], [lat_f[1], lat_b[1]],
                                    [lat_f[2], lat_b[2]], True)
        dcrk, dcrv = unpack_call(tag + "_dctx", n_ctx, [], [ctx_f[0], ctx_b[0]], [ctx_f[1], ctx_b[1]], False)
        return drq, drk, drv, dcrk, dcrv

    ret_pack.defvjp(fwd, bwd)
    return ret_pack


def _ret_out_tile(y, g):
    mu = jnp.mean(y, axis=-1, keepdims=True)
    var = jnp.mean(jnp.square(y - mu), axis=-1, keepdims=True)
    return (g * jax.nn.sigmoid(g)) * ((y - mu) * lax.rsqrt(var + GN_EPS))


def _make_ret_out(tag):
    def specs(tm):
        y_spec = pl.BlockSpec((None, tm, RET_DV), lambda h, i: (h, i, 0))
        g_spec = pl.BlockSpec((tm, RET_DV), lambda h, i: (i, h))
        return y_spec, g_spec

    def fwd_call(yf, yb, g):
        hh, n, _ = yf.shape
        tm = _pick(n, 1024, 16)
        y_spec, g_spec = specs(tm)

        def body(yf_ref, yb_ref, g_ref, o_ref):
            o_ref[...] = _ret_out_tile(yf_ref[...] + yb_ref[...], g_ref[...]).astype(BF16)

        return pl.pallas_call(
            body, name=tag + "_fwd", grid=(hh, n // tm), in_specs=[y_spec, y_spec, g_spec], out_specs=g_spec,
            out_shape=jax.ShapeDtypeStruct((n, hh * RET_DV), BF16), compiler_params=_params(),
        )(yf, yb, g)

    def bwd_call(yf, yb, g, do):
        hh, n, _ = yf.shape
        tm = _pick(n, 1024, 16)
        y_spec, g_spec = specs(tm)

        def body(yf_ref, yb_ref, g_ref, do_ref, dy_ref, dg_ref):
            _, vjp = jax.vjp(_ret_out_tile, yf_ref[...] + yb_ref[...], g_ref[...])
            dy, dg = vjp(do_ref[...].astype(F32))
            dy_ref[...] = dy
            dg_ref[...] = dg

        return pl.pallas_call(
            body, name=tag + "_bwd", grid=(hh, n // tm), in_specs=[y_spec, y_spec, g_spec, g_spec],
            out_specs=[y_spec, g_spec],
            out_shape=[jax.ShapeDtypeStruct(yf.shape, F32), jax.ShapeDtypeStruct(g.shape, F32)],
            compiler_params=_params(),
        )(yf, yb, g, do)

    @jax.custom_vjp
    def ret_out(yf, yb, g):
        return fwd_call(yf, yb, g)

    def fwd(yf, yb, g):
        return fwd_call(yf, yb, g), (yf, yb, g)

    def bwd(res, do):
        dy, dg = bwd_call(*res, do)
        return dy, dy, dg

    ret_out.defvjp(fwd, bwd)
    return ret_out


MLA_DQ_PAD = 2 * LANE
MLA_PACK_ROWS = 256


def _rope128(x, cos, s1, s2, shift=16):
    return x * cos + pltpu.roll(x, LANE - shift, 1) * s1 + pltpu.roll(x, shift, 1) * s2


def _rope128_t(g, cos, s1, s2, shift=16):
    return g * cos + pltpu.roll(g * s1, shift, 1) + pltpu.roll(g * s2, LANE - shift, 1)


def _axial_tables(n_lat):
    half = MLA_ROPE // 2
    inv = AXIAL_BASE ** (-jnp.arange(0, half, 2, dtype=F32) / half)
    return _rope_tables_call("mla_tables", n_lat, inv, half // 2, True)


def _make_mla_pack(tag, n_lat, n_ctx, scale):
    hh = MLA_HEADS
    tm = MLA_PACK_ROWS
    ll = n_lat + n_ctx
    rope0 = hh * MLA_NOPE
    tabs = _axial_tables(n_lat)

    def rope_lanes():
        return lax.broadcasted_iota(jnp.int32, (1, LANE), 1) < MLA_ROPE

    def rows(width):
        return pl.BlockSpec((tm, width), lambda i: (i, 0))

    def heads(width, off):
        return pl.BlockSpec((hh, tm, width), lambda i: (0, i + off, 0))

    def heads_t(width, off):
        return pl.BlockSpec((hh, width, tm), lambda i: (0, 0, i + off))

    def put_kv(kv_ref, kr_rot, k_ref, v_ref, kt_ref, vt_ref):
        kr_b = kr_rot.astype(BF16)
        kr_t = jnp.transpose(kr_rot).astype(BF16)
        for h in range(hh):
            k_nope = kv_ref[:, 2 * LANE * h:2 * LANE * h + MLA_NOPE]
            val = kv_ref[:, 2 * LANE * h + MLA_NOPE:2 * LANE * (h + 1)]
            k_ref[h, :, :MLA_NOPE] = k_nope.astype(BF16)
            k_ref[h, :, MLA_NOPE:] = kr_b
            v_ref[h] = val.astype(BF16)
            kt_ref[h, :MLA_NOPE, :] = jnp.transpose(k_nope).astype(BF16)
            kt_ref[h, MLA_NOPE:, :] = kr_t
            vt_ref[h] = jnp.transpose(val).astype(BF16)

    def fwd_lat(qp, kv, kr):
        def body(qp_ref, kv_ref, kr_ref, cos_ref, s1_ref, s2_ref, q_ref, k_ref, v_ref, kt_ref, vt_ref):
            cos, s1, s2 = cos_ref[...], s1_ref[...], s2_ref[...]
            keep = rope_lanes()
            for j in range(hh // 2):
                rot = _rope128(qp_ref[:, rope0 + LANE * j:rope0 + LANE * (j + 1)], cos, s1, s2)
                q_ref[2 * j, :, MLA_NOPE:] = jnp.where(keep, rot, 0.0).astype(BF16)
                q_ref[2 * j + 1, :, MLA_NOPE:] = jnp.where(keep, pltpu.roll(rot, MLA_ROPE, 1), 0.0).astype(BF16)
            for h in range(hh):
                q_ref[h, :, :MLA_NOPE] = qp_ref[:, MLA_NOPE * h:MLA_NOPE * (h + 1)].astype(BF16)
            kr_rot = jnp.where(keep, _rope128(kr_ref[...], cos, s1, s2), 0.0)
            put_kv(kv_ref, kr_rot, k_ref, v_ref, kt_ref, vt_ref)

        return pl.pallas_call(
            body, name=tag + "_lat", grid=(n_lat // tm,),
            in_specs=[rows(qp.shape[1]), rows(kv.shape[1]), rows(LANE), rows(LANE), rows(LANE), rows(LANE)],
            out_specs=[heads(MLA_DQ_PAD, 0), heads(MLA_DQ_PAD, 0), heads(MLA_V, 0), heads_t(MLA_DQ_PAD, 0),
                       heads_t(MLA_V, 0)],
            out_shape=[jax.ShapeDtypeStruct((hh, n_lat, MLA_DQ_PAD), BF16),
                       jax.ShapeDtypeStruct((hh, ll, MLA_DQ_PAD), BF16), jax.ShapeDtypeStruct((hh, ll, MLA_V), BF16),
                       jax.ShapeDtypeStruct((hh, MLA_DQ_PAD, ll), BF16), jax.ShapeDtypeStruct((hh, MLA_V, ll), BF16)],
            compiler_params=_params(),
        )(qp, kv, kr, *tabs)

    def fwd_ctx(kv_c, kr_c, bufs):
        def body(kv_ref, kr_ref, k_in, v_in, kt_in, vt_in, k_ref, v_ref, kt_ref, vt_ref):
            kr_rot = jnp.where(rope_lanes(), kr_ref[...], 0.0)
            put_kv(kv_ref, kr_rot, k_ref, v_ref, kt_ref, vt_ref)

        any_spec = pl.BlockSpec(memory_space=pl.ANY)
        off = n_lat // tm
        return pl.pallas_call(
            body, name=tag + "_ctx", grid=(n_ctx // tm,),
            in_specs=[rows(kv_c.shape[1]), rows(LANE)] + [any_spec] * 4,
            out_specs=[heads(MLA_DQ_PAD, off), heads(MLA_V, off), heads_t(MLA_DQ_PAD, off), heads_t(MLA_V, off)],
            out_shape=[jax.ShapeDtypeStruct(b.shape, BF16) for b in bufs],
            input_output_aliases={2: 0, 3: 1, 4: 2, 5: 3}, compiler_params=_params(),
        )(kv_c, kr_c, *bufs)

    def take_kv(dk_ref, dv_ref, dkv_ref):
        dkr = jnp.zeros((tm, LANE), F32)
        for h in range(hh):
            dkv_ref[:, 2 * LANE * h:2 * LANE * h + MLA_NOPE] = dk_ref[h, :, :MLA_NOPE].astype(F32)
            dkv_ref[:, 2 * LANE * h + MLA_NOPE:2 * LANE * (h + 1)] = dv_ref[h].astype(F32)
            dkr = dkr + dk_ref[h, :, MLA_NOPE:].astype(F32)
        return jnp.where(rope_lanes(), dkr, 0.0)

    def bwd_lat(dqt, dk, dv, qp_width, kv_width):
        def body(dqt_ref, dk_ref, dv_ref, cos_ref, s1_ref, s2_ref, dqp_ref, dkv_ref, dkr_ref):
            cos, s1, s2 = cos_ref[...], s1_ref[...], s2_ref[...]
            keep = rope_lanes()
            for j in range(hh // 2):
                even = jnp.transpose(dqt_ref[2 * j]) * scale
                odd = jnp.transpose(dqt_ref[2 * j + 1]) * scale
                dqp_ref[:, MLA_NOPE * 2 * j:MLA_NOPE * (2 * j + 1)] = even[:, :MLA_NOPE]
                dqp_ref[:, MLA_NOPE * (2 * j + 1):MLA_NOPE * (2 * j + 2)] = odd[:, :MLA_NOPE]
                g = jnp.where(keep, even[:, MLA_NOPE:], pltpu.roll(odd[:, MLA_NOPE:], MLA_ROPE, 1))
                dqp_ref[:, rope0 + LANE * j:rope0 + LANE * (j + 1)] = _rope128_t(g, cos, s1, s2)
            dkr_ref[...] = jnp.where(keep, _rope128_t(take_kv(dk_ref, dv_ref, dkv_ref), cos, s1, s2), 0.0)

        return pl.pallas_call(
            body, name=tag + "_dlat", grid=(n_lat // tm,),
            in_specs=[pl.BlockSpec((hh, MLA_DQ_PAD, tm), lambda i: (0, 0, i)),
                      heads(MLA_DQ_PAD, 0), heads(MLA_V, 0), rows(LANE), rows(LANE), rows(LANE)],
            out_specs=[rows(qp_width), rows(kv_width), rows(LANE)],
            out_shape=[jax.ShapeDtypeStruct((n_lat, qp_width), F32), jax.ShapeDtypeStruct((n_lat, kv_width), F32),
                       jax.ShapeDtypeStruct((n_lat, LANE), F32)],
            compiler_params=_params(),
        )(dqt, dk, dv, *tabs)

    def bwd_ctx(dk, dv, kv_width):
        def body(dk_ref, dv_ref, dkv_ref, dkr_ref):
            dkr_ref[...] = take_kv(dk_ref, dv_ref, dkv_ref)

        off = n_lat // tm
        return pl.pallas_call(
            body, name=tag + "_dctx", grid=(n_ctx // tm,),
            in_specs=[heads(MLA_DQ_PAD, off), heads(MLA_V, off)],
            out_specs=[rows(kv_width), rows(LANE)],
            out_shape=[jax.ShapeDtypeStruct((n_ctx, kv_width), F32), jax.ShapeDtypeStruct((n_ctx, LANE), F32)],
            compiler_params=_params(),
        )(dk, dv)

    def pack(qp, kv, kr, kv_c, kr_c):
        q, *bufs = fwd_lat(qp, kv, kr)
        return (q, *fwd_ctx(kv_c, kr_c, bufs))

    def unpack(dqt, dk, dv):
        qp_width, kv_width = hh * (MLA_NOPE + MLA_ROPE), hh * (MLA_NOPE + MLA_V)
        dqp, dkv, dkr = bwd_lat(dqt, dk, dv, qp_width, kv_width)
        dkv_c, dkr_c = bwd_ctx(dk, dv, kv_width)
        return dqp, dkv, dkr, dkv_c, dkr_c

    return pack, unpack


def _make_mla(tag, n_lat, n_ctx):
    scale = (MLA_NOPE + MLA_ROPE) ** -0.5
    pack, unpack = _make_mla_pack(tag + "pack", n_lat, n_ctx, scale)
    attn_fwd, attn_delta, attn_bwd = _make_attention(tag, scale, MLA_NOPE + MLA_ROPE)

    @jax.custom_vjp
    def mla(qp, kv, kr, kv_c, kr_c):
        q, k, _, _, vt = pack(qp, kv, kr, kv_c, kr_c)
        return attn_fwd(q, k, vt)[0]

    def fwd(qp, kv, kr, kv_c, kr_c):
        q, k, v, kt, vt = pack(qp, kv, kr, kv_c, kr_c)
        o, lse = attn_fwd(q, k, vt)
        return o, (q, k, kt, v, o, lse)

    def bwd(res, do):
        q, k, kt, v, o, lse = res
        delta = attn_delta(o, do, q.shape[0])
        dqt, dk, dv = attn_bwd(q, k, kt, v, do, lse, delta)
        return unpack(dqt, dk, dv)

    mla.defvjp(fwd, bwd)
    return mla


def _make_attention(tag, scale, dq_live=None):
    neg_big = -1e30
    log2e = 1.4426950408889634
    sub = 256

    def fwd_call(q, k, vt):
        hh, n, dq = q.shape
        dv, ll = vt.shape[1], vt.shape[2]
        tq, tk = _pick(n, 2048), _pick(ll, 1408)
        sb = sub if tk % sub == 0 else tk
        c2 = scale * log2e
        k_steps = ll // tk

        def body(q_ref, k_ref, vt_ref, o_ref, lse_ref, m_scr, l_scr, acc_scr, s_scr, p_scr):
            j = pl.program_id(2)

            @pl.when(j == 0)
            def _():
                m_scr[...] = jnp.full_like(m_scr, neg_big)
                l_scr[...] = jnp.zeros_like(l_scr)
                acc_scr[...] = jnp.zeros_like(acc_scr)

            q_t = q_ref[...]
            m_prev = m_scr[...]
            m_new = m_prev
            for kk in range(tk // sb):
                rows = slice(kk * sb, (kk + 1) * sb)
                s_t = _dot(k_ref[rows, :], q_t, NT)
                s_scr[rows, :] = s_t
                m_new = jnp.maximum(m_new, jnp.max(s_t, axis=0, keepdims=True))
            mc = m_new * c2
            l_part = jnp.zeros_like(m_new)
            for kk in range(tk // sb):
                rows = slice(kk * sb, (kk + 1) * sb)
                p_t = jnp.exp2(s_scr[rows, :] * c2 - mc)
                l_part = l_part + jnp.sum(p_t, axis=0, keepdims=True)
                p_scr[rows, :] = p_t.astype(BF16)
            alpha = jnp.exp2((m_prev - m_new) * c2)
            l_scr[...] = alpha * l_scr[...] + l_part
            acc_scr[...] = alpha * acc_scr[...] + _dot(vt_ref[...], p_scr[...], NN)
            m_scr[...] = m_new

            @pl.when(j == k_steps - 1)
            def _():
                o_ref[...] = jnp.transpose(acc_scr[...] / l_scr[...]).astype(BF16)
                lse_ref[...] = m_scr[...] * scale + jnp.log(l_scr[...])

        return pl.pallas_call(
            body, name=tag + "_fwd", grid=(hh, n // tq, k_steps),
            in_specs=[pl.BlockSpec((None, tq, dq), lambda h, i, j: (h, i, 0)),
                      pl.BlockSpec((None, tk, dq), lambda h, i, j: (h, j, 0)),
                      pl.BlockSpec((None, dv, tk), lambda h, i, j: (h, 0, j))],
            out_specs=[pl.BlockSpec((tq, dv), lambda h, i, j: (i, h)),
                       pl.BlockSpec((None, 1, tq), lambda h, i, j: (h, 0, i))],
            out_shape=[jax.ShapeDtypeStruct((n, hh * dv), BF16), jax.ShapeDtypeStruct((hh, 1, n), F32)],
            scratch_shapes=[pltpu.VMEM((1, tq), F32), pltpu.VMEM((1, tq), F32), pltpu.VMEM((dv, tq), F32),
                            pltpu.VMEM((tk, tq), F32), pltpu.VMEM((tk, tq), BF16)],
            compiler_params=_params(),
        )(q, k, vt)

    def delta_call(o, do, hh):
        n = o.shape[0]
        dv = o.shape[1] // hh
        tq = _pick(n, 1024)

        def body(o_ref, do_ref, d_ref):
            prod_t = jnp.transpose(o_ref[...].astype(F32) * do_ref[...].astype(F32))
            d_ref[...] = jnp.sum(prod_t, axis=0, keepdims=True)

        spec = pl.BlockSpec((tq, dv), lambda h, i: (i, h))
        return pl.pallas_call(
            body, name=tag + "_delta", grid=(hh, n // tq), in_specs=[spec, spec],
            out_specs=pl.BlockSpec((None, 1, tq), lambda h, i: (h, 0, i)),
            out_shape=jax.ShapeDtypeStruct((hh, 1, n), F32), compiler_params=_params(),
        )(o, do)

    def bwd_call(q, k, kt, v, do, lse, delta):
        hh, n, dq = q.shape
        ll, dv = k.shape[1], v.shape[2]
        tq, tk = _pick(n, 1024), _pick(ll, 1408)
        sb = tk
        c2 = scale * log2e
        q_steps = n // tq
        live = dq_live or dq

        def body(q_ref, k_ref, kt_ref, v_ref, do_ref, lse_ref, d_ref, dqt_ref, dk_ref, dv_ref, dk_scr, dv_scr):
            j = pl.program_id(1)
            i = pl.program_id(2)

            @pl.when(i == 0)
            def _():
                dk_scr[...] = jnp.zeros_like(dk_scr)
                dv_scr[...] = jnp.zeros_like(dv_scr)

            q_t, do_t = q_ref[...], do_ref[...]
            lse2 = lse_ref[...] * log2e
            delta_t = d_ref[...]
            dq_part = None
            for kk in range(tk // sb):
                rows = slice(kk * sb, (kk + 1) * sb)
                s_t = _dot(k_ref[rows, :], q_t, NT)
                p_t = jnp.exp2(s_t * c2 - lse2)
                ds_t = p_t * (_dot(v_ref[rows, :], do_t, NT) - delta_t)
                dv_scr[rows, :] += _dot(p_t, do_t, NN)
                dk_scr[rows, :] += _dot(ds_t, q_t, NN)
                part = _dot(kt_ref[:live, rows], ds_t, NN)
                dq_part = part if dq_part is None else dq_part + part
            cols = pl.ds(pl.multiple_of(i * tq, tq), tq)

            @pl.when(j == 0)
            def _():
                dqt_ref[:live, cols] = dq_part
                if live < dq:
                    dqt_ref[live:, cols] = jnp.zeros((dq - live, tq), F32)

            @pl.when(j > 0)
            def _():
                dqt_ref[:live, cols] += dq_part

            @pl.when(i == q_steps - 1)
            def _():
                dk_ref[...] = (dk_scr[...] * scale).astype(BF16)
                dv_ref[...] = dv_scr[...].astype(BF16)

        return pl.pallas_call(
            body, name=tag + "_bwd", grid=(hh, ll // tk, q_steps),
            in_specs=[pl.BlockSpec((None, tq, dq), lambda h, j, i: (h, i, 0)),
                      pl.BlockSpec((None, tk, dq), lambda h, j, i: (h, j, 0)),
                      pl.BlockSpec((None, dq, tk), lambda h, j, i: (h, 0, j)),
                      pl.BlockSpec((None, tk, dv), lambda h, j, i: (h, j, 0)),
                      pl.BlockSpec((tq, dv), lambda h, j, i: (i, h)),
                      pl.BlockSpec((None, 1, tq), lambda h, j, i: (h, 0, i)),
                      pl.BlockSpec((None, 1, tq), lambda h, j, i: (h, 0, i))],
            out_specs=[pl.BlockSpec((None, dq, n), lambda h, j, i: (h, 0, 0)),
                       pl.BlockSpec((None, tk, dq), lambda h, j, i: (h, j, 0)),
                       pl.BlockSpec((None, tk, dv), lambda h, j, i: (h, j, 0))],
            out_shape=[jax.ShapeDtypeStruct((hh, dq, n), F32), jax.ShapeDtypeStruct((hh, ll, dq), BF16),
                       jax.ShapeDtypeStruct((hh, ll, dv), BF16)],
            scratch_shapes=[pltpu.VMEM((tk, dq), F32), pltpu.VMEM((tk, dv), F32)],
            compiler_params=_params(),
        )(q, k, kt, v, do, lse, delta)

    return fwd_call, delta_call, bwd_call


def _loss_tile(x, g, tgt):
    r = lax.rsqrt(jnp.mean(x * x, axis=-1, keepdims=True) + RMS_EPS)
    err = x * r * g - tgt
    per_tok = jnp.mean(err * err, axis=-1, keepdims=True)
    return 0.5 * jnp.sum(per_tok, axis=0, keepdims=True)


def _make_final_loss(tag):
    def fwd_call(x, g, tgt):
        t, d = x.shape
        tm = _pick(t, 512, 16)

        def body(x_ref, g_ref, t_ref, l_ref):
            l_ref[...] = jnp.broadcast_to(_loss_tile(x_ref[...], g_ref[...], t_ref[...]), (1, LANE))

        parts = pl.pallas_call(
            body, name=tag + "_fwd", grid=(t // tm,),
            in_specs=[_row_spec(tm, d), _vec_spec(d), _row_spec(tm, d)],
            out_specs=pl.BlockSpec((None, 1, LANE), lambda i: (i, 0, 0)),
            out_shape=jax.ShapeDtypeStruct((t // tm, 1, LANE), F32), compiler_params=_params(),
        )(x, g, tgt)
        return jnp.sum(parts[:, 0, 0])

    def bwd_call(x, g, tgt, dl):
        t, d = x.shape
        tm = _pick(t, 256, 16)

        def body(x_ref, g_ref, t_ref, dl_ref, dx_ref, dg_ref):
            _, vjp = jax.vjp(_loss_tile, x_ref[...], g_ref[...], t_ref[...])
            dx, dg, _ = vjp(dl_ref[...])
            dx_ref[...] = dx

            @pl.when(pl.program_id(0) == 0)
            def _():
                dg_ref[...] = jnp.zeros_like(dg_ref)

            dg_ref[...] += dg

        return pl.pallas_call(
            body, name=tag + "_bwd", grid=(t // tm,),
            in_specs=[_row_spec(tm, d), _vec_spec(d), _row_spec(tm, d), pl.BlockSpec((1, 1), lambda i: (0, 0))],
            out_specs=[_row_spec(tm, d), _vec_spec(d)],
            out_shape=[jax.ShapeDtypeStruct((t, d), F32), jax.ShapeDtypeStruct((1, d), F32)],
            compiler_params=_params(),
        )(x, g, tgt, dl)

    @jax.custom_vjp
    def final_loss(x, g, tgt):
        return fwd_call(x, g, tgt)

    def fwd(x, g, tgt):
        return fwd_call(x, g, tgt), (x, g, tgt)

    def bwd(res, dl):
        x, g, tgt = res
        dx, dg = bwd_call(x, g, tgt, dl.reshape(1, 1).astype(F32))
        return dx, dg, jnp.zeros_like(tgt)

    final_loss.defvjp(fwd, bwd)
    return final_loss


def _exchange(arrays, gather, name):
    n = len(arrays)

    def body(*refs):
        ins, outs = refs[:n], refs[n:2 * n]
        send_sems, recv_sems, local_sems = refs[2 * n:]
        me = 4 * lax.axis_index("x") + 2 * lax.axis_index("y") + lax.axis_index("c")

        def remote(a, d, wait_side=False):
            peer = (me + d) % N_DEV
            origin = (me + N_DEV - d) % N_DEV
            src = ins[a] if gather else ins[a].at[peer]
            dst = outs[a].at[origin if wait_side else me]
            return pltpu.make_async_remote_copy(
                src_ref=src, dst_ref=dst, send_sem=send_sems.at[a, d - 1], recv_sem=recv_sems.at[a, d - 1],
                device_id=(peer // 4, (peer // 2) % 2, peer % 2), device_id_type=pl.DeviceIdType.MESH)

        def local(a):
            src = ins[a] if gather else ins[a].at[me]
            return pltpu.make_async_copy(src, outs[a].at[me], local_sems.at[a])

        for a in range(n):
            for d in range(1, N_DEV):
                remote(a, d).start()
            local(a).start()
        for a in range(n):
            local(a).wait()
            for d in range(1, N_DEV):
                remote(a, d, wait_side=True).wait_recv()
                remote(a, d).wait_send()

    out_shape = []
    for arr in arrays:
        shape = (N_DEV,) + arr.shape if gather else arr.shape
        out_shape.append(jax.ShapeDtypeStruct(shape, arr.dtype))
    any_spec = pl.BlockSpec(memory_space=pl.ANY)
    return pl.pallas_call(
        body, name=name, in_specs=[any_spec] * n, out_specs=[any_spec] * n, out_shape=out_shape,
        scratch_shapes=[pltpu.SemaphoreType.DMA((n, N_DEV - 1)), pltpu.SemaphoreType.DMA((n, N_DEV - 1)),
                        pltpu.SemaphoreType.DMA((n,))],
        compiler_params=pltpu.CompilerParams(has_side_effects=True),
    )(*arrays)


def _split_copy(ins, lands, send_sems, recv_sems, a, d, gather, wait_side):
    me = 4 * lax.axis_index("x") + 2 * lax.axis_index("y") + lax.axis_index("c")
    peer = (me + d) % N_DEV
    origin = (me + N_DEV - d) % N_DEV
    return pltpu.make_async_remote_copy(
        src_ref=ins[a] if gather else ins[a].at[peer], dst_ref=lands[a].at[origin if wait_side else me],
        send_sem=send_sems.at[a * (N_DEV - 1) + d - 1], recv_sem=recv_sems.at[a * (N_DEV - 1) + d - 1],
        device_id=(peer // 4, (peer // 2) % 2, peer % 2), device_id_type=pl.DeviceIdType.MESH)


def _exchange_start(srcs, lands, after, gather, name):
    n = len(srcs)

    def body(*refs):
        ins, lnd = refs[:n], refs[n:2 * n]
        send_sems, recv_sems = refs[2 * n + 1], refs[2 * n + 2]
        for a in range(n):
            for d in range(1, N_DEV):
                _split_copy(ins, lnd, send_sems, recv_sems, a, d, gather, False).start()

    hbm = pl.BlockSpec(memory_space=pltpu.HBM)
    sem = pl.BlockSpec(memory_space=pltpu.SEMAPHORE)
    bufs = [pltpu.with_memory_space_constraint(t, pltpu.HBM) for t in list(srcs) + list(lands) + [after]]
    res = pl.pallas_call(
        body, name=name,
        in_specs=[hbm] * (2 * n + 1), out_specs=[sem, sem] + [hbm] * (2 * n + 1),
        out_shape=[pltpu.SemaphoreType.DMA((n * (N_DEV - 1),)), pltpu.SemaphoreType.DMA((n * (N_DEV - 1),))]
        + [pltpu.HBM(t.shape, t.dtype) for t in bufs],
        input_output_aliases={i: 2 + i for i in range(2 * n + 1)},
        compiler_params=pltpu.CompilerParams(has_side_effects=pltpu.SideEffectType.DATAFLOW_SIDE_EFFECTING),
    )(*bufs)
    return res[0], res[1], res[2:2 + n], res[2 + n:2 + 2 * n], res[-1]


def _exchange_wait(send_sems, recv_sems, srcs, lands, after, gather, name):
    n = len(srcs)

    def body(*refs):
        ins, lnd = refs[:n], refs[n:2 * n]
        send_sems_ref, recv_sems_ref = refs[2 * n], refs[2 * n + 1]
        for a in range(n):
            for d in range(1, N_DEV):
                _split_copy(ins, lnd, send_sems_ref, recv_sems_ref, a, d, gather, False).wait_send()
                _split_copy(ins, lnd, send_sems_ref, recv_sems_ref, a, d, gather, True).wait_recv()

    hbm = pl.BlockSpec(memory_space=pltpu.HBM)
    sem = pl.BlockSpec(memory_space=pltpu.SEMAPHORE)
    bufs = list(srcs) + list(lands)
    res = pl.pallas_call(
        body, name=name,
        in_specs=[hbm] * (2 * n) + [sem, sem, pl.BlockSpec(memory_space=pl.ANY)],
        out_specs=[hbm] * (2 * n),
        out_shape=[pltpu.HBM(t.shape, t.dtype) for t in bufs],
        input_output_aliases={i: i for i in range(2 * n)},
        compiler_params=pltpu.CompilerParams(has_side_effects=pltpu.SideEffectType.DATAFLOW_SIDE_EFFECTING),
    )(*bufs, send_sems, recv_sems, after)
    return res[n:]


def _own_slot(block, me):
    empty = lax.empty((N_DEV,) + block.shape, block.dtype)
    return lax.dynamic_update_slice(empty, block[None], (me,) + (0,) * block.ndim)


def _coords():
    return lax.axis_index("x"), lax.axis_index("y"), lax.axis_index("c")


def _other_chips(x, y):
    return [(1 - x, y), (x, 1 - y), (1 - x, 1 - y)]


def _gather_two_level(arrays, name):
    n = len(arrays)

    def body(*refs):
        ins, outs = refs[:n], refs[n:2 * n]
        send_sems, recv_sems, local_sems = refs[2 * n:]
        x, y, c = _coords()
        me, sib = (x, y, c), (x, y, 1 - c)
        chips = _other_chips(x, y)

        def copy(a, k, block, to, from_input=False):
            slot = 4 * block[0] + 2 * block[1] + block[2]
            return pltpu.make_async_remote_copy(
                src_ref=ins[a] if from_input else outs[a].at[slot], dst_ref=outs[a].at[slot],
                send_sem=send_sems.at[a, k], recv_sem=recv_sems.at[a, k],
                device_id=to, device_id_type=pl.DeviceIdType.MESH)

        def local(a):
            return pltpu.make_async_copy(ins[a], outs[a].at[4 * x + 2 * y + c], local_sems.at[a])

        for a in range(n):
            for j, chip in enumerate(chips):
                copy(a, 1 + j, me, (*chip, c), True).start()
            copy(a, 0, me, sib, True).start()
            local(a).start()
        for a in range(n):
            for j, chip in enumerate(chips):
                copy(a, 1 + j, (*chip, c), me).wait_recv()
                copy(a, 4 + j, (*chip, c), sib).start()
        for a in range(n):
            copy(a, 0, sib, me).wait_recv()
            for j, chip in enumerate(chips):
                copy(a, 4 + j, (*chip, 1 - c), me).wait_recv()
            for k in range(N_DEV - 1):
                copy(a, k, me, sib, True).wait_send()
            local(a).wait()

    any_spec = pl.BlockSpec(memory_space=pl.ANY)
    return pl.pallas_call(
        body, name=name, in_specs=[any_spec] * n, out_specs=[any_spec] * n,
        out_shape=[jax.ShapeDtypeStruct((N_DEV,) + arr.shape, arr.dtype) for arr in arrays],
        scratch_shapes=[pltpu.SemaphoreType.DMA((n, N_DEV - 1)), pltpu.SemaphoreType.DMA((n, N_DEV - 1)),
                        pltpu.SemaphoreType.DMA((n,))],
        compiler_params=pltpu.CompilerParams(has_side_effects=True),
    )(*arrays)


def _make_gather_op(tag):
    @jax.custom_vjp
    def gather_op(xl):
        return _exchange([xl], True, tag + "_gather")[0]

    def fwd(xl):
        return gather_op(xl), None

    def bwd(_, g):
        return (jnp.sum(_exchange([g], False, tag + "_scatter")[0], axis=0),)

    gather_op.defvjp(fwd, bwd)
    return gather_op


def _adamw(gstack, w, m, v, name):
    s, r, cn = gstack.shape
    tr = _pick(r, max(8, (2 * 1024 * 1024) // (4 * cn) // 8 * 8), 8)
    c1 = 1.0 - ADAM_B1 ** ADAM_STEP
    c2 = 1.0 - ADAM_B2 ** ADAM_STEP

    def body(g_ref, w_ref, m_ref, v_ref, go_ref, d_ref, mo_ref, vo_ref):
        g = g_ref[0].astype(F32)
        for q in range(1, s):
            g = g + g_ref[q].astype(F32)
        m_new = ADAM_B1 * m_ref[...] + (1.0 - ADAM_B1) * g
        v_new = ADAM_B2 * v_ref[...] + (1.0 - ADAM_B2) * (g * g)
        go_ref[...] = g
        mo_ref[...] = m_new
        vo_ref[...] = v_new
        d_ref[...] = -ADAM_LR * ((m_new / c1) / (jnp.sqrt(v_new / c2) + ADAM_EPS) + ADAM_WD * w_ref[...])

    tile = pl.BlockSpec((tr, cn), lambda i: (i, 0))
    out = jax.ShapeDtypeStruct((r, cn), F32)
    return pl.pallas_call(
        body, name=name, grid=(r // tr,),
        in_specs=[pl.BlockSpec((s, tr, cn), lambda i: (0, i, 0)), tile, tile, tile],
        out_specs=[tile, tile, tile, tile], out_shape=[out, out, out, out],
        compiler_params=_params(),
    )(gstack, w, m, v)


def _cols_from_stack(w):
    return jnp.swapaxes(w, 0, 1).reshape(w.shape[1], N_DEV * w.shape[2])


def _ada_vectors(p, silu_c_all, me):
    d = p["c_ctx"].shape[0]
    n_a = p["ada_w"].shape[1]
    a_in = jnp.concatenate([silu_c_all, jax.nn.silu(p["c_ctx"])[None, :], jnp.zeros((7, d), F32)], axis=0)
    b_loc = lax.dynamic_slice(p["ada_b"], (0, me * n_a), (1, n_a))
    r_loc = _make_small_mm("ada")(a_in, p["ada_w"]) + b_loc
    r_full = _make_gather_op("ada")(r_loc)
    m_lat = lax.dynamic_index_in_dim(r_full, me, axis=1, keepdims=False).reshape(N_MOD, 1, d)
    m_ctx = r_full[:, N_DEV, :].reshape(N_MOD, 1, d)
    return m_lat, m_ctx


def _stage_a_fwd(p_ada, x, ctx, ng, w_in, wait_w_out, silu_c_all, me):
    (m_lat, m_ctx), vjp_ada = jax.vjp(lambda q: _ada_vectors(q, silu_c_all, me), p_ada)
    lat, cx = _ffn_parts("ffn1"), _ffn_parts("ffn1c")
    h, a, gu = lat[0](x, ng, m_lat[0], m_lat[1], w_in)
    hc, ac, guc = cx[0](ctx, ng, m_ctx[0], m_ctx[1], w_in)
    w_out = wait_w_out(a)
    x1, f1 = lat[1](a, w_out, x, m_lat[2])
    c1, f1c = cx[1](ac, w_out, ctx, m_ctx[2])
    res = dict(vjp_ada=vjp_ada, m_lat=m_lat, m_ctx=m_ctx, x=x, ctx=ctx, ng=ng, w_in=w_in, w_out=w_out,
               lat=(h, a, gu, f1), cx=(hc, ac, guc, f1c))
    return (x1, c1, m_lat, m_ctx), res


def _stage_a_bwd(res, dx1, dc1, dm_lat, dm_ctx, start_grads):
    lat, cx = _ffn_parts("ffn1"), _ffn_parts("ffn1c")
    m_lat, m_ctx, ng, w_in, w_out = res["m_lat"], res["m_ctx"], res["ng"], res["w_in"], res["w_out"]
    hc, ac, guc, f1c = res["cx"]
    dgate_c, dgu_c, dw_out_c = cx[2](dc1, f1c, m_ctx[2], w_out, guc, ac)
    _, dng_c, dsh_c, dsc_c, dw_in_c = cx[3](res["ctx"], ng, m_ctx[0], m_ctx[1], w_in, hc, dgu_c, dc1)
    h, a, gu, f1 = res["lat"]
    dgate, dgu, dw_out = lat[2](dx1, f1, m_lat[2], w_out, gu, a)
    dgu = start_grads("ffn1_w_out", dw_out + dw_out_c, dgu)
    dw_in = _ffn_dwin_bwd(h, dgu, "ffn1_dwin") + dw_in_c
    dgu = start_grads("ffn1_w_in", dw_in, dgu)
    dh = _ffn_dh_bwd(dgu, w_in, "ffn1_dh")
    dx, dng, dsc, dsh = _norm_mod_bwd(res["x"], ng, m_lat[1], m_lat[0], dh, dx1, "ffn1_dnorm")

    def rows(dsh_, dsc_, dgate_):
        return jnp.concatenate([dsh_, dsc_, dgate_, jnp.zeros((N_MOD - 3,) + dsh_.shape, F32)[:, 0]], axis=0)[:, None, :]

    (g_ada,) = res["vjp_ada"]((dm_lat + rows(dsh, dsc, dgate), dm_ctx + rows(dsh_c, dsc_c, dgate_c)))
    return g_ada, dx, dng + dng_c


def _stage_b(p, x1, c1, m_lat, m_ctx):
    n_lat, d = x1.shape
    n_ctx = c1.shape[0]
    w_mix = jnp.pad(_cols_from_stack(p["mix_w_in"]), ((0, 0), (0, MIX_IN_PAD - MIX_IN)))
    proj, x1 = _make_norm_proj_carry("mix")(x1, p["norm2_g"], m_lat[3], m_lat[4], w_mix)
    proj_c = _make_norm_proj("mixc")(c1, p["norm2_g"], m_ctx[3], m_ctx[4], w_mix)
    widths = SPLITS[:6] + (LANE,)
    rq, rk, rv, rg, cq, ckv, kr = _make_split("mixsplit", widths, MIX_IN_PAD)(proj)
    _, crk, crv, _, _, cckv, ckr = _make_split("mixsplitc", widths, MIX_IN_PAD)(proj_c)

    zq = jnp.zeros((1, MLA_Q_RANK), F32)
    zkv = jnp.zeros((1, MLA_KV_RANK), F32)
    w_uq3 = _cols_from_stack(p["mla_w_uq"]).reshape(MLA_Q_RANK, MLA_HEADS, MLA_NOPE + MLA_ROPE)
    w_uq = jnp.concatenate([w_uq3[:, :, :MLA_NOPE].reshape(MLA_Q_RANK, -1),
                            w_uq3[:, :, MLA_NOPE:].reshape(MLA_Q_RANK, -1)], axis=1)
    w_ukv = _cols_from_stack(p["mla_w_ukv"])
    q = _make_norm_proj("uq")(cq, p["mla_q_norm_g"], zq, zq, w_uq)
    kv = _make_norm_proj("ukv")(ckv, p["mla_kv_norm_g"], zkv, zkv, w_ukv)
    kv_c = _make_norm_proj("ukvc")(cckv, p["mla_kv_norm_g"], zkv, zkv, w_ukv)

    lg_f = jax.nn.log_sigmoid(p["ret_decay_fwd"][0])
    lg_b = jax.nn.log_sigmoid(p["ret_decay_bwd"][0])
    lat_f, lat_b, ctx_f, ctx_b = _make_ret_pack("retpack", n_lat, n_ctx)(rq, rk, rv, crk, crv)
    assert n_ctx == RET_CHUNK, "the context prefix is one retention chunk"

    def lanes(lg):
        return jnp.broadcast_to(lg[:, None, None], (RET_HEADS, 1, LANE))

    s0_f = _make_ctx_state("retcf", False)(*ctx_f, lanes(lg_f))
    s0_b = _make_ctx_state("retcb", True)(*ctx_b, lanes(lg_b))
    y_f = _make_ret_dir("retf", False)(*lat_f, lanes(lg_f), s0_f)
    y_b = _make_ret_dir("retb", True)(*lat_b, lanes(lg_b), s0_b)
    ret_o = _make_ret_out("reto")(y_f, y_b, rg)

    mla_o = _make_mla("mla", n_lat, n_ctx)(q, kv, kr, kv_c, ckr)

    w_mo = p["mix_w_out"].reshape(-1, d)
    return _make_res_proj("mixo")(jnp.concatenate([ret_o, mla_o], axis=-1), w_mo, x1, m_lat[5])


def _stage_c(p, x2, m_lat, tgt):
    x3 = _make_ffn_block("ffn2")(x2, p["norm3_g"], m_lat[6], m_lat[7], m_lat[8], p["ffn2_w_in"], p["ffn2_w_out"])
    return _make_final_loss("loss")(x3, p["final_norm_g"], tgt)


FIRST = ("ffn1_w_in", "ffn1_w_out")
MID = ("mix_w_in", "mla_w_uq", "mla_w_ukv", "mix_w_out")
LAST = ("ffn2_w_in", "ffn2_w_out")
BIG = FIRST + MID + LAST
SMALL = ("c_ctx", "ada_b", "norm1_g", "norm2_g", "ret_decay_fwd", "ret_decay_bwd", "mla_q_norm_g",
         "mla_kv_norm_g", "norm3_g", "final_norm_g")
WEIGHTS = ("c_ctx", "ada_w", "ada_b", "norm1_g", "ffn1_w_in", "ffn1_w_out", "norm2_g", "mix_w_in", "ret_decay_fwd",
           "ret_decay_bwd", "mla_q_norm_g", "mla_w_uq", "mla_kv_norm_g", "mla_w_ukv", "mix_w_out", "norm3_g",
           "ffn2_w_in", "ffn2_w_out", "final_norm_g")


def _pack(parts):
    flat = jnp.concatenate([t.reshape(-1) for t in parts])
    pad = (-flat.shape[0]) % LANE
    return jnp.pad(flat, (0, pad)).reshape(1, -1)


def _unpack(flat, like):
    out, off = [], 0
    for t in like:
        out.append(flat[0, off:off + t.size].reshape(t.shape))
        off += t.size
    return out


def kernel(x, c, ctx, c_ctx, ada_w, ada_b, norm1_g, ffn1_w_in, ffn1_w_out, norm2_g, mix_w_in, ret_decay_fwd, ret_decay_bwd, mla_q_norm_g, mla_w_uq, mla_kv_norm_g, mla_w_ukv, mix_w_out, norm3_g, ffn2_w_in, ffn2_w_out, final_norm_g, loss_target, m_c_ctx, m_ada_w, m_ada_b, m_norm1_g, m_ffn1_w_in, m_ffn1_w_out, m_norm2_g, m_mix_w_in, m_ret_decay_fwd, m_ret_decay_bwd, m_mla_q_norm_g, m_mla_w_uq, m_mla_kv_norm_g, m_mla_w_ukv, m_mix_w_out, m_norm3_g, m_ffn2_w_in, m_ffn2_w_out, m_final_norm_g, v_c_ctx, v_ada_w, v_ada_b, v_norm1_g, v_ffn1_w_in, v_ffn1_w_out, v_norm2_g, v_mix_w_in, v_ret_decay_fwd, v_ret_decay_bwd, v_mla_q_norm_g, v_mla_w_uq, v_mla_kv_norm_g, v_mla_w_ukv, v_mix_w_out, v_norm3_g, v_ffn2_w_in, v_ffn2_w_out, v_final_norm_g):
    w = dict(c_ctx=c_ctx, ada_w=ada_w, ada_b=ada_b, norm1_g=norm1_g, ffn1_w_in=ffn1_w_in, ffn1_w_out=ffn1_w_out,
             norm2_g=norm2_g, mix_w_in=mix_w_in, ret_decay_fwd=ret_decay_fwd, ret_decay_bwd=ret_decay_bwd,
             mla_q_norm_g=mla_q_norm_g, mla_w_uq=mla_w_uq, mla_kv_norm_g=mla_kv_norm_g, mla_w_ukv=mla_w_ukv,
             mix_w_out=mix_w_out, norm3_g=norm3_g, ffn2_w_in=ffn2_w_in, ffn2_w_out=ffn2_w_out,
             final_norm_g=final_norm_g)
    mom = dict(c_ctx=m_c_ctx, ada_w=m_ada_w, ada_b=m_ada_b, norm1_g=m_norm1_g, ffn1_w_in=m_ffn1_w_in,
               ffn1_w_out=m_ffn1_w_out, norm2_g=m_norm2_g, mix_w_in=m_mix_w_in, ret_decay_fwd=m_ret_decay_fwd,
               ret_decay_bwd=m_ret_decay_bwd, mla_q_norm_g=m_mla_q_norm_g, mla_w_uq=m_mla_w_uq,
               mla_kv_norm_g=m_mla_kv_norm_g, mla_w_ukv=m_mla_w_ukv, mix_w_out=m_mix_w_out, norm3_g=m_norm3_g,
               ffn2_w_in=m_ffn2_w_in, ffn2_w_out=m_ffn2_w_out, final_norm_g=m_final_norm_g)
    var = dict(c_ctx=v_c_ctx, ada_w=v_ada_w, ada_b=v_ada_b, norm1_g=v_norm1_g, ffn1_w_in=v_ffn1_w_in,
               ffn1_w_out=v_ffn1_w_out, norm2_g=v_norm2_g, mix_w_in=v_mix_w_in, ret_decay_fwd=v_ret_decay_fwd,
               ret_decay_bwd=v_ret_decay_bwd, mla_q_norm_g=v_mla_q_norm_g, mla_w_uq=v_mla_w_uq,
               mla_kv_norm_g=v_mla_kv_norm_g, mla_w_ukv=v_mla_w_ukv, mix_w_out=v_mix_w_out, norm3_g=v_norm3_g,
               ffn2_w_in=v_ffn2_w_in, ffn2_w_out=v_ffn2_w_out, final_norm_g=v_final_norm_g)
    me = 4 * lax.axis_index("x") + 2 * lax.axis_index("y") + lax.axis_index("c")

    shard = {k: w[k][0].astype(BF16) for k in BIG}
    first = _gather_two_level([shard["ffn1_w_in"], jax.nn.silu(c)], "weights_gather")
    silu_c_all = first[-1][:, 0, :]

    def start_gather(names, after, name):
        return _exchange_start([shard[k] for k in names], [_own_slot(shard[k], me) for k in names], after, True, name)

    wout_start = start_gather(FIRST[1:], first[0], "ffn1_wout_start")
    mid_start = start_gather(MID, wout_start[4], "mixer_weights_start")
    last_start = start_gather(LAST, mid_start[4], "ffn2_weights_start")

    def wait_w_out(after):
        return _exchange_wait(*wout_start[:4], after, True, "ffn1_wout_wait")[0]

    pa = dict(ada_w=ada_w[0], c_ctx=c_ctx, ada_b=ada_b)
    (x1, c1, m_lat, m_ctx), res_a = _stage_a_fwd(pa, x[0], ctx[0], norm1_g, last_start[4], wait_w_out, silu_c_all, me)

    mid = _exchange_wait(mid_start[0], mid_start[1], mid_start[2], mid_start[3], x1, True, "mixer_weights_wait")
    pb = dict(zip(MID, mid))
    for k in ("norm2_g", "mla_q_norm_g", "mla_kv_norm_g", "ret_decay_fwd", "ret_decay_bwd"):
        pb[k] = w[k]
    x2, vjp_b = jax.vjp(_stage_b, pb, x1, c1, m_lat, m_ctx)

    last = _exchange_wait(last_start[0], last_start[1], last_start[2], last_start[3], x2, True, "ffn2_weights_wait")
    pc = dict(zip(LAST, last), norm3_g=norm3_g, final_norm_g=final_norm_g[None, :])
    loss_local, vjp_c = jax.vjp(lambda q, t, m: _stage_c(q, t, m, loss_target[0]), pc, x2, m_lat)

    gc, dx2, dm_c = vjp_c(jnp.ones((), F32))
    last_scat = _exchange_start([gc[k] for k in LAST],
                                [_own_slot(lax.dynamic_index_in_dim(gc[k], me, 0, False), me) for k in LAST],
                                dx2, False, "ffn2_grads_start")
    gb, dx1, dc1, dm_b, dmc_b = vjp_b(last_scat[4])
    mid_scat = _exchange_start([gb[k] for k in MID],
                               [_own_slot(lax.dynamic_index_in_dim(gb[k], me, 0, False), me) for k in MID],
                               dx1, False, "mixer_grads_start")
    first_scat = {}

    def start_grads(name, dw, after):
        first_scat[name] = _exchange_start(
            [dw], [_own_slot(lax.dynamic_index_in_dim(dw, me, 0, False), me)], after, False, name + "_grads_start")
        return first_scat[name][4]

    g_ada, dx, dng1 = _stage_a_bwd(res_a, mid_scat[4], dc1, dm_b + dm_c, dmc_b, start_grads)
    grads = {**g_ada, **gb, **gc, "x": dx, "norm1_g": dng1}
    grads["final_norm_g"] = grads["final_norm_g"][0]

    exchanged = {k: _exchange_wait(*first_scat[k][:4], dx, False, k + "_grads_wait")[0] for k in FIRST}
    exchanged.update(zip(LAST, _exchange_wait(*last_scat[:4], dx, False, "ffn2_grads_wait")))
    exchanged.update(zip(MID, _exchange_wait(*mid_scat[:4], dx, False, "mixer_grads_wait")))
    zero1 = [jnp.zeros((1,), F32)]
    small_like = zero1 + [w[k] for k in SMALL]
    small_all = _exchange([_pack([loss_local.reshape(1)] + [grads[k] for k in SMALL])], True, "small_grads_gather")[0]
    loss = jnp.sum(small_all[:, 0, 0])

    out_g, out_d, out_m, out_v = {}, {}, {}, {}

    def update(name, gstack, shape2d):
        res = _adamw(gstack, w[name].reshape(shape2d), mom[name].reshape(shape2d), var[name].reshape(shape2d),
                     "adamw_" + name)
        out_g[name], out_d[name], out_m[name], out_v[name] = [t.reshape(w[name].shape) for t in res]

    for k in BIG:
        update(k, exchanged[k], exchanged[k].shape[1:])
    update("ada_w", grads["ada_w"][None], ada_w.shape[1:])
    res = _adamw(small_all, _pack(small_like), _pack(zero1 + [mom[k] for k in SMALL]),
                 _pack(zero1 + [var[k] for k in SMALL]), "adamw_small")
    for dst, flat in zip((out_g, out_d, out_m, out_v), res):
        for k, t in zip(SMALL, _unpack(flat, small_like)[1:]):
            dst[k] = t

    return (loss, grads["x"][None], *[out_g[k] for k in WEIGHTS], *[out_d[k] for k in WEIGHTS],
            *[out_m[k] for k in WEIGHTS], *[out_v[k] for k in WEIGHTS])
```

```python
import functools

import jax
import jax.numpy as jnp
from jax import lax
from jax.experimental import pallas as pl
from jax.experimental.pallas import tpu as pltpu

F32 = jnp.float32
BF16 = jnp.bfloat16

N_DEV = 8
MESH_AXES = ("x", "y", "c")

GRID_W = 64
N_MOD = 9
RET_HEADS = 8
RET_DK = 64
RET_DV = 128
RET_CHUNK = 256
RET_ROPE_BASE = 10000.0
MLA_HEADS = 8
MLA_Q_RANK = 512
MLA_KV_RANK = 256
MLA_NOPE = 128
MLA_ROPE = 64
MLA_V = 128
AXIAL_BASE = 10000.0
RMS_EPS = 1e-6
GN_EPS = 1e-5
SPLITS = (RET_HEADS * RET_DK, RET_HEADS * RET_DK, RET_HEADS * RET_DV, RET_HEADS * RET_DV,
          MLA_Q_RANK, MLA_KV_RANK, MLA_ROPE)
MIX_IN = sum(SPLITS)
MIX_IN_PAD = 4096

ADAM_LR = 0.001
ADAM_B1 = 0.9
ADAM_B2 = 0.999
ADAM_EPS = 1e-08
ADAM_WD = 0.01
ADAM_STEP = 10

LANE = 128
RET_DKP = LANE
VMEM_LIMIT_BYTES = 56 * 1024 * 1024

NN = ((1,), (0,))
NT = ((1,), (1,))
TN = ((0,), (0,))


def _pick(dim, target, align=LANE):
    t = min(dim, target)
    t -= t % align
    while t >= align:
        if dim % t == 0:
            return t
        t -= align
    return dim


def _params():
    return pltpu.CompilerParams(vmem_limit_bytes=VMEM_LIMIT_BYTES)


def _dot(a, b, dims):
    return lax.dot_general(a.astype(BF16), b.astype(BF16), (dims, ((), ())), preferred_element_type=F32)


def _mm_call(name, grid, ins, pairs, outs, acc_shapes, epilogue):
    n_in, n_out = len(ins), len(outs)
    k_axis = len(grid) - 1
    k_steps = grid[k_axis]

    def body(*refs):
        in_refs = refs[:n_in]
        out_refs = refs[n_in:n_in + n_out]
        accs = refs[n_in + n_out:]
        k = pl.program_id(k_axis)

        @pl.when(k == 0)
        def _():
            for acc in accs:
                acc[...] = jnp.zeros_like(acc)

        for ai, bi, dims, ci in pairs:
            accs[ci][...] += _dot(in_refs[ai][...], in_refs[bi][...], dims)

        @pl.when(k == k_steps - 1)
        def _():
            epilogue([acc[...] for acc in accs], in_refs, out_refs)

    res = pl.pallas_call(
        body, name=name, grid=grid,
        in_specs=[s for _, s in ins], out_specs=[s for _, s in outs],
        out_shape=[s for s, _ in outs],
        scratch_shapes=[pltpu.VMEM(s, F32) for s in acc_shapes],
        compiler_params=_params(),
    )(*[a for a, _ in ins])
    return res


def _matmul(a, b, mode, out_dtype, name, tm=1024, tn=1024, tk=512):
    if mode == "nn":
        (m, kd), n = a.shape, b.shape[1]
    elif mode == "nt":
        (m, kd), n = a.shape, b.shape[0]
    else:
        (kd, m), n = a.shape, b.shape[1]
    tm, tn = _pick(m, tm, 16), _pick(n, tn)
    tk = _pick(kd, tk) if mode != "tn" else _pick(kd, tk, 16)
    if mode == "nn":
        a_spec = pl.BlockSpec((tm, tk), lambda i, j, k: (i, k))
        b_spec = pl.BlockSpec((tk, tn), lambda i, j, k: (k, j))
        dims = NN
    elif mode == "nt":
        a_spec = pl.BlockSpec((tm, tk), lambda i, j, k: (i, k))
        b_spec = pl.BlockSpec((tn, tk), lambda i, j, k: (j, k))
        dims = NT
    else:
        a_spec = pl.BlockSpec((tk, tm), lambda i, j, k: (k, i))
        b_spec = pl.BlockSpec((tk, tn), lambda i, j, k: (k, j))
        dims = TN

    def epilogue(accs, in_refs, out_refs):
        out_refs[0][...] = accs[0].astype(out_dtype)

    return _mm_call(
        name, (m // tm, n // tn, kd // tk), [(a, a_spec), (b, b_spec)], [(0, 1, dims, 0)],
        [(jax.ShapeDtypeStruct((m, n), out_dtype), pl.BlockSpec((tm, tn), lambda i, j, k: (i, j)))],
        [(tm, tn)], epilogue)[0]


def _norm_mod_tile(x, ng, sc, sh):
    r = lax.rsqrt(jnp.mean(x * x, axis=-1, keepdims=True) + RMS_EPS)
    return (x * r * ng) * (1.0 + sc) + sh


def _row_spec(tm, d):
    return pl.BlockSpec((tm, d), lambda i: (i, 0))


def _vec_spec(d):
    return pl.BlockSpec((1, d), lambda i: (0, 0))


def _norm_mod_fwd(x, ng, sc, sh, name):
    t, d = x.shape
    tm = _pick(t, 512, 16)

    def body(x_ref, ng_ref, sc_ref, sh_ref, h_ref):
        h_ref[...] = _norm_mod_tile(x_ref[...], ng_ref[...], sc_ref[...], sh_ref[...]).astype(BF16)

    return pl.pallas_call(
        body, name=name, grid=(t // tm,),
        in_specs=[_row_spec(tm, d), _vec_spec(d), _vec_spec(d), _vec_spec(d)],
        out_specs=_row_spec(tm, d), out_shape=jax.ShapeDtypeStruct((t, d), BF16),
        compiler_params=_params(),
    )(x, ng, sc, sh)


def _norm_mod_bwd(x, ng, sc, sh, dh, dres, name):
    t, d = x.shape
    tm = _pick(t, 256, 16)
    has_res = dres is not None

    def body(*refs):
        if has_res:
            x_ref, ng_ref, sc_ref, sh_ref, dh_ref, dres_ref, dx_ref, dng_ref, dsc_ref, dsh_ref = refs
        else:
            x_ref, ng_ref, sc_ref, sh_ref, dh_ref, dx_ref, dng_ref, dsc_ref, dsh_ref = refs
        _, vjp = jax.vjp(_norm_mod_tile, x_ref[...], ng_ref[...], sc_ref[...], sh_ref[...])
        dx, dng, dsc, dsh = vjp(dh_ref[...].astype(F32))
        if has_res:
            dx = dx + dres_ref[...]
        dx_ref[...] = dx

        @pl.when(pl.program_id(0) == 0)
        def _():
            dng_ref[...] = jnp.zeros_like(dng_ref)
            dsc_ref[...] = jnp.zeros_like(dsc_ref)
            dsh_ref[...] = jnp.zeros_like(dsh_ref)

        dng_ref[...] += dng
        dsc_ref[...] += dsc
        dsh_ref[...] += dsh

    ins = [x, ng, sc, sh, dh] + ([dres] if has_res else [])
    in_specs = [_row_spec(tm, d), _vec_spec(d), _vec_spec(d), _vec_spec(d), _row_spec(tm, d)]
    in_specs += [_row_spec(tm, d)] if has_res else []
    vec = jax.ShapeDtypeStruct((1, d), F32)
    return pl.pallas_call(
        body, name=name, grid=(t // tm,), in_specs=in_specs,
        out_specs=[_row_spec(tm, d), _vec_spec(d), _vec_spec(d), _vec_spec(d)],
        out_shape=[jax.ShapeDtypeStruct((t, d), F32), vec, vec, vec],
        compiler_params=_params(),
    )(*ins)


def _res_mm_fwd(a, w, x, gate, coef, name):
    t, kd = a.shape
    d = w.shape[1]
    tm, tn, tk = _pick(t, 1024, 16), _pick(d, 1024), _pick(kd, 2816)

    def epilogue(accs, in_refs, out_refs):
        f = accs[0]
        out_refs[0][...] = in_refs[2][...] + (coef * in_refs[3][...]) * f
        out_refs[1][...] = f.astype(BF16)

    tile = pl.BlockSpec((tm, tn), lambda i, j, k: (i, j))
    return _mm_call(
        name, (t // tm, d // tn, kd // tk),
        [(a, pl.BlockSpec((tm, tk), lambda i, j, k: (i, k))), (w, pl.BlockSpec((tk, tn), lambda i, j, k: (k, j))),
         (x, tile), (gate, pl.BlockSpec((1, tn), lambda i, j, k: (0, j)))],
        [(0, 1, NN, 0)],
        [(jax.ShapeDtypeStruct((t, d), F32), tile), (jax.ShapeDtypeStruct((t, d), BF16), tile)],
        [(tm, tn)], epilogue)


def _gate_bwd(dxo, f, gate, coef, name):
    t, d = dxo.shape
    tm = _pick(t, 512, 16)

    def body(dxo_ref, f_ref, gate_ref, df_ref, dgate_ref):
        dxo_t = dxo_ref[...]
        df_ref[...] = ((coef * gate_ref[...]) * dxo_t).astype(BF16)

        @pl.when(pl.program_id(0) == 0)
        def _():
            dgate_ref[...] = jnp.zeros_like(dgate_ref)

        dgate_ref[...] += coef * jnp.sum(dxo_t * f_ref[...].astype(F32), axis=0, keepdims=True)

    return pl.pallas_call(
        body, name=name, grid=(t // tm,),
        in_specs=[_row_spec(tm, d), _row_spec(tm, d), _vec_spec(d)],
        out_specs=[_row_spec(tm, d), _vec_spec(d)],
        out_shape=[jax.ShapeDtypeStruct((t, d), BF16), jax.ShapeDtypeStruct((1, d), F32)],
        compiler_params=_params(),
    )(dxo, f, gate)


def _ffn_in_fwd(h, w_in, name):
    t, d = h.shape
    n = w_in.shape[2]
    half = N_DEV // 2
    f = half * n
    tm = _pick(t, 512, 16)

    def epilogue(accs, in_refs, out_refs):
        g, u = accs
        s = jax.nn.sigmoid(g)
        silu = g * s
        out_refs[0][...] = (silu * u).astype(BF16)
        out_refs[1][0] = (u * (s * (1.0 + g * (1.0 - s)))).astype(BF16)
        out_refs[1][1] = silu.astype(BF16)

    return _mm_call(
        name, (half, t // tm, 1),
        [(h, pl.BlockSpec((tm, d), lambda j, i, k: (i, 0))),
         (w_in, pl.BlockSpec((None, d, n), lambda j, i, k: (j, 0, 0))),
         (w_in, pl.BlockSpec((None, d, n), lambda j, i, k: (j + half, 0, 0)))],
        [(0, 1, NN, 0), (0, 2, NN, 1)],
        [(jax.ShapeDtypeStruct((t, f), BF16), pl.BlockSpec((tm, n), lambda j, i, k: (i, j))),
         (jax.ShapeDtypeStruct((2, t, f), BF16), pl.BlockSpec((2, tm, n), lambda j, i, k: (0, i, j)))],
        [(tm, n), (tm, n)], epilogue)


def _ffn_da_bwd(df, w_out2d, gu, name):
    t, d = df.shape
    f = w_out2d.shape[0]
    half = N_DEV // 2
    n = f // half
    tm = _pick(t, 512, 16)
    step = 4 * LANE
    chunks = [(c, min(c + step, n)) for c in range(0, n, step)]

    def body(df_ref, w_ref, gu_ref, o_ref):
        df_t = df_ref[...]
        for c0, c1 in chunks:
            da = _dot(df_t, w_ref[c0:c1, :], NT)
            o_ref[0, :, c0:c1] = (da * gu_ref[0, :, c0:c1].astype(F32)).astype(BF16)
            o_ref[1, :, c0:c1] = (da * gu_ref[1, :, c0:c1].astype(F32)).astype(BF16)

    gu_spec = pl.BlockSpec((2, tm, n), lambda j, i: (0, i, j))
    return pl.pallas_call(
        body, name=name, grid=(half, t // tm),
        in_specs=[pl.BlockSpec((tm, d), lambda j, i: (i, 0)), pl.BlockSpec((n, d), lambda j, i: (j, 0)), gu_spec],
        out_specs=gu_spec, out_shape=jax.ShapeDtypeStruct((2, t, f), BF16), compiler_params=_params(),
    )(df, w_out2d, gu)


def _ffn_dh_bwd(dgu, w_in, name):
    _, t, f = dgu.shape
    d, n = w_in.shape[1], w_in.shape[2]
    half = N_DEV // 2
    tm = _pick(t, 512, 16)

    def epilogue(accs, in_refs, out_refs):
        out_refs[0][...] = accs[0]

    return _mm_call(
        name, (t // tm, 1, half),
        [(dgu, pl.BlockSpec((None, tm, n), lambda i, j, k: (0, i, k))),
         (dgu, pl.BlockSpec((None, tm, n), lambda i, j, k: (1, i, k))),
         (w_in, pl.BlockSpec((None, d, n), lambda i, j, k: (k, 0, 0))),
         (w_in, pl.BlockSpec((None, d, n), lambda i, j, k: (k + half, 0, 0)))],
        [(0, 2, NT, 0), (1, 3, NT, 0)],
        [(jax.ShapeDtypeStruct((t, d), F32), pl.BlockSpec((tm, d), lambda i, j, k: (i, 0)))],
        [(tm, d)], epilogue)[0]


def _ffn_dwin_bwd(h, dgu, name):
    t, d = h.shape
    f = dgu.shape[2]
    half = N_DEV // 2
    n = f // half
    tk = _pick(t, 1024, 16)

    def epilogue(accs, in_refs, out_refs):
        out_refs[0][...] = accs[0].astype(BF16)

    return _mm_call(
        name, (N_DEV, 1, t // tk),
        [(h, pl.BlockSpec((tk, d), lambda j, i, k: (k, 0))),
         (dgu, pl.BlockSpec((None, tk, n), lambda j, i, k: (j // half, k, j % half)))],
        [(0, 1, TN, 0)],
        [(jax.ShapeDtypeStruct((N_DEV, d, n), BF16), pl.BlockSpec((None, d, n), lambda j, i, k: (j, 0, 0)))],
        [(d, n)], epilogue)[0]


def _ffn_parts(tag):
    def w2d(w_out):
        return w_out.reshape(w_out.shape[0] * w_out.shape[1], w_out.shape[2])

    def fwd_in(x, ng, sh, sc, w_in):
        h = _norm_mod_fwd(x, ng, sc, sh, tag + "_norm")
        a, gu = _ffn_in_fwd(h, w_in, tag + "_in")
        return h, a, gu

    def fwd_out(a, w_out, x, gate):
        return _res_mm_fwd(a, w2d(w_out), x, gate, 0.5, tag + "_out")

    def bwd_out(dxo, f1, gate, w_out, gu, a):
        df, dgate = _gate_bwd(dxo, f1, gate, 0.5, tag + "_dgate")
        dgu = _ffn_da_bwd(df, w2d(w_out), gu, tag + "_da")
        f = w_out.shape[0] * w_out.shape[1]
        dw_out = _matmul(a, df, "tn", BF16, tag + "_dwout", tm=_pick(f, 1408, 16), tn=2048, tk=1024)
        return dgate, dgu, dw_out.reshape(w_out.shape)

    def bwd_in(x, ng, sh, sc, w_in, h, dgu, dxo):
        dh = _ffn_dh_bwd(dgu, w_in, tag + "_dh")
        dw_in = _ffn_dwin_bwd(h, dgu, tag + "_dwin")
        dx, dng, dsc, dsh = _norm_mod_bwd(x, ng, sc, sh, dh, dxo, tag + "_dnorm")
        return dx, dng, dsh, dsc, dw_in

    return fwd_in, fwd_out, bwd_out, bwd_in


def _make_ffn_block(tag):
    fwd_in, fwd_out, bwd_out, bwd_in = _ffn_parts(tag)

    @jax.custom_vjp
    def ffn_block(x, ng, sh, sc, gate, w_in, w_out):
        return fwd(x, ng, sh, sc, gate, w_in, w_out)[0]

    def fwd(x, ng, sh, sc, gate, w_in, w_out):
        h, a, gu = fwd_in(x, ng, sh, sc, w_in)
        xo, f1 = fwd_out(a, w_out, x, gate)
        return xo, (x, ng, sh, sc, gate, w_in, w_out, h, a, gu, f1)

    def bwd(res, dxo):
        x, ng, sh, sc, gate, w_in, w_out, h, a, gu, f1 = res
        dgate, dgu, dw_out = bwd_out(dxo, f1, gate, w_out, gu, a)
        dx, dng, dsh, dsc, dw_in = bwd_in(x, ng, sh, sc, w_in, h, dgu, dxo)
        return dx, dng, dsh, dsc, dgate, dw_in, dw_out

    ffn_block.defvjp(fwd, bwd)
    return ffn_block


def _make_norm_proj(tag):
    @jax.custom_vjp
    def norm_proj(x, ng, sh, sc, w):
        return fwd(x, ng, sh, sc, w)[0]

    def fwd(x, ng, sh, sc, w):
        h = _norm_mod_fwd(x, ng, sc, sh, tag + "_norm")
        p = _matmul(h, w, "nn", F32, tag + "_mm", tm=1024, tn=1024, tk=w.shape[0])
        return p, (x, ng, sh, sc, w, h)

    def bwd(res, dp):
        x, ng, sh, sc, w, h = res
        dh = _matmul(dp, w, "nt", F32, tag + "_dh", tm=512, tn=w.shape[0], tk=2048)
        dw = _matmul(h, dp, "tn", BF16, tag + "_dw", tm=w.shape[0], tn=1024, tk=1024)
        dx, dng, dsc, dsh = _norm_mod_bwd(x, ng, sc, sh, dh, None, tag + "_dnorm")
        return dx, dng, dsh, dsc, dw

    norm_proj.defvjp(fwd, bwd)
    return norm_proj


def _make_norm_proj_carry(tag):
    @jax.custom_vjp
    def norm_proj(x, ng, sh, sc, w):
        return fwd(x, ng, sh, sc, w)[0]

    def fwd(x, ng, sh, sc, w):
        h = _norm_mod_fwd(x, ng, sc, sh, tag + "_norm")
        p = _matmul(h, w, "nn", F32, tag + "_mm", tm=1024, tn=1024, tk=w.shape[0])
        return (p, x), (x, ng, sh, sc, w, h)

    def bwd(res, cts):
        x, ng, sh, sc, w, h = res
        dp, dx_carry = cts
        dh = _matmul(dp, w, "nt", F32, tag + "_dh", tm=512, tn=w.shape[0], tk=2048)
        dw = _matmul(h, dp, "tn", BF16, tag + "_dw", tm=w.shape[0], tn=1024, tk=1024)
        dx, dng, dsc, dsh = _norm_mod_bwd(x, ng, sc, sh, dh, dx_carry, tag + "_dnorm")
        return dx, dng, dsh, dsc, dw

    norm_proj.defvjp(fwd, bwd)
    return norm_proj


def _make_split(tag, widths, total):
    offs = [sum(widths[:i]) for i in range(len(widths))]

    def concat_call(pieces):
        t = pieces[0].shape[0]
        tm = _pick(t, 256, 16)

        def body(*refs):
            o_ref = refs[-1]
            for ref, off, wd in zip(refs[:-1], offs, widths):
                o_ref[:, off:off + wd] = ref[...]
            end = offs[-1] + widths[-1]
            if end < total:
                o_ref[:, end:] = jnp.zeros((tm, total - end), F32)

        return pl.pallas_call(
            body, name=tag + "_concat", grid=(t // tm,),
            in_specs=[_row_spec(tm, wd) for wd in widths], out_specs=_row_spec(tm, total),
            out_shape=jax.ShapeDtypeStruct((t, total), F32), compiler_params=_params(),
        )(*pieces)

    @jax.custom_vjp
    def split(p):
        return tuple(p[:, off:off + wd] for off, wd in zip(offs, widths))

    def fwd(p):
        return split(p), None

    def bwd(_, cts):
        return (concat_call(list(cts)),)

    split.defvjp(fwd, bwd)
    return split


def _make_res_proj(tag):
    @jax.custom_vjp
    def res_proj(a, w, x, gate):
        return fwd(a, w, x, gate)[0]

    def fwd(a, w, x, gate):
        xo, f = _res_mm_fwd(a, w, x, gate, 1.0, tag + "_mm")
        return xo, (a, w, gate, f)

    def bwd(res, dxo):
        a, w, gate, f = res
        df, dgate = _gate_bwd(dxo, f, gate, 1.0, tag + "_dgate")
        da = _matmul(df, w, "nt", BF16, tag + "_da", tm=1024, tn=1024, tk=2048)
        dw = _matmul(a, df, "tn", BF16, tag + "_dw", tm=1024, tn=2048, tk=1024)
        return da, dw, dxo, dgate

    res_proj.defvjp(fwd, bwd)
    return res_proj


def _make_small_mm(tag):
    @jax.custom_vjp
    def small_mm(a, w):
        return _matmul(a, w, "nn", F32, tag + "_mm", tm=a.shape[0], tn=768, tk=w.shape[0])

    def fwd(a, w):
        return small_mm(a, w), (a, w)

    def bwd(res, dr):
        a, w = res
        da = _matmul(dr, w, "nt", F32, tag + "_da", tm=a.shape[0], tn=w.shape[0], tk=768)
        dw = _matmul(a, dr, "tn", F32, tag + "_dw", tm=1024, tn=768, tk=a.shape[0])
        return da, dw

    small_mm.defvjp(fwd, bwd)
    return small_mm


def _ret_chunk_terms(lg, c, reverse):
    row = lax.broadcasted_iota(jnp.int32, (c, c), 0).astype(F32)
    col = lax.broadcasted_iota(jnp.int32, (c, c), 1).astype(F32)
    pos = lax.broadcasted_iota(jnp.int32, (c, 1), 0).astype(F32)
    if reverse:
        diff = col - row
        mask = diff > 0.0
        e_exp = float(c) - pos
        f_exp = pos
    else:
        diff = row - col
        mask = diff >= 0.0
        e_exp = pos + 1.0
        f_exp = float(c - 1) - pos
    diffm = jnp.where(mask, diff, 0.0)
    dm = jnp.where(mask, jnp.exp(lg * diffm), 0.0)
    return diffm, dm, e_exp, jnp.exp(lg * e_exp), f_exp, jnp.exp(lg * f_exp)


def _ret_mask_t(lg, c, reverse):
    row = lax.broadcasted_iota(jnp.int32, (c, c), 0).astype(F32)
    col = lax.broadcasted_iota(jnp.int32, (c, c), 1).astype(F32)
    diff = row - col if reverse else col - row
    mask = diff > 0.0 if reverse else diff >= 0.0
    return jnp.where(mask, jnp.exp(lg * jnp.where(mask, diff, 0.0)), 0.0)


def _lane0(val):
    lane = lax.broadcasted_iota(jnp.int32, (1, LANE), 1)
    return jnp.where(lane == 0, val, 0.0)


RET_HEAD_BLOCK = 4


def _make_ret_dir(tag, reverse):
    hb = RET_HEAD_BLOCK

    def heads_spec(nc, width, flip):
        if flip:
            return pl.BlockSpec((hb, RET_CHUNK, width), lambda h, t: (h, nc - 1 - t, 0))
        return pl.BlockSpec((hb, RET_CHUNK, width), lambda h, t: (h, t, 0))

    def state_spec(nc, flip):
        if flip:
            return pl.BlockSpec((hb, None, RET_DKP, RET_DV), lambda h, t: (h, nc - 1 - t, 0, 0))
        return pl.BlockSpec((hb, None, RET_DKP, RET_DV), lambda h, t: (h, t, 0, 0))

    lg_spec = pl.BlockSpec((hb, 1, LANE), lambda h, t: (h, 0, 0))
    s0_spec = pl.BlockSpec((hb, RET_DKP, RET_DV), lambda h, t: (h, 0, 0))

    def fwd_call(q, k, v, lgb, s0):
        hh, ll, _ = q.shape
        c = RET_CHUNK
        nc = ll // c

        def body(q_ref, k_ref, v_ref, lg_ref, s0_ref, y_ref, sall_ref, s_scr):
            @pl.when(pl.program_id(1) == 0)
            def _():
                s_scr[...] = s0_ref[...]

            for b in range(hb):
                lg = lg_ref[b][:, :1]
                _, dm, _, xi, _, zeta = _ret_chunk_terms(lg, c, reverse)
                q_t, k_t, v_t = q_ref[b], k_ref[b], v_ref[b]
                s = s_scr[b]
                p = _dot(q_t, k_t, NT) * dm
                y_ref[b] = _dot(p, v_t, NN) + _dot(q_t * xi, s, NN)
                sall_ref[b] = s
                s_scr[b] = jnp.exp(lg * float(c)) * s + _dot(k_t * zeta, v_t, TN)

        return pl.pallas_call(
            body, name=tag + "_fwd", grid=(hh // hb, nc),
            in_specs=[heads_spec(nc, RET_DKP,reverse), heads_spec(nc, RET_DKP,reverse),
                      heads_spec(nc, RET_DV, reverse), lg_spec, s0_spec],
            out_specs=[heads_spec(nc, RET_DV, reverse), state_spec(nc, reverse)],
            out_shape=[jax.ShapeDtypeStruct((hh, ll, RET_DV), F32),
                       jax.ShapeDtypeStruct((hh, nc, RET_DKP, RET_DV), F32)],
            scratch_shapes=[pltpu.VMEM((hb, RET_DKP, RET_DV), F32)],
            compiler_params=_params(),
        )(q, k, v, lgb, s0)

    def bwd_call(q, k, v, lgb, sall, dy):
        hh, ll, _ = q.shape
        c = RET_CHUNK
        nc = ll // c
        flip = not reverse

        def body(q_ref, k_ref, v_ref, lg_ref, sall_ref, dy_ref, dq_ref, dk_ref, dv_ref, dlg_ref, ds0_ref, ds_scr):
            @pl.when(pl.program_id(1) == 0)
            def _():
                ds_scr[...] = jnp.zeros_like(ds_scr)
                dlg_ref[...] = jnp.zeros_like(dlg_ref)

            def total(m):
                return jnp.sum(jnp.sum(m, axis=1, keepdims=True), axis=0, keepdims=True)

            for b in range(hb):
                lg = lg_ref[b][:, :1]
                diffm, dm, e_exp, xi, f_exp, zeta = _ret_chunk_terms(lg, c, reverse)
                q_t, k_t, v_t, dy_t = q_ref[b], k_ref[b], v_ref[b], dy_ref[b]
                s = sall_ref[b]
                dsn = ds_scr[b]
                a = _dot(q_t, k_t, NT)
                da = _dot(dy_t, v_t, NT) * dm
                dm_t = _ret_mask_t(lg, c, reverse)
                a_t = _dot(k_t, q_t, NT)
                da_t = _dot(v_t, dy_t, NT) * dm_t
                g = _dot(dy_t, s, NT)
                hm = _dot(v_t, dsn, NT)
                dq_ref[b] = _dot(da, k_t, NN) + xi * g
                dk_ref[b] = _dot(da_t, q_t, NN) + zeta * hm
                dv_ref[b] = _dot(a_t * dm_t, dy_t, NN) + _dot(k_t * zeta, dsn, NN)
                gc = jnp.exp(lg * float(c))
                ds_scr[b] = gc * dsn + _dot(q_t * xi, dy_t, TN)
                dl = (total(da * a * diffm) + total(e_exp * xi * q_t * g)
                      + float(c) * gc * total(s * dsn) + total(f_exp * zeta * k_t * hm))
                dlg_ref[b] += _lane0(dl)

            @pl.when(pl.program_id(1) == nc - 1)
            def _():
                ds0_ref[...] = ds_scr[...]

        return pl.pallas_call(
            body, name=tag + "_bwd", grid=(hh // hb, nc),
            in_specs=[heads_spec(nc, RET_DKP,flip), heads_spec(nc, RET_DKP,flip), heads_spec(nc, RET_DV, flip),
                      lg_spec, state_spec(nc, flip), heads_spec(nc, RET_DV, flip)],
            out_specs=[heads_spec(nc, RET_DKP,flip), heads_spec(nc, RET_DKP,flip), heads_spec(nc, RET_DV, flip),
                       lg_spec, s0_spec],
            out_shape=[jax.ShapeDtypeStruct((hh, ll, RET_DKP), F32), jax.ShapeDtypeStruct((hh, ll, RET_DKP), F32),
                       jax.ShapeDtypeStruct((hh, ll, RET_DV), F32), jax.ShapeDtypeStruct((hh, 1, LANE), F32),
                       jax.ShapeDtypeStruct((hh, RET_DKP, RET_DV), F32)],
            scratch_shapes=[pltpu.VMEM((hb, RET_DKP, RET_DV), F32)],
            compiler_params=_params(),
        )(q, k, v, lgb, sall, dy)

    @jax.custom_vjp
    def ret_dir(q, k, v, lgb, s0):
        return fwd_call(q, k, v, lgb, s0)[0]

    def fwd(q, k, v, lgb, s0):
        y, sall = fwd_call(q, k, v, lgb, s0)
        return y, (q, k, v, lgb, sall)

    def bwd(res, dy):
        q, k, v, lgb, sall = res
        return tuple(bwd_call(q, k, v, lgb, sall, dy))

    ret_dir.defvjp(fwd, bwd)
    return ret_dir


def _make_ctx_state(tag, reverse):
    hb = RET_HEAD_BLOCK
    c = RET_CHUNK
    k_spec = pl.BlockSpec((hb, c, RET_DKP), lambda h: (h, 0, 0))
    v_spec = pl.BlockSpec((hb, c, RET_DV), lambda h: (h, 0, 0))
    lg_spec = pl.BlockSpec((hb, 1, LANE), lambda h: (h, 0, 0))
    s_spec = pl.BlockSpec((hb, RET_DKP, RET_DV), lambda h: (h, 0, 0))

    def fwd_call(k, v, lgb):
        hh = k.shape[0]

        def body(k_ref, v_ref, lg_ref, s_ref):
            for b in range(hb):
                _, _, _, _, _, zeta = _ret_chunk_terms(lg_ref[b][:, :1], c, reverse)
                s_ref[b] = _dot(k_ref[b] * zeta, v_ref[b], TN)

        return pl.pallas_call(
            body, name=tag + "_fwd", grid=(hh // hb,), in_specs=[k_spec, v_spec, lg_spec], out_specs=s_spec,
            out_shape=jax.ShapeDtypeStruct((hh, RET_DKP, RET_DV), F32), compiler_params=_params(),
        )(k, v, lgb)

    def bwd_call(k, v, lgb, ds):
        hh = k.shape[0]

        def body(k_ref, v_ref, lg_ref, ds_ref, dk_ref, dv_ref, dlg_ref):
            for b in range(hb):
                _, _, _, _, f_exp, zeta = _ret_chunk_terms(lg_ref[b][:, :1], c, reverse)
                k_t, v_t, ds = k_ref[b], v_ref[b], ds_ref[b]
                hm = _dot(v_t, ds, NT)
                dk_ref[b] = zeta * hm
                dv_ref[b] = _dot(k_t * zeta, ds, NN)
                tot = jnp.sum(jnp.sum(f_exp * zeta * k_t * hm, axis=1, keepdims=True), axis=0, keepdims=True)
                dlg_ref[b] = _lane0(tot)

        return pl.pallas_call(
            body, name=tag + "_bwd", grid=(hh // hb,), in_specs=[k_spec, v_spec, lg_spec, s_spec],
            out_specs=[k_spec, v_spec, lg_spec],
            out_shape=[jax.ShapeDtypeStruct(k.shape, F32), jax.ShapeDtypeStruct(v.shape, F32),
                       jax.ShapeDtypeStruct((hh, 1, LANE), F32)],
            compiler_params=_params(),
        )(k, v, lgb, ds)

    @jax.custom_vjp
    def ctx_state(k, v, lgb):
        return fwd_call(k, v, lgb)

    def fwd(k, v, lgb):
        return fwd_call(k, v, lgb), (k, v, lgb)

    def bwd(res, ds):
        return tuple(bwd_call(*res, ds))

    ctx_state.defvjp(fwd, bwd)
    return ctx_state


def _rope_tables_call(name, n, inv, shift, axial):
    tm = _pick(n, 1024, 8)
    inv_lane = jnp.tile(inv, LANE // inv.shape[0])[None, :]

    def body(inv_ref, cos_ref, s1_ref, s2_ref):
        t = lax.broadcasted_iota(jnp.int32, (tm, LANE), 0) + pl.program_id(0) * tm
        lane = lax.broadcasted_iota(jnp.int32, (tm, LANE), 1)
        if axial:
            pos = jnp.where(lane % (2 * MLA_ROPE // 2) < MLA_ROPE // 2, t // GRID_W, t % GRID_W)
        else:
            pos = t
        ang = pos.astype(F32) * inv_ref[...]
        sin = jnp.sin(ang)
        first = lane % (2 * shift) < shift
        cos_ref[...] = jnp.cos(ang)
        s1_ref[...] = jnp.where(first, -sin, 0.0)
        s2_ref[...] = jnp.where(first, 0.0, sin)

    tab = jax.ShapeDtypeStruct((n, LANE), F32)
    return tuple(pl.pallas_call(
        body, name=name, grid=(n // tm,), in_specs=[_vec_spec(LANE)], out_specs=[_row_spec(tm, LANE)] * 3,
        out_shape=[tab, tab, tab], compiler_params=_params(),
    )(inv_lane))


def _ret_tables(n_lat):
    inv = RET_ROPE_BASE ** (-jnp.arange(0, RET_DK, 2, dtype=F32) / RET_DK)
    return _rope_tables_call("ret_tables", n_lat, inv, RET_DK // 2, False)


def _make_ret_pack(tag, n_lat, n_ctx):
    hh = RET_HEADS
    tm = MLA_PACK_ROWS
    k_scale = RET_DK ** -0.5
    shift = RET_DK // 2
    tabs = _ret_tables(n_lat)

    def low_lanes():
        return lax.broadcasted_iota(jnp.int32, (1, LANE), 1) < RET_DK

    def rows(width):
        return pl.BlockSpec((tm, width), lambda i: (i, 0))

    def heads(width):
        return pl.BlockSpec((hh, tm, width), lambda i: (0, i, 0))

    def split_pairs(src_ref, dst_ref, scale, rope):
        keep = low_lanes()
        for j in range(hh // 2):
            blk = src_ref[:, LANE * j:LANE * (j + 1)]
            if scale != 1.0:
                blk = blk * scale
            if rope is not None:
                blk = _rope128(blk, *rope, shift=shift)
            dst_ref[2 * j] = jnp.where(keep, blk, 0.0)
            dst_ref[2 * j + 1] = jnp.where(keep, pltpu.roll(blk, RET_DK, 1), 0.0)

    def merge_pairs(src_refs, dst_ref, scale, rope):
        keep = low_lanes()
        for j in range(hh // 2):
            even = sum(r[2 * j] for r in src_refs)
            odd = sum(r[2 * j + 1] for r in src_refs)
            g = jnp.where(keep, even, pltpu.roll(odd, RET_DK, 1))
            if rope is not None:
                g = _rope128_t(g, *rope, shift=shift)
            dst_ref[:, LANE * j:LANE * (j + 1)] = g * scale if scale != 1.0 else g

    def pack_call(name, n, q, k, v, rope):
        with_q = q is not None

        def body(*refs):
            refs = list(refs)
            q_ref = refs.pop(0) if with_q else None
            k_ref, v_ref = refs.pop(0), refs.pop(0)
            tab = tuple(r[...] for r in refs[:3]) if rope else None
            outs = refs[3:] if rope else refs
            if with_q:
                split_pairs(q_ref, outs[0], 1.0, tab)
                outs = outs[1:]
            split_pairs(k_ref, outs[0], k_scale, tab)
            for h in range(hh):
                outs[1][h] = v_ref[:, RET_DV * h:RET_DV * (h + 1)]

        ins = ([q] if with_q else []) + [k, v] + (list(tabs) if rope else [])
        in_specs = ([rows(q.shape[1])] if with_q else []) + [rows(k.shape[1]), rows(v.shape[1])]
        in_specs += [rows(LANE)] * 3 if rope else []
        n_out = 3 if with_q else 2
        return pl.pallas_call(
            body, name=name, grid=(n // tm,), in_specs=in_specs,
            out_specs=[heads(RET_DKP)] * (n_out - 1) + [heads(RET_DV)],
            out_shape=[jax.ShapeDtypeStruct((hh, n, RET_DKP), F32)] * (n_out - 1)
            + [jax.ShapeDtypeStruct((hh, n, RET_DV), F32)],
            compiler_params=_params(),
        )(*ins)

    def unpack_call(name, n, dqs, dks, dvs, rope):
        with_q = len(dqs) > 0
        uses = len(dks)

        def body(*refs):
            refs = list(refs)
            dq_refs = [refs.pop(0) for _ in range(len(dqs))]
            dk_refs = [refs.pop(0) for _ in range(uses)]
            dv_refs = [refs.pop(0) for _ in range(uses)]
            tab = tuple(r[...] for r in refs[:3]) if rope else None
            outs = refs[3:] if rope else refs
            if with_q:
                merge_pairs(dq_refs, outs[0], 1.0, tab)
                outs = outs[1:]
            merge_pairs(dk_refs, outs[0], k_scale, tab)
            for h in range(hh):
                outs[1][:, RET_DV * h:RET_DV * (h + 1)] = sum(r[h] for r in dv_refs)

        ins = list(dqs) + list(dks) + list(dvs) + (list(tabs) if rope else [])
        in_specs = [heads(RET_DKP)] * (len(dqs) + uses) + [heads(RET_DV)] * uses + ([rows(LANE)] * 3 if rope else [])
        n_out = 3 if with_q else 2
        return pl.pallas_call(
            body, name=name, grid=(n // tm,), in_specs=in_specs,
            out_specs=[rows(hh * RET_DK)] * (n_out - 1) + [rows(hh * RET_DV)],
            out_shape=[jax.ShapeDtypeStruct((n, hh * RET_DK), F32)] * (n_out - 1)
            + [jax.ShapeDtypeStruct((n, hh * RET_DV), F32)],
            compiler_params=_params(),
        )(*ins)

    @jax.custom_vjp
    def ret_pack(rq, rk, rv, crk, crv):
        q, k, v = pack_call(tag + "_lat", n_lat, rq, rk, rv, True)
        k_c, v_c = pack_call(tag + "_ctx", n_ctx, None, crk, crv, False)
        return (q, k, v), (q, k, v), (k_c, v_c), (k_c, v_c)

    def fwd(rq, rk, rv, crk, crv):
        return ret_pack(rq, rk, rv, crk, crv), None

    def bwd(_, cts):
        lat_f, lat_b, ctx_f, ctx_b = cts
        drq, drk, drv = unpack_call(tag + "_dlat", n_lat, [lat_f[0], lat_b[0]], [lat_f[1], lat_b[1]],
                                    [lat_f[2], lat_b[2]], True)
        dcrk, dcrv = unpack_call(tag + "_dctx", n_ctx, [], [ctx_f[0], ctx_b[0]], [ctx_f[1], ctx_b[1]], False)
        return drq, drk, drv, dcrk, dcrv

    ret_pack.defvjp(fwd, bwd)
    return ret_pack


def _ret_out_tile(y, g):
    mu = jnp.mean(y, axis=-1, keepdims=True)
    var = jnp.mean(jnp.square(y - mu), axis=-1, keepdims=True)
    return (g * jax.nn.sigmoid(g)) * ((y - mu) * lax.rsqrt(var + GN_EPS))


def _make_ret_out(tag):
    def specs(tm):
        y_spec = pl.BlockSpec((None, tm, RET_DV), lambda h, i: (h, i, 0))
        g_spec = pl.BlockSpec((tm, RET_DV), lambda h, i: (i, h))
        return y_spec, g_spec

    def fwd_call(yf, yb, g):
        hh, n, _ = yf.shape
        tm = _pick(n, 1024, 16)
        y_spec, g_spec = specs(tm)

        def body(yf_ref, yb_ref, g_ref, o_ref):
            o_ref[...] = _ret_out_tile(yf_ref[...] + yb_ref[...], g_ref[...]).astype(BF16)

        return pl.pallas_call(
            body, name=tag + "_fwd", grid=(hh, n // tm), in_specs=[y_spec, y_spec, g_spec], out_specs=g_spec,
            out_shape=jax.ShapeDtypeStruct((n, hh * RET_DV), BF16), compiler_params=_params(),
        )(yf, yb, g)

    def bwd_call(yf, yb, g, do):
        hh, n, _ = yf.shape
        tm = _pick(n, 1024, 16)
        y_spec, g_spec = specs(tm)

        def body(yf_ref, yb_ref, g_ref, do_ref, dy_ref, dg_ref):
            _, vjp = jax.vjp(_ret_out_tile, yf_ref[...] + yb_ref[...], g_ref[...])
            dy, dg = vjp(do_ref[...].astype(F32))
            dy_ref[...] = dy
            dg_ref[...] = dg

        return pl.pallas_call(
            body, name=tag + "_bwd", grid=(hh, n // tm), in_specs=[y_spec, y_spec, g_spec, g_spec],
            out_specs=[y_spec, g_spec],
            out_shape=[jax.ShapeDtypeStruct(yf.shape, F32), jax.ShapeDtypeStruct(g.shape, F32)],
            compiler_params=_params(),
        )(yf, yb, g, do)

    @jax.custom_vjp
    def ret_out(yf, yb, g):
        return fwd_call(yf, yb, g)

    def fwd(yf, yb, g):
        return fwd_call(yf, yb, g), (yf, yb, g)

    def bwd(res, do):
        dy, dg = bwd_call(*res, do)
        return dy, dy, dg

    ret_out.defvjp(fwd, bwd)
    return ret_out


MLA_DQ_PAD = 2 * LANE
MLA_PACK_ROWS = 256


def _rope128(x, cos, s1, s2, shift=16):
    return x * cos + pltpu.roll(x, LANE - shift, 1) * s1 + pltpu.roll(x, shift, 1) * s2


def _rope128_t(g, cos, s1, s2, shift=16):
    return g * cos + pltpu.roll(g * s1, shift, 1) + pltpu.roll(g * s2, LANE - shift, 1)


def _axial_tables(n_lat):
    half = MLA_ROPE // 2
    inv = AXIAL_BASE ** (-jnp.arange(0, half, 2, dtype=F32) / half)
    return _rope_tables_call("mla_tables", n_lat, inv, half // 2, True)


def _make_mla_pack(tag, n_lat, n_ctx, scale):
    hh = MLA_HEADS
    tm = MLA_PACK_ROWS
    ll = n_lat + n_ctx
    rope0 = hh * MLA_NOPE
    tabs = _axial_tables(n_lat)

    def rope_lanes():
        return lax.broadcasted_iota(jnp.int32, (1, LANE), 1) < MLA_ROPE

    def rows(width):
        return pl.BlockSpec((tm, width), lambda i: (i, 0))

    def heads(width, off):
        return pl.BlockSpec((hh, tm, width), lambda i: (0, i + off, 0))

    def heads_t(width, off):
        return pl.BlockSpec((hh, width, tm), lambda i: (0, 0, i + off))

    def put_kv(kv_ref, kr_rot, k_ref, v_ref, kt_ref, vt_ref):
        kr_b = kr_rot.astype(BF16)
        kr_t = jnp.transpose(kr_rot).astype(BF16)
        for h in range(hh):
            k_nope = kv_ref[:, 2 * LANE * h:2 * LANE * h + MLA_NOPE]
            val = kv_ref[:, 2 * LANE * h + MLA_NOPE:2 * LANE * (h + 1)]
            k_ref[h, :, :MLA_NOPE] = k_nope.astype(BF16)
            k_ref[h, :, MLA_NOPE:] = kr_b
            v_ref[h] = val.astype(BF16)
            kt_ref[h, :MLA_NOPE, :] = jnp.transpose(k_nope).astype(BF16)
            kt_ref[h, MLA_NOPE:, :] = kr_t
            vt_ref[h] = jnp.transpose(val).astype(BF16)

    def fwd_lat(qp, kv, kr):
        def body(qp_ref, kv_ref, kr_ref, cos_ref, s1_ref, s2_ref, q_ref, k_ref, v_ref, kt_ref, vt_ref):
            cos, s1, s2 = cos_ref[...], s1_ref[...], s2_ref[...]
            keep = rope_lanes()
            for j in range(hh // 2):
                rot = _rope128(qp_ref[:, rope0 + LANE * j:rope0 + LANE * (j + 1)], cos, s1, s2)
                q_ref[2 * j, :, MLA_NOPE:] = jnp.where(keep, rot, 0.0).astype(BF16)
                q_ref[2 * j + 1, :, MLA_NOPE:] = jnp.where(keep, pltpu.roll(rot, MLA_ROPE, 1), 0.0).astype(BF16)
            for h in range(hh):
                q_ref[h, :, :MLA_NOPE] = qp_ref[:, MLA_NOPE * h:MLA_NOPE * (h + 1)].astype(BF16)
            kr_rot = jnp.where(keep, _rope128(kr_ref[...], cos, s1, s2), 0.0)
            put_kv(kv_ref, kr_rot, k_ref, v_ref, kt_ref, vt_ref)

        return pl.pallas_call(
            body, name=tag + "_lat", grid=(n_lat // tm,),
            in_specs=[rows(qp.shape[1]), rows(kv.shape[1]), rows(LANE), rows(LANE), rows(LANE), rows(LANE)],
            out_specs=[heads(MLA_DQ_PAD, 0), heads(MLA_DQ_PAD, 0), heads(MLA_V, 0), heads_t(MLA_DQ_PAD, 0),
                       heads_t(MLA_V, 0)],
            out_shape=[jax.ShapeDtypeStruct((hh, n_lat, MLA_DQ_PAD), BF16),
                       jax.ShapeDtypeStruct((hh, ll, MLA_DQ_PAD), BF16), jax.ShapeDtypeStruct((hh, ll, MLA_V), BF16),
                       jax.ShapeDtypeStruct((hh, MLA_DQ_PAD, ll), BF16), jax.ShapeDtypeStruct((hh, MLA_V, ll), BF16)],
            compiler_params=_params(),
        )(qp, kv, kr, *tabs)

    def fwd_ctx(kv_c, kr_c, bufs):
        def body(kv_ref, kr_ref, k_in, v_in, kt_in, vt_in, k_ref, v_ref, kt_ref, vt_ref):
            kr_rot = jnp.where(rope_lanes(), kr_ref[...], 0.0)
            put_kv(kv_ref, kr_rot, k_ref, v_ref, kt_ref, vt_ref)

        any_spec = pl.BlockSpec(memory_space=pl.ANY)
        off = n_lat // tm
        return pl.pallas_call(
            body, name=tag + "_ctx", grid=(n_ctx // tm,),
            in_specs=[rows(kv_c.shape[1]), rows(LANE)] + [any_spec] * 4,
            out_specs=[heads(MLA_DQ_PAD, off), heads(MLA_V, off), heads_t(MLA_DQ_PAD, off), heads_t(MLA_V, off)],
            out_shape=[jax.ShapeDtypeStruct(b.shape, BF16) for b in bufs],
            input_output_aliases={2: 0, 3: 1, 4: 2, 5: 3}, compiler_params=_params(),
        )(kv_c, kr_c, *bufs)

    def take_kv(dk_ref, dv_ref, dkv_ref):
        dkr = jnp.zeros((tm, LANE), F32)
        for h in range(hh):
            dkv_ref[:, 2 * LANE * h:2 * LANE * h + MLA_NOPE] = dk_ref[h, :, :MLA_NOPE].astype(F32)
            dkv_ref[:, 2 * LANE * h + MLA_NOPE:2 * LANE * (h + 1)] = dv_ref[h].astype(F32)
            dkr = dkr + dk_ref[h, :, MLA_NOPE:].astype(F32)
        return jnp.where(rope_lanes(), dkr, 0.0)

    def bwd_lat(dqt, dk, dv, qp_width, kv_width):
        def body(dqt_ref, dk_ref, dv_ref, cos_ref, s1_ref, s2_ref, dqp_ref, dkv_ref, dkr_ref):
            cos, s1, s2 = cos_ref[...], s1_ref[...], s2_ref[...]
            keep = rope_lanes()
            for j in range(hh // 2):
                even = jnp.transpose(dqt_ref[2 * j]) * scale
                odd = jnp.transpose(dqt_ref[2 * j + 1]) * scale
                dqp_ref[:, MLA_NOPE * 2 * j:MLA_NOPE * (2 * j + 1)] = even[:, :MLA_NOPE]
                dqp_ref[:, MLA_NOPE * (2 * j + 1):MLA_NOPE * (2 * j + 2)] = odd[:, :MLA_NOPE]
                g = jnp.where(keep, even[:, MLA_NOPE:], pltpu.roll(odd[:, MLA_NOPE:], MLA_ROPE, 1))
                dqp_ref[:, rope0 + LANE * j:rope0 + LANE * (j + 1)] = _rope128_t(g, cos, s1, s2)
            dkr_ref[...] = jnp.where(keep, _rope128_t(take_kv(dk_ref, dv_ref, dkv_ref), cos, s1, s2), 0.0)

        return pl.pallas_call(
            body, name=tag + "_dlat", grid=(n_lat // tm,),
            in_specs=[pl.BlockSpec((hh, MLA_DQ_PAD, tm), lambda i: (0, 0, i)),
                      heads(MLA_DQ_PAD, 0), heads(MLA_V, 0), rows(LANE), rows(LANE), rows(LANE)],
            out_specs=[rows(qp_width), rows(kv_width), rows(LANE)],
            out_shape=[jax.ShapeDtypeStruct((n_lat, qp_width), F32), jax.ShapeDtypeStruct((n_lat, kv_width), F32),
                       jax.ShapeDtypeStruct((n_lat, LANE), F32)],
            compiler_params=_params(),
        )(dqt, dk, dv, *tabs)

    def bwd_ctx(dk, dv, kv_width):
        def body(dk_ref, dv_ref, dkv_ref, dkr_ref):
            dkr_ref[...] = take_kv(dk_ref, dv_ref, dkv_ref)

        off = n_lat // tm
        return pl.pallas_call(
            body, name=tag + "_dctx", grid=(n_ctx // tm,),
            in_specs=[heads(MLA_DQ_PAD, off), heads(MLA_V, off)],
            out_specs=[rows(kv_width), rows(LANE)],
            out_shape=[jax.ShapeDtypeStruct((n_ctx, kv_width), F32), jax.ShapeDtypeStruct((n_ctx, LANE), F32)],
            compiler_params=_params(),
        )(dk, dv)

    def pack(qp, kv, kr, kv_c, kr_c):
        q, *bufs = fwd_lat(qp, kv, kr)
        return (q, *fwd_ctx(kv_c, kr_c, bufs))

    def unpack(dqt, dk, dv):
        qp_width, kv_width = hh * (MLA_NOPE + MLA_ROPE), hh * (MLA_NOPE + MLA_V)
        dqp, dkv, dkr = bwd_lat(dqt, dk, dv, qp_width, kv_width)
        dkv_c, dkr_c = bwd_ctx(dk, dv, kv_width)
        return dqp, dkv, dkr, dkv_c, dkr_c

    return pack, unpack


def _make_mla(tag, n_lat, n_ctx):
    scale = (MLA_NOPE + MLA_ROPE) ** -0.5
    pack, unpack = _make_mla_pack(tag + "pack", n_lat, n_ctx, scale)
    attn_fwd, attn_delta, attn_bwd = _make_attention(tag, scale, MLA_NOPE + MLA_ROPE)

    @jax.custom_vjp
    def mla(qp, kv, kr, kv_c, kr_c):
        q, k, _, _, vt = pack(qp, kv, kr, kv_c, kr_c)
        return attn_fwd(q, k, vt)[0]

    def fwd(qp, kv, kr, kv_c, kr_c):
        q, k, v, kt, vt = pack(qp, kv, kr, kv_c, kr_c)
        o, lse = attn_fwd(q, k, vt)
        return o, (q, k, kt, v, o, lse)

    def bwd(res, do):
        q, k, kt, v, o, lse = res
        delta = attn_delta(o, do, q.shape[0])
        dqt, dk, dv = attn_bwd(q, k, kt, v, do, lse, delta)
        return unpack(dqt, dk, dv)

    mla.defvjp(fwd, bwd)
    return mla


def _make_attention(tag, scale, dq_live=None):
    neg_big = -1e30
    log2e = 1.4426950408889634
    sub = 256

    def fwd_call(q, k, vt):
        hh, n, dq = q.shape
        dv, ll = vt.shape[1], vt.shape[2]
        tq, tk = _pick(n, 2048), _pick(ll, 1408)
        sb = sub if tk % sub == 0 else tk
        c2 = scale * log2e
        k_steps = ll // tk

        def body(q_ref, k_ref, vt_ref, o_ref, lse_ref, m_scr, l_scr, acc_scr, s_scr, p_scr):
            j = pl.program_id(2)

            @pl.when(j == 0)
            def _():
                m_scr[...] = jnp.full_like(m_scr, neg_big)
                l_scr[...] = jnp.zeros_like(l_scr)
                acc_scr[...] = jnp.zeros_like(acc_scr)

            q_t = q_ref[...]
            m_prev = m_scr[...]
            m_new = m_prev
            for kk in range(tk // sb):
                rows = slice(kk * sb, (kk + 1) * sb)
                s_t = _dot(k_ref[rows, :], q_t, NT)
                s_scr[rows, :] = s_t
                m_new = jnp.maximum(m_new, jnp.max(s_t, axis=0, keepdims=True))
            mc = m_new * c2
            l_part = jnp.zeros_like(m_new)
            for kk in range(tk // sb):
                rows = slice(kk * sb, (kk + 1) * sb)
                p_t = jnp.exp2(s_scr[rows, :] * c2 - mc)
                l_part = l_part + jnp.sum(p_t, axis=0, keepdims=True)
                p_scr[rows, :] = p_t.astype(BF16)
            alpha = jnp.exp2((m_prev - m_new) * c2)
            l_scr[...] = alpha * l_scr[...] + l_part
            acc_scr[...] = alpha * acc_scr[...] + _dot(vt_ref[...], p_scr[...], NN)
            m_scr[...] = m_new

            @pl.when(j == k_steps - 1)
            def _():
                o_ref[...] = jnp.transpose(acc_scr[...] / l_scr[...]).astype(BF16)
                lse_ref[...] = m_scr[...] * scale + jnp.log(l_scr[...])

        return pl.pallas_call(
            body, name=tag + "_fwd", grid=(hh, n // tq, k_steps),
            in_specs=[pl.BlockSpec((None, tq, dq), lambda h, i, j: (h, i, 0)),
                      pl.BlockSpec((None, tk, dq), lambda h, i, j: (h, j, 0)),
                      pl.BlockSpec((None, dv, tk), lambda h, i, j: (h, 0, j))],
            out_specs=[pl.BlockSpec((tq, dv), lambda h, i, j: (i, h)),
                       pl.BlockSpec((None, 1, tq), lambda h, i, j: (h, 0, i))],
            out_shape=[jax.ShapeDtypeStruct((n, hh * dv), BF16), jax.ShapeDtypeStruct((hh, 1, n), F32)],
            scratch_shapes=[pltpu.VMEM((1, tq), F32), pltpu.VMEM((1, tq), F32), pltpu.VMEM((dv, tq), F32),
                            pltpu.VMEM((tk, tq), F32), pltpu.VMEM((tk, tq), BF16)],
            compiler_params=_params(),
        )(q, k, vt)

    def delta_call(o, do, hh):
        n = o.shape[0]
        dv = o.shape[1] // hh
        tq = _pick(n, 1024)

        def body(o_ref, do_ref, d_ref):
            prod_t = jnp.transpose(o_ref[...].astype(F32) * do_ref[...].astype(F32))
            d_ref[...] = jnp.sum(prod_t, axis=0, keepdims=True)

        spec = pl.BlockSpec((tq, dv), lambda h, i: (i, h))
        return pl.pallas_call(
            body, name=tag + "_delta", grid=(hh, n // tq), in_specs=[spec, spec],
            out_specs=pl.BlockSpec((None, 1, tq), lambda h, i: (h, 0, i)),
            out_shape=jax.ShapeDtypeStruct((hh, 1, n), F32), compiler_params=_params(),
        )(o, do)

    def bwd_call(q, k, kt, v, do, lse, delta):
        hh, n, dq = q.shape
        ll, dv = k.shape[1], v.shape[2]
        tq, tk = _pick(n, 2048), _pick(ll, 1408)
        sb = tk
        c2 = scale * log2e
        q_steps = n // tq
        live = dq_live or dq

        def body(q_ref, k_ref, kt_ref, v_ref, do_ref, lse_ref, d_ref, dqt_ref, dk_ref, dv_ref, dk_scr, dv_scr):
            j = pl.program_id(1)
            i = pl.program_id(2)

            @pl.when(i == 0)
            def _():
                dk_scr[...] = jnp.zeros_like(dk_scr)
                dv_scr[...] = jnp.zeros_like(dv_scr)

            q_t, do_t = q_ref[...], do_ref[...]
            lse2 = lse_ref[...] * log2e
            delta_t = d_ref[...]
            dq_part = None
            for kk in range(tk // sb):
                rows = slice(kk * sb, (kk + 1) * sb)
                s_t = _dot(k_ref[rows, :], q_t, NT)
                p_t = jnp.exp2(s_t * c2 - lse2)
                ds_t = p_t * (_dot(v_ref[rows, :], do_t, NT) - delta_t)
                dv_scr[rows, :] += _dot(p_t, do_t, NN)
                dk_scr[rows, :] += _dot(ds_t, q_t, NN)
                part = _dot(kt_ref[:live, rows], ds_t, NN)
                dq_part = part if dq_part is None else dq_part + part
            cols = pl.ds(pl.multiple_of(i * tq, tq), tq)

            @pl.when(j == 0)
            def _():
                dqt_ref[:live, cols] = dq_part
                if live < dq:
                    dqt_ref[live:, cols] = jnp.zeros((dq - live, tq), F32)

            @pl.when(j > 0)
            def _():
                dqt_ref[:live, cols] += dq_part

            @pl.when(i == q_steps - 1)
            def _():
                dk_ref[...] = (dk_scr[...] * scale).astype(BF16)
                dv_ref[...] = dv_scr[...].astype(BF16)

        return pl.pallas_call(
            body, name=tag + "_bwd", grid=(hh, ll // tk, q_steps),
            in_specs=[pl.BlockSpec((None, tq, dq), lambda h, j, i: (h, i, 0)),
                      pl.BlockSpec((None, tk, dq), lambda h, j, i: (h, j, 0)),
                      pl.BlockSpec((None, dq, tk), lambda h, j, i: (h, 0, j)),
                      pl.BlockSpec((None, tk, dv), lambda h, j, i: (h, j, 0)),
                      pl.BlockSpec((tq, dv), lambda h, j, i: (i, h)),
                      pl.BlockSpec((None, 1, tq), lambda h, j, i: (h, 0, i)),
                      pl.BlockSpec((None, 1, tq), lambda h, j, i: (h, 0, i))],
            out_specs=[pl.BlockSpec((None, dq, n), lambda h, j, i: (h, 0, 0)),
                       pl.BlockSpec((None, tk, dq), lambda h, j, i: (h, j, 0)),
                       pl.BlockSpec((None, tk, dv), lambda h, j, i: (h, j, 0))],
            out_shape=[jax.ShapeDtypeStruct((hh, dq, n), F32), jax.ShapeDtypeStruct((hh, ll, dq), BF16),
                       jax.ShapeDtypeStruct((hh, ll, dv), BF16)],
            scratch_shapes=[pltpu.VMEM((tk, dq), F32), pltpu.VMEM((tk, dv), F32)],
            compiler_params=_params(),
        )(q, k, kt, v, do, lse, delta)

    return fwd_call, delta_call, bwd_call


def _loss_tile(x, g, tgt):
    r = lax.rsqrt(jnp.mean(x * x, axis=-1, keepdims=True) + RMS_EPS)
    err = x * r * g - tgt
    per_tok = jnp.mean(err * err, axis=-1, keepdims=True)
    return 0.5 * jnp.sum(per_tok, axis=0, keepdims=True)


def _make_final_loss(tag):
    def fwd_call(x, g, tgt):
        t, d = x.shape
        tm = _pick(t, 512, 16)

        def body(x_ref, g_ref, t_ref, l_ref):
            l_ref[...] = jnp.broadcast_to(_loss_tile(x_ref[...], g_ref[...], t_ref[...]), (1, LANE))

        parts = pl.pallas_call(
            body, name=tag + "_fwd", grid=(t // tm,),
            in_specs=[_row_spec(tm, d), _vec_spec(d), _row_spec(tm, d)],
            out_specs=pl.BlockSpec((None, 1, LANE), lambda i: (i, 0, 0)),
            out_shape=jax.ShapeDtypeStruct((t // tm, 1, LANE), F32), compiler_params=_params(),
        )(x, g, tgt)
        return jnp.sum(parts[:, 0, 0])

    def bwd_call(x, g, tgt, dl):
        t, d = x.shape
        tm = _pick(t, 256, 16)

        def body(x_ref, g_ref, t_ref, dl_ref, dx_ref, dg_ref):
            _, vjp = jax.vjp(_loss_tile, x_ref[...], g_ref[...], t_ref[...])
            dx, dg, _ = vjp(dl_ref[...])
            dx_ref[...] = dx

            @pl.when(pl.program_id(0) == 0)
            def _():
                dg_ref[...] = jnp.zeros_like(dg_ref)

            dg_ref[...] += dg

        return pl.pallas_call(
            body, name=tag + "_bwd", grid=(t // tm,),
            in_specs=[_row_spec(tm, d), _vec_spec(d), _row_spec(tm, d), pl.BlockSpec((1, 1), lambda i: (0, 0))],
            out_specs=[_row_spec(tm, d), _vec_spec(d)],
            out_shape=[jax.ShapeDtypeStruct((t, d), F32), jax.ShapeDtypeStruct((1, d), F32)],
            compiler_params=_params(),
        )(x, g, tgt, dl)

    @jax.custom_vjp
    def final_loss(x, g, tgt):
        return fwd_call(x, g, tgt)

    def fwd(x, g, tgt):
        return fwd_call(x, g, tgt), (x, g, tgt)

    def bwd(res, dl):
        x, g, tgt = res
        dx, dg = bwd_call(x, g, tgt, dl.reshape(1, 1).astype(F32))
        return dx, dg, jnp.zeros_like(tgt)

    final_loss.defvjp(fwd, bwd)
    return final_loss


def _exchange(arrays, gather, name):
    n = len(arrays)

    def body(*refs):
        ins, outs = refs[:n], refs[n:2 * n]
        send_sems, recv_sems, local_sems = refs[2 * n:]
        me = 4 * lax.axis_index("x") + 2 * lax.axis_index("y") + lax.axis_index("c")

        def remote(a, d, wait_side=False):
            peer = (me + d) % N_DEV
            origin = (me + N_DEV - d) % N_DEV
            src = ins[a] if gather else ins[a].at[peer]
            dst = outs[a].at[origin if wait_side else me]
            return pltpu.make_async_remote_copy(
                src_ref=src, dst_ref=dst, send_sem=send_sems.at[a, d - 1], recv_sem=recv_sems.at[a, d - 1],
                device_id=(peer // 4, (peer // 2) % 2, peer % 2), device_id_type=pl.DeviceIdType.MESH)

        def local(a):
            src = ins[a] if gather else ins[a].at[me]
            return pltpu.make_async_copy(src, outs[a].at[me], local_sems.at[a])

        for a in range(n):
            for d in range(1, N_DEV):
                remote(a, d).start()
            local(a).start()
        for a in range(n):
            local(a).wait()
            for d in range(1, N_DEV):
                remote(a, d, wait_side=True).wait_recv()
                remote(a, d).wait_send()

    out_shape = []
    for arr in arrays:
        shape = (N_DEV,) + arr.shape if gather else arr.shape
        out_shape.append(jax.ShapeDtypeStruct(shape, arr.dtype))
    any_spec = pl.BlockSpec(memory_space=pl.ANY)
    return pl.pallas_call(
        body, name=name, in_specs=[any_spec] * n, out_specs=[any_spec] * n, out_shape=out_shape,
        scratch_shapes=[pltpu.SemaphoreType.DMA((n, N_DEV - 1)), pltpu.SemaphoreType.DMA((n, N_DEV - 1)),
                        pltpu.SemaphoreType.DMA((n,))],
        compiler_params=pltpu.CompilerParams(has_side_effects=True),
    )(*arrays)


def _split_copy(ins, lands, send_sems, recv_sems, a, d, gather, wait_side):
    me = 4 * lax.axis_index("x") + 2 * lax.axis_index("y") + lax.axis_index("c")
    peer = (me + d) % N_DEV
    origin = (me + N_DEV - d) % N_DEV
    return pltpu.make_async_remote_copy(
        src_ref=ins[a] if gather else ins[a].at[peer], dst_ref=lands[a].at[origin if wait_side else me],
        send_sem=send_sems.at[a * (N_DEV - 1) + d - 1], recv_sem=recv_sems.at[a * (N_DEV - 1) + d - 1],
        device_id=(peer // 4, (peer // 2) % 2, peer % 2), device_id_type=pl.DeviceIdType.MESH)


def _exchange_start(srcs, lands, after, gather, name):
    n = len(srcs)

    def body(*refs):
        ins, lnd = refs[:n], refs[n:2 * n]
        send_sems, recv_sems = refs[2 * n + 1], refs[2 * n + 2]
        for a in range(n):
            for d in range(1, N_DEV):
                _split_copy(ins, lnd, send_sems, recv_sems, a, d, gather, False).start()

    hbm = pl.BlockSpec(memory_space=pltpu.HBM)
    sem = pl.BlockSpec(memory_space=pltpu.SEMAPHORE)
    bufs = [pltpu.with_memory_space_constraint(t, pltpu.HBM) for t in list(srcs) + list(lands) + [after]]
    res = pl.pallas_call(
        body, name=name,
        in_specs=[hbm] * (2 * n + 1), out_specs=[sem, sem] + [hbm] * (2 * n + 1),
        out_shape=[pltpu.SemaphoreType.DMA((n * (N_DEV - 1),)), pltpu.SemaphoreType.DMA((n * (N_DEV - 1),))]
        + [pltpu.HBM(t.shape, t.dtype) for t in bufs],
        input_output_aliases={i: 2 + i for i in range(2 * n + 1)},
        compiler_params=pltpu.CompilerParams(has_side_effects=pltpu.SideEffectType.DATAFLOW_SIDE_EFFECTING),
    )(*bufs)
    return res[0], res[1], res[2:2 + n], res[2 + n:2 + 2 * n], res[-1]


def _exchange_wait(send_sems, recv_sems, srcs, lands, after, gather, name):
    n = len(srcs)

    def body(*refs):
        ins, lnd = refs[:n], refs[n:2 * n]
        send_sems_ref, recv_sems_ref = refs[2 * n], refs[2 * n + 1]
        for a in range(n):
            for d in range(1, N_DEV):
                _split_copy(ins, lnd, send_sems_ref, recv_sems_ref, a, d, gather, False).wait_send()
                _split_copy(ins, lnd, send_sems_ref, recv_sems_ref, a, d, gather, True).wait_recv()

    hbm = pl.BlockSpec(memory_space=pltpu.HBM)
    sem = pl.BlockSpec(memory_space=pltpu.SEMAPHORE)
    bufs = list(srcs) + list(lands)
    res = pl.pallas_call(
        body, name=name,
        in_specs=[hbm] * (2 * n) + [sem, sem, pl.BlockSpec(memory_space=pl.ANY)],
        out_specs=[hbm] * (2 * n),
        out_shape=[pltpu.HBM(t.shape, t.dtype) for t in bufs],
        input_output_aliases={i: i for i in range(2 * n)},
        compiler_params=pltpu.CompilerParams(has_side_effects=pltpu.SideEffectType.DATAFLOW_SIDE_EFFECTING),
    )(*bufs, send_sems, recv_sems, after)
    return res[n:]


def _own_slot(block, me):
    empty = lax.empty((N_DEV,) + block.shape, block.dtype)
    return lax.dynamic_update_slice(empty, block[None], (me,) + (0,) * block.ndim)


def _coords():
    return lax.axis_index("x"), lax.axis_index("y"), lax.axis_index("c")


def _other_chips(x, y):
    return [(1 - x, y), (x, 1 - y), (1 - x, 1 - y)]


def _gather_two_level(arrays, name):
    n = len(arrays)

    def body(*refs):
        ins, outs = refs[:n], refs[n:2 * n]
        send_sems, recv_sems, local_sems = refs[2 * n:]
        x, y, c = _coords()
        me, sib = (x, y, c), (x, y, 1 - c)
        chips = _other_chips(x, y)

        def copy(a, k, block, to, from_input=False):
            slot = 4 * block[0] + 2 * block[1] + block[2]
            return pltpu.make_async_remote_copy(
                src_ref=ins[a] if from_input else outs[a].at[slot], dst_ref=outs[a].at[slot],
                send_sem=send_sems.at[a, k], recv_sem=recv_sems.at[a, k],
                device_id=to, device_id_type=pl.DeviceIdType.MESH)

        def local(a):
            return pltpu.make_async_copy(ins[a], outs[a].at[4 * x + 2 * y + c], local_sems.at[a])

        for a in range(n):
            for j, chip in enumerate(chips):
                copy(a, 1 + j, me, (*chip, c), True).start()
            copy(a, 0, me, sib, True).start()
            local(a).start()
        for a in range(n):
            for j, chip in enumerate(chips):
                copy(a, 1 + j, (*chip, c), me).wait_recv()
                copy(a, 4 + j, (*chip, c), sib).start()
        for a in range(n):
            copy(a, 0, sib, me).wait_recv()
            for j, chip in enumerate(chips):
                copy(a, 4 + j, (*chip, 1 - c), me).wait_recv()
            for k in range(N_DEV - 1):
                copy(a, k, me, sib, True).wait_send()
            local(a).wait()

    any_spec = pl.BlockSpec(memory_space=pl.ANY)
    return pl.pallas_call(
        body, name=name, in_specs=[any_spec] * n, out_specs=[any_spec] * n,
        out_shape=[jax.ShapeDtypeStruct((N_DEV,) + arr.shape, arr.dtype) for arr in arrays],
        scratch_shapes=[pltpu.SemaphoreType.DMA((n, N_DEV - 1)), pltpu.SemaphoreType.DMA((n, N_DEV - 1)),
                        pltpu.SemaphoreType.DMA((n,))],
        compiler_params=pltpu.CompilerParams(has_side_effects=True),
    )(*arrays)


def _make_gather_op(tag):
    @jax.custom_vjp
    def gather_op(xl):
        return _exchange([xl], True, tag + "_gather")[0]

    def fwd(xl):
        return gather_op(xl), None

    def bwd(_, g):
        return (jnp.sum(_exchange([g], False, tag + "_scatter")[0], axis=0),)

    gather_op.defvjp(fwd, bwd)
    return gather_op


def _adamw(gstack, w, m, v, name):
    s, r, cn = gstack.shape
    tr = _pick(r, max(8, (2 * 1024 * 1024) // (4 * cn) // 8 * 8), 8)
    c1 = 1.0 - ADAM_B1 ** ADAM_STEP
    c2 = 1.0 - ADAM_B2 ** ADAM_STEP

    def body(g_ref, w_ref, m_ref, v_ref, go_ref, d_ref, mo_ref, vo_ref):
        g = g_ref[0].astype(F32)
        for q in range(1, s):
            g = g + g_ref[q].astype(F32)
        m_new = ADAM_B1 * m_ref[...] + (1.0 - ADAM_B1) * g
        v_new = ADAM_B2 * v_ref[...] + (1.0 - ADAM_B2) * (g * g)
        go_ref[...] = g
        mo_ref[...] = m_new
        vo_ref[...] = v_new
        d_ref[...] = -ADAM_LR * ((m_new / c1) / (jnp.sqrt(v_new / c2) + ADAM_EPS) + ADAM_WD * w_ref[...])

    tile = pl.BlockSpec((tr, cn), lambda i: (i, 0))
    out = jax.ShapeDtypeStruct((r, cn), F32)
    return pl.pallas_call(
        body, name=name, grid=(r // tr,),
        in_specs=[pl.BlockSpec((s, tr, cn), lambda i: (0, i, 0)), tile, tile, tile],
        out_specs=[tile, tile, tile, tile], out_shape=[out, out, out, out],
        compiler_params=_params(),
    )(gstack, w, m, v)


def _cols_from_stack(w):
    return jnp.swapaxes(w, 0, 1).reshape(w.shape[1], N_DEV * w.shape[2])


def _ada_vectors(p, silu_c_all, me):
    d = p["c_ctx"].shape[0]
    n_a = p["ada_w"].shape[1]
    a_in = jnp.concatenate([silu_c_all, jax.nn.silu(p["c_ctx"])[None, :], jnp.zeros((7, d), F32)], axis=0)
    b_loc = lax.dynamic_slice(p["ada_b"], (0, me * n_a), (1, n_a))
    r_loc = _make_small_mm("ada")(a_in, p["ada_w"]) + b_loc
    r_full = _make_gather_op("ada")(r_loc)
    m_lat = lax.dynamic_index_in_dim(r_full, me, axis=1, keepdims=False).reshape(N_MOD, 1, d)
    m_ctx = r_full[:, N_DEV, :].reshape(N_MOD, 1, d)
    return m_lat, m_ctx


def _stage_a_fwd(p_ada, x, ctx, ng, w_in, wait_w_out, silu_c_all, me):
    (m_lat, m_ctx), vjp_ada = jax.vjp(lambda q: _ada_vectors(q, silu_c_all, me), p_ada)
    lat, cx = _ffn_parts("ffn1"), _ffn_parts("ffn1c")
    h, a, gu = lat[0](x, ng, m_lat[0], m_lat[1], w_in)
    hc, ac, guc = cx[0](ctx, ng, m_ctx[0], m_ctx[1], w_in)
    w_out = wait_w_out(a)
    x1, f1 = lat[1](a, w_out, x, m_lat[2])
    c1, f1c = cx[1](ac, w_out, ctx, m_ctx[2])
    res = dict(vjp_ada=vjp_ada, m_lat=m_lat, m_ctx=m_ctx, x=x, ctx=ctx, ng=ng, w_in=w_in, w_out=w_out,
               lat=(h, a, gu, f1), cx=(hc, ac, guc, f1c))
    return (x1, c1, m_lat, m_ctx), res


def _stage_a_bwd(res, dx1, dc1, dm_lat, dm_ctx, start_grads):
    lat, cx = _ffn_parts("ffn1"), _ffn_parts("ffn1c")
    m_lat, m_ctx, ng, w_in, w_out = res["m_lat"], res["m_ctx"], res["ng"], res["w_in"], res["w_out"]
    hc, ac, guc, f1c = res["cx"]
    dgate_c, dgu_c, dw_out_c = cx[2](dc1, f1c, m_ctx[2], w_out, guc, ac)
    _, dng_c, dsh_c, dsc_c, dw_in_c = cx[3](res["ctx"], ng, m_ctx[0], m_ctx[1], w_in, hc, dgu_c, dc1)
    h, a, gu, f1 = res["lat"]
    dgate, dgu, dw_out = lat[2](dx1, f1, m_lat[2], w_out, gu, a)
    dgu = start_grads("ffn1_w_out", dw_out + dw_out_c, dgu)
    dw_in = _ffn_dwin_bwd(h, dgu, "ffn1_dwin") + dw_in_c
    dgu = start_grads("ffn1_w_in", dw_in, dgu)
    dh = _ffn_dh_bwd(dgu, w_in, "ffn1_dh")
    dx, dng, dsc, dsh = _norm_mod_bwd(res["x"], ng, m_lat[1], m_lat[0], dh, dx1, "ffn1_dnorm")

    def rows(dsh_, dsc_, dgate_):
        return jnp.concatenate([dsh_, dsc_, dgate_, jnp.zeros((N_MOD - 3,) + dsh_.shape, F32)[:, 0]], axis=0)[:, None, :]

    (g_ada,) = res["vjp_ada"]((dm_lat + rows(dsh, dsc, dgate), dm_ctx + rows(dsh_c, dsc_c, dgate_c)))
    return g_ada, dx, dng + dng_c


def _stage_b(p, x1, c1, m_lat, m_ctx):
    n_lat, d = x1.shape
    n_ctx = c1.shape[0]
    w_mix = jnp.pad(_cols_from_stack(p["mix_w_in"]), ((0, 0), (0, MIX_IN_PAD - MIX_IN)))
    proj, x1 = _make_norm_proj_carry("mix")(x1, p["norm2_g"], m_lat[3], m_lat[4], w_mix)
    proj_c = _make_norm_proj("mixc")(c1, p["norm2_g"], m_ctx[3], m_ctx[4], w_mix)
    widths = SPLITS[:6] + (LANE,)
    rq, rk, rv, rg, cq, ckv, kr = _make_split("mixsplit", widths, MIX_IN_PAD)(proj)
    _, crk, crv, _, _, cckv, ckr = _make_split("mixsplitc", widths, MIX_IN_PAD)(proj_c)

    zq = jnp.zeros((1, MLA_Q_RANK), F32)
    zkv = jnp.zeros((1, MLA_KV_RANK), F32)
    w_uq3 = _cols_from_stack(p["mla_w_uq"]).reshape(MLA_Q_RANK, MLA_HEADS, MLA_NOPE + MLA_ROPE)
    w_uq = jnp.concatenate([w_uq3[:, :, :MLA_NOPE].reshape(MLA_Q_RANK, -1),
                            w_uq3[:, :, MLA_NOPE:].reshape(MLA_Q_RANK, -1)], axis=1)
    w_ukv = _cols_from_stack(p["mla_w_ukv"])
    q = _make_norm_proj("uq")(cq, p["mla_q_norm_g"], zq, zq, w_uq)
    kv = _make_norm_proj("ukv")(ckv, p["mla_kv_norm_g"], zkv, zkv, w_ukv)
    kv_c = _make_norm_proj("ukvc")(cckv, p["mla_kv_norm_g"], zkv, zkv, w_ukv)

    lg_f = jax.nn.log_sigmoid(p["ret_decay_fwd"][0])
    lg_b = jax.nn.log_sigmoid(p["ret_decay_bwd"][0])
    lat_f, lat_b, ctx_f, ctx_b = _make_ret_pack("retpack", n_lat, n_ctx)(rq, rk, rv, crk, crv)
    assert n_ctx == RET_CHUNK, "the context prefix is one retention chunk"

    def lanes(lg):
        return jnp.broadcast_to(lg[:, None, None], (RET_HEADS, 1, LANE))

    s0_f = _make_ctx_state("retcf", False)(*ctx_f, lanes(lg_f))
    s0_b = _make_ctx_state("retcb", True)(*ctx_b, lanes(lg_b))
    y_f = _make_ret_dir("retf", False)(*lat_f, lanes(lg_f), s0_f)
    y_b = _make_ret_dir("retb", True)(*lat_b, lanes(lg_b), s0_b)
    ret_o = _make_ret_out("reto")(y_f, y_b, rg)

    mla_o = _make_mla("mla", n_lat, n_ctx)(q, kv, kr, kv_c, ckr)

    w_mo = p["mix_w_out"].reshape(-1, d)
    return _make_res_proj("mixo")(jnp.concatenate([ret_o, mla_o], axis=-1), w_mo, x1, m_lat[5])


def _stage_c(p, x2, m_lat, tgt):
    x3 = _make_ffn_block("ffn2")(x2, p["norm3_g"], m_lat[6], m_lat[7], m_lat[8], p["ffn2_w_in"], p["ffn2_w_out"])
    return _make_final_loss("loss")(x3, p["final_norm_g"], tgt)


FIRST = ("ffn1_w_in", "ffn1_w_out")
MID = ("mix_w_in", "mla_w_uq", "mla_w_ukv", "mix_w_out")
LAST = ("ffn2_w_in", "ffn2_w_out")
BIG = FIRST + MID + LAST
SMALL = ("c_ctx", "ada_b", "norm1_g", "norm2_g", "ret_decay_fwd", "ret_decay_bwd", "mla_q_norm_g",
         "mla_kv_norm_g", "norm3_g", "final_norm_g")
WEIGHTS = ("c_ctx", "ada_w", "ada_b", "norm1_g", "ffn1_w_in", "ffn1_w_out", "norm2_g", "mix_w_in", "ret_decay_fwd",
           "ret_decay_bwd", "mla_q_norm_g", "mla_w_uq", "mla_kv_norm_g", "mla_w_ukv", "mix_w_out", "norm3_g",
           "ffn2_w_in", "ffn2_w_out", "final_norm_g")


def _pack(parts):
    flat = jnp.concatenate([t.reshape(-1) for t in parts])
    pad = (-flat.shape[0]) % LANE
    return jnp.pad(flat, (0, pad)).reshape(1, -1)


def _unpack(flat, like):
    out, off = [], 0
    for t in like:
        out.append(flat[0, off:off + t.size].reshape(t.shape))
        off += t.size
    return out


def kernel(x, c, ctx, c_ctx, ada_w, ada_b, norm1_g, ffn1_w_in, ffn1_w_out, norm2_g, mix_w_in, ret_decay_fwd, ret_decay_bwd, mla_q_norm_g, mla_w_uq, mla_kv_norm_g, mla_w_ukv, mix_w_out, norm3_g, ffn2_w_in, ffn2_w_out, final_norm_g, loss_target, m_c_ctx, m_ada_w, m_ada_b, m_norm1_g, m_ffn1_w_in, m_ffn1_w_out, m_norm2_g, m_mix_w_in, m_ret_decay_fwd, m_ret_decay_bwd, m_mla_q_norm_g, m_mla_w_uq, m_mla_kv_norm_g, m_mla_w_ukv, m_mix_w_out, m_norm3_g, m_ffn2_w_in, m_ffn2_w_out, m_final_norm_g, v_c_ctx, v_ada_w, v_ada_b, v_norm1_g, v_ffn1_w_in, v_ffn1_w_out, v_norm2_g, v_mix_w_in, v_ret_decay_fwd, v_ret_decay_bwd, v_mla_q_norm_g, v_mla_w_uq, v_mla_kv_norm_g, v_mla_w_ukv, v_mix_w_out, v_norm3_g, v_ffn2_w_in, v_ffn2_w_out, v_final_norm_g):
    w = dict(c_ctx=c_ctx, ada_w=ada_w, ada_b=ada_b, norm1_g=norm1_g, ffn1_w_in=ffn1_w_in, ffn1_w_out=ffn1_w_out,
             norm2_g=norm2_g, mix_w_in=mix_w_in, ret_decay_fwd=ret_decay_fwd, ret_decay_bwd=ret_decay_bwd,
             mla_q_norm_g=mla_q_norm_g, mla_w_uq=mla_w_uq, mla_kv_norm_g=mla_kv_norm_g, mla_w_ukv=mla_w_ukv,
             mix_w_out=mix_w_out, norm3_g=norm3_g, ffn2_w_in=ffn2_w_in, ffn2_w_out=ffn2_w_out,
             final_norm_g=final_norm_g)
    mom = dict(c_ctx=m_c_ctx, ada_w=m_ada_w, ada_b=m_ada_b, norm1_g=m_norm1_g, ffn1_w_in=m_ffn1_w_in,
               ffn1_w_out=m_ffn1_w_out, norm2_g=m_norm2_g, mix_w_in=m_mix_w_in, ret_decay_fwd=m_ret_decay_fwd,
               ret_decay_bwd=m_ret_decay_bwd, mla_q_norm_g=m_mla_q_norm_g, mla_w_uq=m_mla_w_uq,
               mla_kv_norm_g=m_mla_kv_norm_g, mla_w_ukv=m_mla_w_ukv, mix_w_out=m_mix_w_out, norm3_g=m_norm3_g,
               ffn2_w_in=m_ffn2_w_in, ffn2_w_out=m_ffn2_w_out, final_norm_g=m_final_norm_g)
    var = dict(c_ctx=v_c_ctx, ada_w=v_ada_w, ada_b=v_ada_b, norm1_g=v_norm1_g, ffn1_w_in=v_ffn1_w_in,
               ffn1_w_out=v_ffn1_w_out, norm2_g=v_norm2_g, mix_w_in=v_mix_w_in, ret_decay_fwd=v_ret_decay_fwd,
               ret_decay_bwd=v_ret_decay_bwd, mla_q_norm_g=v_mla_q_norm_g, mla_w_uq=v_mla_w_uq,
               mla_kv_norm_g=v_mla_kv_norm_g, mla_w_ukv=v_mla_w_ukv, mix_w_out=v_mix_w_out, norm3_g=v_norm3_g,
               ffn2_w_in=v_ffn2_w_in, ffn2_w_out=v_ffn2_w_out, final_norm_g=v_final_norm_g)
    me = 4 * lax.axis_index("x") + 2 * lax.axis_index("y") + lax.axis_index("c")

    shard = {k: w[k][0].astype(BF16) for k in BIG}
    first = _gather_two_level([shard["ffn1_w_in"], jax.nn.silu(c)], "weights_gather")
    silu_c_all = first[-1][:, 0, :]

    def start_gather(names, after, name):
        return _exchange_start([shard[k] for k in names], [_own_slot(shard[k], me) for k in names], after, True, name)

    wout_start = start_gather(FIRST[1:], first[0], "ffn1_wout_start")
    mid_start = start_gather(MID, wout_start[4], "mixer_weights_start")
    last_start = start_gather(LAST, mid_start[4], "ffn2_weights_start")

    def wait_w_out(after):
        return _exchange_wait(*wout_start[:4], after, True, "ffn1_wout_wait")[0]

    pa = dict(ada_w=ada_w[0], c_ctx=c_ctx, ada_b=ada_b)
    (x1, c1, m_lat, m_ctx), res_a = _stage_a_fwd(pa, x[0], ctx[0], norm1_g, last_start[4], wait_w_out, silu_c_all, me)

    mid = _exchange_wait(mid_start[0], mid_start[1], mid_start[2], mid_start[3], x1, True, "mixer_weights_wait")
    pb = dict(zip(MID, mid))
    for k in ("norm2_g", "mla_q_norm_g", "mla_kv_norm_g", "ret_decay_fwd", "ret_decay_bwd"):
        pb[k] = w[k]
    x2, vjp_b = jax.vjp(_stage_b, pb, x1, c1, m_lat, m_ctx)

    last = _exchange_wait(last_start[0], last_start[1], last_start[2], last_start[3], x2, True, "ffn2_weights_wait")
    pc = dict(zip(LAST, last), norm3_g=norm3_g, final_norm_g=final_norm_g[None, :])
    loss_local, vjp_c = jax.vjp(lambda q, t, m: _stage_c(q, t, m, loss_target[0]), pc, x2, m_lat)

    gc, dx2, dm_c = vjp_c(jnp.ones((), F32))
    last_scat = _exchange_start([gc[k] for k in LAST],
                                [_own_slot(lax.dynamic_index_in_dim(gc[k], me, 0, False), me) for k in LAST],
                                dx2, False, "ffn2_grads_start")
    gb, dx1, dc1, dm_b, dmc_b = vjp_b(last_scat[4])
    mid_scat = _exchange_start([gb[k] for k in MID],
                               [_own_slot(lax.dynamic_index_in_dim(gb[k], me, 0, False), me) for k in MID],
                               dx1, False, "mixer_grads_start")
    first_scat = {}

    def start_grads(name, dw, after):
        first_scat[name] = _exchange_start(
            [dw], [_own_slot(lax.dynamic_index_in_dim(dw, me, 0, False), me)], after, False, name + "_grads_start")
        return first_scat[name][4]

    g_ada, dx, dng1 = _stage_a_bwd(res_a, mid_scat[4], dc1, dm_b + dm_c, dmc_b, start_grads)
    grads = {**g_ada, **gb, **gc, "x": dx, "norm1_g": dng1}
    grads["final_norm_g"] = grads["final_norm_g"][0]

    exchanged = {k: _exchange_wait(*first_scat[k][:4], dx, False, k + "_grads_wait")[0] for k in FIRST}
    exchanged.update(zip(LAST, _exchange_wait(*last_scat[:4], dx, False, "ffn2_grads_wait")))
    exchanged.update(zip(MID, _exchange_wait(*mid_scat[:4], dx, False, "mixer_grads_wait")))
    zero1 = [jnp.zeros((1,), F32)]
    small_like = zero1 + [w[k] for k in SMALL]
    small_all = _exchange([_pack([loss_local.reshape(1)] + [grads[k] for k in SMALL])], True, "small_grads_gather")[0]
    loss = jnp.sum(small_all[:, 0, 0])

    out_g, out_d, out_m, out_v = {}, {}, {}, {}

    def update(name, gstack, shape2d):
        res = _adamw(gstack, w[name].reshape(shape2d), mom[name].reshape(shape2d), var[name].reshape(shape2d),
                     "adamw_" + name)
        out_g[name], out_d[name], out_m[name], out_v[name] = [t.reshape(w[name].shape) for t in res]

    for k in BIG:
        update(k, exchanged[k], exchanged[k].shape[1:])
    update("ada_w", grads["ada_w"][None], ada_w.shape[1:])
    res = _adamw(small_all, _pack(small_like), _pack(zero1 + [mom[k] for k in SMALL]),
                 _pack(zero1 + [var[k] for k in SMALL]), "adamw_small")
    for dst, flat in zip((out_g, out_d, out_m, out_v), res):
        for k, t in zip(SMALL, _unpack(flat, small_like)[1:]):
            dst[k] = t

    return (loss, grads["x"][None], *[out_g[k] for k in WEIGHTS], *[out_d[k] for k in WEIGHTS],
            *[out_m[k] for k in WEIGHTS], *[out_v[k] for k in WEIGHTS])
```

```python
import functools

import jax
import jax.numpy as jnp
from jax import lax
from jax.experimental import pallas as pl
from jax.experimental.pallas import tpu as pltpu

F32 = jnp.float32
BF16 = jnp.bfloat16

N_DEV = 8
MESH_AXES = ("x", "y", "c")

GRID_W = 64
N_MOD = 9
RET_HEADS = 8
RET_DK = 64
RET_DV = 128
RET_CHUNK = 256
RET_ROPE_BASE = 10000.0
MLA_HEADS = 8
MLA_Q_RANK = 512
MLA_KV_RANK = 256
MLA_NOPE = 128
MLA_ROPE = 64
MLA_V = 128
AXIAL_BASE = 10000.0
RMS_EPS = 1e-6
GN_EPS = 1e-5
SPLITS = (RET_HEADS * RET_DK, RET_HEADS * RET_DK, RET_HEADS * RET_DV, RET_HEADS * RET_DV,
          MLA_Q_RANK, MLA_KV_RANK, MLA_ROPE)
MIX_IN = sum(SPLITS)
MIX_IN_PAD = 4096

ADAM_LR = 0.001
ADAM_B1 = 0.9
ADAM_B2 = 0.999
ADAM_EPS = 1e-08
ADAM_WD = 0.01
ADAM_STEP = 10

LANE = 128
RET_DKP = LANE
VMEM_LIMIT_BYTES = 56 * 1024 * 1024

NN = ((1,), (0,))
NT = ((1,), (1,))
TN = ((0,), (0,))


def _pick(dim, target, align=LANE):
    t = min(dim, target)
    t -= t % align
    while t >= align:
        if dim % t == 0:
            return t
        t -= align
    return dim


def _params():
    return pltpu.CompilerParams(vmem_limit_bytes=VMEM_LIMIT_BYTES)


def _dot(a, b, dims):
    return lax.dot_general(a.astype(BF16), b.astype(BF16), (dims, ((), ())), preferred_element_type=F32)


def _mm_call(name, grid, ins, pairs, outs, acc_shapes, epilogue):
    n_in, n_out = len(ins), len(outs)
    k_axis = len(grid) - 1
    k_steps = grid[k_axis]

    def body(*refs):
        in_refs = refs[:n_in]
        out_refs = refs[n_in:n_in + n_out]
        accs = refs[n_in + n_out:]
        k = pl.program_id(k_axis)

        @pl.when(k == 0)
        def _():
            for acc in accs:
                acc[...] = jnp.zeros_like(acc)

        for ai, bi, dims, ci in pairs:
            accs[ci][...] += _dot(in_refs[ai][...], in_refs[bi][...], dims)

        @pl.when(k == k_steps - 1)
        def _():
            epilogue([acc[...] for acc in accs], in_refs, out_refs)

    res = pl.pallas_call(
        body, name=name, grid=grid,
        in_specs=[s for _, s in ins], out_specs=[s for _, s in outs],
        out_shape=[s for s, _ in outs],
        scratch_shapes=[pltpu.VMEM(s, F32) for s in acc_shapes],
        compiler_params=_params(),
    )(*[a for a, _ in ins])
    return res


def _matmul(a, b, mode, out_dtype, name, tm=1024, tn=1024, tk=512):
    if mode == "nn":
        (m, kd), n = a.shape, b.shape[1]
    elif mode == "nt":
        (m, kd), n = a.shape, b.shape[0]
    else:
        (kd, m), n = a.shape, b.shape[1]
    tm, tn = _pick(m, tm, 16), _pick(n, tn)
    tk = _pick(kd, tk) if mode != "tn" else _pick(kd, tk, 16)
    if mode == "nn":
        a_spec = pl.BlockSpec((tm, tk), lambda i, j, k: (i, k))
        b_spec = pl.BlockSpec((tk, tn), lambda i, j, k: (k, j))
        dims = NN
    elif mode == "nt":
        a_spec = pl.BlockSpec((tm, tk), lambda i, j, k: (i, k))
        b_spec = pl.BlockSpec((tn, tk), lambda i, j, k: (j, k))
        dims = NT
    else:
        a_spec = pl.BlockSpec((tk, tm), lambda i, j, k: (k, i))
        b_spec = pl.BlockSpec((tk, tn), lambda i, j, k: (k, j))
        dims = TN

    def epilogue(accs, in_refs, out_refs):
        out_refs[0][...] = accs[0].astype(out_dtype)

    return _mm_call(
        name, (m // tm, n // tn, kd // tk), [(a, a_spec), (b, b_spec)], [(0, 1, dims, 0)],
        [(jax.ShapeDtypeStruct((m, n), out_dtype), pl.BlockSpec((tm, tn), lambda i, j, k: (i, j)))],
        [(tm, tn)], epilogue)[0]


def _norm_mod_tile(x, ng, sc, sh):
    r = lax.rsqrt(jnp.mean(x * x, axis=-1, keepdims=True) + RMS_EPS)
    return (x * r * ng) * (1.0 + sc) + sh


def _row_spec(tm, d):
    return pl.BlockSpec((tm, d), lambda i: (i, 0))


def _vec_spec(d):
    return pl.BlockSpec((1, d), lambda i: (0, 0))


def _norm_mod_fwd(x, ng, sc, sh, name):
    t, d = x.shape
    tm = _pick(t, 512, 16)

    def body(x_ref, ng_ref, sc_ref, sh_ref, h_ref):
        h_ref[...] = _norm_mod_tile(x_ref[...], ng_ref[...], sc_ref[...], sh_ref[...]).astype(BF16)

    return pl.pallas_call(
        body, name=name, grid=(t // tm,),
        in_specs=[_row_spec(tm, d), _vec_spec(d), _vec_spec(d), _vec_spec(d)],
        out_specs=_row_spec(tm, d), out_shape=jax.ShapeDtypeStruct((t, d), BF16),
        compiler_params=_params(),
    )(x, ng, sc, sh)


def _norm_mod_bwd(x, ng, sc, sh, dh, dres, name):
    t, d = x.shape
    tm = _pick(t, 256, 16)
    has_res = dres is not None

    def body(*refs):
        if has_res:
            x_ref, ng_ref, sc_ref, sh_ref, dh_ref, dres_ref, dx_ref, dng_ref, dsc_ref, dsh_ref = refs
        else:
            x_ref, ng_ref, sc_ref, sh_ref, dh_ref, dx_ref, dng_ref, dsc_ref, dsh_ref = refs
        _, vjp = jax.vjp(_norm_mod_tile, x_ref[...], ng_ref[...], sc_ref[...], sh_ref[...])
        dx, dng, dsc, dsh = vjp(dh_ref[...].astype(F32))
        if has_res:
            dx = dx + dres_ref[...]
        dx_ref[...] = dx

        @pl.when(pl.program_id(0) == 0)
        def _():
            dng_ref[...] = jnp.zeros_like(dng_ref)
            dsc_ref[...] = jnp.zeros_like(dsc_ref)
            dsh_ref[...] = jnp.zeros_like(dsh_ref)

        dng_ref[...] += dng
        dsc_ref[...] += dsc
        dsh_ref[...] += dsh

    ins = [x, ng, sc, sh, dh] + ([dres] if has_res else [])
    in_specs = [_row_spec(tm, d), _vec_spec(d), _vec_spec(d), _vec_spec(d), _row_spec(tm, d)]
    in_specs += [_row_spec(tm, d)] if has_res else []
    vec = jax.ShapeDtypeStruct((1, d), F32)
    return pl.pallas_call(
        body, name=name, grid=(t // tm,), in_specs=in_specs,
        out_specs=[_row_spec(tm, d), _vec_spec(d), _vec_spec(d), _vec_spec(d)],
        out_shape=[jax.ShapeDtypeStruct((t, d), F32), vec, vec, vec],
        compiler_params=_params(),
    )(*ins)


def _res_mm_fwd(a, w, x, gate, coef, name):
    t, kd = a.shape
    d = w.shape[1]
    tm, tn, tk = _pick(t, 1024, 16), _pick(d, 1024), _pick(kd, 2816)

    def epilogue(accs, in_refs, out_refs):
        f = accs[0]
        out_refs[0][...] = in_refs[2][...] + (coef * in_refs[3][...]) * f
        out_refs[1][...] = f.astype(BF16)

    tile = pl.BlockSpec((tm, tn), lambda i, j, k: (i, j))
    return _mm_call(
        name, (t // tm, d // tn, kd // tk),
        [(a, pl.BlockSpec((tm, tk), lambda i, j, k: (i, k))), (w, pl.BlockSpec((tk, tn), lambda i, j, k: (k, j))),
         (x, tile), (gate, pl.BlockSpec((1, tn), lambda i, j, k: (0, j)))],
        [(0, 1, NN, 0)],
        [(jax.ShapeDtypeStruct((t, d), F32), tile), (jax.ShapeDtypeStruct((t, d), BF16), tile)],
        [(tm, tn)], epilogue)


def _gate_bwd(dxo, f, gate, coef, name):
    t, d = dxo.shape
    tm = _pick(t, 512, 16)

    def body(dxo_ref, f_ref, gate_ref, df_ref, dgate_ref):
        dxo_t = dxo_ref[...]
        df_ref[...] = ((coef * gate_ref[...]) * dxo_t).astype(BF16)

        @pl.when(pl.program_id(0) == 0)
        def _():
            dgate_ref[...] = jnp.zeros_like(dgate_ref)

        dgate_ref[...] += coef * jnp.sum(dxo_t * f_ref[...].astype(F32), axis=0, keepdims=True)

    return pl.pallas_call(
        body, name=name, grid=(t // tm,),
        in_specs=[_row_spec(tm, d), _row_spec(tm, d), _vec_spec(d)],
        out_specs=[_row_spec(tm, d), _vec_spec(d)],
        out_shape=[jax.ShapeDtypeStruct((t, d), BF16), jax.ShapeDtypeStruct((1, d), F32)],
        compiler_params=_params(),
    )(dxo, f, gate)


def _ffn_in_fwd(h, w_in, name):
    t, d = h.shape
    n = w_in.shape[2]
    half = N_DEV // 2
    f = half * n
    tm = _pick(t, 512, 16)

    def epilogue(accs, in_refs, out_refs):
        g, u = accs
        s = jax.nn.sigmoid(g)
        silu = g * s
        out_refs[0][...] = (silu * u).astype(BF16)
        out_refs[1][0] = (u * (s * (1.0 + g * (1.0 - s)))).astype(BF16)
        out_refs[1][1] = silu.astype(BF16)

    return _mm_call(
        name, (half, t // tm, 1),
        [(h, pl.BlockSpec((tm, d), lambda j, i, k: (i, 0))),
         (w_in, pl.BlockSpec((None, d, n), lambda j, i, k: (j, 0, 0))),
         (w_in, pl.BlockSpec((None, d, n), lambda j, i, k: (j + half, 0, 0)))],
        [(0, 1, NN, 0), (0, 2, NN, 1)],
        [(jax.ShapeDtypeStruct((t, f), BF16), pl.BlockSpec((tm, n), lambda j, i, k: (i, j))),
         (jax.ShapeDtypeStruct((2, t, f), BF16), pl.BlockSpec((2, tm, n), lambda j, i, k: (0, i, j)))],
        [(tm, n), (tm, n)], epilogue)


def _ffn_da_bwd(df, w_out2d, gu, name):
    t, d = df.shape
    f = w_out2d.shape[0]
    half = N_DEV // 2
    n = f // half
    tm = _pick(t, 512, 16)
    step = 4 * LANE
    chunks = [(c, min(c + step, n)) for c in range(0, n, step)]

    def body(df_ref, w_ref, gu_ref, o_ref):
        df_t = df_ref[...]
        for c0, c1 in chunks:
            da = _dot(df_t, w_ref[c0:c1, :], NT)
            o_ref[0, :, c0:c1] = (da * gu_ref[0, :, c0:c1].astype(F32)).astype(BF16)
            o_ref[1, :, c0:c1] = (da * gu_ref[1, :, c0:c1].astype(F32)).astype(BF16)

    gu_spec = pl.BlockSpec((2, tm, n), lambda j, i: (0, i, j))
    return pl.pallas_call(
        body, name=name, grid=(half, t // tm),
        in_specs=[pl.BlockSpec((tm, d), lambda j, i: (i, 0)), pl.BlockSpec((n, d), lambda j, i: (j, 0)), gu_spec],
        out_specs=gu_spec, out_shape=jax.ShapeDtypeStruct((2, t, f), BF16), compiler_params=_params(),
    )(df, w_out2d, gu)


def _ffn_dh_bwd(dgu, w_in, name):
    _, t, f = dgu.shape
    d, n = w_in.shape[1], w_in.shape[2]
    half = N_DEV // 2
    tm = _pick(t, 512, 16)

    def epilogue(accs, in_refs, out_refs):
        out_refs[0][...] = accs[0]

    return _mm_call(
        name, (t // tm, 1, half),
        [(dgu, pl.BlockSpec((None, tm, n), lambda i, j, k: (0, i, k))),
         (dgu, pl.BlockSpec((None, tm, n), lambda i, j, k: (1, i, k))),
         (w_in, pl.BlockSpec((None, d, n), lambda i, j, k: (k, 0, 0))),
         (w_in, pl.BlockSpec((None, d, n), lambda i, j, k: (k + half, 0, 0)))],
        [(0, 2, NT, 0), (1, 3, NT, 0)],
        [(jax.ShapeDtypeStruct((t, d), F32), pl.BlockSpec((tm, d), lambda i, j, k: (i, 0)))],
        [(tm, d)], epilogue)[0]


def _ffn_dwin_bwd(h, dgu, name):
    t, d = h.shape
    f = dgu.shape[2]
    half = N_DEV // 2
    n = f // half
    tk = _pick(t, 1024, 16)

    def epilogue(accs, in_refs, out_refs):
        out_refs[0][...] = accs[0].astype(BF16)

    return _mm_call(
        name, (N_DEV, 1, t // tk),
        [(h, pl.BlockSpec((tk, d), lambda j, i, k: (k, 0))),
         (dgu, pl.BlockSpec((None, tk, n), lambda j, i, k: (j // half, k, j % half)))],
        [(0, 1, TN, 0)],
        [(jax.ShapeDtypeStruct((N_DEV, d, n), BF16), pl.BlockSpec((None, d, n), lambda j, i, k: (j, 0, 0)))],
        [(d, n)], epilogue)[0]


def _ffn_parts(tag):
    def w2d(w_out):
        return w_out.reshape(w_out.shape[0] * w_out.shape[1], w_out.shape[2])

    def fwd_in(x, ng, sh, sc, w_in):
        h = _norm_mod_fwd(x, ng, sc, sh, tag + "_norm")
        a, gu = _ffn_in_fwd(h, w_in, tag + "_in")
        return h, a, gu

    def fwd_out(a, w_out, x, gate):
        return _res_mm_fwd(a, w2d(w_out), x, gate, 0.5, tag + "_out")

    def bwd_out(dxo, f1, gate, w_out, gu, a):
        df, dgate = _gate_bwd(dxo, f1, gate, 0.5, tag + "_dgate")
        dgu = _ffn_da_bwd(df, w2d(w_out), gu, tag + "_da")
        f = w_out.shape[0] * w_out.shape[1]
        dw_out = _matmul(a, df, "tn", BF16, tag + "_dwout", tm=_pick(f, 1408, 16), tn=2048, tk=1024)
        return dgate, dgu, dw_out.reshape(w_out.shape)

    def bwd_in(x, ng, sh, sc, w_in, h, dgu, dxo):
        dh = _ffn_dh_bwd(dgu, w_in, tag + "_dh")
        dw_in = _ffn_dwin_bwd(h, dgu, tag + "_dwin")
        dx, dng, dsc, dsh = _norm_mod_bwd(x, ng, sc, sh, dh, dxo, tag + "_dnorm")
        return dx, dng, dsh, dsc, dw_in

    return fwd_in, fwd_out, bwd_out, bwd_in


def _make_ffn_block(tag):
    fwd_in, fwd_out, bwd_out, bwd_in = _ffn_parts(tag)

    @jax.custom_vjp
    def ffn_block(x, ng, sh, sc, gate, w_in, w_out):
        return fwd(x, ng, sh, sc, gate, w_in, w_out)[0]

    def fwd(x, ng, sh, sc, gate, w_in, w_out):
        h, a, gu = fwd_in(x, ng, sh, sc, w_in)
        xo, f1 = fwd_out(a, w_out, x, gate)
        return xo, (x, ng, sh, sc, gate, w_in, w_out, h, a, gu, f1)

    def bwd(res, dxo):
        x, ng, sh, sc, gate, w_in, w_out, h, a, gu, f1 = res
        dgate, dgu, dw_out = bwd_out(dxo, f1, gate, w_out, gu, a)
        dx, dng, dsh, dsc, dw_in = bwd_in(x, ng, sh, sc, w_in, h, dgu, dxo)
        return dx, dng, dsh, dsc, dgate, dw_in, dw_out

    ffn_block.defvjp(fwd, bwd)
    return ffn_block


def _make_norm_proj(tag):
    @jax.custom_vjp
    def norm_proj(x, ng, sh, sc, w):
        return fwd(x, ng, sh, sc, w)[0]

    def fwd(x, ng, sh, sc, w):
        h = _norm_mod_fwd(x, ng, sc, sh, tag + "_norm")
        p = _matmul(h, w, "nn", F32, tag + "_mm", tm=1024, tn=1024, tk=w.shape[0])
        return p, (x, ng, sh, sc, w, h)

    def bwd(res, dp):
        x, ng, sh, sc, w, h = res
        dh = _matmul(dp, w, "nt", F32, tag + "_dh", tm=512, tn=w.shape[0], tk=2048)
        dw = _matmul(h, dp, "tn", BF16, tag + "_dw", tm=w.shape[0], tn=1024, tk=1024)
        dx, dng, dsc, dsh = _norm_mod_bwd(x, ng, sc, sh, dh, None, tag + "_dnorm")
        return dx, dng, dsh, dsc, dw

    norm_proj.defvjp(fwd, bwd)
    return norm_proj


def _make_norm_proj_carry(tag):
    @jax.custom_vjp
    def norm_proj(x, ng, sh, sc, w):
        return fwd(x, ng, sh, sc, w)[0]

    def fwd(x, ng, sh, sc, w):
        h = _norm_mod_fwd(x, ng, sc, sh, tag + "_norm")
        p = _matmul(h, w, "nn", F32, tag + "_mm", tm=1024, tn=1024, tk=w.shape[0])
        return (p, x), (x, ng, sh, sc, w, h)

    def bwd(res, cts):
        x, ng, sh, sc, w, h = res
        dp, dx_carry = cts
        dh = _matmul(dp, w, "nt", F32, tag + "_dh", tm=512, tn=w.shape[0], tk=2048)
        dw = _matmul(h, dp, "tn", BF16, tag + "_dw", tm=w.shape[0], tn=1024, tk=1024)
        dx, dng, dsc, dsh = _norm_mod_bwd(x, ng, sc, sh, dh, dx_carry, tag + "_dnorm")
        return dx, dng, dsh, dsc, dw

    norm_proj.defvjp(fwd, bwd)
    return norm_proj


def _make_split(tag, widths, total):
    offs = [sum(widths[:i]) for i in range(len(widths))]

    def concat_call(pieces):
        t = pieces[0].shape[0]
        tm = _pick(t, 256, 16)

        def body(*refs):
            o_ref = refs[-1]
            for ref, off, wd in zip(refs[:-1], offs, widths):
                o_ref[:, off:off + wd] = ref[...]
            end = offs[-1] + widths[-1]
            if end < total:
                o_ref[:, end:] = jnp.zeros((tm, total - end), F32)

        return pl.pallas_call(
            body, name=tag + "_concat", grid=(t // tm,),
            in_specs=[_row_spec(tm, wd) for wd in widths], out_specs=_row_spec(tm, total),
            out_shape=jax.ShapeDtypeStruct((t, total), F32), compiler_params=_params(),
        )(*pieces)

    @jax.custom_vjp
    def split(p):
        return tuple(p[:, off:off + wd] for off, wd in zip(offs, widths))

    def fwd(p):
        return split(p), None

    def bwd(_, cts):
        return (concat_call(list(cts)),)

    split.defvjp(fwd, bwd)
    return split


def _make_res_proj(tag):
    @jax.custom_vjp
    def res_proj(a, w, x, gate):
        return fwd(a, w, x, gate)[0]

    def fwd(a, w, x, gate):
        xo, f = _res_mm_fwd(a, w, x, gate, 1.0, tag + "_mm")
        return xo, (a, w, gate, f)

    def bwd(res, dxo):
        a, w, gate, f = res
        df, dgate = _gate_bwd(dxo, f, gate, 1.0, tag + "_dgate")
        da = _matmul(df, w, "nt", BF16, tag + "_da", tm=1024, tn=1024, tk=2048)
        dw = _matmul(a, df, "tn", BF16, tag + "_dw", tm=1024, tn=2048, tk=1024)
        return da, dw, dxo, dgate

    res_proj.defvjp(fwd, bwd)
    return res_proj


def _make_small_mm(tag):
    @jax.custom_vjp
    def small_mm(a, w):
        return _matmul(a, w, "nn", F32, tag + "_mm", tm=a.shape[0], tn=768, tk=w.shape[0])

    def fwd(a, w):
        return small_mm(a, w), (a, w)

    def bwd(res, dr):
        a, w = res
        da = _matmul(dr, w, "nt", F32, tag + "_da", tm=a.shape[0], tn=w.shape[0], tk=768)
        dw = _matmul(a, dr, "tn", F32, tag + "_dw", tm=1024, tn=768, tk=a.shape[0])
        return da, dw

    small_mm.defvjp(fwd, bwd)
    return small_mm


def _ret_chunk_terms(lg, c, reverse):
    row = lax.broadcasted_iota(jnp.int32, (c, c), 0).astype(F32)
    col = lax.broadcasted_iota(jnp.int32, (c, c), 1).astype(F32)
    pos = lax.broadcasted_iota(jnp.int32, (c, 1), 0).astype(F32)
    if reverse:
        diff = col - row
        mask = diff > 0.0
        e_exp = float(c) - pos
        f_exp = pos
    else:
        diff = row - col
        mask = diff >= 0.0
        e_exp = pos + 1.0
        f_exp = float(c - 1) - pos
    diffm = jnp.where(mask, diff, 0.0)
    dm = jnp.where(mask, jnp.exp(lg * diffm), 0.0)
    return diffm, dm, e_exp, jnp.exp(lg * e_exp), f_exp, jnp.exp(lg * f_exp)


def _ret_mask_t(lg, c, reverse):
    row = lax.broadcasted_iota(jnp.int32, (c, c), 0).astype(F32)
    col = lax.broadcasted_iota(jnp.int32, (c, c), 1).astype(F32)
    diff = row - col if reverse else col - row
    mask = diff > 0.0 if reverse else diff >= 0.0
    return jnp.where(mask, jnp.exp(lg * jnp.where(mask, diff, 0.0)), 0.0)


def _lane0(val):
    lane = lax.broadcasted_iota(jnp.int32, (1, LANE), 1)
    return jnp.where(lane == 0, val, 0.0)


RET_HEAD_BLOCK = 4


def _make_ret_dir(tag, reverse):
    hb = RET_HEAD_BLOCK

    def heads_spec(nc, width, flip):
        if flip:
            return pl.BlockSpec((hb, RET_CHUNK, width), lambda h, t: (h, nc - 1 - t, 0))
        return pl.BlockSpec((hb, RET_CHUNK, width), lambda h, t: (h, t, 0))

    def state_spec(nc, flip):
        if flip:
            return pl.BlockSpec((hb, None, RET_DKP, RET_DV), lambda h, t: (h, nc - 1 - t, 0, 0))
        return pl.BlockSpec((hb, None, RET_DKP, RET_DV), lambda h, t: (h, t, 0, 0))

    lg_spec = pl.BlockSpec((hb, 1, LANE), lambda h, t: (h, 0, 0))
    s0_spec = pl.BlockSpec((hb, RET_DKP, RET_DV), lambda h, t: (h, 0, 0))

    def fwd_call(q, k, v, lgb, s0):
        hh, ll, _ = q.shape
        c = RET_CHUNK
        nc = ll // c

        def body(q_ref, k_ref, v_ref, lg_ref, s0_ref, y_ref, sall_ref, s_scr):
            @pl.when(pl.program_id(1) == 0)
            def _():
                s_scr[...] = s0_ref[...]

            for b in range(hb):
                lg = lg_ref[b][:, :1]
                _, dm, _, xi, _, zeta = _ret_chunk_terms(lg, c, reverse)
                q_t, k_t, v_t = q_ref[b], k_ref[b], v_ref[b]
                s = s_scr[b]
                p = _dot(q_t, k_t, NT) * dm
                y_ref[b] = _dot(p, v_t, NN) + _dot(q_t * xi, s, NN)
                sall_ref[b] = s
                s_scr[b] = jnp.exp(lg * float(c)) * s + _dot(k_t * zeta, v_t, TN)

        return pl.pallas_call(
            body, name=tag + "_fwd", grid=(hh // hb, nc),
            in_specs=[heads_spec(nc, RET_DKP,reverse), heads_spec(nc, RET_DKP,reverse),
                      heads_spec(nc, RET_DV, reverse), lg_spec, s0_spec],
            out_specs=[heads_spec(nc, RET_DV, reverse), state_spec(nc, reverse)],
            out_shape=[jax.ShapeDtypeStruct((hh, ll, RET_DV), F32),
                       jax.ShapeDtypeStruct((hh, nc, RET_DKP, RET_DV), F32)],
            scratch_shapes=[pltpu.VMEM((hb, RET_DKP, RET_DV), F32)],
            compiler_params=_params(),
        )(q, k, v, lgb, s0)

    def bwd_call(q, k, v, lgb, sall, dy):
        hh, ll, _ = q.shape
        c = RET_CHUNK
        nc = ll // c
        flip = not reverse

        def body(q_ref, k_ref, v_ref, lg_ref, sall_ref, dy_ref, dq_ref, dk_ref, dv_ref, dlg_ref, ds0_ref, ds_scr):
            @pl.when(pl.program_id(1) == 0)
            def _():
                ds_scr[...] = jnp.zeros_like(ds_scr)
                dlg_ref[...] = jnp.zeros_like(dlg_ref)

            def total(m):
                return jnp.sum(jnp.sum(m, axis=1, keepdims=True), axis=0, keepdims=True)

            for b in range(hb):
                lg = lg_ref[b][:, :1]
                diffm, dm, e_exp, xi, f_exp, zeta = _ret_chunk_terms(lg, c, reverse)
                q_t, k_t, v_t, dy_t = q_ref[b], k_ref[b], v_ref[b], dy_ref[b]
                s = sall_ref[b]
                dsn = ds_scr[b]
                a = _dot(q_t, k_t, NT)
                da = _dot(dy_t, v_t, NT) * dm
                dm_t = _ret_mask_t(lg, c, reverse)
                a_t = _dot(k_t, q_t, NT)
                da_t = _dot(v_t, dy_t, NT) * dm_t
                g = _dot(dy_t, s, NT)
                hm = _dot(v_t, dsn, NT)
                dq_ref[b] = _dot(da, k_t, NN) + xi * g
                dk_ref[b] = _dot(da_t, q_t, NN) + zeta * hm
                dv_ref[b] = _dot(a_t * dm_t, dy_t, NN) + _dot(k_t * zeta, dsn, NN)
                gc = jnp.exp(lg * float(c))
                ds_scr[b] = gc * dsn + _dot(q_t * xi, dy_t, TN)
                dl = (total(da * a * diffm) + total(e_exp * xi * q_t * g)
                      + float(c) * gc * total(s * dsn) + total(f_exp * zeta * k_t * hm))
                dlg_ref[b] += _lane0(dl)

            @pl.when(pl.program_id(1) == nc - 1)
            def _():
                ds0_ref[...] = ds_scr[...]

        return pl.pallas_call(
            body, name=tag + "_bwd", grid=(hh // hb, nc),
            in_specs=[heads_spec(nc, RET_DKP,flip), heads_spec(nc, RET_DKP,flip), heads_spec(nc, RET_DV, flip),
                      lg_spec, state_spec(nc, flip), heads_spec(nc, RET_DV, flip)],
            out_specs=[heads_spec(nc, RET_DKP,flip), heads_spec(nc, RET_DKP,flip), heads_spec(nc, RET_DV, flip),
                       lg_spec, s0_spec],
            out_shape=[jax.ShapeDtypeStruct((hh, ll, RET_DKP), F32), jax.ShapeDtypeStruct((hh, ll, RET_DKP), F32),
                       jax.ShapeDtypeStruct((hh, ll, RET_DV), F32), jax.ShapeDtypeStruct((hh, 1, LANE), F32),
                       jax.ShapeDtypeStruct((hh, RET_DKP, RET_DV), F32)],
            scratch_shapes=[pltpu.VMEM((hb, RET_DKP, RET_DV), F32)],
            compiler_params=_params(),
        )(q, k, v, lgb, sall, dy)

    @jax.custom_vjp
    def ret_dir(q, k, v, lgb, s0):
        return fwd_call(q, k, v, lgb, s0)[0]

    def fwd(q, k, v, lgb, s0):
        y, sall = fwd_call(q, k, v, lgb, s0)
        return y, (q, k, v, lgb, sall)

    def bwd(res, dy):
        q, k, v, lgb, sall = res
        return tuple(bwd_call(q, k, v, lgb, sall, dy))

    ret_dir.defvjp(fwd, bwd)
    return ret_dir


def _make_ctx_state(tag, reverse):
    hb = RET_HEAD_BLOCK
    c = RET_CHUNK
    k_spec = pl.BlockSpec((hb, c, RET_DKP), lambda h: (h, 0, 0))
    v_spec = pl.BlockSpec((hb, c, RET_DV), lambda h: (h, 0, 0))
    lg_spec = pl.BlockSpec((hb, 1, LANE), lambda h: (h, 0, 0))
    s_spec = pl.BlockSpec((hb, RET_DKP, RET_DV), lambda h: (h, 0, 0))

    def fwd_call(k, v, lgb):
        hh = k.shape[0]

        def body(k_ref, v_ref, lg_ref, s_ref):
            for b in range(hb):
                _, _, _, _, _, zeta = _ret_chunk_terms(lg_ref[b][:, :1], c, reverse)
                s_ref[b] = _dot(k_ref[b] * zeta, v_ref[b], TN)

        return pl.pallas_call(
            body, name=tag + "_fwd", grid=(hh // hb,), in_specs=[k_spec, v_spec, lg_spec], out_specs=s_spec,
            out_shape=jax.ShapeDtypeStruct((hh, RET_DKP, RET_DV), F32), compiler_params=_params(),
        )(k, v, lgb)

    def bwd_call(k, v, lgb, ds):
        hh = k.shape[0]

        def body(k_ref, v_ref, lg_ref, ds_ref, dk_ref, dv_ref, dlg_ref):
            for b in range(hb):
                _, _, _, _, f_exp, zeta = _ret_chunk_terms(lg_ref[b][:, :1], c, reverse)
                k_t, v_t, ds = k_ref[b], v_ref[b], ds_ref[b]
                hm = _dot(v_t, ds, NT)
                dk_ref[b] = zeta * hm
                dv_ref[b] = _dot(k_t * zeta, ds, NN)
                tot = jnp.sum(jnp.sum(f_exp * zeta * k_t * hm, axis=1, keepdims=True), axis=0, keepdims=True)
                dlg_ref[b] = _lane0(tot)

        return pl.pallas_call(
            body, name=tag + "_bwd", grid=(hh // hb,), in_specs=[k_spec, v_spec, lg_spec, s_spec],
            out_specs=[k_spec, v_spec, lg_spec],
            out_shape=[jax.ShapeDtypeStruct(k.shape, F32), jax.ShapeDtypeStruct(v.shape, F32),
                       jax.ShapeDtypeStruct((hh, 1, LANE), F32)],
            compiler_params=_params(),
        )(k, v, lgb, ds)

    @jax.custom_vjp
    def ctx_state(k, v, lgb):
        return fwd_call(k, v, lgb)

    def fwd(k, v, lgb):
        return fwd_call(k, v, lgb), (k, v, lgb)

    def bwd(res, ds):
        return tuple(bwd_call(*res, ds))

    ctx_state.defvjp(fwd, bwd)
    return ctx_state


def _rope_tables_call(name, n, inv, shift, axial):
    tm = _pick(n, 1024, 8)
    inv_lane = jnp.tile(inv, LANE // inv.shape[0])[None, :]

    def body(inv_ref, cos_ref, s1_ref, s2_ref):
        t = lax.broadcasted_iota(jnp.int32, (tm, LANE), 0) + pl.program_id(0) * tm
        lane = lax.broadcasted_iota(jnp.int32, (tm, LANE), 1)
        if axial:
            pos = jnp.where(lane % (2 * MLA_ROPE // 2) < MLA_ROPE // 2, t // GRID_W, t % GRID_W)
        else:
            pos = t
        ang = pos.astype(F32) * inv_ref[...]
        sin = jnp.sin(ang)
        first = lane % (2 * shift) < shift
        cos_ref[...] = jnp.cos(ang)
        s1_ref[...] = jnp.where(first, -sin, 0.0)
        s2_ref[...] = jnp.where(first, 0.0, sin)

    tab = jax.ShapeDtypeStruct((n, LANE), F32)
    return tuple(pl.pallas_call(
        body, name=name, grid=(n // tm,), in_specs=[_vec_spec(LANE)], out_specs=[_row_spec(tm, LANE)] * 3,
        out_shape=[tab, tab, tab], compiler_params=_params(),
    )(inv_lane))


def _ret_tables(n_lat):
    inv = RET_ROPE_BASE ** (-jnp.arange(0, RET_DK, 2, dtype=F32) / RET_DK)
    return _rope_tables_call("ret_tables", n_lat, inv, RET_DK // 2, False)


def _make_ret_pack(tag, n_lat, n_ctx):
    hh = RET_HEADS
    tm = MLA_PACK_ROWS
    k_scale = RET_DK ** -0.5
    shift = RET_DK // 2
    tabs = _ret_tables(n_lat)

    def low_lanes():
        return lax.broadcasted_iota(jnp.int32, (1, LANE), 1) < RET_DK

    def rows(width):
        return pl.BlockSpec((tm, width), lambda i: (i, 0))

    def heads(width):
        return pl.BlockSpec((hh, tm, width), lambda i: (0, i, 0))

    def split_pairs(src_ref, dst_ref, scale, rope):
        keep = low_lanes()
        for j in range(hh // 2):
            blk = src_ref[:, LANE * j:LANE * (j + 1)]
            if scale != 1.0:
                blk = blk * scale
            if rope is not None:
                blk = _rope128(blk, *rope, shift=shift)
            dst_ref[2 * j] = jnp.where(keep, blk, 0.0)
            dst_ref[2 * j + 1] = jnp.where(keep, pltpu.roll(blk, RET_DK, 1), 0.0)

    def merge_pairs(src_refs, dst_ref, scale, rope):
        keep = low_lanes()
        for j in range(hh // 2):
            even = sum(r[2 * j] for r in src_refs)
            odd = sum(r[2 * j + 1] for r in src_refs)
            g = jnp.where(keep, even, pltpu.roll(odd, RET_DK, 1))
            if rope is not None:
                g = _rope128_t(g, *rope, shift=shift)
            dst_ref[:, LANE * j:LANE * (j + 1)] = g * scale if scale != 1.0 else g

    def pack_call(name, n, q, k, v, rope):
        with_q = q is not None

        def body(*refs):
            refs = list(refs)
            q_ref = refs.pop(0) if with_q else None
            k_ref, v_ref = refs.pop(0), refs.pop(0)
            tab = tuple(r[...] for r in refs[:3]) if rope else None
            outs = refs[3:] if rope else refs
            if with_q:
                split_pairs(q_ref, outs[0], 1.0, tab)
                outs = outs[1:]
            split_pairs(k_ref, outs[0], k_scale, tab)
            for h in range(hh):
                outs[1][h] = v_ref[:, RET_DV * h:RET_DV * (h + 1)]

        ins = ([q] if with_q else []) + [k, v] + (list(tabs) if rope else [])
        in_specs = ([rows(q.shape[1])] if with_q else []) + [rows(k.shape[1]), rows(v.shape[1])]
        in_specs += [rows(LANE)] * 3 if rope else []
        n_out = 3 if with_q else 2
        return pl.pallas_call(
            body, name=name, grid=(n // tm,), in_specs=in_specs,
            out_specs=[heads(RET_DKP)] * (n_out - 1) + [heads(RET_DV)],
            out_shape=[jax.ShapeDtypeStruct((hh, n, RET_DKP), F32)] * (n_out - 1)
            + [jax.ShapeDtypeStruct((hh, n, RET_DV), F32)],
            compiler_params=_params(),
        )(*ins)

    def unpack_call(name, n, dqs, dks, dvs, rope):
        with_q = len(dqs) > 0
        uses = len(dks)

        def body(*refs):
            refs = list(refs)
            dq_refs = [refs.pop(0) for _ in range(len(dqs))]
            dk_refs = [refs.pop(0) for _ in range(uses)]
            dv_refs = [refs.pop(0) for _ in range(uses)]
            tab = tuple(r[...] for r in refs[:3]) if rope else None
            outs = refs[3:] if rope else refs
            if with_q:
                merge_pairs(dq_refs, outs[0], 1.0, tab)
                outs = outs[1:]
            merge_pairs(dk_refs, outs[0], k_scale, tab)
            for h in range(hh):
                outs[1][:, RET_DV * h:RET_DV * (h + 1)] = sum(r[h] for r in dv_refs)

        ins = list(dqs) + list(dks) + list(dvs) + (list(tabs) if rope else [])
        in_specs = [heads(RET_DKP)] * (len(dqs) + uses) + [heads(RET_DV)] * uses + ([rows(LANE)] * 3 if rope else [])
        n_out = 3 if with_q else 2
        return pl.pallas_call(
            body, name=name, grid=(n // tm,), in_specs=in_specs,
            out_specs=[rows(hh * RET_DK)] * (n_out - 1) + [rows(hh * RET_DV)],
            out_shape=[jax.ShapeDtypeStruct((n, hh * RET_DK), F32)] * (n_out - 1)
            + [jax.ShapeDtypeStruct((n, hh * RET_DV), F32)],
            compiler_params=_params(),
        )(*ins)

    @jax.custom_vjp
    def ret_pack(rq, rk, rv, crk, crv):
        q, k, v = pack_call(tag + "_lat", n_lat, rq, rk, rv, True)
        k_c, v_c = pack_call(tag + "_ctx", n_ctx, None, crk, crv, False)
        return (q, k, v), (q, k, v), (k_c, v_c), (k_c, v_c)

    def fwd(rq, rk, rv, crk, crv):
        return ret_pack(rq, rk, rv, crk, crv), None

    def bwd(_, cts):
        lat_f, lat_b, ctx_f, ctx_b = cts
        drq, drk, drv = unpack_call(tag + "_dlat", n_lat, [lat_f[0], lat_b[0]], [lat_f[1], lat_b[1]],
                                    [lat_f[2], lat_b[2]], True)
        dcrk, dcrv = unpack_call(tag + "_dctx", n_ctx, [], [ctx_f[0], ctx_b[0]], [ctx_f[1], ctx_b[1]], False)
        return drq, drk, drv, dcrk, dcrv

    ret_pack.defvjp(fwd, bwd)
    return ret_pack


def _ret_out_tile(y, g):
    mu = jnp.mean(y, axis=-1, keepdims=True)
    var = jnp.mean(jnp.square(y - mu), axis=-1, keepdims=True)
    return (g * jax.nn.sigmoid(g)) * ((y - mu) * lax.rsqrt(var + GN_EPS))


def _make_ret_out(tag):
    def specs(tm):
        y_spec = pl.BlockSpec((None, tm, RET_DV), lambda h, i: (h, i, 0))
        g_spec = pl.BlockSpec((tm, RET_DV), lambda h, i: (i, h))
        return y_spec, g_spec

    def fwd_call(yf, yb, g):
        hh, n, _ = yf.shape
        tm = _pick(n, 1024, 16)
        y_spec, g_spec = specs(tm)

        def body(yf_ref, yb_ref, g_ref, o_ref):
            o_ref[...] = _ret_out_tile(yf_ref[...] + yb_ref[...], g_ref[...]).astype(BF16)

        return pl.pallas_call(
            body, name=tag + "_fwd", grid=(hh, n // tm), in_specs=[y_spec, y_spec, g_spec], out_specs=g_spec,
            out_shape=jax.ShapeDtypeStruct((n, hh * RET_DV), BF16), compiler_params=_params(),
        )(yf, yb, g)

    def bwd_call(yf, yb, g, do):
        hh, n, _ = yf.shape
        tm = _pick(n, 1024, 16)
        y_spec, g_spec = specs(tm)

        def body(yf_ref, yb_ref, g_ref, do_ref, dy_ref, dg_ref):
            _, vjp = jax.vjp(_ret_out_tile, yf_ref[...] + yb_ref[...], g_ref[...])
            dy, dg = vjp(do_ref[...].astype(F32))
            dy_ref[...] = dy
            dg_ref[...] = dg

        return pl.pallas_call(
            body, name=tag + "_bwd", grid=(hh, n // tm), in_specs=[y_spec, y_spec, g_spec, g_spec],
            out_specs=[y_spec, g_spec],
            out_shape=[jax.ShapeDtypeStruct(yf.shape, F32), jax.ShapeDtypeStruct(g.shape, F32)],
            compiler_params=_params(),
        )(yf, yb, g, do)

    @jax.custom_vjp
    def ret_out(yf, yb, g):
        return fwd_call(yf, yb, g)

    def fwd(yf, yb, g):
        return fwd_call(yf, yb, g), (yf, yb, g)

    def bwd(res, do):
        dy, dg = bwd_call(*res, do)
        return dy, dy, dg

    ret_out.defvjp(fwd, bwd)
    return ret_out


MLA_DQ_PAD = 2 * LANE
MLA_PACK_ROWS = 256


def _rope128(x, cos, s1, s2, shift=16):
    return x * cos + pltpu.roll(x, LANE - shift, 1) * s1 + pltpu.roll(x, shift, 1) * s2


def _rope128_t(g, cos, s1, s2, shift=16):
    return g * cos + pltpu.roll(g * s1, shift, 1) + pltpu.roll(g * s2, LANE - shift, 1)


def _axial_tables(n_lat):
    half = MLA_ROPE // 2
    inv = AXIAL_BASE ** (-jnp.arange(0, half, 2, dtype=F32) / half)
    return _rope_tables_call("mla_tables", n_lat, inv, half // 2, True)


def _make_mla_pack(tag, n_lat, n_ctx, scale):
    hh = MLA_HEADS
    tm = MLA_PACK_ROWS
    ll = n_lat + n_ctx
    rope0 = hh * MLA_NOPE
    tabs = _axial_tables(n_lat)

    def rope_lanes():
        return lax.broadcasted_iota(jnp.int32, (1, LANE), 1) < MLA_ROPE

    def rows(width):
        return pl.BlockSpec((tm, width), lambda i: (i, 0))

    def heads(width, off):
        return pl.BlockSpec((hh, tm, width), lambda i: (0, i + off, 0))

    def heads_t(width, off):
        return pl.BlockSpec((hh, width, tm), lambda i: (0, 0, i + off))

    def put_kv(kv_ref, kr_rot, k_ref, v_ref, kt_ref, vt_ref):
        kr_b = kr_rot.astype(BF16)
        kr_t = jnp.transpose(kr_rot).astype(BF16)
        for h in range(hh):
            k_nope = kv_ref[:, 2 * LANE * h:2 * LANE * h + MLA_NOPE]
            val = kv_ref[:, 2 * LANE * h + MLA_NOPE:2 * LANE * (h + 1)]
            k_ref[h, :, :MLA_NOPE] = k_nope.astype(BF16)
            k_ref[h, :, MLA_NOPE:] = kr_b
            v_ref[h] = val.astype(BF16)
            kt_ref[h, :MLA_NOPE, :] = jnp.transpose(k_nope).astype(BF16)
            kt_ref[h, MLA_NOPE:, :] = kr_t
            vt_ref[h] = jnp.transpose(val).astype(BF16)

    def fwd_lat(qp, kv, kr):
        def body(qp_ref, kv_ref, kr_ref, cos_ref, s1_ref, s2_ref, q_ref, k_ref, v_ref, kt_ref, vt_ref):
            cos, s1, s2 = cos_ref[...], s1_ref[...], s2_ref[...]
            keep = rope_lanes()
            for j in range(hh // 2):
                rot = _rope128(qp_ref[:, rope0 + LANE * j:rope0 + LANE * (j + 1)], cos, s1, s2)
                q_ref[2 * j, :, MLA_NOPE:] = jnp.where(keep, rot, 0.0).astype(BF16)
                q_ref[2 * j + 1, :, MLA_NOPE:] = jnp.where(keep, pltpu.roll(rot, MLA_ROPE, 1), 0.0).astype(BF16)
            for h in range(hh):
                q_ref[h, :, :MLA_NOPE] = qp_ref[:, MLA_NOPE * h:MLA_NOPE * (h + 1)].astype(BF16)
            kr_rot = jnp.where(keep, _rope128(kr_ref[...], cos, s1, s2), 0.0)
            put_kv(kv_ref, kr_rot, k_ref, v_ref, kt_ref, vt_ref)

        return pl.pallas_call(
            body, name=tag + "_lat", grid=(n_lat // tm,),
            in_specs=[rows(qp.shape[1]), rows(kv.shape[1]), rows(LANE), rows(LANE), rows(LANE), rows(LANE)],
            out_specs=[heads(MLA_DQ_PAD, 0), heads(MLA_DQ_PAD, 0), heads(MLA_V, 0), heads_t(MLA_DQ_PAD, 0),
                       heads_t(MLA_V, 0)],
            out_shape=[jax.ShapeDtypeStruct((hh, n_lat, MLA_DQ_PAD), BF16),
                       jax.ShapeDtypeStruct((hh, ll, MLA_DQ_PAD), BF16), jax.ShapeDtypeStruct((hh, ll, MLA_V), BF16),
                       jax.ShapeDtypeStruct((hh, MLA_DQ_PAD, ll), BF16), jax.ShapeDtypeStruct((hh, MLA_V, ll), BF16)],
            compiler_params=_params(),
        )(qp, kv, kr, *tabs)

    def fwd_ctx(kv_c, kr_c, bufs):
        def body(kv_ref, kr_ref, k_in, v_in, kt_in, vt_in, k_ref, v_ref, kt_ref, vt_ref):
            kr_rot = jnp.where(rope_lanes(), kr_ref[...], 0.0)
            put_kv(kv_ref, kr_rot, k_ref, v_ref, kt_ref, vt_ref)

        any_spec = pl.BlockSpec(memory_space=pl.ANY)
        off = n_lat // tm
        return pl.pallas_call(
            body, name=tag + "_ctx", grid=(n_ctx // tm,),
            in_specs=[rows(kv_c.shape[1]), rows(LANE)] + [any_spec] * 4,
            out_specs=[heads(MLA_DQ_PAD, off), heads(MLA_V, off), heads_t(MLA_DQ_PAD, off), heads_t(MLA_V, off)],
            out_shape=[jax.ShapeDtypeStruct(b.shape, BF16) for b in bufs],
            input_output_aliases={2: 0, 3: 1, 4: 2, 5: 3}, compiler_params=_params(),
        )(kv_c, kr_c, *bufs)

    def take_kv(dk_ref, dv_ref, dkv_ref):
        dkr = jnp.zeros((tm, LANE), F32)
        for h in range(hh):
            dkv_ref[:, 2 * LANE * h:2 * LANE * h + MLA_NOPE] = dk_ref[h, :, :MLA_NOPE].astype(F32)
            dkv_ref[:, 2 * LANE * h + MLA_NOPE:2 * LANE * (h + 1)] = dv_ref[h].astype(F32)
            dkr = dkr + dk_ref[h, :, MLA_NOPE:].astype(F32)
        return jnp.where(rope_lanes(), dkr, 0.0)

    def bwd_lat(dqt, dk, dv, qp_width, kv_width):
        def body(dqt_ref, dk_ref, dv_ref, cos_ref, s1_ref, s2_ref, dqp_ref, dkv_ref, dkr_ref):
            cos, s1, s2 = cos_ref[...], s1_ref[...], s2_ref[...]
            keep = rope_lanes()
            for j in range(hh // 2):
                even = jnp.transpose(dqt_ref[2 * j]) * scale
                odd = jnp.transpose(dqt_ref[2 * j + 1]) * scale
                dqp_ref[:, MLA_NOPE * 2 * j:MLA_NOPE * (2 * j + 1)] = even[:, :MLA_NOPE]
                dqp_ref[:, MLA_NOPE * (2 * j + 1):MLA_NOPE * (2 * j + 2)] = odd[:, :MLA_NOPE]
                g = jnp.where(keep, even[:, MLA_NOPE:], pltpu.roll(odd[:, MLA_NOPE:], MLA_ROPE, 1))
                dqp_ref[:, rope0 + LANE * j:rope0 + LANE * (j + 1)] = _rope128_t(g, cos, s1, s2)
            dkr_ref[...] = jnp.where(keep, _rope128_t(take_kv(dk_ref, dv_ref, dkv_ref), cos, s1, s2), 0.0)

        return pl.pallas_call(
            body, name=tag + "_dlat", grid=(n_lat // tm,),
            in_specs=[pl.BlockSpec((hh, MLA_DQ_PAD, tm), lambda i: (0, 0, i)),
                      heads(MLA_DQ_PAD, 0), heads(MLA_V, 0), rows(LANE), rows(LANE), rows(LANE)],
            out_specs=[rows(qp_width), rows(kv_width), rows(LANE)],
            out_shape=[jax.ShapeDtypeStruct((n_lat, qp_width), F32), jax.ShapeDtypeStruct((n_lat, kv_width), F32),
                       jax.ShapeDtypeStruct((n_lat, LANE), F32)],
            compiler_params=_params(),
        )(dqt, dk, dv, *tabs)

    def bwd_ctx(dk, dv, kv_width):
        def body(dk_ref, dv_ref, dkv_ref, dkr_ref):
            dkr_ref[...] = take_kv(dk_ref, dv_ref, dkv_ref)

        off = n_lat // tm
        return pl.pallas_call(
            body, name=tag + "_dctx", grid=(n_ctx // tm,),
            in_specs=[heads(MLA_DQ_PAD, off), heads(MLA_V, off)],
            out_specs=[rows(kv_width), rows(LANE)],
            out_shape=[jax.ShapeDtypeStruct((n_ctx, kv_width), F32), jax.ShapeDtypeStruct((n_ctx, LANE), F32)],
            compiler_params=_params(),
        )(dk, dv)

    def pack(qp, kv, kr, kv_c, kr_c):
        q, *bufs = fwd_lat(qp, kv, kr)
        return (q, *fwd_ctx(kv_c, kr_c, bufs))

    def unpack(dqt, dk, dv):
        qp_width, kv_width = hh * (MLA_NOPE + MLA_ROPE), hh * (MLA_NOPE + MLA_V)
        dqp, dkv, dkr = bwd_lat(dqt, dk, dv, qp_width, kv_width)
        dkv_c, dkr_c = bwd_ctx(dk, dv, kv_width)
        return dqp, dkv, dkr, dkv_c, dkr_c

    return pack, unpack


def _make_mla(tag, n_lat, n_ctx):
    scale = (MLA_NOPE + MLA_ROPE) ** -0.5
    pack, unpack = _make_mla_pack(tag + "pack", n_lat, n_ctx, scale)
    attn_fwd, attn_delta, attn_bwd = _make_attention(tag, scale, MLA_NOPE + MLA_ROPE)

    @jax.custom_vjp
    def mla(qp, kv, kr, kv_c, kr_c):
        q, k, _, _, vt = pack(qp, kv, kr, kv_c, kr_c)
        return attn_fwd(q, k, vt)[0]

    def fwd(qp, kv, kr, kv_c, kr_c):
        q, k, v, kt, vt = pack(qp, kv, kr, kv_c, kr_c)
        o, lse = attn_fwd(q, k, vt)
        return o, (q, k, kt, v, o, lse)

    def bwd(res, do):
        q, k, kt, v, o, lse = res
        delta = attn_delta(o, do, q.shape[0])
        dqt, dk, dv = attn_bwd(q, k, kt, v, do, lse, delta)
        return unpack(dqt, dk, dv)

    mla.defvjp(fwd, bwd)
    return mla


def _make_attention(tag, scale, dq_live=None):
    neg_big = -1e30
    log2e = 1.4426950408889634
    sub = 256

    def fwd_call(q, k, vt):
        hh, n, dq = q.shape
        dv, ll = vt.shape[1], vt.shape[2]
        tq, tk = _pick(n, 2048), _pick(ll, 1408)
        sb = sub if tk % sub == 0 else tk
        c2 = scale * log2e
        k_steps = ll // tk

        def body(q_ref, k_ref, vt_ref, o_ref, lse_ref, m_scr, l_scr, acc_scr, s_scr, p_scr):
            j = pl.program_id(2)

            @pl.when(j == 0)
            def _():
                m_scr[...] = jnp.full_like(m_scr, neg_big)
                l_scr[...] = jnp.zeros_like(l_scr)
                acc_scr[...] = jnp.zeros_like(acc_scr)

            q_t = q_ref[...]
            m_prev = m_scr[...]
            m_new = m_prev
            for kk in range(tk // sb):
                rows = slice(kk * sb, (kk + 1) * sb)
                s_t = _dot(k_ref[rows, :], q_t, NT)
                s_scr[rows, :] = s_t
                m_new = jnp.maximum(m_new, jnp.max(s_t, axis=0, keepdims=True))
            mc = m_new * c2
            l_part = jnp.zeros_like(m_new)
            for kk in range(tk // sb):
                rows = slice(kk * sb, (kk + 1) * sb)
                p_t = jnp.exp2(s_scr[rows, :] * c2 - mc)
                l_part = l_part + jnp.sum(p_t, axis=0, keepdims=True)
                p_scr[rows, :] = p_t.astype(BF16)
            alpha = jnp.exp2((m_prev - m_new) * c2)
            l_scr[...] = alpha * l_scr[...] + l_part
            acc_scr[...] = alpha * acc_scr[...] + _dot(vt_ref[...], p_scr[...], NN)
            m_scr[...] = m_new

            @pl.when(j == k_steps - 1)
            def _():
                o_ref[...] = jnp.transpose(acc_scr[...] / l_scr[...]).astype(BF16)
                lse_ref[...] = m_scr[...] * scale + jnp.log(l_scr[...])

        return pl.pallas_call(
            body, name=tag + "_fwd", grid=(hh, n // tq, k_steps),
            in_specs=[pl.BlockSpec((None, tq, dq), lambda h, i, j: (h, i, 0)),
                      pl.BlockSpec((None, tk, dq), lambda h, i, j: (h, j, 0)),
                      pl.BlockSpec((None, dv, tk), lambda h, i, j: (h, 0, j))],
            out_specs=[pl.BlockSpec((tq, dv), lambda h, i, j: (i, h)),
                       pl.BlockSpec((None, 1, tq), lambda h, i, j: (h, 0, i))],
            out_shape=[jax.ShapeDtypeStruct((n, hh * dv), BF16), jax.ShapeDtypeStruct((hh, 1, n), F32)],
            scratch_shapes=[pltpu.VMEM((1, tq), F32), pltpu.VMEM((1, tq), F32), pltpu.VMEM((dv, tq), F32),
                            pltpu.VMEM((tk, tq), F32), pltpu.VMEM((tk, tq), BF16)],
            compiler_params=_params(),
        )(q, k, vt)

    def delta_call(o, do, hh):
        n = o.shape[0]
        dv = o.shape[1] // hh
        tq = _pick(n, 1024)

        def body(o_ref, do_ref, d_ref):
            prod_t = jnp.transpose(o_ref[...].astype(F32) * do_ref[...].astype(F32))
            d_ref[...] = jnp.sum(prod_t, axis=0, keepdims=True)

        spec = pl.BlockSpec((tq, dv), lambda h, i: (i, h))
        return pl.pallas_call(
            body, name=tag + "_delta", grid=(hh, n // tq), in_specs=[spec, spec],
            out_specs=pl.BlockSpec((None, 1, tq), lambda h, i: (h, 0, i)),
            out_shape=jax.ShapeDtypeStruct((hh, 1, n), F32), compiler_params=_params(),
        )(o, do)

    def bwd_call(q, k, kt, v, do, lse, delta):
        hh, n, dq = q.shape
        ll, dv = k.shape[1], v.shape[2]
        tq, tk = _pick(n, 2048), _pick(ll, 1408)
        sb = tk
        c2 = scale * log2e
        q_steps = n // tq
        live = dq_live or dq

        def body(q_ref, k_ref, kt_ref, v_ref, do_ref, lse_ref, d_ref, dqt_ref, dk_ref, dv_ref, dk_scr, dv_scr):
            j = pl.program_id(1)
            i = pl.program_id(2)

            @pl.when(i == 0)
            def _():
                dk_scr[...] = jnp.zeros_like(dk_scr)
                dv_scr[...] = jnp.zeros_like(dv_scr)

            q_t, do_t = q_ref[...], do_ref[...]
            lse2 = lse_ref[...] * log2e
            delta_t = d_ref[...]
            dq_part = None
            for kk in range(tk // sb):
                rows = slice(kk * sb, (kk + 1) * sb)
                s_t = _dot(k_ref[rows, :], q_t, NT)
                p_t = jnp.exp2(s_t * c2 - lse2)
                ds_t = p_t * (_dot(v_ref[rows, :], do_t, NT) - delta_t)
                dv_scr[rows, :] += _dot(p_t, do_t, NN)
                dk_scr[rows, :] += _dot(ds_t, q_t, NN)
                part = _dot(kt_ref[:live, rows], ds_t, NN)
                dq_part = part if dq_part is None else dq_part + part
            cols = pl.ds(pl.multiple_of(i * tq, tq), tq)

            @pl.when(j == 0)
            def _():
                dqt_ref[:live, cols] = dq_part
                if live < dq:
                    dqt_ref[live:, cols] = jnp.zeros((dq - live, tq), F32)

            @pl.when(j > 0)
            def _():
                dqt_ref[:live, cols] += dq_part

            @pl.when(i == q_steps - 1)
            def _():
                dk_ref[...] = (dk_scr[...] * scale).astype(BF16)
                dv_ref[...] = dv_scr[...].astype(BF16)

        return pl.pallas_call(
            body, name=tag + "_bwd", grid=(hh, ll // tk, q_steps),
            in_specs=[pl.BlockSpec((None, tq, dq), lambda h, j, i: (h, i, 0)),
                      pl.BlockSpec((None, tk, dq), lambda h, j, i: (h, j, 0)),
                      pl.BlockSpec((None, dq, tk), lambda h, j, i: (h, 0, j)),
                      pl.BlockSpec((None, tk, dv), lambda h, j, i: (h, j, 0)),
                      pl.BlockSpec((tq, dv), lambda h, j, i: (i, h)),
                      pl.BlockSpec((None, 1, tq), lambda h, j, i: (h, 0, i)),
                      pl.BlockSpec((None, 1, tq), lambda h, j, i: (h, 0, i))],
            out_specs=[pl.BlockSpec((None, dq, n), lambda h, j, i: (h, 0, 0)),
                       pl.BlockSpec((None, tk, dq), lambda h, j, i: (h, j, 0)),
                       pl.BlockSpec((None, tk, dv), lambda h, j, i: (h, j, 0))],
            out_shape=[jax.ShapeDtypeStruct((hh, dq, n), F32), jax.ShapeDtypeStruct((hh, ll, dq), BF16),
                       jax.ShapeDtypeStruct((hh, ll, dv), BF16)],
            scratch_shapes=[pltpu.VMEM((tk, dq), F32), pltpu.VMEM((tk, dv), F32)],
            compiler_params=_params(),
        )(q, k, kt, v, do, lse, delta)

    return fwd_call, delta_call, bwd_call


def _loss_tile(x, g, tgt):
    r = lax.rsqrt(jnp.mean(x * x, axis=-1, keepdims=True) + RMS_EPS)
    err = x * r * g - tgt
    per_tok = jnp.mean(err * err, axis=-1, keepdims=True)
    return 0.5 * jnp.sum(per_tok, axis=0, keepdims=True)


def _loss_and_grad(x, g, tgt, name):
    t, d = x.shape
    tm = _pick(t, 256, 16)

    def body(x_ref, g_ref, t_ref, l_ref, dx_ref, dg_ref):
        loss, vjp = jax.vjp(_loss_tile, x_ref[...], g_ref[...], t_ref[...])
        dx, dg, _ = vjp(jnp.ones((1, 1), F32))
        l_ref[...] = jnp.broadcast_to(loss, (1, LANE))
        dx_ref[...] = dx

        @pl.when(pl.program_id(0) == 0)
        def _():
            dg_ref[...] = jnp.zeros_like(dg_ref)

        dg_ref[...] += dg

    parts, dx, dg = pl.pallas_call(
        body, name=name, grid=(t // tm,),
        in_specs=[_row_spec(tm, d), _vec_spec(d), _row_spec(tm, d)],
        out_specs=[pl.BlockSpec((None, 1, LANE), lambda i: (i, 0, 0)), _row_spec(tm, d), _vec_spec(d)],
        out_shape=[jax.ShapeDtypeStruct((t // tm, 1, LANE), F32), jax.ShapeDtypeStruct((t, d), F32),
                   jax.ShapeDtypeStruct((1, d), F32)],
        compiler_params=_params(),
    )(x, g, tgt)
    return jnp.sum(parts[:, 0, 0]), dx, dg


def _exchange(arrays, gather, name):
    n = len(arrays)

    def body(*refs):
        ins, outs = refs[:n], refs[n:2 * n]
        send_sems, recv_sems, local_sems = refs[2 * n:]
        me = 4 * lax.axis_index("x") + 2 * lax.axis_index("y") + lax.axis_index("c")

        def remote(a, d, wait_side=False):
            peer = (me + d) % N_DEV
            origin = (me + N_DEV - d) % N_DEV
            src = ins[a] if gather else ins[a].at[peer]
            dst = outs[a].at[origin if wait_side else me]
            return pltpu.make_async_remote_copy(
                src_ref=src, dst_ref=dst, send_sem=send_sems.at[a, d - 1], recv_sem=recv_sems.at[a, d - 1],
                device_id=(peer // 4, (peer // 2) % 2, peer % 2), device_id_type=pl.DeviceIdType.MESH)

        def local(a):
            src = ins[a] if gather else ins[a].at[me]
            return pltpu.make_async_copy(src, outs[a].at[me], local_sems.at[a])

        for a in range(n):
            for d in range(1, N_DEV):
                remote(a, d).start()
            local(a).start()
        for a in range(n):
            local(a).wait()
            for d in range(1, N_DEV):
                remote(a, d, wait_side=True).wait_recv()
                remote(a, d).wait_send()

    out_shape = []
    for arr in arrays:
        shape = (N_DEV,) + arr.shape if gather else arr.shape
        out_shape.append(jax.ShapeDtypeStruct(shape, arr.dtype))
    any_spec = pl.BlockSpec(memory_space=pl.ANY)
    return pl.pallas_call(
        body, name=name, in_specs=[any_spec] * n, out_specs=[any_spec] * n, out_shape=out_shape,
        scratch_shapes=[pltpu.SemaphoreType.DMA((n, N_DEV - 1)), pltpu.SemaphoreType.DMA((n, N_DEV - 1)),
                        pltpu.SemaphoreType.DMA((n,))],
        compiler_params=pltpu.CompilerParams(has_side_effects=True),
    )(*arrays)


def _split_copy(ins, lands, send_sems, recv_sems, a, d, gather, wait_side):
    me = 4 * lax.axis_index("x") + 2 * lax.axis_index("y") + lax.axis_index("c")
    peer = (me + d) % N_DEV
    origin = (me + N_DEV - d) % N_DEV
    return pltpu.make_async_remote_copy(
        src_ref=ins[a] if gather else ins[a].at[peer], dst_ref=lands[a].at[origin if wait_side else me],
        send_sem=send_sems.at[a * (N_DEV - 1) + d - 1], recv_sem=recv_sems.at[a * (N_DEV - 1) + d - 1],
        device_id=(peer // 4, (peer // 2) % 2, peer % 2), device_id_type=pl.DeviceIdType.MESH)


def _exchange_start(srcs, lands, after, gather, name):
    n = len(srcs)

    def body(*refs):
        ins, lnd = refs[:n], refs[n:2 * n]
        send_sems, recv_sems = refs[2 * n + 1], refs[2 * n + 2]
        for a in range(n):
            for d in range(1, N_DEV):
                _split_copy(ins, lnd, send_sems, recv_sems, a, d, gather, False).start()

    hbm = pl.BlockSpec(memory_space=pltpu.HBM)
    sem = pl.BlockSpec(memory_space=pltpu.SEMAPHORE)
    bufs = [pltpu.with_memory_space_constraint(t, pltpu.HBM) for t in list(srcs) + list(lands) + [after]]
    res = pl.pallas_call(
        body, name=name,
        in_specs=[hbm] * (2 * n + 1), out_specs=[sem, sem] + [hbm] * (2 * n + 1),
        out_shape=[pltpu.SemaphoreType.DMA((n * (N_DEV - 1),)), pltpu.SemaphoreType.DMA((n * (N_DEV - 1),))]
        + [pltpu.HBM(t.shape, t.dtype) for t in bufs],
        input_output_aliases={i: 2 + i for i in range(2 * n + 1)},
        compiler_params=pltpu.CompilerParams(has_side_effects=pltpu.SideEffectType.DATAFLOW_SIDE_EFFECTING),
    )(*bufs)
    return res[0], res[1], res[2:2 + n], res[2 + n:2 + 2 * n], res[-1]


def _exchange_wait(send_sems, recv_sems, srcs, lands, after, gather, name):
    n = len(srcs)

    def body(*refs):
        ins, lnd = refs[:n], refs[n:2 * n]
        send_sems_ref, recv_sems_ref = refs[2 * n], refs[2 * n + 1]
        for a in range(n):
            for d in range(1, N_DEV):
                _split_copy(ins, lnd, send_sems_ref, recv_sems_ref, a, d, gather, False).wait_send()
                _split_copy(ins, lnd, send_sems_ref, recv_sems_ref, a, d, gather, True).wait_recv()

    hbm = pl.BlockSpec(memory_space=pltpu.HBM)
    sem = pl.BlockSpec(memory_space=pltpu.SEMAPHORE)
    bufs = list(srcs) + list(lands)
    res = pl.pallas_call(
        body, name=name,
        in_specs=[hbm] * (2 * n) + [sem, sem, pl.BlockSpec(memory_space=pl.ANY)],
        out_specs=[hbm] * (2 * n),
        out_shape=[pltpu.HBM(t.shape, t.dtype) for t in bufs],
        input_output_aliases={i: i for i in range(2 * n)},
        compiler_params=pltpu.CompilerParams(has_side_effects=pltpu.SideEffectType.DATAFLOW_SIDE_EFFECTING),
    )(*bufs, send_sems, recv_sems, after)
    return res[n:]


def _own_slot(block, me):
    empty = lax.empty((N_DEV,) + block.shape, block.dtype)
    return lax.dynamic_update_slice(empty, block[None], (me,) + (0,) * block.ndim)


def _coords():
    return lax.axis_index("x"), lax.axis_index("y"), lax.axis_index("c")


def _other_chips(x, y):
    return [(1 - x, y), (x, 1 - y), (1 - x, 1 - y)]


def _gather_two_level(arrays, name):
    n = len(arrays)

    def body(*refs):
        ins, outs = refs[:n], refs[n:2 * n]
        send_sems, recv_sems, local_sems = refs[2 * n:]
        x, y, c = _coords()
        me, sib = (x, y, c), (x, y, 1 - c)
        chips = _other_chips(x, y)

        def copy(a, k, block, to, from_input=False):
            slot = 4 * block[0] + 2 * block[1] + block[2]
            return pltpu.make_async_remote_copy(
                src_ref=ins[a] if from_input else outs[a].at[slot], dst_ref=outs[a].at[slot],
                send_sem=send_sems.at[a, k], recv_sem=recv_sems.at[a, k],
                device_id=to, device_id_type=pl.DeviceIdType.MESH)

        def local(a):
            return pltpu.make_async_copy(ins[a], outs[a].at[4 * x + 2 * y + c], local_sems.at[a])

        for a in range(n):
            for j, chip in enumerate(chips):
                copy(a, 1 + j, me, (*chip, c), True).start()
            copy(a, 0, me, sib, True).start()
            local(a).start()
        for a in range(n):
            for j, chip in enumerate(chips):
                copy(a, 1 + j, (*chip, c), me).wait_recv()
                copy(a, 4 + j, (*chip, c), sib).start()
        for a in range(n):
            copy(a, 0, sib, me).wait_recv()
            for j, chip in enumerate(chips):
                copy(a, 4 + j, (*chip, 1 - c), me).wait_recv()
            for k in range(N_DEV - 1):
                copy(a, k, me, sib, True).wait_send()
            local(a).wait()

    any_spec = pl.BlockSpec(memory_space=pl.ANY)
    return pl.pallas_call(
        body, name=name, in_specs=[any_spec] * n, out_specs=[any_spec] * n,
        out_shape=[jax.ShapeDtypeStruct((N_DEV,) + arr.shape, arr.dtype) for arr in arrays],
        scratch_shapes=[pltpu.SemaphoreType.DMA((n, N_DEV - 1)), pltpu.SemaphoreType.DMA((n, N_DEV - 1)),
                        pltpu.SemaphoreType.DMA((n,))],
        compiler_params=pltpu.CompilerParams(has_side_effects=True),
    )(*arrays)


def _make_gather_op(tag):
    @jax.custom_vjp
    def gather_op(xl):
        return _exchange([xl], True, tag + "_gather")[0]

    def fwd(xl):
        return gather_op(xl), None

    def bwd(_, g):
        return (jnp.sum(_exchange([g], False, tag + "_scatter")[0], axis=0),)

    gather_op.defvjp(fwd, bwd)
    return gather_op


def _adamw(gstack, w, m, v, name):
    s, r, cn = gstack.shape
    tr = _pick(r, max(8, (2 * 1024 * 1024) // (4 * cn) // 8 * 8), 8)
    c1 = 1.0 - ADAM_B1 ** ADAM_STEP
    c2 = 1.0 - ADAM_B2 ** ADAM_STEP

    def body(g_ref, w_ref, m_ref, v_ref, go_ref, d_ref, mo_ref, vo_ref):
        g = g_ref[0].astype(F32)
        for q in range(1, s):
            g = g + g_ref[q].astype(F32)
        m_new = ADAM_B1 * m_ref[...] + (1.0 - ADAM_B1) * g
        v_new = ADAM_B2 * v_ref[...] + (1.0 - ADAM_B2) * (g * g)
        go_ref[...] = g
        mo_ref[...] = m_new
        vo_ref[...] = v_new
        d_ref[...] = -ADAM_LR * ((m_new / c1) / (jnp.sqrt(v_new / c2) + ADAM_EPS) + ADAM_WD * w_ref[...])

    tile = pl.BlockSpec((tr, cn), lambda i: (i, 0))
    out = jax.ShapeDtypeStruct((r, cn), F32)
    return pl.pallas_call(
        body, name=name, grid=(r // tr,),
        in_specs=[pl.BlockSpec((s, tr, cn), lambda i: (0, i, 0)), tile, tile, tile],
        out_specs=[tile, tile, tile, tile], out_shape=[out, out, out, out],
        compiler_params=_params(),
    )(gstack, w, m, v)


def _cols_from_stack(w):
    return jnp.swapaxes(w, 0, 1).reshape(w.shape[1], N_DEV * w.shape[2])


def _ada_vectors(p, silu_c_all, me):
    d = p["c_ctx"].shape[0]
    n_a = p["ada_w"].shape[1]
    a_in = jnp.concatenate([silu_c_all, jax.nn.silu(p["c_ctx"])[None, :], jnp.zeros((7, d), F32)], axis=0)
    b_loc = lax.dynamic_slice(p["ada_b"], (0, me * n_a), (1, n_a))
    r_loc = _make_small_mm("ada")(a_in, p["ada_w"]) + b_loc
    r_full = _make_gather_op("ada")(r_loc)
    m_lat = lax.dynamic_index_in_dim(r_full, me, axis=1, keepdims=False).reshape(N_MOD, 1, d)
    m_ctx = r_full[:, N_DEV, :].reshape(N_MOD, 1, d)
    return m_lat, m_ctx


def _stage_a_fwd(p_ada, x, ctx, ng, w_in, wait_w_out, silu_c_all, me):
    (m_lat, m_ctx), vjp_ada = jax.vjp(lambda q: _ada_vectors(q, silu_c_all, me), p_ada)
    lat, cx = _ffn_parts("ffn1"), _ffn_parts("ffn1c")
    h, a, gu = lat[0](x, ng, m_lat[0], m_lat[1], w_in)
    hc, ac, guc = cx[0](ctx, ng, m_ctx[0], m_ctx[1], w_in)
    w_out = wait_w_out(a)
    x1, f1 = lat[1](a, w_out, x, m_lat[2])
    c1, f1c = cx[1](ac, w_out, ctx, m_ctx[2])
    res = dict(vjp_ada=vjp_ada, m_lat=m_lat, m_ctx=m_ctx, x=x, ctx=ctx, ng=ng, w_in=w_in, w_out=w_out,
               lat=(h, a, gu, f1), cx=(hc, ac, guc, f1c))
    return (x1, c1, m_lat, m_ctx), res


def _stage_a_bwd(res, dx1, dc1, dm_lat, dm_ctx, start_grads):
    lat, cx = _ffn_parts("ffn1"), _ffn_parts("ffn1c")
    m_lat, m_ctx, ng, w_in, w_out = res["m_lat"], res["m_ctx"], res["ng"], res["w_in"], res["w_out"]
    hc, ac, guc, f1c = res["cx"]
    dgate_c, dgu_c, dw_out_c = cx[2](dc1, f1c, m_ctx[2], w_out, guc, ac)
    _, dng_c, dsh_c, dsc_c, dw_in_c = cx[3](res["ctx"], ng, m_ctx[0], m_ctx[1], w_in, hc, dgu_c, dc1)
    h, a, gu, f1 = res["lat"]
    dgate, dgu, dw_out = lat[2](dx1, f1, m_lat[2], w_out, gu, a)
    dgu = start_grads("ffn1_w_out", dw_out + dw_out_c, dgu)
    dw_in = _ffn_dwin_bwd(h, dgu, "ffn1_dwin") + dw_in_c
    dgu = start_grads("ffn1_w_in", dw_in, dgu)
    dh = _ffn_dh_bwd(dgu, w_in, "ffn1_dh")
    dx, dng, dsc, dsh = _norm_mod_bwd(res["x"], ng, m_lat[1], m_lat[0], dh, dx1, "ffn1_dnorm")

    def rows(dsh_, dsc_, dgate_):
        return jnp.concatenate([dsh_, dsc_, dgate_, jnp.zeros((N_MOD - 3,) + dsh_.shape, F32)[:, 0]], axis=0)[:, None, :]

    (g_ada,) = res["vjp_ada"]((dm_lat + rows(dsh, dsc, dgate), dm_ctx + rows(dsh_c, dsc_c, dgate_c)))
    return g_ada, dx, dng + dng_c


def _stage_b(p, x1, c1, m_lat, m_ctx):
    n_lat, d = x1.shape
    n_ctx = c1.shape[0]
    w_mix = jnp.pad(_cols_from_stack(p["mix_w_in"]), ((0, 0), (0, MIX_IN_PAD - MIX_IN)))
    proj, x1 = _make_norm_proj_carry("mix")(x1, p["norm2_g"], m_lat[3], m_lat[4], w_mix)
    proj_c = _make_norm_proj("mixc")(c1, p["norm2_g"], m_ctx[3], m_ctx[4], w_mix)
    widths = SPLITS[:6] + (LANE,)
    rq, rk, rv, rg, cq, ckv, kr = _make_split("mixsplit", widths, MIX_IN_PAD)(proj)
    _, crk, crv, _, _, cckv, ckr = _make_split("mixsplitc", widths, MIX_IN_PAD)(proj_c)

    zq = jnp.zeros((1, MLA_Q_RANK), F32)
    zkv = jnp.zeros((1, MLA_KV_RANK), F32)
    w_uq3 = _cols_from_stack(p["mla_w_uq"]).reshape(MLA_Q_RANK, MLA_HEADS, MLA_NOPE + MLA_ROPE)
    w_uq = jnp.concatenate([w_uq3[:, :, :MLA_NOPE].reshape(MLA_Q_RANK, -1),
                            w_uq3[:, :, MLA_NOPE:].reshape(MLA_Q_RANK, -1)], axis=1)
    w_ukv = _cols_from_stack(p["mla_w_ukv"])
    q = _make_norm_proj("uq")(cq, p["mla_q_norm_g"], zq, zq, w_uq)
    kv = _make_norm_proj("ukv")(ckv, p["mla_kv_norm_g"], zkv, zkv, w_ukv)
    kv_c = _make_norm_proj("ukvc")(cckv, p["mla_kv_norm_g"], zkv, zkv, w_ukv)

    lg_f = jax.nn.log_sigmoid(p["ret_decay_fwd"][0])
    lg_b = jax.nn.log_sigmoid(p["ret_decay_bwd"][0])
    lat_f, lat_b, ctx_f, ctx_b = _make_ret_pack("retpack", n_lat, n_ctx)(rq, rk, rv, crk, crv)
    assert n_ctx == RET_CHUNK, "the context prefix is one retention chunk"

    def lanes(lg):
        return jnp.broadcast_to(lg[:, None, None], (RET_HEADS, 1, LANE))

    s0_f = _make_ctx_state("retcf", False)(*ctx_f, lanes(lg_f))
    s0_b = _make_ctx_state("retcb", True)(*ctx_b, lanes(lg_b))
    y_f = _make_ret_dir("retf", False)(*lat_f, lanes(lg_f), s0_f)
    y_b = _make_ret_dir("retb", True)(*lat_b, lanes(lg_b), s0_b)
    ret_o = _make_ret_out("reto")(y_f, y_b, rg)

    mla_o = _make_mla("mla", n_lat, n_ctx)(q, kv, kr, kv_c, ckr)

    w_mo = p["mix_w_out"].reshape(-1, d)
    return _make_res_proj("mixo")(jnp.concatenate([ret_o, mla_o], axis=-1), w_mo, x1, m_lat[5])


def _stage_c(p, x2, m_lat):
    return _make_ffn_block("ffn2")(x2, p["norm3_g"], m_lat[6], m_lat[7], m_lat[8], p["ffn2_w_in"], p["ffn2_w_out"])


FIRST = ("ffn1_w_in", "ffn1_w_out")
MID = ("mix_w_in", "mla_w_uq", "mla_w_ukv", "mix_w_out")
LAST = ("ffn2_w_in", "ffn2_w_out")
BIG = FIRST + MID + LAST
SMALL = ("c_ctx", "ada_b", "norm1_g", "norm2_g", "ret_decay_fwd", "ret_decay_bwd", "mla_q_norm_g",
         "mla_kv_norm_g", "norm3_g", "final_norm_g")
WEIGHTS = ("c_ctx", "ada_w", "ada_b", "norm1_g", "ffn1_w_in", "ffn1_w_out", "norm2_g", "mix_w_in", "ret_decay_fwd",
           "ret_decay_bwd", "mla_q_norm_g", "mla_w_uq", "mla_kv_norm_g", "mla_w_ukv", "mix_w_out", "norm3_g",
           "ffn2_w_in", "ffn2_w_out", "final_norm_g")


def _pack(parts):
    flat = jnp.concatenate([t.reshape(-1) for t in parts])
    pad = (-flat.shape[0]) % LANE
    return jnp.pad(flat, (0, pad)).reshape(1, -1)


def _unpack(flat, like):
    out, off = [], 0
    for t in like:
        out.append(flat[0, off:off + t.size].reshape(t.shape))
        off += t.size
    return out


def kernel(x, c, ctx, c_ctx, ada_w, ada_b, norm1_g, ffn1_w_in, ffn1_w_out, norm2_g, mix_w_in, ret_decay_fwd, ret_decay_bwd, mla_q_norm_g, mla_w_uq, mla_kv_norm_g, mla_w_ukv, mix_w_out, norm3_g, ffn2_w_in, ffn2_w_out, final_norm_g, loss_target, m_c_ctx, m_ada_w, m_ada_b, m_norm1_g, m_ffn1_w_in, m_ffn1_w_out, m_norm2_g, m_mix_w_in, m_ret_decay_fwd, m_ret_decay_bwd, m_mla_q_norm_g, m_mla_w_uq, m_mla_kv_norm_g, m_mla_w_ukv, m_mix_w_out, m_norm3_g, m_ffn2_w_in, m_ffn2_w_out, m_final_norm_g, v_c_ctx, v_ada_w, v_ada_b, v_norm1_g, v_ffn1_w_in, v_ffn1_w_out, v_norm2_g, v_mix_w_in, v_ret_decay_fwd, v_ret_decay_bwd, v_mla_q_norm_g, v_mla_w_uq, v_mla_kv_norm_g, v_mla_w_ukv, v_mix_w_out, v_norm3_g, v_ffn2_w_in, v_ffn2_w_out, v_final_norm_g):
    w = dict(c_ctx=c_ctx, ada_w=ada_w, ada_b=ada_b, norm1_g=norm1_g, ffn1_w_in=ffn1_w_in, ffn1_w_out=ffn1_w_out,
             norm2_g=norm2_g, mix_w_in=mix_w_in, ret_decay_fwd=ret_decay_fwd, ret_decay_bwd=ret_decay_bwd,
             mla_q_norm_g=mla_q_norm_g, mla_w_uq=mla_w_uq, mla_kv_norm_g=mla_kv_norm_g, mla_w_ukv=mla_w_ukv,
             mix_w_out=mix_w_out, norm3_g=norm3_g, ffn2_w_in=ffn2_w_in, ffn2_w_out=ffn2_w_out,
             final_norm_g=final_norm_g)
    mom = dict(c_ctx=m_c_ctx, ada_w=m_ada_w, ada_b=m_ada_b, norm1_g=m_norm1_g, ffn1_w_in=m_ffn1_w_in,
               ffn1_w_out=m_ffn1_w_out, norm2_g=m_norm2_g, mix_w_in=m_mix_w_in, ret_decay_fwd=m_ret_decay_fwd,
               ret_decay_bwd=m_ret_decay_bwd, mla_q_norm_g=m_mla_q_norm_g, mla_w_uq=m_mla_w_uq,
               mla_kv_norm_g=m_mla_kv_norm_g, mla_w_ukv=m_mla_w_ukv, mix_w_out=m_mix_w_out, norm3_g=m_norm3_g,
               ffn2_w_in=m_ffn2_w_in, ffn2_w_out=m_ffn2_w_out, final_norm_g=m_final_norm_g)
    var = dict(c_ctx=v_c_ctx, ada_w=v_ada_w, ada_b=v_ada_b, norm1_g=v_norm1_g, ffn1_w_in=v_ffn1_w_in,
               ffn1_w_out=v_ffn1_w_out, norm2_g=v_norm2_g, mix_w_in=v_mix_w_in, ret_decay_fwd=v_ret_decay_fwd,
               ret_decay_bwd=v_ret_decay_bwd, mla_q_norm_g=v_mla_q_norm_g, mla_w_uq=v_mla_w_uq,
               mla_kv_norm_g=v_mla_kv_norm_g, mla_w_ukv=v_mla_w_ukv, mix_w_out=v_mix_w_out, norm3_g=v_norm3_g,
               ffn2_w_in=v_ffn2_w_in, ffn2_w_out=v_ffn2_w_out, final_norm_g=v_final_norm_g)
    me = 4 * lax.axis_index("x") + 2 * lax.axis_index("y") + lax.axis_index("c")

    shard = {k: w[k][0].astype(BF16) for k in BIG}
    first = _gather_two_level([shard["ffn1_w_in"], jax.nn.silu(c)], "weights_gather")
    silu_c_all = first[-1][:, 0, :]

    def start_gather(names, after, name):
        return _exchange_start([shard[k] for k in names], [_own_slot(shard[k], me) for k in names], after, True, name)

    wout_start = start_gather(FIRST[1:], first[0], "ffn1_wout_start")
    mid_start = start_gather(MID, wout_start[4], "mixer_weights_start")
    last_start = start_gather(LAST, mid_start[4], "ffn2_weights_start")

    def wait_w_out(after):
        return _exchange_wait(*wout_start[:4], after, True, "ffn1_wout_wait")[0]

    pa = dict(ada_w=ada_w[0], c_ctx=c_ctx, ada_b=ada_b)
    (x1, c1, m_lat, m_ctx), res_a = _stage_a_fwd(pa, x[0], ctx[0], norm1_g, last_start[4], wait_w_out, silu_c_all, me)

    mid = _exchange_wait(mid_start[0], mid_start[1], mid_start[2], mid_start[3], x1, True, "mixer_weights_wait")
    pb = dict(zip(MID, mid))
    for k in ("norm2_g", "mla_q_norm_g", "mla_kv_norm_g", "ret_decay_fwd", "ret_decay_bwd"):
        pb[k] = w[k]
    x2, vjp_b = jax.vjp(_stage_b, pb, x1, c1, m_lat, m_ctx)

    last = _exchange_wait(last_start[0], last_start[1], last_start[2], last_start[3], x2, True, "ffn2_weights_wait")
    pc = dict(zip(LAST, last), norm3_g=norm3_g)
    x3, vjp_c = jax.vjp(_stage_c, pc, x2, m_lat)
    loss_local, dx3, d_final_g = _loss_and_grad(x3, final_norm_g[None, :], loss_target[0], "loss_head")

    gc, dx2, dm_c = vjp_c(dx3)
    last_scat = _exchange_start([gc[k] for k in LAST],
                                [_own_slot(lax.dynamic_index_in_dim(gc[k], me, 0, False), me) for k in LAST],
                                dx2, False, "ffn2_grads_start")
    gb, dx1, dc1, dm_b, dmc_b = vjp_b(last_scat[4])
    mid_scat = _exchange_start([gb[k] for k in MID],
                               [_own_slot(lax.dynamic_index_in_dim(gb[k], me, 0, False), me) for k in MID],
                               dx1, False, "mixer_grads_start")
    first_scat = {}

    def start_grads(name, dw, after):
        first_scat[name] = _exchange_start(
            [dw], [_own_slot(lax.dynamic_index_in_dim(dw, me, 0, False), me)], after, False, name + "_grads_start")
        return first_scat[name][4]

    g_ada, dx, dng1 = _stage_a_bwd(res_a, mid_scat[4], dc1, dm_b + dm_c, dmc_b, start_grads)
    grads = {**g_ada, **gb, **gc, "x": dx, "norm1_g": dng1, "final_norm_g": d_final_g[0]}

    exchanged = {k: _exchange_wait(*first_scat[k][:4], dx, False, k + "_grads_wait")[0] for k in FIRST}
    exchanged.update(zip(LAST, _exchange_wait(*last_scat[:4], dx, False, "ffn2_grads_wait")))
    exchanged.update(zip(MID, _exchange_wait(*mid_scat[:4], dx, False, "mixer_grads_wait")))
    zero1 = [jnp.zeros((1,), F32)]
    small_like = zero1 + [w[k] for k in SMALL]
    small_all = _exchange([_pack([loss_local.reshape(1)] + [grads[k] for k in SMALL])], True, "small_grads_gather")[0]
    loss = jnp.sum(small_all[:, 0, 0])

    out_g, out_d, out_m, out_v = {}, {}, {}, {}

    def update(name, gstack, shape2d):
        res = _adamw(gstack, w[name].reshape(shape2d), mom[name].reshape(shape2d), var[name].reshape(shape2d),
                     "adamw_" + name)
        out_g[name], out_d[name], out_m[name], out_v[name] = [t.reshape(w[name].shape) for t in res]

    for k in BIG:
        update(k, exchanged[k], exchanged[k].shape[1:])
    update("ada_w", grads["ada_w"][None], ada_w.shape[1:])
    res = _adamw(small_all, _pack(small_like), _pack(zero1 + [mom[k] for k in SMALL]),
                 _pack(zero1 + [var[k] for k in SMALL]), "adamw_small")
    for dst, flat in zip((out_g, out_d, out_m, out_v), res):
        for k, t in zip(SMALL, _unpack(flat, small_like)[1:]):
            dst[k] = t

    return (loss, grads["x"][None], *[out_g[k] for k in WEIGHTS], *[out_d[k] for k in WEIGHTS],
            *[out_m[k] for k in WEIGHTS], *[out_v[k] for k in WEIGHTS])
```

```python
import functools

import jax
import jax.numpy as jnp
from jax import lax
from jax.experimental import pallas as pl
from jax.experimental.pallas import tpu as pltpu

F32 = jnp.float32
BF16 = jnp.bfloat16

N_DEV = 8
MESH_AXES = ("x", "y", "c")

GRID_W = 64
N_MOD = 9
RET_HEADS = 8
RET_DK = 64
RET_DV = 128
RET_CHUNK = 256
RET_ROPE_BASE = 10000.0
MLA_HEADS = 8
MLA_Q_RANK = 512
MLA_KV_RANK = 256
MLA_NOPE = 128
MLA_ROPE = 64
MLA_V = 128
AXIAL_BASE = 10000.0
RMS_EPS = 1e-6
GN_EPS = 1e-5
SPLITS = (RET_HEADS * RET_DK, RET_HEADS * RET_DK, RET_HEADS * RET_DV, RET_HEADS * RET_DV,
          MLA_Q_RANK, MLA_KV_RANK, MLA_ROPE)
MIX_IN = sum(SPLITS)
MIX_IN_PAD = 4096

ADAM_LR = 0.001
ADAM_B1 = 0.9
ADAM_B2 = 0.999
ADAM_EPS = 1e-08
ADAM_WD = 0.01
ADAM_STEP = 10

LANE = 128
RET_DKP = LANE
VMEM_LIMIT_BYTES = 56 * 1024 * 1024

NN = ((1,), (0,))
NT = ((1,), (1,))
TN = ((0,), (0,))


def _pick(dim, target, align=LANE):
    t = min(dim, target)
    t -= t % align
    while t >= align:
        if dim % t == 0:
            return t
        t -= align
    return dim


def _params():
    return pltpu.CompilerParams(vmem_limit_bytes=VMEM_LIMIT_BYTES)


def _dot(a, b, dims):
    return lax.dot_general(a.astype(BF16), b.astype(BF16), (dims, ((), ())), preferred_element_type=F32)


def _mm_call(name, grid, ins, pairs, outs, acc_shapes, epilogue):
    n_in, n_out = len(ins), len(outs)
    k_axis = len(grid) - 1
    k_steps = grid[k_axis]

    def body(*refs):
        in_refs = refs[:n_in]
        out_refs = refs[n_in:n_in + n_out]
        accs = refs[n_in + n_out:]
        k = pl.program_id(k_axis)

        @pl.when(k == 0)
        def _():
            for acc in accs:
                acc[...] = jnp.zeros_like(acc)

        for ai, bi, dims, ci in pairs:
            accs[ci][...] += _dot(in_refs[ai][...], in_refs[bi][...], dims)

        @pl.when(k == k_steps - 1)
        def _():
            epilogue([acc[...] for acc in accs], in_refs, out_refs)

    res = pl.pallas_call(
        body, name=name, grid=grid,
        in_specs=[s for _, s in ins], out_specs=[s for _, s in outs],
        out_shape=[s for s, _ in outs],
        scratch_shapes=[pltpu.VMEM(s, F32) for s in acc_shapes],
        compiler_params=_params(),
    )(*[a for a, _ in ins])
    return res


def _matmul(a, b, mode, out_dtype, name, tm=1024, tn=1024, tk=512, add=None):
    if mode == "nn":
        (m, kd), n = a.shape, b.shape[1]
    elif mode == "nt":
        (m, kd), n = a.shape, b.shape[0]
    else:
        (kd, m), n = a.shape, b.shape[1]
    tm, tn = _pick(m, tm, 16), _pick(n, tn)
    tk = _pick(kd, tk) if mode != "tn" else _pick(kd, tk, 16)
    if mode == "nn":
        a_spec = pl.BlockSpec((tm, tk), lambda i, j, k: (i, k))
        b_spec = pl.BlockSpec((tk, tn), lambda i, j, k: (k, j))
        dims = NN
    elif mode == "nt":
        a_spec = pl.BlockSpec((tm, tk), lambda i, j, k: (i, k))
        b_spec = pl.BlockSpec((tn, tk), lambda i, j, k: (j, k))
        dims = NT
    else:
        a_spec = pl.BlockSpec((tk, tm), lambda i, j, k: (k, i))
        b_spec = pl.BlockSpec((tk, tn), lambda i, j, k: (k, j))
        dims = TN

    def epilogue(accs, in_refs, out_refs):
        acc = accs[0] if add is None else accs[0] + in_refs[2][...].astype(F32)
        out_refs[0][...] = acc.astype(out_dtype)

    tile = pl.BlockSpec((tm, tn), lambda i, j, k: (i, j))
    return _mm_call(
        name, (m // tm, n // tn, kd // tk), [(a, a_spec), (b, b_spec)] + ([] if add is None else [(add, tile)]),
        [(0, 1, dims, 0)], [(jax.ShapeDtypeStruct((m, n), out_dtype), tile)], [(tm, tn)], epilogue)[0]


def _norm_mod_tile(x, ng, sc, sh):
    r = lax.rsqrt(jnp.mean(x * x, axis=-1, keepdims=True) + RMS_EPS)
    return (x * r * ng) * (1.0 + sc) + sh


def _row_spec(tm, d):
    return pl.BlockSpec((tm, d), lambda i: (i, 0))


def _vec_spec(d):
    return pl.BlockSpec((1, d), lambda i: (0, 0))


def _norm_mod_fwd(x, ng, sc, sh, name):
    t, d = x.shape
    tm = _pick(t, 512, 16)

    def body(x_ref, ng_ref, sc_ref, sh_ref, h_ref):
        h_ref[...] = _norm_mod_tile(x_ref[...], ng_ref[...], sc_ref[...], sh_ref[...]).astype(BF16)

    return pl.pallas_call(
        body, name=name, grid=(t // tm,),
        in_specs=[_row_spec(tm, d), _vec_spec(d), _vec_spec(d), _vec_spec(d)],
        out_specs=_row_spec(tm, d), out_shape=jax.ShapeDtypeStruct((t, d), BF16),
        compiler_params=_params(),
    )(x, ng, sc, sh)


def _norm_mod_bwd(x, ng, sc, sh, dh, dres, name):
    t, d = x.shape
    tm = _pick(t, 256, 16)
    has_res = dres is not None

    def body(*refs):
        if has_res:
            x_ref, ng_ref, sc_ref, sh_ref, dh_ref, dres_ref, dx_ref, dng_ref, dsc_ref, dsh_ref = refs
        else:
            x_ref, ng_ref, sc_ref, sh_ref, dh_ref, dx_ref, dng_ref, dsc_ref, dsh_ref = refs
        _, vjp = jax.vjp(_norm_mod_tile, x_ref[...], ng_ref[...], sc_ref[...], sh_ref[...])
        dx, dng, dsc, dsh = vjp(dh_ref[...].astype(F32))
        if has_res:
            dx = dx + dres_ref[...]
        dx_ref[...] = dx

        @pl.when(pl.program_id(0) == 0)
        def _():
            dng_ref[...] = jnp.zeros_like(dng_ref)
            dsc_ref[...] = jnp.zeros_like(dsc_ref)
            dsh_ref[...] = jnp.zeros_like(dsh_ref)

        dng_ref[...] += dng
        dsc_ref[...] += dsc
        dsh_ref[...] += dsh

    ins = [x, ng, sc, sh, dh] + ([dres] if has_res else [])
    in_specs = [_row_spec(tm, d), _vec_spec(d), _vec_spec(d), _vec_spec(d), _row_spec(tm, d)]
    in_specs += [_row_spec(tm, d)] if has_res else []
    vec = jax.ShapeDtypeStruct((1, d), F32)
    return pl.pallas_call(
        body, name=name, grid=(t // tm,), in_specs=in_specs,
        out_specs=[_row_spec(tm, d), _vec_spec(d), _vec_spec(d), _vec_spec(d)],
        out_shape=[jax.ShapeDtypeStruct((t, d), F32), vec, vec, vec],
        compiler_params=_params(),
    )(*ins)


def _res_mm_fwd(a, w, x, gate, coef, name):
    t, kd = a.shape
    d = w.shape[1]
    tm, tn, tk = _pick(t, 1024, 16), _pick(d, 1024), _pick(kd, 2816)

    def epilogue(accs, in_refs, out_refs):
        f = accs[0]
        out_refs[0][...] = in_refs[2][...] + (coef * in_refs[3][...]) * f
        out_refs[1][...] = f.astype(BF16)

    tile = pl.BlockSpec((tm, tn), lambda i, j, k: (i, j))
    return _mm_call(
        name, (t // tm, d // tn, kd // tk),
        [(a, pl.BlockSpec((tm, tk), lambda i, j, k: (i, k))), (w, pl.BlockSpec((tk, tn), lambda i, j, k: (k, j))),
         (x, tile), (gate, pl.BlockSpec((1, tn), lambda i, j, k: (0, j)))],
        [(0, 1, NN, 0)],
        [(jax.ShapeDtypeStruct((t, d), F32), tile), (jax.ShapeDtypeStruct((t, d), BF16), tile)],
        [(tm, tn)], epilogue)


def _gate_bwd(dxo, f, gate, coef, name):
    t, d = dxo.shape
    tm = _pick(t, 512, 16)

    def body(dxo_ref, f_ref, gate_ref, df_ref, dgate_ref):
        dxo_t = dxo_ref[...]
        df_ref[...] = ((coef * gate_ref[...]) * dxo_t).astype(BF16)

        @pl.when(pl.program_id(0) == 0)
        def _():
            dgate_ref[...] = jnp.zeros_like(dgate_ref)

        dgate_ref[...] += coef * jnp.sum(dxo_t * f_ref[...].astype(F32), axis=0, keepdims=True)

    return pl.pallas_call(
        body, name=name, grid=(t // tm,),
        in_specs=[_row_spec(tm, d), _row_spec(tm, d), _vec_spec(d)],
        out_specs=[_row_spec(tm, d), _vec_spec(d)],
        out_shape=[jax.ShapeDtypeStruct((t, d), BF16), jax.ShapeDtypeStruct((1, d), F32)],
        compiler_params=_params(),
    )(dxo, f, gate)


def _ffn_in_fwd(h, w_in, name):
    t, d = h.shape
    n = w_in.shape[2]
    half = N_DEV // 2
    f = half * n
    tm = _pick(t, 512, 16)

    def epilogue(accs, in_refs, out_refs):
        g, u = accs
        s = jax.nn.sigmoid(g)
        silu = g * s
        out_refs[0][...] = (silu * u).astype(BF16)
        out_refs[1][0] = (u * (s * (1.0 + g * (1.0 - s)))).astype(BF16)
        out_refs[1][1] = silu.astype(BF16)

    return _mm_call(
        name, (half, t // tm, 1),
        [(h, pl.BlockSpec((tm, d), lambda j, i, k: (i, 0))),
         (w_in, pl.BlockSpec((None, d, n), lambda j, i, k: (j, 0, 0))),
         (w_in, pl.BlockSpec((None, d, n), lambda j, i, k: (j + half, 0, 0)))],
        [(0, 1, NN, 0), (0, 2, NN, 1)],
        [(jax.ShapeDtypeStruct((t, f), BF16), pl.BlockSpec((tm, n), lambda j, i, k: (i, j))),
         (jax.ShapeDtypeStruct((2, t, f), BF16), pl.BlockSpec((2, tm, n), lambda j, i, k: (0, i, j)))],
        [(tm, n), (tm, n)], epilogue)


def _ffn_da_bwd(df, w_out2d, gu, name):
    t, d = df.shape
    f = w_out2d.shape[0]
    half = N_DEV // 2
    n = f // half
    tm = _pick(t, 512, 16)
    step = 4 * LANE
    chunks = [(c, min(c + step, n)) for c in range(0, n, step)]

    def body(df_ref, w_ref, gu_ref, o_ref):
        df_t = df_ref[...]
        for c0, c1 in chunks:
            da = _dot(df_t, w_ref[c0:c1, :], NT)
            o_ref[0, :, c0:c1] = (da * gu_ref[0, :, c0:c1].astype(F32)).astype(BF16)
            o_ref[1, :, c0:c1] = (da * gu_ref[1, :, c0:c1].astype(F32)).astype(BF16)

    gu_spec = pl.BlockSpec((2, tm, n), lambda j, i: (0, i, j))
    return pl.pallas_call(
        body, name=name, grid=(half, t // tm),
        in_specs=[pl.BlockSpec((tm, d), lambda j, i: (i, 0)), pl.BlockSpec((n, d), lambda j, i: (j, 0)), gu_spec],
        out_specs=gu_spec, out_shape=jax.ShapeDtypeStruct((2, t, f), BF16), compiler_params=_params(),
    )(df, w_out2d, gu)


def _ffn_dh_bwd(dgu, w_in, name):
    _, t, f = dgu.shape
    d, n = w_in.shape[1], w_in.shape[2]
    half = N_DEV // 2
    tm = _pick(t, 512, 16)

    def epilogue(accs, in_refs, out_refs):
        out_refs[0][...] = accs[0]

    return _mm_call(
        name, (t // tm, 1, half),
        [(dgu, pl.BlockSpec((None, tm, n), lambda i, j, k: (0, i, k))),
         (dgu, pl.BlockSpec((None, tm, n), lambda i, j, k: (1, i, k))),
         (w_in, pl.BlockSpec((None, d, n), lambda i, j, k: (k, 0, 0))),
         (w_in, pl.BlockSpec((None, d, n), lambda i, j, k: (k + half, 0, 0)))],
        [(0, 2, NT, 0), (1, 3, NT, 0)],
        [(jax.ShapeDtypeStruct((t, d), F32), pl.BlockSpec((tm, d), lambda i, j, k: (i, 0)))],
        [(tm, d)], epilogue)[0]


def _ffn_dwin_bwd(h, dgu, name, add=None):
    t, d = h.shape
    f = dgu.shape[2]
    half = N_DEV // 2
    n = f // half
    tk = _pick(t, 1024, 16)

    def epilogue(accs, in_refs, out_refs):
        acc = accs[0] if add is None else accs[0] + in_refs[2][...].astype(F32)
        out_refs[0][...] = acc.astype(BF16)

    shard = pl.BlockSpec((None, d, n), lambda j, i, k: (j, 0, 0))
    return _mm_call(
        name, (N_DEV, 1, t // tk),
        [(h, pl.BlockSpec((tk, d), lambda j, i, k: (k, 0))),
         (dgu, pl.BlockSpec((None, tk, n), lambda j, i, k: (j // half, k, j % half)))]
        + ([] if add is None else [(add, shard)]),
        [(0, 1, TN, 0)], [(jax.ShapeDtypeStruct((N_DEV, d, n), BF16), shard)], [(d, n)], epilogue)[0]


def _ffn_parts(tag):
    def w2d(w_out):
        return w_out.reshape(w_out.shape[0] * w_out.shape[1], w_out.shape[2])

    def fwd_in(x, ng, sh, sc, w_in):
        h = _norm_mod_fwd(x, ng, sc, sh, tag + "_norm")
        a, gu = _ffn_in_fwd(h, w_in, tag + "_in")
        return h, a, gu

    def fwd_out(a, w_out, x, gate):
        return _res_mm_fwd(a, w2d(w_out), x, gate, 0.5, tag + "_out")

    def bwd_out(dxo, f1, gate, w_out, gu, a, dw_out_add=None):
        df, dgate = _gate_bwd(dxo, f1, gate, 0.5, tag + "_dgate")
        dgu = _ffn_da_bwd(df, w2d(w_out), gu, tag + "_da")
        f = w_out.shape[0] * w_out.shape[1]
        dw_out = _matmul(a, df, "tn", BF16, tag + "_dwout", tm=_pick(f, 1408, 16), tn=2048, tk=1024,
                         add=None if dw_out_add is None else w2d(dw_out_add))
        return dgate, dgu, dw_out.reshape(w_out.shape)

    def bwd_in(x, ng, sh, sc, w_in, h, dgu, dxo):
        dh = _ffn_dh_bwd(dgu, w_in, tag + "_dh")
        dw_in = _ffn_dwin_bwd(h, dgu, tag + "_dwin")
        dx, dng, dsc, dsh = _norm_mod_bwd(x, ng, sc, sh, dh, dxo, tag + "_dnorm")
        return dx, dng, dsh, dsc, dw_in

    return fwd_in, fwd_out, bwd_out, bwd_in


def _make_ffn_block(tag):
    fwd_in, fwd_out, bwd_out, bwd_in = _ffn_parts(tag)

    @jax.custom_vjp
    def ffn_block(x, ng, sh, sc, gate, w_in, w_out):
        return fwd(x, ng, sh, sc, gate, w_in, w_out)[0]

    def fwd(x, ng, sh, sc, gate, w_in, w_out):
        h, a, gu = fwd_in(x, ng, sh, sc, w_in)
        xo, f1 = fwd_out(a, w_out, x, gate)
        return xo, (x, ng, sh, sc, gate, w_in, w_out, h, a, gu, f1)

    def bwd(res, dxo):
        x, ng, sh, sc, gate, w_in, w_out, h, a, gu, f1 = res
        dgate, dgu, dw_out = bwd_out(dxo, f1, gate, w_out, gu, a)
        dx, dng, dsh, dsc, dw_in = bwd_in(x, ng, sh, sc, w_in, h, dgu, dxo)
        return dx, dng, dsh, dsc, dgate, dw_in, dw_out

    ffn_block.defvjp(fwd, bwd)
    return ffn_block


def _make_norm_proj(tag):
    @jax.custom_vjp
    def norm_proj(x, ng, sh, sc, w):
        return fwd(x, ng, sh, sc, w)[0]

    def fwd(x, ng, sh, sc, w):
        h = _norm_mod_fwd(x, ng, sc, sh, tag + "_norm")
        p = _matmul(h, w, "nn", F32, tag + "_mm", tm=1024, tn=1024, tk=w.shape[0])
        return p, (x, ng, sh, sc, w, h)

    def bwd(res, dp):
        x, ng, sh, sc, w, h = res
        dh = _matmul(dp, w, "nt", F32, tag + "_dh", tm=512, tn=w.shape[0], tk=2048)
        dw = _matmul(h, dp, "tn", BF16, tag + "_dw", tm=w.shape[0], tn=1024, tk=1024)
        dx, dng, dsc, dsh = _norm_mod_bwd(x, ng, sc, sh, dh, None, tag + "_dnorm")
        return dx, dng, dsh, dsc, dw

    norm_proj.defvjp(fwd, bwd)
    return norm_proj


def _make_norm_proj_carry(tag):
    @jax.custom_vjp
    def norm_proj(x, ng, sh, sc, w):
        return fwd(x, ng, sh, sc, w)[0]

    def fwd(x, ng, sh, sc, w):
        h = _norm_mod_fwd(x, ng, sc, sh, tag + "_norm")
        p = _matmul(h, w, "nn", F32, tag + "_mm", tm=1024, tn=1024, tk=w.shape[0])
        return (p, x), (x, ng, sh, sc, w, h)

    def bwd(res, cts):
        x, ng, sh, sc, w, h = res
        dp, dx_carry = cts
        dh = _matmul(dp, w, "nt", F32, tag + "_dh", tm=512, tn=w.shape[0], tk=2048)
        dw = _matmul(h, dp, "tn", BF16, tag + "_dw", tm=w.shape[0], tn=1024, tk=1024)
        dx, dng, dsc, dsh = _norm_mod_bwd(x, ng, sc, sh, dh, dx_carry, tag + "_dnorm")
        return dx, dng, dsh, dsc, dw

    norm_proj.defvjp(fwd, bwd)
    return norm_proj


def _make_split(tag, widths, total):
    offs = [sum(widths[:i]) for i in range(len(widths))]

    def concat_call(pieces):
        t = pieces[0].shape[0]
        tm = _pick(t, 256, 16)

        def body(*refs):
            o_ref = refs[-1]
            for ref, off, wd in zip(refs[:-1], offs, widths):
                o_ref[:, off:off + wd] = ref[...]
            end = offs[-1] + widths[-1]
            if end < total:
                o_ref[:, end:] = jnp.zeros((tm, total - end), F32)

        return pl.pallas_call(
            body, name=tag + "_concat", grid=(t // tm,),
            in_specs=[_row_spec(tm, wd) for wd in widths], out_specs=_row_spec(tm, total),
            out_shape=jax.ShapeDtypeStruct((t, total), F32), compiler_params=_params(),
        )(*pieces)

    @jax.custom_vjp
    def split(p):
        return tuple(p[:, off:off + wd] for off, wd in zip(offs, widths))

    def fwd(p):
        return split(p), None

    def bwd(_, cts):
        return (concat_call(list(cts)),)

    split.defvjp(fwd, bwd)
    return split


def _make_res_proj(tag):
    @jax.custom_vjp
    def res_proj(a, w, x, gate):
        return fwd(a, w, x, gate)[0]

    def fwd(a, w, x, gate):
        xo, f = _res_mm_fwd(a, w, x, gate, 1.0, tag + "_mm")
        return xo, (a, w, gate, f)

    def bwd(res, dxo):
        a, w, gate, f = res
        df, dgate = _gate_bwd(dxo, f, gate, 1.0, tag + "_dgate")
        da = _matmul(df, w, "nt", BF16, tag + "_da", tm=1024, tn=1024, tk=2048)
        dw = _matmul(a, df, "tn", BF16, tag + "_dw", tm=1024, tn=2048, tk=1024)
        return da, dw, dxo, dgate

    res_proj.defvjp(fwd, bwd)
    return res_proj


def _make_small_mm(tag):
    @jax.custom_vjp
    def small_mm(a, w):
        return _matmul(a, w, "nn", F32, tag + "_mm", tm=a.shape[0], tn=768, tk=w.shape[0])

    def fwd(a, w):
        return small_mm(a, w), (a, w)

    def bwd(res, dr):
        a, w = res
        da = _matmul(dr, w, "nt", F32, tag + "_da", tm=a.shape[0], tn=w.shape[0], tk=768)
        dw = _matmul(a, dr, "tn", F32, tag + "_dw", tm=1024, tn=768, tk=a.shape[0])
        return da, dw

    small_mm.defvjp(fwd, bwd)
    return small_mm


def _ret_chunk_terms(lg, c, reverse):
    row = lax.broadcasted_iota(jnp.int32, (c, c), 0).astype(F32)
    col = lax.broadcasted_iota(jnp.int32, (c, c), 1).astype(F32)
    pos = lax.broadcasted_iota(jnp.int32, (c, 1), 0).astype(F32)
    if reverse:
        diff = col - row
        mask = diff > 0.0
        e_exp = float(c) - pos
        f_exp = pos
    else:
        diff = row - col
        mask = diff >= 0.0
        e_exp = pos + 1.0
        f_exp = float(c - 1) - pos
    diffm = jnp.where(mask, diff, 0.0)
    dm = jnp.where(mask, jnp.exp(lg * diffm), 0.0)
    return diffm, dm, e_exp, jnp.exp(lg * e_exp), f_exp, jnp.exp(lg * f_exp)


def _ret_mask_t(lg, c, reverse):
    row = lax.broadcasted_iota(jnp.int32, (c, c), 0).astype(F32)
    col = lax.broadcasted_iota(jnp.int32, (c, c), 1).astype(F32)
    diff = row - col if reverse else col - row
    mask = diff > 0.0 if reverse else diff >= 0.0
    return jnp.where(mask, jnp.exp(lg * jnp.where(mask, diff, 0.0)), 0.0)


def _lane0(val):
    lane = lax.broadcasted_iota(jnp.int32, (1, LANE), 1)
    return jnp.where(lane == 0, val, 0.0)


RET_HEAD_BLOCK = 4


def _make_ret_dir(tag, reverse):
    hb = RET_HEAD_BLOCK

    def heads_spec(nc, width, flip):
        if flip:
            return pl.BlockSpec((hb, RET_CHUNK, width), lambda h, t: (h, nc - 1 - t, 0))
        return pl.BlockSpec((hb, RET_CHUNK, width), lambda h, t: (h, t, 0))

    def state_spec(nc, flip):
        if flip:
            return pl.BlockSpec((hb, None, RET_DKP, RET_DV), lambda h, t: (h, nc - 1 - t, 0, 0))
        return pl.BlockSpec((hb, None, RET_DKP, RET_DV), lambda h, t: (h, t, 0, 0))

    lg_spec = pl.BlockSpec((hb, 1, LANE), lambda h, t: (h, 0, 0))
    s0_spec = pl.BlockSpec((hb, RET_DKP, RET_DV), lambda h, t: (h, 0, 0))

    def fwd_call(q, k, v, lgb, s0):
        hh, ll, _ = q.shape
        c = RET_CHUNK
        nc = ll // c

        def body(q_ref, k_ref, v_ref, lg_ref, s0_ref, y_ref, sall_ref, s_scr):
            @pl.when(pl.program_id(1) == 0)
            def _():
                s_scr[...] = s0_ref[...]

            for b in range(hb):
                lg = lg_ref[b][:, :1]
                _, dm, _, xi, _, zeta = _ret_chunk_terms(lg, c, reverse)
                q_t, k_t, v_t = q_ref[b], k_ref[b], v_ref[b]
                s = s_scr[b]
                p = _dot(q_t, k_t, NT) * dm
                y_ref[b] = _dot(p, v_t, NN) + _dot(q_t * xi, s, NN)
                sall_ref[b] = s
                s_scr[b] = jnp.exp(lg * float(c)) * s + _dot(k_t * zeta, v_t, TN)

        return pl.pallas_call(
            body, name=tag + "_fwd", grid=(hh // hb, nc),
            in_specs=[heads_spec(nc, RET_DKP,reverse), heads_spec(nc, RET_DKP,reverse),
                      heads_spec(nc, RET_DV, reverse), lg_spec, s0_spec],
            out_specs=[heads_spec(nc, RET_DV, reverse), state_spec(nc, reverse)],
            out_shape=[jax.ShapeDtypeStruct((hh, ll, RET_DV), F32),
                       jax.ShapeDtypeStruct((hh, nc, RET_DKP, RET_DV), F32)],
            scratch_shapes=[pltpu.VMEM((hb, RET_DKP, RET_DV), F32)],
            compiler_params=_params(),
        )(q, k, v, lgb, s0)

    def bwd_call(q, k, v, lgb, sall, dy):
        hh, ll, _ = q.shape
        c = RET_CHUNK
        nc = ll // c
        flip = not reverse

        def body(q_ref, k_ref, v_ref, lg_ref, sall_ref, dy_ref, dq_ref, dk_ref, dv_ref, dlg_ref, ds0_ref, ds_scr):
            @pl.when(pl.program_id(1) == 0)
            def _():
                ds_scr[...] = jnp.zeros_like(ds_scr)
                dlg_ref[...] = jnp.zeros_like(dlg_ref)

            def total(m):
                return jnp.sum(jnp.sum(m, axis=1, keepdims=True), axis=0, keepdims=True)

            for b in range(hb):
                lg = lg_ref[b][:, :1]
                diffm, dm, e_exp, xi, f_exp, zeta = _ret_chunk_terms(lg, c, reverse)
                q_t, k_t, v_t, dy_t = q_ref[b], k_ref[b], v_ref[b], dy_ref[b]
                s = sall_ref[b]
                dsn = ds_scr[b]
                a = _dot(q_t, k_t, NT)
                da = _dot(dy_t, v_t, NT) * dm
                dm_t = _ret_mask_t(lg, c, reverse)
                a_t = _dot(k_t, q_t, NT)
                da_t = _dot(v_t, dy_t, NT) * dm_t
                g = _dot(dy_t, s, NT)
                hm = _dot(v_t, dsn, NT)
                dq_ref[b] = _dot(da, k_t, NN) + xi * g
                dk_ref[b] = _dot(da_t, q_t, NN) + zeta * hm
                dv_ref[b] = _dot(a_t * dm_t, dy_t, NN) + _dot(k_t * zeta, dsn, NN)
                gc = jnp.exp(lg * float(c))
                ds_scr[b] = gc * dsn + _dot(q_t * xi, dy_t, TN)
                dl = (total(da * a * diffm) + total(e_exp * xi * q_t * g)
                      + float(c) * gc * total(s * dsn) + total(f_exp * zeta * k_t * hm))
                dlg_ref[b] += _lane0(dl)

            @pl.when(pl.program_id(1) == nc - 1)
            def _():
                ds0_ref[...] = ds_scr[...]

        return pl.pallas_call(
            body, name=tag + "_bwd", grid=(hh // hb, nc),
            in_specs=[heads_spec(nc, RET_DKP,flip), heads_spec(nc, RET_DKP,flip), heads_spec(nc, RET_DV, flip),
                      lg_spec, state_spec(nc, flip), heads_spec(nc, RET_DV, flip)],
            out_specs=[heads_spec(nc, RET_DKP,flip), heads_spec(nc, RET_DKP,flip), heads_spec(nc, RET_DV, flip),
                       lg_spec, s0_spec],
            out_shape=[jax.ShapeDtypeStruct((hh, ll, RET_DKP), F32), jax.ShapeDtypeStruct((hh, ll, RET_DKP), F32),
                       jax.ShapeDtypeStruct((hh, ll, RET_DV), F32), jax.ShapeDtypeStruct((hh, 1, LANE), F32),
                       jax.ShapeDtypeStruct((hh, RET_DKP, RET_DV), F32)],
            scratch_shapes=[pltpu.VMEM((hb, RET_DKP, RET_DV), F32)],
            compiler_params=_params(),
        )(q, k, v, lgb, sall, dy)

    @jax.custom_vjp
    def ret_dir(q, k, v, lgb, s0):
        return fwd_call(q, k, v, lgb, s0)[0]

    def fwd(q, k, v, lgb, s0):
        y, sall = fwd_call(q, k, v, lgb, s0)
        return y, (q, k, v, lgb, sall)

    def bwd(res, dy):
        q, k, v, lgb, sall = res
        return tuple(bwd_call(q, k, v, lgb, sall, dy))

    ret_dir.defvjp(fwd, bwd)
    return ret_dir


def _make_ctx_state(tag, reverse):
    hb = RET_HEAD_BLOCK
    c = RET_CHUNK
    k_spec = pl.BlockSpec((hb, c, RET_DKP), lambda h: (h, 0, 0))
    v_spec = pl.BlockSpec((hb, c, RET_DV), lambda h: (h, 0, 0))
    lg_spec = pl.BlockSpec((hb, 1, LANE), lambda h: (h, 0, 0))
    s_spec = pl.BlockSpec((hb, RET_DKP, RET_DV), lambda h: (h, 0, 0))

    def fwd_call(k, v, lgb):
        hh = k.shape[0]

        def body(k_ref, v_ref, lg_ref, s_ref):
            for b in range(hb):
                _, _, _, _, _, zeta = _ret_chunk_terms(lg_ref[b][:, :1], c, reverse)
                s_ref[b] = _dot(k_ref[b] * zeta, v_ref[b], TN)

        return pl.pallas_call(
            body, name=tag + "_fwd", grid=(hh // hb,), in_specs=[k_spec, v_spec, lg_spec], out_specs=s_spec,
            out_shape=jax.ShapeDtypeStruct((hh, RET_DKP, RET_DV), F32), compiler_params=_params(),
        )(k, v, lgb)

    def bwd_call(k, v, lgb, ds):
        hh = k.shape[0]

        def body(k_ref, v_ref, lg_ref, ds_ref, dk_ref, dv_ref, dlg_ref):
            for b in range(hb):
                _, _, _, _, f_exp, zeta = _ret_chunk_terms(lg_ref[b][:, :1], c, reverse)
                k_t, v_t, ds = k_ref[b], v_ref[b], ds_ref[b]
                hm = _dot(v_t, ds, NT)
                dk_ref[b] = zeta * hm
                dv_ref[b] = _dot(k_t * zeta, ds, NN)
                tot = jnp.sum(jnp.sum(f_exp * zeta * k_t * hm, axis=1, keepdims=True), axis=0, keepdims=True)
                dlg_ref[b] = _lane0(tot)

        return pl.pallas_call(
            body, name=tag + "_bwd", grid=(hh // hb,), in_specs=[k_spec, v_spec, lg_spec, s_spec],
            out_specs=[k_spec, v_spec, lg_spec],
            out_shape=[jax.ShapeDtypeStruct(k.shape, F32), jax.ShapeDtypeStruct(v.shape, F32),
                       jax.ShapeDtypeStruct((hh, 1, LANE), F32)],
            compiler_params=_params(),
        )(k, v, lgb, ds)

    @jax.custom_vjp
    def ctx_state(k, v, lgb):
        return fwd_call(k, v, lgb)

    def fwd(k, v, lgb):
        return fwd_call(k, v, lgb), (k, v, lgb)

    def bwd(res, ds):
        return tuple(bwd_call(*res, ds))

    ctx_state.defvjp(fwd, bwd)
    return ctx_state


def _rope_tables_call(name, n, inv, shift, axial):
    tm = _pick(n, 1024, 8)
    inv_lane = jnp.tile(inv, LANE // inv.shape[0])[None, :]

    def body(inv_ref, cos_ref, s1_ref, s2_ref):
        t = lax.broadcasted_iota(jnp.int32, (tm, LANE), 0) + pl.program_id(0) * tm
        lane = lax.broadcasted_iota(jnp.int32, (tm, LANE), 1)
        if axial:
            pos = jnp.where(lane % (2 * MLA_ROPE // 2) < MLA_ROPE // 2, t // GRID_W, t % GRID_W)
        else:
            pos = t
        ang = pos.astype(F32) * inv_ref[...]
        sin = jnp.sin(ang)
        first = lane % (2 * shift) < shift
        cos_ref[...] = jnp.cos(ang)
        s1_ref[...] = jnp.where(first, -sin, 0.0)
        s2_ref[...] = jnp.where(first, 0.0, sin)

    tab = jax.ShapeDtypeStruct((n, LANE), F32)
    return tuple(pl.pallas_call(
        body, name=name, grid=(n // tm,), in_specs=[_vec_spec(LANE)], out_specs=[_row_spec(tm, LANE)] * 3,
        out_shape=[tab, tab, tab], compiler_params=_params(),
    )(inv_lane))


def _ret_tables(n_lat):
    inv = RET_ROPE_BASE ** (-jnp.arange(0, RET_DK, 2, dtype=F32) / RET_DK)
    return _rope_tables_call("ret_tables", n_lat, inv, RET_DK // 2, False)


def _make_ret_pack(tag, n_lat, n_ctx):
    hh = RET_HEADS
    tm = MLA_PACK_ROWS
    k_scale = RET_DK ** -0.5
    shift = RET_DK // 2
    tabs = _ret_tables(n_lat)

    def low_lanes():
        return lax.broadcasted_iota(jnp.int32, (1, LANE), 1) < RET_DK

    def rows(width):
        return pl.BlockSpec((tm, width), lambda i: (i, 0))

    def heads(width):
        return pl.BlockSpec((hh, tm, width), lambda i: (0, i, 0))

    def split_pairs(src_ref, dst_ref, scale, rope):
        keep = low_lanes()
        for j in range(hh // 2):
            blk = src_ref[:, LANE * j:LANE * (j + 1)]
            if scale != 1.0:
                blk = blk * scale
            if rope is not None:
                blk = _rope128(blk, *rope, shift=shift)
            dst_ref[2 * j] = jnp.where(keep, blk, 0.0)
            dst_ref[2 * j + 1] = jnp.where(keep, pltpu.roll(blk, RET_DK, 1), 0.0)

    def merge_pairs(src_refs, dst_ref, scale, rope):
        keep = low_lanes()
        for j in range(hh // 2):
            even = sum(r[2 * j] for r in src_refs)
            odd = sum(r[2 * j + 1] for r in src_refs)
            g = jnp.where(keep, even, pltpu.roll(odd, RET_DK, 1))
            if rope is not None:
                g = _rope128_t(g, *rope, shift=shift)
            dst_ref[:, LANE * j:LANE * (j + 1)] = g * scale if scale != 1.0 else g

    def pack_call(name, n, q, k, v, rope):
        with_q = q is not None

        def body(*refs):
            refs = list(refs)
            q_ref = refs.pop(0) if with_q else None
            k_ref, v_ref = refs.pop(0), refs.pop(0)
            tab = tuple(r[...] for r in refs[:3]) if rope else None
            outs = refs[3:] if rope else refs
            if with_q:
                split_pairs(q_ref, outs[0], 1.0, tab)
                outs = outs[1:]
            split_pairs(k_ref, outs[0], k_scale, tab)
            for h in range(hh):
                outs[1][h] = v_ref[:, RET_DV * h:RET_DV * (h + 1)]

        ins = ([q] if with_q else []) + [k, v] + (list(tabs) if rope else [])
        in_specs = ([rows(q.shape[1])] if with_q else []) + [rows(k.shape[1]), rows(v.shape[1])]
        in_specs += [rows(LANE)] * 3 if rope else []
        n_out = 3 if with_q else 2
        return pl.pallas_call(
            body, name=name, grid=(n // tm,), in_specs=in_specs,
            out_specs=[heads(RET_DKP)] * (n_out - 1) + [heads(RET_DV)],
            out_shape=[jax.ShapeDtypeStruct((hh, n, RET_DKP), F32)] * (n_out - 1)
            + [jax.ShapeDtypeStruct((hh, n, RET_DV), F32)],
            compiler_params=_params(),
        )(*ins)

    def unpack_call(name, n, dqs, dks, dvs, rope):
        with_q = len(dqs) > 0
        uses = len(dks)

        def body(*refs):
            refs = list(refs)
            dq_refs = [refs.pop(0) for _ in range(len(dqs))]
            dk_refs = [refs.pop(0) for _ in range(uses)]
            dv_refs = [refs.pop(0) for _ in range(uses)]
            tab = tuple(r[...] for r in refs[:3]) if rope else None
            outs = refs[3:] if rope else refs
            if with_q:
                merge_pairs(dq_refs, outs[0], 1.0, tab)
                outs = outs[1:]
            merge_pairs(dk_refs, outs[0], k_scale, tab)
            for h in range(hh):
                outs[1][:, RET_DV * h:RET_DV * (h + 1)] = sum(r[h] for r in dv_refs)

        ins = list(dqs) + list(dks) + list(dvs) + (list(tabs) if rope else [])
        in_specs = [heads(RET_DKP)] * (len(dqs) + uses) + [heads(RET_DV)] * uses + ([rows(LANE)] * 3 if rope else [])
        n_out = 3 if with_q else 2
        return pl.pallas_call(
            body, name=name, grid=(n // tm,), in_specs=in_specs,
            out_specs=[rows(hh * RET_DK)] * (n_out - 1) + [rows(hh * RET_DV)],
            out_shape=[jax.ShapeDtypeStruct((n, hh * RET_DK), F32)] * (n_out - 1)
            + [jax.ShapeDtypeStruct((n, hh * RET_DV), F32)],
            compiler_params=_params(),
        )(*ins)

    @jax.custom_vjp
    def ret_pack(rq, rk, rv, crk, crv):
        q, k, v = pack_call(tag + "_lat", n_lat, rq, rk, rv, True)
        k_c, v_c = pack_call(tag + "_ctx", n_ctx, None, crk, crv, False)
        return (q, k, v), (q, k, v), (k_c, v_c), (k_c, v_c)

    def fwd(rq, rk, rv, crk, crv):
        return ret_pack(rq, rk, rv, crk, crv), None

    def bwd(_, cts):
        lat_f, lat_b, ctx_f, ctx_b = cts
        drq, drk, drv = unpack_call(tag + "_dlat", n_lat, [lat_f[0], lat_b[0]], [lat_f[1], lat_b[1]],
                                    [lat_f[2], lat_b[2]], True)
        dcrk, dcrv = unpack_call(tag + "_dctx", n_ctx, [], [ctx_f[0], ctx_b[0]], [ctx_f[1], ctx_b[1]], False)
        return drq, drk, drv, dcrk, dcrv

    ret_pack.defvjp(fwd, bwd)
    return ret_pack


def _ret_out_tile(y, g):
    mu = jnp.mean(y, axis=-1, keepdims=True)
    var = jnp.mean(jnp.square(y - mu), axis=-1, keepdims=True)
    return (g * jax.nn.sigmoid(g)) * ((y - mu) * lax.rsqrt(var + GN_EPS))


def _make_ret_out(tag):
    def specs(tm):
        y_spec = pl.BlockSpec((None, tm, RET_DV), lambda h, i: (h, i, 0))
        g_spec = pl.BlockSpec((tm, RET_DV), lambda h, i: (i, h))
        return y_spec, g_spec

    def fwd_call(yf, yb, g):
        hh, n, _ = yf.shape
        tm = _pick(n, 1024, 16)
        y_spec, g_spec = specs(tm)

        def body(yf_ref, yb_ref, g_ref, o_ref):
            o_ref[...] = _ret_out_tile(yf_ref[...] + yb_ref[...], g_ref[...]).astype(BF16)

        return pl.pallas_call(
            body, name=tag + "_fwd", grid=(hh, n // tm), in_specs=[y_spec, y_spec, g_spec], out_specs=g_spec,
            out_shape=jax.ShapeDtypeStruct((n, hh * RET_DV), BF16), compiler_params=_params(),
        )(yf, yb, g)

    def bwd_call(yf, yb, g, do):
        hh, n, _ = yf.shape
        tm = _pick(n, 1024, 16)
        y_spec, g_spec = specs(tm)

        def body(yf_ref, yb_ref, g_ref, do_ref, dy_ref, dg_ref):
            _, vjp = jax.vjp(_ret_out_tile, yf_ref[...] + yb_ref[...], g_ref[...])
            dy, dg = vjp(do_ref[...].astype(F32))
            dy_ref[...] = dy
            dg_ref[...] = dg

        return pl.pallas_call(
            body, name=tag + "_bwd", grid=(hh, n // tm), in_specs=[y_spec, y_spec, g_spec, g_spec],
            out_specs=[y_spec, g_spec],
            out_shape=[jax.ShapeDtypeStruct(yf.shape, F32), jax.ShapeDtypeStruct(g.shape, F32)],
            compiler_params=_params(),
        )(yf, yb, g, do)

    @jax.custom_vjp
    def ret_out(yf, yb, g):
        return fwd_call(yf, yb, g)

    def fwd(yf, yb, g):
        return fwd_call(yf, yb, g), (yf, yb, g)

    def bwd(res, do):
        dy, dg = bwd_call(*res, do)
        return dy, dy, dg

    ret_out.defvjp(fwd, bwd)
    return ret_out


MLA_DQ_PAD = 2 * LANE
MLA_PACK_ROWS = 256


def _rope128(x, cos, s1, s2, shift=16):
    return x * cos + pltpu.roll(x, LANE - shift, 1) * s1 + pltpu.roll(x, shift, 1) * s2


def _rope128_t(g, cos, s1, s2, shift=16):
    return g * cos + pltpu.roll(g * s1, shift, 1) + pltpu.roll(g * s2, LANE - shift, 1)


def _axial_tables(n_lat):
    half = MLA_ROPE // 2
    inv = AXIAL_BASE ** (-jnp.arange(0, half, 2, dtype=F32) / half)
    return _rope_tables_call("mla_tables", n_lat, inv, half // 2, True)


def _make_mla_pack(tag, n_lat, n_ctx, scale):
    hh = MLA_HEADS
    tm = MLA_PACK_ROWS
    ll = n_lat + n_ctx
    rope0 = hh * MLA_NOPE
    tabs = _axial_tables(n_lat)

    def rope_lanes():
        return lax.broadcasted_iota(jnp.int32, (1, LANE), 1) < MLA_ROPE

    def rows(width):
        return pl.BlockSpec((tm, width), lambda i: (i, 0))

    def heads(width, off):
        return pl.BlockSpec((hh, tm, width), lambda i: (0, i + off, 0))

    def heads_t(width, off):
        return pl.BlockSpec((hh, width, tm), lambda i: (0, 0, i + off))

    def put_kv(kv_ref, kr_rot, k_ref, v_ref, kt_ref, vt_ref):
        kr_b = kr_rot.astype(BF16)
        kr_t = jnp.transpose(kr_rot).astype(BF16)
        for h in range(hh):
            k_nope = kv_ref[:, 2 * LANE * h:2 * LANE * h + MLA_NOPE]
            val = kv_ref[:, 2 * LANE * h + MLA_NOPE:2 * LANE * (h + 1)]
            k_ref[h, :, :MLA_NOPE] = k_nope.astype(BF16)
            k_ref[h, :, MLA_NOPE:] = kr_b
            v_ref[h] = val.astype(BF16)
            kt_ref[h, :MLA_NOPE, :] = jnp.transpose(k_nope).astype(BF16)
            kt_ref[h, MLA_NOPE:, :] = kr_t
            vt_ref[h] = jnp.transpose(val).astype(BF16)

    def fwd_lat(qp, kv, kr):
        def body(qp_ref, kv_ref, kr_ref, cos_ref, s1_ref, s2_ref, q_ref, k_ref, v_ref, kt_ref, vt_ref):
            cos, s1, s2 = cos_ref[...], s1_ref[...], s2_ref[...]
            keep = rope_lanes()
            for j in range(hh // 2):
                rot = _rope128(qp_ref[:, rope0 + LANE * j:rope0 + LANE * (j + 1)], cos, s1, s2)
                q_ref[2 * j, :, MLA_NOPE:] = jnp.where(keep, rot, 0.0).astype(BF16)
                q_ref[2 * j + 1, :, MLA_NOPE:] = jnp.where(keep, pltpu.roll(rot, MLA_ROPE, 1), 0.0).astype(BF16)
            for h in range(hh):
                q_ref[h, :, :MLA_NOPE] = qp_ref[:, MLA_NOPE * h:MLA_NOPE * (h + 1)].astype(BF16)
            kr_rot = jnp.where(keep, _rope128(kr_ref[...], cos, s1, s2), 0.0)
            put_kv(kv_ref, kr_rot, k_ref, v_ref, kt_ref, vt_ref)

        return pl.pallas_call(
            body, name=tag + "_lat", grid=(n_lat // tm,),
            in_specs=[rows(qp.shape[1]), rows(kv.shape[1]), rows(LANE), rows(LANE), rows(LANE), rows(LANE)],
            out_specs=[heads(MLA_DQ_PAD, 0), heads(MLA_DQ_PAD, 0), heads(MLA_V, 0), heads_t(MLA_DQ_PAD, 0),
                       heads_t(MLA_V, 0)],
            out_shape=[jax.ShapeDtypeStruct((hh, n_lat, MLA_DQ_PAD), BF16),
                       jax.ShapeDtypeStruct((hh, ll, MLA_DQ_PAD), BF16), jax.ShapeDtypeStruct((hh, ll, MLA_V), BF16),
                       jax.ShapeDtypeStruct((hh, MLA_DQ_PAD, ll), BF16), jax.ShapeDtypeStruct((hh, MLA_V, ll), BF16)],
            compiler_params=_params(),
        )(qp, kv, kr, *tabs)

    def fwd_ctx(kv_c, kr_c, bufs):
        def body(kv_ref, kr_ref, k_in, v_in, kt_in, vt_in, k_ref, v_ref, kt_ref, vt_ref):
            kr_rot = jnp.where(rope_lanes(), kr_ref[...], 0.0)
            put_kv(kv_ref, kr_rot, k_ref, v_ref, kt_ref, vt_ref)

        any_spec = pl.BlockSpec(memory_space=pl.ANY)
        off = n_lat // tm
        return pl.pallas_call(
            body, name=tag + "_ctx", grid=(n_ctx // tm,),
            in_specs=[rows(kv_c.shape[1]), rows(LANE)] + [any_spec] * 4,
            out_specs=[heads(MLA_DQ_PAD, off), heads(MLA_V, off), heads_t(MLA_DQ_PAD, off), heads_t(MLA_V, off)],
            out_shape=[jax.ShapeDtypeStruct(b.shape, BF16) for b in bufs],
            input_output_aliases={2: 0, 3: 1, 4: 2, 5: 3}, compiler_params=_params(),
        )(kv_c, kr_c, *bufs)

    def take_kv(dk_ref, dv_ref, dkv_ref):
        dkr = jnp.zeros((tm, LANE), F32)
        for h in range(hh):
            dkv_ref[:, 2 * LANE * h:2 * LANE * h + MLA_NOPE] = dk_ref[h, :, :MLA_NOPE].astype(F32)
            dkv_ref[:, 2 * LANE * h + MLA_NOPE:2 * LANE * (h + 1)] = dv_ref[h].astype(F32)
            dkr = dkr + dk_ref[h, :, MLA_NOPE:].astype(F32)
        return jnp.where(rope_lanes(), dkr, 0.0)

    def bwd_lat(dqt, dk, dv, qp_width, kv_width):
        def body(dqt_ref, dk_ref, dv_ref, cos_ref, s1_ref, s2_ref, dqp_ref, dkv_ref, dkr_ref):
            cos, s1, s2 = cos_ref[...], s1_ref[...], s2_ref[...]
            keep = rope_lanes()
            for j in range(hh // 2):
                even = jnp.transpose(dqt_ref[2 * j]) * scale
                odd = jnp.transpose(dqt_ref[2 * j + 1]) * scale
                dqp_ref[:, MLA_NOPE * 2 * j:MLA_NOPE * (2 * j + 1)] = even[:, :MLA_NOPE]
                dqp_ref[:, MLA_NOPE * (2 * j + 1):MLA_NOPE * (2 * j + 2)] = odd[:, :MLA_NOPE]
                g = jnp.where(keep, even[:, MLA_NOPE:], pltpu.roll(odd[:, MLA_NOPE:], MLA_ROPE, 1))
                dqp_ref[:, rope0 + LANE * j:rope0 + LANE * (j + 1)] = _rope128_t(g, cos, s1, s2)
            dkr_ref[...] = jnp.where(keep, _rope128_t(take_kv(dk_ref, dv_ref, dkv_ref), cos, s1, s2), 0.0)

        return pl.pallas_call(
            body, name=tag + "_dlat", grid=(n_lat // tm,),
            in_specs=[pl.BlockSpec((hh, MLA_DQ_PAD, tm), lambda i: (0, 0, i)),
                      heads(MLA_DQ_PAD, 0), heads(MLA_V, 0), rows(LANE), rows(LANE), rows(LANE)],
            out_specs=[rows(qp_width), rows(kv_width), rows(LANE)],
            out_shape=[jax.ShapeDtypeStruct((n_lat, qp_width), F32), jax.ShapeDtypeStruct((n_lat, kv_width), F32),
                       jax.ShapeDtypeStruct((n_lat, LANE), F32)],
            compiler_params=_params(),
        )(dqt, dk, dv, *tabs)

    def bwd_ctx(dk, dv, kv_width):
        def body(dk_ref, dv_ref, dkv_ref, dkr_ref):
            dkr_ref[...] = take_kv(dk_ref, dv_ref, dkv_ref)

        off = n_lat // tm
        return pl.pallas_call(
            body, name=tag + "_dctx", grid=(n_ctx // tm,),
            in_specs=[heads(MLA_DQ_PAD, off), heads(MLA_V, off)],
            out_specs=[rows(kv_width), rows(LANE)],
            out_shape=[jax.ShapeDtypeStruct((n_ctx, kv_width), F32), jax.ShapeDtypeStruct((n_ctx, LANE), F32)],
            compiler_params=_params(),
        )(dk, dv)

    def pack(qp, kv, kr, kv_c, kr_c):
        q, *bufs = fwd_lat(qp, kv, kr)
        return (q, *fwd_ctx(kv_c, kr_c, bufs))

    def unpack(dqt, dk, dv):
        qp_width, kv_width = hh * (MLA_NOPE + MLA_ROPE), hh * (MLA_NOPE + MLA_V)
        dqp, dkv, dkr = bwd_lat(dqt, dk, dv, qp_width, kv_width)
        dkv_c, dkr_c = bwd_ctx(dk, dv, kv_width)
        return dqp, dkv, dkr, dkv_c, dkr_c

    return pack, unpack


def _make_mla(tag, n_lat, n_ctx):
    scale = (MLA_NOPE + MLA_ROPE) ** -0.5
    pack, unpack = _make_mla_pack(tag + "pack", n_lat, n_ctx, scale)
    attn_fwd, attn_delta, attn_bwd = _make_attention(tag, scale, MLA_NOPE + MLA_ROPE)

    @jax.custom_vjp
    def mla(qp, kv, kr, kv_c, kr_c):
        q, k, _, _, vt = pack(qp, kv, kr, kv_c, kr_c)
        return attn_fwd(q, k, vt)[0]

    def fwd(qp, kv, kr, kv_c, kr_c):
        q, k, v, kt, vt = pack(qp, kv, kr, kv_c, kr_c)
        o, lse = attn_fwd(q, k, vt)
        return o, (q, k, kt, v, o, lse)

    def bwd(res, do):
        q, k, kt, v, o, lse = res
        delta = attn_delta(o, do, q.shape[0])
        dqt, dk, dv = attn_bwd(q, k, kt, v, do, lse, delta)
        return unpack(dqt, dk, dv)

    mla.defvjp(fwd, bwd)
    return mla


def _make_attention(tag, scale, dq_live=None):
    neg_big = -1e30
    log2e = 1.4426950408889634
    sub = 256

    def fwd_call(q, k, vt):
        hh, n, dq = q.shape
        dv, ll = vt.shape[1], vt.shape[2]
        tq, tk = _pick(n, 2048), _pick(ll, 1408)
        sb = sub if tk % sub == 0 else tk
        c2 = scale * log2e
        k_steps = ll // tk

        def body(q_ref, k_ref, vt_ref, o_ref, lse_ref, m_scr, l_scr, acc_scr, s_scr, p_scr):
            j = pl.program_id(2)

            @pl.when(j == 0)
            def _():
                m_scr[...] = jnp.full_like(m_scr, neg_big)
                l_scr[...] = jnp.zeros_like(l_scr)
                acc_scr[...] = jnp.zeros_like(acc_scr)

            q_t = q_ref[...]
            m_prev = m_scr[...]
            m_new = m_prev
            for kk in range(tk // sb):
                rows = slice(kk * sb, (kk + 1) * sb)
                s_t = _dot(k_ref[rows, :], q_t, NT)
                s_scr[rows, :] = s_t
                m_new = jnp.maximum(m_new, jnp.max(s_t, axis=0, keepdims=True))
            mc = m_new * c2
            l_part = jnp.zeros_like(m_new)
            for kk in range(tk // sb):
                rows = slice(kk * sb, (kk + 1) * sb)
                p_t = jnp.exp2(s_scr[rows, :] * c2 - mc)
                l_part = l_part + jnp.sum(p_t, axis=0, keepdims=True)
                p_scr[rows, :] = p_t.astype(BF16)
            alpha = jnp.exp2((m_prev - m_new) * c2)
            l_scr[...] = alpha * l_scr[...] + l_part
            acc_scr[...] = alpha * acc_scr[...] + _dot(vt_ref[...], p_scr[...], NN)
            m_scr[...] = m_new

            @pl.when(j == k_steps - 1)
            def _():
                o_ref[...] = jnp.transpose(acc_scr[...] / l_scr[...]).astype(BF16)
                lse_ref[...] = m_scr[...] * scale + jnp.log(l_scr[...])

        return pl.pallas_call(
            body, name=tag + "_fwd", grid=(hh, n // tq, k_steps),
            in_specs=[pl.BlockSpec((None, tq, dq), lambda h, i, j: (h, i, 0)),
                      pl.BlockSpec((None, tk, dq), lambda h, i, j: (h, j, 0)),
                      pl.BlockSpec((None, dv, tk), lambda h, i, j: (h, 0, j))],
            out_specs=[pl.BlockSpec((tq, dv), lambda h, i, j: (i, h)),
                       pl.BlockSpec((None, 1, tq), lambda h, i, j: (h, 0, i))],
            out_shape=[jax.ShapeDtypeStruct((n, hh * dv), BF16), jax.ShapeDtypeStruct((hh, 1, n), F32)],
            scratch_shapes=[pltpu.VMEM((1, tq), F32), pltpu.VMEM((1, tq), F32), pltpu.VMEM((dv, tq), F32),
                            pltpu.VMEM((tk, tq), F32), pltpu.VMEM((tk, tq), BF16)],
            compiler_params=_params(),
        )(q, k, vt)

    def delta_call(o, do, hh):
        n = o.shape[0]
        dv = o.shape[1] // hh
        tq = _pick(n, 1024)

        def body(o_ref, do_ref, d_ref):
            prod_t = jnp.transpose(o_ref[...].astype(F32) * do_ref[...].astype(F32))
            d_ref[...] = jnp.sum(prod_t, axis=0, keepdims=True)

        spec = pl.BlockSpec((tq, dv), lambda h, i: (i, h))
        return pl.pallas_call(
            body, name=tag + "_delta", grid=(hh, n // tq), in_specs=[spec, spec],
            out_specs=pl.BlockSpec((None, 1, tq), lambda h, i: (h, 0, i)),
            out_shape=jax.ShapeDtypeStruct((hh, 1, n), F32), compiler_params=_params(),
        )(o, do)

    def bwd_call(q, k, kt, v, do, lse, delta):
        hh, n, dq = q.shape
        ll, dv = k.shape[1], v.shape[2]
        tq, tk = _pick(n, 2048), _pick(ll, 1408)
        sb = tk
        c2 = scale * log2e
        q_steps = n // tq
        live = dq_live or dq

        def body(q_ref, k_ref, kt_ref, v_ref, do_ref, lse_ref, d_ref, dqt_ref, dk_ref, dv_ref, dk_scr, dv_scr):
            j = pl.program_id(1)
            i = pl.program_id(2)

            @pl.when(i == 0)
            def _():
                dk_scr[...] = jnp.zeros_like(dk_scr)
                dv_scr[...] = jnp.zeros_like(dv_scr)

            q_t, do_t = q_ref[...], do_ref[...]
            lse2 = lse_ref[...] * log2e
            delta_t = d_ref[...]
            dq_part = None
            for kk in range(tk // sb):
                rows = slice(kk * sb, (kk + 1) * sb)
                s_t = _dot(k_ref[rows, :], q_t, NT)
                p_t = jnp.exp2(s_t * c2 - lse2)
                ds_t = p_t * (_dot(v_ref[rows, :], do_t, NT) - delta_t)
                dv_scr[rows, :] += _dot(p_t, do_t, NN)
                dk_scr[rows, :] += _dot(ds_t, q_t, NN)
                part = _dot(kt_ref[:live, rows], ds_t, NN)
                dq_part = part if dq_part is None else dq_part + part
            cols = pl.ds(pl.multiple_of(i * tq, tq), tq)

            @pl.when(j == 0)
            def _():
                dqt_ref[:live, cols] = dq_part
                if live < dq:
                    dqt_ref[live:, cols] = jnp.zeros((dq - live, tq), F32)

            @pl.when(j > 0)
            def _():
                dqt_ref[:live, cols] += dq_part

            @pl.when(i == q_steps - 1)
            def _():
                dk_ref[...] = (dk_scr[...] * scale).astype(BF16)
                dv_ref[...] = dv_scr[...].astype(BF16)

        return pl.pallas_call(
            body, name=tag + "_bwd", grid=(hh, ll // tk, q_steps),
            in_specs=[pl.BlockSpec((None, tq, dq), lambda h, j, i: (h, i, 0)),
                      pl.BlockSpec((None, tk, dq), lambda h, j, i: (h, j, 0)),
                      pl.BlockSpec((None, dq, tk), lambda h, j, i: (h, 0, j)),
                      pl.BlockSpec((None, tk, dv), lambda h, j, i: (h, j, 0)),
                      pl.BlockSpec((tq, dv), lambda h, j, i: (i, h)),
                      pl.BlockSpec((None, 1, tq), lambda h, j, i: (h, 0, i)),
                      pl.BlockSpec((None, 1, tq), lambda h, j, i: (h, 0, i))],
            out_specs=[pl.BlockSpec((None, dq, n), lambda h, j, i: (h, 0, 0)),
                       pl.BlockSpec((None, tk, dq), lambda h, j, i: (h, j, 0)),
                       pl.BlockSpec((None, tk, dv), lambda h, j, i: (h, j, 0))],
            out_shape=[jax.ShapeDtypeStruct((hh, dq, n), F32), jax.ShapeDtypeStruct((hh, ll, dq), BF16),
                       jax.ShapeDtypeStruct((hh, ll, dv), BF16)],
            scratch_shapes=[pltpu.VMEM((tk, dq), F32), pltpu.VMEM((tk, dv), F32)],
            compiler_params=_params(),
        )(q, k, kt, v, do, lse, delta)

    return fwd_call, delta_call, bwd_call


def _loss_tile(x, g, tgt):
    r = lax.rsqrt(jnp.mean(x * x, axis=-1, keepdims=True) + RMS_EPS)
    err = x * r * g - tgt
    per_tok = jnp.mean(err * err, axis=-1, keepdims=True)
    return 0.5 * jnp.sum(per_tok, axis=0, keepdims=True)


def _loss_and_grad(x, g, tgt, name):
    t, d = x.shape
    tm = _pick(t, 256, 16)

    def body(x_ref, g_ref, t_ref, l_ref, dx_ref, dg_ref):
        loss, vjp = jax.vjp(_loss_tile, x_ref[...], g_ref[...], t_ref[...])
        dx, dg, _ = vjp(jnp.ones((1, 1), F32))
        l_ref[...] = jnp.broadcast_to(loss, (1, LANE))
        dx_ref[...] = dx

        @pl.when(pl.program_id(0) == 0)
        def _():
            dg_ref[...] = jnp.zeros_like(dg_ref)

        dg_ref[...] += dg

    parts, dx, dg = pl.pallas_call(
        body, name=name, grid=(t // tm,),
        in_specs=[_row_spec(tm, d), _vec_spec(d), _row_spec(tm, d)],
        out_specs=[pl.BlockSpec((None, 1, LANE), lambda i: (i, 0, 0)), _row_spec(tm, d), _vec_spec(d)],
        out_shape=[jax.ShapeDtypeStruct((t // tm, 1, LANE), F32), jax.ShapeDtypeStruct((t, d), F32),
                   jax.ShapeDtypeStruct((1, d), F32)],
        compiler_params=_params(),
    )(x, g, tgt)
    return jnp.sum(parts[:, 0, 0]), dx, dg


def _exchange(arrays, gather, name):
    n = len(arrays)

    def body(*refs):
        ins, outs = refs[:n], refs[n:2 * n]
        send_sems, recv_sems, local_sems = refs[2 * n:]
        me = 4 * lax.axis_index("x") + 2 * lax.axis_index("y") + lax.axis_index("c")

        def remote(a, d, wait_side=False):
            peer = (me + d) % N_DEV
            origin = (me + N_DEV - d) % N_DEV
            src = ins[a] if gather else ins[a].at[peer]
            dst = outs[a].at[origin if wait_side else me]
            return pltpu.make_async_remote_copy(
                src_ref=src, dst_ref=dst, send_sem=send_sems.at[a, d - 1], recv_sem=recv_sems.at[a, d - 1],
                device_id=(peer // 4, (peer // 2) % 2, peer % 2), device_id_type=pl.DeviceIdType.MESH)

        def local(a):
            src = ins[a] if gather else ins[a].at[me]
            return pltpu.make_async_copy(src, outs[a].at[me], local_sems.at[a])

        for a in range(n):
            for d in range(1, N_DEV):
                remote(a, d).start()
            local(a).start()
        for a in range(n):
            local(a).wait()
            for d in range(1, N_DEV):
                remote(a, d, wait_side=True).wait_recv()
                remote(a, d).wait_send()

    out_shape = []
    for arr in arrays:
        shape = (N_DEV,) + arr.shape if gather else arr.shape
        out_shape.append(jax.ShapeDtypeStruct(shape, arr.dtype))
    any_spec = pl.BlockSpec(memory_space=pl.ANY)
    return pl.pallas_call(
        body, name=name, in_specs=[any_spec] * n, out_specs=[any_spec] * n, out_shape=out_shape,
        scratch_shapes=[pltpu.SemaphoreType.DMA((n, N_DEV - 1)), pltpu.SemaphoreType.DMA((n, N_DEV - 1)),
                        pltpu.SemaphoreType.DMA((n,))],
        compiler_params=pltpu.CompilerParams(has_side_effects=True),
    )(*arrays)


def _split_copy(ins, lands, send_sems, recv_sems, a, d, gather, wait_side):
    me = 4 * lax.axis_index("x") + 2 * lax.axis_index("y") + lax.axis_index("c")
    peer = (me + d) % N_DEV
    origin = (me + N_DEV - d) % N_DEV
    return pltpu.make_async_remote_copy(
        src_ref=ins[a] if gather else ins[a].at[peer], dst_ref=lands[a].at[origin if wait_side else me],
        send_sem=send_sems.at[a * (N_DEV - 1) + d - 1], recv_sem=recv_sems.at[a * (N_DEV - 1) + d - 1],
        device_id=(peer // 4, (peer // 2) % 2, peer % 2), device_id_type=pl.DeviceIdType.MESH)


def _exchange_start(srcs, lands, after, gather, name):
    n = len(srcs)

    def body(*refs):
        ins, lnd = refs[:n], refs[n:2 * n]
        send_sems, recv_sems = refs[2 * n + 1], refs[2 * n + 2]
        for a in range(n):
            for d in range(1, N_DEV):
                _split_copy(ins, lnd, send_sems, recv_sems, a, d, gather, False).start()

    hbm = pl.BlockSpec(memory_space=pltpu.HBM)
    sem = pl.BlockSpec(memory_space=pltpu.SEMAPHORE)
    bufs = [pltpu.with_memory_space_constraint(t, pltpu.HBM) for t in list(srcs) + list(lands) + [after]]
    res = pl.pallas_call(
        body, name=name,
        in_specs=[hbm] * (2 * n + 1), out_specs=[sem, sem] + [hbm] * (2 * n + 1),
        out_shape=[pltpu.SemaphoreType.DMA((n * (N_DEV - 1),)), pltpu.SemaphoreType.DMA((n * (N_DEV - 1),))]
        + [pltpu.HBM(t.shape, t.dtype) for t in bufs],
        input_output_aliases={i: 2 + i for i in range(2 * n + 1)},
        compiler_params=pltpu.CompilerParams(has_side_effects=pltpu.SideEffectType.DATAFLOW_SIDE_EFFECTING),
    )(*bufs)
    return res[0], res[1], res[2:2 + n], res[2 + n:2 + 2 * n], res[-1]


def _exchange_wait(send_sems, recv_sems, srcs, lands, after, gather, name):
    n = len(srcs)

    def body(*refs):
        ins, lnd = refs[:n], refs[n:2 * n]
        send_sems_ref, recv_sems_ref = refs[2 * n], refs[2 * n + 1]
        for a in range(n):
            for d in range(1, N_DEV):
                _split_copy(ins, lnd, send_sems_ref, recv_sems_ref, a, d, gather, False).wait_send()
                _split_copy(ins, lnd, send_sems_ref, recv_sems_ref, a, d, gather, True).wait_recv()

    hbm = pl.BlockSpec(memory_space=pltpu.HBM)
    sem = pl.BlockSpec(memory_space=pltpu.SEMAPHORE)
    bufs = list(srcs) + list(lands)
    res = pl.pallas_call(
        body, name=name,
        in_specs=[hbm] * (2 * n) + [sem, sem, pl.BlockSpec(memory_space=pl.ANY)],
        out_specs=[hbm] * (2 * n),
        out_shape=[pltpu.HBM(t.shape, t.dtype) for t in bufs],
        input_output_aliases={i: i for i in range(2 * n)},
        compiler_params=pltpu.CompilerParams(has_side_effects=pltpu.SideEffectType.DATAFLOW_SIDE_EFFECTING),
    )(*bufs, send_sems, recv_sems, after)
    return res[n:]


def _own_slot(block, me):
    empty = lax.empty((N_DEV,) + block.shape, block.dtype)
    return lax.dynamic_update_slice(empty, block[None], (me,) + (0,) * block.ndim)


def _coords():
    return lax.axis_index("x"), lax.axis_index("y"), lax.axis_index("c")


def _other_chips(x, y):
    return [(1 - x, y), (x, 1 - y), (1 - x, 1 - y)]


def _gather_two_level(arrays, name):
    n = len(arrays)

    def body(*refs):
        ins, outs = refs[:n], refs[n:2 * n]
        send_sems, recv_sems, local_sems = refs[2 * n:]
        x, y, c = _coords()
        me, sib = (x, y, c), (x, y, 1 - c)
        chips = _other_chips(x, y)

        def copy(a, k, block, to, from_input=False):
            slot = 4 * block[0] + 2 * block[1] + block[2]
            return pltpu.make_async_remote_copy(
                src_ref=ins[a] if from_input else outs[a].at[slot], dst_ref=outs[a].at[slot],
                send_sem=send_sems.at[a, k], recv_sem=recv_sems.at[a, k],
                device_id=to, device_id_type=pl.DeviceIdType.MESH)

        def local(a):
            return pltpu.make_async_copy(ins[a], outs[a].at[4 * x + 2 * y + c], local_sems.at[a])

        for a in range(n):
            for j, chip in enumerate(chips):
                copy(a, 1 + j, me, (*chip, c), True).start()
            copy(a, 0, me, sib, True).start()
            local(a).start()
        for a in range(n):
            for j, chip in enumerate(chips):
                copy(a, 1 + j, (*chip, c), me).wait_recv()
                copy(a, 4 + j, (*chip, c), sib).start()
        for a in range(n):
            copy(a, 0, sib, me).wait_recv()
            for j, chip in enumerate(chips):
                copy(a, 4 + j, (*chip, 1 - c), me).wait_recv()
            for k in range(N_DEV - 1):
                copy(a, k, me, sib, True).wait_send()
            local(a).wait()

    any_spec = pl.BlockSpec(memory_space=pl.ANY)
    return pl.pallas_call(
        body, name=name, in_specs=[any_spec] * n, out_specs=[any_spec] * n,
        out_shape=[jax.ShapeDtypeStruct((N_DEV,) + arr.shape, arr.dtype) for arr in arrays],
        scratch_shapes=[pltpu.SemaphoreType.DMA((n, N_DEV - 1)), pltpu.SemaphoreType.DMA((n, N_DEV - 1)),
                        pltpu.SemaphoreType.DMA((n,))],
        compiler_params=pltpu.CompilerParams(has_side_effects=True),
    )(*arrays)


def _make_gather_op(tag):
    @jax.custom_vjp
    def gather_op(xl):
        return _exchange([xl], True, tag + "_gather")[0]

    def fwd(xl):
        return gather_op(xl), None

    def bwd(_, g):
        return (jnp.sum(_exchange([g], False, tag + "_scatter")[0], axis=0),)

    gather_op.defvjp(fwd, bwd)
    return gather_op


def _adamw(gstack, w, m, v, name):
    s, r, cn = gstack.shape
    tr = _pick(r, max(8, (2 * 1024 * 1024) // (4 * cn) // 8 * 8), 8)
    c1 = 1.0 - ADAM_B1 ** ADAM_STEP
    c2 = 1.0 - ADAM_B2 ** ADAM_STEP

    def body(g_ref, w_ref, m_ref, v_ref, go_ref, d_ref, mo_ref, vo_ref):
        g = g_ref[0].astype(F32)
        for q in range(1, s):
            g = g + g_ref[q].astype(F32)
        m_new = ADAM_B1 * m_ref[...] + (1.0 - ADAM_B1) * g
        v_new = ADAM_B2 * v_ref[...] + (1.0 - ADAM_B2) * (g * g)
        go_ref[...] = g
        mo_ref[...] = m_new
        vo_ref[...] = v_new
        d_ref[...] = -ADAM_LR * ((m_new / c1) / (jnp.sqrt(v_new / c2) + ADAM_EPS) + ADAM_WD * w_ref[...])

    tile = pl.BlockSpec((tr, cn), lambda i: (i, 0))
    out = jax.ShapeDtypeStruct((r, cn), F32)
    return pl.pallas_call(
        body, name=name, grid=(r // tr,),
        in_specs=[pl.BlockSpec((s, tr, cn), lambda i: (0, i, 0)), tile, tile, tile],
        out_specs=[tile, tile, tile, tile], out_shape=[out, out, out, out],
        compiler_params=_params(),
    )(gstack, w, m, v)


def _cols_from_stack(w):
    return jnp.swapaxes(w, 0, 1).reshape(w.shape[1], N_DEV * w.shape[2])


def _ada_vectors(p, silu_c_all, me):
    d = p["c_ctx"].shape[0]
    n_a = p["ada_w"].shape[1]
    a_in = jnp.concatenate([silu_c_all, jax.nn.silu(p["c_ctx"])[None, :], jnp.zeros((7, d), F32)], axis=0)
    b_loc = lax.dynamic_slice(p["ada_b"], (0, me * n_a), (1, n_a))
    r_loc = _make_small_mm("ada")(a_in, p["ada_w"]) + b_loc
    r_full = _make_gather_op("ada")(r_loc)
    m_lat = lax.dynamic_index_in_dim(r_full, me, axis=1, keepdims=False).reshape(N_MOD, 1, d)
    m_ctx = r_full[:, N_DEV, :].reshape(N_MOD, 1, d)
    return m_lat, m_ctx


def _stage_a_fwd(p_ada, x, ctx, ng, w_in, wait_w_out, silu_c_all, me):
    (m_lat, m_ctx), vjp_ada = jax.vjp(lambda q: _ada_vectors(q, silu_c_all, me), p_ada)
    lat, cx = _ffn_parts("ffn1"), _ffn_parts("ffn1c")
    h, a, gu = lat[0](x, ng, m_lat[0], m_lat[1], w_in)
    hc, ac, guc = cx[0](ctx, ng, m_ctx[0], m_ctx[1], w_in)
    w_out = wait_w_out(a)
    x1, f1 = lat[1](a, w_out, x, m_lat[2])
    c1, f1c = cx[1](ac, w_out, ctx, m_ctx[2])
    res = dict(vjp_ada=vjp_ada, m_lat=m_lat, m_ctx=m_ctx, x=x, ctx=ctx, ng=ng, w_in=w_in, w_out=w_out,
               lat=(h, a, gu, f1), cx=(hc, ac, guc, f1c))
    return (x1, c1, m_lat, m_ctx), res


def _stage_a_bwd(res, dx1, dc1, dm_lat, dm_ctx, start_grads):
    lat, cx = _ffn_parts("ffn1"), _ffn_parts("ffn1c")
    m_lat, m_ctx, ng, w_in, w_out = res["m_lat"], res["m_ctx"], res["ng"], res["w_in"], res["w_out"]
    hc, ac, guc, f1c = res["cx"]
    dgate_c, dgu_c, dw_out_c = cx[2](dc1, f1c, m_ctx[2], w_out, guc, ac)
    _, dng_c, dsh_c, dsc_c, dw_in_c = cx[3](res["ctx"], ng, m_ctx[0], m_ctx[1], w_in, hc, dgu_c, dc1)
    h, a, gu, f1 = res["lat"]
    dgate, dgu, dw_out = lat[2](dx1, f1, m_lat[2], w_out, gu, a, dw_out_c)
    dgu = start_grads("ffn1_w_out", dw_out, dgu)
    dw_in = _ffn_dwin_bwd(h, dgu, "ffn1_dwin", add=dw_in_c)
    dgu = start_grads("ffn1_w_in", dw_in, dgu)
    dh = _ffn_dh_bwd(dgu, w_in, "ffn1_dh")
    dx, dng, dsc, dsh = _norm_mod_bwd(res["x"], ng, m_lat[1], m_lat[0], dh, dx1, "ffn1_dnorm")

    def rows(dsh_, dsc_, dgate_):
        return jnp.concatenate([dsh_, dsc_, dgate_, jnp.zeros((N_MOD - 3,) + dsh_.shape, F32)[:, 0]], axis=0)[:, None, :]

    (g_ada,) = res["vjp_ada"]((dm_lat + rows(dsh, dsc, dgate), dm_ctx + rows(dsh_c, dsc_c, dgate_c)))
    return g_ada, dx, dng + dng_c


def _stage_b(p, x1, c1, m_lat, m_ctx):
    n_lat, d = x1.shape
    n_ctx = c1.shape[0]
    w_mix = jnp.pad(_cols_from_stack(p["mix_w_in"]), ((0, 0), (0, MIX_IN_PAD - MIX_IN)))
    proj, x1 = _make_norm_proj_carry("mix")(x1, p["norm2_g"], m_lat[3], m_lat[4], w_mix)
    proj_c = _make_norm_proj("mixc")(c1, p["norm2_g"], m_ctx[3], m_ctx[4], w_mix)
    widths = SPLITS[:6] + (LANE,)
    rq, rk, rv, rg, cq, ckv, kr = _make_split("mixsplit", widths, MIX_IN_PAD)(proj)
    _, crk, crv, _, _, cckv, ckr = _make_split("mixsplitc", widths, MIX_IN_PAD)(proj_c)

    zq = jnp.zeros((1, MLA_Q_RANK), F32)
    zkv = jnp.zeros((1, MLA_KV_RANK), F32)
    w_uq3 = _cols_from_stack(p["mla_w_uq"]).reshape(MLA_Q_RANK, MLA_HEADS, MLA_NOPE + MLA_ROPE)
    w_uq = jnp.concatenate([w_uq3[:, :, :MLA_NOPE].reshape(MLA_Q_RANK, -1),
                            w_uq3[:, :, MLA_NOPE:].reshape(MLA_Q_RANK, -1)], axis=1)
    w_ukv = _cols_from_stack(p["mla_w_ukv"])
    q = _make_norm_proj("uq")(cq, p["mla_q_norm_g"], zq, zq, w_uq)
    kv = _make_norm_proj("ukv")(ckv, p["mla_kv_norm_g"], zkv, zkv, w_ukv)
    kv_c = _make_norm_proj("ukvc")(cckv, p["mla_kv_norm_g"], zkv, zkv, w_ukv)

    lg_f = jax.nn.log_sigmoid(p["ret_decay_fwd"][0])
    lg_b = jax.nn.log_sigmoid(p["ret_decay_bwd"][0])
    lat_f, lat_b, ctx_f, ctx_b = _make_ret_pack("retpack", n_lat, n_ctx)(rq, rk, rv, crk, crv)
    assert n_ctx == RET_CHUNK, "the context prefix is one retention chunk"

    def lanes(lg):
        return jnp.broadcast_to(lg[:, None, None], (RET_HEADS, 1, LANE))

    s0_f = _make_ctx_state("retcf", False)(*ctx_f, lanes(lg_f))
    s0_b = _make_ctx_state("retcb", True)(*ctx_b, lanes(lg_b))
    y_f = _make_ret_dir("retf", False)(*lat_f, lanes(lg_f), s0_f)
    y_b = _make_ret_dir("retb", True)(*lat_b, lanes(lg_b), s0_b)
    ret_o = _make_ret_out("reto")(y_f, y_b, rg)

    mla_o = _make_mla("mla", n_lat, n_ctx)(q, kv, kr, kv_c, ckr)

    w_mo = p["mix_w_out"].reshape(-1, d)
    return _make_res_proj("mixo")(jnp.concatenate([ret_o, mla_o], axis=-1), w_mo, x1, m_lat[5])


def _stage_c(p, x2, m_lat):
    return _make_ffn_block("ffn2")(x2, p["norm3_g"], m_lat[6], m_lat[7], m_lat[8], p["ffn2_w_in"], p["ffn2_w_out"])


FIRST = ("ffn1_w_in", "ffn1_w_out")
MID = ("mix_w_in", "mla_w_uq", "mla_w_ukv", "mix_w_out")
LAST = ("ffn2_w_in", "ffn2_w_out")
BIG = FIRST + MID + LAST
SMALL = ("c_ctx", "ada_b", "norm1_g", "norm2_g", "ret_decay_fwd", "ret_decay_bwd", "mla_q_norm_g",
         "mla_kv_norm_g", "norm3_g", "final_norm_g")
WEIGHTS = ("c_ctx", "ada_w", "ada_b", "norm1_g", "ffn1_w_in", "ffn1_w_out", "norm2_g", "mix_w_in", "ret_decay_fwd",
           "ret_decay_bwd", "mla_q_norm_g", "mla_w_uq", "mla_kv_norm_g", "mla_w_ukv", "mix_w_out", "norm3_g",
           "ffn2_w_in", "ffn2_w_out", "final_norm_g")


def _pack(parts):
    flat = jnp.concatenate([t.reshape(-1) for t in parts])
    pad = (-flat.shape[0]) % LANE
    return jnp.pad(flat, (0, pad)).reshape(1, -1)


def _unpack(flat, like):
    out, off = [], 0
    for t in like:
        out.append(flat[0, off:off + t.size].reshape(t.shape))
        off += t.size
    return out


def kernel(x, c, ctx, c_ctx, ada_w, ada_b, norm1_g, ffn1_w_in, ffn1_w_out, norm2_g, mix_w_in, ret_decay_fwd, ret_decay_bwd, mla_q_norm_g, mla_w_uq, mla_kv_norm_g, mla_w_ukv, mix_w_out, norm3_g, ffn2_w_in, ffn2_w_out, final_norm_g, loss_target, m_c_ctx, m_ada_w, m_ada_b, m_norm1_g, m_ffn1_w_in, m_ffn1_w_out, m_norm2_g, m_mix_w_in, m_ret_decay_fwd, m_ret_decay_bwd, m_mla_q_norm_g, m_mla_w_uq, m_mla_kv_norm_g, m_mla_w_ukv, m_mix_w_out, m_norm3_g, m_ffn2_w_in, m_ffn2_w_out, m_final_norm_g, v_c_ctx, v_ada_w, v_ada_b, v_norm1_g, v_ffn1_w_in, v_ffn1_w_out, v_norm2_g, v_mix_w_in, v_ret_decay_fwd, v_ret_decay_bwd, v_mla_q_norm_g, v_mla_w_uq, v_mla_kv_norm_g, v_mla_w_ukv, v_mix_w_out, v_norm3_g, v_ffn2_w_in, v_ffn2_w_out, v_final_norm_g):
    w = dict(c_ctx=c_ctx, ada_w=ada_w, ada_b=ada_b, norm1_g=norm1_g, ffn1_w_in=ffn1_w_in, ffn1_w_out=ffn1_w_out,
             norm2_g=norm2_g, mix_w_in=mix_w_in, ret_decay_fwd=ret_decay_fwd, ret_decay_bwd=ret_decay_bwd,
             mla_q_norm_g=mla_q_norm_g, mla_w_uq=mla_w_uq, mla_kv_norm_g=mla_kv_norm_g, mla_w_ukv=mla_w_ukv,
             mix_w_out=mix_w_out, norm3_g=norm3_g, ffn2_w_in=ffn2_w_in, ffn2_w_out=ffn2_w_out,
             final_norm_g=final_norm_g)
    mom = dict(c_ctx=m_c_ctx, ada_w=m_ada_w, ada_b=m_ada_b, norm1_g=m_norm1_g, ffn1_w_in=m_ffn1_w_in,
               ffn1_w_out=m_ffn1_w_out, norm2_g=m_norm2_g, mix_w_in=m_mix_w_in, ret_decay_fwd=m_ret_decay_fwd,
               ret_decay_bwd=m_ret_decay_bwd, mla_q_norm_g=m_mla_q_norm_g, mla_w_uq=m_mla_w_uq,
               mla_kv_norm_g=m_mla_kv_norm_g, mla_w_ukv=m_mla_w_ukv, mix_w_out=m_mix_w_out, norm3_g=m_norm3_g,
               ffn2_w_in=m_ffn2_w_in, ffn2_w_out=m_ffn2_w_out, final_norm_g=m_final_norm_g)
    var = dict(c_ctx=v_c_ctx, ada_w=v_ada_w, ada_b=v_ada_b, norm1_g=v_norm1_g, ffn1_w_in=v_ffn1_w_in,
               ffn1_w_out=v_ffn1_w_out, norm2_g=v_norm2_g, mix_w_in=v_mix_w_in, ret_decay_fwd=v_ret_decay_fwd,
               ret_decay_bwd=v_ret_decay_bwd, mla_q_norm_g=v_mla_q_norm_g, mla_w_uq=v_mla_w_uq,
               mla_kv_norm_g=v_mla_kv_norm_g, mla_w_ukv=v_mla_w_ukv, mix_w_out=v_mix_w_out, norm3_g=v_norm3_g,
               ffn2_w_in=v_ffn2_w_in, ffn2_w_out=v_ffn2_w_out, final_norm_g=v_final_norm_g)
    me = 4 * lax.axis_index("x") + 2 * lax.axis_index("y") + lax.axis_index("c")

    shard = {k: w[k][0].astype(BF16) for k in BIG}
    first = _gather_two_level([shard["ffn1_w_in"], jax.nn.silu(c)], "weights_gather")
    silu_c_all = first[-1][:, 0, :]

    def start_gather(names, after, name):
        return _exchange_start([shard[k] for k in names], [_own_slot(shard[k], me) for k in names], after, True, name)

    wout_start = start_gather(FIRST[1:], first[0], "ffn1_wout_start")
    mid_start = start_gather(MID, wout_start[4], "mixer_weights_start")
    last_start = start_gather(LAST, mid_start[4], "ffn2_weights_start")

    def wait_w_out(after):
        return _exchange_wait(*wout_start[:4], after, True, "ffn1_wout_wait")[0]

    pa = dict(ada_w=ada_w[0], c_ctx=c_ctx, ada_b=ada_b)
    (x1, c1, m_lat, m_ctx), res_a = _stage_a_fwd(pa, x[0], ctx[0], norm1_g, last_start[4], wait_w_out, silu_c_all, me)

    mid = _exchange_wait(mid_start[0], mid_start[1], mid_start[2], mid_start[3], x1, True, "mixer_weights_wait")
    pb = dict(zip(MID, mid))
    for k in ("norm2_g", "mla_q_norm_g", "mla_kv_norm_g", "ret_decay_fwd", "ret_decay_bwd"):
        pb[k] = w[k]
    x2, vjp_b = jax.vjp(_stage_b, pb, x1, c1, m_lat, m_ctx)

    last = _exchange_wait(last_start[0], last_start[1], last_start[2], last_start[3], x2, True, "ffn2_weights_wait")
    pc = dict(zip(LAST, last), norm3_g=norm3_g)
    x3, vjp_c = jax.vjp(_stage_c, pc, x2, m_lat)
    loss_local, dx3, d_final_g = _loss_and_grad(x3, final_norm_g[None, :], loss_target[0], "loss_head")

    gc, dx2, dm_c = vjp_c(dx3)
    last_scat = _exchange_start([gc[k] for k in LAST],
                                [_own_slot(lax.dynamic_index_in_dim(gc[k], me, 0, False), me) for k in LAST],
                                dx2, False, "ffn2_grads_start")
    gb, dx1, dc1, dm_b, dmc_b = vjp_b(last_scat[4])
    mid_scat = _exchange_start([gb[k] for k in MID],
                               [_own_slot(lax.dynamic_index_in_dim(gb[k], me, 0, False), me) for k in MID],
                               dx1, False, "mixer_grads_start")
    first_scat = {}

    def start_grads(name, dw, after):
        first_scat[name] = _exchange_start(
            [dw], [_own_slot(lax.dynamic_index_in_dim(dw, me, 0, False), me)], after, False, name + "_grads_start")
        return first_scat[name][4]

    g_ada, dx, dng1 = _stage_a_bwd(res_a, mid_scat[4], dc1, dm_b + dm_c, dmc_b, start_grads)
    grads = {**g_ada, **gb, **gc, "x": dx, "norm1_g": dng1, "final_norm_g": d_final_g[0]}

    exchanged = {k: _exchange_wait(*first_scat[k][:4], dx, False, k + "_grads_wait")[0] for k in FIRST}
    exchanged.update(zip(LAST, _exchange_wait(*last_scat[:4], dx, False, "ffn2_grads_wait")))
    exchanged.update(zip(MID, _exchange_wait(*mid_scat[:4], dx, False, "mixer_grads_wait")))
    zero1 = [jnp.zeros((1,), F32)]
    small_like = zero1 + [w[k] for k in SMALL]
    small_all = _exchange([_pack([loss_local.reshape(1)] + [grads[k] for k in SMALL])], True, "small_grads_gather")[0]
    loss = jnp.sum(small_all[:, 0, 0])

    out_g, out_d, out_m, out_v = {}, {}, {}, {}

    def update(name, gstack, shape2d):
        res = _adamw(gstack, w[name].reshape(shape2d), mom[name].reshape(shape2d), var[name].reshape(shape2d),
                     "adamw_" + name)
        out_g[name], out_d[name], out_m[name], out_v[name] = [t.reshape(w[name].shape) for t in res]

    for k in BIG:
        update(k, exchanged[k], exchanged[k].shape[1:])
    update("ada_w", grads["ada_w"][None], ada_w.shape[1:])
    res = _adamw(small_all, _pack(small_like), _pack(zero1 + [mom[k] for k in SMALL]),
                 _pack(zero1 + [var[k] for k in SMALL]), "adamw_small")
    for dst, flat in zip((out_g, out_d, out_m, out_v), res):
        for k, t in zip(SMALL, _unpack(flat, small_like)[1:]):
            dst[k] = t

    return (loss, grads["x"][None], *[out_g[k] for k in WEIGHTS], *[out_d[k] for k in WEIGHTS],
            *[out_m[k] for k in WEIGHTS], *[out_v[k] for k in WEIGHTS])
```
